```python
import jax, jax.numpy as jnp
from jax import lax
import numpy as np

D_MODEL = 1024
BATCH = 2
SEQ = 8192
DEPTH = 2

GRID_W = 64
CTX_LEN = 256
HEAD_DIM = 64
ATT_Q_HEADS = 8
ATT_KV_HEADS = 2
ATT_WINDOW = 128
ATT_BLOCK = 128
RET_HEADS = 4
RET_CHUNK = 128
RET_K_SCALE = HEAD_DIM ** -0.5
CONV_CH = 256
CONV_WIDTH = 31
ROPE_BASE = 10000.0
ROPE_FREQS = HEAD_DIM // 4
D_ATT = ATT_Q_HEADS * HEAD_DIM
D_RET = RET_HEADS * HEAD_DIM
D_MIX = D_ATT + D_RET + CONV_CH
ATT_KV_W = ATT_KV_HEADS * HEAD_DIM
C_ATT_K = 0
C_ATT_V = C_ATT_K + ATT_KV_W
C_RET_K = C_ATT_V + ATT_KV_W
C_RET_V = C_RET_K + D_RET
KV_COLS = C_RET_V + D_RET
C_ATT_Q = KV_COLS
C_RET_Q = C_ATT_Q + D_ATT
C_RET_G = C_RET_Q + D_RET
C_CONV = C_RET_G + D_RET
IN_COLS = C_CONV + 2 * CONV_CH
D_FF = 2816
N_EXPERTS = 8
TOP_K = 2
D_FF_EXPERT = 3584
N_DENSE = (DEPTH + 1) // 2
N_MOE = DEPTH // 2
EPS = 1e-6
NEG_INF = -1e30

kernel_name = 'hybrid_parallel_heads_dit_block'


def _rms(x, w):
    xf = x.astype(jnp.float32)
    y = xf * lax.rsqrt(jnp.mean(xf * xf, axis=-1, keepdims=True) + EPS)
    return (y * w.astype(jnp.float32)).astype(x.dtype)


def _modulate(x, w, shift, scale):
    return _rms(x, w) * (1 + scale) + shift


def _layernorm(x):
    xf = x.astype(jnp.float32)
    mu = jnp.mean(xf, axis=-1, keepdims=True)
    var = jnp.mean(jnp.square(xf - mu), axis=-1, keepdims=True)
    return (xf - mu) * lax.rsqrt(var + EPS)


def _adaln(cond, w, b):
    return jax.nn.silu(cond) @ w + b


def _heads(t, start, n_heads):
    return t[..., start:start + n_heads * HEAD_DIM].reshape(t.shape[:-1] + (n_heads, HEAD_DIM))


def _axial_rope(n):
    rows = n // GRID_W
    r = jnp.repeat(jnp.arange(rows), GRID_W).astype(jnp.float32)
    col = jnp.tile(jnp.arange(GRID_W), rows).astype(jnp.float32)
    freqs = ROPE_BASE ** (-jnp.arange(ROPE_FREQS, dtype=jnp.float32) / ROPE_FREQS)
    ang = jnp.stack([r[:, None] * freqs, col[:, None] * freqs], axis=1)
    return jnp.cos(ang)[:, None], jnp.sin(ang)[:, None]


def _apply_rope(x, cos, sin):
    xs = x.astype(jnp.float32).reshape(x.shape[:-1] + (2, 2, ROPE_FREQS))
    x1, x2 = xs[..., 0, :], xs[..., 1, :]
    out = jnp.stack([x1 * cos - x2 * sin, x2 * cos + x1 * sin], axis=-2)
    return out.reshape(x.shape).astype(x.dtype)


def _ctx_attention(q, k, v, sink):
    b, n, hq, d = q.shape
    g = ATT_KV_HEADS
    r = hq // g
    qg = q.reshape(b, n, g, r, d)
    s = jnp.einsum('bqgrd,bkgd->bgrqk', qg, k).astype(jnp.float32) * (d ** -0.5)
    s_sink = jnp.broadcast_to(sink.astype(jnp.float32).reshape(1, g, r, 1, 1), s.shape[:-1] + (1,))
    p = jax.nn.softmax(jnp.concatenate([s, s_sink], axis=-1), axis=-1)[..., :-1].astype(v.dtype)
    o = jnp.einsum('bgrqk,bkgd->bqgrd', p, v)
    return o.reshape(b, n, hq * d)


def _window_attention(q, k, v, k_ctx, v_ctx, sink):
    b, n, hq, d = q.shape
    g = ATT_KV_HEADS
    r = hq // g
    L = ATT_BLOCK
    nb = n // L
    halo = -(-ATT_WINDOW // L)
    qb = q.reshape(b, nb, L, g, r, d)

    def band(t):
        tp = jnp.pad(t, ((0, 0), (halo * L, halo * L), (0, 0), (0, 0))).reshape(b, nb + 2 * halo, L, g, d)
        return jnp.concatenate([tp[:, j:j + nb] for j in range(2 * halo + 1)], axis=2)

    kw, vw = band(k), band(v)
    kw_len = kw.shape[2]
    qpos = jnp.arange(nb)[:, None] * L + jnp.arange(L)[None, :]
    kpos = (jnp.arange(nb)[:, None] - halo) * L + jnp.arange(kw_len)[None, :]
    valid = ((kpos[:, None, :] >= 0) & (kpos[:, None, :] < n)
             & (jnp.abs(qpos[:, :, None] - kpos[:, None, :]) <= ATT_WINDOW))
    scale = d ** -0.5
    s_win = jnp.einsum('bnqgrd,bnkgd->bngrqk', qb, kw).astype(jnp.float32) * scale
    s_win = jnp.where(valid[None, :, None, None], s_win, NEG_INF)
    s_ctx = jnp.einsum('bnqgrd,bcgd->bngrqc', qb, k_ctx).astype(jnp.float32) * scale
    s_sink = jnp.broadcast_to(sink.astype(jnp.float32).reshape(1, 1, g, r, 1, 1), s_win.shape[:-1] + (1,))
    p = jax.nn.softmax(jnp.concatenate([s_win, s_ctx, s_sink], axis=-1), axis=-1).astype(v.dtype)
    o = (jnp.einsum('bngrqk,bnkgd->bnqgrd', p[..., :kw_len], vw)
         + jnp.einsum('bngrqc,bcgd->bnqgrd', p[..., kw_len:-1], v_ctx))
    return o.reshape(b, n, hq * d)


def _retention_chunks(q, k, v, log_g, s0, strict):
    b, n, h, dk = q.shape
    dv = v.shape[-1]
    L = RET_CHUNK
    nc = n // L
    qc = q.reshape(b, nc, L, h, dk)
    kc = k.reshape(b, nc, L, h, dk)
    vc = v.reshape(b, nc, L, h, dv)
    i = jnp.arange(L, dtype=jnp.float32)
    diff = i[:, None] - i[None, :]
    offset = 1.0 if strict else 0.0
    dmat = jnp.where((diff >= offset)[None], jnp.exp(jnp.maximum(diff, 0.0)[None] * log_g[:, None, None]), 0.0)
    a = jnp.einsum('bcihd,bcjhd->bchij', qc, kc) * dmat
    inner = jnp.einsum('bchij,bcjhe->bcihe', a, vc)
    kdec = jnp.exp((L - 1 - i)[None, :] * log_g[:, None])
    u = jnp.einsum('bcjhd,hj,bcjhe->cbhde', kc, kdec, vc)
    chunk_decay = jnp.exp(L * log_g)[None, :, None, None]

    def step(state, u_c):
        return chunk_decay * state + u_c, state

    final, r_in = lax.scan(step, s0.astype(u.dtype), u)
    xi = jnp.exp((i + 1)[None, :] * log_g[:, None])
    cross = jnp.einsum('bcihd,cbhde,hi->bcihe', qc, r_in, xi)
    return (inner + cross).reshape(b, n, h, dv), final


def _bidir_retention(q, k, v, lg_f, lg_b, s0_f, s0_b):
    o_f, s_f = _retention_chunks(q, k, v, lg_f, s0_f, strict=False)
    o_b, s_b = _retention_chunks(jnp.flip(q, 1), jnp.flip(k, 1), jnp.flip(v, 1), lg_b, s0_b, strict=True)
    return o_f + jnp.flip(o_b, 1), s_f, s_b


def _decayed_state(k, v, log_g, reverse):
    n = k.shape[1]
    pos = jnp.arange(n, dtype=jnp.float32)
    expo = pos if reverse else (n - 1) - pos
    w = jnp.exp(expo[None, :] * log_g[:, None])
    return jnp.einsum('bnhd,hn,bnhe->bhde', k, w, v)


def _retention_readout(o, gate, gn_w):
    y = _layernorm(o).reshape(o.shape[:-2] + (-1,)) * gn_w.astype(jnp.float32)
    return (jax.nn.silu(gate.astype(jnp.float32)) * y).astype(gate.dtype)


def _conformer_conv(u, w, bias, ln_w, ln_b):
    a, gt = jnp.split(u, 2, axis=-1)
    h = a * jax.nn.sigmoid(gt)
    pad = CONV_WIDTH // 2
    h = lax.conv_general_dilated(h, w[:, None, :].astype(h.dtype), window_strides=(1,),
                                 padding=[(pad, pad)], dimension_numbers=('NWC', 'WIO', 'NWC'),
                                 feature_group_count=CONV_CH) + bias
    h = _layernorm(h) * ln_w + ln_b
    return jax.nn.silu(h).astype(u.dtype)


def _swiglu(h, wg, wu, wd):
    return (jax.nn.silu(h @ wg) * (h @ wu)) @ wd


def _moe(h, router, wg, wu, wd):
    logits = (h @ router).astype(jnp.float32)
    top_v, top_i = lax.top_k(logits, TOP_K)
    w = jax.nn.softmax(top_v, axis=-1)
    combine = jnp.sum(jax.nn.one_hot(top_i, N_EXPERTS, dtype=jnp.float32) * w[..., None], axis=-2)
    out = jnp.zeros_like(h)
    for e in range(N_EXPERTS):
        out = out + combine[..., e:e + 1].astype(h.dtype) * _swiglu(h, wg[e], wu[e], wd[e])
    return out


def _channel_mixer(t, l, ffn_w_gate, ffn_w_up, ffn_w_down, router_w, moe_w_gate, moe_w_up, moe_w_down):
    i = l // 2
    if l % 2 == 0:
        return _swiglu(t, ffn_w_gate[i], ffn_w_up[i], ffn_w_down[i])
    return _moe(t, router_w[i], moe_w_gate[i], moe_w_up[i], moe_w_down[i])


def setup_inputs(seed: int = 0) -> dict:
    key = jax.random.key(seed)
    ks = jax.random.split(key, 32)
    f32 = jnp.float32

    def nrm(k, shape, scale):
        return jax.random.normal(k, shape, f32) * scale

    def gain(k, shape):
        return 1.0 + 0.05 * jax.random.normal(k, shape, f32)

    g0 = 1.0 - 2.0 ** (-5.0 - np.arange(RET_HEADS, dtype=np.float32))
    decay_logit = jnp.asarray(np.log(g0 / (1.0 - g0)), f32)
    return {
        'x': nrm(ks[0], (BATCH, SEQ, D_MODEL), 1.0),
        'c': nrm(ks[1], (BATCH, D_MODEL), 1.0),
        'ctx': nrm(ks[2], (BATCH, CTX_LEN, D_MODEL), 1.0),
        'c_ctx': nrm(ks[3], (D_MODEL,), 1.0),
        'ada_w': nrm(ks[4], (DEPTH, D_MODEL, 6 * D_MODEL), 0.5 * D_MODEL ** -0.5),
        'ada_b': nrm(ks[5], (DEPTH, 6 * D_MODEL), 0.02),
        'norm1_w': gain(ks[6], (DEPTH, D_MODEL)),
        'norm2_w': gain(ks[7], (DEPTH, D_MODEL)),
        'w_in': nrm(ks[8], (DEPTH, D_MODEL, IN_COLS), D_MODEL ** -0.5),
        'w_out': nrm(ks[9], (DEPTH, D_MIX, D_MODEL), D_MIX ** -0.5),
        'q_norm_w': gain(ks[10], (DEPTH, HEAD_DIM)),
        'k_norm_w': gain(ks[11], (DEPTH, HEAD_DIM)),
        'attn_sink': nrm(ks[12], (DEPTH, ATT_Q_HEADS), 0.5),
        'ret_decay_f': decay_logit + nrm(ks[13], (DEPTH, RET_HEADS), 0.1),
        'ret_decay_b': decay_logit + nrm(ks[14], (DEPTH, RET_HEADS), 0.1),
        'ret_gn_w': gain(ks[15], (DEPTH, D_RET)),
        'conv_w': nrm(ks[16], (DEPTH, CONV_WIDTH, CONV_CH), CONV_WIDTH ** -0.5),
        'conv_b': nrm(ks[17], (DEPTH, CONV_CH), 0.02),
        'conv_ln_w': gain(ks[18], (DEPTH, CONV_CH)),
        'conv_ln_b': nrm(ks[19], (DEPTH, CONV_CH), 0.02),
        'ffn_w_gate': nrm(ks[20], (N_DENSE, D_MODEL, D_FF), D_MODEL ** -0.5),
        'ffn_w_up': nrm(ks[21], (N_DENSE, D_MODEL, D_FF), D_MODEL ** -0.5),
        'ffn_w_down': nrm(ks[22], (N_DENSE, D_FF, D_MODEL), D_FF ** -0.5),
        'router_w': nrm(ks[23], (N_MOE, D_MODEL, N_EXPERTS), D_MODEL ** -0.5),
        'moe_w_gate': nrm(ks[24], (N_MOE, N_EXPERTS, D_MODEL, D_FF_EXPERT), D_MODEL ** -0.5),
        'moe_w_up': nrm(ks[25], (N_MOE, N_EXPERTS, D_MODEL, D_FF_EXPERT), D_MODEL ** -0.5),
        'moe_w_down': nrm(ks[26], (N_MOE, N_EXPERTS, D_FF_EXPERT, D_MODEL), D_FF_EXPERT ** -0.5),
    }


def reference(x, c, ctx, c_ctx, ada_w, ada_b, norm1_w, norm2_w, w_in, w_out, q_norm_w, k_norm_w,
              attn_sink, ret_decay_f, ret_decay_b, ret_gn_w, conv_w, conv_b, conv_ln_w, conv_ln_b,
              ffn_w_gate, ffn_w_up, ffn_w_down, router_w, moe_w_gate, moe_w_up, moe_w_down):
    b, n, _ = x.shape
    cos, sin = _axial_rope(n)
    zero_state = jnp.zeros((b, RET_HEADS, HEAD_DIM, HEAD_DIM), jnp.float32)
    for l in range(DEPTH):
        last = l == DEPTH - 1
        sh1, sc1, g1, sh2, sc2, g2 = jnp.split(_adaln(c, ada_w[l], ada_b[l])[:, None, :], 6, axis=-1)
        csh1, csc1, cg1, csh2, csc2, cg2 = jnp.split(_adaln(c_ctx, ada_w[l], ada_b[l]), 6, axis=-1)
        h = _modulate(x, norm1_w[l], sh1, sc1)
        hc = _modulate(ctx, norm1_w[l], csh1, csc1)
        p = h @ w_in[l]
        pc = hc @ (w_in[l][:, :KV_COLS] if last else w_in[l])

        k_c = _rms(_heads(pc, C_ATT_K, ATT_KV_HEADS), k_norm_w[l])
        v_c = _heads(pc, C_ATT_V, ATT_KV_HEADS)
        rk_c = _heads(pc, C_RET_K, RET_HEADS) * RET_K_SCALE
        rv_c = _heads(pc, C_RET_V, RET_HEADS)
        lg_f = jax.nn.log_sigmoid(ret_decay_f[l].astype(jnp.float32))
        lg_b = jax.nn.log_sigmoid(ret_decay_b[l].astype(jnp.float32))
        if last:
            s_f = _decayed_state(rk_c, rv_c, lg_f, reverse=False)
            s_b = _decayed_state(rk_c, rv_c, lg_b, reverse=True)
        else:
            ro_c, s_f, s_b = _bidir_retention(_heads(pc, C_RET_Q, RET_HEADS), rk_c, rv_c,
                                              lg_f, lg_b, zero_state, zero_state)

        q = _apply_rope(_rms(_heads(p, C_ATT_Q, ATT_Q_HEADS), q_norm_w[l]), cos, sin)
        k = _apply_rope(_rms(_heads(p, C_ATT_K, ATT_KV_HEADS), k_norm_w[l]), cos, sin)
        v = _heads(p, C_ATT_V, ATT_KV_HEADS)
        att = _window_attention(q, k, v, k_c, v_c, attn_sink[l])
        ro, _, _ = _bidir_retention(_heads(p, C_RET_Q, RET_HEADS), _heads(p, C_RET_K, RET_HEADS) * RET_K_SCALE,
                                    _heads(p, C_RET_V, RET_HEADS), lg_f, lg_b, s_f, s_b)
        ret = _retention_readout(ro, p[..., C_RET_G:C_RET_G + D_RET], ret_gn_w[l])
        cnv = _conformer_conv(p[..., C_CONV:IN_COLS], conv_w[l], conv_b[l], conv_ln_w[l], conv_ln_b[l])
        x = x + g1 * (jnp.concatenate([att, ret, cnv], axis=-1) @ w_out[l])

        if not last:
            q_c = _rms(_heads(pc, C_ATT_Q, ATT_Q_HEADS), q_norm_w[l])
            att_c = _ctx_attention(q_c, k_c, v_c, attn_sink[l])
            ret_c = _retention_readout(ro_c, pc[..., C_RET_G:C_RET_G + D_RET], ret_gn_w[l])
            cnv_c = _conformer_conv(pc[..., C_CONV:IN_COLS], conv_w[l], conv_b[l], conv_ln_w[l], conv_ln_b[l])
            ctx = ctx + cg1 * (jnp.concatenate([att_c, ret_c, cnv_c], axis=-1) @ w_out[l])

        h2 = _modulate(x, norm2_w[l], sh2, sc2)
        x = x + g2 * _channel_mixer(h2, l, ffn_w_gate, ffn_w_up, ffn_w_down,
                                    router_w, moe_w_gate, moe_w_up, moe_w_down)
        if not last:
            h2c = _modulate(ctx, norm2_w[l], csh2, csc2)
            ctx = ctx + cg2 * _channel_mixer(h2c, l, ffn_w_gate, ffn_w_up, ffn_w_down,
                                             router_w, moe_w_gate, moe_w_up, moe_w_down)
    return x
```

```python
import functools

import jax
import jax.numpy as jnp
from jax import lax
from jax.experimental import pallas as pl
from jax.experimental.pallas import tpu as pltpu

F32 = jnp.float32
BF16 = jnp.bfloat16

GRID_W = 64
HEAD_DIM = 64
ATT_Q_HEADS = 8
ATT_KV_HEADS = 2
ATT_WINDOW = 128
ATT_BLOCK = 128
RET_HEADS = 4
RET_CHUNK = 128
RET_K_SCALE = HEAD_DIM ** -0.5
ATT_SCALE = HEAD_DIM ** -0.5
CONV_CH = 256
CONV_WIDTH = 31
ROPE_BASE = 10000.0
ROPE_FREQS = HEAD_DIM // 4
D_ATT = ATT_Q_HEADS * HEAD_DIM
D_RET = RET_HEADS * HEAD_DIM
ATT_KV_W = ATT_KV_HEADS * HEAD_DIM
C_ATT_K = 0
C_ATT_V = C_ATT_K + ATT_KV_W
C_RET_K = C_ATT_V + ATT_KV_W
C_RET_V = C_RET_K + D_RET
C_ATT_Q = C_RET_V + D_RET
C_RET_Q = C_ATT_Q + D_ATT
C_RET_G = C_RET_Q + D_RET
C_CONV = C_RET_G + D_RET
N_EXPERTS = 8
EPS = 1e-6
NEG_INF = -1e30

LANES = 128
SUBLANES = 8
VMEM_LIMIT = 48 * 1024 * 1024
CONV_HALO = 16
CONV_SUB = 32
MOE_TM = 512
GATHER_TG = 256


def _params(*sem):
    return pltpu.CompilerParams(dimension_semantics=sem, vmem_limit_bytes=VMEM_LIMIT)


def _sigmoid(x):
    return 1.0 / (1.0 + jnp.exp(-x))


def _pick_tile(n, pref):
    t = min(n, pref)
    assert n % t == 0, (n, t)
    return t


def _adaln_kernel(c_ref, w_ref, b_ref, o_ref):
    c = c_ref[...]
    s = c * _sigmoid(c)
    o_ref[...] = jnp.dot(s, w_ref[...], preferred_element_type=F32,
                         precision=lax.Precision.HIGHEST) + b_ref[...]


def _adaln(cond, ada_w, ada_b):
    depth, d, n = ada_w.shape
    tn = _pick_tile(n, 1536)
    return pl.pallas_call(
        _adaln_kernel,
        out_shape=jax.ShapeDtypeStruct((depth, cond.shape[0], n), F32),
        grid=(depth, n // tn),
        in_specs=[pl.BlockSpec(cond.shape, lambda l, j: (0, 0)),
                  pl.BlockSpec((None, d, tn), lambda l, j: (l, 0, j)),
                  pl.BlockSpec((None, 1, tn), lambda l, j: (l, 0, j))],
        out_specs=pl.BlockSpec((None, cond.shape[0], tn), lambda l, j: (l, 0, j)),
        compiler_params=_params("parallel", "parallel"),
        name="adaln",
    )(cond, ada_w, ada_b.reshape(depth, 1, n))


def _modulated_rms(x, nw, sc, sh):
    ms = jnp.mean(x * x, axis=-1, keepdims=True)
    return (x * lax.rsqrt(ms + EPS) * nw) * (1.0 + sc) + sh


def _head_group_matrix():
    r = lax.broadcasted_iota(jnp.int32, (LANES, LANES), 0) // HEAD_DIM
    c = lax.broadcasted_iota(jnp.int32, (LANES, LANES), 1) // HEAD_DIM
    return jnp.where(r == c, 1.0, 0.0).astype(BF16)


def _per_head_sum(v, grp):
    hi = v.astype(BF16)
    lo = (v - hi.astype(F32)).astype(BF16)
    return (jnp.dot(hi, grp, preferred_element_type=F32)
            + jnp.dot(lo, grp, preferred_element_type=F32))


def _dup_halves(t):
    sw = pltpu.roll(t, HEAD_DIM, 1)
    lo = lax.broadcasted_iota(jnp.int32, t.shape, 1) < HEAD_DIM
    return jnp.where(lo, t, sw), jnp.where(lo, sw, t)


def _inproj_kernel(x_ref, nw_ref, sc_ref, sh_ref, w_ref, cos_ref, sa_ref, sb_ref, qw_ref, kw_ref,
                   q_ref, k_ref, v_ref, rk_ref, rv_ref, rq_ref, rg_ref, cv_ref):
    hb = _modulated_rms(x_ref[...], nw_ref[...], sc_ref[...], sh_ref[...]).astype(BF16)

    def proj(c0, n):
        return jnp.dot(hb, w_ref[:, c0:c0 + n], preferred_element_type=F32)

    grp = _head_group_matrix()
    cos, sa, sb = cos_ref[...], sa_ref[...], sb_ref[...]

    def norm_rope(p, wrow):
        y = p * lax.rsqrt(_per_head_sum(p * p, grp) * (1.0 / HEAD_DIM) + EPS) * wrow
        return (y * cos + pltpu.roll(y, LANES - ROPE_FREQS, 1) * sa
                + pltpu.roll(y, ROPE_FREQS, 1) * sb)

    k0, k1 = _dup_halves(norm_rope(proj(C_ATT_K, LANES), kw_ref[...]))
    k_ref[:, 0:LANES] = k0.astype(BF16)
    k_ref[:, LANES:2 * LANES] = k1.astype(BF16)
    v0, v1 = _dup_halves(proj(C_ATT_V, LANES))
    v_ref[:, 0:LANES] = v0.astype(BF16)
    v_ref[:, LANES:2 * LANES] = v1.astype(BF16)
    qw = qw_ref[...] * ATT_SCALE
    for j in range(D_ATT // LANES):
        q_ref[:, j * LANES:(j + 1) * LANES] = norm_rope(
            proj(C_ATT_Q + j * LANES, LANES), qw).astype(BF16)
    rk_ref[...] = (proj(C_RET_K, D_RET) * RET_K_SCALE).astype(BF16)
    rv_ref[...] = proj(C_RET_V, D_RET).astype(BF16)
    rq_ref[...] = proj(C_RET_Q, D_RET).astype(BF16)
    rg_ref[...] = proj(C_RET_G, D_RET)
    cv_ref[...] = proj(C_CONV, CONV_CH) * _sigmoid(proj(C_CONV + CONV_CH, CONV_CH))


def _inproj(x, nw, sc, sh, w_bf, cos, sa, sb, qw, kw):
    b, s, d = x.shape
    tm = _pick_tile(s, 512)
    row = lambda n: pl.BlockSpec((1, n), lambda bi, i: (0, 0))
    per_b = pl.BlockSpec((None, 1, d), lambda bi, i: (bi, 0, 0))
    tab = pl.BlockSpec((tm, LANES), lambda bi, i: (i, 0))
    tok = lambda n: pl.BlockSpec((None, tm, n), lambda bi, i: (bi, i, 0))
    shp = lambda n, dt: jax.ShapeDtypeStruct((b, s, n), dt)
    return pl.pallas_call(
        _inproj_kernel,
        out_shape=(shp(D_ATT, BF16), shp(2 * LANES, BF16), shp(2 * LANES, BF16),
                   shp(D_RET, BF16), shp(D_RET, BF16), shp(D_RET, BF16), shp(D_RET, F32),
                   shp(CONV_CH, F32)),
        grid=(b, s // tm),
        in_specs=[tok(d), row(d), per_b, per_b,
                  pl.BlockSpec(w_bf.shape, lambda bi, i: (0, 0)),
                  tab, tab, tab, row(LANES), row(LANES)],
        out_specs=(tok(D_ATT), tok(2 * LANES), tok(2 * LANES), tok(D_RET), tok(D_RET),
                   tok(D_RET), tok(D_RET), tok(CONV_CH)),
        compiler_params=_params("parallel", "parallel"),
        name="inproj",
    )(x, nw, sc, sh, w_bf, cos, sa, sb, qw, kw)


def _attn_kernel(*refs, window, seq):
    if window:
        q_ref, kp_ref, kc_ref, kn_ref, vp_ref, vc_ref, vn_ref, kx_ref, vx_ref, sink_ref, o_ref = refs
    else:
        q_ref, kx_ref, vx_ref, sink_ref, o_ref = refs
    blk = ATT_BLOCK
    n_ctx = kx_ref.shape[0]
    nk = 3 * blk + n_ctx if window else n_ctx
    if window:
        i = pl.program_id(1)
        col = lax.broadcasted_iota(jnp.int32, (blk, nk), 1)
        rw = lax.broadcasted_iota(jnp.int32, (blk, nk), 0)
        rel = col - blk
        kabs = i * blk + rel
        valid = (col >= 3 * blk) | ((jnp.abs(rel - rw) <= ATT_WINDOW) & (kabs >= 0) & (kabs < seq))
    lo = lax.broadcasted_iota(jnp.int32, (nk, LANES), 1) < HEAD_DIM
    for g in range(ATT_KV_HEADS):
        gs = slice(g * LANES, (g + 1) * LANES)
        if window:
            kcat = jnp.concatenate([kp_ref[:, gs], kc_ref[:, gs], kn_ref[:, gs], kx_ref[:, gs]], axis=0)
            vcat = jnp.concatenate([vp_ref[:, gs], vc_ref[:, gs], vn_ref[:, gs], vx_ref[:, gs]], axis=0)
        else:
            kcat, vcat = kx_ref[:, gs], vx_ref[:, gs]
        zero = jnp.zeros_like(kcat)
        kz = (jnp.where(lo, kcat, zero), jnp.where(lo, zero, kcat))
        vz = (jnp.where(lo, vcat, zero), jnp.where(lo, zero, vcat))
        for pr in range(2):
            cs = slice((2 * g + pr) * LANES, (2 * g + pr + 1) * LANES)
            qp = q_ref[:, cs]
            acc = jnp.zeros((blk, LANES), F32)
            for par in range(2):
                h = 4 * g + 2 * pr + par
                s = lax.dot_general(qp, kz[par], (((1,), (1,)), ((), ())),
                                    preferred_element_type=F32)
                if window:
                    s = jnp.where(valid, s, NEG_INF)
                snk = sink_ref[h:h + 1, 0:1]
                m = jnp.maximum(jnp.max(s, axis=-1, keepdims=True), snk)
                p = jnp.exp(s - m)
                den = jnp.sum(p, axis=-1, keepdims=True) + jnp.exp(snk - m)
                o = jnp.dot(p.astype(BF16), vz[par], preferred_element_type=F32)
                acc = acc + o * (1.0 / den)
            o_ref[:, cs] = acc.astype(BF16)


def _attention(q, k, v, kx, vx, sink_tab, window):
    b, s, _ = q.shape
    blk = ATT_BLOCK
    nb = s // blk
    n_ctx = kx.shape[1]
    qspec = pl.BlockSpec((None, blk, D_ATT), lambda bi, i: (bi, i, 0))
    ctx = pl.BlockSpec((None, n_ctx, 2 * LANES), lambda bi, i: (bi, 0, 0))
    snk = pl.BlockSpec(sink_tab.shape, lambda bi, i: (0, 0))
    if window:
        prev = pl.BlockSpec((None, blk, 2 * LANES), lambda bi, i: (bi, jnp.maximum(i - 1, 0), 0))
        cur = pl.BlockSpec((None, blk, 2 * LANES), lambda bi, i: (bi, i, 0))
        nxt = pl.BlockSpec((None, blk, 2 * LANES), lambda bi, i: (bi, jnp.minimum(i + 1, nb - 1), 0))
        in_specs = [qspec, prev, cur, nxt, prev, cur, nxt, ctx, ctx, snk]
        args = (q, k, k, k, v, v, v, kx, vx, sink_tab)
    else:
        in_specs = [qspec, ctx, ctx, snk]
        args = (q, kx, vx, sink_tab)
    return pl.pallas_call(
        functools.partial(_attn_kernel, window=window, seq=s),
        out_shape=jax.ShapeDtypeStruct((b, s, D_ATT), BF16),
        grid=(b, nb),
        in_specs=in_specs,
        out_specs=qspec,
        compiler_params=_params("parallel", "parallel"),
        name="attention_window" if window else "attention_ctx",
    )(*args)


def _ret_state_kernel(lgl_ref, k_ref, v_ref, s0_ref, r_ref, fin_ref, s_scr, *, cb, reverse):
    c = pl.program_id(1)
    L = RET_CHUNK

    @pl.when(c == 0)
    def _():
        s_scr[...] = s0_ref[...]

    jj = lax.broadcasted_iota(jnp.int32, (L, LANES), 0).astype(F32)
    expo = jj if reverse else (L - 1.0) - jj
    same_head = (lax.broadcasted_iota(jnp.int32, (LANES, LANES), 0) // HEAD_DIM
                 == lax.broadcasted_iota(jnp.int32, (LANES, LANES), 1) // HEAD_DIM)
    for pr in range(RET_HEADS // 2):
        cs = slice(pr * LANES, (pr + 1) * LANES)
        lgl = lgl_ref[pr]
        kdec = jnp.exp(expo * lgl)
        cdec = jnp.exp(float(L) * lgl)
        state = s_scr[pr]
        for t in range(cb):
            cc = cb - 1 - t if reverse else t
            rows = slice(cc * L, (cc + 1) * L)
            r_ref[cc, pr] = state
            kd = k_ref[rows, cs].astype(F32) * kdec
            u = jnp.dot(kd.T.astype(BF16), v_ref[rows, cs], preferred_element_type=F32)
            state = cdec * state + jnp.where(same_head, u, 0.0)
        s_scr[pr] = state

    @pl.when(c == pl.num_programs(1) - 1)
    def _():
        fin_ref[...] = s_scr[...]


def _ret_states(lgl, rk, rv, s0, reverse):
    b, s, _ = rk.shape
    nc = s // RET_CHUNK
    cb = _pick_tile(nc, 8)
    nblk = nc // cb
    npair = RET_HEADS // 2
    blk_idx = (lambda c: nblk - 1 - c) if reverse else (lambda c: c)
    tok = pl.BlockSpec((None, cb * RET_CHUNK, D_RET), lambda bi, c: (bi, blk_idx(c), 0))
    st = pl.BlockSpec((None, npair, LANES, LANES), lambda bi, c: (bi, 0, 0, 0))
    return pl.pallas_call(
        functools.partial(_ret_state_kernel, cb=cb, reverse=reverse),
        out_shape=(jax.ShapeDtypeStruct((b, nc, npair, LANES, LANES), F32),
                   jax.ShapeDtypeStruct((b, npair, LANES, LANES), F32)),
        grid=(b, nblk),
        in_specs=[pl.BlockSpec(lgl.shape, lambda bi, c: (0, 0, 0)), tok, tok, st],
        out_specs=(pl.BlockSpec((None, cb, npair, LANES, LANES),
                                lambda bi, c: (bi, blk_idx(c), 0, 0, 0)), st),
        scratch_shapes=[pltpu.VMEM((npair, LANES, LANES), F32)],
        compiler_params=_params("parallel", "arbitrary"),
        name="ret_state_bwd" if reverse else "ret_state_fwd",
    )(lgl, rk, rv, s0)


def _ret_out_kernel(lgf_ref, lgb_ref, q_ref, k_ref, v_ref, g_ref, rf_ref, rb_ref, gnw_ref, o_ref, *, cb):
    L = RET_CHUNK
    i0 = lax.broadcasted_iota(jnp.int32, (L, LANES), 0).astype(F32)
    i1 = lax.broadcasted_iota(jnp.int32, (L, LANES), 1).astype(F32)
    diff = i0 - i1
    lo = lax.broadcasted_iota(jnp.int32, (L, LANES), 1) < HEAD_DIM
    for pr in range(RET_HEADS // 2):
        cs = slice(pr * LANES, (pr + 1) * LANES)
        lgf, lgb = lgf_ref[pr], lgb_ref[pr]
        xif = jnp.exp((i0 + 1.0) * lgf)
        xib = jnp.exp((float(L) - i0) * lgb)
        dmat = []
        for par in range(2):
            a = par * HEAD_DIM
            dmat.append(jnp.where(diff >= 0.0,
                                  jnp.exp(jnp.maximum(diff, 0.0) * lgf[:, a:a + 1]),
                                  jnp.exp(jnp.maximum(-diff, 0.0) * lgb[:, a:a + 1])))
        gnw = gnw_ref[:, cs]

        def chunk(t, carry):
            rows = pl.ds(pl.multiple_of(t * L, L), L)
            qp, kp, vp = q_ref[rows, cs], k_ref[rows, cs], v_ref[rows, cs]
            zero = jnp.zeros_like(kp)
            acc = jnp.zeros((L, LANES), F32)
            for par in range(2):
                keep = lo if par == 0 else jnp.logical_not(lo)
                a = lax.dot_general(qp, jnp.where(keep, kp, zero), (((1,), (1,)), ((), ())),
                                    preferred_element_type=F32)
                acc = acc + jnp.dot((a * dmat[par]).astype(BF16), jnp.where(keep, vp, zero),
                                    preferred_element_type=F32)
            qf = qp.astype(F32)
            acc = acc + jnp.dot((qf * xif).astype(BF16), rf_ref[t, pr].astype(BF16),
                                preferred_element_type=F32)
            acc = acc + jnp.dot((qf * xib).astype(BF16), rb_ref[t, pr].astype(BF16),
                                preferred_element_type=F32)
            inv = 1.0 / HEAD_DIM
            mean = jnp.where(lo, jnp.sum(jnp.where(lo, acc, 0.0), axis=-1, keepdims=True),
                             jnp.sum(jnp.where(lo, 0.0, acc), axis=-1, keepdims=True)) * inv
            dl = acc - mean
            d2 = dl * dl
            var = jnp.where(lo, jnp.sum(jnp.where(lo, d2, 0.0), axis=-1, keepdims=True),
                            jnp.sum(jnp.where(lo, 0.0, d2), axis=-1, keepdims=True)) * inv
            y = dl * lax.rsqrt(var + EPS) * gnw
            gt = g_ref[rows, cs]
            o_ref[rows, cs] = (gt * _sigmoid(gt) * y).astype(BF16)
            return carry

        lax.fori_loop(0, cb, chunk, 0)


def _ret_out(lgf, lgb, rq, rk, rv, rg, rf, rb, gnw):
    b, s, _ = rq.shape
    nc = s // RET_CHUNK
    cb = _pick_tile(nc, 8)
    npair = RET_HEADS // 2
    tok = pl.BlockSpec((None, cb * RET_CHUNK, D_RET), lambda bi, c: (bi, c, 0))
    st = pl.BlockSpec((None, cb, npair, LANES, LANES), lambda bi, c: (bi, c, 0, 0, 0))
    lg = pl.BlockSpec(lgf.shape, lambda bi, c: (0, 0, 0))
    return pl.pallas_call(
        functools.partial(_ret_out_kernel, cb=cb),
        out_shape=jax.ShapeDtypeStruct((b, s, D_RET), BF16),
        grid=(b, nc // cb),
        in_specs=[lg, lg, tok, tok, tok, tok, st, st, pl.BlockSpec((1, D_RET), lambda bi, c: (0, 0))],
        out_specs=tok,
        compiler_params=_params("parallel", "parallel"),
        name="ret_out",
    )(lgf, lgb, rq, rk, rv, rg, rf, rb, gnw)


def _conv_kernel(prev_ref, cur_ref, next_ref, w_ref, b_ref, lnw_ref, lnb_ref, o_ref, xpad, shifted):
    i = pl.program_id(1)
    tm = cur_ref.shape[0]
    halo = CONV_HALO
    first = i == 0
    last = i == pl.num_programs(1) - 1
    xpad[0:halo, :] = jnp.where(first, 0.0, prev_ref[...])
    xpad[halo:halo + tm, :] = cur_ref[...]
    xpad[halo + tm:2 * halo + tm, :] = jnp.where(last, 0.0, next_ref[...])
    span = tm + 2 * halo - SUBLANES
    for r in range(SUBLANES):
        shifted[r] = xpad[r:r + span, :]
    base = halo - CONV_WIDTH // 2
    bias, lnw, lnb = b_ref[...], lnw_ref[...], lnb_ref[...]

    def sub(sb, carry):
        r0 = pl.multiple_of(sb * CONV_SUB, CONV_SUB)
        acc = jnp.zeros((CONV_SUB, CONV_CH), F32)
        for w in range(CONV_WIDTH):
            off = base + w
            xs = shifted[off % SUBLANES, pl.ds(r0 + (off // SUBLANES) * SUBLANES, CONV_SUB), :]
            acc = acc + xs * w_ref[w:w + 1, :]
        h = acc + bias
        mu = jnp.mean(h, axis=-1, keepdims=True)
        dl = h - mu
        var = jnp.mean(dl * dl, axis=-1, keepdims=True)
        y = dl * lax.rsqrt(var + EPS) * lnw + lnb
        o_ref[pl.ds(r0, CONV_SUB), :] = (y * _sigmoid(y)).astype(BF16)
        return carry

    lax.fori_loop(0, tm // CONV_SUB, sub, 0)


def _conv(cv, w, bias, lnw, lnb):
    b, s, ch = cv.shape
    tm = _pick_tile(s, 512)
    hpb = tm // CONV_HALO
    nh = s // CONV_HALO
    row = pl.BlockSpec((1, ch), lambda bi, i: (0, 0))
    span = tm + 2 * CONV_HALO - SUBLANES
    return pl.pallas_call(
        _conv_kernel,
        out_shape=jax.ShapeDtypeStruct((b, s, ch), BF16),
        grid=(b, s // tm),
        in_specs=[pl.BlockSpec((None, CONV_HALO, ch), lambda bi, i: (bi, jnp.maximum(i * hpb - 1, 0), 0)),
                  pl.BlockSpec((None, tm, ch), lambda bi, i: (bi, i, 0)),
                  pl.BlockSpec((None, CONV_HALO, ch),
                               lambda bi, i: (bi, jnp.minimum((i + 1) * hpb, nh - 1), 0)),
                  pl.BlockSpec(w.shape, lambda bi, i: (0, 0)), row, row, row],
        out_specs=pl.BlockSpec((None, tm, ch), lambda bi, i: (bi, i, 0)),
        scratch_shapes=[pltpu.VMEM((tm + 2 * CONV_HALO, ch), F32),
                        pltpu.VMEM((SUBLANES, span, ch), F32)],
        compiler_params=_params("parallel", "parallel"),
        name="conv",
    )(cv, cv, cv, w, bias, lnw, lnb)


def _outproj_kernel(*refs, route):
    if route:
        (att_ref, ret_ref, cnv_ref, w_ref, x_ref, g1_ref, nw_ref, sc_ref, sh_ref, rhi_ref, rlo_ref,
         xo_ref, h_ref, rt_ref) = refs
    else:
        att_ref, ret_ref, cnv_ref, w_ref, x_ref, g1_ref, nw_ref, sc_ref, sh_ref, xo_ref, h_ref = refs
    y = (jnp.dot(att_ref[...], w_ref[0:D_ATT, :], preferred_element_type=F32)
         + jnp.dot(ret_ref[...], w_ref[D_ATT:D_ATT + D_RET, :], preferred_element_type=F32)
         + jnp.dot(cnv_ref[...], w_ref[D_ATT + D_RET:, :], preferred_element_type=F32))
    xn = x_ref[...] + g1_ref[...] * y
    xo_ref[...] = xn
    h = _modulated_rms(xn, nw_ref[...], sc_ref[...], sh_ref[...])
    h_ref[...] = h.astype(h_ref.dtype)
    if route:
        hi = h.astype(BF16)
        lo = (h - hi.astype(F32)).astype(BF16)
        logits = (jnp.dot(hi, rhi_ref[...], preferred_element_type=F32)
                  + jnp.dot(hi, rlo_ref[...], preferred_element_type=F32)
                  + jnp.dot(lo, rhi_ref[...], preferred_element_type=F32))
        lane = lax.broadcasted_iota(jnp.int32, logits.shape, 1).astype(F32)
        logits = jnp.where(lane < N_EXPERTS, logits, NEG_INF)
        m1 = jnp.max(logits, axis=-1, keepdims=True)
        i1 = jnp.min(jnp.where(logits == m1, lane, float(LANES)), axis=-1, keepdims=True)
        rest = jnp.where(lane == i1, NEG_INF, logits)
        m2 = jnp.max(rest, axis=-1, keepdims=True)
        i2 = jnp.min(jnp.where(rest == m2, lane, float(LANES)), axis=-1, keepdims=True)
        e2 = jnp.exp(m2 - m1)
        w1 = 1.0 / (1.0 + e2)
        w2 = e2 / (1.0 + e2)
        rt_ref[...] = jnp.where(lane == 0.0, i1,
                                jnp.where(lane == 1.0, i2,
                                          jnp.where(lane == 2.0, w1, jnp.where(lane == 3.0, w2, 0.0))))


def _outproj(att, ret, cnv, w_bf, x, g1, nw, sc, sh, router=None, h_dtype=BF16):
    b, s, d = x.shape
    tm = _pick_tile(s, 512)
    tok = lambda n: pl.BlockSpec((None, tm, n), lambda bi, i: (bi, i, 0))
    per_b = pl.BlockSpec((None, 1, d), lambda bi, i: (bi, 0, 0))
    row = pl.BlockSpec((1, d), lambda bi, i: (0, 0))
    in_specs = [tok(D_ATT), tok(D_RET), tok(CONV_CH), pl.BlockSpec(w_bf.shape, lambda bi, i: (0, 0)),
                tok(d), per_b, row, per_b, per_b]
    args = [att, ret, cnv, w_bf, x, g1, nw, sc, sh]
    out_shape = [jax.ShapeDtypeStruct((b, s, d), F32), jax.ShapeDtypeStruct((b, s, d), h_dtype)]
    out_specs = [tok(d), tok(d)]
    if router is not None:
        rpad = jnp.zeros((d, LANES), F32).at[:, :N_EXPERTS].set(router)
        rhi = rpad.astype(BF16)
        rlo = (rpad - rhi.astype(F32)).astype(BF16)
        in_specs += [pl.BlockSpec((d, LANES), lambda bi, i: (0, 0))] * 2
        args += [rhi, rlo]
        out_shape.append(jax.ShapeDtypeStruct((b, s, LANES), F32))
        out_specs.append(tok(LANES))
    return pl.pallas_call(
        functools.partial(_outproj_kernel, route=router is not None),
        out_shape=tuple(out_shape),
        grid=(b, s // tm),
        in_specs=in_specs,
        out_specs=tuple(out_specs),
        compiler_params=_params("parallel", "parallel"),
        name="outproj_route" if router is not None else "outproj",
    )(*args)


def _ffn_kernel(h_ref, wg_ref, wu_ref, wd_ref, x_ref, g2_ref, o_ref, acc):
    j = pl.program_id(2)

    @pl.when(j == 0)
    def _():
        acc[...] = jnp.zeros_like(acc)

    h = h_ref[...]
    gate = jnp.dot(h, wg_ref[...], preferred_element_type=F32)
    up = jnp.dot(h, wu_ref[...], preferred_element_type=F32)
    act = (gate * _sigmoid(gate) * up).astype(BF16)
    acc[...] += jnp.dot(act, wd_ref[...], preferred_element_type=F32)

    @pl.when(j == pl.num_programs(2) - 1)
    def _():
        o_ref[...] = x_ref[...] + g2_ref[...] * acc[...]


def _ffn(h, wg, wu, wd, x, g2):
    b, s, d = x.shape
    f = wg.shape[1]
    tm = _pick_tile(s, 512)
    tf = f // 2 if (f // 2) % LANES == 0 else f
    tok = lambda: pl.BlockSpec((None, tm, d), lambda bi, i, j: (bi, i, 0))
    return pl.pallas_call(
        _ffn_kernel,
        out_shape=jax.ShapeDtypeStruct((b, s, d), F32),
        grid=(b, s // tm, f // tf),
        in_specs=[tok(),
                  pl.BlockSpec((d, tf), lambda bi, i, j: (0, j)),
                  pl.BlockSpec((d, tf), lambda bi, i, j: (0, j)),
                  pl.BlockSpec((tf, d), lambda bi, i, j: (j, 0)),
                  tok(),
                  pl.BlockSpec((None, 1, d), lambda bi, i, j: (bi, 0, 0))],
        out_specs=tok(),
        scratch_shapes=[pltpu.VMEM((tm, d), F32)],
        compiler_params=_params("parallel", "parallel", "arbitrary"),
        name="ffn_dense",
    )(h, wg, wu, wd, x, g2)


def _gather_kernel(idx_ref, src_ref, o_ref, sem):
    base = pl.program_id(0) * GATHER_TG

    def copy(r, src_row):
        return pltpu.make_async_copy(src_ref.at[pl.ds(src_row, 1)], o_ref.at[pl.ds(r, 1)], sem)

    def issue(r, carry):
        copy(r, idx_ref[base + r]).start()
        return carry

    def drain(r, carry):
        copy(r, 0).wait()
        return carry

    lax.fori_loop(0, GATHER_TG, issue, 0)
    lax.fori_loop(0, GATHER_TG, drain, 0)


def _gather_rows(idx, src):
    n = idx.shape[0]
    d = src.shape[1]
    return pl.pallas_call(
        _gather_kernel,
        out_shape=jax.ShapeDtypeStruct((n, d), src.dtype),
        grid_spec=pltpu.PrefetchScalarGridSpec(
            num_scalar_prefetch=1,
            grid=(n // GATHER_TG,),
            in_specs=[pl.BlockSpec(memory_space=pl.ANY)],
            out_specs=pl.BlockSpec((GATHER_TG, d), lambda i, idx: (i, 0)),
            scratch_shapes=[pltpu.SemaphoreType.DMA]),
        compiler_params=_params("arbitrary"),
        name="moe_gather",
    )(idx, src)


def _moe_kernel(te_ref, nu_ref, x_ref, wg_ref, wu_ref, wd_ref, o_ref, xb, acc):
    t = pl.program_id(0)
    j = pl.program_id(1)
    nj = pl.num_programs(1)
    used = t < nu_ref[0]

    @pl.when(used & (j == 0))
    def _():
        xb[...] = x_ref[...].astype(BF16)
        acc[...] = jnp.zeros_like(acc)

    @pl.when(used)
    def _():
        h = xb[...]
        gate = jnp.dot(h, wg_ref[...], preferred_element_type=F32)
        up = jnp.dot(h, wu_ref[...], preferred_element_type=F32)
        act = (gate * _sigmoid(gate) * up).astype(BF16)
        acc[...] += jnp.dot(act, wd_ref[...], preferred_element_type=F32)

    @pl.when(used & (j == nj - 1))
    def _():
        o_ref[...] = acc[...]

    @pl.when(jnp.logical_not(used) & (j == nj - 1))
    def _():
        o_ref[...] = jnp.zeros_like(o_ref)


def _moe_grouped(tile_expert, n_used, xs, wg, wu, wd):
    p, d = xs.shape
    f = wg.shape[2]
    tm = MOE_TM
    tf = f // 4
    nj = f // tf

    def jj(t, j, te, nu):
        return jnp.where(t < nu[0], j, nj - 1)

    return pl.pallas_call(
        _moe_kernel,
        out_shape=jax.ShapeDtypeStruct((p, d), F32),
        grid_spec=pltpu.PrefetchScalarGridSpec(
            num_scalar_prefetch=2,
            grid=(p // tm, nj),
            in_specs=[pl.BlockSpec((tm, d), lambda t, j, te, nu: (t, 0)),
                      pl.BlockSpec((None, d, tf), lambda t, j, te, nu: (te[t], 0, jj(t, j, te, nu))),
                      pl.BlockSpec((None, d, tf), lambda t, j, te, nu: (te[t], 0, jj(t, j, te, nu))),
                      pl.BlockSpec((None, tf, d), lambda t, j, te, nu: (te[t], jj(t, j, te, nu), 0))],
            out_specs=pl.BlockSpec((tm, d), lambda t, j, te, nu: (t, 0)),
            scratch_shapes=[pltpu.VMEM((tm, d), BF16), pltpu.VMEM((tm, d), F32)]),
        compiler_params=_params("arbitrary", "arbitrary"),
        name="moe_grouped",
    )(tile_expert, n_used, xs, wg, wu, wd)


def _combine_kernel(pos_ref, y_ref, x_ref, g2_ref, rt_ref, o_ref, b1, b2, sem):
    n_tok = pos_ref.shape[0] // 2
    base = (pl.program_id(0) * pl.num_programs(1) + pl.program_id(1)) * GATHER_TG

    def copy(r, src_row, buf):
        return pltpu.make_async_copy(y_ref.at[pl.ds(src_row, 1)], buf.at[pl.ds(r, 1)], sem)

    def issue(r, carry):
        copy(r, pos_ref[base + r], b1).start()
        copy(r, pos_ref[n_tok + base + r], b2).start()
        return carry

    def drain(r, carry):
        copy(r, 0, b1).wait()
        copy(r, 0, b2).wait()
        return carry

    lax.fori_loop(0, GATHER_TG, issue, 0)
    lax.fori_loop(0, GATHER_TG, drain, 0)
    rt = rt_ref[...]
    o_ref[...] = x_ref[...] + g2_ref[...] * (rt[:, 2:3] * b1[...] + rt[:, 3:4] * b2[...])


def _moe_combine(pos, y, x, g2, route):
    b, s, d = x.shape
    tg = GATHER_TG
    tok = lambda n: pl.BlockSpec((None, tg, n), lambda bi, i, pos: (bi, i, 0))
    return pl.pallas_call(
        _combine_kernel,
        out_shape=jax.ShapeDtypeStruct((b, s, d), F32),
        grid_spec=pltpu.PrefetchScalarGridSpec(
            num_scalar_prefetch=1,
            grid=(b, s // tg),
            in_specs=[pl.BlockSpec(memory_space=pl.ANY), tok(d),
                      pl.BlockSpec((None, 1, d), lambda bi, i, pos: (bi, 0, 0)), tok(LANES)],
            out_specs=tok(d),
            scratch_shapes=[pltpu.VMEM((tg, d), F32), pltpu.VMEM((tg, d), F32),
                            pltpu.SemaphoreType.DMA]),
        compiler_params=_params("arbitrary", "arbitrary"),
        name="moe_combine",
    )(pos, y, x, g2, route)


def _moe(h, route, x, g2, wg, wu, wd):
    b, s, d = x.shape
    n_tok = b * s
    tm = MOE_TM
    rt = route.reshape(n_tok, LANES)
    flat_e = jnp.concatenate([rt[:, 0], rt[:, 1]]).astype(jnp.int32)
    onehot = (flat_e[:, None] == jnp.arange(N_EXPERTS, dtype=jnp.int32)[None, :]).astype(jnp.int32)
    csum = jnp.cumsum(onehot, axis=0)
    rank = jnp.sum(csum * onehot, axis=1) - 1
    counts = csum[-1]
    tiles_e = (counts + tm - 1) // tm
    tiles_cum = jnp.cumsum(tiles_e)
    row_start = (tiles_cum - tiles_e) * tm
    dest = jnp.sum(onehot * row_start[None, :], axis=1) + rank
    n_tiles = 2 * n_tok // tm + N_EXPERTS
    tile_ids = jnp.arange(n_tiles, dtype=jnp.int32)
    tile_expert = jnp.sum((tile_ids[:, None] >= tiles_cum[None, :]).astype(jnp.int32), axis=1)
    last_e = jnp.max(jnp.where(tiles_e > 0, jnp.arange(N_EXPERTS, dtype=jnp.int32), 0))
    tile_expert = jnp.minimum(tile_expert, last_e).astype(jnp.int32)
    n_used = tiles_cum[-1:].astype(jnp.int32)
    token = jnp.tile(jnp.arange(n_tok, dtype=jnp.int32), 2)
    row_src = jnp.zeros((n_tiles * tm,), jnp.int32).at[dest].set(token)
    xs = _gather_rows(row_src, h.reshape(n_tok, d))
    y = _moe_grouped(tile_expert, n_used, xs, wg, wu, wd)
    return _moe_combine(dest.astype(jnp.int32), y, x, g2, route)


def _rope_tables(n):
    rows = n // GRID_W
    r = jnp.repeat(jnp.arange(rows), GRID_W).astype(F32)
    col = jnp.tile(jnp.arange(GRID_W), rows).astype(F32)
    freqs = ROPE_BASE ** (-jnp.arange(ROPE_FREQS, dtype=F32) / ROPE_FREQS)
    ang = jnp.stack([r[:, None] * freqs, col[:, None] * freqs], axis=1)
    ang = jnp.repeat(ang[:, :, None, :], 2, axis=2).reshape(n, HEAD_DIM)
    ang = jnp.tile(ang, (1, LANES // HEAD_DIM))
    cos, sin = jnp.cos(ang), jnp.sin(ang)
    first_half = (jnp.arange(LANES) % (2 * ROPE_FREQS)) < ROPE_FREQS
    return cos, jnp.where(first_half, -sin, 0.0), jnp.where(first_half, 0.0, sin)


def _lane_rows(lg):
    return jnp.repeat(lg.astype(F32), HEAD_DIM).reshape(RET_HEADS // 2, 1, LANES)


def kernel(x, c, ctx, c_ctx, ada_w, ada_b, norm1_w, norm2_w, w_in, w_out, q_norm_w, k_norm_w,
           attn_sink, ret_decay_f, ret_decay_b, ret_gn_w, conv_w, conv_b, conv_ln_w, conv_ln_b,
           ffn_w_gate, ffn_w_up, ffn_w_down, router_w, moe_w_gate, moe_w_up, moe_w_down):
    b, n, d = x.shape
    n_ctx = ctx.shape[1]
    depth = ada_w.shape[0]
    cond = jnp.zeros((SUBLANES, d), F32).at[0:b].set(c).at[b].set(c_ctx)
    mods = _adaln(cond, ada_w, ada_b).reshape(depth, SUBLANES, 6, d)
    cos, sa, sb = _rope_tables(n)
    ones_c = jnp.ones((n_ctx, LANES), F32)
    zeros_c = jnp.zeros((n_ctx, LANES), F32)
    zero_state = jnp.zeros((b, RET_HEADS // 2, LANES, LANES), F32)
    row = lambda v: v.reshape(1, -1)
    for l in range(depth):
        last = l == depth - 1
        m_lat = [mods[l, 0:b, k][:, None, :] for k in range(6)]
        m_ctx = [jnp.broadcast_to(mods[l, b, k][None, None, :], (b, 1, d)) for k in range(6)]
        w_in_bf = w_in[l].astype(BF16)
        w_out_bf = w_out[l].astype(BF16)
        qw = row(jnp.tile(q_norm_w[l], LANES // HEAD_DIM))
        kw = row(jnp.tile(k_norm_w[l], LANES // HEAD_DIM))
        lgf = _lane_rows(jax.nn.log_sigmoid(ret_decay_f[l].astype(F32)))
        lgb = _lane_rows(jax.nn.log_sigmoid(ret_decay_b[l].astype(F32)))
        sink_tab = jnp.broadcast_to(attn_sink[l].astype(F32)[:, None], (ATT_Q_HEADS, LANES))

        q, k, v, rk, rv, rq, rg, cv = _inproj(x, row(norm1_w[l]), m_lat[1], m_lat[0], w_in_bf,
                                               cos, sa, sb, qw, kw)
        qc, kc, vc, rkc, rvc, rqc, rgc, cvc = _inproj(ctx, row(norm1_w[l]), m_ctx[1], m_ctx[0], w_in_bf,
                                                       ones_c, zeros_c, zeros_c, qw, kw)
        rf_c, s_f = _ret_states(lgf, rkc, rvc, zero_state, reverse=False)
        rb_c, s_b = _ret_states(lgb, rkc, rvc, zero_state, reverse=True)
        rf, _ = _ret_states(lgf, rk, rv, s_f, reverse=False)
        rb, _ = _ret_states(lgb, rk, rv, s_b, reverse=True)

        att = _attention(q, k, v, kc, vc, sink_tab, window=True)
        ret = _ret_out(lgf, lgb, rq, rk, rv, rg, rf, rb, row(ret_gn_w[l]))
        cnv = _conv(cv, conv_w[l], row(conv_b[l]), row(conv_ln_w[l]), row(conv_ln_b[l]))

        if l % 2 == 0:
            i = l // 2
            wg, wu, wd = ffn_w_gate[i].astype(BF16), ffn_w_up[i].astype(BF16), ffn_w_down[i].astype(BF16)
            x_mid, h2 = _outproj(att, ret, cnv, w_out_bf, x, m_lat[2], row(norm2_w[l]), m_lat[4], m_lat[3])
            x_new = _ffn(h2, wg, wu, wd, x_mid, m_lat[5])
        else:
            i = l // 2
            wg, wu, wd = moe_w_gate[i].astype(BF16), moe_w_up[i].astype(BF16), moe_w_down[i].astype(BF16)
            x_mid, h2, route = _outproj(att, ret, cnv, w_out_bf, x, m_lat[2], row(norm2_w[l]),
                                        m_lat[4], m_lat[3], router=router_w[i], h_dtype=F32)
            x_new = _moe(h2, route, x_mid, m_lat[5], wg, wu, wd)

        if not last:
            att_c = _attention(qc, None, None, kc, vc, sink_tab, window=False)
            ret_c = _ret_out(lgf, lgb, rqc, rkc, rvc, rgc, rf_c, rb_c, row(ret_gn_w[l]))
            cnv_c = _conv(cvc, conv_w[l], row(conv_b[l]), row(conv_ln_w[l]), row(conv_ln_b[l]))
            if l % 2 == 0:
                c_mid, h2c = _outproj(att_c, ret_c, cnv_c, w_out_bf, ctx, m_ctx[2], row(norm2_w[l]),
                                      m_ctx[4], m_ctx[3])
                ctx = _ffn(h2c, wg, wu, wd, c_mid, m_ctx[5])
            else:
                c_mid, h2c, route_c = _outproj(att_c, ret_c, cnv_c, w_out_bf, ctx, m_ctx[2],
                                               row(norm2_w[l]), m_ctx[4], m_ctx[3],
                                               router=router_w[i], h_dtype=F32)
                ctx = _moe(h2c, route_c, c_mid, m_ctx[5], wg, wu, wd)
        x = x_new
    return x
```

```python
import functools

import jax
import jax.numpy as jnp
from jax import lax
from jax.experimental import pallas as pl
from jax.experimental.pallas import tpu as pltpu

F32 = jnp.float32
BF16 = jnp.bfloat16

GRID_W = 64
HEAD_DIM = 64
ATT_Q_HEADS = 8
ATT_KV_HEADS = 2
ATT_WINDOW = 128
ATT_BLOCK = 128
RET_HEADS = 4
RET_CHUNK = 128
RET_K_SCALE = HEAD_DIM ** -0.5
ATT_SCALE = HEAD_DIM ** -0.5
CONV_CH = 256
CONV_WIDTH = 31
ROPE_BASE = 10000.0
ROPE_FREQS = HEAD_DIM // 4
D_ATT = ATT_Q_HEADS * HEAD_DIM
D_RET = RET_HEADS * HEAD_DIM
ATT_KV_W = ATT_KV_HEADS * HEAD_DIM
C_ATT_K = 0
C_ATT_V = C_ATT_K + ATT_KV_W
C_RET_K = C_ATT_V + ATT_KV_W
C_RET_V = C_RET_K + D_RET
C_ATT_Q = C_RET_V + D_RET
C_RET_Q = C_ATT_Q + D_ATT
C_RET_G = C_RET_Q + D_RET
C_CONV = C_RET_G + D_RET
N_EXPERTS = 8
EPS = 1e-6
NEG_INF = -1e30

LANES = 128
SUBLANES = 8
VMEM_LIMIT = 48 * 1024 * 1024
CONV_HALO = 16
CONV_SUB = 16
CONV_CHAINS = 4
MOE_TM = 512
GATHER_TG = 256


def _params(*sem):
    return pltpu.CompilerParams(dimension_semantics=sem, vmem_limit_bytes=VMEM_LIMIT)


def _sigmoid(x):
    return 1.0 / (1.0 + jnp.exp(-x))


def _pick_tile(n, pref):
    t = min(n, pref)
    assert n % t == 0, (n, t)
    return t


def _adaln_kernel(c_ref, w_ref, b_ref, o_ref):
    c = c_ref[...]
    s = c * _sigmoid(c)
    o_ref[...] = jnp.dot(s, w_ref[...], preferred_element_type=F32,
                         precision=lax.Precision.HIGHEST) + b_ref[...]


def _adaln(cond, ada_w, ada_b):
    depth, d, n = ada_w.shape
    tn = _pick_tile(n, 1536)
    return pl.pallas_call(
        _adaln_kernel,
        out_shape=jax.ShapeDtypeStruct((depth, cond.shape[0], n), F32),
        grid=(depth, n // tn),
        in_specs=[pl.BlockSpec(cond.shape, lambda l, j: (0, 0)),
                  pl.BlockSpec((None, d, tn), lambda l, j: (l, 0, j)),
                  pl.BlockSpec((None, 1, tn), lambda l, j: (l, 0, j))],
        out_specs=pl.BlockSpec((None, cond.shape[0], tn), lambda l, j: (l, 0, j)),
        compiler_params=_params("parallel", "parallel"),
        name="adaln",
    )(cond, ada_w, ada_b.reshape(depth, 1, n))


def _modulated_rms(x, nw, sc, sh):
    ms = jnp.mean(x * x, axis=-1, keepdims=True)
    return (x * lax.rsqrt(ms + EPS) * nw) * (1.0 + sc) + sh


def _head_group_matrix():
    r = lax.broadcasted_iota(jnp.int32, (2 * LANES, 2 * LANES), 0) // HEAD_DIM
    c = lax.broadcasted_iota(jnp.int32, (2 * LANES, 2 * LANES), 1) // HEAD_DIM
    return jnp.where(r == c, 1.0, 0.0).astype(BF16)


def _per_head_sum(v, grp):
    hi = v.astype(BF16)
    lo = (v - hi.astype(F32)).astype(BF16)
    r = jnp.dot(jnp.concatenate([hi, lo], axis=1), grp, preferred_element_type=F32)
    return r[:, 0:LANES] + r[:, LANES:]


def _dup_halves(t):
    sw = pltpu.roll(t, HEAD_DIM, 1)
    lo = lax.broadcasted_iota(jnp.int32, t.shape, 1) < HEAD_DIM
    return jnp.where(lo, t, sw), jnp.where(lo, sw, t)


def _inproj_kernel(x_ref, nw_ref, sc_ref, sh_ref, w_ref, cos_ref, sa_ref, sb_ref, qw_ref, kw_ref,
                   q_ref, k_ref, v_ref, rk_ref, rv_ref, rq_ref, rg_ref, cv_ref):
    hb = _modulated_rms(x_ref[...], nw_ref[...], sc_ref[...], sh_ref[...]).astype(BF16)

    def proj(c0, n):
        return jnp.dot(hb, w_ref[:, c0:c0 + n], preferred_element_type=F32)

    grp = _head_group_matrix()
    cos, sa, sb = cos_ref[...], sa_ref[...], sb_ref[...]

    def norm_rope(p, wrow):
        y = p * lax.rsqrt(_per_head_sum(p * p, grp) * (1.0 / HEAD_DIM) + EPS) * wrow
        return (y * cos + pltpu.roll(y, LANES - ROPE_FREQS, 1) * sa
                + pltpu.roll(y, ROPE_FREQS, 1) * sb)

    kv = proj(C_ATT_K, 2 * ATT_KV_W)
    k0, k1 = _dup_halves(norm_rope(kv[:, 0:ATT_KV_W], kw_ref[...]))
    k_ref[:, 0:LANES] = k0.astype(BF16)
    k_ref[:, LANES:2 * LANES] = k1.astype(BF16)
    v_ref[...] = kv[:, ATT_KV_W:].T.astype(BF16)
    qw = qw_ref[...] * ATT_SCALE
    qall = proj(C_ATT_Q, D_ATT)
    for j in range(D_ATT // LANES):
        q_ref[:, j * LANES:(j + 1) * LANES] = norm_rope(qall[:, j * LANES:(j + 1) * LANES], qw).astype(BF16)
    rk_ref[...] = (proj(C_RET_K, D_RET) * RET_K_SCALE).astype(BF16)
    rv_ref[...] = proj(C_RET_V, D_RET).astype(BF16)
    rq_ref[...] = proj(C_RET_Q, D_RET).astype(BF16)
    rg_ref[...] = proj(C_RET_G, D_RET)
    cv_ref[...] = proj(C_CONV, CONV_CH) * _sigmoid(proj(C_CONV + CONV_CH, CONV_CH))


def _inproj(x, nw, sc, sh, w_bf, cos, sa, sb, qw, kw):
    b, s, d = x.shape
    tm = _pick_tile(s, 512)
    row = lambda n: pl.BlockSpec((1, n), lambda bi, i: (0, 0))
    per_b = pl.BlockSpec((None, 1, d), lambda bi, i: (bi, 0, 0))
    tab = pl.BlockSpec((tm, LANES), lambda bi, i: (i, 0))
    tok = lambda n: pl.BlockSpec((None, tm, n), lambda bi, i: (bi, i, 0))
    shp = lambda n, dt: jax.ShapeDtypeStruct((b, s, n), dt)
    return pl.pallas_call(
        _inproj_kernel,
        out_shape=(shp(D_ATT, BF16), shp(2 * LANES, BF16), jax.ShapeDtypeStruct((b, ATT_KV_W, s), BF16),
                   shp(D_RET, BF16), shp(D_RET, BF16), shp(D_RET, BF16), shp(D_RET, F32),
                   shp(CONV_CH, F32)),
        grid=(b, s // tm),
        in_specs=[tok(d), row(d), per_b, per_b,
                  pl.BlockSpec(w_bf.shape, lambda bi, i: (0, 0)),
                  tab, tab, tab, row(LANES), row(LANES)],
        out_specs=(tok(D_ATT), tok(2 * LANES), pl.BlockSpec((None, ATT_KV_W, tm), lambda bi, i: (bi, 0, i)),
                   tok(D_RET), tok(D_RET), tok(D_RET), tok(D_RET), tok(CONV_CH)),
        compiler_params=_params("parallel", "parallel"),
        name="inproj",
    )(x, nw, sc, sh, w_bf, cos, sa, sb, qw, kw)


def _attn_kernel(*refs, window):
    if window:
        q_ref, kp_ref, kc_ref, kn_ref, vp_ref, vc_ref, vn_ref, kx_ref, vx_ref, sink_ref, o_ref = refs
    else:
        q_ref, kx_ref, vx_ref, sink_ref, o_ref = refs
    blk = ATT_BLOCK
    n_ctx = kx_ref.shape[0]
    nk = 3 * blk + n_ctx if window else n_ctx
    if window:
        i = pl.program_id(1)
        key = lax.broadcasted_iota(jnp.int32, (blk, 2 * blk), 0)
        qry = lax.broadcasted_iota(jnp.int32, (blk, 2 * blk), 1) & (blk - 1)
        mask_prev = (key >= qry) & (i > 0)
        mask_next = (key <= qry) & (i < pl.num_programs(1) - 1)
    first_head = lax.broadcasted_iota(jnp.int32, (blk, LANES), 1) < HEAD_DIM
    ones = jnp.ones((2 * SUBLANES, nk), BF16)
    for g in range(ATT_KV_HEADS):
        gs = slice(g * LANES, (g + 1) * LANES)
        vs = slice(g * HEAD_DIM, (g + 1) * HEAD_DIM)
        if window:
            kcat = jnp.concatenate([kp_ref[:, gs], kc_ref[:, gs], kn_ref[:, gs], kx_ref[:, gs]], axis=0)
            vt = jnp.concatenate([vp_ref[vs, :], vc_ref[vs, :], vn_ref[vs, :], vx_ref[vs, :]], axis=1)
        else:
            kcat, vt = kx_ref[:, gs], vx_ref[vs, :]
        vaug = jnp.concatenate([vt, ones], axis=0)
        for pr in range(2):
            pair = 2 * g + pr
            cs = slice(pair * LANES, (pair + 1) * LANES)
            qp = q_ref[:, cs]
            zero = jnp.zeros_like(qp)
            w = jnp.concatenate([jnp.where(first_head, qp, zero), jnp.where(first_head, zero, qp)], axis=0)
            s = lax.dot_general(kcat, w, (((1,), (1,)), ((), ())), preferred_element_type=F32)
            if window:
                parts = [jnp.where(mask_prev, s[0:blk], NEG_INF), s[blk:2 * blk],
                         jnp.where(mask_next, s[2 * blk:3 * blk], NEG_INF), s[3 * blk:]]
            else:
                parts = [s]
            snk = sink_ref[pair:pair + 1, :]
            m = snk
            for part in parts:
                m = jnp.maximum(m, jnp.max(part, axis=0, keepdims=True))
            p = jnp.concatenate([jnp.exp(part - m).astype(BF16) for part in parts], axis=0)
            o = jnp.dot(vaug, p, preferred_element_type=F32)
            den = o[HEAD_DIM:HEAD_DIM + 1, :] + jnp.exp(snk - m)
            on = o[0:HEAD_DIM, :] * (1.0 / den)
            ot = jnp.concatenate([on[:, 0:blk], on[:, blk:2 * blk]], axis=0)
            o_ref[:, cs] = ot.T.astype(BF16)


def _attention(q, k, vt, kx, vxt, sink_tab, window):
    b, s, _ = q.shape
    blk = ATT_BLOCK
    nb = s // blk
    n_ctx = kx.shape[1]
    qspec = pl.BlockSpec((None, blk, D_ATT), lambda bi, i: (bi, i, 0))
    kctx = pl.BlockSpec((None, n_ctx, 2 * LANES), lambda bi, i: (bi, 0, 0))
    vctx = pl.BlockSpec((None, ATT_KV_W, n_ctx), lambda bi, i: (bi, 0, 0))
    snk = pl.BlockSpec(sink_tab.shape, lambda bi, i: (0, 0))
    if window:
        prev = lambda i: jnp.maximum(i - 1, 0)
        nxt = lambda i: jnp.minimum(i + 1, nb - 1)
        same = lambda i: i
        kspec = lambda f: pl.BlockSpec((None, blk, 2 * LANES), lambda bi, i: (bi, f(i), 0))
        vspec = lambda f: pl.BlockSpec((None, ATT_KV_W, blk), lambda bi, i: (bi, 0, f(i)))
        in_specs = [qspec, kspec(prev), kspec(same), kspec(nxt), vspec(prev), vspec(same), vspec(nxt),
                    kctx, vctx, snk]
        args = (q, k, k, k, vt, vt, vt, kx, vxt, sink_tab)
    else:
        in_specs = [qspec, kctx, vctx, snk]
        args = (q, kx, vxt, sink_tab)
    return pl.pallas_call(
        functools.partial(_attn_kernel, window=window),
        out_shape=jax.ShapeDtypeStruct((b, s, D_ATT), BF16),
        grid=(b, nb),
        in_specs=in_specs,
        out_specs=qspec,
        compiler_params=_params("parallel", "parallel"),
        name="attention_window" if window else "attention_ctx",
    )(*args)


def _ret_state_kernel(lgl_ref, k_ref, v_ref, s0_ref, r_ref, fin_ref, s_scr, *, cb, reverse):
    c = pl.program_id(1)
    L = RET_CHUNK

    @pl.when(c == 0)
    def _():
        s_scr[...] = s0_ref[...]

    jj = lax.broadcasted_iota(jnp.int32, (L, LANES), 0).astype(F32)
    expo = jj if reverse else (L - 1.0) - jj
    same_head = (lax.broadcasted_iota(jnp.int32, (LANES, LANES), 0) // HEAD_DIM
                 == lax.broadcasted_iota(jnp.int32, (LANES, LANES), 1) // HEAD_DIM)
    for pr in range(RET_HEADS // 2):
        cs = slice(pr * LANES, (pr + 1) * LANES)
        lgl = lgl_ref[pr]
        kdec = jnp.exp(expo * lgl)
        cdec = jnp.exp(float(L) * lgl)
        state = s_scr[pr]
        for t in range(cb):
            cc = cb - 1 - t if reverse else t
            rows = slice(cc * L, (cc + 1) * L)
            r_ref[cc, pr] = state
            kd = k_ref[rows, cs].astype(F32) * kdec
            u = jnp.dot(kd.T.astype(BF16), v_ref[rows, cs], preferred_element_type=F32)
            state = cdec * state + jnp.where(same_head, u, 0.0)
        s_scr[pr] = state

    @pl.when(c == pl.num_programs(1) - 1)
    def _():
        fin_ref[...] = s_scr[...]


def _ret_states(lgl, rk, rv, s0, reverse):
    b, s, _ = rk.shape
    nc = s // RET_CHUNK
    cb = _pick_tile(nc, 8)
    nblk = nc // cb
    npair = RET_HEADS // 2
    blk_idx = (lambda c: nblk - 1 - c) if reverse else (lambda c: c)
    tok = pl.BlockSpec((None, cb * RET_CHUNK, D_RET), lambda bi, c: (bi, blk_idx(c), 0))
    st = pl.BlockSpec((None, npair, LANES, LANES), lambda bi, c: (bi, 0, 0, 0))
    return pl.pallas_call(
        functools.partial(_ret_state_kernel, cb=cb, reverse=reverse),
        out_shape=(jax.ShapeDtypeStruct((b, nc, npair, LANES, LANES), F32),
                   jax.ShapeDtypeStruct((b, npair, LANES, LANES), F32)),
        grid=(b, nblk),
        in_specs=[pl.BlockSpec(lgl.shape, lambda bi, c: (0, 0, 0)), tok, tok, st],
        out_specs=(pl.BlockSpec((None, cb, npair, LANES, LANES),
                                lambda bi, c: (bi, blk_idx(c), 0, 0, 0)), st),
        scratch_shapes=[pltpu.VMEM((npair, LANES, LANES), F32)],
        compiler_params=_params("parallel", "arbitrary"),
        name="ret_state_bwd" if reverse else "ret_state_fwd",
    )(lgl, rk, rv, s0)


def _ret_out_kernel(lgf_ref, lgb_ref, q_ref, k_ref, v_ref, g_ref, rf_ref, rb_ref, gnw_ref, o_ref, *, cb):
    L = RET_CHUNK
    i0 = lax.broadcasted_iota(jnp.int32, (L, LANES), 0).astype(F32)
    i1 = lax.broadcasted_iota(jnp.int32, (L, LANES), 1).astype(F32)
    diff = i0 - i1
    lo = lax.broadcasted_iota(jnp.int32, (L, LANES), 1) < HEAD_DIM
    grp = _head_group_matrix()
    inv = 1.0 / HEAD_DIM
    tabs = []
    for pr in range(RET_HEADS // 2):
        lgf, lgb = lgf_ref[pr], lgb_ref[pr]
        dmat = [jnp.where(diff >= 0.0,
                          jnp.exp(jnp.maximum(diff, 0.0) * lgf[:, a:a + 1]),
                          jnp.exp(jnp.maximum(-diff, 0.0) * lgb[:, a:a + 1]))
                for a in (0, HEAD_DIM)]
        tabs.append((jnp.exp((i0 + 1.0) * lgf), jnp.exp((float(L) - i0) * lgb),
                     jnp.concatenate(dmat, axis=0), gnw_ref[:, pr * LANES:(pr + 1) * LANES]))

    def chunk(t, carry):
        rows = pl.ds(pl.multiple_of(t * L, L), L)
        for pr in range(RET_HEADS // 2):
            cs = slice(pr * LANES, (pr + 1) * LANES)
            xif, xib, dmat, gnw = tabs[pr]
            qp, kp, vp = q_ref[rows, cs], k_ref[rows, cs], v_ref[rows, cs]
            zero = jnp.zeros_like(kp)
            qz = jnp.concatenate([jnp.where(lo, qp, zero), jnp.where(lo, zero, qp)], axis=0)
            a = lax.dot_general(qz, kp, (((1,), (1,)), ((), ())), preferred_element_type=F32) * dmat
            qf = qp.astype(F32)
            lhs = jnp.concatenate([a[0:L].astype(BF16), a[L:].astype(BF16),
                                   (qf * xif).astype(BF16), (qf * xib).astype(BF16)], axis=1)
            rhs = jnp.concatenate([jnp.where(lo, vp, zero), jnp.where(lo, zero, vp),
                                   rf_ref[t, pr].astype(BF16), rb_ref[t, pr].astype(BF16)], axis=0)
            acc = jnp.dot(lhs, rhs, preferred_element_type=F32)
            dl = acc - _per_head_sum(acc, grp) * inv
            var = _per_head_sum(dl * dl, grp) * inv
            y = dl * lax.rsqrt(var + EPS) * gnw
            gt = g_ref[rows, cs]
            o_ref[rows, cs] = (gt * _sigmoid(gt) * y).astype(BF16)
        return carry

    lax.fori_loop(0, cb, chunk, 0, unroll=2)


def _ret_out(lgf, lgb, rq, rk, rv, rg, rf, rb, gnw):
    b, s, _ = rq.shape
    nc = s // RET_CHUNK
    cb = _pick_tile(nc, 8)
    npair = RET_HEADS // 2
    tok = pl.BlockSpec((None, cb * RET_CHUNK, D_RET), lambda bi, c: (bi, c, 0))
    st = pl.BlockSpec((None, cb, npair, LANES, LANES), lambda bi, c: (bi, c, 0, 0, 0))
    lg = pl.BlockSpec(lgf.shape, lambda bi, c: (0, 0, 0))
    return pl.pallas_call(
        functools.partial(_ret_out_kernel, cb=cb),
        out_shape=jax.ShapeDtypeStruct((b, s, D_RET), BF16),
        grid=(b, nc // cb),
        in_specs=[lg, lg, tok, tok, tok, tok, st, st, pl.BlockSpec((1, D_RET), lambda bi, c: (0, 0))],
        out_specs=tok,
        compiler_params=_params("parallel", "parallel"),
        name="ret_out",
    )(lgf, lgb, rq, rk, rv, rg, rf, rb, gnw)


def _conv_kernel(prev_ref, cur_ref, next_ref, w_ref, b_ref, lnw_ref, lnb_ref, o_ref, xpad, shifted, hbuf):
    i = pl.program_id(1)
    tm = cur_ref.shape[0]
    halo = CONV_HALO
    first = i == 0
    last = i == pl.num_programs(1) - 1
    xpad[0:halo, :] = jnp.where(first, 0.0, prev_ref[...])
    xpad[halo:halo + tm, :] = cur_ref[...]
    xpad[halo + tm:2 * halo + tm, :] = jnp.where(last, 0.0, next_ref[...])
    span = tm + 2 * halo - SUBLANES
    for r in range(SUBLANES):
        shifted[r] = xpad[r:r + span, :]
    base = halo - CONV_WIDTH // 2
    bias = b_ref[...]

    def sub(sb, carry):
        r0 = pl.multiple_of(sb * CONV_SUB, CONV_SUB)
        chains = [None] * CONV_CHAINS
        for w in range(CONV_WIDTH):
            off = base + w
            xs = shifted[off % SUBLANES, pl.ds(r0 + (off // SUBLANES) * SUBLANES, CONV_SUB), :]
            term = xs * w_ref[w:w + 1, :]
            c = w % CONV_CHAINS
            chains[c] = term if chains[c] is None else chains[c] + term
        hbuf[pl.ds(r0, CONV_SUB), :] = (chains[0] + chains[1]) + (chains[2] + chains[3]) + bias
        return carry

    lax.fori_loop(0, tm // CONV_SUB, sub, 0, unroll=2)
    h = hbuf[...]
    mu = jnp.mean(h, axis=-1, keepdims=True)
    dl = h - mu
    var = jnp.mean(dl * dl, axis=-1, keepdims=True)
    y = dl * lax.rsqrt(var + EPS) * lnw_ref[...] + lnb_ref[...]
    o_ref[...] = (y * _sigmoid(y)).astype(BF16)


def _conv(cv, w, bias, lnw, lnb):
    b, s, ch = cv.shape
    tm = _pick_tile(s, 512)
    hpb = tm // CONV_HALO
    nh = s // CONV_HALO
    row = pl.BlockSpec((1, ch), lambda bi, i: (0, 0))
    span = tm + 2 * CONV_HALO - SUBLANES
    return pl.pallas_call(
        _conv_kernel,
        out_shape=jax.ShapeDtypeStruct((b, s, ch), BF16),
        grid=(b, s // tm),
        in_specs=[pl.BlockSpec((None, CONV_HALO, ch), lambda bi, i: (bi, jnp.maximum(i * hpb - 1, 0), 0)),
                  pl.BlockSpec((None, tm, ch), lambda bi, i: (bi, i, 0)),
                  pl.BlockSpec((None, CONV_HALO, ch),
                               lambda bi, i: (bi, jnp.minimum((i + 1) * hpb, nh - 1), 0)),
                  pl.BlockSpec(w.shape, lambda bi, i: (0, 0)), row, row, row],
        out_specs=pl.BlockSpec((None, tm, ch), lambda bi, i: (bi, i, 0)),
        scratch_shapes=[pltpu.VMEM((tm + 2 * CONV_HALO, ch), F32),
                        pltpu.VMEM((SUBLANES, span, ch), F32),
                        pltpu.VMEM((tm, ch), F32)],
        compiler_params=_params("parallel", "parallel"),
        name="conv",
    )(cv, cv, cv, w, bias, lnw, lnb)


def _outproj_kernel(*refs, route):
    if route:
        (att_ref, ret_ref, cnv_ref, w_ref, x_ref, g1_ref, nw_ref, sc_ref, sh_ref, rcat_ref,
         xo_ref, h_ref, rt_ref) = refs
    else:
        att_ref, ret_ref, cnv_ref, w_ref, x_ref, g1_ref, nw_ref, sc_ref, sh_ref, xo_ref, h_ref = refs
    y = (jnp.dot(att_ref[...], w_ref[0:D_ATT, :], preferred_element_type=F32)
         + jnp.dot(ret_ref[...], w_ref[D_ATT:D_ATT + D_RET, :], preferred_element_type=F32)
         + jnp.dot(cnv_ref[...], w_ref[D_ATT + D_RET:, :], preferred_element_type=F32))
    xn = x_ref[...] + g1_ref[...] * y
    xo_ref[...] = xn
    h = _modulated_rms(xn, nw_ref[...], sc_ref[...], sh_ref[...])
    h_ref[...] = h.astype(h_ref.dtype)
    if route:
        hi = h.astype(BF16)
        lo = (h - hi.astype(F32)).astype(BF16)
        tm = h.shape[0]
        r = jnp.dot(jnp.concatenate([hi, lo], axis=0), rcat_ref[...], preferred_element_type=F32)
        logits = (r[0:tm, 0:LANES] + r[0:tm, LANES:]) + (r[tm:, 0:LANES] + r[tm:, LANES:])
        lane = lax.broadcasted_iota(jnp.int32, logits.shape, 1).astype(F32)
        logits = jnp.where(lane < N_EXPERTS, logits, NEG_INF)
        m1 = jnp.max(logits, axis=-1, keepdims=True)
        i1 = jnp.min(jnp.where(logits == m1, lane, float(LANES)), axis=-1, keepdims=True)
        rest = jnp.where(lane == i1, NEG_INF, logits)
        m2 = jnp.max(rest, axis=-1, keepdims=True)
        i2 = jnp.min(jnp.where(rest == m2, lane, float(LANES)), axis=-1, keepdims=True)
        e2 = jnp.exp(m2 - m1)
        w1 = 1.0 / (1.0 + e2)
        w2 = e2 / (1.0 + e2)
        rt_ref[...] = jnp.where(lane == 0.0, i1,
                                jnp.where(lane == 1.0, i2,
                                          jnp.where(lane == 2.0, w1, jnp.where(lane == 3.0, w2, 0.0))))


def _outproj(att, ret, cnv, w_bf, x, g1, nw, sc, sh, router=None, h_dtype=BF16):
    b, s, d = x.shape
    tm = _pick_tile(s, 512)
    tok = lambda n: pl.BlockSpec((None, tm, n), lambda bi, i: (bi, i, 0))
    per_b = pl.BlockSpec((None, 1, d), lambda bi, i: (bi, 0, 0))
    row = pl.BlockSpec((1, d), lambda bi, i: (0, 0))
    in_specs = [tok(D_ATT), tok(D_RET), tok(CONV_CH), pl.BlockSpec(w_bf.shape, lambda bi, i: (0, 0)),
                tok(d), per_b, row, per_b, per_b]
    args = [att, ret, cnv, w_bf, x, g1, nw, sc, sh]
    out_shape = [jax.ShapeDtypeStruct((b, s, d), F32), jax.ShapeDtypeStruct((b, s, d), h_dtype)]
    out_specs = [tok(d), tok(d)]
    if router is not None:
        rpad = jnp.zeros((d, LANES), F32).at[:, :N_EXPERTS].set(router)
        rhi = rpad.astype(BF16)
        rlo = (rpad - rhi.astype(F32)).astype(BF16)
        in_specs += [pl.BlockSpec((d, 2 * LANES), lambda bi, i: (0, 0))]
        args += [jnp.concatenate([rhi, rlo], axis=1)]
        out_shape.append(jax.ShapeDtypeStruct((b, s, LANES), F32))
        out_specs.append(tok(LANES))
    return pl.pallas_call(
        functools.partial(_outproj_kernel, route=router is not None),
        out_shape=tuple(out_shape),
        grid=(b, s // tm),
        in_specs=in_specs,
        out_specs=tuple(out_specs),
        compiler_params=_params("parallel", "parallel"),
        name="outproj_route" if router is not None else "outproj",
    )(*args)


def _ffn_kernel(h_ref, wg_ref, wu_ref, wd_ref, x_ref, g2_ref, o_ref, acc):
    j = pl.program_id(2)

    @pl.when(j == 0)
    def _():
        acc[...] = jnp.zeros_like(acc)

    h = h_ref[...]
    gate = jnp.dot(h, wg_ref[...], preferred_element_type=F32)
    up = jnp.dot(h, wu_ref[...], preferred_element_type=F32)
    act = (gate * _sigmoid(gate) * up).astype(BF16)
    acc[...] += jnp.dot(act, wd_ref[...], preferred_element_type=F32)

    @pl.when(j == pl.num_programs(2) - 1)
    def _():
        o_ref[...] = x_ref[...] + g2_ref[...] * acc[...]


def _ffn(h, wg, wu, wd, x, g2):
    b, s, d = x.shape
    f = wg.shape[1]
    tm = _pick_tile(s, 512)
    tf = f // 2 if (f // 2) % LANES == 0 else f
    tok = lambda: pl.BlockSpec((None, tm, d), lambda bi, i, j: (bi, i, 0))
    return pl.pallas_call(
        _ffn_kernel,
        out_shape=jax.ShapeDtypeStruct((b, s, d), F32),
        grid=(b, s // tm, f // tf),
        in_specs=[tok(),
                  pl.BlockSpec((d, tf), lambda bi, i, j: (0, j)),
                  pl.BlockSpec((d, tf), lambda bi, i, j: (0, j)),
                  pl.BlockSpec((tf, d), lambda bi, i, j: (j, 0)),
                  tok(),
                  pl.BlockSpec((None, 1, d), lambda bi, i, j: (bi, 0, 0))],
        out_specs=tok(),
        scratch_shapes=[pltpu.VMEM((tm, d), F32)],
        compiler_params=_params("parallel", "parallel", "arbitrary"),
        name="ffn_dense",
    )(h, wg, wu, wd, x, g2)


def _gather_kernel(idx_ref, src_ref, o_ref, sem):
    base = pl.program_id(0) * GATHER_TG

    def copy(r, src_row):
        return pltpu.make_async_copy(src_ref.at[pl.ds(src_row, 1)], o_ref.at[pl.ds(r, 1)], sem)

    def issue(r, carry):
        copy(r, idx_ref[base + r]).start()
        return carry

    def drain(r, carry):
        copy(r, 0).wait()
        return carry

    lax.fori_loop(0, GATHER_TG, issue, 0)
    lax.fori_loop(0, GATHER_TG, drain, 0)


def _gather_rows(idx, src):
    n = idx.shape[0]
    d = src.shape[1]
    return pl.pallas_call(
        _gather_kernel,
        out_shape=jax.ShapeDtypeStruct((n, d), src.dtype),
        grid_spec=pltpu.PrefetchScalarGridSpec(
            num_scalar_prefetch=1,
            grid=(n // GATHER_TG,),
            in_specs=[pl.BlockSpec(memory_space=pl.ANY)],
            out_specs=pl.BlockSpec((GATHER_TG, d), lambda i, idx: (i, 0)),
            scratch_shapes=[pltpu.SemaphoreType.DMA]),
        compiler_params=_params("arbitrary"),
        name="moe_gather",
    )(idx, src)


def _moe_kernel(te_ref, nu_ref, x_ref, wg_ref, wu_ref, wd_ref, o_ref, xb, acc):
    t = pl.program_id(0)
    j = pl.program_id(1)
    nj = pl.num_programs(1)
    used = t < nu_ref[0]

    @pl.when(used & (j == 0))
    def _():
        xb[...] = x_ref[...].astype(BF16)
        acc[...] = jnp.zeros_like(acc)

    @pl.when(used)
    def _():
        h = xb[...]
        gate = jnp.dot(h, wg_ref[...], preferred_element_type=F32)
        up = jnp.dot(h, wu_ref[...], preferred_element_type=F32)
        act = (gate * _sigmoid(gate) * up).astype(BF16)
        acc[...] += jnp.dot(act, wd_ref[...], preferred_element_type=F32)

    @pl.when(used & (j == nj - 1))
    def _():
        o_ref[...] = acc[...]

    @pl.when(jnp.logical_not(used) & (j == nj - 1))
    def _():
        o_ref[...] = jnp.zeros_like(o_ref)


def _moe_grouped(tile_expert, n_used, xs, wg, wu, wd):
    p, d = xs.shape
    f = wg.shape[2]
    tm = MOE_TM
    tf = f // 4
    nj = f // tf

    def jj(t, j, te, nu):
        return jnp.where(t < nu[0], j, nj - 1)

    return pl.pallas_call(
        _moe_kernel,
        out_shape=jax.ShapeDtypeStruct((p, d), F32),
        grid_spec=pltpu.PrefetchScalarGridSpec(
            num_scalar_prefetch=2,
            grid=(p // tm, nj),
            in_specs=[pl.BlockSpec((tm, d), lambda t, j, te, nu: (t, 0)),
                      pl.BlockSpec((None, d, tf), lambda t, j, te, nu: (te[t], 0, jj(t, j, te, nu))),
                      pl.BlockSpec((None, d, tf), lambda t, j, te, nu: (te[t], 0, jj(t, j, te, nu))),
                      pl.BlockSpec((None, tf, d), lambda t, j, te, nu: (te[t], jj(t, j, te, nu), 0))],
            out_specs=pl.BlockSpec((tm, d), lambda t, j, te, nu: (t, 0)),
            scratch_shapes=[pltpu.VMEM((tm, d), BF16), pltpu.VMEM((tm, d), F32)]),
        compiler_params=_params("arbitrary", "arbitrary"),
        name="moe_grouped",
    )(tile_expert, n_used, xs, wg, wu, wd)


def _combine_kernel(pos_ref, y_ref, x_ref, g2_ref, rt_ref, o_ref, b1, b2, sem):
    n_tok = pos_ref.shape[0] // 2
    base = (pl.program_id(0) * pl.num_programs(1) + pl.program_id(1)) * GATHER_TG

    def copy(r, src_row, buf):
        return pltpu.make_async_copy(y_ref.at[pl.ds(src_row, 1)], buf.at[pl.ds(r, 1)], sem)

    def issue(r, carry):
        copy(r, pos_ref[base + r], b1).start()
        copy(r, pos_ref[n_tok + base + r], b2).start()
        return carry

    def drain(r, carry):
        copy(r, 0, b1).wait()
        copy(r, 0, b2).wait()
        return carry

    lax.fori_loop(0, GATHER_TG, issue, 0)
    lax.fori_loop(0, GATHER_TG, drain, 0)
    rt = rt_ref[...]
    o_ref[...] = x_ref[...] + g2_ref[...] * (rt[:, 2:3] * b1[...] + rt[:, 3:4] * b2[...])


def _moe_combine(pos, y, x, g2, route):
    b, s, d = x.shape
    tg = GATHER_TG
    tok = lambda n: pl.BlockSpec((None, tg, n), lambda bi, i, pos: (bi, i, 0))
    return pl.pallas_call(
        _combine_kernel,
        out_shape=jax.ShapeDtypeStruct((b, s, d), F32),
        grid_spec=pltpu.PrefetchScalarGridSpec(
            num_scalar_prefetch=1,
            grid=(b, s // tg),
            in_specs=[pl.BlockSpec(memory_space=pl.ANY), tok(d),
                      pl.BlockSpec((None, 1, d), lambda bi, i, pos: (bi, 0, 0)), tok(LANES)],
            out_specs=tok(d),
            scratch_shapes=[pltpu.VMEM((tg, d), F32), pltpu.VMEM((tg, d), F32),
                            pltpu.SemaphoreType.DMA]),
        compiler_params=_params("arbitrary", "arbitrary"),
        name="moe_combine",
    )(pos, y, x, g2, route)


def _moe(h, route, x, g2, wg, wu, wd):
    b, s, d = x.shape
    n_tok = b * s
    tm = MOE_TM
    rt = route.reshape(n_tok, LANES)
    flat_e = jnp.concatenate([rt[:, 0], rt[:, 1]]).astype(jnp.int32)
    onehot = (flat_e[:, None] == jnp.arange(N_EXPERTS, dtype=jnp.int32)[None, :]).astype(jnp.int32)
    csum = jnp.cumsum(onehot, axis=0)
    rank = jnp.sum(csum * onehot, axis=1) - 1
    counts = csum[-1]
    tiles_e = (counts + tm - 1) // tm
    tiles_cum = jnp.cumsum(tiles_e)
    row_start = (tiles_cum - tiles_e) * tm
    dest = jnp.sum(onehot * row_start[None, :], axis=1) + rank
    n_tiles = 2 * n_tok // tm + N_EXPERTS
    tile_ids = jnp.arange(n_tiles, dtype=jnp.int32)
    tile_expert = jnp.sum((tile_ids[:, None] >= tiles_cum[None, :]).astype(jnp.int32), axis=1)
    last_e = jnp.max(jnp.where(tiles_e > 0, jnp.arange(N_EXPERTS, dtype=jnp.int32), 0))
    tile_expert = jnp.minimum(tile_expert, last_e).astype(jnp.int32)
    n_used = tiles_cum[-1:].astype(jnp.int32)
    token = jnp.tile(jnp.arange(n_tok, dtype=jnp.int32), 2)
    row_src = jnp.zeros((n_tiles * tm,), jnp.int32).at[dest].set(token)
    xs = _gather_rows(row_src, h.reshape(n_tok, d))
    y = _moe_grouped(tile_expert, n_used, xs, wg, wu, wd)
    return _moe_combine(dest.astype(jnp.int32), y, x, g2, route)


def _rope_tables(n):
    rows = n // GRID_W
    r = jnp.repeat(jnp.arange(rows), GRID_W).astype(F32)
    col = jnp.tile(jnp.arange(GRID_W), rows).astype(F32)
    freqs = ROPE_BASE ** (-jnp.arange(ROPE_FREQS, dtype=F32) / ROPE_FREQS)
    ang = jnp.stack([r[:, None] * freqs, col[:, None] * freqs], axis=1)
    ang = jnp.repeat(ang[:, :, None, :], 2, axis=2).reshape(n, HEAD_DIM)
    ang = jnp.tile(ang, (1, LANES // HEAD_DIM))
    cos, sin = jnp.cos(ang), jnp.sin(ang)
    first_half = (jnp.arange(LANES) % (2 * ROPE_FREQS)) < ROPE_FREQS
    return cos, jnp.where(first_half, -sin, 0.0), jnp.where(first_half, 0.0, sin)


def _lane_rows(lg):
    return jnp.repeat(lg.astype(F32), HEAD_DIM).reshape(RET_HEADS // 2, 1, LANES)


def kernel(x, c, ctx, c_ctx, ada_w, ada_b, norm1_w, norm2_w, w_in, w_out, q_norm_w, k_norm_w,
           attn_sink, ret_decay_f, ret_decay_b, ret_gn_w, conv_w, conv_b, conv_ln_w, conv_ln_b,
           ffn_w_gate, ffn_w_up, ffn_w_down, router_w, moe_w_gate, moe_w_up, moe_w_down):
    b, n, d = x.shape
    n_ctx = ctx.shape[1]
    depth = ada_w.shape[0]
    cond = jnp.zeros((SUBLANES, d), F32).at[0:b].set(c).at[b].set(c_ctx)
    mods = _adaln(cond, ada_w, ada_b).reshape(depth, SUBLANES, 6, d)
    cos, sa, sb = _rope_tables(n)
    ones_c = jnp.ones((n_ctx, LANES), F32)
    zeros_c = jnp.zeros((n_ctx, LANES), F32)
    zero_state = jnp.zeros((b, RET_HEADS // 2, LANES, LANES), F32)
    row = lambda v: v.reshape(1, -1)
    for l in range(depth):
        last = l == depth - 1
        m_lat = [mods[l, 0:b, k][:, None, :] for k in range(6)]
        m_ctx = [jnp.broadcast_to(mods[l, b, k][None, None, :], (b, 1, d)) for k in range(6)]
        w_in_bf = w_in[l].astype(BF16)
        w_out_bf = w_out[l].astype(BF16)
        qw = row(jnp.tile(q_norm_w[l], LANES // HEAD_DIM))
        kw = row(jnp.tile(k_norm_w[l], LANES // HEAD_DIM))
        lgf = _lane_rows(jax.nn.log_sigmoid(ret_decay_f[l].astype(F32)))
        lgb = _lane_rows(jax.nn.log_sigmoid(ret_decay_b[l].astype(F32)))
        sink_tab = jnp.repeat(attn_sink[l].astype(F32), ATT_BLOCK).reshape(ATT_Q_HEADS // 2, 2 * ATT_BLOCK)

        q, k, v, rk, rv, rq, rg, cv = _inproj(x, row(norm1_w[l]), m_lat[1], m_lat[0], w_in_bf,
                                               cos, sa, sb, qw, kw)
        qc, kc, vc, rkc, rvc, rqc, rgc, cvc = _inproj(ctx, row(norm1_w[l]), m_ctx[1], m_ctx[0], w_in_bf,
                                                       ones_c, zeros_c, zeros_c, qw, kw)
        rf_c, s_f = _ret_states(lgf, rkc, rvc, zero_state, reverse=False)
        rb_c, s_b = _ret_states(lgb, rkc, rvc, zero_state, reverse=True)
        rf, _ = _ret_states(lgf, rk, rv, s_f, reverse=False)
        rb, _ = _ret_states(lgb, rk, rv, s_b, reverse=True)

        att = _attention(q, k, v, kc, vc, sink_tab, window=True)
        ret = _ret_out(lgf, lgb, rq, rk, rv, rg, rf, rb, row(ret_gn_w[l]))
        cnv = _conv(cv, conv_w[l], row(conv_b[l]), row(conv_ln_w[l]), row(conv_ln_b[l]))

        if l % 2 == 0:
            i = l // 2
            wg, wu, wd = ffn_w_gate[i].astype(BF16), ffn_w_up[i].astype(BF16), ffn_w_down[i].astype(BF16)
            x_mid, h2 = _outproj(att, ret, cnv, w_out_bf, x, m_lat[2], row(norm2_w[l]), m_lat[4], m_lat[3])
            x_new = _ffn(h2, wg, wu, wd, x_mid, m_lat[5])
        else:
            i = l // 2
            wg, wu, wd = moe_w_gate[i].astype(BF16), moe_w_up[i].astype(BF16), moe_w_down[i].astype(BF16)
            x_mid, h2, route = _outproj(att, ret, cnv, w_out_bf, x, m_lat[2], row(norm2_w[l]),
                                        m_lat[4], m_lat[3], router=router_w[i], h_dtype=F32)
            x_new = _moe(h2, route, x_mid, m_lat[5], wg, wu, wd)

        if not last:
            att_c = _attention(qc, None, None, kc, vc, sink_tab, window=False)
            ret_c = _ret_out(lgf, lgb, rqc, rkc, rvc, rgc, rf_c, rb_c, row(ret_gn_w[l]))
            cnv_c = _conv(cvc, conv_w[l], row(conv_b[l]), row(conv_ln_w[l]), row(conv_ln_b[l]))
            if l % 2 == 0:
                c_mid, h2c = _outproj(att_c, ret_c, cnv_c, w_out_bf, ctx, m_ctx[2], row(norm2_w[l]),
                                      m_ctx[4], m_ctx[3])
                ctx = _ffn(h2c, wg, wu, wd, c_mid, m_ctx[5])
            else:
                c_mid, h2c, route_c = _outproj(att_c, ret_c, cnv_c, w_out_bf, ctx, m_ctx[2],
                                               row(norm2_w[l]), m_ctx[4], m_ctx[3],
                                               router=router_w[i], h_dtype=F32)
                ctx = _moe(h2c, route_c, c_mid, m_ctx[5], wg, wu, wd)
        x = x_new
    return x
```

```python
import functools

import jax
import jax.numpy as jnp
from jax import lax
from jax.experimental import pallas as pl
from jax.experimental.pallas import tpu as pltpu
from jax.experimental.pallas import tpu_sc as plsc

F32 = jnp.float32
BF16 = jnp.bfloat16

GRID_W = 64
HEAD_DIM = 64
ATT_Q_HEADS = 8
ATT_KV_HEADS = 2
ATT_WINDOW = 128
ATT_BLOCK = 128
RET_HEADS = 4
RET_CHUNK = 128
RET_K_SCALE = HEAD_DIM ** -0.5
ATT_SCALE = HEAD_DIM ** -0.5
CONV_CH = 256
CONV_WIDTH = 31
ROPE_BASE = 10000.0
ROPE_FREQS = HEAD_DIM // 4
D_ATT = ATT_Q_HEADS * HEAD_DIM
D_RET = RET_HEADS * HEAD_DIM
ATT_KV_W = ATT_KV_HEADS * HEAD_DIM
C_ATT_K = 0
C_ATT_V = C_ATT_K + ATT_KV_W
C_RET_K = C_ATT_V + ATT_KV_W
C_RET_V = C_RET_K + D_RET
C_ATT_Q = C_RET_V + D_RET
C_RET_Q = C_ATT_Q + D_ATT
C_RET_G = C_RET_Q + D_RET
C_CONV = C_RET_G + D_RET
N_EXPERTS = 8
EPS = 1e-6
NEG_INF = -1e30

LANES = 128
SUBLANES = 8
VMEM_LIMIT = 48 * 1024 * 1024
CONV_HALO = 16
CONV_SUB = 16
CONV_CHAINS = 4
MOE_TM = 512
SC_WINDOW = 32


def _params(*sem):
    return pltpu.CompilerParams(dimension_semantics=sem, vmem_limit_bytes=VMEM_LIMIT)


def _sigmoid(x):
    return 1.0 / (1.0 + jnp.exp(-x))


def _pick_tile(n, pref):
    t = min(n, pref)
    assert n % t == 0, (n, t)
    return t


def _adaln_kernel(c_ref, w_ref, b_ref, o_ref):
    c = c_ref[...]
    s = c * _sigmoid(c)
    o_ref[...] = jnp.dot(s, w_ref[...], preferred_element_type=F32,
                         precision=lax.Precision.HIGHEST) + b_ref[...]


def _adaln(cond, ada_w, ada_b):
    depth, d, n = ada_w.shape
    tn = _pick_tile(n, 1536)
    return pl.pallas_call(
        _adaln_kernel,
        out_shape=jax.ShapeDtypeStruct((depth, cond.shape[0], n), F32),
        grid=(depth, n // tn),
        in_specs=[pl.BlockSpec(cond.shape, lambda l, j: (0, 0)),
                  pl.BlockSpec((None, d, tn), lambda l, j: (l, 0, j)),
                  pl.BlockSpec((None, 1, tn), lambda l, j: (l, 0, j))],
        out_specs=pl.BlockSpec((None, cond.shape[0], tn), lambda l, j: (l, 0, j)),
        compiler_params=_params("parallel", "parallel"),
        name="adaln",
    )(cond, ada_w, ada_b.reshape(depth, 1, n))


def _modulated_rms(x, nw, sc, sh):
    ms = jnp.mean(x * x, axis=-1, keepdims=True)
    return (x * lax.rsqrt(ms + EPS) * nw) * (1.0 + sc) + sh


def _head_group_matrix():
    r = lax.broadcasted_iota(jnp.int32, (2 * LANES, 2 * LANES), 0) // HEAD_DIM
    c = lax.broadcasted_iota(jnp.int32, (2 * LANES, 2 * LANES), 1) // HEAD_DIM
    return jnp.where(r == c, 1.0, 0.0).astype(BF16)


def _per_head_sum(v, grp):
    hi = v.astype(BF16)
    lo = (v - hi.astype(F32)).astype(BF16)
    r = jnp.dot(jnp.concatenate([hi, lo], axis=1), grp, preferred_element_type=F32)
    return r[:, 0:LANES] + r[:, LANES:]


def _dup_halves(t):
    sw = pltpu.roll(t, HEAD_DIM, 1)
    lo = lax.broadcasted_iota(jnp.int32, t.shape, 1) < HEAD_DIM
    return jnp.where(lo, t, sw), jnp.where(lo, sw, t)


def _inproj_kernel(x_ref, nw_ref, sc_ref, sh_ref, w_ref, cos_ref, sa_ref, sb_ref, qw_ref, kw_ref,
                   q_ref, k_ref, v_ref, rk_ref, rv_ref, rq_ref, rg_ref, cv_ref):
    hb = _modulated_rms(x_ref[...], nw_ref[...], sc_ref[...], sh_ref[...]).astype(BF16)

    def proj(c0, n):
        return jnp.dot(hb, w_ref[:, c0:c0 + n], preferred_element_type=F32)

    grp = _head_group_matrix()
    cos, sa, sb = cos_ref[...], sa_ref[...], sb_ref[...]

    def norm_rope(p, wrow):
        y = p * lax.rsqrt(_per_head_sum(p * p, grp) * (1.0 / HEAD_DIM) + EPS) * wrow
        return (y * cos + pltpu.roll(y, LANES - ROPE_FREQS, 1) * sa
                + pltpu.roll(y, ROPE_FREQS, 1) * sb)

    kv = proj(C_ATT_K, 2 * ATT_KV_W)
    k0, k1 = _dup_halves(norm_rope(kv[:, 0:ATT_KV_W], kw_ref[...]))
    k_ref[:, 0:LANES] = k0.astype(BF16)
    k_ref[:, LANES:2 * LANES] = k1.astype(BF16)
    v_ref[...] = kv[:, ATT_KV_W:].T.astype(BF16)
    qw = qw_ref[...] * ATT_SCALE
    qall = proj(C_ATT_Q, D_ATT)
    for j in range(D_ATT // LANES):
        q_ref[:, j * LANES:(j + 1) * LANES] = norm_rope(qall[:, j * LANES:(j + 1) * LANES], qw).astype(BF16)
    rk_ref[...] = (proj(C_RET_K, D_RET) * RET_K_SCALE).astype(BF16)
    rv_ref[...] = proj(C_RET_V, D_RET).astype(BF16)
    rq_ref[...] = proj(C_RET_Q, D_RET).astype(BF16)
    rg_ref[...] = proj(C_RET_G, D_RET)
    cv_ref[...] = proj(C_CONV, CONV_CH) * _sigmoid(proj(C_CONV + CONV_CH, CONV_CH))


def _inproj(x, nw, sc, sh, w_bf, cos, sa, sb, qw, kw):
    b, s, d = x.shape
    tm = _pick_tile(s, 512)
    row = lambda n: pl.BlockSpec((1, n), lambda bi, i: (0, 0))
    per_b = pl.BlockSpec((None, 1, d), lambda bi, i: (bi, 0, 0))
    tab = pl.BlockSpec((tm, LANES), lambda bi, i: (i, 0))
    tok = lambda n: pl.BlockSpec((None, tm, n), lambda bi, i: (bi, i, 0))
    shp = lambda n, dt: jax.ShapeDtypeStruct((b, s, n), dt)
    return pl.pallas_call(
        _inproj_kernel,
        out_shape=(shp(D_ATT, BF16), shp(2 * LANES, BF16), jax.ShapeDtypeStruct((b, ATT_KV_W, s), BF16),
                   shp(D_RET, BF16), shp(D_RET, BF16), shp(D_RET, BF16), shp(D_RET, F32),
                   shp(CONV_CH, F32)),
        grid=(b, s // tm),
        in_specs=[tok(d), row(d), per_b, per_b,
                  pl.BlockSpec(w_bf.shape, lambda bi, i: (0, 0)),
                  tab, tab, tab, row(LANES), row(LANES)],
        out_specs=(tok(D_ATT), tok(2 * LANES), pl.BlockSpec((None, ATT_KV_W, tm), lambda bi, i: (bi, 0, i)),
                   tok(D_RET), tok(D_RET), tok(D_RET), tok(D_RET), tok(CONV_CH)),
        compiler_params=_params("parallel", "parallel"),
        name="inproj",
    )(x, nw, sc, sh, w_bf, cos, sa, sb, qw, kw)


def _attn_kernel(*refs, window):
    if window:
        q_ref, kp_ref, kc_ref, kn_ref, vp_ref, vc_ref, vn_ref, kx_ref, vx_ref, sink_ref, o_ref = refs
    else:
        q_ref, kx_ref, vx_ref, sink_ref, o_ref = refs
    blk = ATT_BLOCK
    n_ctx = kx_ref.shape[0]
    nk = 3 * blk + n_ctx if window else n_ctx
    if window:
        i = pl.program_id(1)
        key = lax.broadcasted_iota(jnp.int32, (blk, 2 * blk), 0)
        qry = lax.broadcasted_iota(jnp.int32, (blk, 2 * blk), 1) & (blk - 1)
        mask_prev = (key >= qry) & (i > 0)
        mask_next = (key <= qry) & (i < pl.num_programs(1) - 1)
    first_head = lax.broadcasted_iota(jnp.int32, (blk, LANES), 1) < HEAD_DIM
    ones = jnp.ones((2 * SUBLANES, nk), BF16)
    for g in range(ATT_KV_HEADS):
        gs = slice(g * LANES, (g + 1) * LANES)
        vs = slice(g * HEAD_DIM, (g + 1) * HEAD_DIM)
        if window:
            kcat = jnp.concatenate([kp_ref[:, gs], kc_ref[:, gs], kn_ref[:, gs], kx_ref[:, gs]], axis=0)
            vt = jnp.concatenate([vp_ref[vs, :], vc_ref[vs, :], vn_ref[vs, :], vx_ref[vs, :]], axis=1)
        else:
            kcat, vt = kx_ref[:, gs], vx_ref[vs, :]
        vaug = jnp.concatenate([vt, ones], axis=0)
        for pr in range(2):
            pair = 2 * g + pr
            cs = slice(pair * LANES, (pair + 1) * LANES)
            qp = q_ref[:, cs]
            zero = jnp.zeros_like(qp)
            w = jnp.concatenate([jnp.where(first_head, qp, zero), jnp.where(first_head, zero, qp)], axis=0)
            s = lax.dot_general(kcat, w, (((1,), (1,)), ((), ())), preferred_element_type=F32)
            if window:
                parts = [jnp.where(mask_prev, s[0:blk], NEG_INF), s[blk:2 * blk],
                         jnp.where(mask_next, s[2 * blk:3 * blk], NEG_INF), s[3 * blk:]]
            else:
                parts = [s]
            snk = sink_ref[pair:pair + 1, :]
            m = snk
            for part in parts:
                m = jnp.maximum(m, jnp.max(part, axis=0, keepdims=True))
            p = jnp.concatenate([jnp.exp(part - m).astype(BF16) for part in parts], axis=0)
            o = jnp.dot(vaug, p, preferred_element_type=F32)
            den = o[HEAD_DIM:HEAD_DIM + 1, :] + jnp.exp(snk - m)
            on = o[0:HEAD_DIM, :] * (1.0 / den)
            ot = jnp.concatenate([on[:, 0:blk], on[:, blk:2 * blk]], axis=0)
            o_ref[:, cs] = ot.T.astype(BF16)


def _attention(q, k, vt, kx, vxt, sink_tab, window):
    b, s, _ = q.shape
    blk = ATT_BLOCK
    nb = s // blk
    n_ctx = kx.shape[1]
    qspec = pl.BlockSpec((None, blk, D_ATT), lambda bi, i: (bi, i, 0))
    kctx = pl.BlockSpec((None, n_ctx, 2 * LANES), lambda bi, i: (bi, 0, 0))
    vctx = pl.BlockSpec((None, ATT_KV_W, n_ctx), lambda bi, i: (bi, 0, 0))
    snk = pl.BlockSpec(sink_tab.shape, lambda bi, i: (0, 0))
    if window:
        prev = lambda i: jnp.maximum(i - 1, 0)
        nxt = lambda i: jnp.minimum(i + 1, nb - 1)
        same = lambda i: i
        kspec = lambda f: pl.BlockSpec((None, blk, 2 * LANES), lambda bi, i: (bi, f(i), 0))
        vspec = lambda f: pl.BlockSpec((None, ATT_KV_W, blk), lambda bi, i: (bi, 0, f(i)))
        in_specs = [qspec, kspec(prev), kspec(same), kspec(nxt), vspec(prev), vspec(same), vspec(nxt),
                    kctx, vctx, snk]
        args = (q, k, k, k, vt, vt, vt, kx, vxt, sink_tab)
    else:
        in_specs = [qspec, kctx, vctx, snk]
        args = (q, kx, vxt, sink_tab)
    return pl.pallas_call(
        functools.partial(_attn_kernel, window=window),
        out_shape=jax.ShapeDtypeStruct((b, s, D_ATT), BF16),
        grid=(b, nb),
        in_specs=in_specs,
        out_specs=qspec,
        compiler_params=_params("parallel", "parallel"),
        name="attention_window" if window else "attention_ctx",
    )(*args)


def _ret_state_kernel(lgl_ref, k_ref, v_ref, s0_ref, r_ref, fin_ref, s_scr, *, cb, reverse):
    c = pl.program_id(1)
    L = RET_CHUNK

    @pl.when(c == 0)
    def _():
        s_scr[...] = s0_ref[...]

    jj = lax.broadcasted_iota(jnp.int32, (L, LANES), 0).astype(F32)
    expo = jj if reverse else (L - 1.0) - jj
    same_head = (lax.broadcasted_iota(jnp.int32, (LANES, LANES), 0) // HEAD_DIM
                 == lax.broadcasted_iota(jnp.int32, (LANES, LANES), 1) // HEAD_DIM)
    for pr in range(RET_HEADS // 2):
        cs = slice(pr * LANES, (pr + 1) * LANES)
        lgl = lgl_ref[pr]
        kdec = jnp.exp(expo * lgl)
        cdec = jnp.exp(float(L) * lgl)
        state = s_scr[pr]
        for t in range(cb):
            cc = cb - 1 - t if reverse else t
            rows = slice(cc * L, (cc + 1) * L)
            r_ref[cc, pr] = state
            kd = k_ref[rows, cs].astype(F32) * kdec
            u = jnp.dot(kd.T.astype(BF16), v_ref[rows, cs], preferred_element_type=F32)
            state = cdec * state + jnp.where(same_head, u, 0.0)
        s_scr[pr] = state

    @pl.when(c == pl.num_programs(1) - 1)
    def _():
        fin_ref[...] = s_scr[...]


def _ret_states(lgl, rk, rv, s0, reverse):
    b, s, _ = rk.shape
    nc = s // RET_CHUNK
    cb = _pick_tile(nc, 8)
    nblk = nc // cb
    npair = RET_HEADS // 2
    blk_idx = (lambda c: nblk - 1 - c) if reverse else (lambda c: c)
    tok = pl.BlockSpec((None, cb * RET_CHUNK, D_RET), lambda bi, c: (bi, blk_idx(c), 0))
    st = pl.BlockSpec((None, npair, LANES, LANES), lambda bi, c: (bi, 0, 0, 0))
    return pl.pallas_call(
        functools.partial(_ret_state_kernel, cb=cb, reverse=reverse),
        out_shape=(jax.ShapeDtypeStruct((b, nc, npair, LANES, LANES), F32),
                   jax.ShapeDtypeStruct((b, npair, LANES, LANES), F32)),
        grid=(b, nblk),
        in_specs=[pl.BlockSpec(lgl.shape, lambda bi, c: (0, 0, 0)), tok, tok, st],
        out_specs=(pl.BlockSpec((None, cb, npair, LANES, LANES),
                                lambda bi, c: (bi, blk_idx(c), 0, 0, 0)), st),
        scratch_shapes=[pltpu.VMEM((npair, LANES, LANES), F32)],
        compiler_params=_params("parallel", "arbitrary"),
        name="ret_state_bwd" if reverse else "ret_state_fwd",
    )(lgl, rk, rv, s0)


def _ret_out_kernel(lgf_ref, lgb_ref, q_ref, k_ref, v_ref, g_ref, rf_ref, rb_ref, gnw_ref, o_ref, *, cb):
    L = RET_CHUNK
    i0 = lax.broadcasted_iota(jnp.int32, (L, LANES), 0).astype(F32)
    i1 = lax.broadcasted_iota(jnp.int32, (L, LANES), 1).astype(F32)
    diff = i0 - i1
    lo = lax.broadcasted_iota(jnp.int32, (L, LANES), 1) < HEAD_DIM
    grp = _head_group_matrix()
    inv = 1.0 / HEAD_DIM
    tabs = []
    for pr in range(RET_HEADS // 2):
        lgf, lgb = lgf_ref[pr], lgb_ref[pr]
        dmat = [jnp.where(diff >= 0.0,
                          jnp.exp(jnp.maximum(diff, 0.0) * lgf[:, a:a + 1]),
                          jnp.exp(jnp.maximum(-diff, 0.0) * lgb[:, a:a + 1]))
                for a in (0, HEAD_DIM)]
        tabs.append((jnp.exp((i0 + 1.0) * lgf), jnp.exp((float(L) - i0) * lgb),
                     jnp.concatenate(dmat, axis=0), gnw_ref[:, pr * LANES:(pr + 1) * LANES]))

    def chunk(t, carry):
        rows = pl.ds(pl.multiple_of(t * L, L), L)
        for pr in range(RET_HEADS // 2):
            cs = slice(pr * LANES, (pr + 1) * LANES)
            xif, xib, dmat, gnw = tabs[pr]
            qp, kp, vp = q_ref[rows, cs], k_ref[rows, cs], v_ref[rows, cs]
            zero = jnp.zeros_like(kp)
            qz = jnp.concatenate([jnp.where(lo, qp, zero), jnp.where(lo, zero, qp)], axis=0)
            a = lax.dot_general(qz, kp, (((1,), (1,)), ((), ())), preferred_element_type=F32) * dmat
            qf = qp.astype(F32)
            lhs = jnp.concatenate([a[0:L].astype(BF16), a[L:].astype(BF16),
                                   (qf * xif).astype(BF16), (qf * xib).astype(BF16)], axis=1)
            rhs = jnp.concatenate([jnp.where(lo, vp, zero), jnp.where(lo, zero, vp),
                                   rf_ref[t, pr].astype(BF16), rb_ref[t, pr].astype(BF16)], axis=0)
            acc = jnp.dot(lhs, rhs, preferred_element_type=F32)
            dl = acc - _per_head_sum(acc, grp) * inv
            var = _per_head_sum(dl * dl, grp) * inv
            y = dl * lax.rsqrt(var + EPS) * gnw
            gt = g_ref[rows, cs]
            o_ref[rows, cs] = (gt * _sigmoid(gt) * y).astype(BF16)
        return carry

    lax.fori_loop(0, cb, chunk, 0, unroll=2)


def _ret_out(lgf, lgb, rq, rk, rv, rg, rf, rb, gnw):
    b, s, _ = rq.shape
    nc = s // RET_CHUNK
    cb = _pick_tile(nc, 8)
    npair = RET_HEADS // 2
    tok = pl.BlockSpec((None, cb * RET_CHUNK, D_RET), lambda bi, c: (bi, c, 0))
    st = pl.BlockSpec((None, cb, npair, LANES, LANES), lambda bi, c: (bi, c, 0, 0, 0))
    lg = pl.BlockSpec(lgf.shape, lambda bi, c: (0, 0, 0))
    return pl.pallas_call(
        functools.partial(_ret_out_kernel, cb=cb),
        out_shape=jax.ShapeDtypeStruct((b, s, D_RET), BF16),
        grid=(b, nc // cb),
        in_specs=[lg, lg, tok, tok, tok, tok, st, st, pl.BlockSpec((1, D_RET), lambda bi, c: (0, 0))],
        out_specs=tok,
        compiler_params=_params("parallel", "parallel"),
        name="ret_out",
    )(lgf, lgb, rq, rk, rv, rg, rf, rb, gnw)


def _conv_kernel(prev_ref, cur_ref, next_ref, w_ref, b_ref, lnw_ref, lnb_ref, o_ref, xpad, shifted, hbuf):
    i = pl.program_id(1)
    tm = cur_ref.shape[0]
    halo = CONV_HALO
    first = i == 0
    last = i == pl.num_programs(1) - 1
    xpad[0:halo, :] = jnp.where(first, 0.0, prev_ref[...])
    xpad[halo:halo + tm, :] = cur_ref[...]
    xpad[halo + tm:2 * halo + tm, :] = jnp.where(last, 0.0, next_ref[...])
    span = tm + 2 * halo - SUBLANES
    for r in range(SUBLANES):
        shifted[r] = xpad[r:r + span, :]
    base = halo - CONV_WIDTH // 2
    bias = b_ref[...]

    def sub(sb, carry):
        r0 = pl.multiple_of(sb * CONV_SUB, CONV_SUB)
        chains = [None] * CONV_CHAINS
        for w in range(CONV_WIDTH):
            off = base + w
            xs = shifted[off % SUBLANES, pl.ds(r0 + (off // SUBLANES) * SUBLANES, CONV_SUB), :]
            term = xs * w_ref[w:w + 1, :]
            c = w % CONV_CHAINS
            chains[c] = term if chains[c] is None else chains[c] + term
        hbuf[pl.ds(r0, CONV_SUB), :] = (chains[0] + chains[1]) + (chains[2] + chains[3]) + bias
        return carry

    lax.fori_loop(0, tm // CONV_SUB, sub, 0, unroll=2)
    h = hbuf[...]
    mu = jnp.mean(h, axis=-1, keepdims=True)
    dl = h - mu
    var = jnp.mean(dl * dl, axis=-1, keepdims=True)
    y = dl * lax.rsqrt(var + EPS) * lnw_ref[...] + lnb_ref[...]
    o_ref[...] = (y * _sigmoid(y)).astype(BF16)


def _conv(cv, w, bias, lnw, lnb):
    b, s, ch = cv.shape
    tm = _pick_tile(s, 512)
    hpb = tm // CONV_HALO
    nh = s // CONV_HALO
    row = pl.BlockSpec((1, ch), lambda bi, i: (0, 0))
    span = tm + 2 * CONV_HALO - SUBLANES
    return pl.pallas_call(
        _conv_kernel,
        out_shape=jax.ShapeDtypeStruct((b, s, ch), BF16),
        grid=(b, s // tm),
        in_specs=[pl.BlockSpec((None, CONV_HALO, ch), lambda bi, i: (bi, jnp.maximum(i * hpb - 1, 0), 0)),
                  pl.BlockSpec((None, tm, ch), lambda bi, i: (bi, i, 0)),
                  pl.BlockSpec((None, CONV_HALO, ch),
                               lambda bi, i: (bi, jnp.minimum((i + 1) * hpb, nh - 1), 0)),
                  pl.BlockSpec(w.shape, lambda bi, i: (0, 0)), row, row, row],
        out_specs=pl.BlockSpec((None, tm, ch), lambda bi, i: (bi, i, 0)),
        scratch_shapes=[pltpu.VMEM((tm + 2 * CONV_HALO, ch), F32),
                        pltpu.VMEM((SUBLANES, span, ch), F32),
                        pltpu.VMEM((tm, ch), F32)],
        compiler_params=_params("parallel", "parallel"),
        name="conv",
    )(cv, cv, cv, w, bias, lnw, lnb)


def _outproj_kernel(*refs, route):
    if route:
        (att_ref, ret_ref, cnv_ref, w_ref, x_ref, g1_ref, nw_ref, sc_ref, sh_ref, rcat_ref,
         xo_ref, h_ref, rt_ref) = refs
    else:
        att_ref, ret_ref, cnv_ref, w_ref, x_ref, g1_ref, nw_ref, sc_ref, sh_ref, xo_ref, h_ref = refs
    y = (jnp.dot(att_ref[...], w_ref[0:D_ATT, :], preferred_element_type=F32)
         + jnp.dot(ret_ref[...], w_ref[D_ATT:D_ATT + D_RET, :], preferred_element_type=F32)
         + jnp.dot(cnv_ref[...], w_ref[D_ATT + D_RET:, :], preferred_element_type=F32))
    xn = x_ref[...] + g1_ref[...] * y
    xo_ref[...] = xn
    h = _modulated_rms(xn, nw_ref[...], sc_ref[...], sh_ref[...])
    h_ref[...] = h.astype(h_ref.dtype)
    if route:
        hi = h.astype(BF16)
        lo = (h - hi.astype(F32)).astype(BF16)
        tm = h.shape[0]
        r = jnp.dot(jnp.concatenate([hi, lo], axis=0), rcat_ref[...], preferred_element_type=F32)
        logits = (r[0:tm, 0:LANES] + r[0:tm, LANES:]) + (r[tm:, 0:LANES] + r[tm:, LANES:])
        lane = lax.broadcasted_iota(jnp.int32, logits.shape, 1).astype(F32)
        logits = jnp.where(lane < N_EXPERTS, logits, NEG_INF)
        m1 = jnp.max(logits, axis=-1, keepdims=True)
        i1 = jnp.min(jnp.where(logits == m1, lane, float(LANES)), axis=-1, keepdims=True)
        rest = jnp.where(lane == i1, NEG_INF, logits)
        m2 = jnp.max(rest, axis=-1, keepdims=True)
        i2 = jnp.min(jnp.where(rest == m2, lane, float(LANES)), axis=-1, keepdims=True)
        e2 = jnp.exp(m2 - m1)
        w1 = 1.0 / (1.0 + e2)
        w2 = e2 / (1.0 + e2)
        rt_ref[...] = jnp.where(lane == 0.0, i1,
                                jnp.where(lane == 1.0, i2,
                                          jnp.where(lane == 2.0, w1, jnp.where(lane == 3.0, w2, 0.0))))


def _outproj(att, ret, cnv, w_bf, x, g1, nw, sc, sh, router=None, h_dtype=BF16):
    b, s, d = x.shape
    tm = _pick_tile(s, 512)
    tok = lambda n: pl.BlockSpec((None, tm, n), lambda bi, i: (bi, i, 0))
    per_b = pl.BlockSpec((None, 1, d), lambda bi, i: (bi, 0, 0))
    row = pl.BlockSpec((1, d), lambda bi, i: (0, 0))
    in_specs = [tok(D_ATT), tok(D_RET), tok(CONV_CH), pl.BlockSpec(w_bf.shape, lambda bi, i: (0, 0)),
                tok(d), per_b, row, per_b, per_b]
    args = [att, ret, cnv, w_bf, x, g1, nw, sc, sh]
    out_shape = [jax.ShapeDtypeStruct((b, s, d), F32), jax.ShapeDtypeStruct((b, s, d), h_dtype)]
    out_specs = [tok(d), tok(d)]
    if router is not None:
        rpad = jnp.zeros((d, LANES), F32).at[:, :N_EXPERTS].set(router)
        rhi = rpad.astype(BF16)
        rlo = (rpad - rhi.astype(F32)).astype(BF16)
        in_specs += [pl.BlockSpec((d, 2 * LANES), lambda bi, i: (0, 0))]
        args += [jnp.concatenate([rhi, rlo], axis=1)]
        out_shape.append(jax.ShapeDtypeStruct((b, s, LANES), F32))
        out_specs.append(tok(LANES))
    return pl.pallas_call(
        functools.partial(_outproj_kernel, route=router is not None),
        out_shape=tuple(out_shape),
        grid=(b, s // tm),
        in_specs=in_specs,
        out_specs=tuple(out_specs),
        compiler_params=_params("parallel", "parallel"),
        name="outproj_route" if router is not None else "outproj",
    )(*args)


def _ffn_kernel(h_ref, wg_ref, wu_ref, wd_ref, x_ref, g2_ref, o_ref, acc):
    j = pl.program_id(2)

    @pl.when(j == 0)
    def _():
        acc[...] = jnp.zeros_like(acc)

    h = h_ref[...]
    gate = jnp.dot(h, wg_ref[...], preferred_element_type=F32)
    up = jnp.dot(h, wu_ref[...], preferred_element_type=F32)
    act = (gate * _sigmoid(gate) * up).astype(BF16)
    acc[...] += jnp.dot(act, wd_ref[...], preferred_element_type=F32)

    @pl.when(j == pl.num_programs(2) - 1)
    def _():
        o_ref[...] = x_ref[...] + g2_ref[...] * acc[...]


def _ffn(h, wg, wu, wd, x, g2):
    b, s, d = x.shape
    f = wg.shape[1]
    tm = _pick_tile(s, 512)
    tf = f // 2 if (f // 2) % LANES == 0 else f
    tok = lambda: pl.BlockSpec((None, tm, d), lambda bi, i, j: (bi, i, 0))
    return pl.pallas_call(
        _ffn_kernel,
        out_shape=jax.ShapeDtypeStruct((b, s, d), F32),
        grid=(b, s // tm, f // tf),
        in_specs=[tok(),
                  pl.BlockSpec((d, tf), lambda bi, i, j: (0, j)),
                  pl.BlockSpec((d, tf), lambda bi, i, j: (0, j)),
                  pl.BlockSpec((tf, d), lambda bi, i, j: (j, 0)),
                  tok(),
                  pl.BlockSpec((None, 1, d), lambda bi, i, j: (bi, 0, 0))],
        out_specs=tok(),
        scratch_shapes=[pltpu.VMEM((tm, d), F32)],
        compiler_params=_params("parallel", "parallel", "arbitrary"),
        name="ffn_dense",
    )(h, wg, wu, wd, x, g2)


def _gather_rows(idx, src):
    n = idx.shape[0]
    d = src.shape[1]
    mesh = plsc.VectorSubcoreMesh(core_axis_name="core", subcore_axis_name="subcore")

    @pl.kernel(out_type=jax.ShapeDtypeStruct((n, d), src.dtype), mesh=mesh, name="moe_gather")
    def gather(src_hbm, idx_hbm, out_hbm):
        def body(idx_vmem, out_vmem):
            pltpu.sync_copy(src_hbm.at[idx_vmem.at[0, pl.ds(0, SC_WINDOW)]], out_vmem)

        pltpu.emit_pipeline(
            body,
            grid=(n // SC_WINDOW,),
            in_specs=[pl.BlockSpec((1, LANES), lambda i: (i, 0))],
            out_specs=[pl.BlockSpec((SC_WINDOW, d), lambda i: (i, 0))],
            core_axis_name=("core", "subcore"),
            dimension_semantics=(pltpu.PARALLEL,),
        )(idx_hbm, out_hbm)

    idx_rows = jnp.pad(idx.reshape(n // SC_WINDOW, SC_WINDOW), ((0, 0), (0, LANES - SC_WINDOW)))
    return gather(src, idx_rows)


def _moe_kernel(te_ref, nu_ref, x_ref, wg_ref, wu_ref, wd_ref, o_ref, xb, acc):
    t = pl.program_id(0)
    j = pl.program_id(1)
    nj = pl.num_programs(1)
    used = t < nu_ref[0]

    @pl.when(used & (j == 0))
    def _():
        xb[...] = x_ref[...].astype(BF16)
        acc[...] = jnp.zeros_like(acc)

    @pl.when(used)
    def _():
        h = xb[...]
        gate = jnp.dot(h, wg_ref[...], preferred_element_type=F32)
        up = jnp.dot(h, wu_ref[...], preferred_element_type=F32)
        act = (gate * _sigmoid(gate) * up).astype(BF16)
        acc[...] += jnp.dot(act, wd_ref[...], preferred_element_type=F32)

    @pl.when(used & (j == nj - 1))
    def _():
        o_ref[...] = acc[...]

    @pl.when(jnp.logical_not(used) & (j == nj - 1))
    def _():
        o_ref[...] = jnp.zeros_like(o_ref)


def _moe_grouped(tile_expert, n_used, xs, wg, wu, wd):
    p, d = xs.shape
    f = wg.shape[2]
    tm = MOE_TM
    tf = f // 4
    nj = f // tf

    def jj(t, j, te, nu):
        return jnp.where(t < nu[0], j, nj - 1)

    return pl.pallas_call(
        _moe_kernel,
        out_shape=jax.ShapeDtypeStruct((p, d), F32),
        grid_spec=pltpu.PrefetchScalarGridSpec(
            num_scalar_prefetch=2,
            grid=(p // tm, nj),
            in_specs=[pl.BlockSpec((tm, d), lambda t, j, te, nu: (t, 0)),
                      pl.BlockSpec((None, d, tf), lambda t, j, te, nu: (te[t], 0, jj(t, j, te, nu))),
                      pl.BlockSpec((None, d, tf), lambda t, j, te, nu: (te[t], 0, jj(t, j, te, nu))),
                      pl.BlockSpec((None, tf, d), lambda t, j, te, nu: (te[t], jj(t, j, te, nu), 0))],
            out_specs=pl.BlockSpec((tm, d), lambda t, j, te, nu: (t, 0)),
            scratch_shapes=[pltpu.VMEM((tm, d), BF16), pltpu.VMEM((tm, d), F32)]),
        compiler_params=_params("arbitrary", "arbitrary"),
        name="moe_grouped",
    )(tile_expert, n_used, xs, wg, wu, wd)


def _combine_kernel(y1_ref, y2_ref, x_ref, g2_ref, rt_ref, o_ref):
    rt = rt_ref[...]
    o_ref[...] = x_ref[...] + g2_ref[...] * (rt[:, 2:3] * y1_ref[...] + rt[:, 3:4] * y2_ref[...])


def _moe_combine(yg, x, g2, route):
    b, s, d = x.shape
    tm = _pick_tile(s, 512)
    tok = lambda n: pl.BlockSpec((None, tm, n), lambda bi, i: (bi, i, 0))
    return pl.pallas_call(
        _combine_kernel,
        out_shape=jax.ShapeDtypeStruct((b, s, d), F32),
        grid=(b, s // tm),
        in_specs=[tok(d), pl.BlockSpec((None, tm, d), lambda bi, i: (b + bi, i, 0)), tok(d),
                  pl.BlockSpec((None, 1, d), lambda bi, i: (bi, 0, 0)), tok(LANES)],
        out_specs=tok(d),
        compiler_params=_params("parallel", "parallel"),
        name="moe_combine",
    )(yg, yg, x, g2, route)


def _moe(h, route, x, g2, wg, wu, wd):
    b, s, d = x.shape
    n_tok = b * s
    tm = MOE_TM
    rt = route.reshape(n_tok, LANES)
    flat_e = jnp.concatenate([rt[:, 0], rt[:, 1]]).astype(jnp.int32)
    onehot = (flat_e[:, None] == jnp.arange(N_EXPERTS, dtype=jnp.int32)[None, :]).astype(jnp.int32)
    csum = jnp.cumsum(onehot, axis=0)
    rank = jnp.sum(csum * onehot, axis=1) - 1
    counts = csum[-1]
    tiles_e = (counts + tm - 1) // tm
    tiles_cum = jnp.cumsum(tiles_e)
    row_start = (tiles_cum - tiles_e) * tm
    dest = jnp.sum(onehot * row_start[None, :], axis=1) + rank
    n_tiles = 2 * n_tok // tm + N_EXPERTS
    tile_ids = jnp.arange(n_tiles, dtype=jnp.int32)
    tile_expert = jnp.sum((tile_ids[:, None] >= tiles_cum[None, :]).astype(jnp.int32), axis=1)
    last_e = jnp.max(jnp.where(tiles_e > 0, jnp.arange(N_EXPERTS, dtype=jnp.int32), 0))
    tile_expert = jnp.minimum(tile_expert, last_e).astype(jnp.int32)
    n_used = tiles_cum[-1:].astype(jnp.int32)
    token = jnp.tile(jnp.arange(n_tok, dtype=jnp.int32), 2)
    row_src = jnp.zeros((n_tiles * tm,), jnp.int32).at[dest].set(token)
    xs = _gather_rows(row_src, h.reshape(n_tok, d))
    y = _moe_grouped(tile_expert, n_used, xs, wg, wu, wd)
    yg = _gather_rows(dest.astype(jnp.int32), y).reshape(2 * b, s, d)
    return _moe_combine(yg, x, g2, route)


def _rope_tables(n):
    rows = n // GRID_W
    r = jnp.repeat(jnp.arange(rows), GRID_W).astype(F32)
    col = jnp.tile(jnp.arange(GRID_W), rows).astype(F32)
    freqs = ROPE_BASE ** (-jnp.arange(ROPE_FREQS, dtype=F32) / ROPE_FREQS)
    ang = jnp.stack([r[:, None] * freqs, col[:, None] * freqs], axis=1)
    ang = jnp.repeat(ang[:, :, None, :], 2, axis=2).reshape(n, HEAD_DIM)
    ang = jnp.tile(ang, (1, LANES // HEAD_DIM))
    cos, sin = jnp.cos(ang), jnp.sin(ang)
    first_half = (jnp.arange(LANES) % (2 * ROPE_FREQS)) < ROPE_FREQS
    return cos, jnp.where(first_half, -sin, 0.0), jnp.where(first_half, 0.0, sin)


def _lane_rows(lg):
    return jnp.repeat(lg.astype(F32), HEAD_DIM).reshape(RET_HEADS // 2, 1, LANES)


def kernel(x, c, ctx, c_ctx, ada_w, ada_b, norm1_w, norm2_w, w_in, w_out, q_norm_w, k_norm_w,
           attn_sink, ret_decay_f, ret_decay_b, ret_gn_w, conv_w, conv_b, conv_ln_w, conv_ln_b,
           ffn_w_gate, ffn_w_up, ffn_w_down, router_w, moe_w_gate, moe_w_up, moe_w_down):
    b, n, d = x.shape
    n_ctx = ctx.shape[1]
    depth = ada_w.shape[0]
    cond = jnp.zeros((SUBLANES, d), F32).at[0:b].set(c).at[b].set(c_ctx)
    mods = _adaln(cond, ada_w, ada_b).reshape(depth, SUBLANES, 6, d)
    cos, sa, sb = _rope_tables(n)
    ones_c = jnp.ones((n_ctx, LANES), F32)
    zeros_c = jnp.zeros((n_ctx, LANES), F32)
    zero_state = jnp.zeros((b, RET_HEADS // 2, LANES, LANES), F32)
    row = lambda v: v.reshape(1, -1)
    for l in range(depth):
        last = l == depth - 1
        m_lat = [mods[l, 0:b, k][:, None, :] for k in range(6)]
        m_ctx = [jnp.broadcast_to(mods[l, b, k][None, None, :], (b, 1, d)) for k in range(6)]
        w_in_bf = w_in[l].astype(BF16)
        w_out_bf = w_out[l].astype(BF16)
        qw = row(jnp.tile(q_norm_w[l], LANES // HEAD_DIM))
        kw = row(jnp.tile(k_norm_w[l], LANES // HEAD_DIM))
        lgf = _lane_rows(jax.nn.log_sigmoid(ret_decay_f[l].astype(F32)))
        lgb = _lane_rows(jax.nn.log_sigmoid(ret_decay_b[l].astype(F32)))
        sink_tab = jnp.repeat(attn_sink[l].astype(F32), ATT_BLOCK).reshape(ATT_Q_HEADS // 2, 2 * ATT_BLOCK)

        q, k, v, rk, rv, rq, rg, cv = _inproj(x, row(norm1_w[l]), m_lat[1], m_lat[0], w_in_bf,
                                               cos, sa, sb, qw, kw)
        qc, kc, vc, rkc, rvc, rqc, rgc, cvc = _inproj(ctx, row(norm1_w[l]), m_ctx[1], m_ctx[0], w_in_bf,
                                                       ones_c, zeros_c, zeros_c, qw, kw)
        rf_c, s_f = _ret_states(lgf, rkc, rvc, zero_state, reverse=False)
        rb_c, s_b = _ret_states(lgb, rkc, rvc, zero_state, reverse=True)
        rf, _ = _ret_states(lgf, rk, rv, s_f, reverse=False)
        rb, _ = _ret_states(lgb, rk, rv, s_b, reverse=True)

        att = _attention(q, k, v, kc, vc, sink_tab, window=True)
        ret = _ret_out(lgf, lgb, rq, rk, rv, rg, rf, rb, row(ret_gn_w[l]))
        cnv = _conv(cv, conv_w[l], row(conv_b[l]), row(conv_ln_w[l]), row(conv_ln_b[l]))

        if l % 2 == 0:
            i = l // 2
            wg, wu, wd = ffn_w_gate[i].astype(BF16), ffn_w_up[i].astype(BF16), ffn_w_down[i].astype(BF16)
            x_mid, h2 = _outproj(att, ret, cnv, w_out_bf, x, m_lat[2], row(norm2_w[l]), m_lat[4], m_lat[3])
            x_new = _ffn(h2, wg, wu, wd, x_mid, m_lat[5])
        else:
            i = l // 2
            wg, wu, wd = moe_w_gate[i].astype(BF16), moe_w_up[i].astype(BF16), moe_w_down[i].astype(BF16)
            x_mid, h2, route = _outproj(att, ret, cnv, w_out_bf, x, m_lat[2], row(norm2_w[l]),
                                        m_lat[4], m_lat[3], router=router_w[i], h_dtype=F32)
            x_new = _moe(h2, route, x_mid, m_lat[5], wg, wu, wd)

        if not last:
            att_c = _attention(qc, None, None, kc, vc, sink_tab, window=False)
            ret_c = _ret_out(lgf, lgb, rqc, rkc, rvc, rgc, rf_c, rb_c, row(ret_gn_w[l]))
            cnv_c = _conv(cvc, conv_w[l], row(conv_b[l]), row(conv_ln_w[l]), row(conv_ln_b[l]))
            if l % 2 == 0:
                c_mid, h2c = _outproj(att_c, ret_c, cnv_c, w_out_bf, ctx, m_ctx[2], row(norm2_w[l]),
                                      m_ctx[4], m_ctx[3])
                ctx = _ffn(h2c, wg, wu, wd, c_mid, m_ctx[5])
            else:
                c_mid, h2c, route_c = _outproj(att_c, ret_c, cnv_c, w_out_bf, ctx, m_ctx[2],
                                               row(norm2_w[l]), m_ctx[4], m_ctx[3],
                                               router=router_w[i], h_dtype=F32)
                ctx = _moe(h2c, route_c, c_mid, m_ctx[5], wg, wu, wd)
        x = x_new
    return x
```

```python
import functools

import jax
import jax.numpy as jnp
from jax import lax
from jax.experimental import pallas as pl
from jax.experimental.pallas import tpu as pltpu
from jax.experimental.pallas import tpu_sc as plsc

F32 = jnp.float32
BF16 = jnp.bfloat16

GRID_W = 64
HEAD_DIM = 64
ATT_Q_HEADS = 8
ATT_KV_HEADS = 2
ATT_WINDOW = 128
ATT_BLOCK = 128
RET_HEADS = 4
RET_CHUNK = 128
RET_K_SCALE = HEAD_DIM ** -0.5
ATT_SCALE = HEAD_DIM ** -0.5
CONV_CH = 256
CONV_WIDTH = 31
ROPE_BASE = 10000.0
ROPE_FREQS = HEAD_DIM // 4
D_ATT = ATT_Q_HEADS * HEAD_DIM
D_RET = RET_HEADS * HEAD_DIM
ATT_KV_W = ATT_KV_HEADS * HEAD_DIM
C_ATT_K = 0
C_ATT_V = C_ATT_K + ATT_KV_W
C_RET_K = C_ATT_V + ATT_KV_W
C_RET_V = C_RET_K + D_RET
C_ATT_Q = C_RET_V + D_RET
C_RET_Q = C_ATT_Q + D_ATT
C_RET_G = C_RET_Q + D_RET
C_CONV = C_RET_G + D_RET
N_EXPERTS = 8
EPS = 1e-6
NEG_INF = -1e30

LANES = 128
SUBLANES = 8
VMEM_LIMIT = 48 * 1024 * 1024
CONV_HALO = 16
CONV_SUB = 16
CONV_CHAINS = 4
MOE_TM = 512
SC_WINDOW = 32


def _params(*sem):
    return pltpu.CompilerParams(dimension_semantics=sem, vmem_limit_bytes=VMEM_LIMIT)


def _sigmoid(x):
    return 1.0 / (1.0 + jnp.exp(-x))


def _pick_tile(n, pref):
    t = min(n, pref)
    assert n % t == 0, (n, t)
    return t


def _adaln_kernel(c_ref, w_ref, b_ref, o_ref):
    c = c_ref[...]
    s = c * _sigmoid(c)
    o_ref[...] = jnp.dot(s, w_ref[...], preferred_element_type=F32,
                         precision=lax.Precision.HIGHEST) + b_ref[...]


def _adaln(cond, ada_w, ada_b):
    depth, d, n = ada_w.shape
    tn = _pick_tile(n, 1536)
    return pl.pallas_call(
        _adaln_kernel,
        out_shape=jax.ShapeDtypeStruct((depth, cond.shape[0], n), F32),
        grid=(depth, n // tn),
        in_specs=[pl.BlockSpec(cond.shape, lambda l, j: (0, 0)),
                  pl.BlockSpec((None, d, tn), lambda l, j: (l, 0, j)),
                  pl.BlockSpec((None, 1, tn), lambda l, j: (l, 0, j))],
        out_specs=pl.BlockSpec((None, cond.shape[0], tn), lambda l, j: (l, 0, j)),
        compiler_params=_params("parallel", "parallel"),
        name="adaln",
    )(cond, ada_w, ada_b.reshape(depth, 1, n))


def _modulated_rms(x, nw, sc, sh):
    ms = jnp.mean(x * x, axis=-1, keepdims=True)
    return (x * lax.rsqrt(ms + EPS) * nw) * (1.0 + sc) + sh


def _head_group_matrix():
    r = lax.broadcasted_iota(jnp.int32, (2 * LANES, 2 * LANES), 0) // HEAD_DIM
    c = lax.broadcasted_iota(jnp.int32, (2 * LANES, 2 * LANES), 1) // HEAD_DIM
    return jnp.where(r == c, 1.0, 0.0).astype(BF16)


def _per_head_sum(v, grp):
    hi = v.astype(BF16)
    lo = (v - hi.astype(F32)).astype(BF16)
    r = jnp.dot(jnp.concatenate([hi, lo], axis=1), grp, preferred_element_type=F32)
    return r[:, 0:LANES] + r[:, LANES:]


def _dup_halves(t):
    sw = pltpu.roll(t, HEAD_DIM, 1)
    lo = lax.broadcasted_iota(jnp.int32, t.shape, 1) < HEAD_DIM
    return jnp.where(lo, t, sw), jnp.where(lo, sw, t)


def _inproj_kernel(x_ref, nw_ref, sc_ref, sh_ref, w_ref, cos_ref, sa_ref, sb_ref, qw_ref, kw_ref,
                   q_ref, k_ref, v_ref, rk_ref, rv_ref, rq_ref, rg_ref, cv_ref):
    hb = _modulated_rms(x_ref[...], nw_ref[...], sc_ref[...], sh_ref[...]).astype(BF16)

    def proj(c0, n):
        return jnp.dot(hb, w_ref[:, c0:c0 + n], preferred_element_type=F32)

    grp = _head_group_matrix()
    cos, sa, sb = cos_ref[...], sa_ref[...], sb_ref[...]

    def norm_rope(p, wrow):
        y = p * lax.rsqrt(_per_head_sum(p * p, grp) * (1.0 / HEAD_DIM) + EPS) * wrow
        return (y * cos + pltpu.roll(y, LANES - ROPE_FREQS, 1) * sa
                + pltpu.roll(y, ROPE_FREQS, 1) * sb)

    kv = proj(C_ATT_K, 2 * ATT_KV_W)
    k0, k1 = _dup_halves(norm_rope(kv[:, 0:ATT_KV_W], kw_ref[...]))
    k_ref[:, 0:LANES] = k0.astype(BF16)
    k_ref[:, LANES:2 * LANES] = k1.astype(BF16)
    v_ref[...] = kv[:, ATT_KV_W:].T.astype(BF16)
    qw = qw_ref[...] * ATT_SCALE
    qall = proj(C_ATT_Q, D_ATT)
    for j in range(D_ATT // LANES):
        q_ref[:, j * LANES:(j + 1) * LANES] = norm_rope(qall[:, j * LANES:(j + 1) * LANES], qw).astype(BF16)
    rk_ref[...] = (proj(C_RET_K, D_RET) * RET_K_SCALE).astype(BF16)
    rv_ref[...] = proj(C_RET_V, D_RET).astype(BF16)
    rq_ref[...] = proj(C_RET_Q, D_RET).astype(BF16)
    rg_ref[...] = proj(C_RET_G, D_RET)
    cv_ref[...] = proj(C_CONV, CONV_CH) * _sigmoid(proj(C_CONV + CONV_CH, CONV_CH))


def _inproj(x, nw, sc, sh, w_bf, cos, sa, sb, qw, kw):
    b, s, d = x.shape
    tm = _pick_tile(s, 512)
    row = lambda n: pl.BlockSpec((1, n), lambda bi, i: (0, 0))
    per_b = pl.BlockSpec((None, 1, d), lambda bi, i: (bi, 0, 0))
    tab = pl.BlockSpec((tm, LANES), lambda bi, i: (i, 0))
    tok = lambda n: pl.BlockSpec((None, tm, n), lambda bi, i: (bi, i, 0))
    shp = lambda n, dt: jax.ShapeDtypeStruct((b, s, n), dt)
    return pl.pallas_call(
        _inproj_kernel,
        out_shape=(shp(D_ATT, BF16), shp(2 * LANES, BF16), jax.ShapeDtypeStruct((b, ATT_KV_W, s), BF16),
                   shp(D_RET, BF16), shp(D_RET, BF16), shp(D_RET, BF16), shp(D_RET, F32),
                   shp(CONV_CH, F32)),
        grid=(b, s // tm),
        in_specs=[tok(d), row(d), per_b, per_b,
                  pl.BlockSpec(w_bf.shape, lambda bi, i: (0, 0)),
                  tab, tab, tab, row(LANES), row(LANES)],
        out_specs=(tok(D_ATT), tok(2 * LANES), pl.BlockSpec((None, ATT_KV_W, tm), lambda bi, i: (bi, 0, i)),
                   tok(D_RET), tok(D_RET), tok(D_RET), tok(D_RET), tok(CONV_CH)),
        compiler_params=_params("parallel", "parallel"),
        name="inproj",
    )(x, nw, sc, sh, w_bf, cos, sa, sb, qw, kw)


def _attn_kernel(*refs, window):
    if window:
        q_ref, kp_ref, kc_ref, kn_ref, vp_ref, vc_ref, vn_ref, kx_ref, vx_ref, sink_ref, o_ref = refs
    else:
        q_ref, kx_ref, vx_ref, sink_ref, o_ref = refs
    blk = ATT_BLOCK
    n_ctx = kx_ref.shape[0]
    nk = 3 * blk + n_ctx if window else n_ctx
    if window:
        i = pl.program_id(1)
        key = lax.broadcasted_iota(jnp.int32, (blk, 2 * blk), 0)
        qry = lax.broadcasted_iota(jnp.int32, (blk, 2 * blk), 1) & (blk - 1)
        mask_prev = (key >= qry) & (i > 0)
        mask_next = (key <= qry) & (i < pl.num_programs(1) - 1)
    first_head = lax.broadcasted_iota(jnp.int32, (blk, LANES), 1) < HEAD_DIM
    ones = jnp.ones((2 * SUBLANES, nk), BF16)
    for g in range(ATT_KV_HEADS):
        gs = slice(g * LANES, (g + 1) * LANES)
        vs = slice(g * HEAD_DIM, (g + 1) * HEAD_DIM)
        if window:
            kcat = jnp.concatenate([kp_ref[:, gs], kc_ref[:, gs], kn_ref[:, gs], kx_ref[:, gs]], axis=0)
            vt = jnp.concatenate([vp_ref[vs, :], vc_ref[vs, :], vn_ref[vs, :], vx_ref[vs, :]], axis=1)
        else:
            kcat, vt = kx_ref[:, gs], vx_ref[vs, :]
        vaug = jnp.concatenate([vt, ones], axis=0)
        for pr in range(2):
            pair = 2 * g + pr
            cs = slice(pair * LANES, (pair + 1) * LANES)
            qp = q_ref[:, cs]
            zero = jnp.zeros_like(qp)
            w = jnp.concatenate([jnp.where(first_head, qp, zero), jnp.where(first_head, zero, qp)], axis=0)
            s = lax.dot_general(kcat, w, (((1,), (1,)), ((), ())), preferred_element_type=F32)
            if window:
                parts = [jnp.where(mask_prev, s[0:blk], NEG_INF), s[blk:2 * blk],
                         jnp.where(mask_next, s[2 * blk:3 * blk], NEG_INF), s[3 * blk:]]
            else:
                parts = [s]
            snk = sink_ref[pair:pair + 1, :]
            m = snk
            for part in parts:
                m = jnp.maximum(m, jnp.max(part, axis=0, keepdims=True))
            p = jnp.concatenate([jnp.exp(part - m).astype(BF16) for part in parts], axis=0)
            o = jnp.dot(vaug, p, preferred_element_type=F32)
            den = o[HEAD_DIM:HEAD_DIM + 1, :] + jnp.exp(snk - m)
            on = o[0:HEAD_DIM, :] * (1.0 / den)
            ot = jnp.concatenate([on[:, 0:blk], on[:, blk:2 * blk]], axis=0)
            o_ref[:, cs] = ot.T.astype(BF16)


def _attention(q, k, vt, kx, vxt, sink_tab, window):
    b, s, _ = q.shape
    blk = ATT_BLOCK
    nb = s // blk
    n_ctx = kx.shape[1]
    qspec = pl.BlockSpec((None, blk, D_ATT), lambda bi, i: (bi, i, 0))
    kctx = pl.BlockSpec((None, n_ctx, 2 * LANES), lambda bi, i: (bi, 0, 0))
    vctx = pl.BlockSpec((None, ATT_KV_W, n_ctx), lambda bi, i: (bi, 0, 0))
    snk = pl.BlockSpec(sink_tab.shape, lambda bi, i: (0, 0))
    if window:
        prev = lambda i: jnp.maximum(i - 1, 0)
        nxt = lambda i: jnp.minimum(i + 1, nb - 1)
        same = lambda i: i
        kspec = lambda f: pl.BlockSpec((None, blk, 2 * LANES), lambda bi, i: (bi, f(i), 0))
        vspec = lambda f: pl.BlockSpec((None, ATT_KV_W, blk), lambda bi, i: (bi, 0, f(i)))
        in_specs = [qspec, kspec(prev), kspec(same), kspec(nxt), vspec(prev), vspec(same), vspec(nxt),
                    kctx, vctx, snk]
        args = (q, k, k, k, vt, vt, vt, kx, vxt, sink_tab)
    else:
        in_specs = [qspec, kctx, vctx, snk]
        args = (q, kx, vxt, sink_tab)
    return pl.pallas_call(
        functools.partial(_attn_kernel, window=window),
        out_shape=jax.ShapeDtypeStruct((b, s, D_ATT), BF16),
        grid=(b, nb),
        in_specs=in_specs,
        out_specs=qspec,
        compiler_params=_params("parallel", "parallel"),
        name="attention_window" if window else "attention_ctx",
    )(*args)


def _ret_state_kernel(lgl_ref, k_ref, v_ref, s0_ref, r_ref, fin_ref, s_scr, *, cb, reverse):
    c = pl.program_id(1)
    L = RET_CHUNK

    @pl.when(c == 0)
    def _():
        s_scr[...] = s0_ref[...]

    jj = lax.broadcasted_iota(jnp.int32, (L, LANES), 0).astype(F32)
    expo = jj if reverse else (L - 1.0) - jj
    same_head = (lax.broadcasted_iota(jnp.int32, (LANES, LANES), 0) // HEAD_DIM
                 == lax.broadcasted_iota(jnp.int32, (LANES, LANES), 1) // HEAD_DIM)
    for pr in range(RET_HEADS // 2):
        cs = slice(pr * LANES, (pr + 1) * LANES)
        lgl = lgl_ref[pr]
        kdec = jnp.exp(expo * lgl)
        cdec = jnp.exp(float(L) * lgl)
        state = s_scr[pr]
        for t in range(cb):
            cc = cb - 1 - t if reverse else t
            rows = slice(cc * L, (cc + 1) * L)
            r_ref[cc, pr] = state
            kd = k_ref[rows, cs].astype(F32) * kdec
            u = jnp.dot(kd.T.astype(BF16), v_ref[rows, cs], preferred_element_type=F32)
            state = cdec * state + jnp.where(same_head, u, 0.0)
        s_scr[pr] = state

    @pl.when(c == pl.num_programs(1) - 1)
    def _():
        fin_ref[...] = s_scr[...]


def _ret_states(lgl, rk, rv, s0, reverse):
    b, s, _ = rk.shape
    nc = s // RET_CHUNK
    cb = _pick_tile(nc, 8)
    nblk = nc // cb
    npair = RET_HEADS // 2
    blk_idx = (lambda c: nblk - 1 - c) if reverse else (lambda c: c)
    tok = pl.BlockSpec((None, cb * RET_CHUNK, D_RET), lambda bi, c: (bi, blk_idx(c), 0))
    st = pl.BlockSpec((None, npair, LANES, LANES), lambda bi, c: (bi, 0, 0, 0))
    return pl.pallas_call(
        functools.partial(_ret_state_kernel, cb=cb, reverse=reverse),
        out_shape=(jax.ShapeDtypeStruct((b, nc, npair, LANES, LANES), F32),
                   jax.ShapeDtypeStruct((b, npair, LANES, LANES), F32)),
        grid=(b, nblk),
        in_specs=[pl.BlockSpec(lgl.shape, lambda bi, c: (0, 0, 0)), tok, tok, st],
        out_specs=(pl.BlockSpec((None, cb, npair, LANES, LANES),
                                lambda bi, c: (bi, blk_idx(c), 0, 0, 0)), st),
        scratch_shapes=[pltpu.VMEM((npair, LANES, LANES), F32)],
        compiler_params=_params("parallel", "arbitrary"),
        name="ret_state_bwd" if reverse else "ret_state_fwd",
    )(lgl, rk, rv, s0)


def _ret_out_kernel(lgf_ref, lgb_ref, q_ref, k_ref, v_ref, g_ref, rf_ref, rb_ref, gnw_ref, o_ref, *, cb):
    L = RET_CHUNK
    i0 = lax.broadcasted_iota(jnp.int32, (L, LANES), 0).astype(F32)
    i1 = lax.broadcasted_iota(jnp.int32, (L, LANES), 1).astype(F32)
    diff = i0 - i1
    lo = lax.broadcasted_iota(jnp.int32, (L, LANES), 1) < HEAD_DIM
    grp = _head_group_matrix()
    inv = 1.0 / HEAD_DIM
    tabs = []
    for pr in range(RET_HEADS // 2):
        lgf, lgb = lgf_ref[pr], lgb_ref[pr]
        dmat = [jnp.where(diff >= 0.0,
                          jnp.exp(jnp.maximum(diff, 0.0) * lgf[:, a:a + 1]),
                          jnp.exp(jnp.maximum(-diff, 0.0) * lgb[:, a:a + 1]))
                for a in (0, HEAD_DIM)]
        tabs.append((jnp.exp((i0 + 1.0) * lgf), jnp.exp((float(L) - i0) * lgb),
                     jnp.concatenate(dmat, axis=0), gnw_ref[:, pr * LANES:(pr + 1) * LANES]))

    def chunk(t, carry):
        rows = pl.ds(pl.multiple_of(t * L, L), L)
        for pr in range(RET_HEADS // 2):
            cs = slice(pr * LANES, (pr + 1) * LANES)
            xif, xib, dmat, gnw = tabs[pr]
            qp, kp, vp = q_ref[rows, cs], k_ref[rows, cs], v_ref[rows, cs]
            zero = jnp.zeros_like(kp)
            qz = jnp.concatenate([jnp.where(lo, qp, zero), jnp.where(lo, zero, qp)], axis=0)
            a = lax.dot_general(qz, kp, (((1,), (1,)), ((), ())), preferred_element_type=F32) * dmat
            qf = qp.astype(F32)
            lhs = jnp.concatenate([a[0:L].astype(BF16), a[L:].astype(BF16),
                                   (qf * xif).astype(BF16), (qf * xib).astype(BF16)], axis=1)
            rhs = jnp.concatenate([jnp.where(lo, vp, zero), jnp.where(lo, zero, vp),
                                   rf_ref[t, pr].astype(BF16), rb_ref[t, pr].astype(BF16)], axis=0)
            acc = jnp.dot(lhs, rhs, preferred_element_type=F32)
            dl = acc - _per_head_sum(acc, grp) * inv
            var = _per_head_sum(dl * dl, grp) * inv
            y = dl * lax.rsqrt(var + EPS) * gnw
            gt = g_ref[rows, cs]
            o_ref[rows, cs] = (gt * _sigmoid(gt) * y).astype(BF16)
        return carry

    lax.fori_loop(0, cb, chunk, 0, unroll=2)


def _ret_out(lgf, lgb, rq, rk, rv, rg, rf, rb, gnw):
    b, s, _ = rq.shape
    nc = s // RET_CHUNK
    cb = _pick_tile(nc, 8)
    npair = RET_HEADS // 2
    tok = pl.BlockSpec((None, cb * RET_CHUNK, D_RET), lambda bi, c: (bi, c, 0))
    st = pl.BlockSpec((None, cb, npair, LANES, LANES), lambda bi, c: (bi, c, 0, 0, 0))
    lg = pl.BlockSpec(lgf.shape, lambda bi, c: (0, 0, 0))
    return pl.pallas_call(
        functools.partial(_ret_out_kernel, cb=cb),
        out_shape=jax.ShapeDtypeStruct((b, s, D_RET), BF16),
        grid=(b, nc // cb),
        in_specs=[lg, lg, tok, tok, tok, tok, st, st, pl.BlockSpec((1, D_RET), lambda bi, c: (0, 0))],
        out_specs=tok,
        compiler_params=_params("parallel", "parallel"),
        name="ret_out",
    )(lgf, lgb, rq, rk, rv, rg, rf, rb, gnw)


def _conv_kernel(prev_ref, cur_ref, next_ref, w_ref, b_ref, lnw_ref, lnb_ref, o_ref, xpad, shifted, hbuf):
    i = pl.program_id(1)
    tm = cur_ref.shape[0]
    halo = CONV_HALO
    first = i == 0
    last = i == pl.num_programs(1) - 1
    xpad[0:halo, :] = jnp.where(first, 0.0, prev_ref[...])
    xpad[halo:halo + tm, :] = cur_ref[...]
    xpad[halo + tm:2 * halo + tm, :] = jnp.where(last, 0.0, next_ref[...])
    span = tm + 2 * halo - SUBLANES
    for r in range(SUBLANES):
        shifted[r] = xpad[r:r + span, :]
    base = halo - CONV_WIDTH // 2
    bias = b_ref[...]

    def sub(sb, carry):
        r0 = pl.multiple_of(sb * CONV_SUB, CONV_SUB)
        chains = [None] * CONV_CHAINS
        for w in range(CONV_WIDTH):
            off = base + w
            xs = shifted[off % SUBLANES, pl.ds(r0 + (off // SUBLANES) * SUBLANES, CONV_SUB), :]
            term = xs * w_ref[w:w + 1, :]
            c = w % CONV_CHAINS
            chains[c] = term if chains[c] is None else chains[c] + term
        hbuf[pl.ds(r0, CONV_SUB), :] = (chains[0] + chains[1]) + (chains[2] + chains[3]) + bias
        return carry

    lax.fori_loop(0, tm // CONV_SUB, sub, 0, unroll=2)
    h = hbuf[...]
    mu = jnp.mean(h, axis=-1, keepdims=True)
    dl = h - mu
    var = jnp.mean(dl * dl, axis=-1, keepdims=True)
    y = dl * lax.rsqrt(var + EPS) * lnw_ref[...] + lnb_ref[...]
    o_ref[...] = (y * _sigmoid(y)).astype(BF16)


def _conv(cv, w, bias, lnw, lnb):
    b, s, ch = cv.shape
    tm = _pick_tile(s, 512)
    hpb = tm // CONV_HALO
    nh = s // CONV_HALO
    row = pl.BlockSpec((1, ch), lambda bi, i: (0, 0))
    span = tm + 2 * CONV_HALO - SUBLANES
    return pl.pallas_call(
        _conv_kernel,
        out_shape=jax.ShapeDtypeStruct((b, s, ch), BF16),
        grid=(b, s // tm),
        in_specs=[pl.BlockSpec((None, CONV_HALO, ch), lambda bi, i: (bi, jnp.maximum(i * hpb - 1, 0), 0)),
                  pl.BlockSpec((None, tm, ch), lambda bi, i: (bi, i, 0)),
                  pl.BlockSpec((None, CONV_HALO, ch),
                               lambda bi, i: (bi, jnp.minimum((i + 1) * hpb, nh - 1), 0)),
                  pl.BlockSpec(w.shape, lambda bi, i: (0, 0)), row, row, row],
        out_specs=pl.BlockSpec((None, tm, ch), lambda bi, i: (bi, i, 0)),
        scratch_shapes=[pltpu.VMEM((tm + 2 * CONV_HALO, ch), F32),
                        pltpu.VMEM((SUBLANES, span, ch), F32),
                        pltpu.VMEM((tm, ch), F32)],
        compiler_params=_params("parallel", "parallel"),
        name="conv",
    )(cv, cv, cv, w, bias, lnw, lnb)


def _outproj_kernel(*refs, route):
    if route:
        (att_ref, ret_ref, cnv_ref, w_ref, x_ref, g1_ref, nw_ref, sc_ref, sh_ref, rcat_ref,
         xo_ref, h_ref, rt_ref) = refs
    else:
        att_ref, ret_ref, cnv_ref, w_ref, x_ref, g1_ref, nw_ref, sc_ref, sh_ref, xo_ref, h_ref = refs
    y = (jnp.dot(att_ref[...], w_ref[0:D_ATT, :], preferred_element_type=F32)
         + jnp.dot(ret_ref[...], w_ref[D_ATT:D_ATT + D_RET, :], preferred_element_type=F32)
         + jnp.dot(cnv_ref[...], w_ref[D_ATT + D_RET:, :], preferred_element_type=F32))
    xn = x_ref[...] + g1_ref[...] * y
    xo_ref[...] = xn
    h = _modulated_rms(xn, nw_ref[...], sc_ref[...], sh_ref[...])
    h_ref[...] = h.astype(h_ref.dtype)
    if route:
        hi = h.astype(BF16)
        lo = (h - hi.astype(F32)).astype(BF16)
        tm = h.shape[0]
        r = jnp.dot(jnp.concatenate([hi, lo], axis=0), rcat_ref[...], preferred_element_type=F32)
        logits = (r[0:tm, 0:LANES] + r[0:tm, LANES:]) + (r[tm:, 0:LANES] + r[tm:, LANES:])
        lane = lax.broadcasted_iota(jnp.int32, logits.shape, 1).astype(F32)
        logits = jnp.where(lane < N_EXPERTS, logits, NEG_INF)
        m1 = jnp.max(logits, axis=-1, keepdims=True)
        i1 = jnp.min(jnp.where(logits == m1, lane, float(LANES)), axis=-1, keepdims=True)
        rest = jnp.where(lane == i1, NEG_INF, logits)
        m2 = jnp.max(rest, axis=-1, keepdims=True)
        i2 = jnp.min(jnp.where(rest == m2, lane, float(LANES)), axis=-1, keepdims=True)
        e2 = jnp.exp(m2 - m1)
        w1 = 1.0 / (1.0 + e2)
        w2 = e2 / (1.0 + e2)
        rt_ref[...] = jnp.where(lane == 0.0, i1,
                                jnp.where(lane == 1.0, i2,
                                          jnp.where(lane == 2.0, w1, jnp.where(lane == 3.0, w2, 0.0))))


def _outproj(att, ret, cnv, w_bf, x, g1, nw, sc, sh, router=None, h_dtype=BF16):
    b, s, d = x.shape
    tm = _pick_tile(s, 512)
    tok = lambda n: pl.BlockSpec((None, tm, n), lambda bi, i: (bi, i, 0))
    per_b = pl.BlockSpec((None, 1, d), lambda bi, i: (bi, 0, 0))
    row = pl.BlockSpec((1, d), lambda bi, i: (0, 0))
    in_specs = [tok(D_ATT), tok(D_RET), tok(CONV_CH), pl.BlockSpec(w_bf.shape, lambda bi, i: (0, 0)),
                tok(d), per_b, row, per_b, per_b]
    args = [att, ret, cnv, w_bf, x, g1, nw, sc, sh]
    out_shape = [jax.ShapeDtypeStruct((b, s, d), F32), jax.ShapeDtypeStruct((b, s, d), h_dtype)]
    out_specs = [tok(d), tok(d)]
    if router is not None:
        rpad = jnp.zeros((d, LANES), F32).at[:, :N_EXPERTS].set(router)
        rhi = rpad.astype(BF16)
        rlo = (rpad - rhi.astype(F32)).astype(BF16)
        in_specs += [pl.BlockSpec((d, 2 * LANES), lambda bi, i: (0, 0))]
        args += [jnp.concatenate([rhi, rlo], axis=1)]
        out_shape.append(jax.ShapeDtypeStruct((b, s, LANES), F32))
        out_specs.append(tok(LANES))
    return pl.pallas_call(
        functools.partial(_outproj_kernel, route=router is not None),
        out_shape=tuple(out_shape),
        grid=(b, s // tm),
        in_specs=in_specs,
        out_specs=tuple(out_specs),
        compiler_params=_params("parallel", "parallel"),
        name="outproj_route" if router is not None else "outproj",
    )(*args)


def _ffn_kernel(h_ref, wg_ref, wu_ref, wd_ref, x_ref, g2_ref, o_ref, acc):
    j = pl.program_id(2)

    @pl.when(j == 0)
    def _():
        acc[...] = jnp.zeros_like(acc)

    h = h_ref[...]
    gate = jnp.dot(h, wg_ref[...], preferred_element_type=F32)
    up = jnp.dot(h, wu_ref[...], preferred_element_type=F32)
    act = (gate * _sigmoid(gate) * up).astype(BF16)
    acc[...] += jnp.dot(act, wd_ref[...], preferred_element_type=F32)

    @pl.when(j == pl.num_programs(2) - 1)
    def _():
        o_ref[...] = x_ref[...] + g2_ref[...] * acc[...]


def _ffn(h, wg, wu, wd, x, g2):
    b, s, d = x.shape
    f = wg.shape[1]
    tm = _pick_tile(s, 512)
    tf = f // 2 if (f // 2) % LANES == 0 else f
    tok = lambda: pl.BlockSpec((None, tm, d), lambda bi, i, j: (bi, i, 0))
    return pl.pallas_call(
        _ffn_kernel,
        out_shape=jax.ShapeDtypeStruct((b, s, d), F32),
        grid=(b, s // tm, f // tf),
        in_specs=[tok(),
                  pl.BlockSpec((d, tf), lambda bi, i, j: (0, j)),
                  pl.BlockSpec((d, tf), lambda bi, i, j: (0, j)),
                  pl.BlockSpec((tf, d), lambda bi, i, j: (j, 0)),
                  tok(),
                  pl.BlockSpec((None, 1, d), lambda bi, i, j: (bi, 0, 0))],
        out_specs=tok(),
        scratch_shapes=[pltpu.VMEM((tm, d), F32)],
        compiler_params=_params("parallel", "parallel", "arbitrary"),
        name="ffn_dense",
    )(h, wg, wu, wd, x, g2)


def _gather_rows(idx, src):
    n = idx.shape[0]
    d = src.shape[1]
    mesh = plsc.VectorSubcoreMesh(core_axis_name="core", subcore_axis_name="subcore")

    @pl.kernel(out_type=jax.ShapeDtypeStruct((n, d), src.dtype), mesh=mesh, name="moe_gather")
    def gather(src_hbm, idx_hbm, out_hbm):
        def body(idx_vmem, out_vmem):
            pltpu.sync_copy(src_hbm.at[idx_vmem.at[0, pl.ds(0, SC_WINDOW)]], out_vmem)

        pltpu.emit_pipeline(
            body,
            grid=(n // SC_WINDOW,),
            in_specs=[pl.BlockSpec((1, LANES), lambda i: (i, 0))],
            out_specs=[pl.BlockSpec((SC_WINDOW, d), lambda i: (i, 0))],
            core_axis_name=("core", "subcore"),
            dimension_semantics=(pltpu.PARALLEL,),
        )(idx_hbm, out_hbm)

    idx_rows = jnp.pad(idx.reshape(n // SC_WINDOW, SC_WINDOW), ((0, 0), (0, LANES - SC_WINDOW)))
    return gather(src, idx_rows)


def _moe_kernel(te_ref, nu_ref, x_ref, wg_ref, wu_ref, wd_ref, *rest, tile0):
    o_ref, xb, acc = rest[-3:]
    t = pl.program_id(0) + tile0
    j = pl.program_id(1)
    nj = pl.num_programs(1)
    used = t < nu_ref[0]

    @pl.when(used & (j == 0))
    def _():
        xb[...] = x_ref[...].astype(BF16)
        acc[...] = jnp.zeros_like(acc)

    @pl.when(used)
    def _():
        h = xb[...]
        gate = jnp.dot(h, wg_ref[...], preferred_element_type=F32)
        up = jnp.dot(h, wu_ref[...], preferred_element_type=F32)
        act = (gate * _sigmoid(gate) * up).astype(BF16)
        acc[...] += jnp.dot(act, wd_ref[...], preferred_element_type=F32)

    @pl.when(used & (j == nj - 1))
    def _():
        o_ref[...] = acc[...]

    @pl.when(jnp.logical_not(used) & (j == nj - 1))
    def _():
        o_ref[...] = jnp.zeros_like(o_ref)


def _moe_grouped(tile_expert, n_used, xs, wg, wu, wd, y_prev, tile0, n_rows):
    p, d = xs.shape
    f = wg.shape[2]
    tm = MOE_TM
    tf = f // 2
    nj = f // tf

    def jj(t, j, te, nu):
        return jnp.where(t + tile0 < nu[0], j, nj - 1)

    in_specs = [pl.BlockSpec((tm, d), lambda t, j, te, nu: (t, 0)),
                pl.BlockSpec((None, d, tf), lambda t, j, te, nu: (te[t + tile0], 0, jj(t, j, te, nu))),
                pl.BlockSpec((None, d, tf), lambda t, j, te, nu: (te[t + tile0], 0, jj(t, j, te, nu))),
                pl.BlockSpec((None, tf, d), lambda t, j, te, nu: (te[t + tile0], jj(t, j, te, nu), 0))]
    args = [tile_expert, n_used, xs, wg, wu, wd]
    aliases = {}
    if y_prev is not None:
        in_specs.append(pl.BlockSpec(memory_space=pl.ANY))
        args.append(y_prev)
        aliases = {len(args) - 1: 0}
    return pl.pallas_call(
        functools.partial(_moe_kernel, tile0=tile0),
        out_shape=jax.ShapeDtypeStruct((n_rows, d), F32),
        grid_spec=pltpu.PrefetchScalarGridSpec(
            num_scalar_prefetch=2,
            grid=(p // tm, nj),
            in_specs=in_specs,
            out_specs=pl.BlockSpec((tm, d), lambda t, j, te, nu: (t + tile0, 0)),
            scratch_shapes=[pltpu.VMEM((tm, d), BF16), pltpu.VMEM((tm, d), F32)]),
        input_output_aliases=aliases,
        compiler_params=_params("arbitrary", "arbitrary"),
        name="moe_grouped",
    )(*args)


def _combine_kernel(y1_ref, y2_ref, x_ref, g2_ref, rt_ref, o_ref):
    rt = rt_ref[...]
    o_ref[...] = x_ref[...] + g2_ref[...] * (rt[:, 2:3] * y1_ref[...] + rt[:, 3:4] * y2_ref[...])


def _moe_combine(yg, x, g2, route):
    b, s, d = x.shape
    tm = _pick_tile(s, 512)
    tok = lambda n: pl.BlockSpec((None, tm, n), lambda bi, i: (bi, i, 0))
    return pl.pallas_call(
        _combine_kernel,
        out_shape=jax.ShapeDtypeStruct((b, s, d), F32),
        grid=(b, s // tm),
        in_specs=[tok(d), pl.BlockSpec((None, tm, d), lambda bi, i: (b + bi, i, 0)), tok(d),
                  pl.BlockSpec((None, 1, d), lambda bi, i: (bi, 0, 0)), tok(LANES)],
        out_specs=tok(d),
        compiler_params=_params("parallel", "parallel"),
        name="moe_combine",
    )(yg, yg, x, g2, route)


def _moe(h, route, x, g2, wg, wu, wd):
    b, s, d = x.shape
    n_tok = b * s
    tm = MOE_TM
    rt = route.reshape(n_tok, LANES)
    flat_e = jnp.concatenate([rt[:, 0], rt[:, 1]]).astype(jnp.int32)
    onehot = (flat_e[:, None] == jnp.arange(N_EXPERTS, dtype=jnp.int32)[None, :]).astype(jnp.int32)
    csum = jnp.cumsum(onehot, axis=0)
    rank = jnp.sum(csum * onehot, axis=1) - 1
    counts = csum[-1]
    tiles_e = (counts + tm - 1) // tm
    tiles_cum = jnp.cumsum(tiles_e)
    row_start = (tiles_cum - tiles_e) * tm
    dest = jnp.sum(onehot * row_start[None, :], axis=1) + rank
    n_tiles = 2 * n_tok // tm + N_EXPERTS
    tile_ids = jnp.arange(n_tiles, dtype=jnp.int32)
    tile_expert = jnp.sum((tile_ids[:, None] >= tiles_cum[None, :]).astype(jnp.int32), axis=1)
    last_e = jnp.max(jnp.where(tiles_e > 0, jnp.arange(N_EXPERTS, dtype=jnp.int32), 0))
    tile_expert = jnp.minimum(tile_expert, last_e).astype(jnp.int32)
    n_used = tiles_cum[-1:].astype(jnp.int32)
    token = jnp.tile(jnp.arange(n_tok, dtype=jnp.int32), 2)
    row_src = jnp.zeros((n_tiles * tm,), jnp.int32).at[dest].set(token)
    n_chunks = max(c for c in (1, 2, 3, 4) if n_tiles % c == 0)
    tiles_c = n_tiles // n_chunks
    h_flat = h.reshape(n_tok, d)
    xs = [_gather_rows(row_src[c * tiles_c * tm:(c + 1) * tiles_c * tm], h_flat) for c in range(n_chunks)]
    y = None
    for c in range(n_chunks):
        y = _moe_grouped(tile_expert, n_used, xs[c], wg, wu, wd, y, c * tiles_c, n_tiles * tm)
    yg = _gather_rows(dest.astype(jnp.int32), y).reshape(2 * b, s, d)
    return _moe_combine(yg, x, g2, route)


def _rope_tables(n):
    rows = n // GRID_W
    r = jnp.repeat(jnp.arange(rows), GRID_W).astype(F32)
    col = jnp.tile(jnp.arange(GRID_W), rows).astype(F32)
    freqs = ROPE_BASE ** (-jnp.arange(ROPE_FREQS, dtype=F32) / ROPE_FREQS)
    ang = jnp.stack([r[:, None] * freqs, col[:, None] * freqs], axis=1)
    ang = jnp.repeat(ang[:, :, None, :], 2, axis=2).reshape(n, HEAD_DIM)
    ang = jnp.tile(ang, (1, LANES // HEAD_DIM))
    cos, sin = jnp.cos(ang), jnp.sin(ang)
    first_half = (jnp.arange(LANES) % (2 * ROPE_FREQS)) < ROPE_FREQS
    return cos, jnp.where(first_half, -sin, 0.0), jnp.where(first_half, 0.0, sin)


def _lane_rows(lg):
    return jnp.repeat(lg.astype(F32), HEAD_DIM).reshape(RET_HEADS // 2, 1, LANES)


def kernel(x, c, ctx, c_ctx, ada_w, ada_b, norm1_w, norm2_w, w_in, w_out, q_norm_w, k_norm_w,
           attn_sink, ret_decay_f, ret_decay_b, ret_gn_w, conv_w, conv_b, conv_ln_w, conv_ln_b,
           ffn_w_gate, ffn_w_up, ffn_w_down, router_w, moe_w_gate, moe_w_up, moe_w_down):
    b, n, d = x.shape
    n_ctx = ctx.shape[1]
    depth = ada_w.shape[0]
    cond = jnp.zeros((SUBLANES, d), F32).at[0:b].set(c).at[b].set(c_ctx)
    mods = _adaln(cond, ada_w, ada_b).reshape(depth, SUBLANES, 6, d)
    cos, sa, sb = _rope_tables(n)
    ones_c = jnp.ones((n_ctx, LANES), F32)
    zeros_c = jnp.zeros((n_ctx, LANES), F32)
    zero_state = jnp.zeros((b, RET_HEADS // 2, LANES, LANES), F32)
    row = lambda v: v.reshape(1, -1)
    for l in range(depth):
        last = l == depth - 1
        m_lat = [mods[l, 0:b, k][:, None, :] for k in range(6)]
        m_ctx = [jnp.broadcast_to(mods[l, b, k][None, None, :], (b, 1, d)) for k in range(6)]
        w_in_bf = w_in[l].astype(BF16)
        w_out_bf = w_out[l].astype(BF16)
        qw = row(jnp.tile(q_norm_w[l], LANES // HEAD_DIM))
        kw = row(jnp.tile(k_norm_w[l], LANES // HEAD_DIM))
        lgf = _lane_rows(jax.nn.log_sigmoid(ret_decay_f[l].astype(F32)))
        lgb = _lane_rows(jax.nn.log_sigmoid(ret_decay_b[l].astype(F32)))
        sink_tab = jnp.repeat(attn_sink[l].astype(F32), ATT_BLOCK).reshape(ATT_Q_HEADS // 2, 2 * ATT_BLOCK)

        q, k, v, rk, rv, rq, rg, cv = _inproj(x, row(norm1_w[l]), m_lat[1], m_lat[0], w_in_bf,
                                               cos, sa, sb, qw, kw)
        qc, kc, vc, rkc, rvc, rqc, rgc, cvc = _inproj(ctx, row(norm1_w[l]), m_ctx[1], m_ctx[0], w_in_bf,
                                                       ones_c, zeros_c, zeros_c, qw, kw)
        rf_c, s_f = _ret_states(lgf, rkc, rvc, zero_state, reverse=False)
        rb_c, s_b = _ret_states(lgb, rkc, rvc, zero_state, reverse=True)
        rf, _ = _ret_states(lgf, rk, rv, s_f, reverse=False)
        rb, _ = _ret_states(lgb, rk, rv, s_b, reverse=True)

        att = _attention(q, k, v, kc, vc, sink_tab, window=True)
        ret = _ret_out(lgf, lgb, rq, rk, rv, rg, rf, rb, row(ret_gn_w[l]))
        cnv = _conv(cv, conv_w[l], row(conv_b[l]), row(conv_ln_w[l]), row(conv_ln_b[l]))

        if l % 2 == 0:
            i = l // 2
            wg, wu, wd = ffn_w_gate[i].astype(BF16), ffn_w_up[i].astype(BF16), ffn_w_down[i].astype(BF16)
            x_mid, h2 = _outproj(att, ret, cnv, w_out_bf, x, m_lat[2], row(norm2_w[l]), m_lat[4], m_lat[3])
            x_new = _ffn(h2, wg, wu, wd, x_mid, m_lat[5])
        else:
            i = l // 2
            wg, wu, wd = moe_w_gate[i].astype(BF16), moe_w_up[i].astype(BF16), moe_w_down[i].astype(BF16)
            x_mid, h2, route = _outproj(att, ret, cnv, w_out_bf, x, m_lat[2], row(norm2_w[l]),
                                        m_lat[4], m_lat[3], router=router_w[i], h_dtype=F32)
            x_new = _moe(h2, route, x_mid, m_lat[5], wg, wu, wd)

        if not last:
            att_c = _attention(qc, None, None, kc, vc, sink_tab, window=False)
            ret_c = _ret_out(lgf, lgb, rqc, rkc, rvc, rgc, rf_c, rb_c, row(ret_gn_w[l]))
            cnv_c = _conv(cvc, conv_w[l], row(conv_b[l]), row(conv_ln_w[l]), row(conv_ln_b[l]))
            if l % 2 == 0:
                c_mid, h2c = _outproj(att_c, ret_c, cnv_c, w_out_bf, ctx, m_ctx[2], row(norm2_w[l]),
                                      m_ctx[4], m_ctx[3])
                ctx = _ffn(h2c, wg, wu, wd, c_mid, m_ctx[5])
            else:
                c_mid, h2c, route_c = _outproj(att_c, ret_c, cnv_c, w_out_bf, ctx, m_ctx[2],
                                               row(norm2_w[l]), m_ctx[4], m_ctx[3],
                                               router=router_w[i], h_dtype=F32)
                ctx = _moe(h2c, route_c, c_mid, m_ctx[5], wg, wu, wd)
        x = x_new
    return x
```

```python
import functools

import jax
import jax.numpy as jnp
from jax import lax
from jax.experimental import pallas as pl
from jax.experimental.pallas import tpu as pltpu
from jax.experimental.pallas import tpu_sc as plsc

F32 = jnp.float32
BF16 = jnp.bfloat16

GRID_W = 64
HEAD_DIM = 64
ATT_Q_HEADS = 8
ATT_KV_HEADS = 2
ATT_WINDOW = 128
ATT_BLOCK = 128
RET_HEADS = 4
RET_CHUNK = 128
RET_K_SCALE = HEAD_DIM ** -0.5
ATT_SCALE = HEAD_DIM ** -0.5
CONV_CH = 256
CONV_WIDTH = 31
ROPE_BASE = 10000.0
ROPE_FREQS = HEAD_DIM // 4
D_ATT = ATT_Q_HEADS * HEAD_DIM
D_RET = RET_HEADS * HEAD_DIM
ATT_KV_W = ATT_KV_HEADS * HEAD_DIM
C_ATT_K = 0
C_ATT_V = C_ATT_K + ATT_KV_W
C_RET_K = C_ATT_V + ATT_KV_W
C_RET_V = C_RET_K + D_RET
C_ATT_Q = C_RET_V + D_RET
C_RET_Q = C_ATT_Q + D_ATT
C_RET_G = C_RET_Q + D_RET
C_CONV = C_RET_G + D_RET
N_EXPERTS = 8
EPS = 1e-6
NEG_INF = -1e30

LANES = 128
SUBLANES = 8
VMEM_LIMIT = 48 * 1024 * 1024
CONV_HALO = 16
CONV_SUB = 16
CONV_CHAINS = 4
MOE_TM = 512
MOE_SUB = 512
FFN_SUB = 512
MOE_TF = 1792
SC_LANES = 16
SC_WINDOW = 32


def _params(*sem):
    return pltpu.CompilerParams(dimension_semantics=sem, vmem_limit_bytes=VMEM_LIMIT)


def _sigmoid(x):
    return 1.0 / (1.0 + jnp.exp(-x))


def _pick_tile(n, pref):
    t = min(n, pref)
    assert n % t == 0, (n, t)
    return t


def _adaln_kernel(c_ref, w_ref, b_ref, o_ref):
    c = c_ref[...]
    s = c * _sigmoid(c)
    o_ref[...] = jnp.dot(s, w_ref[...], preferred_element_type=F32,
                         precision=lax.Precision.HIGHEST) + b_ref[...]


def _adaln(cond, ada_w, ada_b):
    depth, d, n = ada_w.shape
    tn = _pick_tile(n, 1536)
    return pl.pallas_call(
        _adaln_kernel,
        out_shape=jax.ShapeDtypeStruct((depth, cond.shape[0], n), F32),
        grid=(depth, n // tn),
        in_specs=[pl.BlockSpec(cond.shape, lambda l, j: (0, 0)),
                  pl.BlockSpec((None, d, tn), lambda l, j: (l, 0, j)),
                  pl.BlockSpec((None, 1, tn), lambda l, j: (l, 0, j))],
        out_specs=pl.BlockSpec((None, cond.shape[0], tn), lambda l, j: (l, 0, j)),
        compiler_params=_params("parallel", "parallel"),
        name="adaln",
    )(cond, ada_w, ada_b.reshape(depth, 1, n))


def _modulated_rms(x, nw, sc, sh):
    ms = jnp.mean(x * x, axis=-1, keepdims=True)
    return (x * lax.rsqrt(ms + EPS) * nw) * (1.0 + sc) + sh


def _head_group_matrix():
    r = lax.broadcasted_iota(jnp.int32, (2 * LANES, 2 * LANES), 0) // HEAD_DIM
    c = lax.broadcasted_iota(jnp.int32, (2 * LANES, 2 * LANES), 1) // HEAD_DIM
    return jnp.where(r == c, 1.0, 0.0).astype(BF16)


def _per_head_sum(v, grp):
    hi = v.astype(BF16)
    lo = (v - hi.astype(F32)).astype(BF16)
    r = jnp.dot(jnp.concatenate([hi, lo], axis=1), grp, preferred_element_type=F32)
    return r[:, 0:LANES] + r[:, LANES:]


def _dup_halves(t):
    sw = pltpu.roll(t, HEAD_DIM, 1)
    lo = lax.broadcasted_iota(jnp.int32, t.shape, 1) < HEAD_DIM
    return jnp.where(lo, t, sw), jnp.where(lo, sw, t)


def _inproj_kernel(x_ref, nw_ref, sc_ref, sh_ref, w_ref, cos_ref, sa_ref, sb_ref, qw_ref, kw_ref,
                   q_ref, k_ref, v_ref, rk_ref, rv_ref, rq_ref, rg_ref, cv_ref):
    hb = _modulated_rms(x_ref[...], nw_ref[...], sc_ref[...], sh_ref[...]).astype(BF16)

    def proj(c0, n):
        return jnp.dot(hb, w_ref[:, c0:c0 + n], preferred_element_type=F32)

    grp = _head_group_matrix()
    cos, sa, sb = cos_ref[...], sa_ref[...], sb_ref[...]

    def norm_rope(p, wrow):
        y = p * lax.rsqrt(_per_head_sum(p * p, grp) * (1.0 / HEAD_DIM) + EPS) * wrow
        return (y * cos + pltpu.roll(y, LANES - ROPE_FREQS, 1) * sa
                + pltpu.roll(y, ROPE_FREQS, 1) * sb)

    kv = proj(C_ATT_K, 2 * ATT_KV_W)
    k0, k1 = _dup_halves(norm_rope(kv[:, 0:ATT_KV_W], kw_ref[...]))
    k_ref[:, 0:LANES] = k0.astype(BF16)
    k_ref[:, LANES:2 * LANES] = k1.astype(BF16)
    v_ref[...] = kv[:, ATT_KV_W:].T.astype(BF16)
    qw = qw_ref[...] * ATT_SCALE
    qall = proj(C_ATT_Q, D_ATT)
    for j in range(D_ATT // LANES):
        q_ref[:, j * LANES:(j + 1) * LANES] = norm_rope(qall[:, j * LANES:(j + 1) * LANES], qw).astype(BF16)
    rk_ref[...] = (proj(C_RET_K, D_RET) * RET_K_SCALE).astype(BF16)
    rv_ref[...] = proj(C_RET_V, D_RET).astype(BF16)
    rq_ref[...] = proj(C_RET_Q, D_RET).astype(BF16)
    rg_ref[...] = proj(C_RET_G, D_RET)
    cv_ref[...] = proj(C_CONV, CONV_CH) * _sigmoid(proj(C_CONV + CONV_CH, CONV_CH))


def _inproj(x, nw, sc, sh, w_bf, cos, sa, sb, qw, kw):
    b, s, d = x.shape
    tm = _pick_tile(s, 512)
    row = lambda n: pl.BlockSpec((1, n), lambda bi, i: (0, 0))
    per_b = pl.BlockSpec((None, 1, d), lambda bi, i: (bi, 0, 0))
    tab = pl.BlockSpec((tm, LANES), lambda bi, i: (i, 0))
    tok = lambda n: pl.BlockSpec((None, tm, n), lambda bi, i: (bi, i, 0))
    shp = lambda n, dt: jax.ShapeDtypeStruct((b, s, n), dt)
    return pl.pallas_call(
        _inproj_kernel,
        out_shape=(shp(D_ATT, BF16), shp(2 * LANES, BF16), jax.ShapeDtypeStruct((b, ATT_KV_W, s), BF16),
                   shp(D_RET, BF16), shp(D_RET, BF16), shp(D_RET, BF16), shp(D_RET, F32),
                   shp(CONV_CH, F32)),
        grid=(b, s // tm),
        in_specs=[tok(d), row(d), per_b, per_b,
                  pl.BlockSpec(w_bf.shape, lambda bi, i: (0, 0)),
                  tab, tab, tab, row(LANES), row(LANES)],
        out_specs=(tok(D_ATT), tok(2 * LANES), pl.BlockSpec((None, ATT_KV_W, tm), lambda bi, i: (bi, 0, i)),
                   tok(D_RET), tok(D_RET), tok(D_RET), tok(D_RET), tok(CONV_CH)),
        compiler_params=_params("parallel", "parallel"),
        name="inproj",
    )(x, nw, sc, sh, w_bf, cos, sa, sb, qw, kw)


def _attn_kernel(*refs, window):
    if window:
        q_ref, kp_ref, kc_ref, kn_ref, vp_ref, vc_ref, vn_ref, kx_ref, vx_ref, sink_ref, o_ref = refs
    else:
        q_ref, kx_ref, vx_ref, sink_ref, o_ref = refs
    blk = ATT_BLOCK
    n_ctx = kx_ref.shape[0]
    nk = 3 * blk + n_ctx if window else n_ctx
    if window:
        i = pl.program_id(1)
        key = lax.broadcasted_iota(jnp.int32, (blk, 2 * blk), 0)
        qry = lax.broadcasted_iota(jnp.int32, (blk, 2 * blk), 1) & (blk - 1)
        mask_prev = (key >= qry) & (i > 0)
        mask_next = (key <= qry) & (i < pl.num_programs(1) - 1)
    first_head = lax.broadcasted_iota(jnp.int32, (blk, LANES), 1) < HEAD_DIM
    ones = jnp.ones((2 * SUBLANES, nk), BF16)
    for g in range(ATT_KV_HEADS):
        gs = slice(g * LANES, (g + 1) * LANES)
        vs = slice(g * HEAD_DIM, (g + 1) * HEAD_DIM)
        if window:
            kcat = jnp.concatenate([kp_ref[:, gs], kc_ref[:, gs], kn_ref[:, gs], kx_ref[:, gs]], axis=0)
            vt = jnp.concatenate([vp_ref[vs, :], vc_ref[vs, :], vn_ref[vs, :], vx_ref[vs, :]], axis=1)
        else:
            kcat, vt = kx_ref[:, gs], vx_ref[vs, :]
        vaug = jnp.concatenate([vt, ones], axis=0)
        for pr in range(2):
            pair = 2 * g + pr
            cs = slice(pair * LANES, (pair + 1) * LANES)
            qp = q_ref[:, cs]
            zero = jnp.zeros_like(qp)
            w = jnp.concatenate([jnp.where(first_head, qp, zero), jnp.where(first_head, zero, qp)], axis=0)
            s = lax.dot_general(kcat, w, (((1,), (1,)), ((), ())), preferred_element_type=F32)
            if window:
                parts = [jnp.where(mask_prev, s[0:blk], NEG_INF), s[blk:2 * blk],
                         jnp.where(mask_next, s[2 * blk:3 * blk], NEG_INF), s[3 * blk:]]
            else:
                parts = [s]
            snk = sink_ref[pair:pair + 1, :]
            m = snk
            for part in parts:
                m = jnp.maximum(m, jnp.max(part, axis=0, keepdims=True))
            p = jnp.concatenate([jnp.exp(part - m).astype(BF16) for part in parts], axis=0)
            o = jnp.dot(vaug, p, preferred_element_type=F32)
            den = o[HEAD_DIM:HEAD_DIM + 1, :] + jnp.exp(snk - m)
            on = o[0:HEAD_DIM, :] * (1.0 / den)
            ot = jnp.concatenate([on[:, 0:blk], on[:, blk:2 * blk]], axis=0)
            o_ref[:, cs] = ot.T.astype(BF16)


def _attention(q, k, vt, kx, vxt, sink_tab, window):
    b, s, _ = q.shape
    blk = ATT_BLOCK
    nb = s // blk
    n_ctx = kx.shape[1]
    qspec = pl.BlockSpec((None, blk, D_ATT), lambda bi, i: (bi, i, 0))
    kctx = pl.BlockSpec((None, n_ctx, 2 * LANES), lambda bi, i: (bi, 0, 0))
    vctx = pl.BlockSpec((None, ATT_KV_W, n_ctx), lambda bi, i: (bi, 0, 0))
    snk = pl.BlockSpec(sink_tab.shape, lambda bi, i: (0, 0))
    if window:
        prev = lambda i: jnp.maximum(i - 1, 0)
        nxt = lambda i: jnp.minimum(i + 1, nb - 1)
        same = lambda i: i
        kspec = lambda f: pl.BlockSpec((None, blk, 2 * LANES), lambda bi, i: (bi, f(i), 0))
        vspec = lambda f: pl.BlockSpec((None, ATT_KV_W, blk), lambda bi, i: (bi, 0, f(i)))
        in_specs = [qspec, kspec(prev), kspec(same), kspec(nxt), vspec(prev), vspec(same), vspec(nxt),
                    kctx, vctx, snk]
        args = (q, k, k, k, vt, vt, vt, kx, vxt, sink_tab)
    else:
        in_specs = [qspec, kctx, vctx, snk]
        args = (q, kx, vxt, sink_tab)
    return pl.pallas_call(
        functools.partial(_attn_kernel, window=window),
        out_shape=jax.ShapeDtypeStruct((b, s, D_ATT), BF16),
        grid=(b, nb),
        in_specs=in_specs,
        out_specs=qspec,
        compiler_params=_params("parallel", "parallel"),
        name="attention_window" if window else "attention_ctx",
    )(*args)


def _ret_state_kernel(lgl_ref, k_ref, v_ref, s0_ref, r_ref, fin_ref, s_scr, *, cb, reverse):
    c = pl.program_id(1)
    L = RET_CHUNK

    @pl.when(c == 0)
    def _():
        s_scr[...] = s0_ref[...]

    jj = lax.broadcasted_iota(jnp.int32, (L, LANES), 0).astype(F32)
    expo = jj if reverse else (L - 1.0) - jj
    same_head = (lax.broadcasted_iota(jnp.int32, (LANES, LANES), 0) // HEAD_DIM
                 == lax.broadcasted_iota(jnp.int32, (LANES, LANES), 1) // HEAD_DIM)
    for pr in range(RET_HEADS // 2):
        cs = slice(pr * LANES, (pr + 1) * LANES)
        lgl = lgl_ref[pr]
        kdec = jnp.exp(expo * lgl)
        cdec = jnp.exp(float(L) * lgl)
        state = s_scr[pr]
        for t in range(cb):
            cc = cb - 1 - t if reverse else t
            rows = slice(cc * L, (cc + 1) * L)
            r_ref[cc, pr] = state
            kd = k_ref[rows, cs].astype(F32) * kdec
            u = jnp.dot(kd.T.astype(BF16), v_ref[rows, cs], preferred_element_type=F32)
            state = cdec * state + jnp.where(same_head, u, 0.0)
        s_scr[pr] = state

    @pl.when(c == pl.num_programs(1) - 1)
    def _():
        fin_ref[...] = s_scr[...]


def _ret_states(lgl, rk, rv, s0, reverse):
    b, s, _ = rk.shape
    nc = s // RET_CHUNK
    cb = _pick_tile(nc, 8)
    nblk = nc // cb
    npair = RET_HEADS // 2
    blk_idx = (lambda c: nblk - 1 - c) if reverse else (lambda c: c)
    tok = pl.BlockSpec((None, cb * RET_CHUNK, D_RET), lambda bi, c: (bi, blk_idx(c), 0))
    st = pl.BlockSpec((None, npair, LANES, LANES), lambda bi, c: (bi, 0, 0, 0))
    return pl.pallas_call(
        functools.partial(_ret_state_kernel, cb=cb, reverse=reverse),
        out_shape=(jax.ShapeDtypeStruct((b, nc, npair, LANES, LANES), F32),
                   jax.ShapeDtypeStruct((b, npair, LANES, LANES), F32)),
        grid=(b, nblk),
        in_specs=[pl.BlockSpec(lgl.shape, lambda bi, c: (0, 0, 0)), tok, tok, st],
        out_specs=(pl.BlockSpec((None, cb, npair, LANES, LANES),
                                lambda bi, c: (bi, blk_idx(c), 0, 0, 0)), st),
        scratch_shapes=[pltpu.VMEM((npair, LANES, LANES), F32)],
        compiler_params=_params("parallel", "arbitrary"),
        name="ret_state_bwd" if reverse else "ret_state_fwd",
    )(lgl, rk, rv, s0)


def _ret_out_kernel(lgf_ref, lgb_ref, q_ref, k_ref, v_ref, g_ref, rf_ref, rb_ref, gnw_ref, o_ref, *, cb):
    L = RET_CHUNK
    i0 = lax.broadcasted_iota(jnp.int32, (L, LANES), 0).astype(F32)
    i1 = lax.broadcasted_iota(jnp.int32, (L, LANES), 1).astype(F32)
    diff = i0 - i1
    lo = lax.broadcasted_iota(jnp.int32, (L, LANES), 1) < HEAD_DIM
    grp = _head_group_matrix()
    inv = 1.0 / HEAD_DIM
    tabs = []
    for pr in range(RET_HEADS // 2):
        lgf, lgb = lgf_ref[pr], lgb_ref[pr]
        dmat = [jnp.where(diff >= 0.0,
                          jnp.exp(jnp.maximum(diff, 0.0) * lgf[:, a:a + 1]),
                          jnp.exp(jnp.maximum(-diff, 0.0) * lgb[:, a:a + 1]))
                for a in (0, HEAD_DIM)]
        tabs.append((jnp.exp((i0 + 1.0) * lgf), jnp.exp((float(L) - i0) * lgb),
                     jnp.concatenate(dmat, axis=0), gnw_ref[:, pr * LANES:(pr + 1) * LANES]))

    def chunk(t, carry):
        rows = pl.ds(pl.multiple_of(t * L, L), L)
        for pr in range(RET_HEADS // 2):
            cs = slice(pr * LANES, (pr + 1) * LANES)
            xif, xib, dmat, gnw = tabs[pr]
            qp, kp, vp = q_ref[rows, cs], k_ref[rows, cs], v_ref[rows, cs]
            zero = jnp.zeros_like(kp)
            qz = jnp.concatenate([jnp.where(lo, qp, zero), jnp.where(lo, zero, qp)], axis=0)
            a = lax.dot_general(qz, kp, (((1,), (1,)), ((), ())), preferred_element_type=F32) * dmat
            qf = qp.astype(F32)
            lhs = jnp.concatenate([a[0:L].astype(BF16), a[L:].astype(BF16),
                                   (qf * xif).astype(BF16), (qf * xib).astype(BF16)], axis=1)
            rhs = jnp.concatenate([jnp.where(lo, vp, zero), jnp.where(lo, zero, vp),
                                   rf_ref[t, pr].astype(BF16), rb_ref[t, pr].astype(BF16)], axis=0)
            acc = jnp.dot(lhs, rhs, preferred_element_type=F32)
            dl = acc - _per_head_sum(acc, grp) * inv
            var = _per_head_sum(dl * dl, grp) * inv
            y = dl * lax.rsqrt(var + EPS) * gnw
            gt = g_ref[rows, cs]
            o_ref[rows, cs] = (gt * _sigmoid(gt) * y).astype(BF16)
        return carry

    lax.fori_loop(0, cb, chunk, 0, unroll=2)


def _ret_out(lgf, lgb, rq, rk, rv, rg, rf, rb, gnw):
    b, s, _ = rq.shape
    nc = s // RET_CHUNK
    cb = _pick_tile(nc, 8)
    npair = RET_HEADS // 2
    tok = pl.BlockSpec((None, cb * RET_CHUNK, D_RET), lambda bi, c: (bi, c, 0))
    st = pl.BlockSpec((None, cb, npair, LANES, LANES), lambda bi, c: (bi, c, 0, 0, 0))
    lg = pl.BlockSpec(lgf.shape, lambda bi, c: (0, 0, 0))
    return pl.pallas_call(
        functools.partial(_ret_out_kernel, cb=cb),
        out_shape=jax.ShapeDtypeStruct((b, s, D_RET), BF16),
        grid=(b, nc // cb),
        in_specs=[lg, lg, tok, tok, tok, tok, st, st, pl.BlockSpec((1, D_RET), lambda bi, c: (0, 0))],
        out_specs=tok,
        compiler_params=_params("parallel", "parallel"),
        name="ret_out",
    )(lgf, lgb, rq, rk, rv, rg, rf, rb, gnw)


def _conv_kernel(prev_ref, cur_ref, next_ref, w_ref, b_ref, lnw_ref, lnb_ref, o_ref, xpad, shifted, hbuf):
    i = pl.program_id(1)
    tm = cur_ref.shape[0]
    halo = CONV_HALO
    first = i == 0
    last = i == pl.num_programs(1) - 1
    xpad[0:halo, :] = jnp.where(first, 0.0, prev_ref[...])
    xpad[halo:halo + tm, :] = cur_ref[...]
    xpad[halo + tm:2 * halo + tm, :] = jnp.where(last, 0.0, next_ref[...])
    span = tm + 2 * halo - SUBLANES
    for r in range(SUBLANES):
        shifted[r] = xpad[r:r + span, :]
    base = halo - CONV_WIDTH // 2
    bias = b_ref[...]

    def sub(sb, carry):
        r0 = pl.multiple_of(sb * CONV_SUB, CONV_SUB)
        chains = [None] * CONV_CHAINS
        for w in range(CONV_WIDTH):
            off = base + w
            xs = shifted[off % SUBLANES, pl.ds(r0 + (off // SUBLANES) * SUBLANES, CONV_SUB), :]
            term = xs * w_ref[w:w + 1, :]
            c = w % CONV_CHAINS
            chains[c] = term if chains[c] is None else chains[c] + term
        hbuf[pl.ds(r0, CONV_SUB), :] = (chains[0] + chains[1]) + (chains[2] + chains[3]) + bias
        return carry

    lax.fori_loop(0, tm // CONV_SUB, sub, 0, unroll=2)
    h = hbuf[...]
    mu = jnp.mean(h, axis=-1, keepdims=True)
    dl = h - mu
    var = jnp.mean(dl * dl, axis=-1, keepdims=True)
    y = dl * lax.rsqrt(var + EPS) * lnw_ref[...] + lnb_ref[...]
    o_ref[...] = (y * _sigmoid(y)).astype(BF16)


def _conv(cv, w, bias, lnw, lnb):
    b, s, ch = cv.shape
    tm = _pick_tile(s, 512)
    hpb = tm // CONV_HALO
    nh = s // CONV_HALO
    row = pl.BlockSpec((1, ch), lambda bi, i: (0, 0))
    span = tm + 2 * CONV_HALO - SUBLANES
    return pl.pallas_call(
        _conv_kernel,
        out_shape=jax.ShapeDtypeStruct((b, s, ch), BF16),
        grid=(b, s // tm),
        in_specs=[pl.BlockSpec((None, CONV_HALO, ch), lambda bi, i: (bi, jnp.maximum(i * hpb - 1, 0), 0)),
                  pl.BlockSpec((None, tm, ch), lambda bi, i: (bi, i, 0)),
                  pl.BlockSpec((None, CONV_HALO, ch),
                               lambda bi, i: (bi, jnp.minimum((i + 1) * hpb, nh - 1), 0)),
                  pl.BlockSpec(w.shape, lambda bi, i: (0, 0)), row, row, row],
        out_specs=pl.BlockSpec((None, tm, ch), lambda bi, i: (bi, i, 0)),
        scratch_shapes=[pltpu.VMEM((tm + 2 * CONV_HALO, ch), F32),
                        pltpu.VMEM((SUBLANES, span, ch), F32),
                        pltpu.VMEM((tm, ch), F32)],
        compiler_params=_params("parallel", "parallel"),
        name="conv",
    )(cv, cv, cv, w, bias, lnw, lnb)


def _outproj_kernel(*refs, route):
    if route:
        (att_ref, ret_ref, cnv_ref, w_ref, x_ref, g1_ref, nw_ref, sc_ref, sh_ref, rcat_ref,
         xo_ref, h_ref, rt_ref) = refs
    else:
        att_ref, ret_ref, cnv_ref, w_ref, x_ref, g1_ref, nw_ref, sc_ref, sh_ref, xo_ref, h_ref = refs
    y = (jnp.dot(att_ref[...], w_ref[0:D_ATT, :], preferred_element_type=F32)
         + jnp.dot(ret_ref[...], w_ref[D_ATT:D_ATT + D_RET, :], preferred_element_type=F32)
         + jnp.dot(cnv_ref[...], w_ref[D_ATT + D_RET:, :], preferred_element_type=F32))
    xn = x_ref[...] + g1_ref[...] * y
    xo_ref[...] = xn
    h = _modulated_rms(xn, nw_ref[...], sc_ref[...], sh_ref[...])
    h_ref[...] = h.astype(h_ref.dtype)
    if route:
        hi = h.astype(BF16)
        lo = (h - hi.astype(F32)).astype(BF16)
        tm = h.shape[0]
        r = jnp.dot(jnp.concatenate([hi, lo], axis=0), rcat_ref[...], preferred_element_type=F32)
        logits = (r[0:tm, 0:LANES] + r[0:tm, LANES:]) + (r[tm:, 0:LANES] + r[tm:, LANES:])
        lane = lax.broadcasted_iota(jnp.int32, logits.shape, 1).astype(F32)
        logits = jnp.where(lane < N_EXPERTS, logits, NEG_INF)
        m1 = jnp.max(logits, axis=-1, keepdims=True)
        i1 = jnp.min(jnp.where(logits == m1, lane, float(LANES)), axis=-1, keepdims=True)
        rest = jnp.where(lane == i1, NEG_INF, logits)
        m2 = jnp.max(rest, axis=-1, keepdims=True)
        i2 = jnp.min(jnp.where(rest == m2, lane, float(LANES)), axis=-1, keepdims=True)
        e2 = jnp.exp(m2 - m1)
        w1 = 1.0 / (1.0 + e2)
        w2 = e2 / (1.0 + e2)
        rt_ref[...] = jnp.where(lane == 0.0, i1,
                                jnp.where(lane == 1.0, i2,
                                          jnp.where(lane == 2.0, w1, jnp.where(lane == 3.0, w2, 0.0))))


def _outproj(att, ret, cnv, w_bf, x, g1, nw, sc, sh, router=None, h_dtype=BF16):
    b, s, d = x.shape
    tm = _pick_tile(s, 512)
    tok = lambda n: pl.BlockSpec((None, tm, n), lambda bi, i: (bi, i, 0))
    per_b = pl.BlockSpec((None, 1, d), lambda bi, i: (bi, 0, 0))
    row = pl.BlockSpec((1, d), lambda bi, i: (0, 0))
    in_specs = [tok(D_ATT), tok(D_RET), tok(CONV_CH), pl.BlockSpec(w_bf.shape, lambda bi, i: (0, 0)),
                tok(d), per_b, row, per_b, per_b]
    args = [att, ret, cnv, w_bf, x, g1, nw, sc, sh]
    out_shape = [jax.ShapeDtypeStruct((b, s, d), F32), jax.ShapeDtypeStruct((b, s, d), h_dtype)]
    out_specs = [tok(d), tok(d)]
    if router is not None:
        rpad = jnp.zeros((d, LANES), F32).at[:, :N_EXPERTS].set(router)
        rhi = rpad.astype(BF16)
        rlo = (rpad - rhi.astype(F32)).astype(BF16)
        in_specs += [pl.BlockSpec((d, 2 * LANES), lambda bi, i: (0, 0))]
        args += [jnp.concatenate([rhi, rlo], axis=1)]
        out_shape.append(jax.ShapeDtypeStruct((b, s, LANES), F32))
        out_specs.append(tok(LANES))
    return pl.pallas_call(
        functools.partial(_outproj_kernel, route=router is not None),
        out_shape=tuple(out_shape),
        grid=(b, s // tm),
        in_specs=in_specs,
        out_specs=tuple(out_specs),
        compiler_params=_params("parallel", "parallel"),
        name="outproj_route" if router is not None else "outproj",
    )(*args)


def _swiglu_chunks(h, wg_ref, wu_ref, wd_ref, sub):
    total = None
    for c0 in range(0, wg_ref.shape[1], sub):
        cs = slice(c0, min(c0 + sub, wg_ref.shape[1]))
        gate = jnp.dot(h, wg_ref[:, cs].astype(BF16), preferred_element_type=F32)
        up = jnp.dot(h, wu_ref[:, cs].astype(BF16), preferred_element_type=F32)
        act = (gate * _sigmoid(gate) * up).astype(BF16)
        part = jnp.dot(act, wd_ref[cs, :].astype(BF16), preferred_element_type=F32)
        total = part if total is None else total + part
    return total


def _ffn_kernel(h_ref, wg_ref, wu_ref, wd_ref, x_ref, g2_ref, o_ref):
    y = _swiglu_chunks(h_ref[...], wg_ref, wu_ref, wd_ref, FFN_SUB)
    o_ref[...] = x_ref[...] + g2_ref[...] * y


def _ffn(h, wg, wu, wd, x, g2):
    b, s, d = x.shape
    f = wg.shape[1]
    tm = _pick_tile(s, 512)
    tok = lambda: pl.BlockSpec((None, tm, d), lambda bi, i: (bi, i, 0))
    res = lambda shape: pl.BlockSpec(shape, lambda bi, i: (0, 0), pipeline_mode=pl.Buffered(1))
    return pl.pallas_call(
        _ffn_kernel,
        out_shape=jax.ShapeDtypeStruct((b, s, d), F32),
        grid=(b, s // tm),
        in_specs=[tok(), res((d, f)), res((d, f)), res((f, d)), tok(),
                  pl.BlockSpec((None, 1, d), lambda bi, i: (bi, 0, 0))],
        out_specs=tok(),
        compiler_params=_params("parallel", "parallel"),
        name="ffn_dense",
    )(h, wg, wu, wd, x, g2)


def _gather_rows(idx, src):
    n = idx.shape[0]
    d = src.shape[1]
    mesh = plsc.VectorSubcoreMesh(core_axis_name="core", subcore_axis_name="subcore")

    @pl.kernel(out_type=jax.ShapeDtypeStruct((n, d), src.dtype), mesh=mesh, name="moe_gather")
    def gather(src_hbm, idx_hbm, out_hbm):
        def body(idx_vmem, out_vmem):
            pltpu.sync_copy(src_hbm.at[idx_vmem.at[0, pl.ds(0, SC_WINDOW)]], out_vmem)

        pltpu.emit_pipeline(
            body,
            grid=(n // SC_WINDOW,),
            in_specs=[pl.BlockSpec((1, LANES), lambda i: (i, 0))],
            out_specs=[pl.BlockSpec((SC_WINDOW, d), lambda i: (i, 0))],
            core_axis_name=("core", "subcore"),
            dimension_semantics=(pltpu.PARALLEL,),
        )(idx_hbm, out_hbm)

    idx_rows = jnp.pad(idx.reshape(n // SC_WINDOW, SC_WINDOW), ((0, 0), (0, LANES - SC_WINDOW)))
    return gather(src, idx_rows)


def _invert_rows(dest, n_rows):
    n = dest.shape[0]
    n_tok = n // 2
    mesh = plsc.VectorSubcoreMesh(core_axis_name="core", subcore_axis_name="subcore")

    @pl.kernel(out_type=jax.ShapeDtypeStruct((n_rows,), jnp.int32), mesh=mesh, name="moe_row_src",
               scratch_types=[pltpu.VMEM((n,), jnp.int32), pltpu.VMEM((n_rows,), jnp.int32)],
               compiler_params=pltpu.CompilerParams(needs_layout_passes=False))
    def invert(dest_hbm, out_hbm, dest_vmem, rows_vmem):
        @pl.when((lax.axis_index("core") == 0) & (lax.axis_index("subcore") == 0))
        def _():
            pltpu.sync_copy(dest_hbm, dest_vmem)

            @pl.loop(0, n_rows, step=SC_LANES)
            def _(r):
                rows_vmem[pl.ds(r, SC_LANES)] = jnp.zeros((SC_LANES,), jnp.int32)

            @pl.loop(0, n, step=SC_LANES)
            def _(a):
                tok = lax.iota(jnp.int32, SC_LANES) + a
                tok = jnp.where(tok >= n_tok, tok - n_tok, tok)
                plsc.store_scatter(rows_vmem, [dest_vmem[pl.ds(a, SC_LANES)]], tok)

            pltpu.sync_copy(rows_vmem, out_hbm)

    return invert(dest)


def _moe_kernel(te_ref, nu_ref, x_ref, wg_ref, wu_ref, wd_ref, *rest, tile0):
    o_ref, xb, acc = rest[-3:]
    t = pl.program_id(0) + tile0
    j = pl.program_id(1)
    nj = pl.num_programs(1)
    used = t < nu_ref[0]

    @pl.when(used & (j == 0))
    def _():
        xb[...] = x_ref[...].astype(BF16)
        acc[...] = jnp.zeros_like(acc)

    @pl.when(used)
    def _():
        acc[...] += _swiglu_chunks(xb[...], wg_ref, wu_ref, wd_ref, MOE_SUB)

    @pl.when(used & (j == nj - 1))
    def _():
        o_ref[...] = acc[...]

    @pl.when(jnp.logical_not(used) & (j == nj - 1))
    def _():
        o_ref[...] = jnp.zeros_like(o_ref)


def _moe_grouped(tile_expert, n_used, xs, wg, wu, wd, y_prev, tile0, n_rows):
    p, d = xs.shape
    f = wg.shape[2]
    tm = MOE_TM
    tf = MOE_TF
    nj = f // tf

    def jj(t, j, te, nu):
        return jnp.where(t + tile0 < nu[0], j, nj - 1)

    in_specs = [pl.BlockSpec((tm, d), lambda t, j, te, nu: (t, 0)),
                pl.BlockSpec((None, d, tf), lambda t, j, te, nu: (te[t + tile0], 0, jj(t, j, te, nu))),
                pl.BlockSpec((None, d, tf), lambda t, j, te, nu: (te[t + tile0], 0, jj(t, j, te, nu))),
                pl.BlockSpec((None, tf, d), lambda t, j, te, nu: (te[t + tile0], jj(t, j, te, nu), 0))]
    args = [tile_expert, n_used, xs, wg, wu, wd]
    aliases = {}
    if y_prev is not None:
        in_specs.append(pl.BlockSpec(memory_space=pl.ANY))
        args.append(y_prev)
        aliases = {len(args) - 1: 0}
    return pl.pallas_call(
        functools.partial(_moe_kernel, tile0=tile0),
        out_shape=jax.ShapeDtypeStruct((n_rows, d), F32),
        grid_spec=pltpu.PrefetchScalarGridSpec(
            num_scalar_prefetch=2,
            grid=(p // tm, nj),
            in_specs=in_specs,
            out_specs=pl.BlockSpec((tm, d), lambda t, j, te, nu: (t + tile0, 0)),
            scratch_shapes=[pltpu.VMEM((tm, d), BF16), pltpu.VMEM((tm, d), F32)]),
        input_output_aliases=aliases,
        compiler_params=_params("arbitrary", "arbitrary"),
        name="moe_grouped",
    )(*args)


def _combine_kernel(y1_ref, y2_ref, x_ref, g2_ref, rt_ref, o_ref):
    rt = rt_ref[...]
    o_ref[...] = x_ref[...] + g2_ref[...] * (rt[:, 2:3] * y1_ref[...] + rt[:, 3:4] * y2_ref[...])


def _moe_combine(yg, x, g2, route):
    b, s, d = x.shape
    tm = _pick_tile(s, 512)
    tok = lambda n: pl.BlockSpec((None, tm, n), lambda bi, i: (bi, i, 0))
    return pl.pallas_call(
        _combine_kernel,
        out_shape=jax.ShapeDtypeStruct((b, s, d), F32),
        grid=(b, s // tm),
        in_specs=[tok(d), pl.BlockSpec((None, tm, d), lambda bi, i: (b + bi, i, 0)), tok(d),
                  pl.BlockSpec((None, 1, d), lambda bi, i: (bi, 0, 0)), tok(LANES)],
        out_specs=tok(d),
        compiler_params=_params("parallel", "parallel"),
        name="moe_combine",
    )(yg, yg, x, g2, route)


def _moe(h, route, x, g2, wg, wu, wd):
    b, s, d = x.shape
    n_tok = b * s
    tm = MOE_TM
    rt = route.reshape(n_tok, LANES)
    flat_e = jnp.concatenate([rt[:, 0], rt[:, 1]]).astype(jnp.int32)
    onehot = (flat_e[:, None] == jnp.arange(N_EXPERTS, dtype=jnp.int32)[None, :]).astype(jnp.int32)
    csum = jnp.cumsum(onehot, axis=0)
    rank = jnp.sum(csum * onehot, axis=1) - 1
    counts = csum[-1]
    tiles_e = (counts + tm - 1) // tm
    tiles_cum = jnp.cumsum(tiles_e)
    row_start = (tiles_cum - tiles_e) * tm
    dest = jnp.sum(onehot * row_start[None, :], axis=1) + rank
    n_tiles = 2 * n_tok // tm + N_EXPERTS
    tile_ids = jnp.arange(n_tiles, dtype=jnp.int32)
    tile_expert = jnp.sum((tile_ids[:, None] >= tiles_cum[None, :]).astype(jnp.int32), axis=1)
    last_e = jnp.max(jnp.where(tiles_e > 0, jnp.arange(N_EXPERTS, dtype=jnp.int32), 0))
    tile_expert = jnp.minimum(tile_expert, last_e).astype(jnp.int32)
    n_used = tiles_cum[-1:].astype(jnp.int32)
    dest = dest.astype(jnp.int32)
    row_src = _invert_rows(dest, n_tiles * tm)
    n_chunks = max(c for c in (1, 2, 3, 4) if n_tiles % c == 0)
    tiles_c = n_tiles // n_chunks
    h_flat = h.reshape(n_tok, d)
    xs = [_gather_rows(row_src[c * tiles_c * tm:(c + 1) * tiles_c * tm], h_flat) for c in range(n_chunks)]
    y = None
    for c in range(n_chunks):
        y = _moe_grouped(tile_expert, n_used, xs[c], wg, wu, wd, y, c * tiles_c, n_tiles * tm)
    yg = _gather_rows(dest.astype(jnp.int32), y).reshape(2 * b, s, d)
    return _moe_combine(yg, x, g2, route)


def _rope_tables(n):
    rows = n // GRID_W
    r = jnp.repeat(jnp.arange(rows), GRID_W).astype(F32)
    col = jnp.tile(jnp.arange(GRID_W), rows).astype(F32)
    freqs = ROPE_BASE ** (-jnp.arange(ROPE_FREQS, dtype=F32) / ROPE_FREQS)
    ang = jnp.stack([r[:, None] * freqs, col[:, None] * freqs], axis=1)
    ang = jnp.repeat(ang[:, :, None, :], 2, axis=2).reshape(n, HEAD_DIM)
    ang = jnp.tile(ang, (1, LANES // HEAD_DIM))
    cos, sin = jnp.cos(ang), jnp.sin(ang)
    first_half = (jnp.arange(LANES) % (2 * ROPE_FREQS)) < ROPE_FREQS
    return cos, jnp.where(first_half, -sin, 0.0), jnp.where(first_half, 0.0, sin)


def _lane_rows(lg):
    return jnp.repeat(lg.astype(F32), HEAD_DIM).reshape(RET_HEADS // 2, 1, LANES)


def kernel(x, c, ctx, c_ctx, ada_w, ada_b, norm1_w, norm2_w, w_in, w_out, q_norm_w, k_norm_w,
           attn_sink, ret_decay_f, ret_decay_b, ret_gn_w, conv_w, conv_b, conv_ln_w, conv_ln_b,
           ffn_w_gate, ffn_w_up, ffn_w_down, router_w, moe_w_gate, moe_w_up, moe_w_down):
    b, n, d = x.shape
    n_ctx = ctx.shape[1]
    depth = ada_w.shape[0]
    cond = jnp.zeros((SUBLANES, d), F32).at[0:b].set(c).at[b].set(c_ctx)
    mods = _adaln(cond, ada_w, ada_b).reshape(depth, SUBLANES, 6, d)
    cos, sa, sb = _rope_tables(n)
    ones_c = jnp.ones((n_ctx, LANES), F32)
    zeros_c = jnp.zeros((n_ctx, LANES), F32)
    zero_state = jnp.zeros((b, RET_HEADS // 2, LANES, LANES), F32)
    row = lambda v: v.reshape(1, -1)
    for l in range(depth):
        last = l == depth - 1
        m_lat = [mods[l, 0:b, k][:, None, :] for k in range(6)]
        m_ctx = [jnp.broadcast_to(mods[l, b, k][None, None, :], (b, 1, d)) for k in range(6)]
        w_in_bf = w_in[l].astype(BF16)
        w_out_bf = w_out[l].astype(BF16)
        qw = row(jnp.tile(q_norm_w[l], LANES // HEAD_DIM))
        kw = row(jnp.tile(k_norm_w[l], LANES // HEAD_DIM))
        lgf = _lane_rows(jax.nn.log_sigmoid(ret_decay_f[l].astype(F32)))
        lgb = _lane_rows(jax.nn.log_sigmoid(ret_decay_b[l].astype(F32)))
        sink_tab = jnp.repeat(attn_sink[l].astype(F32), ATT_BLOCK).reshape(ATT_Q_HEADS // 2, 2 * ATT_BLOCK)

        q, k, v, rk, rv, rq, rg, cv = _inproj(x, row(norm1_w[l]), m_lat[1], m_lat[0], w_in_bf,
                                               cos, sa, sb, qw, kw)
        qc, kc, vc, rkc, rvc, rqc, rgc, cvc = _inproj(ctx, row(norm1_w[l]), m_ctx[1], m_ctx[0], w_in_bf,
                                                       ones_c, zeros_c, zeros_c, qw, kw)
        rf_c, s_f = _ret_states(lgf, rkc, rvc, zero_state, reverse=False)
        rb_c, s_b = _ret_states(lgb, rkc, rvc, zero_state, reverse=True)
        rf, _ = _ret_states(lgf, rk, rv, s_f, reverse=False)
        rb, _ = _ret_states(lgb, rk, rv, s_b, reverse=True)

        att = _attention(q, k, v, kc, vc, sink_tab, window=True)
        ret = _ret_out(lgf, lgb, rq, rk, rv, rg, rf, rb, row(ret_gn_w[l]))
        cnv = _conv(cv, conv_w[l], row(conv_b[l]), row(conv_ln_w[l]), row(conv_ln_b[l]))

        if l % 2 == 0:
            i = l // 2
            wg, wu, wd = ffn_w_gate[i].astype(BF16), ffn_w_up[i].astype(BF16), ffn_w_down[i].astype(BF16)
            x_mid, h2 = _outproj(att, ret, cnv, w_out_bf, x, m_lat[2], row(norm2_w[l]), m_lat[4], m_lat[3])
            x_new = _ffn(h2, wg, wu, wd, x_mid, m_lat[5])
        else:
            i = l // 2
            wg, wu, wd = moe_w_gate[i].astype(BF16), moe_w_up[i].astype(BF16), moe_w_down[i].astype(BF16)
            x_mid, h2, route = _outproj(att, ret, cnv, w_out_bf, x, m_lat[2], row(norm2_w[l]),
                                        m_lat[4], m_lat[3], router=router_w[i], h_dtype=F32)
            x_new = _moe(h2, route, x_mid, m_lat[5], wg, wu, wd)

        if not last:
            att_c = _attention(qc, None, None, kc, vc, sink_tab, window=False)
            ret_c = _ret_out(lgf, lgb, rqc, rkc, rvc, rgc, rf_c, rb_c, row(ret_gn_w[l]))
            cnv_c = _conv(cvc, conv_w[l], row(conv_b[l]), row(conv_ln_w[l]), row(conv_ln_b[l]))
            if l % 2 == 0:
                c_mid, h2c = _outproj(att_c, ret_c, cnv_c, w_out_bf, ctx, m_ctx[2], row(norm2_w[l]),
                                      m_ctx[4], m_ctx[3])
                ctx = _ffn(h2c, wg, wu, wd, c_mid, m_ctx[5])
            else:
                c_mid, h2c, route_c = _outproj(att_c, ret_c, cnv_c, w_out_bf, ctx, m_ctx[2],
                                               row(norm2_w[l]), m_ctx[4], m_ctx[3],
                                               router=router_w[i], h_dtype=F32)
                ctx = _moe(h2c, route_c, c_mid, m_ctx[5], wg, wu, wd)
        x = x_new
    return x
```

```python
import functools

import jax
import jax.numpy as jnp
from jax import lax
from jax.experimental import pallas as pl
from jax.experimental.pallas import tpu as pltpu
from jax.experimental.pallas import tpu_sc as plsc

F32 = jnp.float32
BF16 = jnp.bfloat16

GRID_W = 64
HEAD_DIM = 64
ATT_Q_HEADS = 8
ATT_KV_HEADS = 2
ATT_WINDOW = 128
ATT_BLOCK = 128
RET_HEADS = 4
RET_CHUNK = 128
RET_K_SCALE = HEAD_DIM ** -0.5
ATT_SCALE = HEAD_DIM ** -0.5
CONV_CH = 256
CONV_WIDTH = 31
ROPE_BASE = 10000.0
ROPE_FREQS = HEAD_DIM // 4
D_ATT = ATT_Q_HEADS * HEAD_DIM
D_RET = RET_HEADS * HEAD_DIM
ATT_KV_W = ATT_KV_HEADS * HEAD_DIM
C_ATT_K = 0
C_ATT_V = C_ATT_K + ATT_KV_W
C_RET_K = C_ATT_V + ATT_KV_W
C_RET_V = C_RET_K + D_RET
C_ATT_Q = C_RET_V + D_RET
C_RET_Q = C_ATT_Q + D_ATT
C_RET_G = C_RET_Q + D_RET
C_CONV = C_RET_G + D_RET
N_EXPERTS = 8
EPS = 1e-6
NEG_INF = -1e30

LANES = 128
SUBLANES = 8
VMEM_LIMIT = 48 * 1024 * 1024
CONV_HALO = 16
CONV_SUB = 16
CONV_CHAINS = 4
MOE_TM = 512
MOE_SUB = 512
FFN_SUB = 512
MOE_TF = 1792
SC_LANES = 16
SC_WINDOW = 64


def _params(*sem):
    return pltpu.CompilerParams(dimension_semantics=sem, vmem_limit_bytes=VMEM_LIMIT)


def _sigmoid(x):
    return 1.0 / (1.0 + jnp.exp(-x))


def _pack_bf16_pairs(v):
    c = v.shape[1] // 2
    bits = pltpu.bitcast(v.astype(BF16).astype(F32), jnp.uint32)
    packed = (bits[:, c:] & jnp.uint32(0xFFFF0000)) | (bits[:, :c] >> 16)
    return pltpu.bitcast(packed, jnp.int32)


def _unpack_bf16_pairs(p):
    bits = pltpu.bitcast(p, jnp.uint32)
    return pltpu.bitcast(bits << 16, F32), pltpu.bitcast(bits & jnp.uint32(0xFFFF0000), F32)


def _pick_tile(n, pref):
    t = min(n, pref)
    assert n % t == 0, (n, t)
    return t


def _adaln_kernel(c_ref, w_ref, b_ref, o_ref):
    c = c_ref[...]
    s = c * _sigmoid(c)
    o_ref[...] = jnp.dot(s, w_ref[...], preferred_element_type=F32,
                         precision=lax.Precision.HIGHEST) + b_ref[...]


def _adaln(cond, ada_w, ada_b):
    depth, d, n = ada_w.shape
    tn = _pick_tile(n, 1536)
    return pl.pallas_call(
        _adaln_kernel,
        out_shape=jax.ShapeDtypeStruct((depth, cond.shape[0], n), F32),
        grid=(depth, n // tn),
        in_specs=[pl.BlockSpec(cond.shape, lambda l, j: (0, 0)),
                  pl.BlockSpec((None, d, tn), lambda l, j: (l, 0, j)),
                  pl.BlockSpec((None, 1, tn), lambda l, j: (l, 0, j))],
        out_specs=pl.BlockSpec((None, cond.shape[0], tn), lambda l, j: (l, 0, j)),
        compiler_params=_params("parallel", "parallel"),
        name="adaln",
    )(cond, ada_w, ada_b.reshape(depth, 1, n))


def _modulated_rms(x, nw, sc, sh):
    ms = jnp.mean(x * x, axis=-1, keepdims=True)
    return (x * lax.rsqrt(ms + EPS) * nw) * (1.0 + sc) + sh


def _head_group_matrix():
    r = lax.broadcasted_iota(jnp.int32, (2 * LANES, 2 * LANES), 0) // HEAD_DIM
    c = lax.broadcasted_iota(jnp.int32, (2 * LANES, 2 * LANES), 1) // HEAD_DIM
    return jnp.where(r == c, 1.0, 0.0).astype(BF16)


def _per_head_sum(v, grp):
    hi = v.astype(BF16)
    lo = (v - hi.astype(F32)).astype(BF16)
    r = jnp.dot(jnp.concatenate([hi, lo], axis=1), grp, preferred_element_type=F32)
    return r[:, 0:LANES] + r[:, LANES:]


def _dup_halves(t):
    sw = pltpu.roll(t, HEAD_DIM, 1)
    lo = lax.broadcasted_iota(jnp.int32, t.shape, 1) < HEAD_DIM
    return jnp.where(lo, t, sw), jnp.where(lo, sw, t)


def _inproj_kernel(x_ref, nw_ref, sc_ref, sh_ref, w_ref, cos_ref, sa_ref, sb_ref, qw_ref, kw_ref,
                   q_ref, k_ref, v_ref, rk_ref, rv_ref, rq_ref, rg_ref, cv_ref):
    hb = _modulated_rms(x_ref[...], nw_ref[...], sc_ref[...], sh_ref[...]).astype(BF16)

    def proj(c0, n):
        return jnp.dot(hb, w_ref[:, c0:c0 + n], preferred_element_type=F32)

    grp = _head_group_matrix()
    cos, sa, sb = cos_ref[...], sa_ref[...], sb_ref[...]

    def norm_rope(p, wrow):
        y = p * lax.rsqrt(_per_head_sum(p * p, grp) * (1.0 / HEAD_DIM) + EPS) * wrow
        return (y * cos + pltpu.roll(y, LANES - ROPE_FREQS, 1) * sa
                + pltpu.roll(y, ROPE_FREQS, 1) * sb)

    kv = proj(C_ATT_K, 2 * ATT_KV_W)
    k0, k1 = _dup_halves(norm_rope(kv[:, 0:ATT_KV_W], kw_ref[...]))
    k_ref[:, 0:LANES] = k0.astype(BF16)
    k_ref[:, LANES:2 * LANES] = k1.astype(BF16)
    v_ref[...] = kv[:, ATT_KV_W:].T.astype(BF16)
    qw = qw_ref[...] * ATT_SCALE
    qall = proj(C_ATT_Q, D_ATT)
    for j in range(D_ATT // LANES):
        q_ref[:, j * LANES:(j + 1) * LANES] = norm_rope(qall[:, j * LANES:(j + 1) * LANES], qw).astype(BF16)
    rk_ref[...] = (proj(C_RET_K, D_RET) * RET_K_SCALE).astype(BF16)
    rv_ref[...] = proj(C_RET_V, D_RET).astype(BF16)
    rq_ref[...] = proj(C_RET_Q, D_RET).astype(BF16)
    rg_ref[...] = proj(C_RET_G, D_RET)
    cv_ref[...] = proj(C_CONV, CONV_CH) * _sigmoid(proj(C_CONV + CONV_CH, CONV_CH))


def _inproj(x, nw, sc, sh, w_bf, cos, sa, sb, qw, kw):
    b, s, d = x.shape
    tm = _pick_tile(s, 512)
    row = lambda n: pl.BlockSpec((1, n), lambda bi, i: (0, 0))
    per_b = pl.BlockSpec((None, 1, d), lambda bi, i: (bi, 0, 0))
    tab = pl.BlockSpec((tm, LANES), lambda bi, i: (i, 0))
    tok = lambda n: pl.BlockSpec((None, tm, n), lambda bi, i: (bi, i, 0))
    shp = lambda n, dt: jax.ShapeDtypeStruct((b, s, n), dt)
    return pl.pallas_call(
        _inproj_kernel,
        out_shape=(shp(D_ATT, BF16), shp(2 * LANES, BF16), jax.ShapeDtypeStruct((b, ATT_KV_W, s), BF16),
                   shp(D_RET, BF16), shp(D_RET, BF16), shp(D_RET, BF16), shp(D_RET, F32),
                   shp(CONV_CH, F32)),
        grid=(b, s // tm),
        in_specs=[tok(d), row(d), per_b, per_b,
                  pl.BlockSpec(w_bf.shape, lambda bi, i: (0, 0)),
                  tab, tab, tab, row(LANES), row(LANES)],
        out_specs=(tok(D_ATT), tok(2 * LANES), pl.BlockSpec((None, ATT_KV_W, tm), lambda bi, i: (bi, 0, i)),
                   tok(D_RET), tok(D_RET), tok(D_RET), tok(D_RET), tok(CONV_CH)),
        compiler_params=_params("parallel", "parallel"),
        name="inproj",
    )(x, nw, sc, sh, w_bf, cos, sa, sb, qw, kw)


def _attn_kernel(*refs, window):
    if window:
        q_ref, kp_ref, kc_ref, kn_ref, vp_ref, vc_ref, vn_ref, kx_ref, vx_ref, sink_ref, o_ref = refs
    else:
        q_ref, kx_ref, vx_ref, sink_ref, o_ref = refs
    blk = ATT_BLOCK
    n_ctx = kx_ref.shape[0]
    nk = 3 * blk + n_ctx if window else n_ctx
    if window:
        i = pl.program_id(1)
        key = lax.broadcasted_iota(jnp.int32, (blk, 2 * blk), 0)
        qry = lax.broadcasted_iota(jnp.int32, (blk, 2 * blk), 1) & (blk - 1)
        mask_prev = (key >= qry) & (i > 0)
        mask_next = (key <= qry) & (i < pl.num_programs(1) - 1)
    first_head = lax.broadcasted_iota(jnp.int32, (blk, LANES), 1) < HEAD_DIM
    ones = jnp.ones((2 * SUBLANES, nk), BF16)
    for g in range(ATT_KV_HEADS):
        gs = slice(g * LANES, (g + 1) * LANES)
        vs = slice(g * HEAD_DIM, (g + 1) * HEAD_DIM)
        if window:
            kcat = jnp.concatenate([kp_ref[:, gs], kc_ref[:, gs], kn_ref[:, gs], kx_ref[:, gs]], axis=0)
            vt = jnp.concatenate([vp_ref[vs, :], vc_ref[vs, :], vn_ref[vs, :], vx_ref[vs, :]], axis=1)
        else:
            kcat, vt = kx_ref[:, gs], vx_ref[vs, :]
        vaug = jnp.concatenate([vt, ones], axis=0)
        for pr in range(2):
            pair = 2 * g + pr
            cs = slice(pair * LANES, (pair + 1) * LANES)
            qp = q_ref[:, cs]
            zero = jnp.zeros_like(qp)
            w = jnp.concatenate([jnp.where(first_head, qp, zero), jnp.where(first_head, zero, qp)], axis=0)
            s = lax.dot_general(kcat, w, (((1,), (1,)), ((), ())), preferred_element_type=F32)
            if window:
                parts = [jnp.where(mask_prev, s[0:blk], NEG_INF), s[blk:2 * blk],
                         jnp.where(mask_next, s[2 * blk:3 * blk], NEG_INF), s[3 * blk:]]
            else:
                parts = [s]
            snk = sink_ref[pair:pair + 1, :]
            m = snk
            for part in parts:
                m = jnp.maximum(m, jnp.max(part, axis=0, keepdims=True))
            p = jnp.concatenate([jnp.exp(part - m).astype(BF16) for part in parts], axis=0)
            o = jnp.dot(vaug, p, preferred_element_type=F32)
            den = o[HEAD_DIM:HEAD_DIM + 1, :] + jnp.exp(snk - m)
            on = o[0:HEAD_DIM, :] * (1.0 / den)
            ot = jnp.concatenate([on[:, 0:blk], on[:, blk:2 * blk]], axis=0)
            o_ref[:, cs] = ot.T.astype(BF16)


def _attention(q, k, vt, kx, vxt, sink_tab, window):
    b, s, _ = q.shape
    blk = ATT_BLOCK
    nb = s // blk
    n_ctx = kx.shape[1]
    qspec = pl.BlockSpec((None, blk, D_ATT), lambda bi, i: (bi, i, 0))
    kctx = pl.BlockSpec((None, n_ctx, 2 * LANES), lambda bi, i: (bi, 0, 0))
    vctx = pl.BlockSpec((None, ATT_KV_W, n_ctx), lambda bi, i: (bi, 0, 0))
    snk = pl.BlockSpec(sink_tab.shape, lambda bi, i: (0, 0))
    if window:
        prev = lambda i: jnp.maximum(i - 1, 0)
        nxt = lambda i: jnp.minimum(i + 1, nb - 1)
        same = lambda i: i
        kspec = lambda f: pl.BlockSpec((None, blk, 2 * LANES), lambda bi, i: (bi, f(i), 0))
        vspec = lambda f: pl.BlockSpec((None, ATT_KV_W, blk), lambda bi, i: (bi, 0, f(i)))
        in_specs = [qspec, kspec(prev), kspec(same), kspec(nxt), vspec(prev), vspec(same), vspec(nxt),
                    kctx, vctx, snk]
        args = (q, k, k, k, vt, vt, vt, kx, vxt, sink_tab)
    else:
        in_specs = [qspec, kctx, vctx, snk]
        args = (q, kx, vxt, sink_tab)
    return pl.pallas_call(
        functools.partial(_attn_kernel, window=window),
        out_shape=jax.ShapeDtypeStruct((b, s, D_ATT), BF16),
        grid=(b, nb),
        in_specs=in_specs,
        out_specs=qspec,
        compiler_params=_params("parallel", "parallel"),
        name="attention_window" if window else "attention_ctx",
    )(*args)


def _ret_state_kernel(lgl_ref, k_ref, v_ref, s0_ref, r_ref, fin_ref, s_scr, *, cb, reverse):
    c = pl.program_id(1)
    L = RET_CHUNK

    @pl.when(c == 0)
    def _():
        s_scr[...] = s0_ref[...]

    jj = lax.broadcasted_iota(jnp.int32, (L, LANES), 0).astype(F32)
    expo = jj if reverse else (L - 1.0) - jj
    same_head = (lax.broadcasted_iota(jnp.int32, (LANES, LANES), 0) // HEAD_DIM
                 == lax.broadcasted_iota(jnp.int32, (LANES, LANES), 1) // HEAD_DIM)
    for pr in range(RET_HEADS // 2):
        cs = slice(pr * LANES, (pr + 1) * LANES)
        lgl = lgl_ref[pr]
        kdec = jnp.exp(expo * lgl)
        cdec = jnp.exp(float(L) * lgl)
        state = s_scr[pr]
        for t in range(cb):
            cc = cb - 1 - t if reverse else t
            rows = slice(cc * L, (cc + 1) * L)
            r_ref[cc, pr] = state
            kd = k_ref[rows, cs].astype(F32) * kdec
            u = jnp.dot(kd.T.astype(BF16), v_ref[rows, cs], preferred_element_type=F32)
            state = cdec * state + jnp.where(same_head, u, 0.0)
        s_scr[pr] = state

    @pl.when(c == pl.num_programs(1) - 1)
    def _():
        fin_ref[...] = s_scr[...]


def _ret_states(lgl, rk, rv, s0, reverse):
    b, s, _ = rk.shape
    nc = s // RET_CHUNK
    cb = _pick_tile(nc, 8)
    nblk = nc // cb
    npair = RET_HEADS // 2
    blk_idx = (lambda c: nblk - 1 - c) if reverse else (lambda c: c)
    tok = pl.BlockSpec((None, cb * RET_CHUNK, D_RET), lambda bi, c: (bi, blk_idx(c), 0))
    st = pl.BlockSpec((None, npair, LANES, LANES), lambda bi, c: (bi, 0, 0, 0))
    return pl.pallas_call(
        functools.partial(_ret_state_kernel, cb=cb, reverse=reverse),
        out_shape=(jax.ShapeDtypeStruct((b, nc, npair, LANES, LANES), F32),
                   jax.ShapeDtypeStruct((b, npair, LANES, LANES), F32)),
        grid=(b, nblk),
        in_specs=[pl.BlockSpec(lgl.shape, lambda bi, c: (0, 0, 0)), tok, tok, st],
        out_specs=(pl.BlockSpec((None, cb, npair, LANES, LANES),
                                lambda bi, c: (bi, blk_idx(c), 0, 0, 0)), st),
        scratch_shapes=[pltpu.VMEM((npair, LANES, LANES), F32)],
        compiler_params=_params("parallel", "arbitrary"),
        name="ret_state_bwd" if reverse else "ret_state_fwd",
    )(lgl, rk, rv, s0)


def _ret_out_kernel(lgf_ref, lgb_ref, q_ref, k_ref, v_ref, g_ref, rf_ref, rb_ref, gnw_ref, o_ref, *, cb):
    L = RET_CHUNK
    i0 = lax.broadcasted_iota(jnp.int32, (L, LANES), 0).astype(F32)
    i1 = lax.broadcasted_iota(jnp.int32, (L, LANES), 1).astype(F32)
    diff = i0 - i1
    lo = lax.broadcasted_iota(jnp.int32, (L, LANES), 1) < HEAD_DIM
    grp = _head_group_matrix()
    inv = 1.0 / HEAD_DIM
    tabs = []
    for pr in range(RET_HEADS // 2):
        lgf, lgb = lgf_ref[pr], lgb_ref[pr]
        dmat = [jnp.where(diff >= 0.0,
                          jnp.exp(jnp.maximum(diff, 0.0) * lgf[:, a:a + 1]),
                          jnp.exp(jnp.maximum(-diff, 0.0) * lgb[:, a:a + 1]))
                for a in (0, HEAD_DIM)]
        tabs.append((jnp.exp((i0 + 1.0) * lgf), jnp.exp((float(L) - i0) * lgb),
                     jnp.concatenate(dmat, axis=0), gnw_ref[:, pr * LANES:(pr + 1) * LANES]))

    def chunk(t, carry):
        rows = pl.ds(pl.multiple_of(t * L, L), L)
        for pr in range(RET_HEADS // 2):
            cs = slice(pr * LANES, (pr + 1) * LANES)
            xif, xib, dmat, gnw = tabs[pr]
            qp, kp, vp = q_ref[rows, cs], k_ref[rows, cs], v_ref[rows, cs]
            zero = jnp.zeros_like(kp)
            qz = jnp.concatenate([jnp.where(lo, qp, zero), jnp.where(lo, zero, qp)], axis=0)
            a = lax.dot_general(qz, kp, (((1,), (1,)), ((), ())), preferred_element_type=F32) * dmat
            qf = qp.astype(F32)
            lhs = jnp.concatenate([a[0:L].astype(BF16), a[L:].astype(BF16),
                                   (qf * xif).astype(BF16), (qf * xib).astype(BF16)], axis=1)
            rhs = jnp.concatenate([jnp.where(lo, vp, zero), jnp.where(lo, zero, vp),
                                   rf_ref[t, pr].astype(BF16), rb_ref[t, pr].astype(BF16)], axis=0)
            acc = jnp.dot(lhs, rhs, preferred_element_type=F32)
            dl = acc - _per_head_sum(acc, grp) * inv
            var = _per_head_sum(dl * dl, grp) * inv
            y = dl * lax.rsqrt(var + EPS) * gnw
            gt = g_ref[rows, cs]
            o_ref[rows, cs] = (gt * _sigmoid(gt) * y).astype(BF16)
        return carry

    lax.fori_loop(0, cb, chunk, 0, unroll=4)


def _ret_out(lgf, lgb, rq, rk, rv, rg, rf, rb, gnw):
    b, s, _ = rq.shape
    nc = s // RET_CHUNK
    cb = _pick_tile(nc, 8)
    npair = RET_HEADS // 2
    tok = pl.BlockSpec((None, cb * RET_CHUNK, D_RET), lambda bi, c: (bi, c, 0))
    st = pl.BlockSpec((None, cb, npair, LANES, LANES), lambda bi, c: (bi, c, 0, 0, 0))
    lg = pl.BlockSpec(lgf.shape, lambda bi, c: (0, 0, 0))
    return pl.pallas_call(
        functools.partial(_ret_out_kernel, cb=cb),
        out_shape=jax.ShapeDtypeStruct((b, s, D_RET), BF16),
        grid=(b, nc // cb),
        in_specs=[lg, lg, tok, tok, tok, tok, st, st, pl.BlockSpec((1, D_RET), lambda bi, c: (0, 0))],
        out_specs=tok,
        compiler_params=_params("parallel", "parallel"),
        name="ret_out",
    )(lgf, lgb, rq, rk, rv, rg, rf, rb, gnw)


def _conv_kernel(prev_ref, cur_ref, next_ref, w_ref, b_ref, lnw_ref, lnb_ref, o_ref, xpad, shifted, hbuf):
    i = pl.program_id(1)
    tm = cur_ref.shape[0]
    halo = CONV_HALO
    first = i == 0
    last = i == pl.num_programs(1) - 1
    xpad[0:halo, :] = jnp.where(first, 0.0, prev_ref[...])
    xpad[halo:halo + tm, :] = cur_ref[...]
    xpad[halo + tm:2 * halo + tm, :] = jnp.where(last, 0.0, next_ref[...])
    span = tm + 2 * halo - SUBLANES
    for r in range(SUBLANES):
        shifted[r] = xpad[r:r + span, :]
    base = halo - CONV_WIDTH // 2
    bias = b_ref[...]

    def sub(sb, carry):
        r0 = pl.multiple_of(sb * CONV_SUB, CONV_SUB)
        chains = [None] * CONV_CHAINS
        for w in range(CONV_WIDTH):
            off = base + w
            xs = shifted[off % SUBLANES, pl.ds(r0 + (off // SUBLANES) * SUBLANES, CONV_SUB), :]
            term = xs * w_ref[w:w + 1, :]
            c = w % CONV_CHAINS
            chains[c] = term if chains[c] is None else chains[c] + term
        hbuf[pl.ds(r0, CONV_SUB), :] = (chains[0] + chains[1]) + (chains[2] + chains[3]) + bias
        return carry

    lax.fori_loop(0, tm // CONV_SUB, sub, 0, unroll=2)
    h = hbuf[...]
    mu = jnp.mean(h, axis=-1, keepdims=True)
    dl = h - mu
    var = jnp.mean(dl * dl, axis=-1, keepdims=True)
    y = dl * lax.rsqrt(var + EPS) * lnw_ref[...] + lnb_ref[...]
    o_ref[...] = (y * _sigmoid(y)).astype(BF16)


def _conv(cv, w, bias, lnw, lnb):
    b, s, ch = cv.shape
    tm = _pick_tile(s, 512)
    hpb = tm // CONV_HALO
    nh = s // CONV_HALO
    row = pl.BlockSpec((1, ch), lambda bi, i: (0, 0))
    span = tm + 2 * CONV_HALO - SUBLANES
    return pl.pallas_call(
        _conv_kernel,
        out_shape=jax.ShapeDtypeStruct((b, s, ch), BF16),
        grid=(b, s // tm),
        in_specs=[pl.BlockSpec((None, CONV_HALO, ch), lambda bi, i: (bi, jnp.maximum(i * hpb - 1, 0), 0)),
                  pl.BlockSpec((None, tm, ch), lambda bi, i: (bi, i, 0)),
                  pl.BlockSpec((None, CONV_HALO, ch),
                               lambda bi, i: (bi, jnp.minimum((i + 1) * hpb, nh - 1), 0)),
                  pl.BlockSpec(w.shape, lambda bi, i: (0, 0)), row, row, row],
        out_specs=pl.BlockSpec((None, tm, ch), lambda bi, i: (bi, i, 0)),
        scratch_shapes=[pltpu.VMEM((tm + 2 * CONV_HALO, ch), F32),
                        pltpu.VMEM((SUBLANES, span, ch), F32),
                        pltpu.VMEM((tm, ch), F32)],
        compiler_params=_params("parallel", "parallel"),
        name="conv",
    )(cv, cv, cv, w, bias, lnw, lnb)


def _outproj_kernel(*refs, route):
    if route:
        (att_ref, ret_ref, cnv_ref, w_ref, x_ref, g1_ref, nw_ref, sc_ref, sh_ref, rcat_ref,
         xo_ref, h_ref, rt_ref) = refs
    else:
        att_ref, ret_ref, cnv_ref, w_ref, x_ref, g1_ref, nw_ref, sc_ref, sh_ref, xo_ref, h_ref = refs
    y = (jnp.dot(att_ref[...], w_ref[0:D_ATT, :], preferred_element_type=F32)
         + jnp.dot(ret_ref[...], w_ref[D_ATT:D_ATT + D_RET, :], preferred_element_type=F32)
         + jnp.dot(cnv_ref[...], w_ref[D_ATT + D_RET:, :], preferred_element_type=F32))
    xn = x_ref[...] + g1_ref[...] * y
    xo_ref[...] = xn
    h = _modulated_rms(xn, nw_ref[...], sc_ref[...], sh_ref[...])
    h_ref[...] = _pack_bf16_pairs(h) if route else h.astype(BF16)
    if route:
        hi = h.astype(BF16)
        lo = (h - hi.astype(F32)).astype(BF16)
        tm = h.shape[0]
        r = jnp.dot(jnp.concatenate([hi, lo], axis=0), rcat_ref[...], preferred_element_type=F32)
        logits = (r[0:tm, 0:LANES] + r[0:tm, LANES:]) + (r[tm:, 0:LANES] + r[tm:, LANES:])
        lane = lax.broadcasted_iota(jnp.int32, logits.shape, 1).astype(F32)
        logits = jnp.where(lane < N_EXPERTS, logits, NEG_INF)
        m1 = jnp.max(logits, axis=-1, keepdims=True)
        i1 = jnp.min(jnp.where(logits == m1, lane, float(LANES)), axis=-1, keepdims=True)
        rest = jnp.where(lane == i1, NEG_INF, logits)
        m2 = jnp.max(rest, axis=-1, keepdims=True)
        i2 = jnp.min(jnp.where(rest == m2, lane, float(LANES)), axis=-1, keepdims=True)
        e2 = jnp.exp(m2 - m1)
        w1 = 1.0 / (1.0 + e2)
        w2 = e2 / (1.0 + e2)
        rt_ref[...] = jnp.where(lane == 0.0, i1,
                                jnp.where(lane == 1.0, i2,
                                          jnp.where(lane == 2.0, w1, jnp.where(lane == 3.0, w2, 0.0))))


def _outproj(att, ret, cnv, w_bf, x, g1, nw, sc, sh, router=None):
    b, s, d = x.shape
    tm = _pick_tile(s, 512)
    tok = lambda n: pl.BlockSpec((None, tm, n), lambda bi, i: (bi, i, 0))
    per_b = pl.BlockSpec((None, 1, d), lambda bi, i: (bi, 0, 0))
    row = pl.BlockSpec((1, d), lambda bi, i: (0, 0))
    in_specs = [tok(D_ATT), tok(D_RET), tok(CONV_CH), pl.BlockSpec(w_bf.shape, lambda bi, i: (0, 0)),
                tok(d), per_b, row, per_b, per_b]
    args = [att, ret, cnv, w_bf, x, g1, nw, sc, sh]
    h_shape = (b, s, d // 2) if router is not None else (b, s, d)
    out_shape = [jax.ShapeDtypeStruct((b, s, d), F32),
                 jax.ShapeDtypeStruct(h_shape, jnp.int32 if router is not None else BF16)]
    out_specs = [tok(d), tok(h_shape[2])]
    if router is not None:
        rpad = jnp.zeros((d, LANES), F32).at[:, :N_EXPERTS].set(router)
        rhi = rpad.astype(BF16)
        rlo = (rpad - rhi.astype(F32)).astype(BF16)
        in_specs += [pl.BlockSpec((d, 2 * LANES), lambda bi, i: (0, 0))]
        args += [jnp.concatenate([rhi, rlo], axis=1)]
        out_shape.append(jax.ShapeDtypeStruct((b, s, LANES), F32))
        out_specs.append(tok(LANES))
    return pl.pallas_call(
        functools.partial(_outproj_kernel, route=router is not None),
        out_shape=tuple(out_shape),
        grid=(b, s // tm),
        in_specs=in_specs,
        out_specs=tuple(out_specs),
        compiler_params=_params("parallel", "parallel"),
        name="outproj_route" if router is not None else "outproj",
    )(*args)


def _swiglu_chunks(h, wg_ref, wu_ref, wd_ref, sub):
    total = None
    for c0 in range(0, wg_ref.shape[1], sub):
        cs = slice(c0, min(c0 + sub, wg_ref.shape[1]))
        gate = jnp.dot(h, wg_ref[:, cs].astype(BF16), preferred_element_type=F32)
        up = jnp.dot(h, wu_ref[:, cs].astype(BF16), preferred_element_type=F32)
        act = (gate * _sigmoid(gate) * up).astype(BF16)
        part = jnp.dot(act, wd_ref[cs, :].astype(BF16), preferred_element_type=F32)
        total = part if total is None else total + part
    return total


def _ffn_kernel(h_ref, wg_ref, wu_ref, wd_ref, x_ref, g2_ref, o_ref):
    y = _swiglu_chunks(h_ref[...], wg_ref, wu_ref, wd_ref, FFN_SUB)
    o_ref[...] = x_ref[...] + g2_ref[...] * y


def _ffn(h, wg, wu, wd, x, g2):
    b, s, d = x.shape
    f = wg.shape[1]
    tm = _pick_tile(s, 512)
    tok = lambda: pl.BlockSpec((None, tm, d), lambda bi, i: (bi, i, 0))
    res = lambda shape: pl.BlockSpec(shape, lambda bi, i: (0, 0), pipeline_mode=pl.Buffered(1))
    return pl.pallas_call(
        _ffn_kernel,
        out_shape=jax.ShapeDtypeStruct((b, s, d), F32),
        grid=(b, s // tm),
        in_specs=[tok(), res((d, f)), res((d, f)), res((f, d)), tok(),
                  pl.BlockSpec((None, 1, d), lambda bi, i: (bi, 0, 0))],
        out_specs=tok(),
        compiler_params=_params("parallel", "parallel"),
        name="ffn_dense",
    )(h, wg, wu, wd, x, g2)


def _gather_rows(idx, src):
    n = idx.shape[0]
    d = src.shape[1]
    mesh = plsc.VectorSubcoreMesh(core_axis_name="core", subcore_axis_name="subcore")

    @pl.kernel(out_type=jax.ShapeDtypeStruct((n, d), src.dtype), mesh=mesh, name="moe_gather")
    def gather(src_hbm, idx_hbm, out_hbm):
        def body(idx_vmem, out_vmem):
            pltpu.sync_copy(src_hbm.at[idx_vmem.at[0, pl.ds(0, SC_WINDOW)]], out_vmem)

        pltpu.emit_pipeline(
            body,
            grid=(n // SC_WINDOW,),
            in_specs=[pl.BlockSpec((1, LANES), lambda i: (i, 0))],
            out_specs=[pl.BlockSpec((SC_WINDOW, d), lambda i: (i, 0))],
            core_axis_name=("core", "subcore"),
            dimension_semantics=(pltpu.PARALLEL,),
        )(idx_hbm, out_hbm)

    idx_rows = jnp.pad(idx.reshape(n // SC_WINDOW, SC_WINDOW), ((0, 0), (0, LANES - SC_WINDOW)))
    return gather(src, idx_rows)


def _invert_rows(dest, n_rows):
    n = dest.shape[0]
    n_tok = n // 2
    mesh = plsc.VectorSubcoreMesh(core_axis_name="core", subcore_axis_name="subcore")

    @pl.kernel(out_type=jax.ShapeDtypeStruct((n_rows,), jnp.int32), mesh=mesh, name="moe_row_src",
               scratch_types=[pltpu.VMEM((n,), jnp.int32), pltpu.VMEM((n_rows,), jnp.int32)],
               compiler_params=pltpu.CompilerParams(needs_layout_passes=False))
    def invert(dest_hbm, out_hbm, dest_vmem, rows_vmem):
        @pl.when((lax.axis_index("core") == 0) & (lax.axis_index("subcore") == 0))
        def _():
            pltpu.sync_copy(dest_hbm, dest_vmem)

            @pl.loop(0, n_rows, step=SC_LANES)
            def _(r):
                rows_vmem[pl.ds(r, SC_LANES)] = jnp.zeros((SC_LANES,), jnp.int32)

            @pl.loop(0, n, step=SC_LANES)
            def _(a):
                tok = lax.iota(jnp.int32, SC_LANES) + a
                tok = jnp.where(tok >= n_tok, tok - n_tok, tok)
                plsc.store_scatter(rows_vmem, [dest_vmem[pl.ds(a, SC_LANES)]], tok)

            pltpu.sync_copy(rows_vmem, out_hbm)

    return invert(dest)


def _moe_kernel(te_ref, nu_ref, x_ref, wg_ref, wu_ref, wd_ref, *rest, tile0):
    o_ref, xb, acc = rest[-3:]
    t = pl.program_id(0) + tile0
    j = pl.program_id(1)
    nj = pl.num_programs(1)
    used = t < nu_ref[0]

    @pl.when(used & (j == 0))
    def _():
        lo, hi = _unpack_bf16_pairs(x_ref[...])
        half = lo.shape[1]
        xb[:, 0:half] = lo.astype(BF16)
        xb[:, half:] = hi.astype(BF16)
        acc[...] = jnp.zeros_like(acc)

    @pl.when(used)
    def _():
        acc[...] += _swiglu_chunks(xb[...], wg_ref, wu_ref, wd_ref, MOE_SUB)

    @pl.when(used & (j == nj - 1))
    def _():
        o_ref[...] = _pack_bf16_pairs(acc[...])

    @pl.when(jnp.logical_not(used) & (j == nj - 1))
    def _():
        o_ref[...] = jnp.zeros_like(o_ref)


def _moe_grouped(tile_expert, n_used, xs, wg, wu, wd, y_prev, tile0, n_rows):
    p, dp = xs.shape
    d = 2 * dp
    f = wg.shape[2]
    tm = MOE_TM
    tf = MOE_TF
    nj = f // tf

    def jj(t, j, te, nu):
        return jnp.where(t + tile0 < nu[0], j, nj - 1)

    in_specs = [pl.BlockSpec((tm, dp), lambda t, j, te, nu: (t, 0)),
                pl.BlockSpec((None, d, tf), lambda t, j, te, nu: (te[t + tile0], 0, jj(t, j, te, nu))),
                pl.BlockSpec((None, d, tf), lambda t, j, te, nu: (te[t + tile0], 0, jj(t, j, te, nu))),
                pl.BlockSpec((None, tf, d), lambda t, j, te, nu: (te[t + tile0], jj(t, j, te, nu), 0))]
    args = [tile_expert, n_used, xs, wg, wu, wd]
    aliases = {}
    if y_prev is not None:
        in_specs.append(pl.BlockSpec(memory_space=pl.ANY))
        args.append(y_prev)
        aliases = {len(args) - 1: 0}
    return pl.pallas_call(
        functools.partial(_moe_kernel, tile0=tile0),
        out_shape=jax.ShapeDtypeStruct((n_rows, dp), jnp.int32),
        grid_spec=pltpu.PrefetchScalarGridSpec(
            num_scalar_prefetch=2,
            grid=(p // tm, nj),
            in_specs=in_specs,
            out_specs=pl.BlockSpec((tm, dp), lambda t, j, te, nu: (t + tile0, 0)),
            scratch_shapes=[pltpu.VMEM((tm, d), BF16), pltpu.VMEM((tm, d), F32)]),
        input_output_aliases=aliases,
        compiler_params=_params("arbitrary", "arbitrary"),
        name="moe_grouped",
    )(*args)


def _combine_kernel(y1_ref, y2_ref, x_ref, g2_ref, rt_ref, o_ref):
    rt = rt_ref[...]
    w1, w2 = rt[:, 2:3], rt[:, 3:4]
    half = y1_ref.shape[1]
    for k, (a, c) in enumerate(zip(_unpack_bf16_pairs(y1_ref[...]), _unpack_bf16_pairs(y2_ref[...]))):
        cs = slice(k * half, (k + 1) * half)
        o_ref[:, cs] = x_ref[:, cs] + g2_ref[:, cs] * (w1 * a + w2 * c)


def _moe_combine(yg, x, g2, route):
    b, s, d = x.shape
    tm = _pick_tile(s, 512)
    tok = lambda n: pl.BlockSpec((None, tm, n), lambda bi, i: (bi, i, 0))
    return pl.pallas_call(
        _combine_kernel,
        out_shape=jax.ShapeDtypeStruct((b, s, d), F32),
        grid=(b, s // tm),
        in_specs=[tok(d // 2), pl.BlockSpec((None, tm, d // 2), lambda bi, i: (b + bi, i, 0)), tok(d),
                  pl.BlockSpec((None, 1, d), lambda bi, i: (bi, 0, 0)), tok(LANES)],
        out_specs=tok(d),
        compiler_params=_params("parallel", "parallel"),
        name="moe_combine",
    )(yg, yg, x, g2, route)


def _moe(h, route, x, g2, wg, wu, wd):
    b, s, d = x.shape
    n_tok = b * s
    tm = MOE_TM
    rt = route.reshape(n_tok, LANES)
    flat_e = jnp.concatenate([rt[:, 0], rt[:, 1]]).astype(jnp.int32)
    onehot = (flat_e[:, None] == jnp.arange(N_EXPERTS, dtype=jnp.int32)[None, :]).astype(jnp.int32)
    csum = jnp.cumsum(onehot, axis=0)
    rank = jnp.sum(csum * onehot, axis=1) - 1
    counts = csum[-1]
    tiles_e = (counts + tm - 1) // tm
    tiles_cum = jnp.cumsum(tiles_e)
    row_start = (tiles_cum - tiles_e) * tm
    dest = jnp.sum(onehot * row_start[None, :], axis=1) + rank
    n_tiles = 2 * n_tok // tm + N_EXPERTS
    tile_ids = jnp.arange(n_tiles, dtype=jnp.int32)
    tile_expert = jnp.sum((tile_ids[:, None] >= tiles_cum[None, :]).astype(jnp.int32), axis=1)
    last_e = jnp.max(jnp.where(tiles_e > 0, jnp.arange(N_EXPERTS, dtype=jnp.int32), 0))
    tile_expert = jnp.minimum(tile_expert, last_e).astype(jnp.int32)
    n_used = tiles_cum[-1:].astype(jnp.int32)
    dest = dest.astype(jnp.int32)
    row_src = _invert_rows(dest, n_tiles * tm)
    n_chunks = max(c for c in (1, 2, 3, 4) if n_tiles % c == 0)
    tiles_c = n_tiles // n_chunks
    h_flat = h.reshape(n_tok, d // 2)
    xs = [_gather_rows(row_src[c * tiles_c * tm:(c + 1) * tiles_c * tm], h_flat) for c in range(n_chunks)]
    y = None
    for c in range(n_chunks):
        y = _moe_grouped(tile_expert, n_used, xs[c], wg, wu, wd, y, c * tiles_c, n_tiles * tm)
    yg = _gather_rows(dest, y).reshape(2 * b, s, d // 2)
    return _moe_combine(yg, x, g2, route)


def _rope_tables(n):
    rows = n // GRID_W
    r = jnp.repeat(jnp.arange(rows), GRID_W).astype(F32)
    col = jnp.tile(jnp.arange(GRID_W), rows).astype(F32)
    freqs = ROPE_BASE ** (-jnp.arange(ROPE_FREQS, dtype=F32) / ROPE_FREQS)
    ang = jnp.stack([r[:, None] * freqs, col[:, None] * freqs], axis=1)
    ang = jnp.repeat(ang[:, :, None, :], 2, axis=2).reshape(n, HEAD_DIM)
    ang = jnp.tile(ang, (1, LANES // HEAD_DIM))
    cos, sin = jnp.cos(ang), jnp.sin(ang)
    first_half = (jnp.arange(LANES) % (2 * ROPE_FREQS)) < ROPE_FREQS
    return cos, jnp.where(first_half, -sin, 0.0), jnp.where(first_half, 0.0, sin)


def _lane_rows(lg):
    return jnp.repeat(lg.astype(F32), HEAD_DIM).reshape(RET_HEADS // 2, 1, LANES)


def kernel(x, c, ctx, c_ctx, ada_w, ada_b, norm1_w, norm2_w, w_in, w_out, q_norm_w, k_norm_w,
           attn_sink, ret_decay_f, ret_decay_b, ret_gn_w, conv_w, conv_b, conv_ln_w, conv_ln_b,
           ffn_w_gate, ffn_w_up, ffn_w_down, router_w, moe_w_gate, moe_w_up, moe_w_down):
    b, n, d = x.shape
    n_ctx = ctx.shape[1]
    depth = ada_w.shape[0]
    cond = jnp.zeros((SUBLANES, d), F32).at[0:b].set(c).at[b].set(c_ctx)
    mods = _adaln(cond, ada_w, ada_b).reshape(depth, SUBLANES, 6, d)
    cos, sa, sb = _rope_tables(n)
    ones_c = jnp.ones((n_ctx, LANES), F32)
    zeros_c = jnp.zeros((n_ctx, LANES), F32)
    zero_state = jnp.zeros((b, RET_HEADS // 2, LANES, LANES), F32)
    row = lambda v: v.reshape(1, -1)
    for l in range(depth):
        last = l == depth - 1
        m_lat = [mods[l, 0:b, k][:, None, :] for k in range(6)]
        m_ctx = [jnp.broadcast_to(mods[l, b, k][None, None, :], (b, 1, d)) for k in range(6)]
        w_in_bf = w_in[l].astype(BF16)
        w_out_bf = w_out[l].astype(BF16)
        qw = row(jnp.tile(q_norm_w[l], LANES // HEAD_DIM))
        kw = row(jnp.tile(k_norm_w[l], LANES // HEAD_DIM))
        lgf = _lane_rows(jax.nn.log_sigmoid(ret_decay_f[l].astype(F32)))
        lgb = _lane_rows(jax.nn.log_sigmoid(ret_decay_b[l].astype(F32)))
        sink_tab = jnp.repeat(attn_sink[l].astype(F32), ATT_BLOCK).reshape(ATT_Q_HEADS // 2, 2 * ATT_BLOCK)

        q, k, v, rk, rv, rq, rg, cv = _inproj(x, row(norm1_w[l]), m_lat[1], m_lat[0], w_in_bf,
                                               cos, sa, sb, qw, kw)
        qc, kc, vc, rkc, rvc, rqc, rgc, cvc = _inproj(ctx, row(norm1_w[l]), m_ctx[1], m_ctx[0], w_in_bf,
                                                       ones_c, zeros_c, zeros_c, qw, kw)
        rf_c, s_f = _ret_states(lgf, rkc, rvc, zero_state, reverse=False)
        rb_c, s_b = _ret_states(lgb, rkc, rvc, zero_state, reverse=True)
        rf, _ = _ret_states(lgf, rk, rv, s_f, reverse=False)
        rb, _ = _ret_states(lgb, rk, rv, s_b, reverse=True)

        att = _attention(q, k, v, kc, vc, sink_tab, window=True)
        ret = _ret_out(lgf, lgb, rq, rk, rv, rg, rf, rb, row(ret_gn_w[l]))
        cnv = _conv(cv, conv_w[l], row(conv_b[l]), row(conv_ln_w[l]), row(conv_ln_b[l]))

        if l % 2 == 0:
            i = l // 2
            wg, wu, wd = ffn_w_gate[i].astype(BF16), ffn_w_up[i].astype(BF16), ffn_w_down[i].astype(BF16)
            x_mid, h2 = _outproj(att, ret, cnv, w_out_bf, x, m_lat[2], row(norm2_w[l]), m_lat[4], m_lat[3])
            x_new = _ffn(h2, wg, wu, wd, x_mid, m_lat[5])
        else:
            i = l // 2
            wg, wu, wd = moe_w_gate[i].astype(BF16), moe_w_up[i].astype(BF16), moe_w_down[i].astype(BF16)
            x_mid, h2, route = _outproj(att, ret, cnv, w_out_bf, x, m_lat[2], row(norm2_w[l]),
                                        m_lat[4], m_lat[3], router=router_w[i])
            x_new = _moe(h2, route, x_mid, m_lat[5], wg, wu, wd)

        if not last:
            att_c = _attention(qc, None, None, kc, vc, sink_tab, window=False)
            ret_c = _ret_out(lgf, lgb, rqc, rkc, rvc, rgc, rf_c, rb_c, row(ret_gn_w[l]))
            cnv_c = _conv(cvc, conv_w[l], row(conv_b[l]), row(conv_ln_w[l]), row(conv_ln_b[l]))
            if l % 2 == 0:
                c_mid, h2c = _outproj(att_c, ret_c, cnv_c, w_out_bf, ctx, m_ctx[2], row(norm2_w[l]),
                                      m_ctx[4], m_ctx[3])
                ctx = _ffn(h2c, wg, wu, wd, c_mid, m_ctx[5])
            else:
                c_mid, h2c, route_c = _outproj(att_c, ret_c, cnv_c, w_out_bf, ctx, m_ctx[2],
                                               row(norm2_w[l]), m_ctx[4], m_ctx[3],
                                               router=router_w[i])
                ctx = _moe(h2c, route_c, c_mid, m_ctx[5], wg, wu, wd)
        x = x_new
    return x
```

```python
import functools

import jax
import jax.numpy as jnp
from jax import lax
from jax.experimental import pallas as pl
from jax.experimental.pallas import tpu as pltpu
from jax.experimental.pallas import tpu_sc as plsc

F32 = jnp.float32
BF16 = jnp.bfloat16

GRID_W = 64
HEAD_DIM = 64
ATT_Q_HEADS = 8
ATT_KV_HEADS = 2
ATT_WINDOW = 128
ATT_BLOCK = 128
RET_HEADS = 4
RET_CHUNK = 128
RET_K_SCALE = HEAD_DIM ** -0.5
ATT_SCALE = HEAD_DIM ** -0.5
CONV_CH = 256
CONV_WIDTH = 31
ROPE_BASE = 10000.0
ROPE_FREQS = HEAD_DIM // 4
D_ATT = ATT_Q_HEADS * HEAD_DIM
D_RET = RET_HEADS * HEAD_DIM
ATT_KV_W = ATT_KV_HEADS * HEAD_DIM
C_ATT_K = 0
C_ATT_V = C_ATT_K + ATT_KV_W
C_RET_K = C_ATT_V + ATT_KV_W
C_RET_V = C_RET_K + D_RET
C_ATT_Q = C_RET_V + D_RET
C_RET_Q = C_ATT_Q + D_ATT
C_RET_G = C_RET_Q + D_RET
C_CONV = C_RET_G + D_RET
N_EXPERTS = 8
EPS = 1e-6
NEG_INF = -1e30

LANES = 128
SUBLANES = 8
VMEM_LIMIT = 48 * 1024 * 1024
CONV_HALO = 16
CONV_SUB = 16
CONV_CHAINS = 4
MOE_TM = 512
MOE_CHUNKS = 6
MOE_SUB = 512
FFN_SUB = 512
MOE_TF = 1792
SC_LANES = 16
SC_WINDOW = 64


def _params(*sem):
    return pltpu.CompilerParams(dimension_semantics=sem, vmem_limit_bytes=VMEM_LIMIT)


def _sigmoid(x):
    return 1.0 / (1.0 + jnp.exp(-x))


def _pack_bf16_pairs(v):
    c = v.shape[1] // 2
    bits = pltpu.bitcast(v.astype(BF16).astype(F32), jnp.uint32)
    packed = (bits[:, c:] & jnp.uint32(0xFFFF0000)) | (bits[:, :c] >> 16)
    return pltpu.bitcast(packed, jnp.int32)


def _unpack_bf16_pairs(p):
    bits = pltpu.bitcast(p, jnp.uint32)
    return pltpu.bitcast(bits << 16, F32), pltpu.bitcast(bits & jnp.uint32(0xFFFF0000), F32)


def _pick_tile(n, pref):
    t = min(n, pref)
    assert n % t == 0, (n, t)
    return t


def _adaln_kernel(c_ref, w_ref, b_ref, o_ref):
    c = c_ref[...]
    s = c * _sigmoid(c)
    o_ref[...] = jnp.dot(s, w_ref[...], preferred_element_type=F32,
                         precision=lax.Precision.HIGHEST) + b_ref[...]


def _adaln(cond, ada_w, ada_b):
    depth, d, n = ada_w.shape
    tn = _pick_tile(n, 1536)
    return pl.pallas_call(
        _adaln_kernel,
        out_shape=jax.ShapeDtypeStruct((depth, cond.shape[0], n), F32),
        grid=(depth, n // tn),
        in_specs=[pl.BlockSpec(cond.shape, lambda l, j: (0, 0)),
                  pl.BlockSpec((None, d, tn), lambda l, j: (l, 0, j)),
                  pl.BlockSpec((None, 1, tn), lambda l, j: (l, 0, j))],
        out_specs=pl.BlockSpec((None, cond.shape[0], tn), lambda l, j: (l, 0, j)),
        compiler_params=_params("parallel", "parallel"),
        name="adaln",
    )(cond, ada_w, ada_b.reshape(depth, 1, n))


def _modulated_rms(x, nw, sc, sh):
    ms = jnp.mean(x * x, axis=-1, keepdims=True)
    return (x * lax.rsqrt(ms + EPS) * nw) * (1.0 + sc) + sh


def _head_group_matrix():
    r = lax.broadcasted_iota(jnp.int32, (2 * LANES, 2 * LANES), 0) // HEAD_DIM
    c = lax.broadcasted_iota(jnp.int32, (2 * LANES, 2 * LANES), 1) // HEAD_DIM
    return jnp.where(r == c, 1.0, 0.0).astype(BF16)


def _per_head_sum(v, grp):
    hi = v.astype(BF16)
    lo = (v - hi.astype(F32)).astype(BF16)
    r = jnp.dot(jnp.concatenate([hi, lo], axis=1), grp, preferred_element_type=F32)
    return r[:, 0:LANES] + r[:, LANES:]


def _dup_halves(t):
    sw = pltpu.roll(t, HEAD_DIM, 1)
    lo = lax.broadcasted_iota(jnp.int32, t.shape, 1) < HEAD_DIM
    return jnp.where(lo, t, sw), jnp.where(lo, sw, t)


def _inproj_kernel(x_ref, nw_ref, sc_ref, sh_ref, w_ref, cos_ref, sa_ref, sb_ref, qw_ref, kw_ref,
                   q_ref, k_ref, v_ref, rk_ref, rv_ref, rq_ref, rg_ref, cv_ref):
    hb = _modulated_rms(x_ref[...], nw_ref[...], sc_ref[...], sh_ref[...]).astype(BF16)

    def proj(c0, n):
        return jnp.dot(hb, w_ref[:, c0:c0 + n], preferred_element_type=F32)

    grp = _head_group_matrix()
    cos, sa, sb = cos_ref[...], sa_ref[...], sb_ref[...]

    def norm_rope(p, wrow):
        y = p * lax.rsqrt(_per_head_sum(p * p, grp) * (1.0 / HEAD_DIM) + EPS) * wrow
        return (y * cos + pltpu.roll(y, LANES - ROPE_FREQS, 1) * sa
                + pltpu.roll(y, ROPE_FREQS, 1) * sb)

    kv = proj(C_ATT_K, 2 * ATT_KV_W)
    k0, k1 = _dup_halves(norm_rope(kv[:, 0:ATT_KV_W], kw_ref[...]))
    k_ref[:, 0:LANES] = k0.astype(BF16)
    k_ref[:, LANES:2 * LANES] = k1.astype(BF16)
    v_ref[...] = kv[:, ATT_KV_W:].T.astype(BF16)
    qw = qw_ref[...] * ATT_SCALE
    qall = proj(C_ATT_Q, D_ATT)
    for j in range(D_ATT // LANES):
        q_ref[:, j * LANES:(j + 1) * LANES] = norm_rope(qall[:, j * LANES:(j + 1) * LANES], qw).astype(BF16)
    rk_ref[...] = (proj(C_RET_K, D_RET) * RET_K_SCALE).astype(BF16)
    rv_ref[...] = proj(C_RET_V, D_RET).astype(BF16)
    rq_ref[...] = proj(C_RET_Q, D_RET).astype(BF16)
    rg_ref[...] = proj(C_RET_G, D_RET)
    cv_ref[...] = proj(C_CONV, CONV_CH) * _sigmoid(proj(C_CONV + CONV_CH, CONV_CH))


def _inproj(x, nw, sc, sh, w_bf, cos, sa, sb, qw, kw):
    b, s, d = x.shape
    tm = _pick_tile(s, 512)
    row = lambda n: pl.BlockSpec((1, n), lambda bi, i: (0, 0))
    per_b = pl.BlockSpec((None, 1, d), lambda bi, i: (bi, 0, 0))
    tab = pl.BlockSpec((tm, LANES), lambda bi, i: (i, 0))
    tok = lambda n: pl.BlockSpec((None, tm, n), lambda bi, i: (bi, i, 0))
    shp = lambda n, dt: jax.ShapeDtypeStruct((b, s, n), dt)
    return pl.pallas_call(
        _inproj_kernel,
        out_shape=(shp(D_ATT, BF16), shp(2 * LANES, BF16), jax.ShapeDtypeStruct((b, ATT_KV_W, s), BF16),
                   shp(D_RET, BF16), shp(D_RET, BF16), shp(D_RET, BF16), shp(D_RET, F32),
                   shp(CONV_CH, F32)),
        grid=(b, s // tm),
        in_specs=[tok(d), row(d), per_b, per_b,
                  pl.BlockSpec(w_bf.shape, lambda bi, i: (0, 0)),
                  tab, tab, tab, row(LANES), row(LANES)],
        out_specs=(tok(D_ATT), tok(2 * LANES), pl.BlockSpec((None, ATT_KV_W, tm), lambda bi, i: (bi, 0, i)),
                   tok(D_RET), tok(D_RET), tok(D_RET), tok(D_RET), tok(CONV_CH)),
        compiler_params=_params("parallel", "parallel"),
        name="inproj",
    )(x, nw, sc, sh, w_bf, cos, sa, sb, qw, kw)


def _attn_kernel(*refs, window):
    if window:
        q_ref, kp_ref, kc_ref, kn_ref, vp_ref, vc_ref, vn_ref, kx_ref, vx_ref, sink_ref, o_ref = refs
    else:
        q_ref, kx_ref, vx_ref, sink_ref, o_ref = refs
    blk = ATT_BLOCK
    n_ctx = kx_ref.shape[0]
    nk = 3 * blk + n_ctx if window else n_ctx
    if window:
        i = pl.program_id(1)
        key = lax.broadcasted_iota(jnp.int32, (blk, 2 * blk), 0)
        qry = lax.broadcasted_iota(jnp.int32, (blk, 2 * blk), 1) & (blk - 1)
        mask_prev = (key >= qry) & (i > 0)
        mask_next = (key <= qry) & (i < pl.num_programs(1) - 1)
    first_head = lax.broadcasted_iota(jnp.int32, (blk, LANES), 1) < HEAD_DIM
    ones = jnp.ones((2 * SUBLANES, nk), BF16)
    for g in range(ATT_KV_HEADS):
        gs = slice(g * LANES, (g + 1) * LANES)
        vs = slice(g * HEAD_DIM, (g + 1) * HEAD_DIM)
        if window:
            kcat = jnp.concatenate([kp_ref[:, gs], kc_ref[:, gs], kn_ref[:, gs], kx_ref[:, gs]], axis=0)
            vt = jnp.concatenate([vp_ref[vs, :], vc_ref[vs, :], vn_ref[vs, :], vx_ref[vs, :]], axis=1)
        else:
            kcat, vt = kx_ref[:, gs], vx_ref[vs, :]
        vaug = jnp.concatenate([vt, ones], axis=0)
        for pr in range(2):
            pair = 2 * g + pr
            cs = slice(pair * LANES, (pair + 1) * LANES)
            qp = q_ref[:, cs]
            zero = jnp.zeros_like(qp)
            w = jnp.concatenate([jnp.where(first_head, qp, zero), jnp.where(first_head, zero, qp)], axis=0)
            s = lax.dot_general(kcat, w, (((1,), (1,)), ((), ())), preferred_element_type=F32)
            if window:
                parts = [jnp.where(mask_prev, s[0:blk], NEG_INF), s[blk:2 * blk],
                         jnp.where(mask_next, s[2 * blk:3 * blk], NEG_INF), s[3 * blk:]]
            else:
                parts = [s]
            snk = sink_ref[pair:pair + 1, :]
            m = snk
            for part in parts:
                m = jnp.maximum(m, jnp.max(part, axis=0, keepdims=True))
            p = jnp.concatenate([jnp.exp(part - m).astype(BF16) for part in parts], axis=0)
            o = jnp.dot(vaug, p, preferred_element_type=F32)
            den = o[HEAD_DIM:HEAD_DIM + 1, :] + jnp.exp(snk - m)
            on = o[0:HEAD_DIM, :] * (1.0 / den)
            ot = jnp.concatenate([on[:, 0:blk], on[:, blk:2 * blk]], axis=0)
            o_ref[:, cs] = ot.T.astype(BF16)


def _attention(q, k, vt, kx, vxt, sink_tab, window):
    b, s, _ = q.shape
    blk = ATT_BLOCK
    nb = s // blk
    n_ctx = kx.shape[1]
    qspec = pl.BlockSpec((None, blk, D_ATT), lambda bi, i: (bi, i, 0))
    kctx = pl.BlockSpec((None, n_ctx, 2 * LANES), lambda bi, i: (bi, 0, 0))
    vctx = pl.BlockSpec((None, ATT_KV_W, n_ctx), lambda bi, i: (bi, 0, 0))
    snk = pl.BlockSpec(sink_tab.shape, lambda bi, i: (0, 0))
    if window:
        prev = lambda i: jnp.maximum(i - 1, 0)
        nxt = lambda i: jnp.minimum(i + 1, nb - 1)
        same = lambda i: i
        kspec = lambda f: pl.BlockSpec((None, blk, 2 * LANES), lambda bi, i: (bi, f(i), 0))
        vspec = lambda f: pl.BlockSpec((None, ATT_KV_W, blk), lambda bi, i: (bi, 0, f(i)))
        in_specs = [qspec, kspec(prev), kspec(same), kspec(nxt), vspec(prev), vspec(same), vspec(nxt),
                    kctx, vctx, snk]
        args = (q, k, k, k, vt, vt, vt, kx, vxt, sink_tab)
    else:
        in_specs = [qspec, kctx, vctx, snk]
        args = (q, kx, vxt, sink_tab)
    return pl.pallas_call(
        functools.partial(_attn_kernel, window=window),
        out_shape=jax.ShapeDtypeStruct((b, s, D_ATT), BF16),
        grid=(b, nb),
        in_specs=in_specs,
        out_specs=qspec,
        compiler_params=_params("parallel", "parallel"),
        name="attention_window" if window else "attention_ctx",
    )(*args)


def _ret_state_kernel(lgl_ref, k_ref, v_ref, s0_ref, r_ref, fin_ref, s_scr, *, cb, reverse):
    c = pl.program_id(1)
    L = RET_CHUNK

    @pl.when(c == 0)
    def _():
        s_scr[...] = s0_ref[...]

    jj = lax.broadcasted_iota(jnp.int32, (L, LANES), 0).astype(F32)
    expo = jj if reverse else (L - 1.0) - jj
    same_head = (lax.broadcasted_iota(jnp.int32, (LANES, LANES), 0) // HEAD_DIM
                 == lax.broadcasted_iota(jnp.int32, (LANES, LANES), 1) // HEAD_DIM)
    for pr in range(RET_HEADS // 2):
        cs = slice(pr * LANES, (pr + 1) * LANES)
        lgl = lgl_ref[pr]
        kdec = jnp.exp(expo * lgl)
        cdec = jnp.exp(float(L) * lgl)
        state = s_scr[pr]
        for t in range(cb):
            cc = cb - 1 - t if reverse else t
            rows = slice(cc * L, (cc + 1) * L)
            r_ref[cc, pr] = state
            kd = k_ref[rows, cs].astype(F32) * kdec
            u = jnp.dot(kd.T.astype(BF16), v_ref[rows, cs], preferred_element_type=F32)
            state = cdec * state + jnp.where(same_head, u, 0.0)
        s_scr[pr] = state

    @pl.when(c == pl.num_programs(1) - 1)
    def _():
        fin_ref[...] = s_scr[...]


def _ret_states(lgl, rk, rv, s0, reverse):
    b, s, _ = rk.shape
    nc = s // RET_CHUNK
    cb = _pick_tile(nc, 8)
    nblk = nc // cb
    npair = RET_HEADS // 2
    blk_idx = (lambda c: nblk - 1 - c) if reverse else (lambda c: c)
    tok = pl.BlockSpec((None, cb * RET_CHUNK, D_RET), lambda bi, c: (bi, blk_idx(c), 0))
    st = pl.BlockSpec((None, npair, LANES, LANES), lambda bi, c: (bi, 0, 0, 0))
    return pl.pallas_call(
        functools.partial(_ret_state_kernel, cb=cb, reverse=reverse),
        out_shape=(jax.ShapeDtypeStruct((b, nc, npair, LANES, LANES), F32),
                   jax.ShapeDtypeStruct((b, npair, LANES, LANES), F32)),
        grid=(b, nblk),
        in_specs=[pl.BlockSpec(lgl.shape, lambda bi, c: (0, 0, 0)), tok, tok, st],
        out_specs=(pl.BlockSpec((None, cb, npair, LANES, LANES),
                                lambda bi, c: (bi, blk_idx(c), 0, 0, 0)), st),
        scratch_shapes=[pltpu.VMEM((npair, LANES, LANES), F32)],
        compiler_params=_params("parallel", "arbitrary"),
        name="ret_state_bwd" if reverse else "ret_state_fwd",
    )(lgl, rk, rv, s0)


def _ret_out_kernel(lgf_ref, lgb_ref, q_ref, k_ref, v_ref, g_ref, rf_ref, rb_ref, gnw_ref, o_ref, *, cb):
    L = RET_CHUNK
    i0 = lax.broadcasted_iota(jnp.int32, (L, LANES), 0).astype(F32)
    i1 = lax.broadcasted_iota(jnp.int32, (L, LANES), 1).astype(F32)
    diff = i0 - i1
    lo = lax.broadcasted_iota(jnp.int32, (L, LANES), 1) < HEAD_DIM
    grp = _head_group_matrix()
    inv = 1.0 / HEAD_DIM
    tabs = []
    for pr in range(RET_HEADS // 2):
        lgf, lgb = lgf_ref[pr], lgb_ref[pr]
        dmat = [jnp.where(diff >= 0.0,
                          jnp.exp(jnp.maximum(diff, 0.0) * lgf[:, a:a + 1]),
                          jnp.exp(jnp.maximum(-diff, 0.0) * lgb[:, a:a + 1]))
                for a in (0, HEAD_DIM)]
        tabs.append((jnp.exp((i0 + 1.0) * lgf), jnp.exp((float(L) - i0) * lgb),
                     jnp.concatenate(dmat, axis=0), gnw_ref[:, pr * LANES:(pr + 1) * LANES]))

    def chunk(t, carry):
        rows = pl.ds(pl.multiple_of(t * L, L), L)
        for pr in range(RET_HEADS // 2):
            cs = slice(pr * LANES, (pr + 1) * LANES)
            xif, xib, dmat, gnw = tabs[pr]
            qp, kp, vp = q_ref[rows, cs], k_ref[rows, cs], v_ref[rows, cs]
            zero = jnp.zeros_like(kp)
            qz = jnp.concatenate([jnp.where(lo, qp, zero), jnp.where(lo, zero, qp)], axis=0)
            a = lax.dot_general(qz, kp, (((1,), (1,)), ((), ())), preferred_element_type=F32) * dmat
            qf = qp.astype(F32)
            lhs = jnp.concatenate([a[0:L].astype(BF16), a[L:].astype(BF16),
                                   (qf * xif).astype(BF16), (qf * xib).astype(BF16)], axis=1)
            rhs = jnp.concatenate([jnp.where(lo, vp, zero), jnp.where(lo, zero, vp),
                                   rf_ref[t, pr].astype(BF16), rb_ref[t, pr].astype(BF16)], axis=0)
            acc = jnp.dot(lhs, rhs, preferred_element_type=F32)
            dl = acc - _per_head_sum(acc, grp) * inv
            var = _per_head_sum(dl * dl, grp) * inv
            y = dl * lax.rsqrt(var + EPS) * gnw
            gt = g_ref[rows, cs]
            o_ref[rows, cs] = (gt * _sigmoid(gt) * y).astype(BF16)
        return carry

    lax.fori_loop(0, cb, chunk, 0, unroll=4)


def _ret_out(lgf, lgb, rq, rk, rv, rg, rf, rb, gnw):
    b, s, _ = rq.shape
    nc = s // RET_CHUNK
    cb = _pick_tile(nc, 8)
    npair = RET_HEADS // 2
    tok = pl.BlockSpec((None, cb * RET_CHUNK, D_RET), lambda bi, c: (bi, c, 0))
    st = pl.BlockSpec((None, cb, npair, LANES, LANES), lambda bi, c: (bi, c, 0, 0, 0))
    lg = pl.BlockSpec(lgf.shape, lambda bi, c: (0, 0, 0))
    return pl.pallas_call(
        functools.partial(_ret_out_kernel, cb=cb),
        out_shape=jax.ShapeDtypeStruct((b, s, D_RET), BF16),
        grid=(b, nc // cb),
        in_specs=[lg, lg, tok, tok, tok, tok, st, st, pl.BlockSpec((1, D_RET), lambda bi, c: (0, 0))],
        out_specs=tok,
        compiler_params=_params("parallel", "parallel"),
        name="ret_out",
    )(lgf, lgb, rq, rk, rv, rg, rf, rb, gnw)


def _conv_kernel(prev_ref, cur_ref, next_ref, w_ref, b_ref, lnw_ref, lnb_ref, o_ref, xpad, shifted, hbuf):
    i = pl.program_id(1)
    tm = cur_ref.shape[0]
    halo = CONV_HALO
    first = i == 0
    last = i == pl.num_programs(1) - 1
    xpad[0:halo, :] = jnp.where(first, 0.0, prev_ref[...])
    xpad[halo:halo + tm, :] = cur_ref[...]
    xpad[halo + tm:2 * halo + tm, :] = jnp.where(last, 0.0, next_ref[...])
    span = tm + 2 * halo - SUBLANES
    for r in range(SUBLANES):
        shifted[r] = xpad[r:r + span, :]
    base = halo - CONV_WIDTH // 2
    bias = b_ref[...]

    def sub(sb, carry):
        r0 = pl.multiple_of(sb * CONV_SUB, CONV_SUB)
        chains = [None] * CONV_CHAINS
        for w in range(CONV_WIDTH):
            off = base + w
            xs = shifted[off % SUBLANES, pl.ds(r0 + (off // SUBLANES) * SUBLANES, CONV_SUB), :]
            term = xs * w_ref[w:w + 1, :]
            c = w % CONV_CHAINS
            chains[c] = term if chains[c] is None else chains[c] + term
        hbuf[pl.ds(r0, CONV_SUB), :] = (chains[0] + chains[1]) + (chains[2] + chains[3]) + bias
        return carry

    lax.fori_loop(0, tm // CONV_SUB, sub, 0, unroll=2)
    h = hbuf[...]
    mu = jnp.mean(h, axis=-1, keepdims=True)
    dl = h - mu
    var = jnp.mean(dl * dl, axis=-1, keepdims=True)
    y = dl * lax.rsqrt(var + EPS) * lnw_ref[...] + lnb_ref[...]
    o_ref[...] = (y * _sigmoid(y)).astype(BF16)


def _conv(cv, w, bias, lnw, lnb):
    b, s, ch = cv.shape
    tm = _pick_tile(s, 512)
    hpb = tm // CONV_HALO
    nh = s // CONV_HALO
    row = pl.BlockSpec((1, ch), lambda bi, i: (0, 0))
    span = tm + 2 * CONV_HALO - SUBLANES
    return pl.pallas_call(
        _conv_kernel,
        out_shape=jax.ShapeDtypeStruct((b, s, ch), BF16),
        grid=(b, s // tm),
        in_specs=[pl.BlockSpec((None, CONV_HALO, ch), lambda bi, i: (bi, jnp.maximum(i * hpb - 1, 0), 0)),
                  pl.BlockSpec((None, tm, ch), lambda bi, i: (bi, i, 0)),
                  pl.BlockSpec((None, CONV_HALO, ch),
                               lambda bi, i: (bi, jnp.minimum((i + 1) * hpb, nh - 1), 0)),
                  pl.BlockSpec(w.shape, lambda bi, i: (0, 0)), row, row, row],
        out_specs=pl.BlockSpec((None, tm, ch), lambda bi, i: (bi, i, 0)),
        scratch_shapes=[pltpu.VMEM((tm + 2 * CONV_HALO, ch), F32),
                        pltpu.VMEM((SUBLANES, span, ch), F32),
                        pltpu.VMEM((tm, ch), F32)],
        compiler_params=_params("parallel", "parallel"),
        name="conv",
    )(cv, cv, cv, w, bias, lnw, lnb)


def _outproj_kernel(*refs, route):
    if route:
        (att_ref, ret_ref, cnv_ref, w_ref, x_ref, g1_ref, nw_ref, sc_ref, sh_ref, rcat_ref,
         xo_ref, h_ref, rt_ref) = refs
    else:
        att_ref, ret_ref, cnv_ref, w_ref, x_ref, g1_ref, nw_ref, sc_ref, sh_ref, xo_ref, h_ref = refs
    y = (jnp.dot(att_ref[...], w_ref[0:D_ATT, :], preferred_element_type=F32)
         + jnp.dot(ret_ref[...], w_ref[D_ATT:D_ATT + D_RET, :], preferred_element_type=F32)
         + jnp.dot(cnv_ref[...], w_ref[D_ATT + D_RET:, :], preferred_element_type=F32))
    xn = x_ref[...] + g1_ref[...] * y
    xo_ref[...] = xn
    h = _modulated_rms(xn, nw_ref[...], sc_ref[...], sh_ref[...])
    h_ref[...] = _pack_bf16_pairs(h) if route else h.astype(BF16)
    if route:
        hi = h.astype(BF16)
        lo = (h - hi.astype(F32)).astype(BF16)
        tm = h.shape[0]
        r = jnp.dot(jnp.concatenate([hi, lo], axis=0), rcat_ref[...], preferred_element_type=F32)
        logits = (r[0:tm, 0:LANES] + r[0:tm, LANES:]) + (r[tm:, 0:LANES] + r[tm:, LANES:])
        lane = lax.broadcasted_iota(jnp.int32, logits.shape, 1).astype(F32)
        logits = jnp.where(lane < N_EXPERTS, logits, NEG_INF)
        m1 = jnp.max(logits, axis=-1, keepdims=True)
        i1 = jnp.min(jnp.where(logits == m1, lane, float(LANES)), axis=-1, keepdims=True)
        rest = jnp.where(lane == i1, NEG_INF, logits)
        m2 = jnp.max(rest, axis=-1, keepdims=True)
        i2 = jnp.min(jnp.where(rest == m2, lane, float(LANES)), axis=-1, keepdims=True)
        e2 = jnp.exp(m2 - m1)
        w1 = 1.0 / (1.0 + e2)
        w2 = e2 / (1.0 + e2)
        rt_ref[...] = jnp.where(lane == 0.0, i1,
                                jnp.where(lane == 1.0, i2,
                                          jnp.where(lane == 2.0, w1, jnp.where(lane == 3.0, w2, 0.0))))


def _outproj(att, ret, cnv, w_bf, x, g1, nw, sc, sh, router=None):
    b, s, d = x.shape
    tm = _pick_tile(s, 512)
    tok = lambda n: pl.BlockSpec((None, tm, n), lambda bi, i: (bi, i, 0))
    per_b = pl.BlockSpec((None, 1, d), lambda bi, i: (bi, 0, 0))
    row = pl.BlockSpec((1, d), lambda bi, i: (0, 0))
    in_specs = [tok(D_ATT), tok(D_RET), tok(CONV_CH), pl.BlockSpec(w_bf.shape, lambda bi, i: (0, 0)),
                tok(d), per_b, row, per_b, per_b]
    args = [att, ret, cnv, w_bf, x, g1, nw, sc, sh]
    h_shape = (b, s, d // 2) if router is not None else (b, s, d)
    out_shape = [jax.ShapeDtypeStruct((b, s, d), F32),
                 jax.ShapeDtypeStruct(h_shape, jnp.int32 if router is not None else BF16)]
    out_specs = [tok(d), tok(h_shape[2])]
    if router is not None:
        rpad = jnp.zeros((d, LANES), F32).at[:, :N_EXPERTS].set(router)
        rhi = rpad.astype(BF16)
        rlo = (rpad - rhi.astype(F32)).astype(BF16)
        in_specs += [pl.BlockSpec((d, 2 * LANES), lambda bi, i: (0, 0))]
        args += [jnp.concatenate([rhi, rlo], axis=1)]
        out_shape.append(jax.ShapeDtypeStruct((b, s, LANES), F32))
        out_specs.append(tok(LANES))
    return pl.pallas_call(
        functools.partial(_outproj_kernel, route=router is not None),
        out_shape=tuple(out_shape),
        grid=(b, s // tm),
        in_specs=in_specs,
        out_specs=tuple(out_specs),
        compiler_params=_params("parallel", "parallel"),
        name="outproj_route" if router is not None else "outproj",
    )(*args)


def _swiglu_chunks(h, wg_ref, wu_ref, wd_ref, sub):
    total = None
    for c0 in range(0, wg_ref.shape[1], sub):
        cs = slice(c0, min(c0 + sub, wg_ref.shape[1]))
        gate = jnp.dot(h, wg_ref[:, cs].astype(BF16), preferred_element_type=F32)
        up = jnp.dot(h, wu_ref[:, cs].astype(BF16), preferred_element_type=F32)
        act = (gate * _sigmoid(gate) * up).astype(BF16)
        part = jnp.dot(act, wd_ref[cs, :].astype(BF16), preferred_element_type=F32)
        total = part if total is None else total + part
    return total


def _ffn_kernel(h_ref, wg_ref, wu_ref, wd_ref, x_ref, g2_ref, o_ref):
    y = _swiglu_chunks(h_ref[...], wg_ref, wu_ref, wd_ref, FFN_SUB)
    o_ref[...] = x_ref[...] + g2_ref[...] * y


def _ffn(h, wg, wu, wd, x, g2):
    b, s, d = x.shape
    f = wg.shape[1]
    tm = _pick_tile(s, 512)
    tok = lambda: pl.BlockSpec((None, tm, d), lambda bi, i: (bi, i, 0))
    res = lambda shape: pl.BlockSpec(shape, lambda bi, i: (0, 0), pipeline_mode=pl.Buffered(1))
    return pl.pallas_call(
        _ffn_kernel,
        out_shape=jax.ShapeDtypeStruct((b, s, d), F32),
        grid=(b, s // tm),
        in_specs=[tok(), res((d, f)), res((d, f)), res((f, d)), tok(),
                  pl.BlockSpec((None, 1, d), lambda bi, i: (bi, 0, 0))],
        out_specs=tok(),
        compiler_params=_params("parallel", "parallel"),
        name="ffn_dense",
    )(h, wg, wu, wd, x, g2)


def _gather_rows(idx, src):
    n = idx.shape[0]
    d = src.shape[1]
    mesh = plsc.VectorSubcoreMesh(core_axis_name="core", subcore_axis_name="subcore")

    @pl.kernel(out_type=jax.ShapeDtypeStruct((n, d), src.dtype), mesh=mesh, name="moe_gather")
    def gather(src_hbm, idx_hbm, out_hbm):
        def body(idx_vmem, out_vmem):
            pltpu.sync_copy(src_hbm.at[idx_vmem.at[0, pl.ds(0, SC_WINDOW)]], out_vmem)

        pltpu.emit_pipeline(
            body,
            grid=(n // SC_WINDOW,),
            in_specs=[pl.BlockSpec((1, LANES), lambda i: (i, 0))],
            out_specs=[pl.BlockSpec((SC_WINDOW, d), lambda i: (i, 0))],
            core_axis_name=("core", "subcore"),
            dimension_semantics=(pltpu.PARALLEL,),
        )(idx_hbm, out_hbm)

    idx_rows = jnp.pad(idx.reshape(n // SC_WINDOW, SC_WINDOW), ((0, 0), (0, LANES - SC_WINDOW)))
    return gather(src, idx_rows)


def _invert_rows(dest, n_rows):
    n = dest.shape[0]
    n_tok = n // 2
    mesh = plsc.VectorSubcoreMesh(core_axis_name="core", subcore_axis_name="subcore")

    @pl.kernel(out_type=jax.ShapeDtypeStruct((n_rows,), jnp.int32), mesh=mesh, name="moe_row_src",
               scratch_types=[pltpu.VMEM((n,), jnp.int32), pltpu.VMEM((n_rows,), jnp.int32)],
               compiler_params=pltpu.CompilerParams(needs_layout_passes=False))
    def invert(dest_hbm, out_hbm, dest_vmem, rows_vmem):
        @pl.when((lax.axis_index("core") == 0) & (lax.axis_index("subcore") == 0))
        def _():
            pltpu.sync_copy(dest_hbm, dest_vmem)

            @pl.loop(0, n_rows, step=SC_LANES)
            def _(r):
                rows_vmem[pl.ds(r, SC_LANES)] = jnp.zeros((SC_LANES,), jnp.int32)

            @pl.loop(0, n, step=SC_LANES)
            def _(a):
                tok = lax.iota(jnp.int32, SC_LANES) + a
                tok = jnp.where(tok >= n_tok, tok - n_tok, tok)
                plsc.store_scatter(rows_vmem, [dest_vmem[pl.ds(a, SC_LANES)]], tok)

            pltpu.sync_copy(rows_vmem, out_hbm)

    return invert(dest)


def _moe_kernel(te_ref, nu_ref, x_ref, wg_ref, wu_ref, wd_ref, *rest, tile0):
    o_ref, xb, acc = rest[-3:]
    t = pl.program_id(0) + tile0
    j = pl.program_id(1)
    nj = pl.num_programs(1)
    used = t < nu_ref[0]

    @pl.when(used & (j == 0))
    def _():
        lo, hi = _unpack_bf16_pairs(x_ref[...])
        half = lo.shape[1]
        xb[:, 0:half] = lo.astype(BF16)
        xb[:, half:] = hi.astype(BF16)
        acc[...] = jnp.zeros_like(acc)

    @pl.when(used)
    def _():
        acc[...] += _swiglu_chunks(xb[...], wg_ref, wu_ref, wd_ref, MOE_SUB)

    @pl.when(used & (j == nj - 1))
    def _():
        o_ref[...] = _pack_bf16_pairs(acc[...])

    @pl.when(jnp.logical_not(used) & (j == nj - 1))
    def _():
        o_ref[...] = jnp.zeros_like(o_ref)


def _moe_grouped(tile_expert, n_used, xs, wg, wu, wd, y_prev, tile0, n_rows):
    p, dp = xs.shape
    d = 2 * dp
    f = wg.shape[2]
    tm = MOE_TM
    tf = MOE_TF
    nj = f // tf

    def jj(t, j, te, nu):
        return jnp.where(t + tile0 < nu[0], j, nj - 1)

    in_specs = [pl.BlockSpec((tm, dp), lambda t, j, te, nu: (t, 0)),
                pl.BlockSpec((None, d, tf), lambda t, j, te, nu: (te[t + tile0], 0, jj(t, j, te, nu))),
                pl.BlockSpec((None, d, tf), lambda t, j, te, nu: (te[t + tile0], 0, jj(t, j, te, nu))),
                pl.BlockSpec((None, tf, d), lambda t, j, te, nu: (te[t + tile0], jj(t, j, te, nu), 0))]
    args = [tile_expert, n_used, xs, wg, wu, wd]
    aliases = {}
    if y_prev is not None:
        in_specs.append(pl.BlockSpec(memory_space=pl.ANY))
        args.append(y_prev)
        aliases = {len(args) - 1: 0}
    return pl.pallas_call(
        functools.partial(_moe_kernel, tile0=tile0),
        out_shape=jax.ShapeDtypeStruct((n_rows, dp), jnp.int32),
        grid_spec=pltpu.PrefetchScalarGridSpec(
            num_scalar_prefetch=2,
            grid=(p // tm, nj),
            in_specs=in_specs,
            out_specs=pl.BlockSpec((tm, dp), lambda t, j, te, nu: (t + tile0, 0)),
            scratch_shapes=[pltpu.VMEM((tm, d), BF16), pltpu.VMEM((tm, d), F32)]),
        input_output_aliases=aliases,
        compiler_params=_params("arbitrary", "arbitrary"),
        name="moe_grouped",
    )(*args)


def _combine_kernel(y1_ref, y2_ref, x_ref, g2_ref, rt_ref, o_ref):
    rt = rt_ref[...]
    w1, w2 = rt[:, 2:3], rt[:, 3:4]
    half = y1_ref.shape[1]
    for k, (a, c) in enumerate(zip(_unpack_bf16_pairs(y1_ref[...]), _unpack_bf16_pairs(y2_ref[...]))):
        cs = slice(k * half, (k + 1) * half)
        o_ref[:, cs] = x_ref[:, cs] + g2_ref[:, cs] * (w1 * a + w2 * c)


def _moe_combine(yg, x, g2, route):
    b, s, d = x.shape
    tm = _pick_tile(s, 512)
    tok = lambda n: pl.BlockSpec((None, tm, n), lambda bi, i: (bi, i, 0))
    return pl.pallas_call(
        _combine_kernel,
        out_shape=jax.ShapeDtypeStruct((b, s, d), F32),
        grid=(b, s // tm),
        in_specs=[tok(d // 2), pl.BlockSpec((None, tm, d // 2), lambda bi, i: (b + bi, i, 0)), tok(d),
                  pl.BlockSpec((None, 1, d), lambda bi, i: (bi, 0, 0)), tok(LANES)],
        out_specs=tok(d),
        compiler_params=_params("parallel", "parallel"),
        name="moe_combine",
    )(yg, yg, x, g2, route)


def _moe(h, route, x, g2, wg, wu, wd):
    b, s, d = x.shape
    n_tok = b * s
    tm = MOE_TM
    rt = route.reshape(n_tok, LANES)
    flat_e = jnp.concatenate([rt[:, 0], rt[:, 1]]).astype(jnp.int32)
    onehot = (flat_e[:, None] == jnp.arange(N_EXPERTS, dtype=jnp.int32)[None, :]).astype(jnp.int32)
    csum = jnp.cumsum(onehot, axis=0)
    rank = jnp.sum(csum * onehot, axis=1) - 1
    counts = csum[-1]
    tiles_e = (counts + tm - 1) // tm
    tiles_cum = jnp.cumsum(tiles_e)
    row_start = (tiles_cum - tiles_e) * tm
    dest = jnp.sum(onehot * row_start[None, :], axis=1) + rank
    n_tiles = 2 * n_tok // tm + N_EXPERTS
    tile_ids = jnp.arange(n_tiles, dtype=jnp.int32)
    tile_expert = jnp.sum((tile_ids[:, None] >= tiles_cum[None, :]).astype(jnp.int32), axis=1)
    last_e = jnp.max(jnp.where(tiles_e > 0, jnp.arange(N_EXPERTS, dtype=jnp.int32), 0))
    tile_expert = jnp.minimum(tile_expert, last_e).astype(jnp.int32)
    n_used = tiles_cum[-1:].astype(jnp.int32)
    dest = dest.astype(jnp.int32)
    row_src = _invert_rows(dest, n_tiles * tm)
    n_chunks = max(c for c in range(1, MOE_CHUNKS + 1) if n_tiles % c == 0)
    tiles_c = n_tiles // n_chunks
    h_flat = h.reshape(n_tok, d // 2)
    xs = [_gather_rows(row_src[c * tiles_c * tm:(c + 1) * tiles_c * tm], h_flat) for c in range(n_chunks)]
    y = None
    for c in range(n_chunks):
        y = _moe_grouped(tile_expert, n_used, xs[c], wg, wu, wd, y, c * tiles_c, n_tiles * tm)
    yg = _gather_rows(dest, y).reshape(2 * b, s, d // 2)
    return _moe_combine(yg, x, g2, route)


def _rope_tables(n):
    rows = n // GRID_W
    r = jnp.repeat(jnp.arange(rows), GRID_W).astype(F32)
    col = jnp.tile(jnp.arange(GRID_W), rows).astype(F32)
    freqs = ROPE_BASE ** (-jnp.arange(ROPE_FREQS, dtype=F32) / ROPE_FREQS)
    ang = jnp.stack([r[:, None] * freqs, col[:, None] * freqs], axis=1)
    ang = jnp.repeat(ang[:, :, None, :], 2, axis=2).reshape(n, HEAD_DIM)
    ang = jnp.tile(ang, (1, LANES // HEAD_DIM))
    cos, sin = jnp.cos(ang), jnp.sin(ang)
    first_half = (jnp.arange(LANES) % (2 * ROPE_FREQS)) < ROPE_FREQS
    return cos, jnp.where(first_half, -sin, 0.0), jnp.where(first_half, 0.0, sin)


def _lane_rows(lg):
    return jnp.repeat(lg.astype(F32), HEAD_DIM).reshape(RET_HEADS // 2, 1, LANES)


def kernel(x, c, ctx, c_ctx, ada_w, ada_b, norm1_w, norm2_w, w_in, w_out, q_norm_w, k_norm_w,
           attn_sink, ret_decay_f, ret_decay_b, ret_gn_w, conv_w, conv_b, conv_ln_w, conv_ln_b,
           ffn_w_gate, ffn_w_up, ffn_w_down, router_w, moe_w_gate, moe_w_up, moe_w_down):
    b, n, d = x.shape
    n_ctx = ctx.shape[1]
    depth = ada_w.shape[0]
    cond = jnp.zeros((SUBLANES, d), F32).at[0:b].set(c).at[b].set(c_ctx)
    mods = _adaln(cond, ada_w, ada_b).reshape(depth, SUBLANES, 6, d)
    cos, sa, sb = _rope_tables(n)
    ones_c = jnp.ones((n_ctx, LANES), F32)
    zeros_c = jnp.zeros((n_ctx, LANES), F32)
    zero_state = jnp.zeros((b, RET_HEADS // 2, LANES, LANES), F32)
    row = lambda v: v.reshape(1, -1)
    moe_bf = (moe_w_gate.astype(BF16), moe_w_up.astype(BF16), moe_w_down.astype(BF16))
    x, moe_bf = lax.optimization_barrier((x, moe_bf))
    for l in range(depth):
        last = l == depth - 1
        m_lat = [mods[l, 0:b, k][:, None, :] for k in range(6)]
        m_ctx = [jnp.broadcast_to(mods[l, b, k][None, None, :], (b, 1, d)) for k in range(6)]
        w_in_bf = w_in[l].astype(BF16)
        w_out_bf = w_out[l].astype(BF16)
        qw = row(jnp.tile(q_norm_w[l], LANES // HEAD_DIM))
        kw = row(jnp.tile(k_norm_w[l], LANES // HEAD_DIM))
        lgf = _lane_rows(jax.nn.log_sigmoid(ret_decay_f[l].astype(F32)))
        lgb = _lane_rows(jax.nn.log_sigmoid(ret_decay_b[l].astype(F32)))
        sink_tab = jnp.repeat(attn_sink[l].astype(F32), ATT_BLOCK).reshape(ATT_Q_HEADS // 2, 2 * ATT_BLOCK)

        q, k, v, rk, rv, rq, rg, cv = _inproj(x, row(norm1_w[l]), m_lat[1], m_lat[0], w_in_bf,
                                               cos, sa, sb, qw, kw)
        qc, kc, vc, rkc, rvc, rqc, rgc, cvc = _inproj(ctx, row(norm1_w[l]), m_ctx[1], m_ctx[0], w_in_bf,
                                                       ones_c, zeros_c, zeros_c, qw, kw)
        rf_c, s_f = _ret_states(lgf, rkc, rvc, zero_state, reverse=False)
        rb_c, s_b = _ret_states(lgb, rkc, rvc, zero_state, reverse=True)
        rf, _ = _ret_states(lgf, rk, rv, s_f, reverse=False)
        rb, _ = _ret_states(lgb, rk, rv, s_b, reverse=True)

        att = _attention(q, k, v, kc, vc, sink_tab, window=True)
        ret = _ret_out(lgf, lgb, rq, rk, rv, rg, rf, rb, row(ret_gn_w[l]))
        cnv = _conv(cv, conv_w[l], row(conv_b[l]), row(conv_ln_w[l]), row(conv_ln_b[l]))

        if l % 2 == 0:
            i = l // 2
            wg, wu, wd = ffn_w_gate[i].astype(BF16), ffn_w_up[i].astype(BF16), ffn_w_down[i].astype(BF16)
            x_mid, h2 = _outproj(att, ret, cnv, w_out_bf, x, m_lat[2], row(norm2_w[l]), m_lat[4], m_lat[3])
            x_new = _ffn(h2, wg, wu, wd, x_mid, m_lat[5])
        else:
            i = l // 2
            wg, wu, wd = moe_bf[0][i], moe_bf[1][i], moe_bf[2][i]
            x_mid, h2, route = _outproj(att, ret, cnv, w_out_bf, x, m_lat[2], row(norm2_w[l]),
                                        m_lat[4], m_lat[3], router=router_w[i])
            x_new = _moe(h2, route, x_mid, m_lat[5], wg, wu, wd)

        if not last:
            att_c = _attention(qc, None, None, kc, vc, sink_tab, window=False)
            ret_c = _ret_out(lgf, lgb, rqc, rkc, rvc, rgc, rf_c, rb_c, row(ret_gn_w[l]))
            cnv_c = _conv(cvc, conv_w[l], row(conv_b[l]), row(conv_ln_w[l]), row(conv_ln_b[l]))
            if l % 2 == 0:
                c_mid, h2c = _outproj(att_c, ret_c, cnv_c, w_out_bf, ctx, m_ctx[2], row(norm2_w[l]),
                                      m_ctx[4], m_ctx[3])
                ctx = _ffn(h2c, wg, wu, wd, c_mid, m_ctx[5])
            else:
                c_mid, h2c, route_c = _outproj(att_c, ret_c, cnv_c, w_out_bf, ctx, m_ctx[2],
                                               row(norm2_w[l]), m_ctx[4], m_ctx[3],
                                               router=router_w[i])
                ctx = _moe(h2c, route_c, c_mid, m_ctx[5], wg, wu, wd)
        x = x_new
    return x
```

```python
import functools

import jax
import jax.numpy as jnp
from jax import lax
from jax.experimental import pallas as pl
from jax.experimental.pallas import tpu as pltpu
from jax.experimental.pallas import tpu_sc as plsc

F32 = jnp.float32
BF16 = jnp.bfloat16

GRID_W = 64
HEAD_DIM = 64
ATT_Q_HEADS = 8
ATT_KV_HEADS = 2
ATT_WINDOW = 128
ATT_BLOCK = 128
RET_HEADS = 4
RET_CHUNK = 128
RET_K_SCALE = HEAD_DIM ** -0.5
ATT_SCALE = HEAD_DIM ** -0.5
CONV_CH = 256
CONV_WIDTH = 31
ROPE_BASE = 10000.0
ROPE_FREQS = HEAD_DIM // 4
D_ATT = ATT_Q_HEADS * HEAD_DIM
D_RET = RET_HEADS * HEAD_DIM
ATT_KV_W = ATT_KV_HEADS * HEAD_DIM
C_ATT_K = 0
C_ATT_V = C_ATT_K + ATT_KV_W
C_RET_K = C_ATT_V + ATT_KV_W
C_RET_V = C_RET_K + D_RET
C_ATT_Q = C_RET_V + D_RET
C_RET_Q = C_ATT_Q + D_ATT
C_RET_G = C_RET_Q + D_RET
C_CONV = C_RET_G + D_RET
N_EXPERTS = 8
EPS = 1e-6
NEG_INF = -1e30

LANES = 128
SUBLANES = 8
VMEM_LIMIT = 48 * 1024 * 1024
CONV_HALO = 16
CONV_SUB = 16
CONV_CHAINS = 4
MOE_TM = 512
MOE_CHUNKS = 2
MOE_SUB = 512
FFN_SUB = 512
MOE_TF = 1792
SC_LANES = 16
SC_WINDOW = 64


def _params(*sem):
    return pltpu.CompilerParams(dimension_semantics=sem, vmem_limit_bytes=VMEM_LIMIT)


def _sigmoid(x):
    return 1.0 / (1.0 + jnp.exp(-x))


def _pack_bf16_pairs(v):
    c = v.shape[1] // 2
    bits = pltpu.bitcast(v.astype(BF16).astype(F32), jnp.uint32)
    packed = (bits[:, c:] & jnp.uint32(0xFFFF0000)) | (bits[:, :c] >> 16)
    return pltpu.bitcast(packed, jnp.int32)


def _unpack_bf16_pairs(p):
    bits = pltpu.bitcast(p, jnp.uint32)
    return pltpu.bitcast(bits << 16, F32), pltpu.bitcast(bits & jnp.uint32(0xFFFF0000), F32)


def _pick_tile(n, pref):
    t = min(n, pref)
    assert n % t == 0, (n, t)
    return t


def _adaln_kernel(c_ref, w_ref, b_ref, o_ref):
    c = c_ref[...]
    s = c * _sigmoid(c)
    o_ref[...] = jnp.dot(s, w_ref[...], preferred_element_type=F32,
                         precision=lax.Precision.HIGHEST) + b_ref[...]


def _adaln(cond, ada_w, ada_b):
    depth, d, n = ada_w.shape
    tn = _pick_tile(n, 1536)
    return pl.pallas_call(
        _adaln_kernel,
        out_shape=jax.ShapeDtypeStruct((depth, cond.shape[0], n), F32),
        grid=(depth, n // tn),
        in_specs=[pl.BlockSpec(cond.shape, lambda l, j: (0, 0)),
                  pl.BlockSpec((None, d, tn), lambda l, j: (l, 0, j)),
                  pl.BlockSpec((None, 1, tn), lambda l, j: (l, 0, j))],
        out_specs=pl.BlockSpec((None, cond.shape[0], tn), lambda l, j: (l, 0, j)),
        compiler_params=_params("parallel", "parallel"),
        name="adaln",
    )(cond, ada_w, ada_b.reshape(depth, 1, n))


def _modulated_rms(x, nw, sc, sh):
    ms = jnp.mean(x * x, axis=-1, keepdims=True)
    return (x * lax.rsqrt(ms + EPS) * nw) * (1.0 + sc) + sh


def _head_group_matrix():
    r = lax.broadcasted_iota(jnp.int32, (2 * LANES, 2 * LANES), 0) // HEAD_DIM
    c = lax.broadcasted_iota(jnp.int32, (2 * LANES, 2 * LANES), 1) // HEAD_DIM
    return jnp.where(r == c, 1.0, 0.0).astype(BF16)


def _per_head_sum(v, grp):
    hi = v.astype(BF16)
    lo = (v - hi.astype(F32)).astype(BF16)
    r = jnp.dot(jnp.concatenate([hi, lo], axis=1), grp, preferred_element_type=F32)
    return r[:, 0:LANES] + r[:, LANES:]


def _dup_halves(t):
    sw = pltpu.roll(t, HEAD_DIM, 1)
    lo = lax.broadcasted_iota(jnp.int32, t.shape, 1) < HEAD_DIM
    return jnp.where(lo, t, sw), jnp.where(lo, sw, t)


def _inproj_kernel(x_ref, nw_ref, sc_ref, sh_ref, w_ref, cos_ref, sa_ref, sb_ref, qw_ref, kw_ref,
                   q_ref, k_ref, v_ref, rk_ref, rv_ref, rq_ref, rg_ref, cv_ref):
    hb = _modulated_rms(x_ref[...], nw_ref[...], sc_ref[...], sh_ref[...]).astype(BF16)

    def proj(c0, n):
        return jnp.dot(hb, w_ref[:, c0:c0 + n], preferred_element_type=F32)

    grp = _head_group_matrix()
    cos, sa, sb = cos_ref[...], sa_ref[...], sb_ref[...]

    def norm_rope(p, wrow):
        y = p * lax.rsqrt(_per_head_sum(p * p, grp) * (1.0 / HEAD_DIM) + EPS) * wrow
        return (y * cos + pltpu.roll(y, LANES - ROPE_FREQS, 1) * sa
                + pltpu.roll(y, ROPE_FREQS, 1) * sb)

    kv = proj(C_ATT_K, 2 * ATT_KV_W)
    k0, k1 = _dup_halves(norm_rope(kv[:, 0:ATT_KV_W], kw_ref[...]))
    k_ref[:, 0:LANES] = k0.astype(BF16)
    k_ref[:, LANES:2 * LANES] = k1.astype(BF16)
    v_ref[...] = kv[:, ATT_KV_W:].T.astype(BF16)
    qw = qw_ref[...] * ATT_SCALE
    qall = proj(C_ATT_Q, D_ATT)
    for j in range(D_ATT // LANES):
        q_ref[:, j * LANES:(j + 1) * LANES] = norm_rope(qall[:, j * LANES:(j + 1) * LANES], qw).astype(BF16)
    rk_ref[...] = (proj(C_RET_K, D_RET) * RET_K_SCALE).astype(BF16)
    rv_ref[...] = proj(C_RET_V, D_RET).astype(BF16)
    rq_ref[...] = proj(C_RET_Q, D_RET).astype(BF16)
    rg_ref[...] = proj(C_RET_G, D_RET)
    cv_ref[...] = proj(C_CONV, CONV_CH) * _sigmoid(proj(C_CONV + CONV_CH, CONV_CH))


def _inproj(x, nw, sc, sh, w_bf, cos, sa, sb, qw, kw):
    b, s, d = x.shape
    tm = _pick_tile(s, 512)
    row = lambda n: pl.BlockSpec((1, n), lambda bi, i: (0, 0))
    per_b = pl.BlockSpec((None, 1, d), lambda bi, i: (bi, 0, 0))
    tab = pl.BlockSpec((tm, LANES), lambda bi, i: (i, 0))
    tok = lambda n: pl.BlockSpec((None, tm, n), lambda bi, i: (bi, i, 0))
    shp = lambda n, dt: jax.ShapeDtypeStruct((b, s, n), dt)
    return pl.pallas_call(
        _inproj_kernel,
        out_shape=(shp(D_ATT, BF16), shp(2 * LANES, BF16), jax.ShapeDtypeStruct((b, ATT_KV_W, s), BF16),
                   shp(D_RET, BF16), shp(D_RET, BF16), shp(D_RET, BF16), shp(D_RET, F32),
                   shp(CONV_CH, F32)),
        grid=(b, s // tm),
        in_specs=[tok(d), row(d), per_b, per_b,
                  pl.BlockSpec(w_bf.shape, lambda bi, i: (0, 0)),
                  tab, tab, tab, row(LANES), row(LANES)],
        out_specs=(tok(D_ATT), tok(2 * LANES), pl.BlockSpec((None, ATT_KV_W, tm), lambda bi, i: (bi, 0, i)),
                   tok(D_RET), tok(D_RET), tok(D_RET), tok(D_RET), tok(CONV_CH)),
        compiler_params=_params("parallel", "parallel"),
        name="inproj",
    )(x, nw, sc, sh, w_bf, cos, sa, sb, qw, kw)


def _attn_kernel(*refs, window):
    if window:
        q_ref, kp_ref, kc_ref, kn_ref, vp_ref, vc_ref, vn_ref, kx_ref, vx_ref, sink_ref, o_ref = refs
    else:
        q_ref, kx_ref, vx_ref, sink_ref, o_ref = refs
    blk = ATT_BLOCK
    n_ctx = kx_ref.shape[0]
    nk = 3 * blk + n_ctx if window else n_ctx
    if window:
        i = pl.program_id(1)
        key = lax.broadcasted_iota(jnp.int32, (blk, 2 * blk), 0)
        qry = lax.broadcasted_iota(jnp.int32, (blk, 2 * blk), 1) & (blk - 1)
        mask_prev = (key >= qry) & (i > 0)
        mask_next = (key <= qry) & (i < pl.num_programs(1) - 1)
    first_head = lax.broadcasted_iota(jnp.int32, (blk, LANES), 1) < HEAD_DIM
    ones = jnp.ones((2 * SUBLANES, nk), BF16)
    for g in range(ATT_KV_HEADS):
        gs = slice(g * LANES, (g + 1) * LANES)
        vs = slice(g * HEAD_DIM, (g + 1) * HEAD_DIM)
        if window:
            kcat = jnp.concatenate([kp_ref[:, gs], kc_ref[:, gs], kn_ref[:, gs], kx_ref[:, gs]], axis=0)
            vt = jnp.concatenate([vp_ref[vs, :], vc_ref[vs, :], vn_ref[vs, :], vx_ref[vs, :]], axis=1)
        else:
            kcat, vt = kx_ref[:, gs], vx_ref[vs, :]
        vaug = jnp.concatenate([vt, ones], axis=0)
        for pr in range(2):
            pair = 2 * g + pr
            cs = slice(pair * LANES, (pair + 1) * LANES)
            qp = q_ref[:, cs]
            zero = jnp.zeros_like(qp)
            w = jnp.concatenate([jnp.where(first_head, qp, zero), jnp.where(first_head, zero, qp)], axis=0)
            s = lax.dot_general(kcat, w, (((1,), (1,)), ((), ())), preferred_element_type=F32)
            if window:
                parts = [jnp.where(mask_prev, s[0:blk], NEG_INF), s[blk:2 * blk],
                         jnp.where(mask_next, s[2 * blk:3 * blk], NEG_INF), s[3 * blk:]]
            else:
                parts = [s]
            snk = sink_ref[pair:pair + 1, :]
            m = snk
            for part in parts:
                m = jnp.maximum(m, jnp.max(part, axis=0, keepdims=True))
            p = jnp.concatenate([jnp.exp(part - m).astype(BF16) for part in parts], axis=0)
            o = jnp.dot(vaug, p, preferred_element_type=F32)
            den = o[HEAD_DIM:HEAD_DIM + 1, :] + jnp.exp(snk - m)
            on = o[0:HEAD_DIM, :] * (1.0 / den)
            ot = jnp.concatenate([on[:, 0:blk], on[:, blk:2 * blk]], axis=0)
            o_ref[:, cs] = ot.T.astype(BF16)


def _attention(q, k, vt, kx, vxt, sink_tab, window):
    b, s, _ = q.shape
    blk = ATT_BLOCK
    nb = s // blk
    n_ctx = kx.shape[1]
    qspec = pl.BlockSpec((None, blk, D_ATT), lambda bi, i: (bi, i, 0))
    kctx = pl.BlockSpec((None, n_ctx, 2 * LANES), lambda bi, i: (bi, 0, 0))
    vctx = pl.BlockSpec((None, ATT_KV_W, n_ctx), lambda bi, i: (bi, 0, 0))
    snk = pl.BlockSpec(sink_tab.shape, lambda bi, i: (0, 0))
    if window:
        prev = lambda i: jnp.maximum(i - 1, 0)
        nxt = lambda i: jnp.minimum(i + 1, nb - 1)
        same = lambda i: i
        kspec = lambda f: pl.BlockSpec((None, blk, 2 * LANES), lambda bi, i: (bi, f(i), 0))
        vspec = lambda f: pl.BlockSpec((None, ATT_KV_W, blk), lambda bi, i: (bi, 0, f(i)))
        in_specs = [qspec, kspec(prev), kspec(same), kspec(nxt), vspec(prev), vspec(same), vspec(nxt),
                    kctx, vctx, snk]
        args = (q, k, k, k, vt, vt, vt, kx, vxt, sink_tab)
    else:
        in_specs = [qspec, kctx, vctx, snk]
        args = (q, kx, vxt, sink_tab)
    return pl.pallas_call(
        functools.partial(_attn_kernel, window=window),
        out_shape=jax.ShapeDtypeStruct((b, s, D_ATT), BF16),
        grid=(b, nb),
        in_specs=in_specs,
        out_specs=qspec,
        compiler_params=_params("parallel", "parallel"),
        name="attention_window" if window else "attention_ctx",
    )(*args)


def _ret_state_kernel(lgl_ref, k_ref, v_ref, s0_ref, r_ref, fin_ref, s_scr, *, cb, reverse):
    c = pl.program_id(1)
    L = RET_CHUNK

    @pl.when(c == 0)
    def _():
        s_scr[...] = s0_ref[...]

    jj = lax.broadcasted_iota(jnp.int32, (L, LANES), 0).astype(F32)
    expo = jj if reverse else (L - 1.0) - jj
    same_head = (lax.broadcasted_iota(jnp.int32, (LANES, LANES), 0) // HEAD_DIM
                 == lax.broadcasted_iota(jnp.int32, (LANES, LANES), 1) // HEAD_DIM)
    for pr in range(RET_HEADS // 2):
        cs = slice(pr * LANES, (pr + 1) * LANES)
        lgl = lgl_ref[pr]
        kdec = jnp.exp(expo * lgl)
        cdec = jnp.exp(float(L) * lgl)
        state = s_scr[pr]
        for t in range(cb):
            cc = cb - 1 - t if reverse else t
            rows = slice(cc * L, (cc + 1) * L)
            r_ref[cc, pr] = state
            kd = k_ref[rows, cs].astype(F32) * kdec
            u = jnp.dot(kd.T.astype(BF16), v_ref[rows, cs], preferred_element_type=F32)
            state = cdec * state + jnp.where(same_head, u, 0.0)
        s_scr[pr] = state

    @pl.when(c == pl.num_programs(1) - 1)
    def _():
        fin_ref[...] = s_scr[...]


def _ret_states(lgl, rk, rv, s0, reverse):
    b, s, _ = rk.shape
    nc = s // RET_CHUNK
    cb = _pick_tile(nc, 8)
    nblk = nc // cb
    npair = RET_HEADS // 2
    blk_idx = (lambda c: nblk - 1 - c) if reverse else (lambda c: c)
    tok = pl.BlockSpec((None, cb * RET_CHUNK, D_RET), lambda bi, c: (bi, blk_idx(c), 0))
    st = pl.BlockSpec((None, npair, LANES, LANES), lambda bi, c: (bi, 0, 0, 0))
    return pl.pallas_call(
        functools.partial(_ret_state_kernel, cb=cb, reverse=reverse),
        out_shape=(jax.ShapeDtypeStruct((b, nc, npair, LANES, LANES), F32),
                   jax.ShapeDtypeStruct((b, npair, LANES, LANES), F32)),
        grid=(b, nblk),
        in_specs=[pl.BlockSpec(lgl.shape, lambda bi, c: (0, 0, 0)), tok, tok, st],
        out_specs=(pl.BlockSpec((None, cb, npair, LANES, LANES),
                                lambda bi, c: (bi, blk_idx(c), 0, 0, 0)), st),
        scratch_shapes=[pltpu.VMEM((npair, LANES, LANES), F32)],
        compiler_params=_params("parallel", "arbitrary"),
        name="ret_state_bwd" if reverse else "ret_state_fwd",
    )(lgl, rk, rv, s0)


def _ret_out_kernel(lgf_ref, lgb_ref, q_ref, k_ref, v_ref, g_ref, rf_ref, rb_ref, gnw_ref, o_ref, *, cb):
    L = RET_CHUNK
    i0 = lax.broadcasted_iota(jnp.int32, (L, LANES), 0).astype(F32)
    i1 = lax.broadcasted_iota(jnp.int32, (L, LANES), 1).astype(F32)
    diff = i0 - i1
    lo = lax.broadcasted_iota(jnp.int32, (L, LANES), 1) < HEAD_DIM
    grp = _head_group_matrix()
    inv = 1.0 / HEAD_DIM
    tabs = []
    for pr in range(RET_HEADS // 2):
        lgf, lgb = lgf_ref[pr], lgb_ref[pr]
        dmat = [jnp.where(diff >= 0.0,
                          jnp.exp(jnp.maximum(diff, 0.0) * lgf[:, a:a + 1]),
                          jnp.exp(jnp.maximum(-diff, 0.0) * lgb[:, a:a + 1]))
                for a in (0, HEAD_DIM)]
        tabs.append((jnp.exp((i0 + 1.0) * lgf), jnp.exp((float(L) - i0) * lgb),
                     jnp.concatenate(dmat, axis=0), gnw_ref[:, pr * LANES:(pr + 1) * LANES]))

    def chunk(t, carry):
        rows = pl.ds(pl.multiple_of(t * L, L), L)
        for pr in range(RET_HEADS // 2):
            cs = slice(pr * LANES, (pr + 1) * LANES)
            xif, xib, dmat, gnw = tabs[pr]
            qp, kp, vp = q_ref[rows, cs], k_ref[rows, cs], v_ref[rows, cs]
            zero = jnp.zeros_like(kp)
            qz = jnp.concatenate([jnp.where(lo, qp, zero), jnp.where(lo, zero, qp)], axis=0)
            a = lax.dot_general(qz, kp, (((1,), (1,)), ((), ())), preferred_element_type=F32) * dmat
            qf = qp.astype(F32)
            lhs = jnp.concatenate([a[0:L].astype(BF16), a[L:].astype(BF16),
                                   (qf * xif).astype(BF16), (qf * xib).astype(BF16)], axis=1)
            rhs = jnp.concatenate([jnp.where(lo, vp, zero), jnp.where(lo, zero, vp),
                                   rf_ref[t, pr].astype(BF16), rb_ref[t, pr].astype(BF16)], axis=0)
            acc = jnp.dot(lhs, rhs, preferred_element_type=F32)
            dl = acc - _per_head_sum(acc, grp) * inv
            var = _per_head_sum(dl * dl, grp) * inv
            y = dl * lax.rsqrt(var + EPS) * gnw
            gt = g_ref[rows, cs]
            o_ref[rows, cs] = (gt * _sigmoid(gt) * y).astype(BF16)
        return carry

    lax.fori_loop(0, cb, chunk, 0, unroll=4)


def _ret_out(lgf, lgb, rq, rk, rv, rg, rf, rb, gnw):
    b, s, _ = rq.shape
    nc = s // RET_CHUNK
    cb = _pick_tile(nc, 8)
    npair = RET_HEADS // 2
    tok = pl.BlockSpec((None, cb * RET_CHUNK, D_RET), lambda bi, c: (bi, c, 0))
    st = pl.BlockSpec((None, cb, npair, LANES, LANES), lambda bi, c: (bi, c, 0, 0, 0))
    lg = pl.BlockSpec(lgf.shape, lambda bi, c: (0, 0, 0))
    return pl.pallas_call(
        functools.partial(_ret_out_kernel, cb=cb),
        out_shape=jax.ShapeDtypeStruct((b, s, D_RET), BF16),
        grid=(b, nc // cb),
        in_specs=[lg, lg, tok, tok, tok, tok, st, st, pl.BlockSpec((1, D_RET), lambda bi, c: (0, 0))],
        out_specs=tok,
        compiler_params=_params("parallel", "parallel"),
        name="ret_out",
    )(lgf, lgb, rq, rk, rv, rg, rf, rb, gnw)


def _conv_kernel(prev_ref, cur_ref, next_ref, w_ref, b_ref, lnw_ref, lnb_ref, o_ref, xpad, shifted, hbuf):
    i = pl.program_id(1)
    tm = cur_ref.shape[0]
    halo = CONV_HALO
    first = i == 0
    last = i == pl.num_programs(1) - 1
    xpad[0:halo, :] = jnp.where(first, 0.0, prev_ref[...])
    xpad[halo:halo + tm, :] = cur_ref[...]
    xpad[halo + tm:2 * halo + tm, :] = jnp.where(last, 0.0, next_ref[...])
    span = tm + 2 * halo - SUBLANES
    for r in range(SUBLANES):
        shifted[r] = xpad[r:r + span, :]
    base = halo - CONV_WIDTH // 2
    bias = b_ref[...]

    def sub(sb, carry):
        r0 = pl.multiple_of(sb * CONV_SUB, CONV_SUB)
        chains = [None] * CONV_CHAINS
        for w in range(CONV_WIDTH):
            off = base + w
            xs = shifted[off % SUBLANES, pl.ds(r0 + (off // SUBLANES) * SUBLANES, CONV_SUB), :]
            term = xs * w_ref[w:w + 1, :]
            c = w % CONV_CHAINS
            chains[c] = term if chains[c] is None else chains[c] + term
        hbuf[pl.ds(r0, CONV_SUB), :] = (chains[0] + chains[1]) + (chains[2] + chains[3]) + bias
        return carry

    lax.fori_loop(0, tm // CONV_SUB, sub, 0, unroll=2)
    h = hbuf[...]
    mu = jnp.mean(h, axis=-1, keepdims=True)
    dl = h - mu
    var = jnp.mean(dl * dl, axis=-1, keepdims=True)
    y = dl * lax.rsqrt(var + EPS) * lnw_ref[...] + lnb_ref[...]
    o_ref[...] = (y * _sigmoid(y)).astype(BF16)


def _conv(cv, w, bias, lnw, lnb):
    b, s, ch = cv.shape
    tm = _pick_tile(s, 512)
    hpb = tm // CONV_HALO
    nh = s // CONV_HALO
    row = pl.BlockSpec((1, ch), lambda bi, i: (0, 0))
    span = tm + 2 * CONV_HALO - SUBLANES
    return pl.pallas_call(
        _conv_kernel,
        out_shape=jax.ShapeDtypeStruct((b, s, ch), BF16),
        grid=(b, s // tm),
        in_specs=[pl.BlockSpec((None, CONV_HALO, ch), lambda bi, i: (bi, jnp.maximum(i * hpb - 1, 0), 0)),
                  pl.BlockSpec((None, tm, ch), lambda bi, i: (bi, i, 0)),
                  pl.BlockSpec((None, CONV_HALO, ch),
                               lambda bi, i: (bi, jnp.minimum((i + 1) * hpb, nh - 1), 0)),
                  pl.BlockSpec(w.shape, lambda bi, i: (0, 0)), row, row, row],
        out_specs=pl.BlockSpec((None, tm, ch), lambda bi, i: (bi, i, 0)),
        scratch_shapes=[pltpu.VMEM((tm + 2 * CONV_HALO, ch), F32),
                        pltpu.VMEM((SUBLANES, span, ch), F32),
                        pltpu.VMEM((tm, ch), F32)],
        compiler_params=_params("parallel", "parallel"),
        name="conv",
    )(cv, cv, cv, w, bias, lnw, lnb)


def _outproj_kernel(*refs, route):
    if route:
        (att_ref, ret_ref, cnv_ref, w_ref, x_ref, g1_ref, nw_ref, sc_ref, sh_ref, rcat_ref,
         xo_ref, h_ref, rt_ref) = refs
    else:
        att_ref, ret_ref, cnv_ref, w_ref, x_ref, g1_ref, nw_ref, sc_ref, sh_ref, xo_ref, h_ref = refs
    y = (jnp.dot(att_ref[...], w_ref[0:D_ATT, :], preferred_element_type=F32)
         + jnp.dot(ret_ref[...], w_ref[D_ATT:D_ATT + D_RET, :], preferred_element_type=F32)
         + jnp.dot(cnv_ref[...], w_ref[D_ATT + D_RET:, :], preferred_element_type=F32))
    xn = x_ref[...] + g1_ref[...] * y
    xo_ref[...] = xn
    h = _modulated_rms(xn, nw_ref[...], sc_ref[...], sh_ref[...])
    h_ref[...] = _pack_bf16_pairs(h) if route else h.astype(BF16)
    if route:
        hi = h.astype(BF16)
        lo = (h - hi.astype(F32)).astype(BF16)
        tm = h.shape[0]
        r = jnp.dot(jnp.concatenate([hi, lo], axis=0), rcat_ref[...], preferred_element_type=F32)
        logits = (r[0:tm, 0:LANES] + r[0:tm, LANES:]) + (r[tm:, 0:LANES] + r[tm:, LANES:])
        lane = lax.broadcasted_iota(jnp.int32, logits.shape, 1).astype(F32)
        logits = jnp.where(lane < N_EXPERTS, logits, NEG_INF)
        m1 = jnp.max(logits, axis=-1, keepdims=True)
        i1 = jnp.min(jnp.where(logits == m1, lane, float(LANES)), axis=-1, keepdims=True)
        rest = jnp.where(lane == i1, NEG_INF, logits)
        m2 = jnp.max(rest, axis=-1, keepdims=True)
        i2 = jnp.min(jnp.where(rest == m2, lane, float(LANES)), axis=-1, keepdims=True)
        e2 = jnp.exp(m2 - m1)
        w1 = 1.0 / (1.0 + e2)
        w2 = e2 / (1.0 + e2)
        rt_ref[...] = jnp.where(lane == 0.0, i1,
                                jnp.where(lane == 1.0, i2,
                                          jnp.where(lane == 2.0, w1, jnp.where(lane == 3.0, w2, 0.0))))


def _outproj(att, ret, cnv, w_bf, x, g1, nw, sc, sh, router=None):
    b, s, d = x.shape
    tm = _pick_tile(s, 512)
    tok = lambda n: pl.BlockSpec((None, tm, n), lambda bi, i: (bi, i, 0))
    per_b = pl.BlockSpec((None, 1, d), lambda bi, i: (bi, 0, 0))
    row = pl.BlockSpec((1, d), lambda bi, i: (0, 0))
    in_specs = [tok(D_ATT), tok(D_RET), tok(CONV_CH), pl.BlockSpec(w_bf.shape, lambda bi, i: (0, 0)),
                tok(d), per_b, row, per_b, per_b]
    args = [att, ret, cnv, w_bf, x, g1, nw, sc, sh]
    h_shape = (b, s, d // 2) if router is not None else (b, s, d)
    out_shape = [jax.ShapeDtypeStruct((b, s, d), F32),
                 jax.ShapeDtypeStruct(h_shape, jnp.int32 if router is not None else BF16)]
    out_specs = [tok(d), tok(h_shape[2])]
    if router is not None:
        rpad = jnp.zeros((d, LANES), F32).at[:, :N_EXPERTS].set(router)
        rhi = rpad.astype(BF16)
        rlo = (rpad - rhi.astype(F32)).astype(BF16)
        in_specs += [pl.BlockSpec((d, 2 * LANES), lambda bi, i: (0, 0))]
        args += [jnp.concatenate([rhi, rlo], axis=1)]
        out_shape.append(jax.ShapeDtypeStruct((b, s, LANES), F32))
        out_specs.append(tok(LANES))
    return pl.pallas_call(
        functools.partial(_outproj_kernel, route=router is not None),
        out_shape=tuple(out_shape),
        grid=(b, s // tm),
        in_specs=in_specs,
        out_specs=tuple(out_specs),
        compiler_params=_params("parallel", "parallel"),
        name="outproj_route" if router is not None else "outproj",
    )(*args)


def _swiglu_chunks(h, wg_ref, wu_ref, wd_ref, sub):
    total = None
    for c0 in range(0, wg_ref.shape[1], sub):
        cs = slice(c0, min(c0 + sub, wg_ref.shape[1]))
        gate = jnp.dot(h, wg_ref[:, cs].astype(BF16), preferred_element_type=F32)
        up = jnp.dot(h, wu_ref[:, cs].astype(BF16), preferred_element_type=F32)
        act = (gate * _sigmoid(gate) * up).astype(BF16)
        part = jnp.dot(act, wd_ref[cs, :].astype(BF16), preferred_element_type=F32)
        total = part if total is None else total + part
    return total


def _ffn_kernel(h_ref, wg_ref, wu_ref, wd_ref, x_ref, g2_ref, o_ref):
    y = _swiglu_chunks(h_ref[...], wg_ref, wu_ref, wd_ref, FFN_SUB)
    o_ref[...] = x_ref[...] + g2_ref[...] * y


def _ffn(h, wg, wu, wd, x, g2):
    b, s, d = x.shape
    f = wg.shape[1]
    tm = _pick_tile(s, 512)
    tok = lambda: pl.BlockSpec((None, tm, d), lambda bi, i: (bi, i, 0))
    res = lambda shape: pl.BlockSpec(shape, lambda bi, i: (0, 0), pipeline_mode=pl.Buffered(1))
    return pl.pallas_call(
        _ffn_kernel,
        out_shape=jax.ShapeDtypeStruct((b, s, d), F32),
        grid=(b, s // tm),
        in_specs=[tok(), res((d, f)), res((d, f)), res((f, d)), tok(),
                  pl.BlockSpec((None, 1, d), lambda bi, i: (bi, 0, 0))],
        out_specs=tok(),
        compiler_params=_params("parallel", "parallel"),
        name="ffn_dense",
    )(h, wg, wu, wd, x, g2)


def _gather_rows(idx, src):
    n = idx.shape[0]
    d = src.shape[1]
    mesh = plsc.VectorSubcoreMesh(core_axis_name="core", subcore_axis_name="subcore")

    @pl.kernel(out_type=jax.ShapeDtypeStruct((n, d), src.dtype), mesh=mesh, name="moe_gather")
    def gather(src_hbm, idx_hbm, out_hbm):
        def body(idx_vmem, out_vmem):
            pltpu.sync_copy(src_hbm.at[idx_vmem.at[0, pl.ds(0, SC_WINDOW)]], out_vmem)

        pltpu.emit_pipeline(
            body,
            grid=(n // SC_WINDOW,),
            in_specs=[pl.BlockSpec((1, LANES), lambda i: (i, 0))],
            out_specs=[pl.BlockSpec((SC_WINDOW, d), lambda i: (i, 0))],
            core_axis_name=("core", "subcore"),
            dimension_semantics=(pltpu.PARALLEL,),
        )(idx_hbm, out_hbm)

    idx_rows = jnp.pad(idx.reshape(n // SC_WINDOW, SC_WINDOW), ((0, 0), (0, LANES - SC_WINDOW)))
    return gather(src, idx_rows)


def _invert_rows(dest, n_rows):
    n = dest.shape[0]
    n_tok = n // 2
    mesh = plsc.VectorSubcoreMesh(core_axis_name="core", subcore_axis_name="subcore")

    @pl.kernel(out_type=jax.ShapeDtypeStruct((n_rows,), jnp.int32), mesh=mesh, name="moe_row_src",
               scratch_types=[pltpu.VMEM((n,), jnp.int32), pltpu.VMEM((n_rows,), jnp.int32)],
               compiler_params=pltpu.CompilerParams(needs_layout_passes=False))
    def invert(dest_hbm, out_hbm, dest_vmem, rows_vmem):
        @pl.when((lax.axis_index("core") == 0) & (lax.axis_index("subcore") == 0))
        def _():
            pltpu.sync_copy(dest_hbm, dest_vmem)

            @pl.loop(0, n_rows, step=SC_LANES)
            def _(r):
                rows_vmem[pl.ds(r, SC_LANES)] = jnp.zeros((SC_LANES,), jnp.int32)

            @pl.loop(0, n, step=SC_LANES)
            def _(a):
                tok = lax.iota(jnp.int32, SC_LANES) + a
                tok = jnp.where(tok >= n_tok, tok - n_tok, tok)
                plsc.store_scatter(rows_vmem, [dest_vmem[pl.ds(a, SC_LANES)]], tok)

            pltpu.sync_copy(rows_vmem, out_hbm)

    return invert(dest)


def _moe_kernel(te_ref, nu_ref, x_ref, wg_ref, wu_ref, wd_ref, *rest, tile0):
    o_ref, xb, acc = rest[-3:]
    t = pl.program_id(0) + tile0
    j = pl.program_id(1)
    nj = pl.num_programs(1)
    used = t < nu_ref[0]

    @pl.when(used & (j == 0))
    def _():
        lo, hi = _unpack_bf16_pairs(x_ref[...])
        half = lo.shape[1]
        xb[:, 0:half] = lo.astype(BF16)
        xb[:, half:] = hi.astype(BF16)
        acc[...] = jnp.zeros_like(acc)

    @pl.when(used)
    def _():
        acc[...] += _swiglu_chunks(xb[...], wg_ref, wu_ref, wd_ref, MOE_SUB)

    @pl.when(used & (j == nj - 1))
    def _():
        o_ref[...] = _pack_bf16_pairs(acc[...])

    @pl.when(jnp.logical_not(used) & (j == nj - 1))
    def _():
        o_ref[...] = jnp.zeros_like(o_ref)


def _moe_grouped(tile_expert, n_used, xs, wg, wu, wd, y_prev, tile0, n_rows):
    p, dp = xs.shape
    d = 2 * dp
    f = wg.shape[2]
    tm = MOE_TM
    tf = MOE_TF
    nj = f // tf

    def jj(t, j, te, nu):
        return jnp.where(t + tile0 < nu[0], j, nj - 1)

    in_specs = [pl.BlockSpec((tm, dp), lambda t, j, te, nu: (t, 0)),
                pl.BlockSpec((None, d, tf), lambda t, j, te, nu: (te[t + tile0], 0, jj(t, j, te, nu))),
                pl.BlockSpec((None, d, tf), lambda t, j, te, nu: (te[t + tile0], 0, jj(t, j, te, nu))),
                pl.BlockSpec((None, tf, d), lambda t, j, te, nu: (te[t + tile0], jj(t, j, te, nu), 0))]
    args = [tile_expert, n_used, xs, wg, wu, wd]
    aliases = {}
    if y_prev is not None:
        in_specs.append(pl.BlockSpec(memory_space=pl.ANY))
        args.append(y_prev)
        aliases = {len(args) - 1: 0}
    return pl.pallas_call(
        functools.partial(_moe_kernel, tile0=tile0),
        out_shape=jax.ShapeDtypeStruct((n_rows, dp), jnp.int32),
        grid_spec=pltpu.PrefetchScalarGridSpec(
            num_scalar_prefetch=2,
            grid=(p // tm, nj),
            in_specs=in_specs,
            out_specs=pl.BlockSpec((tm, dp), lambda t, j, te, nu: (t + tile0, 0)),
            scratch_shapes=[pltpu.VMEM((tm, d), BF16), pltpu.VMEM((tm, d), F32)]),
        input_output_aliases=aliases,
        compiler_params=_params("arbitrary", "arbitrary"),
        name="moe_grouped",
    )(*args)


def _combine_kernel(y1_ref, y2_ref, x_ref, g2_ref, rt_ref, o_ref):
    rt = rt_ref[...]
    w1, w2 = rt[:, 2:3], rt[:, 3:4]
    half = y1_ref.shape[1]
    for k, (a, c) in enumerate(zip(_unpack_bf16_pairs(y1_ref[...]), _unpack_bf16_pairs(y2_ref[...]))):
        cs = slice(k * half, (k + 1) * half)
        o_ref[:, cs] = x_ref[:, cs] + g2_ref[:, cs] * (w1 * a + w2 * c)


def _moe_combine(yg, x, g2, route):
    b, s, d = x.shape
    tm = _pick_tile(s, 512)
    tok = lambda n: pl.BlockSpec((None, tm, n), lambda bi, i: (bi, i, 0))
    return pl.pallas_call(
        _combine_kernel,
        out_shape=jax.ShapeDtypeStruct((b, s, d), F32),
        grid=(b, s // tm),
        in_specs=[tok(d // 2), pl.BlockSpec((None, tm, d // 2), lambda bi, i: (b + bi, i, 0)), tok(d),
                  pl.BlockSpec((None, 1, d), lambda bi, i: (bi, 0, 0)), tok(LANES)],
        out_specs=tok(d),
        compiler_params=_params("parallel", "parallel"),
        name="moe_combine",
    )(yg, yg, x, g2, route)


def _moe(h, route, x, g2, wg, wu, wd):
    b, s, d = x.shape
    n_tok = b * s
    tm = MOE_TM
    rt = route.reshape(n_tok, LANES)
    flat_e = jnp.concatenate([rt[:, 0], rt[:, 1]]).astype(jnp.int32)
    onehot = (flat_e[:, None] == jnp.arange(N_EXPERTS, dtype=jnp.int32)[None, :]).astype(jnp.int32)
    csum = jnp.cumsum(onehot, axis=0)
    rank = jnp.sum(csum * onehot, axis=1) - 1
    counts = csum[-1]
    tiles_e = (counts + tm - 1) // tm
    tiles_cum = jnp.cumsum(tiles_e)
    row_start = (tiles_cum - tiles_e) * tm
    dest = jnp.sum(onehot * row_start[None, :], axis=1) + rank
    n_tiles = 2 * n_tok // tm + N_EXPERTS
    tile_ids = jnp.arange(n_tiles, dtype=jnp.int32)
    tile_expert = jnp.sum((tile_ids[:, None] >= tiles_cum[None, :]).astype(jnp.int32), axis=1)
    last_e = jnp.max(jnp.where(tiles_e > 0, jnp.arange(N_EXPERTS, dtype=jnp.int32), 0))
    tile_expert = jnp.minimum(tile_expert, last_e).astype(jnp.int32)
    n_used = tiles_cum[-1:].astype(jnp.int32)
    dest = dest.astype(jnp.int32)
    row_src = _invert_rows(dest, n_tiles * tm)
    n_chunks = max(c for c in range(1, MOE_CHUNKS + 1) if n_tiles % c == 0)
    tiles_c = n_tiles // n_chunks
    h_flat = h.reshape(n_tok, d // 2)
    xs = [_gather_rows(row_src[c * tiles_c * tm:(c + 1) * tiles_c * tm], h_flat) for c in range(n_chunks)]
    y = None
    for c in range(n_chunks):
        y = _moe_grouped(tile_expert, n_used, xs[c], wg, wu, wd, y, c * tiles_c, n_tiles * tm)
    yg = _gather_rows(dest, y).reshape(2 * b, s, d // 2)
    return _moe_combine(yg, x, g2, route)


def _rope_tables(n):
    rows = n // GRID_W
    r = jnp.repeat(jnp.arange(rows), GRID_W).astype(F32)
    col = jnp.tile(jnp.arange(GRID_W), rows).astype(F32)
    freqs = ROPE_BASE ** (-jnp.arange(ROPE_FREQS, dtype=F32) / ROPE_FREQS)
    ang = jnp.stack([r[:, None] * freqs, col[:, None] * freqs], axis=1)
    ang = jnp.repeat(ang[:, :, None, :], 2, axis=2).reshape(n, HEAD_DIM)
    ang = jnp.tile(ang, (1, LANES // HEAD_DIM))
    cos, sin = jnp.cos(ang), jnp.sin(ang)
    first_half = (jnp.arange(LANES) % (2 * ROPE_FREQS)) < ROPE_FREQS
    return cos, jnp.where(first_half, -sin, 0.0), jnp.where(first_half, 0.0, sin)


def _lane_rows(lg):
    return jnp.repeat(lg.astype(F32), HEAD_DIM).reshape(RET_HEADS // 2, 1, LANES)


def kernel(x, c, ctx, c_ctx, ada_w, ada_b, norm1_w, norm2_w, w_in, w_out, q_norm_w, k_norm_w,
           attn_sink, ret_decay_f, ret_decay_b, ret_gn_w, conv_w, conv_b, conv_ln_w, conv_ln_b,
           ffn_w_gate, ffn_w_up, ffn_w_down, router_w, moe_w_gate, moe_w_up, moe_w_down):
    b, n, d = x.shape
    n_ctx = ctx.shape[1]
    depth = ada_w.shape[0]
    cond = jnp.zeros((SUBLANES, d), F32).at[0:b].set(c).at[b].set(c_ctx)
    mods = _adaln(cond, ada_w, ada_b).reshape(depth, SUBLANES, 6, d)
    cos, sa, sb = _rope_tables(n)
    ones_c = jnp.ones((n_ctx, LANES), F32)
    zeros_c = jnp.zeros((n_ctx, LANES), F32)
    zero_state = jnp.zeros((b, RET_HEADS // 2, LANES, LANES), F32)
    row = lambda v: v.reshape(1, -1)
    moe_bf = (moe_w_gate.astype(BF16), moe_w_up.astype(BF16), moe_w_down.astype(BF16))
    x, moe_bf = lax.optimization_barrier((x, moe_bf))
    for l in range(depth):
        last = l == depth - 1
        m_lat = [mods[l, 0:b, k][:, None, :] for k in range(6)]
        m_ctx = [jnp.broadcast_to(mods[l, b, k][None, None, :], (b, 1, d)) for k in range(6)]
        w_in_bf = w_in[l].astype(BF16)
        w_out_bf = w_out[l].astype(BF16)
        qw = row(jnp.tile(q_norm_w[l], LANES // HEAD_DIM))
        kw = row(jnp.tile(k_norm_w[l], LANES // HEAD_DIM))
        lgf = _lane_rows(jax.nn.log_sigmoid(ret_decay_f[l].astype(F32)))
        lgb = _lane_rows(jax.nn.log_sigmoid(ret_decay_b[l].astype(F32)))
        sink_tab = jnp.repeat(attn_sink[l].astype(F32), ATT_BLOCK).reshape(ATT_Q_HEADS // 2, 2 * ATT_BLOCK)

        q, k, v, rk, rv, rq, rg, cv = _inproj(x, row(norm1_w[l]), m_lat[1], m_lat[0], w_in_bf,
                                               cos, sa, sb, qw, kw)
        qc, kc, vc, rkc, rvc, rqc, rgc, cvc = _inproj(ctx, row(norm1_w[l]), m_ctx[1], m_ctx[0], w_in_bf,
                                                       ones_c, zeros_c, zeros_c, qw, kw)
        rf_c, s_f = _ret_states(lgf, rkc, rvc, zero_state, reverse=False)
        rb_c, s_b = _ret_states(lgb, rkc, rvc, zero_state, reverse=True)
        rf, _ = _ret_states(lgf, rk, rv, s_f, reverse=False)
        rb, _ = _ret_states(lgb, rk, rv, s_b, reverse=True)

        att = _attention(q, k, v, kc, vc, sink_tab, window=True)
        ret = _ret_out(lgf, lgb, rq, rk, rv, rg, rf, rb, row(ret_gn_w[l]))
        cnv = _conv(cv, conv_w[l], row(conv_b[l]), row(conv_ln_w[l]), row(conv_ln_b[l]))

        if l % 2 == 0:
            i = l // 2
            wg, wu, wd = ffn_w_gate[i].astype(BF16), ffn_w_up[i].astype(BF16), ffn_w_down[i].astype(BF16)
            x_mid, h2 = _outproj(att, ret, cnv, w_out_bf, x, m_lat[2], row(norm2_w[l]), m_lat[4], m_lat[3])
            x_new = _ffn(h2, wg, wu, wd, x_mid, m_lat[5])
        else:
            i = l // 2
            wg, wu, wd = moe_bf[0][i], moe_bf[1][i], moe_bf[2][i]
            x_mid, h2, route = _outproj(att, ret, cnv, w_out_bf, x, m_lat[2], row(norm2_w[l]),
                                        m_lat[4], m_lat[3], router=router_w[i])
            x_new = _moe(h2, route, x_mid, m_lat[5], wg, wu, wd)

        if not last:
            att_c = _attention(qc, None, None, kc, vc, sink_tab, window=False)
            ret_c = _ret_out(lgf, lgb, rqc, rkc, rvc, rgc, rf_c, rb_c, row(ret_gn_w[l]))
            cnv_c = _conv(cvc, conv_w[l], row(conv_b[l]), row(conv_ln_w[l]), row(conv_ln_b[l]))
            if l % 2 == 0:
                c_mid, h2c = _outproj(att_c, ret_c, cnv_c, w_out_bf, ctx, m_ctx[2], row(norm2_w[l]),
                                      m_ctx[4], m_ctx[3])
                ctx = _ffn(h2c, wg, wu, wd, c_mid, m_ctx[5])
            else:
                c_mid, h2c, route_c = _outproj(att_c, ret_c, cnv_c, w_out_bf, ctx, m_ctx[2],
                                               row(norm2_w[l]), m_ctx[4], m_ctx[3],
                                               router=router_w[i])
                ctx = _moe(h2c, route_c, c_mid, m_ctx[5], wg, wu, wd)
        x = x_new
    return x
```

```python
import functools

import jax
import jax.numpy as jnp
from jax import lax
from jax.experimental import pallas as pl
from jax.experimental.pallas import tpu as pltpu
from jax.experimental.pallas import tpu_sc as plsc

F32 = jnp.float32
BF16 = jnp.bfloat16

GRID_W = 64
HEAD_DIM = 64
ATT_Q_HEADS = 8
ATT_KV_HEADS = 2
ATT_WINDOW = 128
ATT_BLOCK = 128
RET_HEADS = 4
RET_CHUNK = 128
RET_K_SCALE = HEAD_DIM ** -0.5
ATT_SCALE = HEAD_DIM ** -0.5
CONV_CH = 256
CONV_WIDTH = 31
ROPE_BASE = 10000.0
ROPE_FREQS = HEAD_DIM // 4
D_ATT = ATT_Q_HEADS * HEAD_DIM
D_RET = RET_HEADS * HEAD_DIM
ATT_KV_W = ATT_KV_HEADS * HEAD_DIM
C_ATT_K = 0
C_ATT_V = C_ATT_K + ATT_KV_W
C_RET_K = C_ATT_V + ATT_KV_W
C_RET_V = C_RET_K + D_RET
C_ATT_Q = C_RET_V + D_RET
C_RET_Q = C_ATT_Q + D_ATT
C_RET_G = C_RET_Q + D_RET
C_CONV = C_RET_G + D_RET
N_EXPERTS = 8
EPS = 1e-6
NEG_INF = -1e30

LANES = 128
SUBLANES = 8
VMEM_LIMIT = 48 * 1024 * 1024
CONV_HALO = 16
CONV_SUB = 16
CONV_CHAINS = 4
MOE_TM = 512
MOE_CHUNKS = 2
MOE_SUB = 512
FFN_SUB = 512
MOE_TF = 1792
SC_LANES = 16
SC_WINDOW = 64


def _params(*sem):
    return pltpu.CompilerParams(dimension_semantics=sem, vmem_limit_bytes=VMEM_LIMIT)


def _sigmoid(x):
    return 1.0 / (1.0 + jnp.exp(-x))


def _pack_bf16_pairs(v):
    c = v.shape[1] // 2
    bits = pltpu.bitcast(v.astype(BF16).astype(F32), jnp.uint32)
    packed = (bits[:, c:] & jnp.uint32(0xFFFF0000)) | (bits[:, :c] >> 16)
    return pltpu.bitcast(packed, jnp.int32)


def _unpack_bf16_pairs(p):
    bits = pltpu.bitcast(p, jnp.uint32)
    return pltpu.bitcast(bits << 16, F32), pltpu.bitcast(bits & jnp.uint32(0xFFFF0000), F32)


def _pick_tile(n, pref):
    t = min(n, pref)
    assert n % t == 0, (n, t)
    return t


def _adaln_kernel(c_ref, w_ref, b_ref, o_ref):
    c = c_ref[...]
    s = c * _sigmoid(c)
    o_ref[...] = jnp.dot(s, w_ref[...], preferred_element_type=F32,
                         precision=lax.Precision.HIGHEST) + b_ref[...]


def _adaln(cond, ada_w, ada_b):
    depth, d, n = ada_w.shape
    tn = _pick_tile(n, 1536)
    return pl.pallas_call(
        _adaln_kernel,
        out_shape=jax.ShapeDtypeStruct((depth, cond.shape[0], n), F32),
        grid=(depth, n // tn),
        in_specs=[pl.BlockSpec(cond.shape, lambda l, j: (0, 0)),
                  pl.BlockSpec((None, d, tn), lambda l, j: (l, 0, j)),
                  pl.BlockSpec((None, 1, tn), lambda l, j: (l, 0, j))],
        out_specs=pl.BlockSpec((None, cond.shape[0], tn), lambda l, j: (l, 0, j)),
        compiler_params=_params("parallel", "parallel"),
        name="adaln",
    )(cond, ada_w, ada_b.reshape(depth, 1, n))


def _modulated_rms(x, nw, sc, sh):
    ms = jnp.mean(x * x, axis=-1, keepdims=True)
    return (x * lax.rsqrt(ms + EPS) * nw) * (1.0 + sc) + sh


def _head_group_matrix():
    r = lax.broadcasted_iota(jnp.int32, (2 * LANES, 2 * LANES), 0) // HEAD_DIM
    c = lax.broadcasted_iota(jnp.int32, (2 * LANES, 2 * LANES), 1) // HEAD_DIM
    return jnp.where(r == c, 1.0, 0.0).astype(BF16)


def _per_head_sum(v, grp):
    hi = v.astype(BF16)
    lo = (v - hi.astype(F32)).astype(BF16)
    r = jnp.dot(jnp.concatenate([hi, lo], axis=1), grp, preferred_element_type=F32)
    return r[:, 0:LANES] + r[:, LANES:]


def _dup_halves(t):
    sw = pltpu.roll(t, HEAD_DIM, 1)
    lo = lax.broadcasted_iota(jnp.int32, t.shape, 1) < HEAD_DIM
    return jnp.where(lo, t, sw), jnp.where(lo, sw, t)


def _inproj_kernel(x_ref, nw_ref, sc_ref, sh_ref, w_ref, cos_ref, sa_ref, sb_ref, qw_ref, kw_ref,
                   q_ref, k_ref, v_ref, rk_ref, rv_ref, rq_ref, rg_ref, cv_ref):
    hb = _modulated_rms(x_ref[...], nw_ref[...], sc_ref[...], sh_ref[...]).astype(BF16)

    def proj(c0, n):
        return jnp.dot(hb, w_ref[:, c0:c0 + n], preferred_element_type=F32)

    grp = _head_group_matrix()
    cos, sa, sb = cos_ref[...], sa_ref[...], sb_ref[...]

    def norm_rope(p, wrow):
        y = p * lax.rsqrt(_per_head_sum(p * p, grp) * (1.0 / HEAD_DIM) + EPS) * wrow
        return (y * cos + pltpu.roll(y, LANES - ROPE_FREQS, 1) * sa
                + pltpu.roll(y, ROPE_FREQS, 1) * sb)

    kv = proj(C_ATT_K, 2 * ATT_KV_W)
    k0, k1 = _dup_halves(norm_rope(kv[:, 0:ATT_KV_W], kw_ref[...]))
    k_ref[:, 0:LANES] = k0.astype(BF16)
    k_ref[:, LANES:2 * LANES] = k1.astype(BF16)
    v_ref[...] = kv[:, ATT_KV_W:].T.astype(BF16)
    qw = qw_ref[...] * ATT_SCALE
    qall = proj(C_ATT_Q, D_ATT)
    for j in range(D_ATT // LANES):
        q_ref[:, j * LANES:(j + 1) * LANES] = norm_rope(qall[:, j * LANES:(j + 1) * LANES], qw).astype(BF16)
    rk_ref[...] = (proj(C_RET_K, D_RET) * RET_K_SCALE).astype(BF16)
    rv_ref[...] = proj(C_RET_V, D_RET).astype(BF16)
    rq_ref[...] = proj(C_RET_Q, D_RET).astype(BF16)
    rg_ref[...] = proj(C_RET_G, D_RET)
    cv_ref[...] = proj(C_CONV, CONV_CH) * _sigmoid(proj(C_CONV + CONV_CH, CONV_CH))


def _inproj(x, nw, sc, sh, w_bf, cos, sa, sb, qw, kw):
    b, s, d = x.shape
    tm = _pick_tile(s, 512)
    row = lambda n: pl.BlockSpec((1, n), lambda bi, i: (0, 0))
    per_b = pl.BlockSpec((None, 1, d), lambda bi, i: (bi, 0, 0))
    tab = pl.BlockSpec((tm, LANES), lambda bi, i: (i, 0))
    tok = lambda n: pl.BlockSpec((None, tm, n), lambda bi, i: (bi, i, 0))
    shp = lambda n, dt: jax.ShapeDtypeStruct((b, s, n), dt)
    return pl.pallas_call(
        _inproj_kernel,
        out_shape=(shp(D_ATT, BF16), shp(2 * LANES, BF16), jax.ShapeDtypeStruct((b, ATT_KV_W, s), BF16),
                   shp(D_RET, BF16), shp(D_RET, BF16), shp(D_RET, BF16), shp(D_RET, F32),
                   shp(CONV_CH, F32)),
        grid=(b, s // tm),
        in_specs=[tok(d), row(d), per_b, per_b,
                  pl.BlockSpec(w_bf.shape, lambda bi, i: (0, 0)),
                  tab, tab, tab, row(LANES), row(LANES)],
        out_specs=(tok(D_ATT), tok(2 * LANES), pl.BlockSpec((None, ATT_KV_W, tm), lambda bi, i: (bi, 0, i)),
                   tok(D_RET), tok(D_RET), tok(D_RET), tok(D_RET), tok(CONV_CH)),
        compiler_params=_params("parallel", "parallel"),
        name="inproj",
    )(x, nw, sc, sh, w_bf, cos, sa, sb, qw, kw)


def _attn_kernel(*refs, window):
    if window:
        q_ref, kp_ref, kc_ref, kn_ref, vp_ref, vc_ref, vn_ref, kx_ref, vx_ref, sink_ref, o_ref = refs
    else:
        q_ref, kx_ref, vx_ref, sink_ref, o_ref = refs
    blk = ATT_BLOCK
    n_ctx = kx_ref.shape[0]
    nk = 3 * blk + n_ctx if window else n_ctx
    if window:
        i = pl.program_id(1)
        key = lax.broadcasted_iota(jnp.int32, (blk, 2 * blk), 0)
        qry = lax.broadcasted_iota(jnp.int32, (blk, 2 * blk), 1) & (blk - 1)
        mask_prev = (key >= qry) & (i > 0)
        mask_next = (key <= qry) & (i < pl.num_programs(1) - 1)
    first_head = lax.broadcasted_iota(jnp.int32, (blk, LANES), 1) < HEAD_DIM
    ones = jnp.ones((2 * SUBLANES, nk), BF16)
    for g in range(ATT_KV_HEADS):
        gs = slice(g * LANES, (g + 1) * LANES)
        vs = slice(g * HEAD_DIM, (g + 1) * HEAD_DIM)
        if window:
            kcat = jnp.concatenate([kp_ref[:, gs], kc_ref[:, gs], kn_ref[:, gs], kx_ref[:, gs]], axis=0)
            vt = jnp.concatenate([vp_ref[vs, :], vc_ref[vs, :], vn_ref[vs, :], vx_ref[vs, :]], axis=1)
        else:
            kcat, vt = kx_ref[:, gs], vx_ref[vs, :]
        vaug = jnp.concatenate([vt, ones], axis=0)
        for pr in range(2):
            pair = 2 * g + pr
            cs = slice(pair * LANES, (pair + 1) * LANES)
            qp = q_ref[:, cs]
            zero = jnp.zeros_like(qp)
            w = jnp.concatenate([jnp.where(first_head, qp, zero), jnp.where(first_head, zero, qp)], axis=0)
            s = lax.dot_general(kcat, w, (((1,), (1,)), ((), ())), preferred_element_type=F32)
            if window:
                parts = [jnp.where(mask_prev, s[0:blk], NEG_INF), s[blk:2 * blk],
                         jnp.where(mask_next, s[2 * blk:3 * blk], NEG_INF), s[3 * blk:]]
            else:
                parts = [s]
            snk = sink_ref[pair:pair + 1, :]
            m = snk
            for part in parts:
                m = jnp.maximum(m, jnp.max(part, axis=0, keepdims=True))
            p = jnp.concatenate([jnp.exp(part - m).astype(BF16) for part in parts], axis=0)
            o = jnp.dot(vaug, p, preferred_element_type=F32)
            den = o[HEAD_DIM:HEAD_DIM + 1, :] + jnp.exp(snk - m)
            on = o[0:HEAD_DIM, :] * (1.0 / den)
            ot = jnp.concatenate([on[:, 0:blk], on[:, blk:2 * blk]], axis=0)
            o_ref[:, cs] = ot.T.astype(BF16)


def _attention(q, k, vt, kx, vxt, sink_tab, window):
    b, s, _ = q.shape
    blk = ATT_BLOCK
    nb = s // blk
    n_ctx = kx.shape[1]
    qspec = pl.BlockSpec((None, blk, D_ATT), lambda bi, i: (bi, i, 0))
    kctx = pl.BlockSpec((None, n_ctx, 2 * LANES), lambda bi, i: (bi, 0, 0))
    vctx = pl.BlockSpec((None, ATT_KV_W, n_ctx), lambda bi, i: (bi, 0, 0))
    snk = pl.BlockSpec(sink_tab.shape, lambda bi, i: (0, 0))
    if window:
        prev = lambda i: jnp.maximum(i - 1, 0)
        nxt = lambda i: jnp.minimum(i + 1, nb - 1)
        same = lambda i: i
        kspec = lambda f: pl.BlockSpec((None, blk, 2 * LANES), lambda bi, i: (bi, f(i), 0))
        vspec = lambda f: pl.BlockSpec((None, ATT_KV_W, blk), lambda bi, i: (bi, 0, f(i)))
        in_specs = [qspec, kspec(prev), kspec(same), kspec(nxt), vspec(prev), vspec(same), vspec(nxt),
                    kctx, vctx, snk]
        args = (q, k, k, k, vt, vt, vt, kx, vxt, sink_tab)
    else:
        in_specs = [qspec, kctx, vctx, snk]
        args = (q, kx, vxt, sink_tab)
    return pl.pallas_call(
        functools.partial(_attn_kernel, window=window),
        out_shape=jax.ShapeDtypeStruct((b, s, D_ATT), BF16),
        grid=(b, nb),
        in_specs=in_specs,
        out_specs=qspec,
        compiler_params=_params("parallel", "parallel"),
        name="attention_window" if window else "attention_ctx",
    )(*args)


def _ret_state_kernel(lgl_ref, k_ref, v_ref, s0_ref, r_ref, fin_ref, s_scr, *, cb, reverse):
    c = pl.program_id(1)
    L = RET_CHUNK

    @pl.when(c == 0)
    def _():
        s_scr[...] = s0_ref[...]

    jj = lax.broadcasted_iota(jnp.int32, (L, LANES), 0).astype(F32)
    expo = jj if reverse else (L - 1.0) - jj
    same_head = (lax.broadcasted_iota(jnp.int32, (LANES, LANES), 0) // HEAD_DIM
                 == lax.broadcasted_iota(jnp.int32, (LANES, LANES), 1) // HEAD_DIM)
    for pr in range(RET_HEADS // 2):
        cs = slice(pr * LANES, (pr + 1) * LANES)
        lgl = lgl_ref[pr]
        kdec = jnp.exp(expo * lgl)
        cdec = jnp.exp(float(L) * lgl)
        state = s_scr[pr]
        for t in range(cb):
            cc = cb - 1 - t if reverse else t
            rows = slice(cc * L, (cc + 1) * L)
            r_ref[cc, pr] = state
            kd = k_ref[rows, cs].astype(F32) * kdec
            u = jnp.dot(kd.T.astype(BF16), v_ref[rows, cs], preferred_element_type=F32)
            state = cdec * state + jnp.where(same_head, u, 0.0)
        s_scr[pr] = state

    @pl.when(c == pl.num_programs(1) - 1)
    def _():
        fin_ref[...] = s_scr[...]


def _ret_states(lgl, rk, rv, s0, reverse):
    b, s, _ = rk.shape
    nc = s // RET_CHUNK
    cb = _pick_tile(nc, 8)
    nblk = nc // cb
    npair = RET_HEADS // 2
    blk_idx = (lambda c: nblk - 1 - c) if reverse else (lambda c: c)
    tok = pl.BlockSpec((None, cb * RET_CHUNK, D_RET), lambda bi, c: (bi, blk_idx(c), 0))
    st = pl.BlockSpec((None, npair, LANES, LANES), lambda bi, c: (bi, 0, 0, 0))
    return pl.pallas_call(
        functools.partial(_ret_state_kernel, cb=cb, reverse=reverse),
        out_shape=(jax.ShapeDtypeStruct((b, nc, npair, LANES, LANES), F32),
                   jax.ShapeDtypeStruct((b, npair, LANES, LANES), F32)),
        grid=(b, nblk),
        in_specs=[pl.BlockSpec(lgl.shape, lambda bi, c: (0, 0, 0)), tok, tok, st],
        out_specs=(pl.BlockSpec((None, cb, npair, LANES, LANES),
                                lambda bi, c: (bi, blk_idx(c), 0, 0, 0)), st),
        scratch_shapes=[pltpu.VMEM((npair, LANES, LANES), F32)],
        compiler_params=_params("parallel", "arbitrary"),
        name="ret_state_bwd" if reverse else "ret_state_fwd",
    )(lgl, rk, rv, s0)


def _ret_out_kernel(lgf_ref, lgb_ref, q_ref, k_ref, v_ref, g_ref, rf_ref, rb_ref, gnw_ref, o_ref, *, cb):
    L = RET_CHUNK
    i0 = lax.broadcasted_iota(jnp.int32, (L, LANES), 0).astype(F32)
    i1 = lax.broadcasted_iota(jnp.int32, (L, LANES), 1).astype(F32)
    diff = i0 - i1
    lo = lax.broadcasted_iota(jnp.int32, (L, LANES), 1) < HEAD_DIM
    grp = _head_group_matrix()
    inv = 1.0 / HEAD_DIM
    tabs = []
    for pr in range(RET_HEADS // 2):
        lgf, lgb = lgf_ref[pr], lgb_ref[pr]
        dmat = [jnp.where(diff >= 0.0,
                          jnp.exp(jnp.maximum(diff, 0.0) * lgf[:, a:a + 1]),
                          jnp.exp(jnp.maximum(-diff, 0.0) * lgb[:, a:a + 1]))
                for a in (0, HEAD_DIM)]
        tabs.append((jnp.exp((i0 + 1.0) * lgf), jnp.exp((float(L) - i0) * lgb),
                     jnp.concatenate(dmat, axis=0), gnw_ref[:, pr * LANES:(pr + 1) * LANES]))

    def chunk(t, carry):
        rows = pl.ds(pl.multiple_of(t * L, L), L)
        for pr in range(RET_HEADS // 2):
            cs = slice(pr * LANES, (pr + 1) * LANES)
            xif, xib, dmat, gnw = tabs[pr]
            qp, kp, vp = q_ref[rows, cs], k_ref[rows, cs], v_ref[rows, cs]
            zero = jnp.zeros_like(kp)
            qz = jnp.concatenate([jnp.where(lo, qp, zero), jnp.where(lo, zero, qp)], axis=0)
            a = lax.dot_general(qz, kp, (((1,), (1,)), ((), ())), preferred_element_type=F32) * dmat
            qf = qp.astype(F32)
            lhs = jnp.concatenate([a[0:L].astype(BF16), a[L:].astype(BF16),
                                   (qf * xif).astype(BF16), (qf * xib).astype(BF16)], axis=1)
            rhs = jnp.concatenate([jnp.where(lo, vp, zero), jnp.where(lo, zero, vp),
                                   rf_ref[t, pr].astype(BF16), rb_ref[t, pr].astype(BF16)], axis=0)
            acc = jnp.dot(lhs, rhs, preferred_element_type=F32)
            dl = acc - _per_head_sum(acc, grp) * inv
            var = _per_head_sum(dl * dl, grp) * inv
            y = dl * lax.rsqrt(var + EPS) * gnw
            gt = g_ref[rows, cs]
            o_ref[rows, cs] = (gt * _sigmoid(gt) * y).astype(BF16)
        return carry

    lax.fori_loop(0, cb, chunk, 0, unroll=4)


def _ret_out(lgf, lgb, rq, rk, rv, rg, rf, rb, gnw):
    b, s, _ = rq.shape
    nc = s // RET_CHUNK
    cb = _pick_tile(nc, 8)
    npair = RET_HEADS // 2
    tok = pl.BlockSpec((None, cb * RET_CHUNK, D_RET), lambda bi, c: (bi, c, 0))
    st = pl.BlockSpec((None, cb, npair, LANES, LANES), lambda bi, c: (bi, c, 0, 0, 0))
    lg = pl.BlockSpec(lgf.shape, lambda bi, c: (0, 0, 0))
    return pl.pallas_call(
        functools.partial(_ret_out_kernel, cb=cb),
        out_shape=jax.ShapeDtypeStruct((b, s, D_RET), BF16),
        grid=(b, nc // cb),
        in_specs=[lg, lg, tok, tok, tok, tok, st, st, pl.BlockSpec((1, D_RET), lambda bi, c: (0, 0))],
        out_specs=tok,
        compiler_params=_params("parallel", "parallel"),
        name="ret_out",
    )(lgf, lgb, rq, rk, rv, rg, rf, rb, gnw)


def _conv_kernel(prev_ref, cur_ref, next_ref, w_ref, b_ref, lnw_ref, lnb_ref, o_ref, xpad, shifted, hbuf):
    i = pl.program_id(1)
    tm = cur_ref.shape[0]
    halo = CONV_HALO
    first = i == 0
    last = i == pl.num_programs(1) - 1
    xpad[0:halo, :] = jnp.where(first, 0.0, prev_ref[...])
    xpad[halo:halo + tm, :] = cur_ref[...]
    xpad[halo + tm:2 * halo + tm, :] = jnp.where(last, 0.0, next_ref[...])
    span = tm + 2 * halo - SUBLANES
    for r in range(SUBLANES):
        shifted[r] = xpad[r:r + span, :]
    base = halo - CONV_WIDTH // 2
    bias = b_ref[...]

    def sub(sb, carry):
        r0 = pl.multiple_of(sb * CONV_SUB, CONV_SUB)
        chains = [None] * CONV_CHAINS
        for w in range(CONV_WIDTH):
            off = base + w
            xs = shifted[off % SUBLANES, pl.ds(r0 + (off // SUBLANES) * SUBLANES, CONV_SUB), :]
            term = xs * w_ref[w:w + 1, :]
            c = w % CONV_CHAINS
            chains[c] = term if chains[c] is None else chains[c] + term
        hbuf[pl.ds(r0, CONV_SUB), :] = (chains[0] + chains[1]) + (chains[2] + chains[3]) + bias
        return carry

    lax.fori_loop(0, tm // CONV_SUB, sub, 0, unroll=2)
    h = hbuf[...]
    mu = jnp.mean(h, axis=-1, keepdims=True)
    dl = h - mu
    var = jnp.mean(dl * dl, axis=-1, keepdims=True)
    y = dl * lax.rsqrt(var + EPS) * lnw_ref[...] + lnb_ref[...]
    o_ref[...] = (y * _sigmoid(y)).astype(BF16)


def _conv(cv, w, bias, lnw, lnb):
    b, s, ch = cv.shape
    tm = _pick_tile(s, 512)
    hpb = tm // CONV_HALO
    nh = s // CONV_HALO
    row = pl.BlockSpec((1, ch), lambda bi, i: (0, 0))
    span = tm + 2 * CONV_HALO - SUBLANES
    return pl.pallas_call(
        _conv_kernel,
        out_shape=jax.ShapeDtypeStruct((b, s, ch), BF16),
        grid=(b, s // tm),
        in_specs=[pl.BlockSpec((None, CONV_HALO, ch), lambda bi, i: (bi, jnp.maximum(i * hpb - 1, 0), 0)),
                  pl.BlockSpec((None, tm, ch), lambda bi, i: (bi, i, 0)),
                  pl.BlockSpec((None, CONV_HALO, ch),
                               lambda bi, i: (bi, jnp.minimum((i + 1) * hpb, nh - 1), 0)),
                  pl.BlockSpec(w.shape, lambda bi, i: (0, 0)), row, row, row],
        out_specs=pl.BlockSpec((None, tm, ch), lambda bi, i: (bi, i, 0)),
        scratch_shapes=[pltpu.VMEM((tm + 2 * CONV_HALO, ch), F32),
                        pltpu.VMEM((SUBLANES, span, ch), F32),
                        pltpu.VMEM((tm, ch), F32)],
        compiler_params=_params("parallel", "parallel"),
        name="conv",
    )(cv, cv, cv, w, bias, lnw, lnb)


def _outproj_kernel(*refs, route):
    if route:
        (att_ref, ret_ref, cnv_ref, w_ref, x_ref, g1_ref, nw_ref, sc_ref, sh_ref, rcat_ref,
         xo_ref, h_ref, rt_ref) = refs
    else:
        att_ref, ret_ref, cnv_ref, w_ref, x_ref, g1_ref, nw_ref, sc_ref, sh_ref, xo_ref, h_ref = refs
    y = (jnp.dot(att_ref[...], w_ref[0:D_ATT, :], preferred_element_type=F32)
         + jnp.dot(ret_ref[...], w_ref[D_ATT:D_ATT + D_RET, :], preferred_element_type=F32)
         + jnp.dot(cnv_ref[...], w_ref[D_ATT + D_RET:, :], preferred_element_type=F32))
    xn = x_ref[...] + g1_ref[...] * y
    xo_ref[...] = xn
    h = _modulated_rms(xn, nw_ref[...], sc_ref[...], sh_ref[...])
    h_ref[...] = _pack_bf16_pairs(h) if route else h.astype(BF16)
    if route:
        hi = h.astype(BF16)
        lo = (h - hi.astype(F32)).astype(BF16)
        tm = h.shape[0]
        r = jnp.dot(jnp.concatenate([hi, lo], axis=0), rcat_ref[...], preferred_element_type=F32)
        logits = (r[0:tm, 0:LANES] + r[0:tm, LANES:]) + (r[tm:, 0:LANES] + r[tm:, LANES:])
        lane = lax.broadcasted_iota(jnp.int32, logits.shape, 1).astype(F32)
        logits = jnp.where(lane < N_EXPERTS, logits, NEG_INF)
        m1 = jnp.max(logits, axis=-1, keepdims=True)
        i1 = jnp.min(jnp.where(logits == m1, lane, float(LANES)), axis=-1, keepdims=True)
        rest = jnp.where(lane == i1, NEG_INF, logits)
        m2 = jnp.max(rest, axis=-1, keepdims=True)
        i2 = jnp.min(jnp.where(rest == m2, lane, float(LANES)), axis=-1, keepdims=True)
        e2 = jnp.exp(m2 - m1)
        w1 = 1.0 / (1.0 + e2)
        w2 = e2 / (1.0 + e2)
        rt_ref[...] = jnp.where(lane == 0.0, i1,
                                jnp.where(lane == 1.0, i2,
                                          jnp.where(lane == 2.0, w1, jnp.where(lane == 3.0, w2, 0.0))))


def _outproj(att, ret, cnv, w_bf, x, g1, nw, sc, sh, router=None):
    b, s, d = x.shape
    tm = _pick_tile(s, 512)
    tok = lambda n: pl.BlockSpec((None, tm, n), lambda bi, i: (bi, i, 0))
    per_b = pl.BlockSpec((None, 1, d), lambda bi, i: (bi, 0, 0))
    row = pl.BlockSpec((1, d), lambda bi, i: (0, 0))
    in_specs = [tok(D_ATT), tok(D_RET), tok(CONV_CH), pl.BlockSpec(w_bf.shape, lambda bi, i: (0, 0)),
                tok(d), per_b, row, per_b, per_b]
    args = [att, ret, cnv, w_bf, x, g1, nw, sc, sh]
    h_shape = (b, s, d // 2) if router is not None else (b, s, d)
    out_shape = [jax.ShapeDtypeStruct((b, s, d), F32),
                 jax.ShapeDtypeStruct(h_shape, jnp.int32 if router is not None else BF16)]
    out_specs = [tok(d), tok(h_shape[2])]
    if router is not None:
        rpad = jnp.zeros((d, LANES), F32).at[:, :N_EXPERTS].set(router)
        rhi = rpad.astype(BF16)
        rlo = (rpad - rhi.astype(F32)).astype(BF16)
        in_specs += [pl.BlockSpec((d, 2 * LANES), lambda bi, i: (0, 0))]
        args += [jnp.concatenate([rhi, rlo], axis=1)]
        out_shape.append(jax.ShapeDtypeStruct((b, s, LANES), F32))
        out_specs.append(tok(LANES))
    return pl.pallas_call(
        functools.partial(_outproj_kernel, route=router is not None),
        out_shape=tuple(out_shape),
        grid=(b, s // tm),
        in_specs=in_specs,
        out_specs=tuple(out_specs),
        compiler_params=_params("parallel", "parallel"),
        name="outproj_route" if router is not None else "outproj",
    )(*args)


def _swiglu_chunks(h, wg_ref, wu_ref, wd_ref, sub):
    total = None
    for c0 in range(0, wg_ref.shape[1], sub):
        cs = slice(c0, min(c0 + sub, wg_ref.shape[1]))
        gate = jnp.dot(h, wg_ref[:, cs].astype(BF16), preferred_element_type=F32)
        up = jnp.dot(h, wu_ref[:, cs].astype(BF16), preferred_element_type=F32)
        act = (gate * _sigmoid(gate) * up).astype(BF16)
        part = jnp.dot(act, wd_ref[cs, :].astype(BF16), preferred_element_type=F32)
        total = part if total is None else total + part
    return total


def _ffn_kernel(h_ref, wg_ref, wu_ref, wd_ref, x_ref, g2_ref, o_ref):
    y = _swiglu_chunks(h_ref[...], wg_ref, wu_ref, wd_ref, FFN_SUB)
    o_ref[...] = x_ref[...] + g2_ref[...] * y


def _ffn(h, wg, wu, wd, x, g2):
    b, s, d = x.shape
    f = wg.shape[1]
    tm = _pick_tile(s, 512)
    tok = lambda: pl.BlockSpec((None, tm, d), lambda bi, i: (bi, i, 0))
    res = lambda shape: pl.BlockSpec(shape, lambda bi, i: (0, 0), pipeline_mode=pl.Buffered(1))
    return pl.pallas_call(
        _ffn_kernel,
        out_shape=jax.ShapeDtypeStruct((b, s, d), F32),
        grid=(b, s // tm),
        in_specs=[tok(), res((d, f)), res((d, f)), res((f, d)), tok(),
                  pl.BlockSpec((None, 1, d), lambda bi, i: (bi, 0, 0))],
        out_specs=tok(),
        compiler_params=_params("parallel", "parallel"),
        name="ffn_dense",
    )(h, wg, wu, wd, x, g2)


def _gather_rows(idx, src):
    n = idx.shape[0]
    d = src.shape[1]
    mesh = plsc.VectorSubcoreMesh(core_axis_name="core", subcore_axis_name="subcore")

    @pl.kernel(out_type=jax.ShapeDtypeStruct((n, d), src.dtype), mesh=mesh, name="moe_gather")
    def gather(src_hbm, idx_hbm, out_hbm):
        def body(idx_vmem, out_vmem):
            pltpu.sync_copy(src_hbm.at[idx_vmem.at[0, pl.ds(0, SC_WINDOW)]], out_vmem)

        pltpu.emit_pipeline(
            body,
            grid=(n // SC_WINDOW,),
            in_specs=[pl.BlockSpec((1, LANES), lambda i: (i, 0))],
            out_specs=[pl.BlockSpec((SC_WINDOW, d), lambda i: (i, 0))],
            core_axis_name=("core", "subcore"),
            dimension_semantics=(pltpu.PARALLEL,),
        )(idx_hbm, out_hbm)

    idx_rows = jnp.pad(idx.reshape(n // SC_WINDOW, SC_WINDOW), ((0, 0), (0, LANES - SC_WINDOW)))
    return gather(src, idx_rows)


def _invert_rows(dest, n_rows):
    n = dest.shape[0]
    n_tok = n // 2
    mesh = plsc.VectorSubcoreMesh(core_axis_name="core", subcore_axis_name="subcore")

    @pl.kernel(out_type=jax.ShapeDtypeStruct((n_rows,), jnp.int32), mesh=mesh, name="moe_row_src",
               scratch_types=[pltpu.VMEM((n,), jnp.int32), pltpu.VMEM((n_rows,), jnp.int32)],
               compiler_params=pltpu.CompilerParams(needs_layout_passes=False))
    def invert(dest_hbm, out_hbm, dest_vmem, rows_vmem):
        @pl.when((lax.axis_index("core") == 0) & (lax.axis_index("subcore") == 0))
        def _():
            pltpu.sync_copy(dest_hbm, dest_vmem)

            @pl.loop(0, n_rows, step=SC_LANES)
            def _(r):
                rows_vmem[pl.ds(r, SC_LANES)] = lax.rem(lax.iota(jnp.int32, SC_LANES) + r, n_tok)

            @pl.loop(0, n, step=SC_LANES)
            def _(a):
                tok = lax.iota(jnp.int32, SC_LANES) + a
                tok = jnp.where(tok >= n_tok, tok - n_tok, tok)
                plsc.store_scatter(rows_vmem, [dest_vmem[pl.ds(a, SC_LANES)]], tok)

            pltpu.sync_copy(rows_vmem, out_hbm)

    return invert(dest)


def _moe_kernel(te_ref, nu_ref, x_ref, wg_ref, wu_ref, wd_ref, *rest, tile0):
    o_ref, xb, acc = rest[-3:]
    t = pl.program_id(0) + tile0
    j = pl.program_id(1)
    nj = pl.num_programs(1)
    used = t < nu_ref[0]

    @pl.when(used & (j == 0))
    def _():
        lo, hi = _unpack_bf16_pairs(x_ref[...])
        half = lo.shape[1]
        xb[:, 0:half] = lo.astype(BF16)
        xb[:, half:] = hi.astype(BF16)
        acc[...] = jnp.zeros_like(acc)

    @pl.when(used)
    def _():
        acc[...] += _swiglu_chunks(xb[...], wg_ref, wu_ref, wd_ref, MOE_SUB)

    @pl.when(used & (j == nj - 1))
    def _():
        o_ref[...] = _pack_bf16_pairs(acc[...])

    @pl.when(jnp.logical_not(used) & (j == nj - 1))
    def _():
        o_ref[...] = jnp.zeros_like(o_ref)


def _moe_grouped(tile_expert, n_used, xs, wg, wu, wd, y_prev, tile0, n_rows):
    p, dp = xs.shape
    d = 2 * dp
    f = wg.shape[2]
    tm = MOE_TM
    tf = MOE_TF
    nj = f // tf

    def jj(t, j, te, nu):
        return jnp.where(t + tile0 < nu[0], j, nj - 1)

    in_specs = [pl.BlockSpec((tm, dp), lambda t, j, te, nu: (t, 0)),
                pl.BlockSpec((None, d, tf), lambda t, j, te, nu: (te[t + tile0], 0, jj(t, j, te, nu))),
                pl.BlockSpec((None, d, tf), lambda t, j, te, nu: (te[t + tile0], 0, jj(t, j, te, nu))),
                pl.BlockSpec((None, tf, d), lambda t, j, te, nu: (te[t + tile0], jj(t, j, te, nu), 0))]
    args = [tile_expert, n_used, xs, wg, wu, wd]
    aliases = {}
    if y_prev is not None:
        in_specs.append(pl.BlockSpec(memory_space=pl.ANY))
        args.append(y_prev)
        aliases = {len(args) - 1: 0}
    return pl.pallas_call(
        functools.partial(_moe_kernel, tile0=tile0),
        out_shape=jax.ShapeDtypeStruct((n_rows, dp), jnp.int32),
        grid_spec=pltpu.PrefetchScalarGridSpec(
            num_scalar_prefetch=2,
            grid=(p // tm, nj),
            in_specs=in_specs,
            out_specs=pl.BlockSpec((tm, dp), lambda t, j, te, nu: (t + tile0, 0)),
            scratch_shapes=[pltpu.VMEM((tm, d), BF16), pltpu.VMEM((tm, d), F32)]),
        input_output_aliases=aliases,
        compiler_params=_params("arbitrary", "arbitrary"),
        name="moe_grouped",
    )(*args)


def _combine_kernel(y1_ref, y2_ref, x_ref, g2_ref, rt_ref, o_ref):
    rt = rt_ref[...]
    w1, w2 = rt[:, 2:3], rt[:, 3:4]
    half = y1_ref.shape[1]
    for k, (a, c) in enumerate(zip(_unpack_bf16_pairs(y1_ref[...]), _unpack_bf16_pairs(y2_ref[...]))):
        cs = slice(k * half, (k + 1) * half)
        o_ref[:, cs] = x_ref[:, cs] + g2_ref[:, cs] * (w1 * a + w2 * c)


def _moe_combine(yg, x, g2, route):
    b, s, d = x.shape
    tm = _pick_tile(s, 512)
    tok = lambda n: pl.BlockSpec((None, tm, n), lambda bi, i: (bi, i, 0))
    return pl.pallas_call(
        _combine_kernel,
        out_shape=jax.ShapeDtypeStruct((b, s, d), F32),
        grid=(b, s // tm),
        in_specs=[tok(d // 2), pl.BlockSpec((None, tm, d // 2), lambda bi, i: (b + bi, i, 0)), tok(d),
                  pl.BlockSpec((None, 1, d), lambda bi, i: (bi, 0, 0)), tok(LANES)],
        out_specs=tok(d),
        compiler_params=_params("parallel", "parallel"),
        name="moe_combine",
    )(yg, yg, x, g2, route)


def _moe(h, route, x, g2, wg, wu, wd):
    b, s, d = x.shape
    n_tok = b * s
    tm = MOE_TM
    rt = route.reshape(n_tok, LANES)
    flat_e = jnp.concatenate([rt[:, 0], rt[:, 1]]).astype(jnp.int32)
    onehot = (flat_e[:, None] == jnp.arange(N_EXPERTS, dtype=jnp.int32)[None, :]).astype(jnp.int32)
    csum = jnp.cumsum(onehot, axis=0)
    rank = jnp.sum(csum * onehot, axis=1) - 1
    counts = csum[-1]
    tiles_e = (counts + tm - 1) // tm
    tiles_cum = jnp.cumsum(tiles_e)
    row_start = (tiles_cum - tiles_e) * tm
    dest = jnp.sum(onehot * row_start[None, :], axis=1) + rank
    n_tiles = 2 * n_tok // tm + N_EXPERTS
    tile_ids = jnp.arange(n_tiles, dtype=jnp.int32)
    tile_expert = jnp.sum((tile_ids[:, None] >= tiles_cum[None, :]).astype(jnp.int32), axis=1)
    last_e = jnp.max(jnp.where(tiles_e > 0, jnp.arange(N_EXPERTS, dtype=jnp.int32), 0))
    tile_expert = jnp.minimum(tile_expert, last_e).astype(jnp.int32)
    n_used = tiles_cum[-1:].astype(jnp.int32)
    dest = dest.astype(jnp.int32)
    row_src = _invert_rows(dest, n_tiles * tm)
    n_chunks = max(c for c in range(1, MOE_CHUNKS + 1) if n_tiles % c == 0)
    tiles_c = n_tiles // n_chunks
    h_flat = h.reshape(n_tok, d // 2)
    xs = [_gather_rows(row_src[c * tiles_c * tm:(c + 1) * tiles_c * tm], h_flat) for c in range(n_chunks)]
    y = None
    for c in range(n_chunks):
        y = _moe_grouped(tile_expert, n_used, xs[c], wg, wu, wd, y, c * tiles_c, n_tiles * tm)
    yg = _gather_rows(dest, y).reshape(2 * b, s, d // 2)
    return _moe_combine(yg, x, g2, route)


def _rope_tables(n):
    rows = n // GRID_W
    r = jnp.repeat(jnp.arange(rows), GRID_W).astype(F32)
    col = jnp.tile(jnp.arange(GRID_W), rows).astype(F32)
    freqs = ROPE_BASE ** (-jnp.arange(ROPE_FREQS, dtype=F32) / ROPE_FREQS)
    ang = jnp.stack([r[:, None] * freqs, col[:, None] * freqs], axis=1)
    ang = jnp.repeat(ang[:, :, None, :], 2, axis=2).reshape(n, HEAD_DIM)
    ang = jnp.tile(ang, (1, LANES // HEAD_DIM))
    cos, sin = jnp.cos(ang), jnp.sin(ang)
    first_half = (jnp.arange(LANES) % (2 * ROPE_FREQS)) < ROPE_FREQS
    return cos, jnp.where(first_half, -sin, 0.0), jnp.where(first_half, 0.0, sin)


def _lane_rows(lg):
    return jnp.repeat(lg.astype(F32), HEAD_DIM).reshape(RET_HEADS // 2, 1, LANES)


def kernel(x, c, ctx, c_ctx, ada_w, ada_b, norm1_w, norm2_w, w_in, w_out, q_norm_w, k_norm_w,
           attn_sink, ret_decay_f, ret_decay_b, ret_gn_w, conv_w, conv_b, conv_ln_w, conv_ln_b,
           ffn_w_gate, ffn_w_up, ffn_w_down, router_w, moe_w_gate, moe_w_up, moe_w_down):
    b, n, d = x.shape
    n_ctx = ctx.shape[1]
    depth = ada_w.shape[0]
    cond = jnp.zeros((SUBLANES, d), F32).at[0:b].set(c).at[b].set(c_ctx)
    mods = _adaln(cond, ada_w, ada_b).reshape(depth, SUBLANES, 6, d)
    cos, sa, sb = _rope_tables(n)
    ones_c = jnp.ones((n_ctx, LANES), F32)
    zeros_c = jnp.zeros((n_ctx, LANES), F32)
    zero_state = jnp.zeros((b, RET_HEADS // 2, LANES, LANES), F32)
    row = lambda v: v.reshape(1, -1)
    moe_bf = (moe_w_gate.astype(BF16), moe_w_up.astype(BF16), moe_w_down.astype(BF16))
    x, moe_bf = lax.optimization_barrier((x, moe_bf))
    for l in range(depth):
        last = l == depth - 1
        m_lat = [mods[l, 0:b, k][:, None, :] for k in range(6)]
        m_ctx = [jnp.broadcast_to(mods[l, b, k][None, None, :], (b, 1, d)) for k in range(6)]
        w_in_bf = w_in[l].astype(BF16)
        w_out_bf = w_out[l].astype(BF16)
        qw = row(jnp.tile(q_norm_w[l], LANES // HEAD_DIM))
        kw = row(jnp.tile(k_norm_w[l], LANES // HEAD_DIM))
        lgf = _lane_rows(jax.nn.log_sigmoid(ret_decay_f[l].astype(F32)))
        lgb = _lane_rows(jax.nn.log_sigmoid(ret_decay_b[l].astype(F32)))
        sink_tab = jnp.repeat(attn_sink[l].astype(F32), ATT_BLOCK).reshape(ATT_Q_HEADS // 2, 2 * ATT_BLOCK)

        q, k, v, rk, rv, rq, rg, cv = _inproj(x, row(norm1_w[l]), m_lat[1], m_lat[0], w_in_bf,
                                               cos, sa, sb, qw, kw)
        qc, kc, vc, rkc, rvc, rqc, rgc, cvc = _inproj(ctx, row(norm1_w[l]), m_ctx[1], m_ctx[0], w_in_bf,
                                                       ones_c, zeros_c, zeros_c, qw, kw)
        rf_c, s_f = _ret_states(lgf, rkc, rvc, zero_state, reverse=False)
        rb_c, s_b = _ret_states(lgb, rkc, rvc, zero_state, reverse=True)
        rf, _ = _ret_states(lgf, rk, rv, s_f, reverse=False)
        rb, _ = _ret_states(lgb, rk, rv, s_b, reverse=True)

        att = _attention(q, k, v, kc, vc, sink_tab, window=True)
        ret = _ret_out(lgf, lgb, rq, rk, rv, rg, rf, rb, row(ret_gn_w[l]))
        cnv = _conv(cv, conv_w[l], row(conv_b[l]), row(conv_ln_w[l]), row(conv_ln_b[l]))

        if l % 2 == 0:
            i = l // 2
            wg, wu, wd = ffn_w_gate[i].astype(BF16), ffn_w_up[i].astype(BF16), ffn_w_down[i].astype(BF16)
            x_mid, h2 = _outproj(att, ret, cnv, w_out_bf, x, m_lat[2], row(norm2_w[l]), m_lat[4], m_lat[3])
            x_new = _ffn(h2, wg, wu, wd, x_mid, m_lat[5])
        else:
            i = l // 2
            wg, wu, wd = moe_bf[0][i], moe_bf[1][i], moe_bf[2][i]
            x_mid, h2, route = _outproj(att, ret, cnv, w_out_bf, x, m_lat[2], row(norm2_w[l]),
                                        m_lat[4], m_lat[3], router=router_w[i])
            x_new = _moe(h2, route, x_mid, m_lat[5], wg, wu, wd)

        if not last:
            att_c = _attention(qc, None, None, kc, vc, sink_tab, window=False)
            ret_c = _ret_out(lgf, lgb, rqc, rkc, rvc, rgc, rf_c, rb_c, row(ret_gn_w[l]))
            cnv_c = _conv(cvc, conv_w[l], row(conv_b[l]), row(conv_ln_w[l]), row(conv_ln_b[l]))
            if l % 2 == 0:
                c_mid, h2c = _outproj(att_c, ret_c, cnv_c, w_out_bf, ctx, m_ctx[2], row(norm2_w[l]),
                                      m_ctx[4], m_ctx[3])
                ctx = _ffn(h2c, wg, wu, wd, c_mid, m_ctx[5])
            else:
                c_mid, h2c, route_c = _outproj(att_c, ret_c, cnv_c, w_out_bf, ctx, m_ctx[2],
                                               row(norm2_w[l]), m_ctx[4], m_ctx[3],
                                               router=router_w[i])
                ctx = _moe(h2c, route_c, c_mid, m_ctx[5], wg, wu, wd)
        x = x_new
    return x
```

```python
import functools

import jax
import jax.numpy as jnp
from jax import lax
from jax.experimental import pallas as pl
from jax.experimental.pallas import tpu as pltpu
from jax.experimental.pallas import tpu_sc as plsc

F32 = jnp.float32
BF16 = jnp.bfloat16

GRID_W = 64
HEAD_DIM = 64
ATT_Q_HEADS = 8
ATT_KV_HEADS = 2
ATT_WINDOW = 128
ATT_BLOCK = 128
RET_HEADS = 4
RET_CHUNK = 128
RET_K_SCALE = HEAD_DIM ** -0.5
ATT_SCALE = HEAD_DIM ** -0.5
CONV_CH = 256
CONV_WIDTH = 31
ROPE_BASE = 10000.0
ROPE_FREQS = HEAD_DIM // 4
D_ATT = ATT_Q_HEADS * HEAD_DIM
D_RET = RET_HEADS * HEAD_DIM
ATT_KV_W = ATT_KV_HEADS * HEAD_DIM
C_ATT_K = 0
C_ATT_V = C_ATT_K + ATT_KV_W
C_RET_K = C_ATT_V + ATT_KV_W
C_RET_V = C_RET_K + D_RET
C_ATT_Q = C_RET_V + D_RET
C_RET_Q = C_ATT_Q + D_ATT
C_RET_G = C_RET_Q + D_RET
C_CONV = C_RET_G + D_RET
N_EXPERTS = 8
EPS = 1e-6
NEG_INF = -1e30

LANES = 128
SUBLANES = 8
VMEM_LIMIT = 48 * 1024 * 1024
CONV_HALO = 16
CONV_SUB = 16
CONV_CHAINS = 4
RET_UNROLL = 4
MOE_TM = 512
MOE_CHUNKS = 2
MOE_SUB = 512
FFN_SUB = 512
MOE_TF = 1792
SC_LANES = 16
SC_WINDOW = 64


def _params(*sem):
    return pltpu.CompilerParams(dimension_semantics=sem, vmem_limit_bytes=VMEM_LIMIT)


def _sigmoid(x):
    return 1.0 / (1.0 + jnp.exp(-x))


def _pack_bf16_pairs(v):
    c = v.shape[1] // 2
    bits = pltpu.bitcast(v.astype(BF16).astype(F32), jnp.uint32)
    packed = (bits[:, c:] & jnp.uint32(0xFFFF0000)) | (bits[:, :c] >> 16)
    return pltpu.bitcast(packed, jnp.int32)


def _unpack_bf16_pairs(p):
    bits = pltpu.bitcast(p, jnp.uint32)
    return pltpu.bitcast(bits << 16, F32), pltpu.bitcast(bits & jnp.uint32(0xFFFF0000), F32)


def _pick_tile(n, pref):
    t = min(n, pref)
    assert n % t == 0, (n, t)
    return t


def _adaln_kernel(c_ref, w_ref, b_ref, o_ref):
    c = c_ref[...]
    s = c * _sigmoid(c)
    o_ref[...] = jnp.dot(s, w_ref[...], preferred_element_type=F32,
                         precision=lax.Precision.HIGHEST) + b_ref[...]


def _adaln(cond, ada_w, ada_b):
    depth, d, n = ada_w.shape
    tn = _pick_tile(n, 1536)
    return pl.pallas_call(
        _adaln_kernel,
        out_shape=jax.ShapeDtypeStruct((depth, cond.shape[0], n), F32),
        grid=(depth, n // tn),
        in_specs=[pl.BlockSpec(cond.shape, lambda l, j: (0, 0)),
                  pl.BlockSpec((None, d, tn), lambda l, j: (l, 0, j)),
                  pl.BlockSpec((None, 1, tn), lambda l, j: (l, 0, j))],
        out_specs=pl.BlockSpec((None, cond.shape[0], tn), lambda l, j: (l, 0, j)),
        compiler_params=_params("parallel", "parallel"),
        name="adaln",
    )(cond, ada_w, ada_b.reshape(depth, 1, n))


def _modulated_rms(x, nw, sc, sh):
    ms = jnp.mean(x * x, axis=-1, keepdims=True)
    return (x * lax.rsqrt(ms + EPS) * nw) * (1.0 + sc) + sh


def _head_group_matrix():
    r = lax.broadcasted_iota(jnp.int32, (2 * LANES, 2 * LANES), 0) // HEAD_DIM
    c = lax.broadcasted_iota(jnp.int32, (2 * LANES, 2 * LANES), 1) // HEAD_DIM
    return jnp.where(r == c, 1.0, 0.0).astype(BF16)


def _per_head_sum(v, grp):
    hi = v.astype(BF16)
    lo = (v - hi.astype(F32)).astype(BF16)
    r = jnp.dot(jnp.concatenate([hi, lo], axis=1), grp, preferred_element_type=F32)
    return r[:, 0:LANES] + r[:, LANES:]


def _dup_halves(t):
    sw = pltpu.roll(t, HEAD_DIM, 1)
    lo = lax.broadcasted_iota(jnp.int32, t.shape, 1) < HEAD_DIM
    return jnp.where(lo, t, sw), jnp.where(lo, sw, t)


def _inproj_kernel(x_ref, nw_ref, sc_ref, sh_ref, w_ref, cos_ref, sa_ref, sb_ref, qw_ref, kw_ref,
                   q_ref, k_ref, v_ref, rk_ref, rv_ref, rq_ref, rg_ref, cv_ref):
    hb = _modulated_rms(x_ref[...], nw_ref[...], sc_ref[...], sh_ref[...]).astype(BF16)

    def proj(c0, n):
        return jnp.dot(hb, w_ref[:, c0:c0 + n], preferred_element_type=F32)

    grp = _head_group_matrix()
    cos, sa, sb = cos_ref[...], sa_ref[...], sb_ref[...]

    def norm_rope(p, wrow):
        y = p * lax.rsqrt(_per_head_sum(p * p, grp) * (1.0 / HEAD_DIM) + EPS) * wrow
        return (y * cos + pltpu.roll(y, LANES - ROPE_FREQS, 1) * sa
                + pltpu.roll(y, ROPE_FREQS, 1) * sb)

    kv = proj(C_ATT_K, 2 * ATT_KV_W)
    k0, k1 = _dup_halves(norm_rope(kv[:, 0:ATT_KV_W], kw_ref[...]))
    k_ref[:, 0:LANES] = k0.astype(BF16)
    k_ref[:, LANES:2 * LANES] = k1.astype(BF16)
    v_ref[...] = kv[:, ATT_KV_W:].T.astype(BF16)
    qw = qw_ref[...] * ATT_SCALE
    qall = proj(C_ATT_Q, D_ATT)
    for j in range(D_ATT // LANES):
        q_ref[:, j * LANES:(j + 1) * LANES] = norm_rope(qall[:, j * LANES:(j + 1) * LANES], qw).astype(BF16)
    rk_ref[...] = (proj(C_RET_K, D_RET) * RET_K_SCALE).astype(BF16)
    rv_ref[...] = proj(C_RET_V, D_RET).astype(BF16)
    rq_ref[...] = proj(C_RET_Q, D_RET).astype(BF16)
    rg_ref[...] = proj(C_RET_G, D_RET)
    cv_ref[...] = proj(C_CONV, CONV_CH) * _sigmoid(proj(C_CONV + CONV_CH, CONV_CH))


def _inproj(x, nw, sc, sh, w_bf, cos, sa, sb, qw, kw):
    b, s, d = x.shape
    tm = _pick_tile(s, 512)
    row = lambda n: pl.BlockSpec((1, n), lambda bi, i: (0, 0))
    per_b = pl.BlockSpec((None, 1, d), lambda bi, i: (bi, 0, 0))
    tab = pl.BlockSpec((tm, LANES), lambda bi, i: (i, 0))
    tok = lambda n: pl.BlockSpec((None, tm, n), lambda bi, i: (bi, i, 0))
    shp = lambda n, dt: jax.ShapeDtypeStruct((b, s, n), dt)
    return pl.pallas_call(
        _inproj_kernel,
        out_shape=(shp(D_ATT, BF16), shp(2 * LANES, BF16), jax.ShapeDtypeStruct((b, ATT_KV_W, s), BF16),
                   shp(D_RET, BF16), shp(D_RET, BF16), shp(D_RET, BF16), shp(D_RET, F32),
                   shp(CONV_CH, F32)),
        grid=(b, s // tm),
        in_specs=[tok(d), row(d), per_b, per_b,
                  pl.BlockSpec(w_bf.shape, lambda bi, i: (0, 0)),
                  tab, tab, tab, row(LANES), row(LANES)],
        out_specs=(tok(D_ATT), tok(2 * LANES), pl.BlockSpec((None, ATT_KV_W, tm), lambda bi, i: (bi, 0, i)),
                   tok(D_RET), tok(D_RET), tok(D_RET), tok(D_RET), tok(CONV_CH)),
        compiler_params=_params("parallel", "parallel"),
        name="inproj",
    )(x, nw, sc, sh, w_bf, cos, sa, sb, qw, kw)


def _attn_kernel(*refs, window):
    if window:
        q_ref, kp_ref, kc_ref, kn_ref, vp_ref, vc_ref, vn_ref, kx_ref, vx_ref, sink_ref, o_ref = refs
    else:
        q_ref, kx_ref, vx_ref, sink_ref, o_ref = refs
    blk = ATT_BLOCK
    n_ctx = kx_ref.shape[0]
    nk = 3 * blk + n_ctx if window else n_ctx
    if window:
        i = pl.program_id(1)
        key = lax.broadcasted_iota(jnp.int32, (blk, 2 * blk), 0)
        qry = lax.broadcasted_iota(jnp.int32, (blk, 2 * blk), 1) & (blk - 1)
        mask_prev = (key >= qry) & (i > 0)
        mask_next = (key <= qry) & (i < pl.num_programs(1) - 1)
    first_head = lax.broadcasted_iota(jnp.int32, (blk, LANES), 1) < HEAD_DIM
    ones = jnp.ones((2 * SUBLANES, nk), BF16)
    def scores(pair):
        g = pair // 2
        gs = slice(g * LANES, (g + 1) * LANES)
        if window:
            kcat = jnp.concatenate([kp_ref[:, gs], kc_ref[:, gs], kn_ref[:, gs], kx_ref[:, gs]], axis=0)
        else:
            kcat = kx_ref[:, gs]
        qp = q_ref[:, pair * LANES:(pair + 1) * LANES]
        zero = jnp.zeros_like(qp)
        w = jnp.concatenate([jnp.where(first_head, qp, zero), jnp.where(first_head, zero, qp)], axis=0)
        return lax.dot_general(kcat, w, (((1,), (1,)), ((), ())), preferred_element_type=F32)

    def softmax(pair, s):
        if window:
            parts = [jnp.where(mask_prev, s[0:blk], NEG_INF), s[blk:2 * blk],
                     jnp.where(mask_next, s[2 * blk:3 * blk], NEG_INF), s[3 * blk:]]
        else:
            parts = [s]
        snk = sink_ref[pair:pair + 1, :]
        m = snk
        for part in parts:
            m = jnp.maximum(m, jnp.max(part, axis=0, keepdims=True))
        p = jnp.concatenate([jnp.exp(part - m).astype(BF16) for part in parts], axis=0)
        return p, jnp.exp(snk - m)

    def output(pair, p, sink_p):
        g = pair // 2
        vs = slice(g * HEAD_DIM, (g + 1) * HEAD_DIM)
        if window:
            vt = jnp.concatenate([vp_ref[vs, :], vc_ref[vs, :], vn_ref[vs, :], vx_ref[vs, :]], axis=1)
        else:
            vt = vx_ref[vs, :]
        vaug = jnp.concatenate([vt, ones], axis=0)
        o = jnp.dot(vaug, p, preferred_element_type=F32)
        on = o[0:HEAD_DIM, :] * (1.0 / (o[HEAD_DIM:HEAD_DIM + 1, :] + sink_p))
        ot = jnp.concatenate([on[:, 0:blk], on[:, blk:2 * blk]], axis=0)
        o_ref[:, pair * LANES:(pair + 1) * LANES] = ot.T.astype(BF16)

    n_pairs = ATT_Q_HEADS // 2
    s_next = scores(0)
    for pair in range(n_pairs):
        s_cur = s_next
        if pair + 1 < n_pairs:
            s_next = scores(pair + 1)
        output(pair, *softmax(pair, s_cur))


def _attention(q, k, vt, kx, vxt, sink_tab, window):
    b, s, _ = q.shape
    blk = ATT_BLOCK
    nb = s // blk
    n_ctx = kx.shape[1]
    qspec = pl.BlockSpec((None, blk, D_ATT), lambda bi, i: (bi, i, 0))
    kctx = pl.BlockSpec((None, n_ctx, 2 * LANES), lambda bi, i: (bi, 0, 0))
    vctx = pl.BlockSpec((None, ATT_KV_W, n_ctx), lambda bi, i: (bi, 0, 0))
    snk = pl.BlockSpec(sink_tab.shape, lambda bi, i: (0, 0))
    if window:
        prev = lambda i: jnp.maximum(i - 1, 0)
        nxt = lambda i: jnp.minimum(i + 1, nb - 1)
        same = lambda i: i
        kspec = lambda f: pl.BlockSpec((None, blk, 2 * LANES), lambda bi, i: (bi, f(i), 0))
        vspec = lambda f: pl.BlockSpec((None, ATT_KV_W, blk), lambda bi, i: (bi, 0, f(i)))
        in_specs = [qspec, kspec(prev), kspec(same), kspec(nxt), vspec(prev), vspec(same), vspec(nxt),
                    kctx, vctx, snk]
        args = (q, k, k, k, vt, vt, vt, kx, vxt, sink_tab)
    else:
        in_specs = [qspec, kctx, vctx, snk]
        args = (q, kx, vxt, sink_tab)
    return pl.pallas_call(
        functools.partial(_attn_kernel, window=window),
        out_shape=jax.ShapeDtypeStruct((b, s, D_ATT), BF16),
        grid=(b, nb),
        in_specs=in_specs,
        out_specs=qspec,
        compiler_params=_params("parallel", "parallel"),
        name="attention_window" if window else "attention_ctx",
    )(*args)


def _ret_state_kernel(lgl_ref, k_ref, v_ref, s0_ref, r_ref, fin_ref, s_scr, *, cb, reverse):
    c = pl.program_id(1)
    L = RET_CHUNK

    @pl.when(c == 0)
    def _():
        s_scr[...] = s0_ref[...]

    jj = lax.broadcasted_iota(jnp.int32, (L, LANES), 0).astype(F32)
    expo = jj if reverse else (L - 1.0) - jj
    same_head = (lax.broadcasted_iota(jnp.int32, (LANES, LANES), 0) // HEAD_DIM
                 == lax.broadcasted_iota(jnp.int32, (LANES, LANES), 1) // HEAD_DIM)
    for pr in range(RET_HEADS // 2):
        cs = slice(pr * LANES, (pr + 1) * LANES)
        lgl = lgl_ref[pr]
        kdec = jnp.exp(expo * lgl)
        cdec = jnp.exp(float(L) * lgl)
        state = s_scr[pr]
        for t in range(cb):
            cc = cb - 1 - t if reverse else t
            rows = slice(cc * L, (cc + 1) * L)
            r_ref[cc, pr] = state
            kd = k_ref[rows, cs].astype(F32) * kdec
            u = jnp.dot(kd.T.astype(BF16), v_ref[rows, cs], preferred_element_type=F32)
            state = cdec * state + jnp.where(same_head, u, 0.0)
        s_scr[pr] = state

    @pl.when(c == pl.num_programs(1) - 1)
    def _():
        fin_ref[...] = s_scr[...]


def _ret_states(lgl, rk, rv, s0, reverse):
    b, s, _ = rk.shape
    nc = s // RET_CHUNK
    cb = _pick_tile(nc, 8)
    nblk = nc // cb
    npair = RET_HEADS // 2
    blk_idx = (lambda c: nblk - 1 - c) if reverse else (lambda c: c)
    tok = pl.BlockSpec((None, cb * RET_CHUNK, D_RET), lambda bi, c: (bi, blk_idx(c), 0))
    st = pl.BlockSpec((None, npair, LANES, LANES), lambda bi, c: (bi, 0, 0, 0))
    return pl.pallas_call(
        functools.partial(_ret_state_kernel, cb=cb, reverse=reverse),
        out_shape=(jax.ShapeDtypeStruct((b, nc, npair, LANES, LANES), F32),
                   jax.ShapeDtypeStruct((b, npair, LANES, LANES), F32)),
        grid=(b, nblk),
        in_specs=[pl.BlockSpec(lgl.shape, lambda bi, c: (0, 0, 0)), tok, tok, st],
        out_specs=(pl.BlockSpec((None, cb, npair, LANES, LANES),
                                lambda bi, c: (bi, blk_idx(c), 0, 0, 0)), st),
        scratch_shapes=[pltpu.VMEM((npair, LANES, LANES), F32)],
        compiler_params=_params("parallel", "arbitrary"),
        name="ret_state_bwd" if reverse else "ret_state_fwd",
    )(lgl, rk, rv, s0)


def _ret_out_kernel(lgf_ref, lgb_ref, q_ref, k_ref, v_ref, g_ref, rf_ref, rb_ref, gnw_ref, o_ref, *, cb):
    L = RET_CHUNK
    i0 = lax.broadcasted_iota(jnp.int32, (L, LANES), 0).astype(F32)
    i1 = lax.broadcasted_iota(jnp.int32, (L, LANES), 1).astype(F32)
    diff = i0 - i1
    lo = lax.broadcasted_iota(jnp.int32, (L, LANES), 1) < HEAD_DIM
    grp = _head_group_matrix()
    inv = 1.0 / HEAD_DIM
    tabs = []
    for pr in range(RET_HEADS // 2):
        lgf, lgb = lgf_ref[pr], lgb_ref[pr]
        dmat = [jnp.where(diff >= 0.0,
                          jnp.exp(jnp.maximum(diff, 0.0) * lgf[:, a:a + 1]),
                          jnp.exp(jnp.maximum(-diff, 0.0) * lgb[:, a:a + 1]))
                for a in (0, HEAD_DIM)]
        tabs.append((jnp.exp((i0 + 1.0) * lgf), jnp.exp((float(L) - i0) * lgb),
                     jnp.concatenate(dmat, axis=0), gnw_ref[:, pr * LANES:(pr + 1) * LANES]))

    def decayed_scores(t, pr):
        rows = pl.ds(pl.multiple_of(t * L, L), L)
        cs = slice(pr * LANES, (pr + 1) * LANES)
        qp, kp = q_ref[rows, cs], k_ref[rows, cs]
        zero = jnp.zeros_like(kp)
        qz = jnp.concatenate([jnp.where(lo, qp, zero), jnp.where(lo, zero, qp)], axis=0)
        return lax.dot_general(qz, kp, (((1,), (1,)), ((), ())), preferred_element_type=F32) * tabs[pr][2]

    def readout(t, pr, a):
        rows = pl.ds(pl.multiple_of(t * L, L), L)
        cs = slice(pr * LANES, (pr + 1) * LANES)
        xif, xib, _, gnw = tabs[pr]
        vp = v_ref[rows, cs]
        zero = jnp.zeros_like(vp)
        qf = q_ref[rows, cs].astype(F32)
        lhs = jnp.concatenate([a[0:L].astype(BF16), a[L:].astype(BF16),
                               (qf * xif).astype(BF16), (qf * xib).astype(BF16)], axis=1)
        rhs = jnp.concatenate([jnp.where(lo, vp, zero), jnp.where(lo, zero, vp),
                               rf_ref[t, pr].astype(BF16), rb_ref[t, pr].astype(BF16)], axis=0)
        acc = jnp.dot(lhs, rhs, preferred_element_type=F32)
        dl = acc - _per_head_sum(acc, grp) * inv
        var = _per_head_sum(dl * dl, grp) * inv
        y = dl * lax.rsqrt(var + EPS) * gnw
        gt = g_ref[rows, cs]
        o_ref[rows, cs] = (gt * _sigmoid(gt) * y).astype(BF16)

    unroll = min(RET_UNROLL, cb)

    def group(gi, carry):
        units = [(gi * unroll + u, pr) for u in range(unroll) for pr in range(RET_HEADS // 2)]
        a_next = decayed_scores(*units[0])
        for n, unit in enumerate(units):
            a_cur = a_next
            if n + 1 < len(units):
                a_next = decayed_scores(*units[n + 1])
            readout(*unit, a_cur)
        return carry

    lax.fori_loop(0, cb // unroll, group, 0)


def _ret_out(lgf, lgb, rq, rk, rv, rg, rf, rb, gnw):
    b, s, _ = rq.shape
    nc = s // RET_CHUNK
    cb = _pick_tile(nc, 8)
    npair = RET_HEADS // 2
    tok = pl.BlockSpec((None, cb * RET_CHUNK, D_RET), lambda bi, c: (bi, c, 0))
    st = pl.BlockSpec((None, cb, npair, LANES, LANES), lambda bi, c: (bi, c, 0, 0, 0))
    lg = pl.BlockSpec(lgf.shape, lambda bi, c: (0, 0, 0))
    return pl.pallas_call(
        functools.partial(_ret_out_kernel, cb=cb),
        out_shape=jax.ShapeDtypeStruct((b, s, D_RET), BF16),
        grid=(b, nc // cb),
        in_specs=[lg, lg, tok, tok, tok, tok, st, st, pl.BlockSpec((1, D_RET), lambda bi, c: (0, 0))],
        out_specs=tok,
        compiler_params=_params("parallel", "parallel"),
        name="ret_out",
    )(lgf, lgb, rq, rk, rv, rg, rf, rb, gnw)


def _conv_kernel(prev_ref, cur_ref, next_ref, w_ref, b_ref, lnw_ref, lnb_ref, o_ref, xpad, shifted, hbuf):
    i = pl.program_id(1)
    tm = cur_ref.shape[0]
    halo = CONV_HALO
    first = i == 0
    last = i == pl.num_programs(1) - 1
    xpad[0:halo, :] = jnp.where(first, 0.0, prev_ref[...])
    xpad[halo:halo + tm, :] = cur_ref[...]
    xpad[halo + tm:2 * halo + tm, :] = jnp.where(last, 0.0, next_ref[...])
    span = tm + 2 * halo - SUBLANES
    for r in range(SUBLANES):
        shifted[r] = xpad[r:r + span, :]
    base = halo - CONV_WIDTH // 2
    bias = b_ref[...]

    def sub(sb, carry):
        r0 = pl.multiple_of(sb * CONV_SUB, CONV_SUB)
        chains = [None] * CONV_CHAINS
        for w in range(CONV_WIDTH):
            off = base + w
            xs = shifted[off % SUBLANES, pl.ds(r0 + (off // SUBLANES) * SUBLANES, CONV_SUB), :]
            term = xs * w_ref[w:w + 1, :]
            c = w % CONV_CHAINS
            chains[c] = term if chains[c] is None else chains[c] + term
        hbuf[pl.ds(r0, CONV_SUB), :] = (chains[0] + chains[1]) + (chains[2] + chains[3]) + bias
        return carry

    lax.fori_loop(0, tm // CONV_SUB, sub, 0, unroll=2)
    h = hbuf[...]
    mu = jnp.mean(h, axis=-1, keepdims=True)
    dl = h - mu
    var = jnp.mean(dl * dl, axis=-1, keepdims=True)
    y = dl * lax.rsqrt(var + EPS) * lnw_ref[...] + lnb_ref[...]
    o_ref[...] = (y * _sigmoid(y)).astype(BF16)


def _conv(cv, w, bias, lnw, lnb):
    b, s, ch = cv.shape
    tm = _pick_tile(s, 512)
    hpb = tm // CONV_HALO
    nh = s // CONV_HALO
    row = pl.BlockSpec((1, ch), lambda bi, i: (0, 0))
    span = tm + 2 * CONV_HALO - SUBLANES
    return pl.pallas_call(
        _conv_kernel,
        out_shape=jax.ShapeDtypeStruct((b, s, ch), BF16),
        grid=(b, s // tm),
        in_specs=[pl.BlockSpec((None, CONV_HALO, ch), lambda bi, i: (bi, jnp.maximum(i * hpb - 1, 0), 0)),
                  pl.BlockSpec((None, tm, ch), lambda bi, i: (bi, i, 0)),
                  pl.BlockSpec((None, CONV_HALO, ch),
                               lambda bi, i: (bi, jnp.minimum((i + 1) * hpb, nh - 1), 0)),
                  pl.BlockSpec(w.shape, lambda bi, i: (0, 0)), row, row, row],
        out_specs=pl.BlockSpec((None, tm, ch), lambda bi, i: (bi, i, 0)),
        scratch_shapes=[pltpu.VMEM((tm + 2 * CONV_HALO, ch), F32),
                        pltpu.VMEM((SUBLANES, span, ch), F32),
                        pltpu.VMEM((tm, ch), F32)],
        compiler_params=_params("parallel", "parallel"),
        name="conv",
    )(cv, cv, cv, w, bias, lnw, lnb)


def _outproj_kernel(*refs, route):
    if route:
        (att_ref, ret_ref, cnv_ref, w_ref, x_ref, g1_ref, nw_ref, sc_ref, sh_ref, rcat_ref,
         xo_ref, h_ref, rt_ref) = refs
    else:
        att_ref, ret_ref, cnv_ref, w_ref, x_ref, g1_ref, nw_ref, sc_ref, sh_ref, xo_ref, h_ref = refs
    y = (jnp.dot(att_ref[...], w_ref[0:D_ATT, :], preferred_element_type=F32)
         + jnp.dot(ret_ref[...], w_ref[D_ATT:D_ATT + D_RET, :], preferred_element_type=F32)
         + jnp.dot(cnv_ref[...], w_ref[D_ATT + D_RET:, :], preferred_element_type=F32))
    xn = x_ref[...] + g1_ref[...] * y
    xo_ref[...] = xn
    h = _modulated_rms(xn, nw_ref[...], sc_ref[...], sh_ref[...])
    h_ref[...] = _pack_bf16_pairs(h) if route else h.astype(BF16)
    if route:
        hi = h.astype(BF16)
        lo = (h - hi.astype(F32)).astype(BF16)
        tm = h.shape[0]
        r = jnp.dot(jnp.concatenate([hi, lo], axis=0), rcat_ref[...], preferred_element_type=F32)
        logits = (r[0:tm, 0:LANES] + r[0:tm, LANES:]) + (r[tm:, 0:LANES] + r[tm:, LANES:])
        lane = lax.broadcasted_iota(jnp.int32, logits.shape, 1).astype(F32)
        logits = jnp.where(lane < N_EXPERTS, logits, NEG_INF)
        m1 = jnp.max(logits, axis=-1, keepdims=True)
        i1 = jnp.min(jnp.where(logits == m1, lane, float(LANES)), axis=-1, keepdims=True)
        rest = jnp.where(lane == i1, NEG_INF, logits)
        m2 = jnp.max(rest, axis=-1, keepdims=True)
        i2 = jnp.min(jnp.where(rest == m2, lane, float(LANES)), axis=-1, keepdims=True)
        e2 = jnp.exp(m2 - m1)
        w1 = 1.0 / (1.0 + e2)
        w2 = e2 / (1.0 + e2)
        rt_ref[...] = jnp.where(lane == 0.0, i1,
                                jnp.where(lane == 1.0, i2,
                                          jnp.where(lane == 2.0, w1, jnp.where(lane == 3.0, w2, 0.0))))


def _outproj(att, ret, cnv, w_bf, x, g1, nw, sc, sh, router=None):
    b, s, d = x.shape
    tm = _pick_tile(s, 512)
    tok = lambda n: pl.BlockSpec((None, tm, n), lambda bi, i: (bi, i, 0))
    per_b = pl.BlockSpec((None, 1, d), lambda bi, i: (bi, 0, 0))
    row = pl.BlockSpec((1, d), lambda bi, i: (0, 0))
    in_specs = [tok(D_ATT), tok(D_RET), tok(CONV_CH), pl.BlockSpec(w_bf.shape, lambda bi, i: (0, 0)),
                tok(d), per_b, row, per_b, per_b]
    args = [att, ret, cnv, w_bf, x, g1, nw, sc, sh]
    h_shape = (b, s, d // 2) if router is not None else (b, s, d)
    out_shape = [jax.ShapeDtypeStruct((b, s, d), F32),
                 jax.ShapeDtypeStruct(h_shape, jnp.int32 if router is not None else BF16)]
    out_specs = [tok(d), tok(h_shape[2])]
    if router is not None:
        rpad = jnp.zeros((d, LANES), F32).at[:, :N_EXPERTS].set(router)
        rhi = rpad.astype(BF16)
        rlo = (rpad - rhi.astype(F32)).astype(BF16)
        in_specs += [pl.BlockSpec((d, 2 * LANES), lambda bi, i: (0, 0))]
        args += [jnp.concatenate([rhi, rlo], axis=1)]
        out_shape.append(jax.ShapeDtypeStruct((b, s, LANES), F32))
        out_specs.append(tok(LANES))
    return pl.pallas_call(
        functools.partial(_outproj_kernel, route=router is not None),
        out_shape=tuple(out_shape),
        grid=(b, s // tm),
        in_specs=in_specs,
        out_specs=tuple(out_specs),
        compiler_params=_params("parallel", "parallel"),
        name="outproj_route" if router is not None else "outproj",
    )(*args)


def _swiglu_chunks(h, wg_ref, wu_ref, wd_ref, sub):
    total = None
    for c0 in range(0, wg_ref.shape[1], sub):
        cs = slice(c0, min(c0 + sub, wg_ref.shape[1]))
        gate = jnp.dot(h, wg_ref[:, cs].astype(BF16), preferred_element_type=F32)
        up = jnp.dot(h, wu_ref[:, cs].astype(BF16), preferred_element_type=F32)
        act = (gate * _sigmoid(gate) * up).astype(BF16)
        part = jnp.dot(act, wd_ref[cs, :].astype(BF16), preferred_element_type=F32)
        total = part if total is None else total + part
    return total


def _ffn_kernel(h_ref, wg_ref, wu_ref, wd_ref, x_ref, g2_ref, o_ref):
    y = _swiglu_chunks(h_ref[...], wg_ref, wu_ref, wd_ref, FFN_SUB)
    o_ref[...] = x_ref[...] + g2_ref[...] * y


def _ffn(h, wg, wu, wd, x, g2):
    b, s, d = x.shape
    f = wg.shape[1]
    tm = _pick_tile(s, 512)
    tok = lambda: pl.BlockSpec((None, tm, d), lambda bi, i: (bi, i, 0))
    res = lambda shape: pl.BlockSpec(shape, lambda bi, i: (0, 0), pipeline_mode=pl.Buffered(1))
    return pl.pallas_call(
        _ffn_kernel,
        out_shape=jax.ShapeDtypeStruct((b, s, d), F32),
        grid=(b, s // tm),
        in_specs=[tok(), res((d, f)), res((d, f)), res((f, d)), tok(),
                  pl.BlockSpec((None, 1, d), lambda bi, i: (bi, 0, 0))],
        out_specs=tok(),
        compiler_params=_params("parallel", "parallel"),
        name="ffn_dense",
    )(h, wg, wu, wd, x, g2)


def _gather_rows(idx, src):
    n = idx.shape[0]
    d = src.shape[1]
    mesh = plsc.VectorSubcoreMesh(core_axis_name="core", subcore_axis_name="subcore")

    @pl.kernel(out_type=jax.ShapeDtypeStruct((n, d), src.dtype), mesh=mesh, name="moe_gather")
    def gather(src_hbm, idx_hbm, out_hbm):
        def body(idx_vmem, out_vmem):
            pltpu.sync_copy(src_hbm.at[idx_vmem.at[0, pl.ds(0, SC_WINDOW)]], out_vmem)

        pltpu.emit_pipeline(
            body,
            grid=(n // SC_WINDOW,),
            in_specs=[pl.BlockSpec((1, LANES), lambda i: (i, 0))],
            out_specs=[pl.BlockSpec((SC_WINDOW, d), lambda i: (i, 0))],
            core_axis_name=("core", "subcore"),
            dimension_semantics=(pltpu.PARALLEL,),
        )(idx_hbm, out_hbm)

    idx_rows = jnp.pad(idx.reshape(n // SC_WINDOW, SC_WINDOW), ((0, 0), (0, LANES - SC_WINDOW)))
    return gather(src, idx_rows)


def _invert_rows(dest, n_rows):
    n = dest.shape[0]
    n_tok = n // 2
    mesh = plsc.VectorSubcoreMesh(core_axis_name="core", subcore_axis_name="subcore")

    @pl.kernel(out_type=jax.ShapeDtypeStruct((n_rows,), jnp.int32), mesh=mesh, name="moe_row_src",
               scratch_types=[pltpu.VMEM((n,), jnp.int32), pltpu.VMEM((n_rows,), jnp.int32)],
               compiler_params=pltpu.CompilerParams(needs_layout_passes=False))
    def invert(dest_hbm, out_hbm, dest_vmem, rows_vmem):
        @pl.when((lax.axis_index("core") == 0) & (lax.axis_index("subcore") == 0))
        def _():
            pltpu.sync_copy(dest_hbm, dest_vmem)

            @pl.loop(0, n_rows, step=SC_LANES)
            def _(r):
                rows_vmem[pl.ds(r, SC_LANES)] = lax.rem(lax.iota(jnp.int32, SC_LANES) + r, n_tok)

            @pl.loop(0, n, step=SC_LANES)
            def _(a):
                tok = lax.iota(jnp.int32, SC_LANES) + a
                tok = jnp.where(tok >= n_tok, tok - n_tok, tok)
                plsc.store_scatter(rows_vmem, [dest_vmem[pl.ds(a, SC_LANES)]], tok)

            pltpu.sync_copy(rows_vmem, out_hbm)

    return invert(dest)


def _moe_kernel(te_ref, nu_ref, x_ref, wg_ref, wu_ref, wd_ref, *rest, tile0):
    o_ref, xb, acc = rest[-3:]
    t = pl.program_id(0) + tile0
    j = pl.program_id(1)
    nj = pl.num_programs(1)
    used = t < nu_ref[0]

    @pl.when(used & (j == 0))
    def _():
        lo, hi = _unpack_bf16_pairs(x_ref[...])
        half = lo.shape[1]
        xb[:, 0:half] = lo.astype(BF16)
        xb[:, half:] = hi.astype(BF16)
        acc[...] = jnp.zeros_like(acc)

    @pl.when(used)
    def _():
        acc[...] += _swiglu_chunks(xb[...], wg_ref, wu_ref, wd_ref, MOE_SUB)

    @pl.when(used & (j == nj - 1))
    def _():
        o_ref[...] = _pack_bf16_pairs(acc[...])

    @pl.when(jnp.logical_not(used) & (j == nj - 1))
    def _():
        o_ref[...] = jnp.zeros_like(o_ref)


def _moe_grouped(tile_expert, n_used, xs, wg, wu, wd, y_prev, tile0, n_rows):
    p, dp = xs.shape
    d = 2 * dp
    f = wg.shape[2]
    tm = MOE_TM
    tf = MOE_TF
    nj = f // tf

    def jj(t, j, te, nu):
        return jnp.where(t + tile0 < nu[0], j, nj - 1)

    in_specs = [pl.BlockSpec((tm, dp), lambda t, j, te, nu: (t, 0)),
                pl.BlockSpec((None, d, tf), lambda t, j, te, nu: (te[t + tile0], 0, jj(t, j, te, nu))),
                pl.BlockSpec((None, d, tf), lambda t, j, te, nu: (te[t + tile0], 0, jj(t, j, te, nu))),
                pl.BlockSpec((None, tf, d), lambda t, j, te, nu: (te[t + tile0], jj(t, j, te, nu), 0))]
    args = [tile_expert, n_used, xs, wg, wu, wd]
    aliases = {}
    if y_prev is not None:
        in_specs.append(pl.BlockSpec(memory_space=pl.ANY))
        args.append(y_prev)
        aliases = {len(args) - 1: 0}
    return pl.pallas_call(
        functools.partial(_moe_kernel, tile0=tile0),
        out_shape=jax.ShapeDtypeStruct((n_rows, dp), jnp.int32),
        grid_spec=pltpu.PrefetchScalarGridSpec(
            num_scalar_prefetch=2,
            grid=(p // tm, nj),
            in_specs=in_specs,
            out_specs=pl.BlockSpec((tm, dp), lambda t, j, te, nu: (t + tile0, 0)),
            scratch_shapes=[pltpu.VMEM((tm, d), BF16), pltpu.VMEM((tm, d), F32)]),
        input_output_aliases=aliases,
        compiler_params=_params("arbitrary", "arbitrary"),
        name="moe_grouped",
    )(*args)


def _combine_kernel(y1_ref, y2_ref, x_ref, g2_ref, rt_ref, o_ref):
    rt = rt_ref[...]
    w1, w2 = rt[:, 2:3], rt[:, 3:4]
    half = y1_ref.shape[1]
    for k, (a, c) in enumerate(zip(_unpack_bf16_pairs(y1_ref[...]), _unpack_bf16_pairs(y2_ref[...]))):
        cs = slice(k * half, (k + 1) * half)
        o_ref[:, cs] = x_ref[:, cs] + g2_ref[:, cs] * (w1 * a + w2 * c)


def _moe_combine(yg, x, g2, route):
    b, s, d = x.shape
    tm = _pick_tile(s, 512)
    tok = lambda n: pl.BlockSpec((None, tm, n), lambda bi, i: (bi, i, 0))
    return pl.pallas_call(
        _combine_kernel,
        out_shape=jax.ShapeDtypeStruct((b, s, d), F32),
        grid=(b, s // tm),
        in_specs=[tok(d // 2), pl.BlockSpec((None, tm, d // 2), lambda bi, i: (b + bi, i, 0)), tok(d),
                  pl.BlockSpec((None, 1, d), lambda bi, i: (bi, 0, 0)), tok(LANES)],
        out_specs=tok(d),
        compiler_params=_params("parallel", "parallel"),
        name="moe_combine",
    )(yg, yg, x, g2, route)


def _moe(h, route, x, g2, wg, wu, wd):
    b, s, d = x.shape
    n_tok = b * s
    tm = MOE_TM
    rt = route.reshape(n_tok, LANES)
    flat_e = jnp.concatenate([rt[:, 0], rt[:, 1]]).astype(jnp.int32)
    onehot = (flat_e[:, None] == jnp.arange(N_EXPERTS, dtype=jnp.int32)[None, :]).astype(jnp.int32)
    csum = jnp.cumsum(onehot, axis=0)
    rank = jnp.sum(csum * onehot, axis=1) - 1
    counts = csum[-1]
    tiles_e = (counts + tm - 1) // tm
    tiles_cum = jnp.cumsum(tiles_e)
    row_start = (tiles_cum - tiles_e) * tm
    dest = jnp.sum(onehot * row_start[None, :], axis=1) + rank
    n_tiles = 2 * n_tok // tm + N_EXPERTS
    tile_ids = jnp.arange(n_tiles, dtype=jnp.int32)
    tile_expert = jnp.sum((tile_ids[:, None] >= tiles_cum[None, :]).astype(jnp.int32), axis=1)
    last_e = jnp.max(jnp.where(tiles_e > 0, jnp.arange(N_EXPERTS, dtype=jnp.int32), 0))
    tile_expert = jnp.minimum(tile_expert, last_e).astype(jnp.int32)
    n_used = tiles_cum[-1:].astype(jnp.int32)
    dest = dest.astype(jnp.int32)
    row_src = _invert_rows(dest, n_tiles * tm)
    n_chunks = max(c for c in range(1, MOE_CHUNKS + 1) if n_tiles % c == 0)
    tiles_c = n_tiles // n_chunks
    h_flat = h.reshape(n_tok, d // 2)
    xs = [_gather_rows(row_src[c * tiles_c * tm:(c + 1) * tiles_c * tm], h_flat) for c in range(n_chunks)]
    y = None
    for c in range(n_chunks):
        y = _moe_grouped(tile_expert, n_used, xs[c], wg, wu, wd, y, c * tiles_c, n_tiles * tm)
    yg = _gather_rows(dest, y).reshape(2 * b, s, d // 2)
    return _moe_combine(yg, x, g2, route)


def _rope_tables(n):
    rows = n // GRID_W
    r = jnp.repeat(jnp.arange(rows), GRID_W).astype(F32)
    col = jnp.tile(jnp.arange(GRID_W), rows).astype(F32)
    freqs = ROPE_BASE ** (-jnp.arange(ROPE_FREQS, dtype=F32) / ROPE_FREQS)
    ang = jnp.stack([r[:, None] * freqs, col[:, None] * freqs], axis=1)
    ang = jnp.repeat(ang[:, :, None, :], 2, axis=2).reshape(n, HEAD_DIM)
    ang = jnp.tile(ang, (1, LANES // HEAD_DIM))
    cos, sin = jnp.cos(ang), jnp.sin(ang)
    first_half = (jnp.arange(LANES) % (2 * ROPE_FREQS)) < ROPE_FREQS
    return cos, jnp.where(first_half, -sin, 0.0), jnp.where(first_half, 0.0, sin)


def _lane_rows(lg):
    return jnp.repeat(lg.astype(F32), HEAD_DIM).reshape(RET_HEADS // 2, 1, LANES)


def kernel(x, c, ctx, c_ctx, ada_w, ada_b, norm1_w, norm2_w, w_in, w_out, q_norm_w, k_norm_w,
           attn_sink, ret_decay_f, ret_decay_b, ret_gn_w, conv_w, conv_b, conv_ln_w, conv_ln_b,
           ffn_w_gate, ffn_w_up, ffn_w_down, router_w, moe_w_gate, moe_w_up, moe_w_down):
    b, n, d = x.shape
    n_ctx = ctx.shape[1]
    depth = ada_w.shape[0]
    cond = jnp.zeros((SUBLANES, d), F32).at[0:b].set(c).at[b].set(c_ctx)
    mods = _adaln(cond, ada_w, ada_b).reshape(depth, SUBLANES, 6, d)
    cos, sa, sb = _rope_tables(n)
    ones_c = jnp.ones((n_ctx, LANES), F32)
    zeros_c = jnp.zeros((n_ctx, LANES), F32)
    zero_state = jnp.zeros((b, RET_HEADS // 2, LANES, LANES), F32)
    row = lambda v: v.reshape(1, -1)
    x, moe_early = lax.optimization_barrier((x, (moe_w_gate.astype(BF16), moe_w_up.astype(BF16))))
    moe_bf = moe_early + (moe_w_down.astype(BF16),)
    for l in range(depth):
        last = l == depth - 1
        m_lat = [mods[l, 0:b, k][:, None, :] for k in range(6)]
        m_ctx = [jnp.broadcast_to(mods[l, b, k][None, None, :], (b, 1, d)) for k in range(6)]
        w_in_bf = w_in[l].astype(BF16)
        w_out_bf = w_out[l].astype(BF16)
        qw = row(jnp.tile(q_norm_w[l], LANES // HEAD_DIM))
        kw = row(jnp.tile(k_norm_w[l], LANES // HEAD_DIM))
        lgf = _lane_rows(jax.nn.log_sigmoid(ret_decay_f[l].astype(F32)))
        lgb = _lane_rows(jax.nn.log_sigmoid(ret_decay_b[l].astype(F32)))
        sink_tab = jnp.repeat(attn_sink[l].astype(F32), ATT_BLOCK).reshape(ATT_Q_HEADS // 2, 2 * ATT_BLOCK)

        q, k, v, rk, rv, rq, rg, cv = _inproj(x, row(norm1_w[l]), m_lat[1], m_lat[0], w_in_bf,
                                               cos, sa, sb, qw, kw)
        qc, kc, vc, rkc, rvc, rqc, rgc, cvc = _inproj(ctx, row(norm1_w[l]), m_ctx[1], m_ctx[0], w_in_bf,
                                                       ones_c, zeros_c, zeros_c, qw, kw)
        rf_c, s_f = _ret_states(lgf, rkc, rvc, zero_state, reverse=False)
        rb_c, s_b = _ret_states(lgb, rkc, rvc, zero_state, reverse=True)
        rf, _ = _ret_states(lgf, rk, rv, s_f, reverse=False)
        rb, _ = _ret_states(lgb, rk, rv, s_b, reverse=True)

        att = _attention(q, k, v, kc, vc, sink_tab, window=True)
        ret = _ret_out(lgf, lgb, rq, rk, rv, rg, rf, rb, row(ret_gn_w[l]))
        cnv = _conv(cv, conv_w[l], row(conv_b[l]), row(conv_ln_w[l]), row(conv_ln_b[l]))

        if l % 2 == 0:
            i = l // 2
            wg, wu, wd = ffn_w_gate[i].astype(BF16), ffn_w_up[i].astype(BF16), ffn_w_down[i].astype(BF16)
            x_mid, h2 = _outproj(att, ret, cnv, w_out_bf, x, m_lat[2], row(norm2_w[l]), m_lat[4], m_lat[3])
            x_new = _ffn(h2, wg, wu, wd, x_mid, m_lat[5])
        else:
            i = l // 2
            wg, wu, wd = moe_bf[0][i], moe_bf[1][i], moe_bf[2][i]
            x_mid, h2, route = _outproj(att, ret, cnv, w_out_bf, x, m_lat[2], row(norm2_w[l]),
                                        m_lat[4], m_lat[3], router=router_w[i])
            x_new = _moe(h2, route, x_mid, m_lat[5], wg, wu, wd)

        if not last:
            att_c = _attention(qc, None, None, kc, vc, sink_tab, window=False)
            ret_c = _ret_out(lgf, lgb, rqc, rkc, rvc, rgc, rf_c, rb_c, row(ret_gn_w[l]))
            cnv_c = _conv(cvc, conv_w[l], row(conv_b[l]), row(conv_ln_w[l]), row(conv_ln_b[l]))
            if l % 2 == 0:
                c_mid, h2c = _outproj(att_c, ret_c, cnv_c, w_out_bf, ctx, m_ctx[2], row(norm2_w[l]),
                                      m_ctx[4], m_ctx[3])
                ctx = _ffn(h2c, wg, wu, wd, c_mid, m_ctx[5])
            else:
                c_mid, h2c, route_c = _outproj(att_c, ret_c, cnv_c, w_out_bf, ctx, m_ctx[2],
                                               row(norm2_w[l]), m_ctx[4], m_ctx[3],
                                               router=router_w[i])
                ctx = _moe(h2c, route_c, c_mid, m_ctx[5], wg, wu, wd)
        x = x_new
    return x
```

```python
import functools

import jax
import jax.numpy as jnp
from jax import lax
from jax.experimental import pallas as pl
from jax.experimental.pallas import tpu as pltpu
from jax.experimental.pallas import tpu_sc as plsc

F32 = jnp.float32
BF16 = jnp.bfloat16

GRID_W = 64
HEAD_DIM = 64
ATT_Q_HEADS = 8
ATT_KV_HEADS = 2
ATT_WINDOW = 128
ATT_BLOCK = 128
RET_HEADS = 4
RET_CHUNK = 128
RET_K_SCALE = HEAD_DIM ** -0.5
ATT_SCALE = HEAD_DIM ** -0.5
CONV_CH = 256
CONV_WIDTH = 31
ROPE_BASE = 10000.0
ROPE_FREQS = HEAD_DIM // 4
D_ATT = ATT_Q_HEADS * HEAD_DIM
D_RET = RET_HEADS * HEAD_DIM
ATT_KV_W = ATT_KV_HEADS * HEAD_DIM
C_ATT_K = 0
C_ATT_V = C_ATT_K + ATT_KV_W
C_RET_K = C_ATT_V + ATT_KV_W
C_RET_V = C_RET_K + D_RET
C_ATT_Q = C_RET_V + D_RET
C_RET_Q = C_ATT_Q + D_ATT
C_RET_G = C_RET_Q + D_RET
C_CONV = C_RET_G + D_RET
N_EXPERTS = 8
EPS = 1e-6
NEG_INF = -1e30

LANES = 128
SUBLANES = 8
VMEM_LIMIT = 48 * 1024 * 1024
CONV_HALO = 16
CONV_SUB = 16
CONV_CHAINS = 4
ATT_QBLOCKS = 4
RET_UNROLL = 4
MOE_TM = 512
MOE_CHUNKS = 2
MOE_SUB = 512
FFN_SUB = 512
MOE_TF = 1792
SC_LANES = 16
SC_WINDOW = 64


def _params(*sem):
    return pltpu.CompilerParams(dimension_semantics=sem, vmem_limit_bytes=VMEM_LIMIT)


def _sigmoid(x):
    return 1.0 / (1.0 + jnp.exp(-x))


def _pack_bf16_pairs(v):
    c = v.shape[1] // 2
    bits = pltpu.bitcast(v.astype(BF16).astype(F32), jnp.uint32)
    packed = (bits[:, c:] & jnp.uint32(0xFFFF0000)) | (bits[:, :c] >> 16)
    return pltpu.bitcast(packed, jnp.int32)


def _unpack_bf16_pairs(p):
    bits = pltpu.bitcast(p, jnp.uint32)
    return pltpu.bitcast(bits << 16, F32), pltpu.bitcast(bits & jnp.uint32(0xFFFF0000), F32)


def _pick_tile(n, pref):
    t = min(n, pref)
    assert n % t == 0, (n, t)
    return t


def _adaln_kernel(c_ref, w_ref, b_ref, o_ref):
    c = c_ref[...]
    s = c * _sigmoid(c)
    o_ref[...] = jnp.dot(s, w_ref[...], preferred_element_type=F32,
                         precision=lax.Precision.HIGHEST) + b_ref[...]


def _adaln(cond, ada_w, ada_b):
    depth, d, n = ada_w.shape
    tn = _pick_tile(n, 1536)
    return pl.pallas_call(
        _adaln_kernel,
        out_shape=jax.ShapeDtypeStruct((depth, cond.shape[0], n), F32),
        grid=(depth, n // tn),
        in_specs=[pl.BlockSpec(cond.shape, lambda l, j: (0, 0)),
                  pl.BlockSpec((None, d, tn), lambda l, j: (l, 0, j)),
                  pl.BlockSpec((None, 1, tn), lambda l, j: (l, 0, j))],
        out_specs=pl.BlockSpec((None, cond.shape[0], tn), lambda l, j: (l, 0, j)),
        compiler_params=_params("parallel", "parallel"),
        name="adaln",
    )(cond, ada_w, ada_b.reshape(depth, 1, n))


def _modulated_rms(x, nw, sc, sh):
    ms = jnp.mean(x * x, axis=-1, keepdims=True)
    return (x * lax.rsqrt(ms + EPS) * nw) * (1.0 + sc) + sh


def _head_group_matrix():
    r = lax.broadcasted_iota(jnp.int32, (2 * LANES, 2 * LANES), 0) // HEAD_DIM
    c = lax.broadcasted_iota(jnp.int32, (2 * LANES, 2 * LANES), 1) // HEAD_DIM
    return jnp.where(r == c, 1.0, 0.0).astype(BF16)


def _per_head_sum(v, grp):
    hi = v.astype(BF16)
    lo = (v - hi.astype(F32)).astype(BF16)
    r = jnp.dot(jnp.concatenate([hi, lo], axis=1), grp, preferred_element_type=F32)
    return r[:, 0:LANES] + r[:, LANES:]


def _dup_halves(t):
    sw = pltpu.roll(t, HEAD_DIM, 1)
    lo = lax.broadcasted_iota(jnp.int32, t.shape, 1) < HEAD_DIM
    return jnp.where(lo, t, sw), jnp.where(lo, sw, t)


def _inproj_kernel(x_ref, nw_ref, sc_ref, sh_ref, w_ref, cos_ref, sa_ref, sb_ref, qw_ref, kw_ref,
                   q_ref, k_ref, v_ref, rk_ref, rv_ref, rq_ref, rg_ref, cv_ref):
    hb = _modulated_rms(x_ref[...], nw_ref[...], sc_ref[...], sh_ref[...]).astype(BF16)

    def proj(c0, n):
        return jnp.dot(hb, w_ref[:, c0:c0 + n], preferred_element_type=F32)

    grp = _head_group_matrix()
    cos, sa, sb = cos_ref[...], sa_ref[...], sb_ref[...]

    def norm_rope(p, wrow):
        y = p * lax.rsqrt(_per_head_sum(p * p, grp) * (1.0 / HEAD_DIM) + EPS) * wrow
        return (y * cos + pltpu.roll(y, LANES - ROPE_FREQS, 1) * sa
                + pltpu.roll(y, ROPE_FREQS, 1) * sb)

    kv = proj(C_ATT_K, 2 * ATT_KV_W)
    k0, k1 = _dup_halves(norm_rope(kv[:, 0:ATT_KV_W], kw_ref[...]))
    k_ref[:, 0:LANES] = k0.astype(BF16)
    k_ref[:, LANES:2 * LANES] = k1.astype(BF16)
    v_ref[...] = kv[:, ATT_KV_W:].T.astype(BF16)
    qw = qw_ref[...] * ATT_SCALE
    qall = proj(C_ATT_Q, D_ATT)
    for j in range(D_ATT // LANES):
        q_ref[:, j * LANES:(j + 1) * LANES] = norm_rope(qall[:, j * LANES:(j + 1) * LANES], qw).astype(BF16)
    rk_ref[...] = (proj(C_RET_K, D_RET) * RET_K_SCALE).astype(BF16)
    rv_ref[...] = proj(C_RET_V, D_RET).astype(BF16)
    rq_ref[...] = proj(C_RET_Q, D_RET).astype(BF16)
    rg_ref[...] = proj(C_RET_G, D_RET)
    cv_ref[...] = proj(C_CONV, CONV_CH) * _sigmoid(proj(C_CONV + CONV_CH, CONV_CH))


def _inproj(x, nw, sc, sh, w_bf, cos, sa, sb, qw, kw):
    b, s, d = x.shape
    tm = _pick_tile(s, 512)
    row = lambda n: pl.BlockSpec((1, n), lambda bi, i: (0, 0))
    per_b = pl.BlockSpec((None, 1, d), lambda bi, i: (bi, 0, 0))
    tab = pl.BlockSpec((tm, LANES), lambda bi, i: (i, 0))
    tok = lambda n: pl.BlockSpec((None, tm, n), lambda bi, i: (bi, i, 0))
    shp = lambda n, dt: jax.ShapeDtypeStruct((b, s, n), dt)
    return pl.pallas_call(
        _inproj_kernel,
        out_shape=(shp(D_ATT, BF16), shp(2 * LANES, BF16), jax.ShapeDtypeStruct((b, ATT_KV_W, s), BF16),
                   shp(D_RET, BF16), shp(D_RET, BF16), shp(D_RET, BF16), shp(D_RET, F32),
                   shp(CONV_CH, F32)),
        grid=(b, s // tm),
        in_specs=[tok(d), row(d), per_b, per_b,
                  pl.BlockSpec(w_bf.shape, lambda bi, i: (0, 0)),
                  tab, tab, tab, row(LANES), row(LANES)],
        out_specs=(tok(D_ATT), tok(2 * LANES), pl.BlockSpec((None, ATT_KV_W, tm), lambda bi, i: (bi, 0, i)),
                   tok(D_RET), tok(D_RET), tok(D_RET), tok(D_RET), tok(CONV_CH)),
        compiler_params=_params("parallel", "parallel"),
        name="inproj",
    )(x, nw, sc, sh, w_bf, cos, sa, sb, qw, kw)


def _attn_kernel(*refs, window, nq):
    blk = ATT_BLOCK
    if window:
        q_ref = refs[0]
        k_refs = refs[1:nq + 3]
        v_refs = refs[nq + 3:2 * nq + 5]
        kx_ref, vx_ref, sink_ref, o_ref = refs[2 * nq + 5:]
    else:
        q_ref, kx_ref, vx_ref, sink_ref, o_ref = refs
    n_ctx = kx_ref.shape[0]
    nk = 3 * blk + n_ctx if window else n_ctx
    if window:
        i = pl.program_id(1)
        last = nq * pl.num_programs(1) - 1
        key = lax.broadcasted_iota(jnp.int32, (blk, 2 * blk), 0)
        qry = lax.broadcasted_iota(jnp.int32, (blk, 2 * blk), 1) & (blk - 1)
        mask_prev = [(key >= qry) & (nq * i + sub > 0) for sub in range(nq)]
        mask_next = [(key <= qry) & (nq * i + sub < last) for sub in range(nq)]
    first_head = lax.broadcasted_iota(jnp.int32, (blk, LANES), 1) < HEAD_DIM
    ones = jnp.ones((2 * SUBLANES, nk), BF16)

    def scores(sub, pair):
        g = pair // 2
        gs = slice(g * LANES, (g + 1) * LANES)
        if window:
            kcat = jnp.concatenate([r[:, gs] for r in k_refs[sub:sub + 3]] + [kx_ref[:, gs]], axis=0)
        else:
            kcat = kx_ref[:, gs]
        qp = q_ref[sub * blk:(sub + 1) * blk, pair * LANES:(pair + 1) * LANES]
        zero = jnp.zeros_like(qp)
        w = jnp.concatenate([jnp.where(first_head, qp, zero), jnp.where(first_head, zero, qp)], axis=0)
        return lax.dot_general(kcat, w, (((1,), (1,)), ((), ())), preferred_element_type=F32)

    def softmax(sub, pair, s):
        if window:
            parts = [jnp.where(mask_prev[sub], s[0:blk], NEG_INF), s[blk:2 * blk],
                     jnp.where(mask_next[sub], s[2 * blk:3 * blk], NEG_INF), s[3 * blk:]]
        else:
            parts = [s]
        snk = sink_ref[pair:pair + 1, :]
        m = snk
        for part in parts:
            m = jnp.maximum(m, jnp.max(part, axis=0, keepdims=True))
        p = jnp.concatenate([jnp.exp(part - m).astype(BF16) for part in parts], axis=0)
        return p, jnp.exp(snk - m)

    def output(sub, pair, p, sink_p):
        g = pair // 2
        vs = slice(g * HEAD_DIM, (g + 1) * HEAD_DIM)
        if window:
            vt = jnp.concatenate([r[vs, :] for r in v_refs[sub:sub + 3]] + [vx_ref[vs, :]], axis=1)
        else:
            vt = vx_ref[vs, :]
        vaug = jnp.concatenate([vt, ones], axis=0)
        o = jnp.dot(vaug, p, preferred_element_type=F32)
        on = o[0:HEAD_DIM, :] * (1.0 / (o[HEAD_DIM:HEAD_DIM + 1, :] + sink_p))
        ot = jnp.concatenate([on[:, 0:blk], on[:, blk:2 * blk]], axis=0)
        o_ref[sub * blk:(sub + 1) * blk, pair * LANES:(pair + 1) * LANES] = ot.T.astype(BF16)

    units = [(sub, pair) for sub in range(nq) for pair in range(ATT_Q_HEADS // 2)]
    s_next = scores(*units[0])
    for n, unit in enumerate(units):
        s_cur = s_next
        if n + 1 < len(units):
            s_next = scores(*units[n + 1])
        output(*unit, *softmax(*unit, s_cur))


def _attention(q, k, vt, kx, vxt, sink_tab, window):
    b, s, _ = q.shape
    blk = ATT_BLOCK
    nb = s // blk
    nq = ATT_QBLOCKS if nb % ATT_QBLOCKS == 0 else 1
    n_ctx = kx.shape[1]
    qspec = pl.BlockSpec((None, nq * blk, D_ATT), lambda bi, i: (bi, i, 0))
    kctx = pl.BlockSpec((None, n_ctx, 2 * LANES), lambda bi, i: (bi, 0, 0))
    vctx = pl.BlockSpec((None, ATT_KV_W, n_ctx), lambda bi, i: (bi, 0, 0))
    snk = pl.BlockSpec(sink_tab.shape, lambda bi, i: (0, 0))
    if window:
        at = lambda off: (lambda i: jnp.clip(nq * i + off, 0, nb - 1))
        kspec = lambda f: pl.BlockSpec((None, blk, 2 * LANES), lambda bi, i: (bi, f(i), 0))
        vspec = lambda f: pl.BlockSpec((None, ATT_KV_W, blk), lambda bi, i: (bi, 0, f(i)))
        offs = range(-1, nq + 1)
        in_specs = ([qspec] + [kspec(at(o)) for o in offs] + [vspec(at(o)) for o in offs]
                    + [kctx, vctx, snk])
        args = (q,) + (k,) * (nq + 2) + (vt,) * (nq + 2) + (kx, vxt, sink_tab)
    else:
        in_specs = [qspec, kctx, vctx, snk]
        args = (q, kx, vxt, sink_tab)
    return pl.pallas_call(
        functools.partial(_attn_kernel, window=window, nq=nq),
        out_shape=jax.ShapeDtypeStruct((b, s, D_ATT), BF16),
        grid=(b, nb // nq),
        in_specs=in_specs,
        out_specs=qspec,
        compiler_params=_params("parallel", "parallel"),
        name="attention_window" if window else "attention_ctx",
    )(*args)


def _ret_state_kernel(lgl_ref, k_ref, v_ref, s0_ref, r_ref, fin_ref, s_scr, *, cb, reverse):
    c = pl.program_id(1)
    L = RET_CHUNK

    @pl.when(c == 0)
    def _():
        s_scr[...] = s0_ref[...]

    jj = lax.broadcasted_iota(jnp.int32, (L, LANES), 0).astype(F32)
    expo = jj if reverse else (L - 1.0) - jj
    same_head = (lax.broadcasted_iota(jnp.int32, (LANES, LANES), 0) // HEAD_DIM
                 == lax.broadcasted_iota(jnp.int32, (LANES, LANES), 1) // HEAD_DIM)
    for pr in range(RET_HEADS // 2):
        cs = slice(pr * LANES, (pr + 1) * LANES)
        lgl = lgl_ref[pr]
        kdec = jnp.exp(expo * lgl)
        cdec = jnp.exp(float(L) * lgl)
        state = s_scr[pr]
        for t in range(cb):
            cc = cb - 1 - t if reverse else t
            rows = slice(cc * L, (cc + 1) * L)
            r_ref[cc, pr] = state
            kd = k_ref[rows, cs].astype(F32) * kdec
            u = jnp.dot(kd.T.astype(BF16), v_ref[rows, cs], preferred_element_type=F32)
            state = cdec * state + jnp.where(same_head, u, 0.0)
        s_scr[pr] = state

    @pl.when(c == pl.num_programs(1) - 1)
    def _():
        fin_ref[...] = s_scr[...]


def _ret_states(lgl, rk, rv, s0, reverse):
    b, s, _ = rk.shape
    nc = s // RET_CHUNK
    cb = _pick_tile(nc, 8)
    nblk = nc // cb
    npair = RET_HEADS // 2
    blk_idx = (lambda c: nblk - 1 - c) if reverse else (lambda c: c)
    tok = pl.BlockSpec((None, cb * RET_CHUNK, D_RET), lambda bi, c: (bi, blk_idx(c), 0))
    st = pl.BlockSpec((None, npair, LANES, LANES), lambda bi, c: (bi, 0, 0, 0))
    return pl.pallas_call(
        functools.partial(_ret_state_kernel, cb=cb, reverse=reverse),
        out_shape=(jax.ShapeDtypeStruct((b, nc, npair, LANES, LANES), F32),
                   jax.ShapeDtypeStruct((b, npair, LANES, LANES), F32)),
        grid=(b, nblk),
        in_specs=[pl.BlockSpec(lgl.shape, lambda bi, c: (0, 0, 0)), tok, tok, st],
        out_specs=(pl.BlockSpec((None, cb, npair, LANES, LANES),
                                lambda bi, c: (bi, blk_idx(c), 0, 0, 0)), st),
        scratch_shapes=[pltpu.VMEM((npair, LANES, LANES), F32)],
        compiler_params=_params("parallel", "arbitrary"),
        name="ret_state_bwd" if reverse else "ret_state_fwd",
    )(lgl, rk, rv, s0)


def _ret_out_kernel(lgf_ref, lgb_ref, q_ref, k_ref, v_ref, g_ref, rf_ref, rb_ref, gnw_ref, o_ref, *, cb):
    L = RET_CHUNK
    i0 = lax.broadcasted_iota(jnp.int32, (L, LANES), 0).astype(F32)
    i1 = lax.broadcasted_iota(jnp.int32, (L, LANES), 1).astype(F32)
    diff = i0 - i1
    lo = lax.broadcasted_iota(jnp.int32, (L, LANES), 1) < HEAD_DIM
    grp = _head_group_matrix()
    inv = 1.0 / HEAD_DIM
    tabs = []
    for pr in range(RET_HEADS // 2):
        lgf, lgb = lgf_ref[pr], lgb_ref[pr]
        dmat = [jnp.where(diff >= 0.0,
                          jnp.exp(jnp.maximum(diff, 0.0) * lgf[:, a:a + 1]),
                          jnp.exp(jnp.maximum(-diff, 0.0) * lgb[:, a:a + 1]))
                for a in (0, HEAD_DIM)]
        tabs.append((jnp.exp((i0 + 1.0) * lgf), jnp.exp((float(L) - i0) * lgb),
                     jnp.concatenate(dmat, axis=0), gnw_ref[:, pr * LANES:(pr + 1) * LANES]))

    def decayed_scores(t, pr):
        rows = pl.ds(pl.multiple_of(t * L, L), L)
        cs = slice(pr * LANES, (pr + 1) * LANES)
        qp, kp = q_ref[rows, cs], k_ref[rows, cs]
        zero = jnp.zeros_like(kp)
        qz = jnp.concatenate([jnp.where(lo, qp, zero), jnp.where(lo, zero, qp)], axis=0)
        return lax.dot_general(qz, kp, (((1,), (1,)), ((), ())), preferred_element_type=F32) * tabs[pr][2]

    def readout(t, pr, a):
        rows = pl.ds(pl.multiple_of(t * L, L), L)
        cs = slice(pr * LANES, (pr + 1) * LANES)
        xif, xib, _, gnw = tabs[pr]
        vp = v_ref[rows, cs]
        zero = jnp.zeros_like(vp)
        qf = q_ref[rows, cs].astype(F32)
        lhs = jnp.concatenate([a[0:L].astype(BF16), a[L:].astype(BF16),
                               (qf * xif).astype(BF16), (qf * xib).astype(BF16)], axis=1)
        rhs = jnp.concatenate([jnp.where(lo, vp, zero), jnp.where(lo, zero, vp),
                               rf_ref[t, pr].astype(BF16), rb_ref[t, pr].astype(BF16)], axis=0)
        acc = jnp.dot(lhs, rhs, preferred_element_type=F32)
        dl = acc - _per_head_sum(acc, grp) * inv
        var = _per_head_sum(dl * dl, grp) * inv
        y = dl * lax.rsqrt(var + EPS) * gnw
        gt = g_ref[rows, cs]
        o_ref[rows, cs] = (gt * _sigmoid(gt) * y).astype(BF16)

    unroll = min(RET_UNROLL, cb)

    def group(gi, carry):
        units = [(gi * unroll + u, pr) for u in range(unroll) for pr in range(RET_HEADS // 2)]
        a_next = decayed_scores(*units[0])
        for n, unit in enumerate(units):
            a_cur = a_next
            if n + 1 < len(units):
                a_next = decayed_scores(*units[n + 1])
            readout(*unit, a_cur)
        return carry

    lax.fori_loop(0, cb // unroll, group, 0)


def _ret_out(lgf, lgb, rq, rk, rv, rg, rf, rb, gnw):
    b, s, _ = rq.shape
    nc = s // RET_CHUNK
    cb = _pick_tile(nc, 8)
    npair = RET_HEADS // 2
    tok = pl.BlockSpec((None, cb * RET_CHUNK, D_RET), lambda bi, c: (bi, c, 0))
    st = pl.BlockSpec((None, cb, npair, LANES, LANES), lambda bi, c: (bi, c, 0, 0, 0))
    lg = pl.BlockSpec(lgf.shape, lambda bi, c: (0, 0, 0))
    return pl.pallas_call(
        functools.partial(_ret_out_kernel, cb=cb),
        out_shape=jax.ShapeDtypeStruct((b, s, D_RET), BF16),
        grid=(b, nc // cb),
        in_specs=[lg, lg, tok, tok, tok, tok, st, st, pl.BlockSpec((1, D_RET), lambda bi, c: (0, 0))],
        out_specs=tok,
        compiler_params=_params("parallel", "parallel"),
        name="ret_out",
    )(lgf, lgb, rq, rk, rv, rg, rf, rb, gnw)


def _conv_kernel(prev_ref, cur_ref, next_ref, w_ref, b_ref, lnw_ref, lnb_ref, o_ref, xpad, shifted, hbuf):
    i = pl.program_id(1)
    tm = cur_ref.shape[0]
    halo = CONV_HALO
    first = i == 0
    last = i == pl.num_programs(1) - 1
    xpad[0:halo, :] = jnp.where(first, 0.0, prev_ref[...])
    xpad[halo:halo + tm, :] = cur_ref[...]
    xpad[halo + tm:2 * halo + tm, :] = jnp.where(last, 0.0, next_ref[...])
    span = tm + 2 * halo - SUBLANES
    for r in range(SUBLANES):
        shifted[r] = xpad[r:r + span, :]
    base = halo - CONV_WIDTH // 2
    bias = b_ref[...]

    def sub(sb, carry):
        r0 = pl.multiple_of(sb * CONV_SUB, CONV_SUB)
        chains = [None] * CONV_CHAINS
        for w in range(CONV_WIDTH):
            off = base + w
            xs = shifted[off % SUBLANES, pl.ds(r0 + (off // SUBLANES) * SUBLANES, CONV_SUB), :]
            term = xs * w_ref[w:w + 1, :]
            c = w % CONV_CHAINS
            chains[c] = term if chains[c] is None else chains[c] + term
        hbuf[pl.ds(r0, CONV_SUB), :] = (chains[0] + chains[1]) + (chains[2] + chains[3]) + bias
        return carry

    lax.fori_loop(0, tm // CONV_SUB, sub, 0, unroll=2)
    h = hbuf[...]
    mu = jnp.mean(h, axis=-1, keepdims=True)
    dl = h - mu
    var = jnp.mean(dl * dl, axis=-1, keepdims=True)
    y = dl * lax.rsqrt(var + EPS) * lnw_ref[...] + lnb_ref[...]
    o_ref[...] = (y * _sigmoid(y)).astype(BF16)


def _conv(cv, w, bias, lnw, lnb):
    b, s, ch = cv.shape
    tm = _pick_tile(s, 512)
    hpb = tm // CONV_HALO
    nh = s // CONV_HALO
    row = pl.BlockSpec((1, ch), lambda bi, i: (0, 0))
    span = tm + 2 * CONV_HALO - SUBLANES
    return pl.pallas_call(
        _conv_kernel,
        out_shape=jax.ShapeDtypeStruct((b, s, ch), BF16),
        grid=(b, s // tm),
        in_specs=[pl.BlockSpec((None, CONV_HALO, ch), lambda bi, i: (bi, jnp.maximum(i * hpb - 1, 0), 0)),
                  pl.BlockSpec((None, tm, ch), lambda bi, i: (bi, i, 0)),
                  pl.BlockSpec((None, CONV_HALO, ch),
                               lambda bi, i: (bi, jnp.minimum((i + 1) * hpb, nh - 1), 0)),
                  pl.BlockSpec(w.shape, lambda bi, i: (0, 0)), row, row, row],
        out_specs=pl.BlockSpec((None, tm, ch), lambda bi, i: (bi, i, 0)),
        scratch_shapes=[pltpu.VMEM((tm + 2 * CONV_HALO, ch), F32),
                        pltpu.VMEM((SUBLANES, span, ch), F32),
                        pltpu.VMEM((tm, ch), F32)],
        compiler_params=_params("parallel", "parallel"),
        name="conv",
    )(cv, cv, cv, w, bias, lnw, lnb)


def _outproj_kernel(*refs, route):
    if route:
        (att_ref, ret_ref, cnv_ref, w_ref, x_ref, g1_ref, nw_ref, sc_ref, sh_ref, rcat_ref,
         xo_ref, h_ref, rt_ref) = refs
    else:
        att_ref, ret_ref, cnv_ref, w_ref, x_ref, g1_ref, nw_ref, sc_ref, sh_ref, xo_ref, h_ref = refs
    y = (jnp.dot(att_ref[...], w_ref[0:D_ATT, :], preferred_element_type=F32)
         + jnp.dot(ret_ref[...], w_ref[D_ATT:D_ATT + D_RET, :], preferred_element_type=F32)
         + jnp.dot(cnv_ref[...], w_ref[D_ATT + D_RET:, :], preferred_element_type=F32))
    xn = x_ref[...] + g1_ref[...] * y
    xo_ref[...] = xn
    h = _modulated_rms(xn, nw_ref[...], sc_ref[...], sh_ref[...])
    h_ref[...] = _pack_bf16_pairs(h) if route else h.astype(BF16)
    if route:
        hi = h.astype(BF16)
        lo = (h - hi.astype(F32)).astype(BF16)
        tm = h.shape[0]
        r = jnp.dot(jnp.concatenate([hi, lo], axis=0), rcat_ref[...], preferred_element_type=F32)
        logits = (r[0:tm, 0:LANES] + r[0:tm, LANES:]) + (r[tm:, 0:LANES] + r[tm:, LANES:])
        lane = lax.broadcasted_iota(jnp.int32, logits.shape, 1).astype(F32)
        logits = jnp.where(lane < N_EXPERTS, logits, NEG_INF)
        m1 = jnp.max(logits, axis=-1, keepdims=True)
        i1 = jnp.min(jnp.where(logits == m1, lane, float(LANES)), axis=-1, keepdims=True)
        rest = jnp.where(lane == i1, NEG_INF, logits)
        m2 = jnp.max(rest, axis=-1, keepdims=True)
        i2 = jnp.min(jnp.where(rest == m2, lane, float(LANES)), axis=-1, keepdims=True)
        e2 = jnp.exp(m2 - m1)
        w1 = 1.0 / (1.0 + e2)
        w2 = e2 / (1.0 + e2)
        rt_ref[...] = jnp.where(lane == 0.0, i1,
                                jnp.where(lane == 1.0, i2,
                                          jnp.where(lane == 2.0, w1, jnp.where(lane == 3.0, w2, 0.0))))


def _outproj(att, ret, cnv, w_bf, x, g1, nw, sc, sh, router=None):
    b, s, d = x.shape
    tm = _pick_tile(s, 512)
    tok = lambda n: pl.BlockSpec((None, tm, n), lambda bi, i: (bi, i, 0))
    per_b = pl.BlockSpec((None, 1, d), lambda bi, i: (bi, 0, 0))
    row = pl.BlockSpec((1, d), lambda bi, i: (0, 0))
    in_specs = [tok(D_ATT), tok(D_RET), tok(CONV_CH), pl.BlockSpec(w_bf.shape, lambda bi, i: (0, 0)),
                tok(d), per_b, row, per_b, per_b]
    args = [att, ret, cnv, w_bf, x, g1, nw, sc, sh]
    h_shape = (b, s, d // 2) if router is not None else (b, s, d)
    out_shape = [jax.ShapeDtypeStruct((b, s, d), F32),
                 jax.ShapeDtypeStruct(h_shape, jnp.int32 if router is not None else BF16)]
    out_specs = [tok(d), tok(h_shape[2])]
    if router is not None:
        rpad = jnp.zeros((d, LANES), F32).at[:, :N_EXPERTS].set(router)
        rhi = rpad.astype(BF16)
        rlo = (rpad - rhi.astype(F32)).astype(BF16)
        in_specs += [pl.BlockSpec((d, 2 * LANES), lambda bi, i: (0, 0))]
        args += [jnp.concatenate([rhi, rlo], axis=1)]
        out_shape.append(jax.ShapeDtypeStruct((b, s, LANES), F32))
        out_specs.append(tok(LANES))
    return pl.pallas_call(
        functools.partial(_outproj_kernel, route=router is not None),
        out_shape=tuple(out_shape),
        grid=(b, s // tm),
        in_specs=in_specs,
        out_specs=tuple(out_specs),
        compiler_params=_params("parallel", "parallel"),
        name="outproj_route" if router is not None else "outproj",
    )(*args)


def _swiglu_chunks(h, wg_ref, wu_ref, wd_ref, sub):
    total = None
    for c0 in range(0, wg_ref.shape[1], sub):
        cs = slice(c0, min(c0 + sub, wg_ref.shape[1]))
        gate = jnp.dot(h, wg_ref[:, cs].astype(BF16), preferred_element_type=F32)
        up = jnp.dot(h, wu_ref[:, cs].astype(BF16), preferred_element_type=F32)
        act = (gate * _sigmoid(gate) * up).astype(BF16)
        part = jnp.dot(act, wd_ref[cs, :].astype(BF16), preferred_element_type=F32)
        total = part if total is None else total + part
    return total


def _ffn_kernel(h_ref, wg_ref, wu_ref, wd_ref, x_ref, g2_ref, o_ref):
    y = _swiglu_chunks(h_ref[...], wg_ref, wu_ref, wd_ref, FFN_SUB)
    o_ref[...] = x_ref[...] + g2_ref[...] * y


def _ffn(h, wg, wu, wd, x, g2):
    b, s, d = x.shape
    f = wg.shape[1]
    tm = _pick_tile(s, 512)
    tok = lambda: pl.BlockSpec((None, tm, d), lambda bi, i: (bi, i, 0))
    res = lambda shape: pl.BlockSpec(shape, lambda bi, i: (0, 0), pipeline_mode=pl.Buffered(1))
    return pl.pallas_call(
        _ffn_kernel,
        out_shape=jax.ShapeDtypeStruct((b, s, d), F32),
        grid=(b, s // tm),
        in_specs=[tok(), res((d, f)), res((d, f)), res((f, d)), tok(),
                  pl.BlockSpec((None, 1, d), lambda bi, i: (bi, 0, 0))],
        out_specs=tok(),
        compiler_params=_params("parallel", "parallel"),
        name="ffn_dense",
    )(h, wg, wu, wd, x, g2)


def _gather_rows(idx, src):
    n = idx.shape[0]
    d = src.shape[1]
    mesh = plsc.VectorSubcoreMesh(core_axis_name="core", subcore_axis_name="subcore")

    @pl.kernel(out_type=jax.ShapeDtypeStruct((n, d), src.dtype), mesh=mesh, name="moe_gather")
    def gather(src_hbm, idx_hbm, out_hbm):
        def body(idx_vmem, out_vmem):
            pltpu.sync_copy(src_hbm.at[idx_vmem.at[0, pl.ds(0, SC_WINDOW)]], out_vmem)

        pltpu.emit_pipeline(
            body,
            grid=(n // SC_WINDOW,),
            in_specs=[pl.BlockSpec((1, LANES), lambda i: (i, 0))],
            out_specs=[pl.BlockSpec((SC_WINDOW, d), lambda i: (i, 0))],
            core_axis_name=("core", "subcore"),
            dimension_semantics=(pltpu.PARALLEL,),
        )(idx_hbm, out_hbm)

    idx_rows = jnp.pad(idx.reshape(n // SC_WINDOW, SC_WINDOW), ((0, 0), (0, LANES - SC_WINDOW)))
    return gather(src, idx_rows)


def _invert_rows(dest, n_rows):
    n = dest.shape[0]
    n_tok = n // 2
    mesh = plsc.VectorSubcoreMesh(core_axis_name="core", subcore_axis_name="subcore")

    @pl.kernel(out_type=jax.ShapeDtypeStruct((n_rows,), jnp.int32), mesh=mesh, name="moe_row_src",
               scratch_types=[pltpu.VMEM((n,), jnp.int32), pltpu.VMEM((n_rows,), jnp.int32)],
               compiler_params=pltpu.CompilerParams(needs_layout_passes=False))
    def invert(dest_hbm, out_hbm, dest_vmem, rows_vmem):
        @pl.when((lax.axis_index("core") == 0) & (lax.axis_index("subcore") == 0))
        def _():
            pltpu.sync_copy(dest_hbm, dest_vmem)

            @pl.loop(0, n_rows, step=SC_LANES)
            def _(r):
                rows_vmem[pl.ds(r, SC_LANES)] = lax.rem(lax.iota(jnp.int32, SC_LANES) + r, n_tok)

            @pl.loop(0, n, step=SC_LANES)
            def _(a):
                tok = lax.iota(jnp.int32, SC_LANES) + a
                tok = jnp.where(tok >= n_tok, tok - n_tok, tok)
                plsc.store_scatter(rows_vmem, [dest_vmem[pl.ds(a, SC_LANES)]], tok)

            pltpu.sync_copy(rows_vmem, out_hbm)

    return invert(dest)


def _moe_kernel(te_ref, nu_ref, x_ref, wg_ref, wu_ref, wd_ref, *rest, tile0):
    o_ref, xb, acc = rest[-3:]
    t = pl.program_id(0) + tile0
    j = pl.program_id(1)
    nj = pl.num_programs(1)
    used = t < nu_ref[0]

    @pl.when(used & (j == 0))
    def _():
        lo, hi = _unpack_bf16_pairs(x_ref[...])
        half = lo.shape[1]
        xb[:, 0:half] = lo.astype(BF16)
        xb[:, half:] = hi.astype(BF16)
        acc[...] = jnp.zeros_like(acc)

    @pl.when(used)
    def _():
        acc[...] += _swiglu_chunks(xb[...], wg_ref, wu_ref, wd_ref, MOE_SUB)

    @pl.when(used & (j == nj - 1))
    def _():
        o_ref[...] = _pack_bf16_pairs(acc[...])

    @pl.when(jnp.logical_not(used) & (j == nj - 1))
    def _():
        o_ref[...] = jnp.zeros_like(o_ref)


def _moe_grouped(tile_expert, n_used, xs, wg, wu, wd, y_prev, tile0, n_rows):
    p, dp = xs.shape
    d = 2 * dp
    f = wg.shape[2]
    tm = MOE_TM
    tf = MOE_TF
    nj = f // tf

    def jj(t, j, te, nu):
        return jnp.where(t + tile0 < nu[0], j, nj - 1)

    in_specs = [pl.BlockSpec((tm, dp), lambda t, j, te, nu: (t, 0)),
                pl.BlockSpec((None, d, tf), lambda t, j, te, nu: (te[t + tile0], 0, jj(t, j, te, nu))),
                pl.BlockSpec((None, d, tf), lambda t, j, te, nu: (te[t + tile0], 0, jj(t, j, te, nu))),
                pl.BlockSpec((None, tf, d), lambda t, j, te, nu: (te[t + tile0], jj(t, j, te, nu), 0))]
    args = [tile_expert, n_used, xs, wg, wu, wd]
    aliases = {}
    if y_prev is not None:
        in_specs.append(pl.BlockSpec(memory_space=pl.ANY))
        args.append(y_prev)
        aliases = {len(args) - 1: 0}
    return pl.pallas_call(
        functools.partial(_moe_kernel, tile0=tile0),
        out_shape=jax.ShapeDtypeStruct((n_rows, dp), jnp.int32),
        grid_spec=pltpu.PrefetchScalarGridSpec(
            num_scalar_prefetch=2,
            grid=(p // tm, nj),
            in_specs=in_specs,
            out_specs=pl.BlockSpec((tm, dp), lambda t, j, te, nu: (t + tile0, 0)),
            scratch_shapes=[pltpu.VMEM((tm, d), BF16), pltpu.VMEM((tm, d), F32)]),
        input_output_aliases=aliases,
        compiler_params=_params("arbitrary", "arbitrary"),
        name="moe_grouped",
    )(*args)


def _combine_kernel(y1_ref, y2_ref, x_ref, g2_ref, rt_ref, o_ref):
    rt = rt_ref[...]
    w1, w2 = rt[:, 2:3], rt[:, 3:4]
    half = y1_ref.shape[1]
    for k, (a, c) in enumerate(zip(_unpack_bf16_pairs(y1_ref[...]), _unpack_bf16_pairs(y2_ref[...]))):
        cs = slice(k * half, (k + 1) * half)
        o_ref[:, cs] = x_ref[:, cs] + g2_ref[:, cs] * (w1 * a + w2 * c)


def _moe_combine(yg, x, g2, route):
    b, s, d = x.shape
    tm = _pick_tile(s, 512)
    tok = lambda n: pl.BlockSpec((None, tm, n), lambda bi, i: (bi, i, 0))
    return pl.pallas_call(
        _combine_kernel,
        out_shape=jax.ShapeDtypeStruct((b, s, d), F32),
        grid=(b, s // tm),
        in_specs=[tok(d // 2), pl.BlockSpec((None, tm, d // 2), lambda bi, i: (b + bi, i, 0)), tok(d),
                  pl.BlockSpec((None, 1, d), lambda bi, i: (bi, 0, 0)), tok(LANES)],
        out_specs=tok(d),
        compiler_params=_params("parallel", "parallel"),
        name="moe_combine",
    )(yg, yg, x, g2, route)


def _moe(h, route, x, g2, wg, wu, wd):
    b, s, d = x.shape
    n_tok = b * s
    tm = MOE_TM
    rt = route.reshape(n_tok, LANES)
    flat_e = jnp.concatenate([rt[:, 0], rt[:, 1]]).astype(jnp.int32)
    onehot = (flat_e[:, None] == jnp.arange(N_EXPERTS, dtype=jnp.int32)[None, :]).astype(jnp.int32)
    csum = jnp.cumsum(onehot, axis=0)
    rank = jnp.sum(csum * onehot, axis=1) - 1
    counts = csum[-1]
    tiles_e = (counts + tm - 1) // tm
    tiles_cum = jnp.cumsum(tiles_e)
    row_start = (tiles_cum - tiles_e) * tm
    dest = jnp.sum(onehot * row_start[None, :], axis=1) + rank
    n_tiles = 2 * n_tok // tm + N_EXPERTS
    tile_ids = jnp.arange(n_tiles, dtype=jnp.int32)
    tile_expert = jnp.sum((tile_ids[:, None] >= tiles_cum[None, :]).astype(jnp.int32), axis=1)
    last_e = jnp.max(jnp.where(tiles_e > 0, jnp.arange(N_EXPERTS, dtype=jnp.int32), 0))
    tile_expert = jnp.minimum(tile_expert, last_e).astype(jnp.int32)
    n_used = tiles_cum[-1:].astype(jnp.int32)
    dest = dest.astype(jnp.int32)
    row_src = _invert_rows(dest, n_tiles * tm)
    n_chunks = max(c for c in range(1, MOE_CHUNKS + 1) if n_tiles % c == 0)
    tiles_c = n_tiles // n_chunks
    h_flat = h.reshape(n_tok, d // 2)
    xs = [_gather_rows(row_src[c * tiles_c * tm:(c + 1) * tiles_c * tm], h_flat) for c in range(n_chunks)]
    y = None
    for c in range(n_chunks):
        y = _moe_grouped(tile_expert, n_used, xs[c], wg, wu, wd, y, c * tiles_c, n_tiles * tm)
    yg = _gather_rows(dest, y).reshape(2 * b, s, d // 2)
    return _moe_combine(yg, x, g2, route)


def _rope_tables(n):
    rows = n // GRID_W
    r = jnp.repeat(jnp.arange(rows), GRID_W).astype(F32)
    col = jnp.tile(jnp.arange(GRID_W), rows).astype(F32)
    freqs = ROPE_BASE ** (-jnp.arange(ROPE_FREQS, dtype=F32) / ROPE_FREQS)
    ang = jnp.stack([r[:, None] * freqs, col[:, None] * freqs], axis=1)
    ang = jnp.repeat(ang[:, :, None, :], 2, axis=2).reshape(n, HEAD_DIM)
    ang = jnp.tile(ang, (1, LANES // HEAD_DIM))
    cos, sin = jnp.cos(ang), jnp.sin(ang)
    first_half = (jnp.arange(LANES) % (2 * ROPE_FREQS)) < ROPE_FREQS
    return cos, jnp.where(first_half, -sin, 0.0), jnp.where(first_half, 0.0, sin)


def _lane_rows(lg):
    return jnp.repeat(lg.astype(F32), HEAD_DIM).reshape(RET_HEADS // 2, 1, LANES)


def kernel(x, c, ctx, c_ctx, ada_w, ada_b, norm1_w, norm2_w, w_in, w_out, q_norm_w, k_norm_w,
           attn_sink, ret_decay_f, ret_decay_b, ret_gn_w, conv_w, conv_b, conv_ln_w, conv_ln_b,
           ffn_w_gate, ffn_w_up, ffn_w_down, router_w, moe_w_gate, moe_w_up, moe_w_down):
    b, n, d = x.shape
    n_ctx = ctx.shape[1]
    depth = ada_w.shape[0]
    cond = jnp.zeros((SUBLANES, d), F32).at[0:b].set(c).at[b].set(c_ctx)
    mods = _adaln(cond, ada_w, ada_b).reshape(depth, SUBLANES, 6, d)
    cos, sa, sb = _rope_tables(n)
    ones_c = jnp.ones((n_ctx, LANES), F32)
    zeros_c = jnp.zeros((n_ctx, LANES), F32)
    zero_state = jnp.zeros((b, RET_HEADS // 2, LANES, LANES), F32)
    row = lambda v: v.reshape(1, -1)
    x, moe_early = lax.optimization_barrier((x, (moe_w_gate.astype(BF16), moe_w_up.astype(BF16))))
    moe_bf = moe_early + (moe_w_down.astype(BF16),)
    for l in range(depth):
        last = l == depth - 1
        m_lat = [mods[l, 0:b, k][:, None, :] for k in range(6)]
        m_ctx = [jnp.broadcast_to(mods[l, b, k][None, None, :], (b, 1, d)) for k in range(6)]
        w_in_bf = w_in[l].astype(BF16)
        w_out_bf = w_out[l].astype(BF16)
        qw = row(jnp.tile(q_norm_w[l], LANES // HEAD_DIM))
        kw = row(jnp.tile(k_norm_w[l], LANES // HEAD_DIM))
        lgf = _lane_rows(jax.nn.log_sigmoid(ret_decay_f[l].astype(F32)))
        lgb = _lane_rows(jax.nn.log_sigmoid(ret_decay_b[l].astype(F32)))
        sink_tab = jnp.repeat(attn_sink[l].astype(F32), ATT_BLOCK).reshape(ATT_Q_HEADS // 2, 2 * ATT_BLOCK)

        q, k, v, rk, rv, rq, rg, cv = _inproj(x, row(norm1_w[l]), m_lat[1], m_lat[0], w_in_bf,
                                               cos, sa, sb, qw, kw)
        qc, kc, vc, rkc, rvc, rqc, rgc, cvc = _inproj(ctx, row(norm1_w[l]), m_ctx[1], m_ctx[0], w_in_bf,
                                                       ones_c, zeros_c, zeros_c, qw, kw)
        rf_c, s_f = _ret_states(lgf, rkc, rvc, zero_state, reverse=False)
        rb_c, s_b = _ret_states(lgb, rkc, rvc, zero_state, reverse=True)
        rf, _ = _ret_states(lgf, rk, rv, s_f, reverse=False)
        rb, _ = _ret_states(lgb, rk, rv, s_b, reverse=True)

        att = _attention(q, k, v, kc, vc, sink_tab, window=True)
        ret = _ret_out(lgf, lgb, rq, rk, rv, rg, rf, rb, row(ret_gn_w[l]))
        cnv = _conv(cv, conv_w[l], row(conv_b[l]), row(conv_ln_w[l]), row(conv_ln_b[l]))

        if l % 2 == 0:
            i = l // 2
            wg, wu, wd = ffn_w_gate[i].astype(BF16), ffn_w_up[i].astype(BF16), ffn_w_down[i].astype(BF16)
            x_mid, h2 = _outproj(att, ret, cnv, w_out_bf, x, m_lat[2], row(norm2_w[l]), m_lat[4], m_lat[3])
            x_new = _ffn(h2, wg, wu, wd, x_mid, m_lat[5])
        else:
            i = l // 2
            wg, wu, wd = moe_bf[0][i], moe_bf[1][i], moe_bf[2][i]
            x_mid, h2, route = _outproj(att, ret, cnv, w_out_bf, x, m_lat[2], row(norm2_w[l]),
                                        m_lat[4], m_lat[3], router=router_w[i])
            x_new = _moe(h2, route, x_mid, m_lat[5], wg, wu, wd)

        if not last:
            att_c = _attention(qc, None, None, kc, vc, sink_tab, window=False)
            ret_c = _ret_out(lgf, lgb, rqc, rkc, rvc, rgc, rf_c, rb_c, row(ret_gn_w[l]))
            cnv_c = _conv(cvc, conv_w[l], row(conv_b[l]), row(conv_ln_w[l]), row(conv_ln_b[l]))
            if l % 2 == 0:
                c_mid, h2c = _outproj(att_c, ret_c, cnv_c, w_out_bf, ctx, m_ctx[2], row(norm2_w[l]),
                                      m_ctx[4], m_ctx[3])
                ctx = _ffn(h2c, wg, wu, wd, c_mid, m_ctx[5])
            else:
                c_mid, h2c, route_c = _outproj(att_c, ret_c, cnv_c, w_out_bf, ctx, m_ctx[2],
                                               row(norm2_w[l]), m_ctx[4], m_ctx[3],
                                               router=router_w[i])
                ctx = _moe(h2c, route_c, c_mid, m_ctx[5], wg, wu, wd)
        x = x_new
    return x
```

```python
import functools

import jax
import jax.numpy as jnp
from jax import lax
from jax.experimental import pallas as pl
from jax.experimental.pallas import tpu as pltpu
from jax.experimental.pallas import tpu_sc as plsc

F32 = jnp.float32
BF16 = jnp.bfloat16

GRID_W = 64
HEAD_DIM = 64
ATT_Q_HEADS = 8
ATT_KV_HEADS = 2
ATT_WINDOW = 128
ATT_BLOCK = 128
RET_HEADS = 4
RET_CHUNK = 128
RET_K_SCALE = HEAD_DIM ** -0.5
ATT_SCALE = HEAD_DIM ** -0.5
CONV_CH = 256
CONV_WIDTH = 31
ROPE_BASE = 10000.0
ROPE_FREQS = HEAD_DIM // 4
D_ATT = ATT_Q_HEADS * HEAD_DIM
D_RET = RET_HEADS * HEAD_DIM
ATT_KV_W = ATT_KV_HEADS * HEAD_DIM
C_ATT_K = 0
C_ATT_V = C_ATT_K + ATT_KV_W
C_RET_K = C_ATT_V + ATT_KV_W
C_RET_V = C_RET_K + D_RET
C_ATT_Q = C_RET_V + D_RET
C_RET_Q = C_ATT_Q + D_ATT
C_RET_G = C_RET_Q + D_RET
C_CONV = C_RET_G + D_RET
N_EXPERTS = 8
EPS = 1e-6
NEG_INF = -1e30

LANES = 128
SUBLANES = 8
VMEM_LIMIT = 48 * 1024 * 1024
CONV_HALO = 16
CONV_SUB = 16
CONV_CHAINS = 4
ATT_QBLOCKS = 4
RET_UNROLL = 4
MOE_TM = 512
MOE_CHUNKS = 2
MOE_SUB = 512
FFN_SUB = 512
MOE_TF = 1792
SC_LANES = 16
SC_WINDOW = 64


def _params(*sem):
    return pltpu.CompilerParams(dimension_semantics=sem, vmem_limit_bytes=VMEM_LIMIT)


def _sigmoid(x):
    return 1.0 / (1.0 + jnp.exp(-x))


def _pack_bf16_pairs(v):
    c = v.shape[1] // 2
    bits = pltpu.bitcast(v.astype(BF16).astype(F32), jnp.uint32)
    packed = (bits[:, c:] & jnp.uint32(0xFFFF0000)) | (bits[:, :c] >> 16)
    return pltpu.bitcast(packed, jnp.int32)


def _unpack_bf16_pairs(p):
    bits = pltpu.bitcast(p, jnp.uint32)
    return pltpu.bitcast(bits << 16, F32), pltpu.bitcast(bits & jnp.uint32(0xFFFF0000), F32)


def _pick_tile(n, pref):
    t = min(n, pref)
    assert n % t == 0, (n, t)
    return t


def _adaln_kernel(c_ref, w_ref, b_ref, o_ref):
    c = c_ref[...]
    s = c * _sigmoid(c)
    o_ref[...] = jnp.dot(s, w_ref[...], preferred_element_type=F32,
                         precision=lax.Precision.HIGHEST) + b_ref[...]


def _adaln(cond, ada_w, ada_b):
    depth, d, n = ada_w.shape
    tn = _pick_tile(n, 1536)
    return pl.pallas_call(
        _adaln_kernel,
        out_shape=jax.ShapeDtypeStruct((depth, cond.shape[0], n), F32),
        grid=(depth, n // tn),
        in_specs=[pl.BlockSpec(cond.shape, lambda l, j: (0, 0)),
                  pl.BlockSpec((None, d, tn), lambda l, j: (l, 0, j)),
                  pl.BlockSpec((None, 1, tn), lambda l, j: (l, 0, j))],
        out_specs=pl.BlockSpec((None, cond.shape[0], tn), lambda l, j: (l, 0, j)),
        compiler_params=_params("parallel", "parallel"),
        name="adaln",
    )(cond, ada_w, ada_b.reshape(depth, 1, n))


def _modulated_rms(x, nw, sc, sh):
    ms = jnp.mean(x * x, axis=-1, keepdims=True)
    return (x * lax.rsqrt(ms + EPS) * nw) * (1.0 + sc) + sh


def _head_group_matrix():
    r = lax.broadcasted_iota(jnp.int32, (2 * LANES, 2 * LANES), 0) // HEAD_DIM
    c = lax.broadcasted_iota(jnp.int32, (2 * LANES, 2 * LANES), 1) // HEAD_DIM
    return jnp.where(r == c, 1.0, 0.0).astype(BF16)


def _per_head_sum(v, grp):
    hi = v.astype(BF16)
    lo = (v - hi.astype(F32)).astype(BF16)
    r = jnp.dot(jnp.concatenate([hi, lo], axis=1), grp, preferred_element_type=F32)
    return r[:, 0:LANES] + r[:, LANES:]


def _dup_halves(t):
    sw = pltpu.roll(t, HEAD_DIM, 1)
    lo = lax.broadcasted_iota(jnp.int32, t.shape, 1) < HEAD_DIM
    return jnp.where(lo, t, sw), jnp.where(lo, sw, t)


def _inproj_kernel(x_ref, nw_ref, sc_ref, sh_ref, w_ref, cos_ref, sa_ref, sb_ref, qw_ref, kw_ref,
                   q_ref, k_ref, v_ref, rk_ref, rv_ref, rq_ref, rg_ref, cv_ref):
    hb = _modulated_rms(x_ref[...], nw_ref[...], sc_ref[...], sh_ref[...]).astype(BF16)

    def proj(c0, n):
        return jnp.dot(hb, w_ref[:, c0:c0 + n], preferred_element_type=F32)

    grp = _head_group_matrix()
    cos, sa, sb = cos_ref[...], sa_ref[...], sb_ref[...]

    def norm_rope(p, wrow):
        y = p * lax.rsqrt(_per_head_sum(p * p, grp) * (1.0 / HEAD_DIM) + EPS) * wrow
        return (y * cos + pltpu.roll(y, LANES - ROPE_FREQS, 1) * sa
                + pltpu.roll(y, ROPE_FREQS, 1) * sb)

    kv = proj(C_ATT_K, 2 * ATT_KV_W)
    k0, k1 = _dup_halves(norm_rope(kv[:, 0:ATT_KV_W], kw_ref[...]))
    k_ref[:, 0:LANES] = k0.astype(BF16)
    k_ref[:, LANES:2 * LANES] = k1.astype(BF16)
    v_ref[...] = kv[:, ATT_KV_W:].T.astype(BF16)
    qw = qw_ref[...] * ATT_SCALE
    qall = proj(C_ATT_Q, D_ATT)
    for j in range(D_ATT // LANES):
        q_ref[:, j * LANES:(j + 1) * LANES] = norm_rope(qall[:, j * LANES:(j + 1) * LANES], qw).astype(BF16)
    rk_ref[...] = (proj(C_RET_K, D_RET) * RET_K_SCALE).astype(BF16)
    rv_ref[...] = proj(C_RET_V, D_RET).astype(BF16)
    rq_ref[...] = proj(C_RET_Q, D_RET).astype(BF16)
    rg_ref[...] = proj(C_RET_G, D_RET)
    cv_ref[...] = proj(C_CONV, CONV_CH) * _sigmoid(proj(C_CONV + CONV_CH, CONV_CH))


def _inproj(x, nw, sc, sh, w_bf, cos, sa, sb, qw, kw):
    b, s, d = x.shape
    tm = _pick_tile(s, 512)
    row = lambda n: pl.BlockSpec((1, n), lambda bi, i: (0, 0))
    per_b = pl.BlockSpec((None, 1, d), lambda bi, i: (bi, 0, 0))
    tab = pl.BlockSpec((tm, LANES), lambda bi, i: (i, 0))
    tok = lambda n: pl.BlockSpec((None, tm, n), lambda bi, i: (bi, i, 0))
    shp = lambda n, dt: jax.ShapeDtypeStruct((b, s, n), dt)
    return pl.pallas_call(
        _inproj_kernel,
        out_shape=(shp(D_ATT, BF16), shp(2 * LANES, BF16), jax.ShapeDtypeStruct((b, ATT_KV_W, s), BF16),
                   shp(D_RET, BF16), shp(D_RET, BF16), shp(D_RET, BF16), shp(D_RET, F32),
                   shp(CONV_CH, F32)),
        grid=(b, s // tm),
        in_specs=[tok(d), row(d), per_b, per_b,
                  pl.BlockSpec(w_bf.shape, lambda bi, i: (0, 0)),
                  tab, tab, tab, row(LANES), row(LANES)],
        out_specs=(tok(D_ATT), tok(2 * LANES), pl.BlockSpec((None, ATT_KV_W, tm), lambda bi, i: (bi, 0, i)),
                   tok(D_RET), tok(D_RET), tok(D_RET), tok(D_RET), tok(CONV_CH)),
        compiler_params=_params("parallel", "parallel"),
        name="inproj",
    )(x, nw, sc, sh, w_bf, cos, sa, sb, qw, kw)


def _attn_kernel(*refs, window, nq, n_cast):
    blk = ATT_BLOCK
    n_in = len(refs) - 1 - n_cast
    for src, dst in zip(refs[n_in - n_cast:n_in], refs[n_in + 1:]):
        dst[...] = src[...].astype(BF16)
    refs = refs[:n_in - n_cast] + refs[n_in:n_in + 1]
    if window:
        q_ref = refs[0]
        k_refs = refs[1:nq + 3]
        v_refs = refs[nq + 3:2 * nq + 5]
        kx_ref, vx_ref, sink_ref, o_ref = refs[2 * nq + 5:]
    else:
        q_ref, kx_ref, vx_ref, sink_ref, o_ref = refs
    n_ctx = kx_ref.shape[0]
    nk = 3 * blk + n_ctx if window else n_ctx
    if window:
        i = pl.program_id(1)
        last = nq * pl.num_programs(1) - 1
        key = lax.broadcasted_iota(jnp.int32, (blk, 2 * blk), 0)
        qry = lax.broadcasted_iota(jnp.int32, (blk, 2 * blk), 1) & (blk - 1)
        mask_prev = [(key >= qry) & (nq * i + sub > 0) for sub in range(nq)]
        mask_next = [(key <= qry) & (nq * i + sub < last) for sub in range(nq)]
    first_head = lax.broadcasted_iota(jnp.int32, (blk, LANES), 1) < HEAD_DIM
    ones = jnp.ones((2 * SUBLANES, nk), BF16)

    def scores(sub, pair):
        g = pair // 2
        gs = slice(g * LANES, (g + 1) * LANES)
        if window:
            kcat = jnp.concatenate([r[:, gs] for r in k_refs[sub:sub + 3]] + [kx_ref[:, gs]], axis=0)
        else:
            kcat = kx_ref[:, gs]
        qp = q_ref[sub * blk:(sub + 1) * blk, pair * LANES:(pair + 1) * LANES]
        zero = jnp.zeros_like(qp)
        w = jnp.concatenate([jnp.where(first_head, qp, zero), jnp.where(first_head, zero, qp)], axis=0)
        return lax.dot_general(kcat, w, (((1,), (1,)), ((), ())), preferred_element_type=F32)

    def softmax(sub, pair, s):
        if window:
            parts = [jnp.where(mask_prev[sub], s[0:blk], NEG_INF), s[blk:2 * blk],
                     jnp.where(mask_next[sub], s[2 * blk:3 * blk], NEG_INF), s[3 * blk:]]
        else:
            parts = [s]
        snk = sink_ref[pair:pair + 1, :]
        m = snk
        for part in parts:
            m = jnp.maximum(m, jnp.max(part, axis=0, keepdims=True))
        p = jnp.concatenate([jnp.exp(part - m).astype(BF16) for part in parts], axis=0)
        return p, jnp.exp(snk - m)

    def output(sub, pair, p, sink_p):
        g = pair // 2
        vs = slice(g * HEAD_DIM, (g + 1) * HEAD_DIM)
        if window:
            vt = jnp.concatenate([r[vs, :] for r in v_refs[sub:sub + 3]] + [vx_ref[vs, :]], axis=1)
        else:
            vt = vx_ref[vs, :]
        vaug = jnp.concatenate([vt, ones], axis=0)
        o = jnp.dot(vaug, p, preferred_element_type=F32)
        on = o[0:HEAD_DIM, :] * (1.0 / (o[HEAD_DIM:HEAD_DIM + 1, :] + sink_p))
        ot = jnp.concatenate([on[:, 0:blk], on[:, blk:2 * blk]], axis=0)
        o_ref[sub * blk:(sub + 1) * blk, pair * LANES:(pair + 1) * LANES] = ot.T.astype(BF16)

    units = [(sub, pair) for sub in range(nq) for pair in range(ATT_Q_HEADS // 2)]
    s_next = scores(*units[0])
    for n, unit in enumerate(units):
        s_cur = s_next
        if n + 1 < len(units):
            s_next = scores(*units[n + 1])
        output(*unit, *softmax(*unit, s_cur))


def _attention(q, k, vt, kx, vxt, sink_tab, window, cast=()):
    b, s, _ = q.shape
    blk = ATT_BLOCK
    nb = s // blk
    nq = ATT_QBLOCKS if nb % ATT_QBLOCKS == 0 else 1
    n_ctx = kx.shape[1]
    qspec = pl.BlockSpec((None, nq * blk, D_ATT), lambda bi, i: (bi, i, 0))
    kctx = pl.BlockSpec((None, n_ctx, 2 * LANES), lambda bi, i: (bi, 0, 0))
    vctx = pl.BlockSpec((None, ATT_KV_W, n_ctx), lambda bi, i: (bi, 0, 0))
    snk = pl.BlockSpec(sink_tab.shape, lambda bi, i: (0, 0))
    if window:
        at = lambda off: (lambda i: jnp.clip(nq * i + off, 0, nb - 1))
        kspec = lambda f: pl.BlockSpec((None, blk, 2 * LANES), lambda bi, i: (bi, f(i), 0))
        vspec = lambda f: pl.BlockSpec((None, ATT_KV_W, blk), lambda bi, i: (bi, 0, f(i)))
        offs = range(-1, nq + 1)
        in_specs = ([qspec] + [kspec(at(o)) for o in offs] + [vspec(at(o)) for o in offs]
                    + [kctx, vctx, snk])
        args = (q,) + (k,) * (nq + 2) + (vt,) * (nq + 2) + (kx, vxt, sink_tab)
    else:
        in_specs = [qspec, kctx, vctx, snk]
        args = (q, kx, vxt, sink_tab)
    steps = nb // nq
    out_shape, out_specs = [jax.ShapeDtypeStruct((b, s, D_ATT), BF16)], [qspec]
    for w in cast:
        rows = w.shape[0] // (b * steps)
        assert rows * b * steps == w.shape[0] and rows % (2 * SUBLANES) == 0, w.shape
        slab = pl.BlockSpec((rows, w.shape[1]), lambda bi, i: (bi * steps + i, 0))
        in_specs.append(slab)
        out_specs.append(slab)
        out_shape.append(jax.ShapeDtypeStruct(w.shape, BF16))
    res = pl.pallas_call(
        functools.partial(_attn_kernel, window=window, nq=nq, n_cast=len(cast)),
        out_shape=tuple(out_shape),
        grid=(b, steps),
        in_specs=in_specs,
        out_specs=tuple(out_specs),
        compiler_params=_params("parallel", "parallel"),
        name="attention_window" if window else "attention_ctx",
    )(*args, *cast)
    return res[0], tuple(res[1:])


def _ret_state_kernel(lgl_ref, k_ref, v_ref, s0_ref, r_ref, fin_ref, s_scr, *, cb, reverse):
    c = pl.program_id(1)
    L = RET_CHUNK

    @pl.when(c == 0)
    def _():
        s_scr[...] = s0_ref[...]

    jj = lax.broadcasted_iota(jnp.int32, (L, LANES), 0).astype(F32)
    expo = jj if reverse else (L - 1.0) - jj
    same_head = (lax.broadcasted_iota(jnp.int32, (LANES, LANES), 0) // HEAD_DIM
                 == lax.broadcasted_iota(jnp.int32, (LANES, LANES), 1) // HEAD_DIM)
    for pr in range(RET_HEADS // 2):
        cs = slice(pr * LANES, (pr + 1) * LANES)
        lgl = lgl_ref[pr]
        kdec = jnp.exp(expo * lgl)
        cdec = jnp.exp(float(L) * lgl)
        state = s_scr[pr]
        for t in range(cb):
            cc = cb - 1 - t if reverse else t
            rows = slice(cc * L, (cc + 1) * L)
            r_ref[cc, pr] = state
            kd = k_ref[rows, cs].astype(F32) * kdec
            u = jnp.dot(kd.T.astype(BF16), v_ref[rows, cs], preferred_element_type=F32)
            state = cdec * state + jnp.where(same_head, u, 0.0)
        s_scr[pr] = state

    @pl.when(c == pl.num_programs(1) - 1)
    def _():
        fin_ref[...] = s_scr[...]


def _ret_states(lgl, rk, rv, s0, reverse):
    b, s, _ = rk.shape
    nc = s // RET_CHUNK
    cb = _pick_tile(nc, 8)
    nblk = nc // cb
    npair = RET_HEADS // 2
    blk_idx = (lambda c: nblk - 1 - c) if reverse else (lambda c: c)
    tok = pl.BlockSpec((None, cb * RET_CHUNK, D_RET), lambda bi, c: (bi, blk_idx(c), 0))
    st = pl.BlockSpec((None, npair, LANES, LANES), lambda bi, c: (bi, 0, 0, 0))
    return pl.pallas_call(
        functools.partial(_ret_state_kernel, cb=cb, reverse=reverse),
        out_shape=(jax.ShapeDtypeStruct((b, nc, npair, LANES, LANES), F32),
                   jax.ShapeDtypeStruct((b, npair, LANES, LANES), F32)),
        grid=(b, nblk),
        in_specs=[pl.BlockSpec(lgl.shape, lambda bi, c: (0, 0, 0)), tok, tok, st],
        out_specs=(pl.BlockSpec((None, cb, npair, LANES, LANES),
                                lambda bi, c: (bi, blk_idx(c), 0, 0, 0)), st),
        scratch_shapes=[pltpu.VMEM((npair, LANES, LANES), F32)],
        compiler_params=_params("parallel", "arbitrary"),
        name="ret_state_bwd" if reverse else "ret_state_fwd",
    )(lgl, rk, rv, s0)


def _ret_out_kernel(lgf_ref, lgb_ref, q_ref, k_ref, v_ref, g_ref, rf_ref, rb_ref, gnw_ref, o_ref, *, cb):
    L = RET_CHUNK
    i0 = lax.broadcasted_iota(jnp.int32, (L, LANES), 0).astype(F32)
    i1 = lax.broadcasted_iota(jnp.int32, (L, LANES), 1).astype(F32)
    diff = i0 - i1
    lo = lax.broadcasted_iota(jnp.int32, (L, LANES), 1) < HEAD_DIM
    grp = _head_group_matrix()
    inv = 1.0 / HEAD_DIM
    tabs = []
    for pr in range(RET_HEADS // 2):
        lgf, lgb = lgf_ref[pr], lgb_ref[pr]
        dmat = [jnp.where(diff >= 0.0,
                          jnp.exp(jnp.maximum(diff, 0.0) * lgf[:, a:a + 1]),
                          jnp.exp(jnp.maximum(-diff, 0.0) * lgb[:, a:a + 1]))
                for a in (0, HEAD_DIM)]
        tabs.append((jnp.exp((i0 + 1.0) * lgf), jnp.exp((float(L) - i0) * lgb),
                     jnp.concatenate(dmat, axis=0), gnw_ref[:, pr * LANES:(pr + 1) * LANES]))

    def decayed_scores(t, pr):
        rows = pl.ds(pl.multiple_of(t * L, L), L)
        cs = slice(pr * LANES, (pr + 1) * LANES)
        qp, kp = q_ref[rows, cs], k_ref[rows, cs]
        zero = jnp.zeros_like(kp)
        qz = jnp.concatenate([jnp.where(lo, qp, zero), jnp.where(lo, zero, qp)], axis=0)
        return lax.dot_general(qz, kp, (((1,), (1,)), ((), ())), preferred_element_type=F32) * tabs[pr][2]

    def readout(t, pr, a):
        rows = pl.ds(pl.multiple_of(t * L, L), L)
        cs = slice(pr * LANES, (pr + 1) * LANES)
        xif, xib, _, gnw = tabs[pr]
        vp = v_ref[rows, cs]
        zero = jnp.zeros_like(vp)
        qf = q_ref[rows, cs].astype(F32)
        lhs = jnp.concatenate([a[0:L].astype(BF16), a[L:].astype(BF16),
                               (qf * xif).astype(BF16), (qf * xib).astype(BF16)], axis=1)
        rhs = jnp.concatenate([jnp.where(lo, vp, zero), jnp.where(lo, zero, vp),
                               rf_ref[t, pr].astype(BF16), rb_ref[t, pr].astype(BF16)], axis=0)
        acc = jnp.dot(lhs, rhs, preferred_element_type=F32)
        dl = acc - _per_head_sum(acc, grp) * inv
        var = _per_head_sum(dl * dl, grp) * inv
        y = dl * lax.rsqrt(var + EPS) * gnw
        gt = g_ref[rows, cs]
        o_ref[rows, cs] = (gt * _sigmoid(gt) * y).astype(BF16)

    unroll = min(RET_UNROLL, cb)

    def group(gi, carry):
        units = [(gi * unroll + u, pr) for u in range(unroll) for pr in range(RET_HEADS // 2)]
        a_next = decayed_scores(*units[0])
        for n, unit in enumerate(units):
            a_cur = a_next
            if n + 1 < len(units):
                a_next = decayed_scores(*units[n + 1])
            readout(*unit, a_cur)
        return carry

    lax.fori_loop(0, cb // unroll, group, 0)


def _ret_out(lgf, lgb, rq, rk, rv, rg, rf, rb, gnw):
    b, s, _ = rq.shape
    nc = s // RET_CHUNK
    cb = _pick_tile(nc, 8)
    npair = RET_HEADS // 2
    tok = pl.BlockSpec((None, cb * RET_CHUNK, D_RET), lambda bi, c: (bi, c, 0))
    st = pl.BlockSpec((None, cb, npair, LANES, LANES), lambda bi, c: (bi, c, 0, 0, 0))
    lg = pl.BlockSpec(lgf.shape, lambda bi, c: (0, 0, 0))
    return pl.pallas_call(
        functools.partial(_ret_out_kernel, cb=cb),
        out_shape=jax.ShapeDtypeStruct((b, s, D_RET), BF16),
        grid=(b, nc // cb),
        in_specs=[lg, lg, tok, tok, tok, tok, st, st, pl.BlockSpec((1, D_RET), lambda bi, c: (0, 0))],
        out_specs=tok,
        compiler_params=_params("parallel", "parallel"),
        name="ret_out",
    )(lgf, lgb, rq, rk, rv, rg, rf, rb, gnw)


def _conv_kernel(prev_ref, cur_ref, next_ref, w_ref, b_ref, lnw_ref, lnb_ref, o_ref, xpad, shifted, hbuf):
    i = pl.program_id(1)
    tm = cur_ref.shape[0]
    halo = CONV_HALO
    first = i == 0
    last = i == pl.num_programs(1) - 1
    xpad[0:halo, :] = jnp.where(first, 0.0, prev_ref[...])
    xpad[halo:halo + tm, :] = cur_ref[...]
    xpad[halo + tm:2 * halo + tm, :] = jnp.where(last, 0.0, next_ref[...])
    span = tm + 2 * halo - SUBLANES
    for r in range(SUBLANES):
        shifted[r] = xpad[r:r + span, :]
    base = halo - CONV_WIDTH // 2
    bias = b_ref[...]

    def sub(sb, carry):
        r0 = pl.multiple_of(sb * CONV_SUB, CONV_SUB)
        chains = [None] * CONV_CHAINS
        for w in range(CONV_WIDTH):
            off = base + w
            xs = shifted[off % SUBLANES, pl.ds(r0 + (off // SUBLANES) * SUBLANES, CONV_SUB), :]
            term = xs * w_ref[w:w + 1, :]
            c = w % CONV_CHAINS
            chains[c] = term if chains[c] is None else chains[c] + term
        hbuf[pl.ds(r0, CONV_SUB), :] = (chains[0] + chains[1]) + (chains[2] + chains[3]) + bias
        return carry

    lax.fori_loop(0, tm // CONV_SUB, sub, 0, unroll=2)
    h = hbuf[...]
    mu = jnp.mean(h, axis=-1, keepdims=True)
    dl = h - mu
    var = jnp.mean(dl * dl, axis=-1, keepdims=True)
    y = dl * lax.rsqrt(var + EPS) * lnw_ref[...] + lnb_ref[...]
    o_ref[...] = (y * _sigmoid(y)).astype(BF16)


def _conv(cv, w, bias, lnw, lnb):
    b, s, ch = cv.shape
    tm = _pick_tile(s, 512)
    hpb = tm // CONV_HALO
    nh = s // CONV_HALO
    row = pl.BlockSpec((1, ch), lambda bi, i: (0, 0))
    span = tm + 2 * CONV_HALO - SUBLANES
    return pl.pallas_call(
        _conv_kernel,
        out_shape=jax.ShapeDtypeStruct((b, s, ch), BF16),
        grid=(b, s // tm),
        in_specs=[pl.BlockSpec((None, CONV_HALO, ch), lambda bi, i: (bi, jnp.maximum(i * hpb - 1, 0), 0)),
                  pl.BlockSpec((None, tm, ch), lambda bi, i: (bi, i, 0)),
                  pl.BlockSpec((None, CONV_HALO, ch),
                               lambda bi, i: (bi, jnp.minimum((i + 1) * hpb, nh - 1), 0)),
                  pl.BlockSpec(w.shape, lambda bi, i: (0, 0)), row, row, row],
        out_specs=pl.BlockSpec((None, tm, ch), lambda bi, i: (bi, i, 0)),
        scratch_shapes=[pltpu.VMEM((tm + 2 * CONV_HALO, ch), F32),
                        pltpu.VMEM((SUBLANES, span, ch), F32),
                        pltpu.VMEM((tm, ch), F32)],
        compiler_params=_params("parallel", "parallel"),
        name="conv",
    )(cv, cv, cv, w, bias, lnw, lnb)


def _outproj_kernel(*refs, route):
    if route:
        (att_ref, ret_ref, cnv_ref, w_ref, x_ref, g1_ref, nw_ref, sc_ref, sh_ref, rcat_ref,
         xo_ref, h_ref, rt_ref) = refs
    else:
        att_ref, ret_ref, cnv_ref, w_ref, x_ref, g1_ref, nw_ref, sc_ref, sh_ref, xo_ref, h_ref = refs
    y = (jnp.dot(att_ref[...], w_ref[0:D_ATT, :], preferred_element_type=F32)
         + jnp.dot(ret_ref[...], w_ref[D_ATT:D_ATT + D_RET, :], preferred_element_type=F32)
         + jnp.dot(cnv_ref[...], w_ref[D_ATT + D_RET:, :], preferred_element_type=F32))
    xn = x_ref[...] + g1_ref[...] * y
    xo_ref[...] = xn
    h = _modulated_rms(xn, nw_ref[...], sc_ref[...], sh_ref[...])
    h_ref[...] = _pack_bf16_pairs(h) if route else h.astype(BF16)
    if route:
        hi = h.astype(BF16)
        lo = (h - hi.astype(F32)).astype(BF16)
        tm = h.shape[0]
        r = jnp.dot(jnp.concatenate([hi, lo], axis=0), rcat_ref[...], preferred_element_type=F32)
        logits = (r[0:tm, 0:LANES] + r[0:tm, LANES:]) + (r[tm:, 0:LANES] + r[tm:, LANES:])
        lane = lax.broadcasted_iota(jnp.int32, logits.shape, 1).astype(F32)
        logits = jnp.where(lane < N_EXPERTS, logits, NEG_INF)
        m1 = jnp.max(logits, axis=-1, keepdims=True)
        i1 = jnp.min(jnp.where(logits == m1, lane, float(LANES)), axis=-1, keepdims=True)
        rest = jnp.where(lane == i1, NEG_INF, logits)
        m2 = jnp.max(rest, axis=-1, keepdims=True)
        i2 = jnp.min(jnp.where(rest == m2, lane, float(LANES)), axis=-1, keepdims=True)
        e2 = jnp.exp(m2 - m1)
        w1 = 1.0 / (1.0 + e2)
        w2 = e2 / (1.0 + e2)
        rt_ref[...] = jnp.where(lane == 0.0, i1,
                                jnp.where(lane == 1.0, i2,
                                          jnp.where(lane == 2.0, w1, jnp.where(lane == 3.0, w2, 0.0))))


def _outproj(att, ret, cnv, w_bf, x, g1, nw, sc, sh, router=None):
    b, s, d = x.shape
    tm = _pick_tile(s, 512)
    tok = lambda n: pl.BlockSpec((None, tm, n), lambda bi, i: (bi, i, 0))
    per_b = pl.BlockSpec((None, 1, d), lambda bi, i: (bi, 0, 0))
    row = pl.BlockSpec((1, d), lambda bi, i: (0, 0))
    in_specs = [tok(D_ATT), tok(D_RET), tok(CONV_CH), pl.BlockSpec(w_bf.shape, lambda bi, i: (0, 0)),
                tok(d), per_b, row, per_b, per_b]
    args = [att, ret, cnv, w_bf, x, g1, nw, sc, sh]
    h_shape = (b, s, d // 2) if router is not None else (b, s, d)
    out_shape = [jax.ShapeDtypeStruct((b, s, d), F32),
                 jax.ShapeDtypeStruct(h_shape, jnp.int32 if router is not None else BF16)]
    out_specs = [tok(d), tok(h_shape[2])]
    if router is not None:
        rpad = jnp.zeros((d, LANES), F32).at[:, :N_EXPERTS].set(router)
        rhi = rpad.astype(BF16)
        rlo = (rpad - rhi.astype(F32)).astype(BF16)
        in_specs += [pl.BlockSpec((d, 2 * LANES), lambda bi, i: (0, 0))]
        args += [jnp.concatenate([rhi, rlo], axis=1)]
        out_shape.append(jax.ShapeDtypeStruct((b, s, LANES), F32))
        out_specs.append(tok(LANES))
    return pl.pallas_call(
        functools.partial(_outproj_kernel, route=router is not None),
        out_shape=tuple(out_shape),
        grid=(b, s // tm),
        in_specs=in_specs,
        out_specs=tuple(out_specs),
        compiler_params=_params("parallel", "parallel"),
        name="outproj_route" if router is not None else "outproj",
    )(*args)


def _swiglu_chunks(h, wg_ref, wu_ref, wd_ref, sub):
    total = None
    for c0 in range(0, wg_ref.shape[1], sub):
        cs = slice(c0, min(c0 + sub, wg_ref.shape[1]))
        gate = jnp.dot(h, wg_ref[:, cs].astype(BF16), preferred_element_type=F32)
        up = jnp.dot(h, wu_ref[:, cs].astype(BF16), preferred_element_type=F32)
        act = (gate * _sigmoid(gate) * up).astype(BF16)
        part = jnp.dot(act, wd_ref[cs, :].astype(BF16), preferred_element_type=F32)
        total = part if total is None else total + part
    return total


def _ffn_kernel(h_ref, wg_ref, wu_ref, wd_ref, x_ref, g2_ref, o_ref):
    y = _swiglu_chunks(h_ref[...], wg_ref, wu_ref, wd_ref, FFN_SUB)
    o_ref[...] = x_ref[...] + g2_ref[...] * y


def _ffn(h, wg, wu, wd, x, g2):
    b, s, d = x.shape
    f = wg.shape[1]
    tm = _pick_tile(s, 512)
    tok = lambda: pl.BlockSpec((None, tm, d), lambda bi, i: (bi, i, 0))
    res = lambda shape: pl.BlockSpec(shape, lambda bi, i: (0, 0), pipeline_mode=pl.Buffered(1))
    return pl.pallas_call(
        _ffn_kernel,
        out_shape=jax.ShapeDtypeStruct((b, s, d), F32),
        grid=(b, s // tm),
        in_specs=[tok(), res((d, f)), res((d, f)), res((f, d)), tok(),
                  pl.BlockSpec((None, 1, d), lambda bi, i: (bi, 0, 0))],
        out_specs=tok(),
        compiler_params=_params("parallel", "parallel"),
        name="ffn_dense",
    )(h, wg, wu, wd, x, g2)


def _gather_rows(idx, src):
    n = idx.shape[0]
    d = src.shape[1]
    mesh = plsc.VectorSubcoreMesh(core_axis_name="core", subcore_axis_name="subcore")

    @pl.kernel(out_type=jax.ShapeDtypeStruct((n, d), src.dtype), mesh=mesh, name="moe_gather")
    def gather(src_hbm, idx_hbm, out_hbm):
        def body(idx_vmem, out_vmem):
            pltpu.sync_copy(src_hbm.at[idx_vmem.at[0, pl.ds(0, SC_WINDOW)]], out_vmem)

        pltpu.emit_pipeline(
            body,
            grid=(n // SC_WINDOW,),
            in_specs=[pl.BlockSpec((1, LANES), lambda i: (i, 0))],
            out_specs=[pl.BlockSpec((SC_WINDOW, d), lambda i: (i, 0))],
            core_axis_name=("core", "subcore"),
            dimension_semantics=(pltpu.PARALLEL,),
        )(idx_hbm, out_hbm)

    idx_rows = jnp.pad(idx.reshape(n // SC_WINDOW, SC_WINDOW), ((0, 0), (0, LANES - SC_WINDOW)))
    return gather(src, idx_rows)


def _invert_rows(dest, n_rows):
    n = dest.shape[0]
    n_tok = n // 2
    mesh = plsc.VectorSubcoreMesh(core_axis_name="core", subcore_axis_name="subcore")

    @pl.kernel(out_type=jax.ShapeDtypeStruct((n_rows,), jnp.int32), mesh=mesh, name="moe_row_src",
               scratch_types=[pltpu.VMEM((n,), jnp.int32), pltpu.VMEM((n_rows,), jnp.int32)],
               compiler_params=pltpu.CompilerParams(needs_layout_passes=False))
    def invert(dest_hbm, out_hbm, dest_vmem, rows_vmem):
        @pl.when((lax.axis_index("core") == 0) & (lax.axis_index("subcore") == 0))
        def _():
            pltpu.sync_copy(dest_hbm, dest_vmem)

            @pl.loop(0, n_rows, step=SC_LANES)
            def _(r):
                rows_vmem[pl.ds(r, SC_LANES)] = lax.rem(lax.iota(jnp.int32, SC_LANES) + r, n_tok)

            @pl.loop(0, n, step=SC_LANES)
            def _(a):
                tok = lax.iota(jnp.int32, SC_LANES) + a
                tok = jnp.where(tok >= n_tok, tok - n_tok, tok)
                plsc.store_scatter(rows_vmem, [dest_vmem[pl.ds(a, SC_LANES)]], tok)

            pltpu.sync_copy(rows_vmem, out_hbm)

    return invert(dest)


def _moe_kernel(te_ref, nu_ref, x_ref, wg_ref, wu_ref, wd_ref, *rest, tile0):
    o_ref, xb, acc = rest[-3:]
    t = pl.program_id(0) + tile0
    j = pl.program_id(1)
    nj = pl.num_programs(1)
    used = t < nu_ref[0]

    @pl.when(used & (j == 0))
    def _():
        lo, hi = _unpack_bf16_pairs(x_ref[...])
        half = lo.shape[1]
        xb[:, 0:half] = lo.astype(BF16)
        xb[:, half:] = hi.astype(BF16)
        acc[...] = jnp.zeros_like(acc)

    @pl.when(used)
    def _():
        acc[...] += _swiglu_chunks(xb[...], wg_ref, wu_ref, wd_ref, MOE_SUB)

    @pl.when(used & (j == nj - 1))
    def _():
        o_ref[...] = _pack_bf16_pairs(acc[...])

    @pl.when(jnp.logical_not(used) & (j == nj - 1))
    def _():
        o_ref[...] = jnp.zeros_like(o_ref)


def _moe_grouped(tile_expert, n_used, xs, wg, wu, wd, y_prev, tile0, n_rows):
    p, dp = xs.shape
    d = 2 * dp
    f = wg.shape[2]
    tm = MOE_TM
    tf = MOE_TF
    nj = f // tf

    def jj(t, j, te, nu):
        return jnp.where(t + tile0 < nu[0], j, nj - 1)

    in_specs = [pl.BlockSpec((tm, dp), lambda t, j, te, nu: (t, 0)),
                pl.BlockSpec((None, d, tf), lambda t, j, te, nu: (te[t + tile0], 0, jj(t, j, te, nu))),
                pl.BlockSpec((None, d, tf), lambda t, j, te, nu: (te[t + tile0], 0, jj(t, j, te, nu))),
                pl.BlockSpec((None, tf, d), lambda t, j, te, nu: (te[t + tile0], jj(t, j, te, nu), 0))]
    args = [tile_expert, n_used, xs, wg, wu, wd]
    aliases = {}
    if y_prev is not None:
        in_specs.append(pl.BlockSpec(memory_space=pl.ANY))
        args.append(y_prev)
        aliases = {len(args) - 1: 0}
    return pl.pallas_call(
        functools.partial(_moe_kernel, tile0=tile0),
        out_shape=jax.ShapeDtypeStruct((n_rows, dp), jnp.int32),
        grid_spec=pltpu.PrefetchScalarGridSpec(
            num_scalar_prefetch=2,
            grid=(p // tm, nj),
            in_specs=in_specs,
            out_specs=pl.BlockSpec((tm, dp), lambda t, j, te, nu: (t + tile0, 0)),
            scratch_shapes=[pltpu.VMEM((tm, d), BF16), pltpu.VMEM((tm, d), F32)]),
        input_output_aliases=aliases,
        compiler_params=_params("arbitrary", "arbitrary"),
        name="moe_grouped",
    )(*args)


def _combine_kernel(y1_ref, y2_ref, x_ref, g2_ref, rt_ref, o_ref):
    rt = rt_ref[...]
    w1, w2 = rt[:, 2:3], rt[:, 3:4]
    half = y1_ref.shape[1]
    for k, (a, c) in enumerate(zip(_unpack_bf16_pairs(y1_ref[...]), _unpack_bf16_pairs(y2_ref[...]))):
        cs = slice(k * half, (k + 1) * half)
        o_ref[:, cs] = x_ref[:, cs] + g2_ref[:, cs] * (w1 * a + w2 * c)


def _moe_combine(yg, x, g2, route):
    b, s, d = x.shape
    tm = _pick_tile(s, 512)
    tok = lambda n: pl.BlockSpec((None, tm, n), lambda bi, i: (bi, i, 0))
    return pl.pallas_call(
        _combine_kernel,
        out_shape=jax.ShapeDtypeStruct((b, s, d), F32),
        grid=(b, s // tm),
        in_specs=[tok(d // 2), pl.BlockSpec((None, tm, d // 2), lambda bi, i: (b + bi, i, 0)), tok(d),
                  pl.BlockSpec((None, 1, d), lambda bi, i: (bi, 0, 0)), tok(LANES)],
        out_specs=tok(d),
        compiler_params=_params("parallel", "parallel"),
        name="moe_combine",
    )(yg, yg, x, g2, route)


def _moe(h, route, x, g2, wg, wu, wd):
    b, s, d = x.shape
    n_tok = b * s
    tm = MOE_TM
    rt = route.reshape(n_tok, LANES)
    flat_e = jnp.concatenate([rt[:, 0], rt[:, 1]]).astype(jnp.int32)
    onehot = (flat_e[:, None] == jnp.arange(N_EXPERTS, dtype=jnp.int32)[None, :]).astype(jnp.int32)
    csum = jnp.cumsum(onehot, axis=0)
    rank = jnp.sum(csum * onehot, axis=1) - 1
    counts = csum[-1]
    tiles_e = (counts + tm - 1) // tm
    tiles_cum = jnp.cumsum(tiles_e)
    row_start = (tiles_cum - tiles_e) * tm
    dest = jnp.sum(onehot * row_start[None, :], axis=1) + rank
    n_tiles = 2 * n_tok // tm + N_EXPERTS
    tile_ids = jnp.arange(n_tiles, dtype=jnp.int32)
    tile_expert = jnp.sum((tile_ids[:, None] >= tiles_cum[None, :]).astype(jnp.int32), axis=1)
    last_e = jnp.max(jnp.where(tiles_e > 0, jnp.arange(N_EXPERTS, dtype=jnp.int32), 0))
    tile_expert = jnp.minimum(tile_expert, last_e).astype(jnp.int32)
    n_used = tiles_cum[-1:].astype(jnp.int32)
    dest = dest.astype(jnp.int32)
    row_src = _invert_rows(dest, n_tiles * tm)
    n_chunks = max(c for c in range(1, MOE_CHUNKS + 1) if n_tiles % c == 0)
    tiles_c = n_tiles // n_chunks
    h_flat = h.reshape(n_tok, d // 2)
    xs = [_gather_rows(row_src[c * tiles_c * tm:(c + 1) * tiles_c * tm], h_flat) for c in range(n_chunks)]
    y = None
    for c in range(n_chunks):
        y = _moe_grouped(tile_expert, n_used, xs[c], wg, wu, wd, y, c * tiles_c, n_tiles * tm)
    yg = _gather_rows(dest, y).reshape(2 * b, s, d // 2)
    return _moe_combine(yg, x, g2, route)


def _rope_tables(n):
    rows = n // GRID_W
    r = jnp.repeat(jnp.arange(rows), GRID_W).astype(F32)
    col = jnp.tile(jnp.arange(GRID_W), rows).astype(F32)
    freqs = ROPE_BASE ** (-jnp.arange(ROPE_FREQS, dtype=F32) / ROPE_FREQS)
    ang = jnp.stack([r[:, None] * freqs, col[:, None] * freqs], axis=1)
    ang = jnp.repeat(ang[:, :, None, :], 2, axis=2).reshape(n, HEAD_DIM)
    ang = jnp.tile(ang, (1, LANES // HEAD_DIM))
    cos, sin = jnp.cos(ang), jnp.sin(ang)
    first_half = (jnp.arange(LANES) % (2 * ROPE_FREQS)) < ROPE_FREQS
    return cos, jnp.where(first_half, -sin, 0.0), jnp.where(first_half, 0.0, sin)


def _lane_rows(lg):
    return jnp.repeat(lg.astype(F32), HEAD_DIM).reshape(RET_HEADS // 2, 1, LANES)


def kernel(x, c, ctx, c_ctx, ada_w, ada_b, norm1_w, norm2_w, w_in, w_out, q_norm_w, k_norm_w,
           attn_sink, ret_decay_f, ret_decay_b, ret_gn_w, conv_w, conv_b, conv_ln_w, conv_ln_b,
           ffn_w_gate, ffn_w_up, ffn_w_down, router_w, moe_w_gate, moe_w_up, moe_w_down):
    b, n, d = x.shape
    n_ctx = ctx.shape[1]
    depth = ada_w.shape[0]
    cond = jnp.zeros((SUBLANES, d), F32).at[0:b].set(c).at[b].set(c_ctx)
    mods = _adaln(cond, ada_w, ada_b).reshape(depth, SUBLANES, 6, d)
    cos, sa, sb = _rope_tables(n)
    ones_c = jnp.ones((n_ctx, LANES), F32)
    zeros_c = jnp.zeros((n_ctx, LANES), F32)
    zero_state = jnp.zeros((b, RET_HEADS // 2, LANES, LANES), F32)
    row = lambda v: v.reshape(1, -1)
    moe_f32 = [w.reshape(-1, w.shape[-1]) for w in (moe_w_gate, moe_w_up, moe_w_down)]
    cast_jobs = {l: [j for j in range(3) if j * min(depth, 2) // 3 == l] for l in range(depth)}
    moe_bf = [None, None, None]
    for l in range(depth):
        last = l == depth - 1
        m_lat = [mods[l, 0:b, k][:, None, :] for k in range(6)]
        m_ctx = [jnp.broadcast_to(mods[l, b, k][None, None, :], (b, 1, d)) for k in range(6)]
        w_in_bf = w_in[l].astype(BF16)
        w_out_bf = w_out[l].astype(BF16)
        qw = row(jnp.tile(q_norm_w[l], LANES // HEAD_DIM))
        kw = row(jnp.tile(k_norm_w[l], LANES // HEAD_DIM))
        lgf = _lane_rows(jax.nn.log_sigmoid(ret_decay_f[l].astype(F32)))
        lgb = _lane_rows(jax.nn.log_sigmoid(ret_decay_b[l].astype(F32)))
        sink_tab = jnp.repeat(attn_sink[l].astype(F32), ATT_BLOCK).reshape(ATT_Q_HEADS // 2, 2 * ATT_BLOCK)

        q, k, v, rk, rv, rq, rg, cv = _inproj(x, row(norm1_w[l]), m_lat[1], m_lat[0], w_in_bf,
                                               cos, sa, sb, qw, kw)
        qc, kc, vc, rkc, rvc, rqc, rgc, cvc = _inproj(ctx, row(norm1_w[l]), m_ctx[1], m_ctx[0], w_in_bf,
                                                       ones_c, zeros_c, zeros_c, qw, kw)
        rf_c, s_f = _ret_states(lgf, rkc, rvc, zero_state, reverse=False)
        rb_c, s_b = _ret_states(lgb, rkc, rvc, zero_state, reverse=True)
        rf, _ = _ret_states(lgf, rk, rv, s_f, reverse=False)
        rb, _ = _ret_states(lgb, rk, rv, s_b, reverse=True)

        att, cast_out = _attention(q, k, v, kc, vc, sink_tab, window=True,
                                   cast=[moe_f32[j] for j in cast_jobs[l]])
        for j, w_bf in zip(cast_jobs[l], cast_out):
            moe_bf[j] = w_bf.reshape((moe_w_gate, moe_w_up, moe_w_down)[j].shape)
        ret = _ret_out(lgf, lgb, rq, rk, rv, rg, rf, rb, row(ret_gn_w[l]))
        cnv = _conv(cv, conv_w[l], row(conv_b[l]), row(conv_ln_w[l]), row(conv_ln_b[l]))

        if l % 2 == 0:
            i = l // 2
            wg, wu, wd = ffn_w_gate[i].astype(BF16), ffn_w_up[i].astype(BF16), ffn_w_down[i].astype(BF16)
            x_mid, h2 = _outproj(att, ret, cnv, w_out_bf, x, m_lat[2], row(norm2_w[l]), m_lat[4], m_lat[3])
            x_new = _ffn(h2, wg, wu, wd, x_mid, m_lat[5])
        else:
            i = l // 2
            wg, wu, wd = moe_bf[0][i], moe_bf[1][i], moe_bf[2][i]
            x_mid, h2, route = _outproj(att, ret, cnv, w_out_bf, x, m_lat[2], row(norm2_w[l]),
                                        m_lat[4], m_lat[3], router=router_w[i])
            x_new = _moe(h2, route, x_mid, m_lat[5], wg, wu, wd)

        if not last:
            att_c, _ = _attention(qc, None, None, kc, vc, sink_tab, window=False)
            ret_c = _ret_out(lgf, lgb, rqc, rkc, rvc, rgc, rf_c, rb_c, row(ret_gn_w[l]))
            cnv_c = _conv(cvc, conv_w[l], row(conv_b[l]), row(conv_ln_w[l]), row(conv_ln_b[l]))
            if l % 2 == 0:
                c_mid, h2c = _outproj(att_c, ret_c, cnv_c, w_out_bf, ctx, m_ctx[2], row(norm2_w[l]),
                                      m_ctx[4], m_ctx[3])
                ctx = _ffn(h2c, wg, wu, wd, c_mid, m_ctx[5])
            else:
                c_mid, h2c, route_c = _outproj(att_c, ret_c, cnv_c, w_out_bf, ctx, m_ctx[2],
                                               row(norm2_w[l]), m_ctx[4], m_ctx[3],
                                               router=router_w[i])
                ctx = _moe(h2c, route_c, c_mid, m_ctx[5], wg, wu, wd)
        x = x_new
    return x
```

```python
import functools

import jax
import jax.numpy as jnp
from jax import lax
from jax.experimental import pallas as pl
from jax.experimental.pallas import tpu as pltpu
from jax.experimental.pallas import tpu_sc as plsc

F32 = jnp.float32
BF16 = jnp.bfloat16

GRID_W = 64
HEAD_DIM = 64
ATT_Q_HEADS = 8
ATT_KV_HEADS = 2
ATT_WINDOW = 128
ATT_BLOCK = 128
RET_HEADS = 4
RET_CHUNK = 128
RET_K_SCALE = HEAD_DIM ** -0.5
ATT_SCALE = HEAD_DIM ** -0.5
CONV_CH = 256
CONV_WIDTH = 31
ROPE_BASE = 10000.0
ROPE_FREQS = HEAD_DIM // 4
D_ATT = ATT_Q_HEADS * HEAD_DIM
D_RET = RET_HEADS * HEAD_DIM
ATT_KV_W = ATT_KV_HEADS * HEAD_DIM
C_ATT_K = 0
C_ATT_V = C_ATT_K + ATT_KV_W
C_RET_K = C_ATT_V + ATT_KV_W
C_RET_V = C_RET_K + D_RET
C_ATT_Q = C_RET_V + D_RET
C_RET_Q = C_ATT_Q + D_ATT
C_RET_G = C_RET_Q + D_RET
C_CONV = C_RET_G + D_RET
N_EXPERTS = 8
EPS = 1e-6
NEG_INF = -1e30

LANES = 128
SUBLANES = 8
VMEM_LIMIT = 48 * 1024 * 1024
CONV_HALO = 16
CONV_SUB = 16
CONV_CHAINS = 4
ATT_QBLOCKS = 4
RET_UNROLL = 4
MOE_TM = 512
MOE_CHUNKS = 2
MOE_SUB = 512
FFN_SUB = 512
MOE_TF = 1792
SC_LANES = 16
SC_WINDOW = 64


def _params(*sem):
    return pltpu.CompilerParams(dimension_semantics=sem, vmem_limit_bytes=VMEM_LIMIT)


def _sigmoid(x):
    return 1.0 / (1.0 + jnp.exp(-x))


def _pack_bf16_pairs(v):
    c = v.shape[1] // 2
    bits = pltpu.bitcast(v.astype(BF16).astype(F32), jnp.uint32)
    packed = (bits[:, c:] & jnp.uint32(0xFFFF0000)) | (bits[:, :c] >> 16)
    return pltpu.bitcast(packed, jnp.int32)


def _unpack_bf16_pairs(p):
    bits = pltpu.bitcast(p, jnp.uint32)
    return pltpu.bitcast(bits << 16, F32), pltpu.bitcast(bits & jnp.uint32(0xFFFF0000), F32)


def _pick_tile(n, pref):
    t = min(n, pref)
    assert n % t == 0, (n, t)
    return t


def _adaln_kernel(c_ref, w_ref, b_ref, o_ref):
    c = c_ref[...]
    s = c * _sigmoid(c)
    o_ref[...] = jnp.dot(s, w_ref[...], preferred_element_type=F32,
                         precision=lax.Precision.HIGHEST) + b_ref[...]


def _adaln(cond, ada_w, ada_b):
    depth, d, n = ada_w.shape
    tn = _pick_tile(n, 1536)
    return pl.pallas_call(
        _adaln_kernel,
        out_shape=jax.ShapeDtypeStruct((depth, cond.shape[0], n), F32),
        grid=(depth, n // tn),
        in_specs=[pl.BlockSpec(cond.shape, lambda l, j: (0, 0)),
                  pl.BlockSpec((None, d, tn), lambda l, j: (l, 0, j)),
                  pl.BlockSpec((None, 1, tn), lambda l, j: (l, 0, j))],
        out_specs=pl.BlockSpec((None, cond.shape[0], tn), lambda l, j: (l, 0, j)),
        compiler_params=_params("parallel", "parallel"),
        name="adaln",
    )(cond, ada_w, ada_b.reshape(depth, 1, n))


def _modulated_rms(x, nw, sc, sh):
    ms = jnp.mean(x * x, axis=-1, keepdims=True)
    return (x * lax.rsqrt(ms + EPS) * nw) * (1.0 + sc) + sh


def _head_group_matrix():
    r = lax.broadcasted_iota(jnp.int32, (2 * LANES, 2 * LANES), 0) // HEAD_DIM
    c = lax.broadcasted_iota(jnp.int32, (2 * LANES, 2 * LANES), 1) // HEAD_DIM
    return jnp.where(r == c, 1.0, 0.0).astype(BF16)


def _per_head_sum(v, grp):
    hi = v.astype(BF16)
    lo = (v - hi.astype(F32)).astype(BF16)
    r = jnp.dot(jnp.concatenate([hi, lo], axis=1), grp, preferred_element_type=F32)
    return r[:, 0:LANES] + r[:, LANES:]


def _dup_halves(t):
    sw = pltpu.roll(t, HEAD_DIM, 1)
    lo = lax.broadcasted_iota(jnp.int32, t.shape, 1) < HEAD_DIM
    return jnp.where(lo, t, sw), jnp.where(lo, sw, t)


def _inproj_kernel(x_ref, nw_ref, sc_ref, sh_ref, w_ref, cos_ref, sa_ref, sb_ref, qw_ref, kw_ref,
                   q_ref, k_ref, v_ref, rk_ref, rv_ref, rq_ref, rg_ref, cv_ref):
    hb = _modulated_rms(x_ref[...], nw_ref[...], sc_ref[...], sh_ref[...]).astype(BF16)

    def proj(c0, n):
        return jnp.dot(hb, w_ref[:, c0:c0 + n], preferred_element_type=F32)

    grp = _head_group_matrix()
    cos, sa, sb = cos_ref[...], sa_ref[...], sb_ref[...]

    def norm_rope(p, wrow):
        y = p * lax.rsqrt(_per_head_sum(p * p, grp) * (1.0 / HEAD_DIM) + EPS) * wrow
        return (y * cos + pltpu.roll(y, LANES - ROPE_FREQS, 1) * sa
                + pltpu.roll(y, ROPE_FREQS, 1) * sb)

    def put_kv(kv):
        k0, k1 = _dup_halves(norm_rope(kv[:, 0:ATT_KV_W], kw_ref[...]))
        k_ref[:, 0:LANES] = k0.astype(BF16)
        k_ref[:, LANES:2 * LANES] = k1.astype(BF16)
        v_ref[...] = kv[:, ATT_KV_W:].T.astype(BF16)

    def put_q(qall):
        qw = qw_ref[...] * ATT_SCALE
        for j in range(D_ATT // LANES):
            q_ref[:, j * LANES:(j + 1) * LANES] = norm_rope(qall[:, j * LANES:(j + 1) * LANES], qw).astype(BF16)

    def put_rk(p):
        rk_ref[...] = (p * RET_K_SCALE).astype(BF16)

    def put_rv(p):
        rv_ref[...] = p.astype(BF16)

    def put_rq(p):
        rq_ref[...] = p.astype(BF16)

    def put_rg(p):
        rg_ref[...] = p

    def put_conv(p):
        cv_ref[...] = p[:, 0:CONV_CH] * _sigmoid(p[:, CONV_CH:])

    stages = [(C_ATT_K, 2 * ATT_KV_W, put_kv), (C_ATT_Q, D_ATT, put_q), (C_RET_K, D_RET, put_rk),
              (C_RET_V, D_RET, put_rv), (C_RET_Q, D_RET, put_rq), (C_RET_G, D_RET, put_rg),
              (C_CONV, 2 * CONV_CH, put_conv)]
    p_next = proj(*stages[0][:2])
    for n, (_, _, put) in enumerate(stages):
        p_cur = p_next
        if n + 1 < len(stages):
            p_next = proj(*stages[n + 1][:2])
        put(p_cur)


def _inproj(x, nw, sc, sh, w_bf, cos, sa, sb, qw, kw):
    b, s, d = x.shape
    tm = _pick_tile(s, 512)
    row = lambda n: pl.BlockSpec((1, n), lambda bi, i: (0, 0))
    per_b = pl.BlockSpec((None, 1, d), lambda bi, i: (bi, 0, 0))
    tab = pl.BlockSpec((tm, LANES), lambda bi, i: (i, 0))
    tok = lambda n: pl.BlockSpec((None, tm, n), lambda bi, i: (bi, i, 0))
    shp = lambda n, dt: jax.ShapeDtypeStruct((b, s, n), dt)
    return pl.pallas_call(
        _inproj_kernel,
        out_shape=(shp(D_ATT, BF16), shp(2 * LANES, BF16), jax.ShapeDtypeStruct((b, ATT_KV_W, s), BF16),
                   shp(D_RET, BF16), shp(D_RET, BF16), shp(D_RET, BF16), shp(D_RET, F32),
                   shp(CONV_CH, F32)),
        grid=(b, s // tm),
        in_specs=[tok(d), row(d), per_b, per_b,
                  pl.BlockSpec(w_bf.shape, lambda bi, i: (0, 0)),
                  tab, tab, tab, row(LANES), row(LANES)],
        out_specs=(tok(D_ATT), tok(2 * LANES), pl.BlockSpec((None, ATT_KV_W, tm), lambda bi, i: (bi, 0, i)),
                   tok(D_RET), tok(D_RET), tok(D_RET), tok(D_RET), tok(CONV_CH)),
        compiler_params=_params("parallel", "parallel"),
        name="inproj",
    )(x, nw, sc, sh, w_bf, cos, sa, sb, qw, kw)


def _attn_kernel(*refs, window, nq, n_cast):
    blk = ATT_BLOCK
    n_in = len(refs) - 1 - n_cast
    for src, dst in zip(refs[n_in - n_cast:n_in], refs[n_in + 1:]):
        dst[...] = src[...].astype(BF16)
    refs = refs[:n_in - n_cast] + refs[n_in:n_in + 1]
    if window:
        q_ref = refs[0]
        k_refs = refs[1:nq + 3]
        v_refs = refs[nq + 3:2 * nq + 5]
        kx_ref, vx_ref, sink_ref, o_ref = refs[2 * nq + 5:]
    else:
        q_ref, kx_ref, vx_ref, sink_ref, o_ref = refs
    n_ctx = kx_ref.shape[0]
    nk = 3 * blk + n_ctx if window else n_ctx
    if window:
        i = pl.program_id(1)
        last = nq * pl.num_programs(1) - 1
        key = lax.broadcasted_iota(jnp.int32, (blk, 2 * blk), 0)
        qry = lax.broadcasted_iota(jnp.int32, (blk, 2 * blk), 1) & (blk - 1)

        def band_masks(sub):
            off_prev = jnp.where(nq * i + sub > 0, 0, blk)
            off_next = jnp.where(nq * i + sub < last, 0, blk)
            return key >= qry + off_prev, key + off_next <= qry
    first_head = lax.broadcasted_iota(jnp.int32, (blk, LANES), 1) < HEAD_DIM
    ones = jnp.ones((2 * SUBLANES, nk), BF16)

    def scores(sub, pair):
        g = pair // 2
        gs = slice(g * LANES, (g + 1) * LANES)
        if window:
            kcat = jnp.concatenate([r[:, gs] for r in k_refs[sub:sub + 3]] + [kx_ref[:, gs]], axis=0)
        else:
            kcat = kx_ref[:, gs]
        qp = q_ref[sub * blk:(sub + 1) * blk, pair * LANES:(pair + 1) * LANES]
        zero = jnp.zeros_like(qp)
        w = jnp.concatenate([jnp.where(first_head, qp, zero), jnp.where(first_head, zero, qp)], axis=0)
        return lax.dot_general(kcat, w, (((1,), (1,)), ((), ())), preferred_element_type=F32)

    def softmax(sub, pair, s):
        if window:
            mask_prev, mask_next = band_masks(sub)
            parts = [jnp.where(mask_prev, s[0:blk], NEG_INF), s[blk:2 * blk],
                     jnp.where(mask_next, s[2 * blk:3 * blk], NEG_INF), s[3 * blk:]]
        else:
            parts = [s]
        snk = sink_ref[pair:pair + 1, :]
        m = snk
        for part in parts:
            m = jnp.maximum(m, jnp.max(part, axis=0, keepdims=True))
        p = jnp.concatenate([jnp.exp(part - m).astype(BF16) for part in parts], axis=0)
        return p, jnp.exp(snk - m)

    def output(sub, pair, p, sink_p):
        g = pair // 2
        vs = slice(g * HEAD_DIM, (g + 1) * HEAD_DIM)
        if window:
            vt = jnp.concatenate([r[vs, :] for r in v_refs[sub:sub + 3]] + [vx_ref[vs, :]], axis=1)
        else:
            vt = vx_ref[vs, :]
        vaug = jnp.concatenate([vt, ones], axis=0)
        o = jnp.dot(vaug, p, preferred_element_type=F32)
        on = o[0:HEAD_DIM, :] * (1.0 / (o[HEAD_DIM:HEAD_DIM + 1, :] + sink_p))
        ot = jnp.concatenate([on[:, 0:blk], on[:, blk:2 * blk]], axis=0)
        o_ref[sub * blk:(sub + 1) * blk, pair * LANES:(pair + 1) * LANES] = ot.T.astype(BF16)

    units = [(sub, pair) for sub in range(nq) for pair in range(ATT_Q_HEADS // 2)]
    s_next = scores(*units[0])
    for n, unit in enumerate(units):
        s_cur = s_next
        if n + 1 < len(units):
            s_next = scores(*units[n + 1])
        output(*unit, *softmax(*unit, s_cur))


def _attention(q, k, vt, kx, vxt, sink_tab, window, cast=()):
    b, s, _ = q.shape
    blk = ATT_BLOCK
    nb = s // blk
    nq = ATT_QBLOCKS if nb % ATT_QBLOCKS == 0 else 1
    n_ctx = kx.shape[1]
    qspec = pl.BlockSpec((None, nq * blk, D_ATT), lambda bi, i: (bi, i, 0))
    kctx = pl.BlockSpec((None, n_ctx, 2 * LANES), lambda bi, i: (bi, 0, 0))
    vctx = pl.BlockSpec((None, ATT_KV_W, n_ctx), lambda bi, i: (bi, 0, 0))
    snk = pl.BlockSpec(sink_tab.shape, lambda bi, i: (0, 0))
    if window:
        at = lambda off: (lambda i: jnp.clip(nq * i + off, 0, nb - 1))
        kspec = lambda f: pl.BlockSpec((None, blk, 2 * LANES), lambda bi, i: (bi, f(i), 0))
        vspec = lambda f: pl.BlockSpec((None, ATT_KV_W, blk), lambda bi, i: (bi, 0, f(i)))
        offs = range(-1, nq + 1)
        in_specs = ([qspec] + [kspec(at(o)) for o in offs] + [vspec(at(o)) for o in offs]
                    + [kctx, vctx, snk])
        args = (q,) + (k,) * (nq + 2) + (vt,) * (nq + 2) + (kx, vxt, sink_tab)
    else:
        in_specs = [qspec, kctx, vctx, snk]
        args = (q, kx, vxt, sink_tab)
    steps = nb // nq
    out_shape, out_specs = [jax.ShapeDtypeStruct((b, s, D_ATT), BF16)], [qspec]
    for w in cast:
        rows = w.shape[0] // (b * steps)
        assert rows * b * steps == w.shape[0] and rows % (2 * SUBLANES) == 0, w.shape
        slab = pl.BlockSpec((rows, w.shape[1]), lambda bi, i: (bi * steps + i, 0))
        in_specs.append(slab)
        out_specs.append(slab)
        out_shape.append(jax.ShapeDtypeStruct(w.shape, BF16))
    res = pl.pallas_call(
        functools.partial(_attn_kernel, window=window, nq=nq, n_cast=len(cast)),
        out_shape=tuple(out_shape),
        grid=(b, steps),
        in_specs=in_specs,
        out_specs=tuple(out_specs),
        compiler_params=_params("parallel", "parallel"),
        name="attention_window" if window else "attention_ctx",
    )(*args, *cast)
    return res[0], tuple(res[1:])


def _ret_state_kernel(lgl_ref, k_ref, v_ref, s0_ref, r_ref, fin_ref, s_scr, *, cb, reverse):
    c = pl.program_id(1)
    L = RET_CHUNK

    @pl.when(c == 0)
    def _():
        s_scr[...] = s0_ref[...]

    jj = lax.broadcasted_iota(jnp.int32, (L, LANES), 0).astype(F32)
    expo = jj if reverse else (L - 1.0) - jj
    same_head = (lax.broadcasted_iota(jnp.int32, (LANES, LANES), 0) // HEAD_DIM
                 == lax.broadcasted_iota(jnp.int32, (LANES, LANES), 1) // HEAD_DIM)
    for pr in range(RET_HEADS // 2):
        cs = slice(pr * LANES, (pr + 1) * LANES)
        lgl = lgl_ref[pr]
        kdec = jnp.exp(expo * lgl)
        cdec = jnp.exp(float(L) * lgl)
        state = s_scr[pr]
        for t in range(cb):
            cc = cb - 1 - t if reverse else t
            rows = slice(cc * L, (cc + 1) * L)
            r_ref[cc, pr] = state
            kd = k_ref[rows, cs].astype(F32) * kdec
            u = jnp.dot(kd.T.astype(BF16), v_ref[rows, cs], preferred_element_type=F32)
            state = cdec * state + jnp.where(same_head, u, 0.0)
        s_scr[pr] = state

    @pl.when(c == pl.num_programs(1) - 1)
    def _():
        fin_ref[...] = s_scr[...]


def _ret_states(lgl, rk, rv, s0, reverse):
    b, s, _ = rk.shape
    nc = s // RET_CHUNK
    cb = _pick_tile(nc, 8)
    nblk = nc // cb
    npair = RET_HEADS // 2
    blk_idx = (lambda c: nblk - 1 - c) if reverse else (lambda c: c)
    tok = pl.BlockSpec((None, cb * RET_CHUNK, D_RET), lambda bi, c: (bi, blk_idx(c), 0))
    st = pl.BlockSpec((None, npair, LANES, LANES), lambda bi, c: (bi, 0, 0, 0))
    return pl.pallas_call(
        functools.partial(_ret_state_kernel, cb=cb, reverse=reverse),
        out_shape=(jax.ShapeDtypeStruct((b, nc, npair, LANES, LANES), F32),
                   jax.ShapeDtypeStruct((b, npair, LANES, LANES), F32)),
        grid=(b, nblk),
        in_specs=[pl.BlockSpec(lgl.shape, lambda bi, c: (0, 0, 0)), tok, tok, st],
        out_specs=(pl.BlockSpec((None, cb, npair, LANES, LANES),
                                lambda bi, c: (bi, blk_idx(c), 0, 0, 0)), st),
        scratch_shapes=[pltpu.VMEM((npair, LANES, LANES), F32)],
        compiler_params=_params("parallel", "arbitrary"),
        name="ret_state_bwd" if reverse else "ret_state_fwd",
    )(lgl, rk, rv, s0)


def _ret_out_kernel(lgf_ref, lgb_ref, q_ref, k_ref, v_ref, g_ref, rf_ref, rb_ref, gnw_ref, o_ref, *, cb):
    L = RET_CHUNK
    i0 = lax.broadcasted_iota(jnp.int32, (L, LANES), 0).astype(F32)
    i1 = lax.broadcasted_iota(jnp.int32, (L, LANES), 1).astype(F32)
    diff = i0 - i1
    lo = lax.broadcasted_iota(jnp.int32, (L, LANES), 1) < HEAD_DIM
    grp = _head_group_matrix()
    inv = 1.0 / HEAD_DIM
    tabs = []
    for pr in range(RET_HEADS // 2):
        lgf, lgb = lgf_ref[pr], lgb_ref[pr]
        dmat = [jnp.where(diff >= 0.0,
                          jnp.exp(jnp.maximum(diff, 0.0) * lgf[:, a:a + 1]),
                          jnp.exp(jnp.maximum(-diff, 0.0) * lgb[:, a:a + 1]))
                for a in (0, HEAD_DIM)]
        tabs.append((jnp.exp((i0 + 1.0) * lgf), jnp.exp((float(L) - i0) * lgb),
                     jnp.concatenate(dmat, axis=0), gnw_ref[:, pr * LANES:(pr + 1) * LANES]))

    def decayed_scores(t, pr):
        rows = pl.ds(pl.multiple_of(t * L, L), L)
        cs = slice(pr * LANES, (pr + 1) * LANES)
        qp, kp = q_ref[rows, cs], k_ref[rows, cs]
        zero = jnp.zeros_like(kp)
        qz = jnp.concatenate([jnp.where(lo, qp, zero), jnp.where(lo, zero, qp)], axis=0)
        return lax.dot_general(qz, kp, (((1,), (1,)), ((), ())), preferred_element_type=F32) * tabs[pr][2]

    def readout(t, pr, a):
        rows = pl.ds(pl.multiple_of(t * L, L), L)
        cs = slice(pr * LANES, (pr + 1) * LANES)
        xif, xib, _, gnw = tabs[pr]
        vp = v_ref[rows, cs]
        zero = jnp.zeros_like(vp)
        qf = q_ref[rows, cs].astype(F32)
        lhs = jnp.concatenate([a[0:L].astype(BF16), a[L:].astype(BF16),
                               (qf * xif).astype(BF16), (qf * xib).astype(BF16)], axis=1)
        rhs = jnp.concatenate([jnp.where(lo, vp, zero), jnp.where(lo, zero, vp),
                               rf_ref[t, pr].astype(BF16), rb_ref[t, pr].astype(BF16)], axis=0)
        acc = jnp.dot(lhs, rhs, preferred_element_type=F32)
        dl = acc - _per_head_sum(acc, grp) * inv
        var = _per_head_sum(dl * dl, grp) * inv
        y = dl * lax.rsqrt(var + EPS) * gnw
        gt = g_ref[rows, cs]
        o_ref[rows, cs] = (gt * _sigmoid(gt) * y).astype(BF16)

    unroll = min(RET_UNROLL, cb)

    def group(gi, carry):
        units = [(gi * unroll + u, pr) for u in range(unroll) for pr in range(RET_HEADS // 2)]
        a_next = decayed_scores(*units[0])
        for n, unit in enumerate(units):
            a_cur = a_next
            if n + 1 < len(units):
                a_next = decayed_scores(*units[n + 1])
            readout(*unit, a_cur)
        return carry

    lax.fori_loop(0, cb // unroll, group, 0)


def _ret_out(lgf, lgb, rq, rk, rv, rg, rf, rb, gnw):
    b, s, _ = rq.shape
    nc = s // RET_CHUNK
    cb = _pick_tile(nc, 8)
    npair = RET_HEADS // 2
    tok = pl.BlockSpec((None, cb * RET_CHUNK, D_RET), lambda bi, c: (bi, c, 0))
    st = pl.BlockSpec((None, cb, npair, LANES, LANES), lambda bi, c: (bi, c, 0, 0, 0))
    lg = pl.BlockSpec(lgf.shape, lambda bi, c: (0, 0, 0))
    return pl.pallas_call(
        functools.partial(_ret_out_kernel, cb=cb),
        out_shape=jax.ShapeDtypeStruct((b, s, D_RET), BF16),
        grid=(b, nc // cb),
        in_specs=[lg, lg, tok, tok, tok, tok, st, st, pl.BlockSpec((1, D_RET), lambda bi, c: (0, 0))],
        out_specs=tok,
        compiler_params=_params("parallel", "parallel"),
        name="ret_out",
    )(lgf, lgb, rq, rk, rv, rg, rf, rb, gnw)


def _conv_kernel(prev_ref, cur_ref, next_ref, w_ref, b_ref, lnw_ref, lnb_ref, o_ref, xpad, shifted, hbuf):
    i = pl.program_id(1)
    tm = cur_ref.shape[0]
    halo = CONV_HALO
    first = i == 0
    last = i == pl.num_programs(1) - 1
    xpad[0:halo, :] = jnp.where(first, 0.0, prev_ref[...])
    xpad[halo:halo + tm, :] = cur_ref[...]
    xpad[halo + tm:2 * halo + tm, :] = jnp.where(last, 0.0, next_ref[...])
    span = tm + 2 * halo - SUBLANES
    for r in range(SUBLANES):
        shifted[r] = xpad[r:r + span, :]
    base = halo - CONV_WIDTH // 2
    bias = b_ref[...]

    def sub(sb, carry):
        r0 = pl.multiple_of(sb * CONV_SUB, CONV_SUB)
        chains = [None] * CONV_CHAINS
        for w in range(CONV_WIDTH):
            off = base + w
            xs = shifted[off % SUBLANES, pl.ds(r0 + (off // SUBLANES) * SUBLANES, CONV_SUB), :]
            term = xs * w_ref[w:w + 1, :]
            c = w % CONV_CHAINS
            chains[c] = term if chains[c] is None else chains[c] + term
        hbuf[pl.ds(r0, CONV_SUB), :] = (chains[0] + chains[1]) + (chains[2] + chains[3]) + bias
        return carry

    lax.fori_loop(0, tm // CONV_SUB, sub, 0, unroll=2)
    h = hbuf[...]
    mu = jnp.mean(h, axis=-1, keepdims=True)
    dl = h - mu
    var = jnp.mean(dl * dl, axis=-1, keepdims=True)
    y = dl * lax.rsqrt(var + EPS) * lnw_ref[...] + lnb_ref[...]
    o_ref[...] = (y * _sigmoid(y)).astype(BF16)


def _conv(cv, w, bias, lnw, lnb):
    b, s, ch = cv.shape
    tm = _pick_tile(s, 512)
    hpb = tm // CONV_HALO
    nh = s // CONV_HALO
    row = pl.BlockSpec((1, ch), lambda bi, i: (0, 0))
    span = tm + 2 * CONV_HALO - SUBLANES
    return pl.pallas_call(
        _conv_kernel,
        out_shape=jax.ShapeDtypeStruct((b, s, ch), BF16),
        grid=(b, s // tm),
        in_specs=[pl.BlockSpec((None, CONV_HALO, ch), lambda bi, i: (bi, jnp.maximum(i * hpb - 1, 0), 0)),
                  pl.BlockSpec((None, tm, ch), lambda bi, i: (bi, i, 0)),
                  pl.BlockSpec((None, CONV_HALO, ch),
                               lambda bi, i: (bi, jnp.minimum((i + 1) * hpb, nh - 1), 0)),
                  pl.BlockSpec(w.shape, lambda bi, i: (0, 0)), row, row, row],
        out_specs=pl.BlockSpec((None, tm, ch), lambda bi, i: (bi, i, 0)),
        scratch_shapes=[pltpu.VMEM((tm + 2 * CONV_HALO, ch), F32),
                        pltpu.VMEM((SUBLANES, span, ch), F32),
                        pltpu.VMEM((tm, ch), F32)],
        compiler_params=_params("parallel", "parallel"),
        name="conv",
    )(cv, cv, cv, w, bias, lnw, lnb)


def _outproj_kernel(*refs, route):
    if route:
        (att_ref, ret_ref, cnv_ref, w_ref, x_ref, g1_ref, nw_ref, sc_ref, sh_ref, rcat_ref,
         xo_ref, h_ref, rt_ref) = refs
    else:
        att_ref, ret_ref, cnv_ref, w_ref, x_ref, g1_ref, nw_ref, sc_ref, sh_ref, xo_ref, h_ref = refs
    y = (jnp.dot(att_ref[...], w_ref[0:D_ATT, :], preferred_element_type=F32)
         + jnp.dot(ret_ref[...], w_ref[D_ATT:D_ATT + D_RET, :], preferred_element_type=F32)
         + jnp.dot(cnv_ref[...], w_ref[D_ATT + D_RET:, :], preferred_element_type=F32))
    xn = x_ref[...] + g1_ref[...] * y
    xo_ref[...] = xn
    h = _modulated_rms(xn, nw_ref[...], sc_ref[...], sh_ref[...])
    h_ref[...] = _pack_bf16_pairs(h) if route else h.astype(BF16)
    if route:
        hi = h.astype(BF16)
        lo = (h - hi.astype(F32)).astype(BF16)
        tm = h.shape[0]
        r = jnp.dot(jnp.concatenate([hi, lo], axis=0), rcat_ref[...], preferred_element_type=F32)
        logits = (r[0:tm, 0:LANES] + r[0:tm, LANES:]) + (r[tm:, 0:LANES] + r[tm:, LANES:])
        lane = lax.broadcasted_iota(jnp.int32, logits.shape, 1).astype(F32)
        logits = jnp.where(lane < N_EXPERTS, logits, NEG_INF)
        m1 = jnp.max(logits, axis=-1, keepdims=True)
        i1 = jnp.min(jnp.where(logits == m1, lane, float(LANES)), axis=-1, keepdims=True)
        rest = jnp.where(lane == i1, NEG_INF, logits)
        m2 = jnp.max(rest, axis=-1, keepdims=True)
        i2 = jnp.min(jnp.where(rest == m2, lane, float(LANES)), axis=-1, keepdims=True)
        e2 = jnp.exp(m2 - m1)
        w1 = 1.0 / (1.0 + e2)
        w2 = e2 / (1.0 + e2)
        rt_ref[...] = jnp.where(lane == 0.0, i1,
                                jnp.where(lane == 1.0, i2,
                                          jnp.where(lane == 2.0, w1, jnp.where(lane == 3.0, w2, 0.0))))


def _outproj(att, ret, cnv, w_bf, x, g1, nw, sc, sh, router=None):
    b, s, d = x.shape
    tm = _pick_tile(s, 512)
    tok = lambda n: pl.BlockSpec((None, tm, n), lambda bi, i: (bi, i, 0))
    per_b = pl.BlockSpec((None, 1, d), lambda bi, i: (bi, 0, 0))
    row = pl.BlockSpec((1, d), lambda bi, i: (0, 0))
    in_specs = [tok(D_ATT), tok(D_RET), tok(CONV_CH), pl.BlockSpec(w_bf.shape, lambda bi, i: (0, 0)),
                tok(d), per_b, row, per_b, per_b]
    args = [att, ret, cnv, w_bf, x, g1, nw, sc, sh]
    h_shape = (b, s, d // 2) if router is not None else (b, s, d)
    out_shape = [jax.ShapeDtypeStruct((b, s, d), F32),
                 jax.ShapeDtypeStruct(h_shape, jnp.int32 if router is not None else BF16)]
    out_specs = [tok(d), tok(h_shape[2])]
    if router is not None:
        rpad = jnp.zeros((d, LANES), F32).at[:, :N_EXPERTS].set(router)
        rhi = rpad.astype(BF16)
        rlo = (rpad - rhi.astype(F32)).astype(BF16)
        in_specs += [pl.BlockSpec((d, 2 * LANES), lambda bi, i: (0, 0))]
        args += [jnp.concatenate([rhi, rlo], axis=1)]
        out_shape.append(jax.ShapeDtypeStruct((b, s, LANES), F32))
        out_specs.append(tok(LANES))
    return pl.pallas_call(
        functools.partial(_outproj_kernel, route=router is not None),
        out_shape=tuple(out_shape),
        grid=(b, s // tm),
        in_specs=in_specs,
        out_specs=tuple(out_specs),
        compiler_params=_params("parallel", "parallel"),
        name="outproj_route" if router is not None else "outproj",
    )(*args)


def _swiglu_chunks(h, wg_ref, wu_ref, wd_ref, sub):
    total = None
    for c0 in range(0, wg_ref.shape[1], sub):
        cs = slice(c0, min(c0 + sub, wg_ref.shape[1]))
        gate = jnp.dot(h, wg_ref[:, cs].astype(BF16), preferred_element_type=F32)
        up = jnp.dot(h, wu_ref[:, cs].astype(BF16), preferred_element_type=F32)
        act = (gate * _sigmoid(gate) * up).astype(BF16)
        part = jnp.dot(act, wd_ref[cs, :].astype(BF16), preferred_element_type=F32)
        total = part if total is None else total + part
    return total


def _ffn_kernel(h_ref, wg_ref, wu_ref, wd_ref, x_ref, g2_ref, o_ref):
    y = _swiglu_chunks(h_ref[...], wg_ref, wu_ref, wd_ref, FFN_SUB)
    o_ref[...] = x_ref[...] + g2_ref[...] * y


def _ffn(h, wg, wu, wd, x, g2):
    b, s, d = x.shape
    f = wg.shape[1]
    tm = _pick_tile(s, 512)
    tok = lambda: pl.BlockSpec((None, tm, d), lambda bi, i: (bi, i, 0))
    res = lambda shape: pl.BlockSpec(shape, lambda bi, i: (0, 0), pipeline_mode=pl.Buffered(1))
    return pl.pallas_call(
        _ffn_kernel,
        out_shape=jax.ShapeDtypeStruct((b, s, d), F32),
        grid=(b, s // tm),
        in_specs=[tok(), res((d, f)), res((d, f)), res((f, d)), tok(),
                  pl.BlockSpec((None, 1, d), lambda bi, i: (bi, 0, 0))],
        out_specs=tok(),
        compiler_params=_params("parallel", "parallel"),
        name="ffn_dense",
    )(h, wg, wu, wd, x, g2)


def _gather_rows(idx, src):
    n = idx.shape[0]
    d = src.shape[1]
    mesh = plsc.VectorSubcoreMesh(core_axis_name="core", subcore_axis_name="subcore")

    @pl.kernel(out_type=jax.ShapeDtypeStruct((n, d), src.dtype), mesh=mesh, name="moe_gather")
    def gather(src_hbm, idx_hbm, out_hbm):
        def body(idx_vmem, out_vmem):
            pltpu.sync_copy(src_hbm.at[idx_vmem.at[0, pl.ds(0, SC_WINDOW)]], out_vmem)

        pltpu.emit_pipeline(
            body,
            grid=(n // SC_WINDOW,),
            in_specs=[pl.BlockSpec((1, LANES), lambda i: (i, 0))],
            out_specs=[pl.BlockSpec((SC_WINDOW, d), lambda i: (i, 0))],
            core_axis_name=("core", "subcore"),
            dimension_semantics=(pltpu.PARALLEL,),
        )(idx_hbm, out_hbm)

    idx_rows = jnp.pad(idx.reshape(n // SC_WINDOW, SC_WINDOW), ((0, 0), (0, LANES - SC_WINDOW)))
    return gather(src, idx_rows)


def _invert_rows(dest, n_rows):
    n = dest.shape[0]
    n_tok = n // 2
    mesh = plsc.VectorSubcoreMesh(core_axis_name="core", subcore_axis_name="subcore")

    @pl.kernel(out_type=jax.ShapeDtypeStruct((n_rows,), jnp.int32), mesh=mesh, name="moe_row_src",
               scratch_types=[pltpu.VMEM((n,), jnp.int32), pltpu.VMEM((n_rows,), jnp.int32)],
               compiler_params=pltpu.CompilerParams(needs_layout_passes=False))
    def invert(dest_hbm, out_hbm, dest_vmem, rows_vmem):
        @pl.when((lax.axis_index("core") == 0) & (lax.axis_index("subcore") == 0))
        def _():
            pltpu.sync_copy(dest_hbm, dest_vmem)

            @pl.loop(0, n_rows, step=SC_LANES)
            def _(r):
                rows_vmem[pl.ds(r, SC_LANES)] = lax.rem(lax.iota(jnp.int32, SC_LANES) + r, n_tok)

            @pl.loop(0, n, step=SC_LANES)
            def _(a):
                tok = lax.iota(jnp.int32, SC_LANES) + a
                tok = jnp.where(tok >= n_tok, tok - n_tok, tok)
                plsc.store_scatter(rows_vmem, [dest_vmem[pl.ds(a, SC_LANES)]], tok)

            pltpu.sync_copy(rows_vmem, out_hbm)

    return invert(dest)


def _moe_kernel(te_ref, nu_ref, x_ref, wg_ref, wu_ref, wd_ref, *rest, tile0, nj):
    o_ref, xb, acc = rest[-3:]
    t = pl.program_id(0) + tile0
    j = pl.program_id(1)
    used = t < nu_ref[0]

    def partial_sum():
        return _swiglu_chunks(xb[...], wg_ref, wu_ref, wd_ref, MOE_SUB)

    @pl.when(used & (j == 0))
    def _():
        lo, hi = _unpack_bf16_pairs(x_ref[...])
        half = lo.shape[1]
        xb[:, 0:half] = lo.astype(BF16)
        xb[:, half:] = hi.astype(BF16)
        if nj == 1:
            o_ref[...] = _pack_bf16_pairs(partial_sum())
        else:
            acc[...] = partial_sum()

    if nj > 2:
        @pl.when(used & (j > 0) & (j < nj - 1))
        def _():
            acc[...] += partial_sum()

    if nj > 1:
        @pl.when(used & (j == nj - 1))
        def _():
            o_ref[...] = _pack_bf16_pairs(acc[...] + partial_sum())

    @pl.when(jnp.logical_not(used) & (j == nj - 1))
    def _():
        o_ref[...] = jnp.zeros_like(o_ref)


def _moe_grouped(tile_expert, n_used, xs, wg, wu, wd, y_prev, tile0, n_rows):
    p, dp = xs.shape
    d = 2 * dp
    f = wg.shape[2]
    tm = MOE_TM
    tf = MOE_TF
    nj = f // tf

    def jj(t, j, te, nu):
        return jnp.where(t + tile0 < nu[0], j, nj - 1)

    in_specs = [pl.BlockSpec((tm, dp), lambda t, j, te, nu: (t, 0)),
                pl.BlockSpec((None, d, tf), lambda t, j, te, nu: (te[t + tile0], 0, jj(t, j, te, nu))),
                pl.BlockSpec((None, d, tf), lambda t, j, te, nu: (te[t + tile0], 0, jj(t, j, te, nu))),
                pl.BlockSpec((None, tf, d), lambda t, j, te, nu: (te[t + tile0], jj(t, j, te, nu), 0))]
    args = [tile_expert, n_used, xs, wg, wu, wd]
    aliases = {}
    if y_prev is not None:
        in_specs.append(pl.BlockSpec(memory_space=pl.ANY))
        args.append(y_prev)
        aliases = {len(args) - 1: 0}
    return pl.pallas_call(
        functools.partial(_moe_kernel, tile0=tile0, nj=nj),
        out_shape=jax.ShapeDtypeStruct((n_rows, dp), jnp.int32),
        grid_spec=pltpu.PrefetchScalarGridSpec(
            num_scalar_prefetch=2,
            grid=(p // tm, nj),
            in_specs=in_specs,
            out_specs=pl.BlockSpec((tm, dp), lambda t, j, te, nu: (t + tile0, 0)),
            scratch_shapes=[pltpu.VMEM((tm, d), BF16), pltpu.VMEM((tm, d), F32)]),
        input_output_aliases=aliases,
        compiler_params=_params("arbitrary", "arbitrary"),
        name="moe_grouped",
    )(*args)


def _combine_kernel(y1_ref, y2_ref, x_ref, g2_ref, rt_ref, o_ref):
    rt = rt_ref[...]
    w1, w2 = rt[:, 2:3], rt[:, 3:4]
    half = y1_ref.shape[1]
    for k, (a, c) in enumerate(zip(_unpack_bf16_pairs(y1_ref[...]), _unpack_bf16_pairs(y2_ref[...]))):
        cs = slice(k * half, (k + 1) * half)
        o_ref[:, cs] = x_ref[:, cs] + g2_ref[:, cs] * (w1 * a + w2 * c)


def _moe_combine(yg, x, g2, route):
    b, s, d = x.shape
    tm = _pick_tile(s, 512)
    tok = lambda n: pl.BlockSpec((None, tm, n), lambda bi, i: (bi, i, 0))
    return pl.pallas_call(
        _combine_kernel,
        out_shape=jax.ShapeDtypeStruct((b, s, d), F32),
        grid=(b, s // tm),
        in_specs=[tok(d // 2), pl.BlockSpec((None, tm, d // 2), lambda bi, i: (b + bi, i, 0)), tok(d),
                  pl.BlockSpec((None, 1, d), lambda bi, i: (bi, 0, 0)), tok(LANES)],
        out_specs=tok(d),
        compiler_params=_params("parallel", "parallel"),
        name="moe_combine",
    )(yg, yg, x, g2, route)


def _moe(h, route, x, g2, wg, wu, wd):
    b, s, d = x.shape
    n_tok = b * s
    tm = MOE_TM
    rt = route.reshape(n_tok, LANES)
    flat_e = jnp.concatenate([rt[:, 0], rt[:, 1]]).astype(jnp.int32)
    onehot = (flat_e[:, None] == jnp.arange(N_EXPERTS, dtype=jnp.int32)[None, :]).astype(jnp.int32)
    csum = jnp.cumsum(onehot, axis=0)
    rank = jnp.sum(csum * onehot, axis=1) - 1
    counts = csum[-1]
    tiles_e = (counts + tm - 1) // tm
    tiles_cum = jnp.cumsum(tiles_e)
    row_start = (tiles_cum - tiles_e) * tm
    dest = jnp.sum(onehot * row_start[None, :], axis=1) + rank
    n_tiles = 2 * n_tok // tm + N_EXPERTS
    tile_ids = jnp.arange(n_tiles, dtype=jnp.int32)
    tile_expert = jnp.sum((tile_ids[:, None] >= tiles_cum[None, :]).astype(jnp.int32), axis=1)
    last_e = jnp.max(jnp.where(tiles_e > 0, jnp.arange(N_EXPERTS, dtype=jnp.int32), 0))
    tile_expert = jnp.minimum(tile_expert, last_e).astype(jnp.int32)
    n_used = tiles_cum[-1:].astype(jnp.int32)
    dest = dest.astype(jnp.int32)
    row_src = _invert_rows(dest, n_tiles * tm)
    n_chunks = max(c for c in range(1, MOE_CHUNKS + 1) if n_tiles % c == 0)
    tiles_c = n_tiles // n_chunks
    h_flat = h.reshape(n_tok, d // 2)
    xs = [_gather_rows(row_src[c * tiles_c * tm:(c + 1) * tiles_c * tm], h_flat) for c in range(n_chunks)]
    y = None
    for c in range(n_chunks):
        y = _moe_grouped(tile_expert, n_used, xs[c], wg, wu, wd, y, c * tiles_c, n_tiles * tm)
    yg = _gather_rows(dest, y).reshape(2 * b, s, d // 2)
    return _moe_combine(yg, x, g2, route)


def _rope_tables(n):
    rows = n // GRID_W
    r = jnp.repeat(jnp.arange(rows), GRID_W).astype(F32)
    col = jnp.tile(jnp.arange(GRID_W), rows).astype(F32)
    freqs = ROPE_BASE ** (-jnp.arange(ROPE_FREQS, dtype=F32) / ROPE_FREQS)
    ang = jnp.stack([r[:, None] * freqs, col[:, None] * freqs], axis=1)
    ang = jnp.repeat(ang[:, :, None, :], 2, axis=2).reshape(n, HEAD_DIM)
    ang = jnp.tile(ang, (1, LANES // HEAD_DIM))
    cos, sin = jnp.cos(ang), jnp.sin(ang)
    first_half = (jnp.arange(LANES) % (2 * ROPE_FREQS)) < ROPE_FREQS
    return cos, jnp.where(first_half, -sin, 0.0), jnp.where(first_half, 0.0, sin)


def _lane_rows(lg):
    return jnp.repeat(lg.astype(F32), HEAD_DIM).reshape(RET_HEADS // 2, 1, LANES)


def kernel(x, c, ctx, c_ctx, ada_w, ada_b, norm1_w, norm2_w, w_in, w_out, q_norm_w, k_norm_w,
           attn_sink, ret_decay_f, ret_decay_b, ret_gn_w, conv_w, conv_b, conv_ln_w, conv_ln_b,
           ffn_w_gate, ffn_w_up, ffn_w_down, router_w, moe_w_gate, moe_w_up, moe_w_down):
    b, n, d = x.shape
    n_ctx = ctx.shape[1]
    depth = ada_w.shape[0]
    cond = jnp.zeros((SUBLANES, d), F32).at[0:b].set(c).at[b].set(c_ctx)
    mods = _adaln(cond, ada_w, ada_b).reshape(depth, SUBLANES, 6, d)
    cos, sa, sb = _rope_tables(n)
    ones_c = jnp.ones((n_ctx, LANES), F32)
    zeros_c = jnp.zeros((n_ctx, LANES), F32)
    zero_state = jnp.zeros((b, RET_HEADS // 2, LANES, LANES), F32)
    row = lambda v: v.reshape(1, -1)
    moe_f32 = [w.reshape(-1, w.shape[-1]) for w in (moe_w_gate, moe_w_up, moe_w_down)]
    cast_jobs = {l: [j for j in range(3) if j * min(depth, 2) // 3 == l] for l in range(depth)}
    moe_bf = [None, None, None]
    for l in range(depth):
        last = l == depth - 1
        m_lat = [mods[l, 0:b, k][:, None, :] for k in range(6)]
        m_ctx = [jnp.broadcast_to(mods[l, b, k][None, None, :], (b, 1, d)) for k in range(6)]
        w_in_bf = w_in[l].astype(BF16)
        w_out_bf = w_out[l].astype(BF16)
        qw = row(jnp.tile(q_norm_w[l], LANES // HEAD_DIM))
        kw = row(jnp.tile(k_norm_w[l], LANES // HEAD_DIM))
        lgf = _lane_rows(jax.nn.log_sigmoid(ret_decay_f[l].astype(F32)))
        lgb = _lane_rows(jax.nn.log_sigmoid(ret_decay_b[l].astype(F32)))
        sink_tab = jnp.repeat(attn_sink[l].astype(F32), ATT_BLOCK).reshape(ATT_Q_HEADS // 2, 2 * ATT_BLOCK)

        q, k, v, rk, rv, rq, rg, cv = _inproj(x, row(norm1_w[l]), m_lat[1], m_lat[0], w_in_bf,
                                               cos, sa, sb, qw, kw)
        qc, kc, vc, rkc, rvc, rqc, rgc, cvc = _inproj(ctx, row(norm1_w[l]), m_ctx[1], m_ctx[0], w_in_bf,
                                                       ones_c, zeros_c, zeros_c, qw, kw)
        rf_c, s_f = _ret_states(lgf, rkc, rvc, zero_state, reverse=False)
        rb_c, s_b = _ret_states(lgb, rkc, rvc, zero_state, reverse=True)
        rf, _ = _ret_states(lgf, rk, rv, s_f, reverse=False)
        rb, _ = _ret_states(lgb, rk, rv, s_b, reverse=True)

        att, cast_out = _attention(q, k, v, kc, vc, sink_tab, window=True,
                                   cast=[moe_f32[j] for j in cast_jobs[l]])
        for j, w_bf in zip(cast_jobs[l], cast_out):
            moe_bf[j] = w_bf.reshape((moe_w_gate, moe_w_up, moe_w_down)[j].shape)
        ret = _ret_out(lgf, lgb, rq, rk, rv, rg, rf, rb, row(ret_gn_w[l]))
        cnv = _conv(cv, conv_w[l], row(conv_b[l]), row(conv_ln_w[l]), row(conv_ln_b[l]))

        if l % 2 == 0:
            i = l // 2
            wg, wu, wd = ffn_w_gate[i].astype(BF16), ffn_w_up[i].astype(BF16), ffn_w_down[i].astype(BF16)
            x_mid, h2 = _outproj(att, ret, cnv, w_out_bf, x, m_lat[2], row(norm2_w[l]), m_lat[4], m_lat[3])
            x_new = _ffn(h2, wg, wu, wd, x_mid, m_lat[5])
        else:
            i = l // 2
            wg, wu, wd = moe_bf[0][i], moe_bf[1][i], moe_bf[2][i]
            x_mid, h2, route = _outproj(att, ret, cnv, w_out_bf, x, m_lat[2], row(norm2_w[l]),
                                        m_lat[4], m_lat[3], router=router_w[i])
            x_new = _moe(h2, route, x_mid, m_lat[5], wg, wu, wd)

        if not last:
            att_c, _ = _attention(qc, None, None, kc, vc, sink_tab, window=False)
            ret_c = _ret_out(lgf, lgb, rqc, rkc, rvc, rgc, rf_c, rb_c, row(ret_gn_w[l]))
            cnv_c = _conv(cvc, conv_w[l], row(conv_b[l]), row(conv_ln_w[l]), row(conv_ln_b[l]))
            if l % 2 == 0:
                c_mid, h2c = _outproj(att_c, ret_c, cnv_c, w_out_bf, ctx, m_ctx[2], row(norm2_w[l]),
                                      m_ctx[4], m_ctx[3])
                ctx = _ffn(h2c, wg, wu, wd, c_mid, m_ctx[5])
            else:
                c_mid, h2c, route_c = _outproj(att_c, ret_c, cnv_c, w_out_bf, ctx, m_ctx[2],
                                               row(norm2_w[l]), m_ctx[4], m_ctx[3],
                                               router=router_w[i])
                ctx = _moe(h2c, route_c, c_mid, m_ctx[5], wg, wu, wd)
        x = x_new
    return x
```

```python
import functools

import jax
import jax.numpy as jnp
from jax import lax
from jax.experimental import pallas as pl
from jax.experimental.pallas import tpu as pltpu
from jax.experimental.pallas import tpu_sc as plsc

F32 = jnp.float32
BF16 = jnp.bfloat16

GRID_W = 64
HEAD_DIM = 64
ATT_Q_HEADS = 8
ATT_KV_HEADS = 2
ATT_WINDOW = 128
ATT_BLOCK = 128
RET_HEADS = 4
RET_CHUNK = 128
RET_K_SCALE = HEAD_DIM ** -0.5
ATT_SCALE = HEAD_DIM ** -0.5
CONV_CH = 256
CONV_WIDTH = 31
ROPE_BASE = 10000.0
ROPE_FREQS = HEAD_DIM // 4
D_ATT = ATT_Q_HEADS * HEAD_DIM
D_RET = RET_HEADS * HEAD_DIM
ATT_KV_W = ATT_KV_HEADS * HEAD_DIM
C_ATT_K = 0
C_ATT_V = C_ATT_K + ATT_KV_W
C_RET_K = C_ATT_V + ATT_KV_W
C_RET_V = C_RET_K + D_RET
C_ATT_Q = C_RET_V + D_RET
C_RET_Q = C_ATT_Q + D_ATT
C_RET_G = C_RET_Q + D_RET
C_CONV = C_RET_G + D_RET
N_EXPERTS = 8
EPS = 1e-6
NEG_INF = -1e30

LANES = 128
SUBLANES = 8
VMEM_LIMIT = 48 * 1024 * 1024
CONV_HALO = 16
CONV_SUB = 32
CONV_CHAINS = 4
ATT_QBLOCKS = 4
RET_UNROLL = 4
MOE_TM = 512
MOE_CHUNKS = 2
MOE_SUB = 512
FFN_SUB = 512
MOE_TF = 1792
SC_LANES = 16
SC_WINDOW = 64


def _params(*sem):
    return pltpu.CompilerParams(dimension_semantics=sem, vmem_limit_bytes=VMEM_LIMIT)


def _sigmoid(x):
    return 1.0 / (1.0 + jnp.exp(-x))


def _pack_bf16_pairs(v):
    c = v.shape[1] // 2
    bits = pltpu.bitcast(v.astype(BF16).astype(F32), jnp.uint32)
    packed = (bits[:, c:] & jnp.uint32(0xFFFF0000)) | (bits[:, :c] >> 16)
    return pltpu.bitcast(packed, jnp.int32)


def _unpack_bf16_pairs(p):
    bits = pltpu.bitcast(p, jnp.uint32)
    return pltpu.bitcast(bits << 16, F32), pltpu.bitcast(bits & jnp.uint32(0xFFFF0000), F32)


def _pick_tile(n, pref):
    t = min(n, pref)
    assert n % t == 0, (n, t)
    return t


def _adaln_kernel(c_ref, w_ref, b_ref, o_ref):
    c = c_ref[...]
    s = c * _sigmoid(c)
    o_ref[...] = jnp.dot(s, w_ref[...], preferred_element_type=F32,
                         precision=lax.Precision.HIGHEST) + b_ref[...]


def _adaln(cond, ada_w, ada_b):
    depth, d, n = ada_w.shape
    tn = _pick_tile(n, 1536)
    return pl.pallas_call(
        _adaln_kernel,
        out_shape=jax.ShapeDtypeStruct((depth, cond.shape[0], n), F32),
        grid=(depth, n // tn),
        in_specs=[pl.BlockSpec(cond.shape, lambda l, j: (0, 0)),
                  pl.BlockSpec((None, d, tn), lambda l, j: (l, 0, j)),
                  pl.BlockSpec((None, 1, tn), lambda l, j: (l, 0, j))],
        out_specs=pl.BlockSpec((None, cond.shape[0], tn), lambda l, j: (l, 0, j)),
        compiler_params=_params("parallel", "parallel"),
        name="adaln",
    )(cond, ada_w, ada_b.reshape(depth, 1, n))


def _modulated_rms(x, nw, sc, sh):
    ms = jnp.mean(x * x, axis=-1, keepdims=True)
    return (x * lax.rsqrt(ms + EPS) * nw) * (1.0 + sc) + sh


def _head_group_matrix():
    r = lax.broadcasted_iota(jnp.int32, (2 * LANES, 2 * LANES), 0) // HEAD_DIM
    c = lax.broadcasted_iota(jnp.int32, (2 * LANES, 2 * LANES), 1) // HEAD_DIM
    return jnp.where(r == c, 1.0, 0.0).astype(BF16)


def _per_head_sum(v, grp):
    hi = v.astype(BF16)
    lo = (v - hi.astype(F32)).astype(BF16)
    r = jnp.dot(jnp.concatenate([hi, lo], axis=1), grp, preferred_element_type=F32)
    return r[:, 0:LANES] + r[:, LANES:]


def _dup_halves(t):
    sw = pltpu.roll(t, HEAD_DIM, 1)
    lo = lax.broadcasted_iota(jnp.int32, t.shape, 1) < HEAD_DIM
    return jnp.where(lo, t, sw), jnp.where(lo, sw, t)


def _inproj_kernel(x_ref, nw_ref, sc_ref, sh_ref, w_ref, cos_ref, sa_ref, sb_ref, qw_ref, kw_ref,
                   q_ref, k_ref, v_ref, rk_ref, rv_ref, rq_ref, rg_ref, cv_ref):
    hb = _modulated_rms(x_ref[...], nw_ref[...], sc_ref[...], sh_ref[...]).astype(BF16)

    def proj(c0, n):
        return jnp.dot(hb, w_ref[:, c0:c0 + n], preferred_element_type=F32)

    grp = _head_group_matrix()
    cos, sa, sb = cos_ref[...], sa_ref[...], sb_ref[...]

    def norm_rope(p, wrow):
        y = p * lax.rsqrt(_per_head_sum(p * p, grp) * (1.0 / HEAD_DIM) + EPS) * wrow
        return (y * cos + pltpu.roll(y, LANES - ROPE_FREQS, 1) * sa
                + pltpu.roll(y, ROPE_FREQS, 1) * sb)

    def put_kv(kv):
        k0, k1 = _dup_halves(norm_rope(kv[:, 0:ATT_KV_W], kw_ref[...]))
        k_ref[:, 0:LANES] = k0.astype(BF16)
        k_ref[:, LANES:2 * LANES] = k1.astype(BF16)
        v_ref[...] = kv[:, ATT_KV_W:].T.astype(BF16)

    def put_q(qall):
        qw = qw_ref[...] * ATT_SCALE
        for j in range(D_ATT // LANES):
            q_ref[:, j * LANES:(j + 1) * LANES] = norm_rope(qall[:, j * LANES:(j + 1) * LANES], qw).astype(BF16)

    def put_rk(p):
        rk_ref[...] = (p * RET_K_SCALE).astype(BF16)

    def put_rv(p):
        rv_ref[...] = p.astype(BF16)

    def put_rq(p):
        rq_ref[...] = p.astype(BF16)

    def put_rg(p):
        rg_ref[...] = p

    def put_conv(p):
        cv_ref[...] = p[:, 0:CONV_CH] * _sigmoid(p[:, CONV_CH:])

    stages = [(C_ATT_K, 2 * ATT_KV_W, put_kv), (C_ATT_Q, D_ATT, put_q), (C_RET_K, D_RET, put_rk),
              (C_RET_V, D_RET, put_rv), (C_RET_Q, D_RET, put_rq), (C_RET_G, D_RET, put_rg),
              (C_CONV, 2 * CONV_CH, put_conv)]
    p_next = proj(*stages[0][:2])
    for n, (_, _, put) in enumerate(stages):
        p_cur = p_next
        if n + 1 < len(stages):
            p_next = proj(*stages[n + 1][:2])
        put(p_cur)


def _inproj(x, nw, sc, sh, w_bf, cos, sa, sb, qw, kw):
    b, s, d = x.shape
    tm = _pick_tile(s, 512)
    row = lambda n: pl.BlockSpec((1, n), lambda bi, i: (0, 0))
    per_b = pl.BlockSpec((None, 1, d), lambda bi, i: (bi, 0, 0))
    tab = pl.BlockSpec((tm, LANES), lambda bi, i: (i, 0))
    tok = lambda n: pl.BlockSpec((None, tm, n), lambda bi, i: (bi, i, 0))
    shp = lambda n, dt: jax.ShapeDtypeStruct((b, s, n), dt)
    return pl.pallas_call(
        _inproj_kernel,
        out_shape=(shp(D_ATT, BF16), shp(2 * LANES, BF16), jax.ShapeDtypeStruct((b, ATT_KV_W, s), BF16),
                   shp(D_RET, BF16), shp(D_RET, BF16), shp(D_RET, BF16), shp(D_RET, F32),
                   shp(CONV_CH, F32)),
        grid=(b, s // tm),
        in_specs=[tok(d), row(d), per_b, per_b,
                  pl.BlockSpec(w_bf.shape, lambda bi, i: (0, 0)),
                  tab, tab, tab, row(LANES), row(LANES)],
        out_specs=(tok(D_ATT), tok(2 * LANES), pl.BlockSpec((None, ATT_KV_W, tm), lambda bi, i: (bi, 0, i)),
                   tok(D_RET), tok(D_RET), tok(D_RET), tok(D_RET), tok(CONV_CH)),
        compiler_params=_params("parallel", "parallel"),
        name="inproj",
    )(x, nw, sc, sh, w_bf, cos, sa, sb, qw, kw)


def _attn_kernel(*refs, window, nq, n_cast):
    blk = ATT_BLOCK
    n_in = len(refs) - 1 - n_cast
    for src, dst in zip(refs[n_in - n_cast:n_in], refs[n_in + 1:]):
        dst[...] = src[...].astype(BF16)
    refs = refs[:n_in - n_cast] + refs[n_in:n_in + 1]
    if window:
        q_ref = refs[0]
        k_refs = refs[1:nq + 3]
        v_refs = refs[nq + 3:2 * nq + 5]
        kx_ref, vx_ref, sink_ref, o_ref = refs[2 * nq + 5:]
    else:
        q_ref, kx_ref, vx_ref, sink_ref, o_ref = refs
    n_ctx = kx_ref.shape[0]
    nk = 3 * blk + n_ctx if window else n_ctx
    if window:
        i = pl.program_id(1)
        last = nq * pl.num_programs(1) - 1
        key = lax.broadcasted_iota(jnp.int32, (blk, 2 * blk), 0)
        qry = lax.broadcasted_iota(jnp.int32, (blk, 2 * blk), 1) & (blk - 1)

        def band_masks(sub):
            off_prev = jnp.where(nq * i + sub > 0, 0, blk)
            off_next = jnp.where(nq * i + sub < last, 0, blk)
            return key >= qry + off_prev, key + off_next <= qry
    first_head = lax.broadcasted_iota(jnp.int32, (blk, LANES), 1) < HEAD_DIM
    ones = jnp.ones((2 * SUBLANES, nk), BF16)

    def scores(sub, pair):
        g = pair // 2
        gs = slice(g * LANES, (g + 1) * LANES)
        if window:
            kcat = jnp.concatenate([r[:, gs] for r in k_refs[sub:sub + 3]] + [kx_ref[:, gs]], axis=0)
        else:
            kcat = kx_ref[:, gs]
        qp = q_ref[sub * blk:(sub + 1) * blk, pair * LANES:(pair + 1) * LANES]
        zero = jnp.zeros_like(qp)
        w = jnp.concatenate([jnp.where(first_head, qp, zero), jnp.where(first_head, zero, qp)], axis=0)
        return lax.dot_general(kcat, w, (((1,), (1,)), ((), ())), preferred_element_type=F32)

    def softmax(sub, pair, s):
        if window:
            mask_prev, mask_next = band_masks(sub)
            parts = [jnp.where(mask_prev, s[0:blk], NEG_INF), s[blk:2 * blk],
                     jnp.where(mask_next, s[2 * blk:3 * blk], NEG_INF), s[3 * blk:]]
        else:
            parts = [s]
        snk = sink_ref[pair:pair + 1, :]
        m = snk
        for part in parts:
            m = jnp.maximum(m, jnp.max(part, axis=0, keepdims=True))
        p = jnp.concatenate([jnp.exp(part - m).astype(BF16) for part in parts], axis=0)
        return p, jnp.exp(snk - m)

    def output(sub, pair, p, sink_p):
        g = pair // 2
        vs = slice(g * HEAD_DIM, (g + 1) * HEAD_DIM)
        if window:
            vt = jnp.concatenate([r[vs, :] for r in v_refs[sub:sub + 3]] + [vx_ref[vs, :]], axis=1)
        else:
            vt = vx_ref[vs, :]
        vaug = jnp.concatenate([vt, ones], axis=0)
        o = jnp.dot(vaug, p, preferred_element_type=F32)
        on = o[0:HEAD_DIM, :] * (1.0 / (o[HEAD_DIM:HEAD_DIM + 1, :] + sink_p))
        ot = jnp.concatenate([on[:, 0:blk], on[:, blk:2 * blk]], axis=0)
        o_ref[sub * blk:(sub + 1) * blk, pair * LANES:(pair + 1) * LANES] = ot.T.astype(BF16)

    units = [(sub, pair) for sub in range(nq) for pair in range(ATT_Q_HEADS // 2)]
    s_next = scores(*units[0])
    for n, unit in enumerate(units):
        s_cur = s_next
        if n + 1 < len(units):
            s_next = scores(*units[n + 1])
        output(*unit, *softmax(*unit, s_cur))


def _attention(q, k, vt, kx, vxt, sink_tab, window, cast=()):
    b, s, _ = q.shape
    blk = ATT_BLOCK
    nb = s // blk
    nq = ATT_QBLOCKS if nb % ATT_QBLOCKS == 0 else 1
    n_ctx = kx.shape[1]
    qspec = pl.BlockSpec((None, nq * blk, D_ATT), lambda bi, i: (bi, i, 0))
    kctx = pl.BlockSpec((None, n_ctx, 2 * LANES), lambda bi, i: (bi, 0, 0))
    vctx = pl.BlockSpec((None, ATT_KV_W, n_ctx), lambda bi, i: (bi, 0, 0))
    snk = pl.BlockSpec(sink_tab.shape, lambda bi, i: (0, 0))
    if window:
        at = lambda off: (lambda i: jnp.clip(nq * i + off, 0, nb - 1))
        kspec = lambda f: pl.BlockSpec((None, blk, 2 * LANES), lambda bi, i: (bi, f(i), 0))
        vspec = lambda f: pl.BlockSpec((None, ATT_KV_W, blk), lambda bi, i: (bi, 0, f(i)))
        offs = range(-1, nq + 1)
        in_specs = ([qspec] + [kspec(at(o)) for o in offs] + [vspec(at(o)) for o in offs]
                    + [kctx, vctx, snk])
        args = (q,) + (k,) * (nq + 2) + (vt,) * (nq + 2) + (kx, vxt, sink_tab)
    else:
        in_specs = [qspec, kctx, vctx, snk]
        args = (q, kx, vxt, sink_tab)
    steps = nb // nq
    out_shape, out_specs = [jax.ShapeDtypeStruct((b, s, D_ATT), BF16)], [qspec]
    for w in cast:
        rows = w.shape[0] // (b * steps)
        assert rows * b * steps == w.shape[0] and rows % (2 * SUBLANES) == 0, w.shape
        slab = pl.BlockSpec((rows, w.shape[1]), lambda bi, i: (bi * steps + i, 0))
        in_specs.append(slab)
        out_specs.append(slab)
        out_shape.append(jax.ShapeDtypeStruct(w.shape, BF16))
    res = pl.pallas_call(
        functools.partial(_attn_kernel, window=window, nq=nq, n_cast=len(cast)),
        out_shape=tuple(out_shape),
        grid=(b, steps),
        in_specs=in_specs,
        out_specs=tuple(out_specs),
        compiler_params=_params("parallel", "parallel"),
        name="attention_window" if window else "attention_ctx",
    )(*args, *cast)
    return res[0], tuple(res[1:])


def _ret_state_kernel(lgl_ref, k_ref, v_ref, s0_ref, r_ref, fin_ref, s_scr, *, cb, reverse):
    c = pl.program_id(1)
    L = RET_CHUNK

    @pl.when(c == 0)
    def _():
        s_scr[...] = s0_ref[...]

    jj = lax.broadcasted_iota(jnp.int32, (L, LANES), 0).astype(F32)
    expo = jj if reverse else (L - 1.0) - jj
    same_head = (lax.broadcasted_iota(jnp.int32, (LANES, LANES), 0) // HEAD_DIM
                 == lax.broadcasted_iota(jnp.int32, (LANES, LANES), 1) // HEAD_DIM)
    for pr in range(RET_HEADS // 2):
        cs = slice(pr * LANES, (pr + 1) * LANES)
        lgl = lgl_ref[pr]
        kdec = jnp.exp(expo * lgl)
        cdec = jnp.exp(float(L) * lgl)
        state = s_scr[pr]
        for t in range(cb):
            cc = cb - 1 - t if reverse else t
            rows = slice(cc * L, (cc + 1) * L)
            r_ref[cc, pr] = state
            kd = k_ref[rows, cs].astype(F32) * kdec
            u = jnp.dot(kd.T.astype(BF16), v_ref[rows, cs], preferred_element_type=F32)
            state = cdec * state + jnp.where(same_head, u, 0.0)
        s_scr[pr] = state

    @pl.when(c == pl.num_programs(1) - 1)
    def _():
        fin_ref[...] = s_scr[...]


def _ret_states(lgl, rk, rv, s0, reverse):
    b, s, _ = rk.shape
    nc = s // RET_CHUNK
    cb = _pick_tile(nc, 8)
    nblk = nc // cb
    npair = RET_HEADS // 2
    blk_idx = (lambda c: nblk - 1 - c) if reverse else (lambda c: c)
    tok = pl.BlockSpec((None, cb * RET_CHUNK, D_RET), lambda bi, c: (bi, blk_idx(c), 0))
    st = pl.BlockSpec((None, npair, LANES, LANES), lambda bi, c: (bi, 0, 0, 0))
    return pl.pallas_call(
        functools.partial(_ret_state_kernel, cb=cb, reverse=reverse),
        out_shape=(jax.ShapeDtypeStruct((b, nc, npair, LANES, LANES), F32),
                   jax.ShapeDtypeStruct((b, npair, LANES, LANES), F32)),
        grid=(b, nblk),
        in_specs=[pl.BlockSpec(lgl.shape, lambda bi, c: (0, 0, 0)), tok, tok, st],
        out_specs=(pl.BlockSpec((None, cb, npair, LANES, LANES),
                                lambda bi, c: (bi, blk_idx(c), 0, 0, 0)), st),
        scratch_shapes=[pltpu.VMEM((npair, LANES, LANES), F32)],
        compiler_params=_params("parallel", "arbitrary"),
        name="ret_state_bwd" if reverse else "ret_state_fwd",
    )(lgl, rk, rv, s0)


def _ret_out_kernel(lgf_ref, lgb_ref, q_ref, k_ref, v_ref, g_ref, rf_ref, rb_ref, gnw_ref, o_ref, *, cb):
    L = RET_CHUNK
    i0 = lax.broadcasted_iota(jnp.int32, (L, LANES), 0).astype(F32)
    i1 = lax.broadcasted_iota(jnp.int32, (L, LANES), 1).astype(F32)
    diff = i0 - i1
    lo = lax.broadcasted_iota(jnp.int32, (L, LANES), 1) < HEAD_DIM
    grp = _head_group_matrix()
    inv = 1.0 / HEAD_DIM
    tabs = []
    for pr in range(RET_HEADS // 2):
        lgf, lgb = lgf_ref[pr], lgb_ref[pr]
        dmat = [jnp.where(diff >= 0.0,
                          jnp.exp(jnp.maximum(diff, 0.0) * lgf[:, a:a + 1]),
                          jnp.exp(jnp.maximum(-diff, 0.0) * lgb[:, a:a + 1]))
                for a in (0, HEAD_DIM)]
        tabs.append((jnp.exp((i0 + 1.0) * lgf), jnp.exp((float(L) - i0) * lgb),
                     jnp.concatenate(dmat, axis=0), gnw_ref[:, pr * LANES:(pr + 1) * LANES]))

    def decayed_scores(t, pr):
        rows = pl.ds(pl.multiple_of(t * L, L), L)
        cs = slice(pr * LANES, (pr + 1) * LANES)
        qp, kp = q_ref[rows, cs], k_ref[rows, cs]
        zero = jnp.zeros_like(kp)
        qz = jnp.concatenate([jnp.where(lo, qp, zero), jnp.where(lo, zero, qp)], axis=0)
        return lax.dot_general(qz, kp, (((1,), (1,)), ((), ())), preferred_element_type=F32) * tabs[pr][2]

    def readout(t, pr, a):
        rows = pl.ds(pl.multiple_of(t * L, L), L)
        cs = slice(pr * LANES, (pr + 1) * LANES)
        xif, xib, _, gnw = tabs[pr]
        vp = v_ref[rows, cs]
        zero = jnp.zeros_like(vp)
        qf = q_ref[rows, cs].astype(F32)
        lhs = jnp.concatenate([a[0:L].astype(BF16), a[L:].astype(BF16),
                               (qf * xif).astype(BF16), (qf * xib).astype(BF16)], axis=1)
        rhs = jnp.concatenate([jnp.where(lo, vp, zero), jnp.where(lo, zero, vp),
                               rf_ref[t, pr].astype(BF16), rb_ref[t, pr].astype(BF16)], axis=0)
        acc = jnp.dot(lhs, rhs, preferred_element_type=F32)
        dl = acc - _per_head_sum(acc, grp) * inv
        var = _per_head_sum(dl * dl, grp) * inv
        y = dl * lax.rsqrt(var + EPS) * gnw
        gt = g_ref[rows, cs]
        o_ref[rows, cs] = (gt * _sigmoid(gt) * y).astype(BF16)

    unroll = min(RET_UNROLL, cb)

    def group(gi, carry):
        units = [(gi * unroll + u, pr) for u in range(unroll) for pr in range(RET_HEADS // 2)]
        a_next = decayed_scores(*units[0])
        for n, unit in enumerate(units):
            a_cur = a_next
            if n + 1 < len(units):
                a_next = decayed_scores(*units[n + 1])
            readout(*unit, a_cur)
        return carry

    lax.fori_loop(0, cb // unroll, group, 0)


def _ret_out(lgf, lgb, rq, rk, rv, rg, rf, rb, gnw):
    b, s, _ = rq.shape
    nc = s // RET_CHUNK
    cb = _pick_tile(nc, 8)
    npair = RET_HEADS // 2
    tok = pl.BlockSpec((None, cb * RET_CHUNK, D_RET), lambda bi, c: (bi, c, 0))
    st = pl.BlockSpec((None, cb, npair, LANES, LANES), lambda bi, c: (bi, c, 0, 0, 0))
    lg = pl.BlockSpec(lgf.shape, lambda bi, c: (0, 0, 0))
    return pl.pallas_call(
        functools.partial(_ret_out_kernel, cb=cb),
        out_shape=jax.ShapeDtypeStruct((b, s, D_RET), BF16),
        grid=(b, nc // cb),
        in_specs=[lg, lg, tok, tok, tok, tok, st, st, pl.BlockSpec((1, D_RET), lambda bi, c: (0, 0))],
        out_specs=tok,
        compiler_params=_params("parallel", "parallel"),
        name="ret_out",
    )(lgf, lgb, rq, rk, rv, rg, rf, rb, gnw)


def _conv_kernel(prev_ref, cur_ref, next_ref, w_ref, b_ref, lnw_ref, lnb_ref, o_ref,
                 xpad, shifted, hbuf, wtile):
    i = pl.program_id(1)
    tm = cur_ref.shape[0]
    halo = CONV_HALO
    first = i == 0
    last = i == pl.num_programs(1) - 1
    xpad[0:halo, :] = jnp.where(first, 0.0, prev_ref[...])
    xpad[halo:halo + tm, :] = cur_ref[...]
    xpad[halo + tm:2 * halo + tm, :] = jnp.where(last, 0.0, next_ref[...])
    span = tm + 2 * halo - SUBLANES
    for r in range(SUBLANES):
        shifted[r] = xpad[r:r + span, :]
    base = halo - CONV_WIDTH // 2
    bias = b_ref[...]
    ch = cur_ref.shape[1]
    for w in range(CONV_WIDTH):
        wtile[w] = jnp.broadcast_to(w_ref[w:w + 1, :], (SUBLANES, ch))

    taps_by_shift = {}
    for w in range(CONV_WIDTH):
        taps_by_shift.setdefault((base + w) % SUBLANES, []).append(((base + w) // SUBLANES, w))

    def sub(sb, carry):
        r0 = pl.multiple_of(sb * CONV_SUB, CONV_SUB)
        chains = [None] * CONV_CHAINS
        groups = CONV_SUB // SUBLANES
        for r, taps in sorted(taps_by_shift.items()):
            a_lo = min(a for a, _ in taps)
            n_g = max(a for a, _ in taps) - a_lo + groups
            win = shifted[r, pl.ds(r0 + a_lo * SUBLANES, n_g * SUBLANES), :].reshape(n_g, SUBLANES, ch)
            for a, w in taps:
                term = win[a - a_lo:a - a_lo + groups] * wtile[w]
                c = w % CONV_CHAINS
                chains[c] = term if chains[c] is None else chains[c] + term
        total = (chains[0] + chains[1]) + (chains[2] + chains[3])
        hbuf[pl.ds(r0, CONV_SUB), :] = total.reshape(CONV_SUB, ch) + bias
        return carry

    lax.fori_loop(0, tm // CONV_SUB, sub, 0)
    h = hbuf[...]
    mu = jnp.mean(h, axis=-1, keepdims=True)
    dl = h - mu
    var = jnp.mean(dl * dl, axis=-1, keepdims=True)
    y = dl * lax.rsqrt(var + EPS) * lnw_ref[...] + lnb_ref[...]
    o_ref[...] = (y * _sigmoid(y)).astype(BF16)


def _conv(cv, w, bias, lnw, lnb):
    b, s, ch = cv.shape
    tm = _pick_tile(s, 512)
    hpb = tm // CONV_HALO
    nh = s // CONV_HALO
    row = pl.BlockSpec((1, ch), lambda bi, i: (0, 0))
    span = tm + 2 * CONV_HALO - SUBLANES
    return pl.pallas_call(
        _conv_kernel,
        out_shape=jax.ShapeDtypeStruct((b, s, ch), BF16),
        grid=(b, s // tm),
        in_specs=[pl.BlockSpec((None, CONV_HALO, ch), lambda bi, i: (bi, jnp.maximum(i * hpb - 1, 0), 0)),
                  pl.BlockSpec((None, tm, ch), lambda bi, i: (bi, i, 0)),
                  pl.BlockSpec((None, CONV_HALO, ch),
                               lambda bi, i: (bi, jnp.minimum((i + 1) * hpb, nh - 1), 0)),
                  pl.BlockSpec(w.shape, lambda bi, i: (0, 0)), row, row, row],
        out_specs=pl.BlockSpec((None, tm, ch), lambda bi, i: (bi, i, 0)),
        scratch_shapes=[pltpu.VMEM((tm + 2 * CONV_HALO, ch), F32),
                        pltpu.VMEM((SUBLANES, span, ch), F32),
                        pltpu.VMEM((tm, ch), F32),
                        pltpu.VMEM((CONV_WIDTH, SUBLANES, ch), F32)],
        compiler_params=_params("parallel", "parallel"),
        name="conv",
    )(cv, cv, cv, w, bias, lnw, lnb)


def _outproj_kernel(*refs, route):
    if route:
        (att_ref, ret_ref, cnv_ref, w_ref, x_ref, g1_ref, nw_ref, sc_ref, sh_ref, rcat_ref,
         xo_ref, h_ref, rt_ref) = refs
    else:
        att_ref, ret_ref, cnv_ref, w_ref, x_ref, g1_ref, nw_ref, sc_ref, sh_ref, xo_ref, h_ref = refs
    y = (jnp.dot(att_ref[...], w_ref[0:D_ATT, :], preferred_element_type=F32)
         + jnp.dot(ret_ref[...], w_ref[D_ATT:D_ATT + D_RET, :], preferred_element_type=F32)
         + jnp.dot(cnv_ref[...], w_ref[D_ATT + D_RET:, :], preferred_element_type=F32))
    xn = x_ref[...] + g1_ref[...] * y
    xo_ref[...] = xn
    h = _modulated_rms(xn, nw_ref[...], sc_ref[...], sh_ref[...])
    h_ref[...] = _pack_bf16_pairs(h) if route else h.astype(BF16)
    if route:
        hi = h.astype(BF16)
        lo = (h - hi.astype(F32)).astype(BF16)
        tm = h.shape[0]
        r = jnp.dot(jnp.concatenate([hi, lo], axis=0), rcat_ref[...], preferred_element_type=F32)
        logits = (r[0:tm, 0:LANES] + r[0:tm, LANES:]) + (r[tm:, 0:LANES] + r[tm:, LANES:])
        lane = lax.broadcasted_iota(jnp.int32, logits.shape, 1).astype(F32)
        logits = jnp.where(lane < N_EXPERTS, logits, NEG_INF)
        m1 = jnp.max(logits, axis=-1, keepdims=True)
        i1 = jnp.min(jnp.where(logits == m1, lane, float(LANES)), axis=-1, keepdims=True)
        rest = jnp.where(lane == i1, NEG_INF, logits)
        m2 = jnp.max(rest, axis=-1, keepdims=True)
        i2 = jnp.min(jnp.where(rest == m2, lane, float(LANES)), axis=-1, keepdims=True)
        e2 = jnp.exp(m2 - m1)
        w1 = 1.0 / (1.0 + e2)
        w2 = e2 / (1.0 + e2)
        rt_ref[...] = jnp.where(lane == 0.0, i1,
                                jnp.where(lane == 1.0, i2,
                                          jnp.where(lane == 2.0, w1, jnp.where(lane == 3.0, w2, 0.0))))


def _outproj(att, ret, cnv, w_bf, x, g1, nw, sc, sh, router=None):
    b, s, d = x.shape
    tm = _pick_tile(s, 512)
    tok = lambda n: pl.BlockSpec((None, tm, n), lambda bi, i: (bi, i, 0))
    per_b = pl.BlockSpec((None, 1, d), lambda bi, i: (bi, 0, 0))
    row = pl.BlockSpec((1, d), lambda bi, i: (0, 0))
    in_specs = [tok(D_ATT), tok(D_RET), tok(CONV_CH), pl.BlockSpec(w_bf.shape, lambda bi, i: (0, 0)),
                tok(d), per_b, row, per_b, per_b]
    args = [att, ret, cnv, w_bf, x, g1, nw, sc, sh]
    h_shape = (b, s, d // 2) if router is not None else (b, s, d)
    out_shape = [jax.ShapeDtypeStruct((b, s, d), F32),
                 jax.ShapeDtypeStruct(h_shape, jnp.int32 if router is not None else BF16)]
    out_specs = [tok(d), tok(h_shape[2])]
    if router is not None:
        rpad = jnp.zeros((d, LANES), F32).at[:, :N_EXPERTS].set(router)
        rhi = rpad.astype(BF16)
        rlo = (rpad - rhi.astype(F32)).astype(BF16)
        in_specs += [pl.BlockSpec((d, 2 * LANES), lambda bi, i: (0, 0))]
        args += [jnp.concatenate([rhi, rlo], axis=1)]
        out_shape.append(jax.ShapeDtypeStruct((b, s, LANES), F32))
        out_specs.append(tok(LANES))
    return pl.pallas_call(
        functools.partial(_outproj_kernel, route=router is not None),
        out_shape=tuple(out_shape),
        grid=(b, s // tm),
        in_specs=in_specs,
        out_specs=tuple(out_specs),
        compiler_params=_params("parallel", "parallel"),
        name="outproj_route" if router is not None else "outproj",
    )(*args)


def _swiglu_chunks(h, wg_ref, wu_ref, wd_ref, sub):
    total = None
    for c0 in range(0, wg_ref.shape[1], sub):
        cs = slice(c0, min(c0 + sub, wg_ref.shape[1]))
        gate = jnp.dot(h, wg_ref[:, cs].astype(BF16), preferred_element_type=F32)
        up = jnp.dot(h, wu_ref[:, cs].astype(BF16), preferred_element_type=F32)
        act = (gate * _sigmoid(gate) * up).astype(BF16)
        part = jnp.dot(act, wd_ref[cs, :].astype(BF16), preferred_element_type=F32)
        total = part if total is None else total + part
    return total


def _ffn_kernel(h_ref, wg_ref, wu_ref, wd_ref, x_ref, g2_ref, o_ref):
    y = _swiglu_chunks(h_ref[...], wg_ref, wu_ref, wd_ref, FFN_SUB)
    o_ref[...] = x_ref[...] + g2_ref[...] * y


def _ffn(h, wg, wu, wd, x, g2):
    b, s, d = x.shape
    f = wg.shape[1]
    tm = _pick_tile(s, 512)
    tok = lambda: pl.BlockSpec((None, tm, d), lambda bi, i: (bi, i, 0))
    res = lambda shape: pl.BlockSpec(shape, lambda bi, i: (0, 0), pipeline_mode=pl.Buffered(1))
    return pl.pallas_call(
        _ffn_kernel,
        out_shape=jax.ShapeDtypeStruct((b, s, d), F32),
        grid=(b, s // tm),
        in_specs=[tok(), res((d, f)), res((d, f)), res((f, d)), tok(),
                  pl.BlockSpec((None, 1, d), lambda bi, i: (bi, 0, 0))],
        out_specs=tok(),
        compiler_params=_params("parallel", "parallel"),
        name="ffn_dense",
    )(h, wg, wu, wd, x, g2)


def _gather_rows(idx, src):
    n = idx.shape[0]
    d = src.shape[1]
    mesh = plsc.VectorSubcoreMesh(core_axis_name="core", subcore_axis_name="subcore")

    @pl.kernel(out_type=jax.ShapeDtypeStruct((n, d), src.dtype), mesh=mesh, name="moe_gather")
    def gather(src_hbm, idx_hbm, out_hbm):
        def body(idx_vmem, out_vmem):
            pltpu.sync_copy(src_hbm.at[idx_vmem.at[0, pl.ds(0, SC_WINDOW)]], out_vmem)

        pltpu.emit_pipeline(
            body,
            grid=(n // SC_WINDOW,),
            in_specs=[pl.BlockSpec((1, LANES), lambda i: (i, 0))],
            out_specs=[pl.BlockSpec((SC_WINDOW, d), lambda i: (i, 0))],
            core_axis_name=("core", "subcore"),
            dimension_semantics=(pltpu.PARALLEL,),
        )(idx_hbm, out_hbm)

    idx_rows = jnp.pad(idx.reshape(n // SC_WINDOW, SC_WINDOW), ((0, 0), (0, LANES - SC_WINDOW)))
    return gather(src, idx_rows)


def _invert_rows(dest, n_rows):
    n = dest.shape[0]
    n_tok = n // 2
    mesh = plsc.VectorSubcoreMesh(core_axis_name="core", subcore_axis_name="subcore")

    @pl.kernel(out_type=jax.ShapeDtypeStruct((n_rows,), jnp.int32), mesh=mesh, name="moe_row_src",
               scratch_types=[pltpu.VMEM((n,), jnp.int32), pltpu.VMEM((n_rows,), jnp.int32)],
               compiler_params=pltpu.CompilerParams(needs_layout_passes=False))
    def invert(dest_hbm, out_hbm, dest_vmem, rows_vmem):
        @pl.when((lax.axis_index("core") == 0) & (lax.axis_index("subcore") == 0))
        def _():
            pltpu.sync_copy(dest_hbm, dest_vmem)

            @pl.loop(0, n_rows, step=SC_LANES)
            def _(r):
                rows_vmem[pl.ds(r, SC_LANES)] = lax.rem(lax.iota(jnp.int32, SC_LANES) + r, n_tok)

            @pl.loop(0, n, step=SC_LANES)
            def _(a):
                tok = lax.iota(jnp.int32, SC_LANES) + a
                tok = jnp.where(tok >= n_tok, tok - n_tok, tok)
                plsc.store_scatter(rows_vmem, [dest_vmem[pl.ds(a, SC_LANES)]], tok)

            pltpu.sync_copy(rows_vmem, out_hbm)

    return invert(dest)


def _moe_kernel(te_ref, nu_ref, x_ref, wg_ref, wu_ref, wd_ref, *rest, tile0, nj):
    o_ref, xb, acc = rest[-3:]
    t = pl.program_id(0) + tile0
    j = pl.program_id(1)
    used = t < nu_ref[0]

    def partial_sum():
        return _swiglu_chunks(xb[...], wg_ref, wu_ref, wd_ref, MOE_SUB)

    @pl.when(used & (j == 0))
    def _():
        lo, hi = _unpack_bf16_pairs(x_ref[...])
        half = lo.shape[1]
        xb[:, 0:half] = lo.astype(BF16)
        xb[:, half:] = hi.astype(BF16)
        if nj == 1:
            o_ref[...] = _pack_bf16_pairs(partial_sum())
        else:
            acc[...] = partial_sum()

    if nj > 2:
        @pl.when(used & (j > 0) & (j < nj - 1))
        def _():
            acc[...] += partial_sum()

    if nj > 1:
        @pl.when(used & (j == nj - 1))
        def _():
            o_ref[...] = _pack_bf16_pairs(acc[...] + partial_sum())

    @pl.when(jnp.logical_not(used) & (j == nj - 1))
    def _():
        o_ref[...] = jnp.zeros_like(o_ref)


def _moe_grouped(tile_expert, n_used, xs, wg, wu, wd, y_prev, tile0, n_rows):
    p, dp = xs.shape
    d = 2 * dp
    f = wg.shape[2]
    tm = MOE_TM
    tf = MOE_TF
    nj = f // tf

    def jj(t, j, te, nu):
        return jnp.where(t + tile0 < nu[0], j, nj - 1)

    in_specs = [pl.BlockSpec((tm, dp), lambda t, j, te, nu: (t, 0)),
                pl.BlockSpec((None, d, tf), lambda t, j, te, nu: (te[t + tile0], 0, jj(t, j, te, nu))),
                pl.BlockSpec((None, d, tf), lambda t, j, te, nu: (te[t + tile0], 0, jj(t, j, te, nu))),
                pl.BlockSpec((None, tf, d), lambda t, j, te, nu: (te[t + tile0], jj(t, j, te, nu), 0))]
    args = [tile_expert, n_used, xs, wg, wu, wd]
    aliases = {}
    if y_prev is not None:
        in_specs.append(pl.BlockSpec(memory_space=pl.ANY))
        args.append(y_prev)
        aliases = {len(args) - 1: 0}
    return pl.pallas_call(
        functools.partial(_moe_kernel, tile0=tile0, nj=nj),
        out_shape=jax.ShapeDtypeStruct((n_rows, dp), jnp.int32),
        grid_spec=pltpu.PrefetchScalarGridSpec(
            num_scalar_prefetch=2,
            grid=(p // tm, nj),
            in_specs=in_specs,
            out_specs=pl.BlockSpec((tm, dp), lambda t, j, te, nu: (t + tile0, 0)),
            scratch_shapes=[pltpu.VMEM((tm, d), BF16), pltpu.VMEM((tm, d), F32)]),
        input_output_aliases=aliases,
        compiler_params=_params("arbitrary", "arbitrary"),
        name="moe_grouped",
    )(*args)


def _combine_kernel(y1_ref, y2_ref, x_ref, g2_ref, rt_ref, o_ref):
    rt = rt_ref[...]
    w1, w2 = rt[:, 2:3], rt[:, 3:4]
    half = y1_ref.shape[1]
    for k, (a, c) in enumerate(zip(_unpack_bf16_pairs(y1_ref[...]), _unpack_bf16_pairs(y2_ref[...]))):
        cs = slice(k * half, (k + 1) * half)
        o_ref[:, cs] = x_ref[:, cs] + g2_ref[:, cs] * (w1 * a + w2 * c)


def _moe_combine(yg, x, g2, route):
    b, s, d = x.shape
    tm = _pick_tile(s, 512)
    tok = lambda n: pl.BlockSpec((None, tm, n), lambda bi, i: (bi, i, 0))
    return pl.pallas_call(
        _combine_kernel,
        out_shape=jax.ShapeDtypeStruct((b, s, d), F32),
        grid=(b, s // tm),
        in_specs=[tok(d // 2), pl.BlockSpec((None, tm, d // 2), lambda bi, i: (b + bi, i, 0)), tok(d),
                  pl.BlockSpec((None, 1, d), lambda bi, i: (bi, 0, 0)), tok(LANES)],
        out_specs=tok(d),
        compiler_params=_params("parallel", "parallel"),
        name="moe_combine",
    )(yg, yg, x, g2, route)


def _moe(h, route, x, g2, wg, wu, wd):
    b, s, d = x.shape
    n_tok = b * s
    tm = MOE_TM
    rt = route.reshape(n_tok, LANES)
    flat_e = jnp.concatenate([rt[:, 0], rt[:, 1]]).astype(jnp.int32)
    onehot = (flat_e[:, None] == jnp.arange(N_EXPERTS, dtype=jnp.int32)[None, :]).astype(jnp.int32)
    csum = jnp.cumsum(onehot, axis=0)
    rank = jnp.sum(csum * onehot, axis=1) - 1
    counts = csum[-1]
    tiles_e = (counts + tm - 1) // tm
    tiles_cum = jnp.cumsum(tiles_e)
    row_start = (tiles_cum - tiles_e) * tm
    dest = jnp.sum(onehot * row_start[None, :], axis=1) + rank
    n_tiles = 2 * n_tok // tm + N_EXPERTS
    tile_ids = jnp.arange(n_tiles, dtype=jnp.int32)
    tile_expert = jnp.sum((tile_ids[:, None] >= tiles_cum[None, :]).astype(jnp.int32), axis=1)
    last_e = jnp.max(jnp.where(tiles_e > 0, jnp.arange(N_EXPERTS, dtype=jnp.int32), 0))
    tile_expert = jnp.minimum(tile_expert, last_e).astype(jnp.int32)
    n_used = tiles_cum[-1:].astype(jnp.int32)
    dest = dest.astype(jnp.int32)
    row_src = _invert_rows(dest, n_tiles * tm)
    n_chunks = max(c for c in range(1, MOE_CHUNKS + 1) if n_tiles % c == 0)
    tiles_c = n_tiles // n_chunks
    h_flat = h.reshape(n_tok, d // 2)
    xs = [_gather_rows(row_src[c * tiles_c * tm:(c + 1) * tiles_c * tm], h_flat) for c in range(n_chunks)]
    y = None
    for c in range(n_chunks):
        y = _moe_grouped(tile_expert, n_used, xs[c], wg, wu, wd, y, c * tiles_c, n_tiles * tm)
    yg = _gather_rows(dest, y).reshape(2 * b, s, d // 2)
    return _moe_combine(yg, x, g2, route)


def _rope_tables(n):
    rows = n // GRID_W
    r = jnp.repeat(jnp.arange(rows), GRID_W).astype(F32)
    col = jnp.tile(jnp.arange(GRID_W), rows).astype(F32)
    freqs = ROPE_BASE ** (-jnp.arange(ROPE_FREQS, dtype=F32) / ROPE_FREQS)
    ang = jnp.stack([r[:, None] * freqs, col[:, None] * freqs], axis=1)
    ang = jnp.repeat(ang[:, :, None, :], 2, axis=2).reshape(n, HEAD_DIM)
    ang = jnp.tile(ang, (1, LANES // HEAD_DIM))
    cos, sin = jnp.cos(ang), jnp.sin(ang)
    first_half = (jnp.arange(LANES) % (2 * ROPE_FREQS)) < ROPE_FREQS
    return cos, jnp.where(first_half, -sin, 0.0), jnp.where(first_half, 0.0, sin)


def _lane_rows(lg):
    return jnp.repeat(lg.astype(F32), HEAD_DIM).reshape(RET_HEADS // 2, 1, LANES)


def kernel(x, c, ctx, c_ctx, ada_w, ada_b, norm1_w, norm2_w, w_in, w_out, q_norm_w, k_norm_w,
           attn_sink, ret_decay_f, ret_decay_b, ret_gn_w, conv_w, conv_b, conv_ln_w, conv_ln_b,
           ffn_w_gate, ffn_w_up, ffn_w_down, router_w, moe_w_gate, moe_w_up, moe_w_down):
    b, n, d = x.shape
    n_ctx = ctx.shape[1]
    depth = ada_w.shape[0]
    cond = jnp.zeros((SUBLANES, d), F32).at[0:b].set(c).at[b].set(c_ctx)
    mods = _adaln(cond, ada_w, ada_b).reshape(depth, SUBLANES, 6, d)
    cos, sa, sb = _rope_tables(n)
    ones_c = jnp.ones((n_ctx, LANES), F32)
    zeros_c = jnp.zeros((n_ctx, LANES), F32)
    zero_state = jnp.zeros((b, RET_HEADS // 2, LANES, LANES), F32)
    row = lambda v: v.reshape(1, -1)
    moe_f32 = [w.reshape(-1, w.shape[-1]) for w in (moe_w_gate, moe_w_up, moe_w_down)]
    cast_jobs = {l: [j for j in range(3) if j * min(depth, 2) // 3 == l] for l in range(depth)}
    moe_bf = [None, None, None]
    for l in range(depth):
        last = l == depth - 1
        m_lat = [mods[l, 0:b, k][:, None, :] for k in range(6)]
        m_ctx = [jnp.broadcast_to(mods[l, b, k][None, None, :], (b, 1, d)) for k in range(6)]
        w_in_bf = w_in[l].astype(BF16)
        w_out_bf = w_out[l].astype(BF16)
        qw = row(jnp.tile(q_norm_w[l], LANES // HEAD_DIM))
        kw = row(jnp.tile(k_norm_w[l], LANES // HEAD_DIM))
        lgf = _lane_rows(jax.nn.log_sigmoid(ret_decay_f[l].astype(F32)))
        lgb = _lane_rows(jax.nn.log_sigmoid(ret_decay_b[l].astype(F32)))
        sink_tab = jnp.repeat(attn_sink[l].astype(F32), ATT_BLOCK).reshape(ATT_Q_HEADS // 2, 2 * ATT_BLOCK)

        q, k, v, rk, rv, rq, rg, cv = _inproj(x, row(norm1_w[l]), m_lat[1], m_lat[0], w_in_bf,
                                               cos, sa, sb, qw, kw)
        qc, kc, vc, rkc, rvc, rqc, rgc, cvc = _inproj(ctx, row(norm1_w[l]), m_ctx[1], m_ctx[0], w_in_bf,
                                                       ones_c, zeros_c, zeros_c, qw, kw)
        rf_c, s_f = _ret_states(lgf, rkc, rvc, zero_state, reverse=False)
        rb_c, s_b = _ret_states(lgb, rkc, rvc, zero_state, reverse=True)
        rf, _ = _ret_states(lgf, rk, rv, s_f, reverse=False)
        rb, _ = _ret_states(lgb, rk, rv, s_b, reverse=True)

        att, cast_out = _attention(q, k, v, kc, vc, sink_tab, window=True,
                                   cast=[moe_f32[j] for j in cast_jobs[l]])
        for j, w_bf in zip(cast_jobs[l], cast_out):
            moe_bf[j] = w_bf.reshape((moe_w_gate, moe_w_up, moe_w_down)[j].shape)
        ret = _ret_out(lgf, lgb, rq, rk, rv, rg, rf, rb, row(ret_gn_w[l]))
        cnv = _conv(cv, conv_w[l], row(conv_b[l]), row(conv_ln_w[l]), row(conv_ln_b[l]))

        if l % 2 == 0:
            i = l // 2
            wg, wu, wd = ffn_w_gate[i].astype(BF16), ffn_w_up[i].astype(BF16), ffn_w_down[i].astype(BF16)
            x_mid, h2 = _outproj(att, ret, cnv, w_out_bf, x, m_lat[2], row(norm2_w[l]), m_lat[4], m_lat[3])
            x_new = _ffn(h2, wg, wu, wd, x_mid, m_lat[5])
        else:
            i = l // 2
            wg, wu, wd = moe_bf[0][i], moe_bf[1][i], moe_bf[2][i]
            x_mid, h2, route = _outproj(att, ret, cnv, w_out_bf, x, m_lat[2], row(norm2_w[l]),
                                        m_lat[4], m_lat[3], router=router_w[i])
            x_new = _moe(h2, route, x_mid, m_lat[5], wg, wu, wd)

        if not last:
            att_c, _ = _attention(qc, None, None, kc, vc, sink_tab, window=False)
            ret_c = _ret_out(lgf, lgb, rqc, rkc, rvc, rgc, rf_c, rb_c, row(ret_gn_w[l]))
            cnv_c = _conv(cvc, conv_w[l], row(conv_b[l]), row(conv_ln_w[l]), row(conv_ln_b[l]))
            if l % 2 == 0:
                c_mid, h2c = _outproj(att_c, ret_c, cnv_c, w_out_bf, ctx, m_ctx[2], row(norm2_w[l]),
                                      m_ctx[4], m_ctx[3])
                ctx = _ffn(h2c, wg, wu, wd, c_mid, m_ctx[5])
            else:
                c_mid, h2c, route_c = _outproj(att_c, ret_c, cnv_c, w_out_bf, ctx, m_ctx[2],
                                               row(norm2_w[l]), m_ctx[4], m_ctx[3],
                                               router=router_w[i])
                ctx = _moe(h2c, route_c, c_mid, m_ctx[5], wg, wu, wd)
        x = x_new
    return x
```

```python
import functools

import jax
import jax.numpy as jnp
from jax import lax
from jax.experimental import pallas as pl
from jax.experimental.pallas import tpu as pltpu
from jax.experimental.pallas import tpu_sc as plsc

F32 = jnp.float32
BF16 = jnp.bfloat16

GRID_W = 64
HEAD_DIM = 64
ATT_Q_HEADS = 8
ATT_KV_HEADS = 2
ATT_WINDOW = 128
ATT_BLOCK = 128
RET_HEADS = 4
RET_CHUNK = 128
RET_K_SCALE = HEAD_DIM ** -0.5
ATT_SCALE = HEAD_DIM ** -0.5
CONV_CH = 256
CONV_WIDTH = 31
ROPE_BASE = 10000.0
ROPE_FREQS = HEAD_DIM // 4
D_ATT = ATT_Q_HEADS * HEAD_DIM
D_RET = RET_HEADS * HEAD_DIM
ATT_KV_W = ATT_KV_HEADS * HEAD_DIM
C_ATT_K = 0
C_ATT_V = C_ATT_K + ATT_KV_W
C_RET_K = C_ATT_V + ATT_KV_W
C_RET_V = C_RET_K + D_RET
C_ATT_Q = C_RET_V + D_RET
C_RET_Q = C_ATT_Q + D_ATT
C_RET_G = C_RET_Q + D_RET
C_CONV = C_RET_G + D_RET
N_EXPERTS = 8
EPS = 1e-6
NEG_INF = -1e30

LANES = 128
SUBLANES = 8
VMEM_LIMIT = 48 * 1024 * 1024
CONV_HALO = 16
CONV_SUB = 32
CONV_CHAINS = 4
ATT_QBLOCKS = 4
RET_UNROLL = 4
MOE_TM = 512
MOE_CHUNKS = 2
MOE_SUB = 512
FFN_SUB = 512
MOE_TF = 1792
SC_LANES = 16
SC_WINDOW = 64


def _params(*sem):
    return pltpu.CompilerParams(dimension_semantics=sem, vmem_limit_bytes=VMEM_LIMIT)


def _sigmoid(x):
    return 1.0 / (1.0 + jnp.exp(-x))


def _pack_bf16_pairs(v):
    c = v.shape[1] // 2
    bits = pltpu.bitcast(v.astype(BF16).astype(F32), jnp.uint32)
    packed = (bits[:, c:] & jnp.uint32(0xFFFF0000)) | (bits[:, :c] >> 16)
    return pltpu.bitcast(packed, jnp.int32)


def _unpack_bf16_pairs(p):
    bits = pltpu.bitcast(p, jnp.uint32)
    return pltpu.bitcast(bits << 16, F32), pltpu.bitcast(bits & jnp.uint32(0xFFFF0000), F32)


def _pick_tile(n, pref):
    t = min(n, pref)
    assert n % t == 0, (n, t)
    return t


def _adaln_kernel(c_ref, w_ref, b_ref, o_ref):
    c = c_ref[...]
    s = c * _sigmoid(c)
    o_ref[...] = jnp.dot(s, w_ref[...], preferred_element_type=F32,
                         precision=lax.Precision.HIGHEST) + b_ref[...]


def _adaln(cond, ada_w, ada_b):
    depth, d, n = ada_w.shape
    tn = _pick_tile(n, 1536)
    return pl.pallas_call(
        _adaln_kernel,
        out_shape=jax.ShapeDtypeStruct((depth, cond.shape[0], n), F32),
        grid=(depth, n // tn),
        in_specs=[pl.BlockSpec(cond.shape, lambda l, j: (0, 0)),
                  pl.BlockSpec((None, d, tn), lambda l, j: (l, 0, j)),
                  pl.BlockSpec((None, 1, tn), lambda l, j: (l, 0, j))],
        out_specs=pl.BlockSpec((None, cond.shape[0], tn), lambda l, j: (l, 0, j)),
        compiler_params=_params("parallel", "parallel"),
        name="adaln",
    )(cond, ada_w, ada_b.reshape(depth, 1, n))


def _modulated_rms(x, nw, sc, sh):
    ms = jnp.mean(x * x, axis=-1, keepdims=True)
    return (x * lax.rsqrt(ms + EPS) * nw) * (1.0 + sc) + sh


def _head_group_matrix():
    r = lax.broadcasted_iota(jnp.int32, (2 * LANES, 2 * LANES), 0) // HEAD_DIM
    c = lax.broadcasted_iota(jnp.int32, (2 * LANES, 2 * LANES), 1) // HEAD_DIM
    return jnp.where(r == c, 1.0, 0.0).astype(BF16)


def _per_head_sum(v, grp):
    hi = v.astype(BF16)
    lo = (v - hi.astype(F32)).astype(BF16)
    r = jnp.dot(jnp.concatenate([hi, lo], axis=1), grp, preferred_element_type=F32)
    return r[:, 0:LANES] + r[:, LANES:]


def _dup_halves(t):
    sw = pltpu.roll(t, HEAD_DIM, 1)
    lo = lax.broadcasted_iota(jnp.int32, t.shape, 1) < HEAD_DIM
    return jnp.where(lo, t, sw), jnp.where(lo, sw, t)


def _inproj_kernel(x_ref, nw_ref, sc_ref, sh_ref, w_ref, cos_ref, sa_ref, sb_ref, qw_ref, kw_ref,
                   q_ref, k_ref, v_ref, rk_ref, rv_ref, rq_ref, rg_ref, cv_ref):
    hb = _modulated_rms(x_ref[...], nw_ref[...], sc_ref[...], sh_ref[...]).astype(BF16)

    def proj(c0, n):
        return jnp.dot(hb, w_ref[:, c0:c0 + n], preferred_element_type=F32)

    grp = _head_group_matrix()
    cos, sa, sb = cos_ref[...], sa_ref[...], sb_ref[...]

    def norm_rope(p, wrow):
        y = p * lax.rsqrt(_per_head_sum(p * p, grp) * (1.0 / HEAD_DIM) + EPS) * wrow
        return (y * cos + pltpu.roll(y, LANES - ROPE_FREQS, 1) * sa
                + pltpu.roll(y, ROPE_FREQS, 1) * sb)

    def put_kv(kv):
        k0, k1 = _dup_halves(norm_rope(kv[:, 0:ATT_KV_W], kw_ref[...]))
        k_ref[:, 0:LANES] = k0.astype(BF16)
        k_ref[:, LANES:2 * LANES] = k1.astype(BF16)
        v_ref[...] = kv[:, ATT_KV_W:].T.astype(BF16)

    def put_q(qall):
        qw = qw_ref[...] * ATT_SCALE
        for j in range(D_ATT // LANES):
            q_ref[:, j * LANES:(j + 1) * LANES] = norm_rope(qall[:, j * LANES:(j + 1) * LANES], qw).astype(BF16)

    def put_rk(p):
        rk_ref[...] = (p * RET_K_SCALE).astype(BF16)

    def put_rv(p):
        rv_ref[...] = p.astype(BF16)

    def put_rq(p):
        rq_ref[...] = p.astype(BF16)

    def put_rg(p):
        rg_ref[...] = p

    def put_conv(p):
        cv_ref[...] = p[:, 0:CONV_CH] * _sigmoid(p[:, CONV_CH:])

    stages = [(C_ATT_K, 2 * ATT_KV_W, put_kv), (C_ATT_Q, D_ATT, put_q), (C_RET_K, D_RET, put_rk),
              (C_RET_V, D_RET, put_rv), (C_RET_Q, D_RET, put_rq), (C_RET_G, D_RET, put_rg),
              (C_CONV, 2 * CONV_CH, put_conv)]
    p_next = proj(*stages[0][:2])
    for n, (_, _, put) in enumerate(stages):
        p_cur = p_next
        if n + 1 < len(stages):
            p_next = proj(*stages[n + 1][:2])
        put(p_cur)


def _inproj(x, nw, sc, sh, w_bf, cos, sa, sb, qw, kw):
    b, s, d = x.shape
    tm = _pick_tile(s, 512)
    row = lambda n: pl.BlockSpec((1, n), lambda bi, i: (0, 0))
    per_b = pl.BlockSpec((None, 1, d), lambda bi, i: (bi, 0, 0))
    tab = pl.BlockSpec((tm, LANES), lambda bi, i: (i, 0))
    tok = lambda n: pl.BlockSpec((None, tm, n), lambda bi, i: (bi, i, 0))
    shp = lambda n, dt: jax.ShapeDtypeStruct((b, s, n), dt)
    return pl.pallas_call(
        _inproj_kernel,
        out_shape=(shp(D_ATT, BF16), shp(2 * LANES, BF16), jax.ShapeDtypeStruct((b, ATT_KV_W, s), BF16),
                   shp(D_RET, BF16), shp(D_RET, BF16), shp(D_RET, BF16), shp(D_RET, F32),
                   shp(CONV_CH, F32)),
        grid=(b, s // tm),
        in_specs=[tok(d), row(d), per_b, per_b,
                  pl.BlockSpec(w_bf.shape, lambda bi, i: (0, 0)),
                  tab, tab, tab, row(LANES), row(LANES)],
        out_specs=(tok(D_ATT), tok(2 * LANES), pl.BlockSpec((None, ATT_KV_W, tm), lambda bi, i: (bi, 0, i)),
                   tok(D_RET), tok(D_RET), tok(D_RET), tok(D_RET), tok(CONV_CH)),
        compiler_params=_params("parallel", "parallel"),
        name="inproj",
    )(x, nw, sc, sh, w_bf, cos, sa, sb, qw, kw)


def _attn_kernel(*refs, window, nq, n_cast):
    blk = ATT_BLOCK
    n_in = len(refs) - 1 - n_cast
    for src, dst in zip(refs[n_in - n_cast:n_in], refs[n_in + 1:]):
        dst[...] = src[...].astype(BF16)
    refs = refs[:n_in - n_cast] + refs[n_in:n_in + 1]
    if window:
        q_ref = refs[0]
        k_refs = refs[1:nq + 3]
        v_refs = refs[nq + 3:2 * nq + 5]
        kx_ref, vx_ref, sink_ref, o_ref = refs[2 * nq + 5:]
    else:
        q_ref, kx_ref, vx_ref, sink_ref, o_ref = refs
    n_ctx = kx_ref.shape[0]
    nk = 3 * blk + n_ctx if window else n_ctx
    if window:
        i = pl.program_id(1)
        last = nq * pl.num_programs(1) - 1
        key = lax.broadcasted_iota(jnp.int32, (blk, 2 * blk), 0)
        qry = lax.broadcasted_iota(jnp.int32, (blk, 2 * blk), 1) & (blk - 1)

        def band_masks(sub):
            off_prev = jnp.where(nq * i + sub > 0, 0, blk)
            off_next = jnp.where(nq * i + sub < last, 0, blk)
            return key >= qry + off_prev, key + off_next <= qry
    first_head = lax.broadcasted_iota(jnp.int32, (blk, LANES), 1) < HEAD_DIM
    ones = jnp.ones((2 * SUBLANES, nk), BF16)

    def scores(sub, pair):
        g = pair // 2
        gs = slice(g * LANES, (g + 1) * LANES)
        if window:
            kcat = jnp.concatenate([r[:, gs] for r in k_refs[sub:sub + 3]] + [kx_ref[:, gs]], axis=0)
        else:
            kcat = kx_ref[:, gs]
        qp = q_ref[sub * blk:(sub + 1) * blk, pair * LANES:(pair + 1) * LANES]
        zero = jnp.zeros_like(qp)
        w = jnp.concatenate([jnp.where(first_head, qp, zero), jnp.where(first_head, zero, qp)], axis=0)
        return lax.dot_general(kcat, w, (((1,), (1,)), ((), ())), preferred_element_type=F32)

    def masked_max(sub, pair, s):
        if window:
            mask_prev, mask_next = band_masks(sub)
            parts = [jnp.where(mask_prev, s[0:blk], NEG_INF), s[blk:2 * blk],
                     jnp.where(mask_next, s[2 * blk:3 * blk], NEG_INF), s[3 * blk:]]
        else:
            parts = [s]
        m = sink_ref[pair:pair + 1, :]
        for part in parts:
            m = jnp.maximum(m, jnp.max(part, axis=0, keepdims=True))
        return parts, m

    def softmax(sub, pair, parts_m):
        parts, m = parts_m
        p = jnp.concatenate([jnp.exp(part - m).astype(BF16) for part in parts], axis=0)
        return p, jnp.exp(sink_ref[pair:pair + 1, :] - m)

    def output(sub, pair, p, sink_p):
        g = pair // 2
        vs = slice(g * HEAD_DIM, (g + 1) * HEAD_DIM)
        if window:
            vt = jnp.concatenate([r[vs, :] for r in v_refs[sub:sub + 3]] + [vx_ref[vs, :]], axis=1)
        else:
            vt = vx_ref[vs, :]
        vaug = jnp.concatenate([vt, ones], axis=0)
        o = jnp.dot(vaug, p, preferred_element_type=F32)
        on = o[0:HEAD_DIM, :] * (1.0 / (o[HEAD_DIM:HEAD_DIM + 1, :] + sink_p))
        ot = jnp.concatenate([on[:, 0:blk], on[:, blk:2 * blk]], axis=0)
        o_ref[sub * blk:(sub + 1) * blk, pair * LANES:(pair + 1) * LANES] = ot.T.astype(BF16)

    units = [(sub, pair) for sub in range(nq) for pair in range(ATT_Q_HEADS // 2)]
    stages = (lambda sub, pair, _: scores(sub, pair), masked_max, softmax,
              lambda sub, pair, ps: output(sub, pair, *ps))
    vals = {}
    for step in range(len(units) + len(stages) - 1):
        for k, stage in enumerate(stages):
            u = step - k
            if 0 <= u < len(units):
                vals[u, k] = stage(*units[u], vals.pop((u, k - 1), None))


def _attention(q, k, vt, kx, vxt, sink_tab, window, cast=()):
    b, s, _ = q.shape
    blk = ATT_BLOCK
    nb = s // blk
    nq = ATT_QBLOCKS if nb % ATT_QBLOCKS == 0 else 1
    n_ctx = kx.shape[1]
    qspec = pl.BlockSpec((None, nq * blk, D_ATT), lambda bi, i: (bi, i, 0))
    kctx = pl.BlockSpec((None, n_ctx, 2 * LANES), lambda bi, i: (bi, 0, 0))
    vctx = pl.BlockSpec((None, ATT_KV_W, n_ctx), lambda bi, i: (bi, 0, 0))
    snk = pl.BlockSpec(sink_tab.shape, lambda bi, i: (0, 0))
    if window:
        at = lambda off: (lambda i: jnp.clip(nq * i + off, 0, nb - 1))
        kspec = lambda f: pl.BlockSpec((None, blk, 2 * LANES), lambda bi, i: (bi, f(i), 0))
        vspec = lambda f: pl.BlockSpec((None, ATT_KV_W, blk), lambda bi, i: (bi, 0, f(i)))
        offs = range(-1, nq + 1)
        in_specs = ([qspec] + [kspec(at(o)) for o in offs] + [vspec(at(o)) for o in offs]
                    + [kctx, vctx, snk])
        args = (q,) + (k,) * (nq + 2) + (vt,) * (nq + 2) + (kx, vxt, sink_tab)
    else:
        in_specs = [qspec, kctx, vctx, snk]
        args = (q, kx, vxt, sink_tab)
    steps = nb // nq
    out_shape, out_specs = [jax.ShapeDtypeStruct((b, s, D_ATT), BF16)], [qspec]
    for w in cast:
        rows = w.shape[0] // (b * steps)
        assert rows * b * steps == w.shape[0] and rows % (2 * SUBLANES) == 0, w.shape
        slab = pl.BlockSpec((rows, w.shape[1]), lambda bi, i: (bi * steps + i, 0))
        in_specs.append(slab)
        out_specs.append(slab)
        out_shape.append(jax.ShapeDtypeStruct(w.shape, BF16))
    res = pl.pallas_call(
        functools.partial(_attn_kernel, window=window, nq=nq, n_cast=len(cast)),
        out_shape=tuple(out_shape),
        grid=(b, steps),
        in_specs=in_specs,
        out_specs=tuple(out_specs),
        compiler_params=_params("parallel", "parallel"),
        name="attention_window" if window else "attention_ctx",
    )(*args, *cast)
    return res[0], tuple(res[1:])


def _ret_state_kernel(lgl_ref, k_ref, v_ref, s0_ref, r_ref, fin_ref, s_scr, *, cb, reverse):
    c = pl.program_id(1)
    L = RET_CHUNK

    @pl.when(c == 0)
    def _():
        s_scr[...] = s0_ref[...]

    jj = lax.broadcasted_iota(jnp.int32, (L, LANES), 0).astype(F32)
    expo = jj if reverse else (L - 1.0) - jj
    same_head = (lax.broadcasted_iota(jnp.int32, (LANES, LANES), 0) // HEAD_DIM
                 == lax.broadcasted_iota(jnp.int32, (LANES, LANES), 1) // HEAD_DIM)
    for pr in range(RET_HEADS // 2):
        cs = slice(pr * LANES, (pr + 1) * LANES)
        lgl = lgl_ref[pr]
        kdec = jnp.exp(expo * lgl)
        cdec = jnp.exp(float(L) * lgl)
        state = s_scr[pr]
        for t in range(cb):
            cc = cb - 1 - t if reverse else t
            rows = slice(cc * L, (cc + 1) * L)
            r_ref[cc, pr] = state
            kd = k_ref[rows, cs].astype(F32) * kdec
            u = jnp.dot(kd.T.astype(BF16), v_ref[rows, cs], preferred_element_type=F32)
            state = cdec * state + jnp.where(same_head, u, 0.0)
        s_scr[pr] = state

    @pl.when(c == pl.num_programs(1) - 1)
    def _():
        fin_ref[...] = s_scr[...]


def _ret_states(lgl, rk, rv, s0, reverse):
    b, s, _ = rk.shape
    nc = s // RET_CHUNK
    cb = _pick_tile(nc, 8)
    nblk = nc // cb
    npair = RET_HEADS // 2
    blk_idx = (lambda c: nblk - 1 - c) if reverse else (lambda c: c)
    tok = pl.BlockSpec((None, cb * RET_CHUNK, D_RET), lambda bi, c: (bi, blk_idx(c), 0))
    st = pl.BlockSpec((None, npair, LANES, LANES), lambda bi, c: (bi, 0, 0, 0))
    return pl.pallas_call(
        functools.partial(_ret_state_kernel, cb=cb, reverse=reverse),
        out_shape=(jax.ShapeDtypeStruct((b, nc, npair, LANES, LANES), F32),
                   jax.ShapeDtypeStruct((b, npair, LANES, LANES), F32)),
        grid=(b, nblk),
        in_specs=[pl.BlockSpec(lgl.shape, lambda bi, c: (0, 0, 0)), tok, tok, st],
        out_specs=(pl.BlockSpec((None, cb, npair, LANES, LANES),
                                lambda bi, c: (bi, blk_idx(c), 0, 0, 0)), st),
        scratch_shapes=[pltpu.VMEM((npair, LANES, LANES), F32)],
        compiler_params=_params("parallel", "arbitrary"),
        name="ret_state_bwd" if reverse else "ret_state_fwd",
    )(lgl, rk, rv, s0)


def _ret_out_kernel(lgf_ref, lgb_ref, q_ref, k_ref, v_ref, g_ref, rf_ref, rb_ref, gnw_ref, o_ref, *, cb):
    L = RET_CHUNK
    i0 = lax.broadcasted_iota(jnp.int32, (L, LANES), 0).astype(F32)
    i1 = lax.broadcasted_iota(jnp.int32, (L, LANES), 1).astype(F32)
    diff = i0 - i1
    lo = lax.broadcasted_iota(jnp.int32, (L, LANES), 1) < HEAD_DIM
    grp = _head_group_matrix()
    inv = 1.0 / HEAD_DIM
    tabs = []
    for pr in range(RET_HEADS // 2):
        lgf, lgb = lgf_ref[pr], lgb_ref[pr]
        dmat = [jnp.where(diff >= 0.0,
                          jnp.exp(jnp.maximum(diff, 0.0) * lgf[:, a:a + 1]),
                          jnp.exp(jnp.maximum(-diff, 0.0) * lgb[:, a:a + 1]))
                for a in (0, HEAD_DIM)]
        tabs.append((jnp.exp((i0 + 1.0) * lgf), jnp.exp((float(L) - i0) * lgb),
                     jnp.concatenate(dmat, axis=0), gnw_ref[:, pr * LANES:(pr + 1) * LANES]))

    def decayed_scores(t, pr):
        rows = pl.ds(pl.multiple_of(t * L, L), L)
        cs = slice(pr * LANES, (pr + 1) * LANES)
        qp, kp = q_ref[rows, cs], k_ref[rows, cs]
        zero = jnp.zeros_like(kp)
        qz = jnp.concatenate([jnp.where(lo, qp, zero), jnp.where(lo, zero, qp)], axis=0)
        return lax.dot_general(qz, kp, (((1,), (1,)), ((), ())), preferred_element_type=F32) * tabs[pr][2]

    def mix(t, pr, a):
        rows = pl.ds(pl.multiple_of(t * L, L), L)
        cs = slice(pr * LANES, (pr + 1) * LANES)
        xif, xib, _, _ = tabs[pr]
        vp = v_ref[rows, cs]
        zero = jnp.zeros_like(vp)
        qf = q_ref[rows, cs].astype(F32)
        lhs = jnp.concatenate([a[0:L].astype(BF16), a[L:].astype(BF16),
                               (qf * xif).astype(BF16), (qf * xib).astype(BF16)], axis=1)
        rhs = jnp.concatenate([jnp.where(lo, vp, zero), jnp.where(lo, zero, vp),
                               rf_ref[t, pr].astype(BF16), rb_ref[t, pr].astype(BF16)], axis=0)
        return jnp.dot(lhs, rhs, preferred_element_type=F32)

    def centre(t, pr, acc):
        return acc - _per_head_sum(acc, grp) * inv

    def readout(t, pr, dl):
        rows = pl.ds(pl.multiple_of(t * L, L), L)
        cs = slice(pr * LANES, (pr + 1) * LANES)
        var = _per_head_sum(dl * dl, grp) * inv
        y = dl * lax.rsqrt(var + EPS) * tabs[pr][3]
        gt = g_ref[rows, cs]
        o_ref[rows, cs] = (gt * _sigmoid(gt) * y).astype(BF16)

    unroll = min(RET_UNROLL, cb)
    stages = (lambda t, pr, _: decayed_scores(t, pr), mix, centre, readout)

    def group(gi, carry):
        units = [(gi * unroll + u, pr) for u in range(unroll) for pr in range(RET_HEADS // 2)]
        vals = {}
        for step in range(len(units) + len(stages) - 1):
            for k, stage in enumerate(stages):
                u = step - k
                if 0 <= u < len(units):
                    vals[u, k] = stage(*units[u], vals.pop((u, k - 1), None))
        return carry

    lax.fori_loop(0, cb // unroll, group, 0)


def _ret_out(lgf, lgb, rq, rk, rv, rg, rf, rb, gnw):
    b, s, _ = rq.shape
    nc = s // RET_CHUNK
    cb = _pick_tile(nc, 8)
    npair = RET_HEADS // 2
    tok = pl.BlockSpec((None, cb * RET_CHUNK, D_RET), lambda bi, c: (bi, c, 0))
    st = pl.BlockSpec((None, cb, npair, LANES, LANES), lambda bi, c: (bi, c, 0, 0, 0))
    lg = pl.BlockSpec(lgf.shape, lambda bi, c: (0, 0, 0))
    return pl.pallas_call(
        functools.partial(_ret_out_kernel, cb=cb),
        out_shape=jax.ShapeDtypeStruct((b, s, D_RET), BF16),
        grid=(b, nc // cb),
        in_specs=[lg, lg, tok, tok, tok, tok, st, st, pl.BlockSpec((1, D_RET), lambda bi, c: (0, 0))],
        out_specs=tok,
        compiler_params=_params("parallel", "parallel"),
        name="ret_out",
    )(lgf, lgb, rq, rk, rv, rg, rf, rb, gnw)


def _conv_kernel(prev_ref, cur_ref, next_ref, w_ref, b_ref, lnw_ref, lnb_ref, o_ref,
                 xpad, shifted, hbuf, wtile):
    i = pl.program_id(1)
    tm = cur_ref.shape[0]
    halo = CONV_HALO
    first = i == 0
    last = i == pl.num_programs(1) - 1
    xpad[0:halo, :] = jnp.where(first, 0.0, prev_ref[...])
    xpad[halo:halo + tm, :] = cur_ref[...]
    xpad[halo + tm:2 * halo + tm, :] = jnp.where(last, 0.0, next_ref[...])
    span = tm + 2 * halo - SUBLANES
    for r in range(SUBLANES):
        shifted[r] = xpad[r:r + span, :]
    base = halo - CONV_WIDTH // 2
    bias = b_ref[...]
    ch = cur_ref.shape[1]
    for w in range(CONV_WIDTH):
        wtile[w] = jnp.broadcast_to(w_ref[w:w + 1, :], (SUBLANES, ch))

    taps_by_shift = {}
    for w in range(CONV_WIDTH):
        taps_by_shift.setdefault((base + w) % SUBLANES, []).append(((base + w) // SUBLANES, w))

    def sub(sb, carry):
        r0 = pl.multiple_of(sb * CONV_SUB, CONV_SUB)
        chains = [None] * CONV_CHAINS
        groups = CONV_SUB // SUBLANES
        for r, taps in sorted(taps_by_shift.items()):
            a_lo = min(a for a, _ in taps)
            n_g = max(a for a, _ in taps) - a_lo + groups
            win = shifted[r, pl.ds(r0 + a_lo * SUBLANES, n_g * SUBLANES), :].reshape(n_g, SUBLANES, ch)
            for a, w in taps:
                term = win[a - a_lo:a - a_lo + groups] * wtile[w]
                c = w % CONV_CHAINS
                chains[c] = term if chains[c] is None else chains[c] + term
        total = (chains[0] + chains[1]) + (chains[2] + chains[3])
        hbuf[pl.ds(r0, CONV_SUB), :] = total.reshape(CONV_SUB, ch) + bias
        return carry

    lax.fori_loop(0, tm // CONV_SUB, sub, 0)
    h = hbuf[...]
    mu = jnp.mean(h, axis=-1, keepdims=True)
    dl = h - mu
    var = jnp.mean(dl * dl, axis=-1, keepdims=True)
    y = dl * lax.rsqrt(var + EPS) * lnw_ref[...] + lnb_ref[...]
    o_ref[...] = (y * _sigmoid(y)).astype(BF16)


def _conv(cv, w, bias, lnw, lnb):
    b, s, ch = cv.shape
    tm = _pick_tile(s, 512)
    hpb = tm // CONV_HALO
    nh = s // CONV_HALO
    row = pl.BlockSpec((1, ch), lambda bi, i: (0, 0))
    span = tm + 2 * CONV_HALO - SUBLANES
    return pl.pallas_call(
        _conv_kernel,
        out_shape=jax.ShapeDtypeStruct((b, s, ch), BF16),
        grid=(b, s // tm),
        in_specs=[pl.BlockSpec((None, CONV_HALO, ch), lambda bi, i: (bi, jnp.maximum(i * hpb - 1, 0), 0)),
                  pl.BlockSpec((None, tm, ch), lambda bi, i: (bi, i, 0)),
                  pl.BlockSpec((None, CONV_HALO, ch),
                               lambda bi, i: (bi, jnp.minimum((i + 1) * hpb, nh - 1), 0)),
                  pl.BlockSpec(w.shape, lambda bi, i: (0, 0)), row, row, row],
        out_specs=pl.BlockSpec((None, tm, ch), lambda bi, i: (bi, i, 0)),
        scratch_shapes=[pltpu.VMEM((tm + 2 * CONV_HALO, ch), F32),
                        pltpu.VMEM((SUBLANES, span, ch), F32),
                        pltpu.VMEM((tm, ch), F32),
                        pltpu.VMEM((CONV_WIDTH, SUBLANES, ch), F32)],
        compiler_params=_params("parallel", "parallel"),
        name="conv",
    )(cv, cv, cv, w, bias, lnw, lnb)


def _outproj_kernel(*refs, route):
    if route:
        (att_ref, ret_ref, cnv_ref, w_ref, x_ref, g1_ref, nw_ref, sc_ref, sh_ref, rcat_ref,
         xo_ref, h_ref, rt_ref) = refs
    else:
        att_ref, ret_ref, cnv_ref, w_ref, x_ref, g1_ref, nw_ref, sc_ref, sh_ref, xo_ref, h_ref = refs
    y = (jnp.dot(att_ref[...], w_ref[0:D_ATT, :], preferred_element_type=F32)
         + jnp.dot(ret_ref[...], w_ref[D_ATT:D_ATT + D_RET, :], preferred_element_type=F32)
         + jnp.dot(cnv_ref[...], w_ref[D_ATT + D_RET:, :], preferred_element_type=F32))
    xn = x_ref[...] + g1_ref[...] * y
    xo_ref[...] = xn
    h = _modulated_rms(xn, nw_ref[...], sc_ref[...], sh_ref[...])
    h_ref[...] = _pack_bf16_pairs(h) if route else h.astype(BF16)
    if route:
        hi = h.astype(BF16)
        lo = (h - hi.astype(F32)).astype(BF16)
        tm = h.shape[0]
        r = jnp.dot(jnp.concatenate([hi, lo], axis=0), rcat_ref[...], preferred_element_type=F32)
        logits = (r[0:tm, 0:LANES] + r[0:tm, LANES:]) + (r[tm:, 0:LANES] + r[tm:, LANES:])
        lane = lax.broadcasted_iota(jnp.int32, logits.shape, 1).astype(F32)
        logits = jnp.where(lane < N_EXPERTS, logits, NEG_INF)
        m1 = jnp.max(logits, axis=-1, keepdims=True)
        i1 = jnp.min(jnp.where(logits == m1, lane, float(LANES)), axis=-1, keepdims=True)
        rest = jnp.where(lane == i1, NEG_INF, logits)
        m2 = jnp.max(rest, axis=-1, keepdims=True)
        i2 = jnp.min(jnp.where(rest == m2, lane, float(LANES)), axis=-1, keepdims=True)
        e2 = jnp.exp(m2 - m1)
        w1 = 1.0 / (1.0 + e2)
        w2 = e2 / (1.0 + e2)
        rt_ref[...] = jnp.where(lane == 0.0, i1,
                                jnp.where(lane == 1.0, i2,
                                          jnp.where(lane == 2.0, w1, jnp.where(lane == 3.0, w2, 0.0))))


def _outproj(att, ret, cnv, w_bf, x, g1, nw, sc, sh, router=None):
    b, s, d = x.shape
    tm = _pick_tile(s, 512)
    tok = lambda n: pl.BlockSpec((None, tm, n), lambda bi, i: (bi, i, 0))
    per_b = pl.BlockSpec((None, 1, d), lambda bi, i: (bi, 0, 0))
    row = pl.BlockSpec((1, d), lambda bi, i: (0, 0))
    in_specs = [tok(D_ATT), tok(D_RET), tok(CONV_CH), pl.BlockSpec(w_bf.shape, lambda bi, i: (0, 0)),
                tok(d), per_b, row, per_b, per_b]
    args = [att, ret, cnv, w_bf, x, g1, nw, sc, sh]
    h_shape = (b, s, d // 2) if router is not None else (b, s, d)
    out_shape = [jax.ShapeDtypeStruct((b, s, d), F32),
                 jax.ShapeDtypeStruct(h_shape, jnp.int32 if router is not None else BF16)]
    out_specs = [tok(d), tok(h_shape[2])]
    if router is not None:
        rpad = jnp.zeros((d, LANES), F32).at[:, :N_EXPERTS].set(router)
        rhi = rpad.astype(BF16)
        rlo = (rpad - rhi.astype(F32)).astype(BF16)
        in_specs += [pl.BlockSpec((d, 2 * LANES), lambda bi, i: (0, 0))]
        args += [jnp.concatenate([rhi, rlo], axis=1)]
        out_shape.append(jax.ShapeDtypeStruct((b, s, LANES), F32))
        out_specs.append(tok(LANES))
    return pl.pallas_call(
        functools.partial(_outproj_kernel, route=router is not None),
        out_shape=tuple(out_shape),
        grid=(b, s // tm),
        in_specs=in_specs,
        out_specs=tuple(out_specs),
        compiler_params=_params("parallel", "parallel"),
        name="outproj_route" if router is not None else "outproj",
    )(*args)


def _swiglu_chunks(h, wg_ref, wu_ref, wd_ref, sub):
    total = None
    for c0 in range(0, wg_ref.shape[1], sub):
        cs = slice(c0, min(c0 + sub, wg_ref.shape[1]))
        gate = jnp.dot(h, wg_ref[:, cs].astype(BF16), preferred_element_type=F32)
        up = jnp.dot(h, wu_ref[:, cs].astype(BF16), preferred_element_type=F32)
        act = (gate * _sigmoid(gate) * up).astype(BF16)
        part = jnp.dot(act, wd_ref[cs, :].astype(BF16), preferred_element_type=F32)
        total = part if total is None else total + part
    return total


def _ffn_kernel(h_ref, wg_ref, wu_ref, wd_ref, x_ref, g2_ref, o_ref):
    y = _swiglu_chunks(h_ref[...], wg_ref, wu_ref, wd_ref, FFN_SUB)
    o_ref[...] = x_ref[...] + g2_ref[...] * y


def _ffn(h, wg, wu, wd, x, g2):
    b, s, d = x.shape
    f = wg.shape[1]
    tm = _pick_tile(s, 512)
    tok = lambda: pl.BlockSpec((None, tm, d), lambda bi, i: (bi, i, 0))
    res = lambda shape: pl.BlockSpec(shape, lambda bi, i: (0, 0), pipeline_mode=pl.Buffered(1))
    return pl.pallas_call(
        _ffn_kernel,
        out_shape=jax.ShapeDtypeStruct((b, s, d), F32),
        grid=(b, s // tm),
        in_specs=[tok(), res((d, f)), res((d, f)), res((f, d)), tok(),
                  pl.BlockSpec((None, 1, d), lambda bi, i: (bi, 0, 0))],
        out_specs=tok(),
        compiler_params=_params("parallel", "parallel"),
        name="ffn_dense",
    )(h, wg, wu, wd, x, g2)


def _gather_rows(idx, src):
    n = idx.shape[0]
    d = src.shape[1]
    mesh = plsc.VectorSubcoreMesh(core_axis_name="core", subcore_axis_name="subcore")

    @pl.kernel(out_type=jax.ShapeDtypeStruct((n, d), src.dtype), mesh=mesh, name="moe_gather")
    def gather(src_hbm, idx_hbm, out_hbm):
        def body(idx_vmem, out_vmem):
            pltpu.sync_copy(src_hbm.at[idx_vmem.at[0, pl.ds(0, SC_WINDOW)]], out_vmem)

        pltpu.emit_pipeline(
            body,
            grid=(n // SC_WINDOW,),
            in_specs=[pl.BlockSpec((1, LANES), lambda i: (i, 0))],
            out_specs=[pl.BlockSpec((SC_WINDOW, d), lambda i: (i, 0))],
            core_axis_name=("core", "subcore"),
            dimension_semantics=(pltpu.PARALLEL,),
        )(idx_hbm, out_hbm)

    idx_rows = jnp.pad(idx.reshape(n // SC_WINDOW, SC_WINDOW), ((0, 0), (0, LANES - SC_WINDOW)))
    return gather(src, idx_rows)


def _invert_rows(dest, n_rows):
    n = dest.shape[0]
    n_tok = n // 2
    mesh = plsc.VectorSubcoreMesh(core_axis_name="core", subcore_axis_name="subcore")

    @pl.kernel(out_type=jax.ShapeDtypeStruct((n_rows,), jnp.int32), mesh=mesh, name="moe_row_src",
               scratch_types=[pltpu.VMEM((n,), jnp.int32), pltpu.VMEM((n_rows,), jnp.int32)],
               compiler_params=pltpu.CompilerParams(needs_layout_passes=False))
    def invert(dest_hbm, out_hbm, dest_vmem, rows_vmem):
        @pl.when((lax.axis_index("core") == 0) & (lax.axis_index("subcore") == 0))
        def _():
            pltpu.sync_copy(dest_hbm, dest_vmem)

            @pl.loop(0, n_rows, step=SC_LANES)
            def _(r):
                rows_vmem[pl.ds(r, SC_LANES)] = lax.rem(lax.iota(jnp.int32, SC_LANES) + r, n_tok)

            @pl.loop(0, n, step=SC_LANES)
            def _(a):
                tok = lax.iota(jnp.int32, SC_LANES) + a
                tok = jnp.where(tok >= n_tok, tok - n_tok, tok)
                plsc.store_scatter(rows_vmem, [dest_vmem[pl.ds(a, SC_LANES)]], tok)

            pltpu.sync_copy(rows_vmem, out_hbm)

    return invert(dest)


def _moe_kernel(te_ref, nu_ref, x_ref, wg_ref, wu_ref, wd_ref, *rest, tile0, nj):
    o_ref, xb, acc = rest[-3:]
    t = pl.program_id(0) + tile0
    j = pl.program_id(1)
    used = t < nu_ref[0]

    def partial_sum():
        return _swiglu_chunks(xb[...], wg_ref, wu_ref, wd_ref, MOE_SUB)

    @pl.when(used & (j == 0))
    def _():
        lo, hi = _unpack_bf16_pairs(x_ref[...])
        half = lo.shape[1]
        xb[:, 0:half] = lo.astype(BF16)
        xb[:, half:] = hi.astype(BF16)
        if nj == 1:
            o_ref[...] = _pack_bf16_pairs(partial_sum())
        else:
            acc[...] = partial_sum()

    if nj > 2:
        @pl.when(used & (j > 0) & (j < nj - 1))
        def _():
            acc[...] += partial_sum()

    if nj > 1:
        @pl.when(used & (j == nj - 1))
        def _():
            o_ref[...] = _pack_bf16_pairs(acc[...] + partial_sum())

    @pl.when(jnp.logical_not(used) & (j == nj - 1))
    def _():
        o_ref[...] = jnp.zeros_like(o_ref)


def _moe_grouped(tile_expert, n_used, xs, wg, wu, wd, y_prev, tile0, n_rows):
    p, dp = xs.shape
    d = 2 * dp
    f = wg.shape[2]
    tm = MOE_TM
    tf = MOE_TF
    nj = f // tf

    def jj(t, j, te, nu):
        return jnp.where(t + tile0 < nu[0], j, nj - 1)

    in_specs = [pl.BlockSpec((tm, dp), lambda t, j, te, nu: (t, 0)),
                pl.BlockSpec((None, d, tf), lambda t, j, te, nu: (te[t + tile0], 0, jj(t, j, te, nu))),
                pl.BlockSpec((None, d, tf), lambda t, j, te, nu: (te[t + tile0], 0, jj(t, j, te, nu))),
                pl.BlockSpec((None, tf, d), lambda t, j, te, nu: (te[t + tile0], jj(t, j, te, nu), 0))]
    args = [tile_expert, n_used, xs, wg, wu, wd]
    aliases = {}
    if y_prev is not None:
        in_specs.append(pl.BlockSpec(memory_space=pl.ANY))
        args.append(y_prev)
        aliases = {len(args) - 1: 0}
    return pl.pallas_call(
        functools.partial(_moe_kernel, tile0=tile0, nj=nj),
        out_shape=jax.ShapeDtypeStruct((n_rows, dp), jnp.int32),
        grid_spec=pltpu.PrefetchScalarGridSpec(
            num_scalar_prefetch=2,
            grid=(p // tm, nj),
            in_specs=in_specs,
            out_specs=pl.BlockSpec((tm, dp), lambda t, j, te, nu: (t + tile0, 0)),
            scratch_shapes=[pltpu.VMEM((tm, d), BF16), pltpu.VMEM((tm, d), F32)]),
        input_output_aliases=aliases,
        compiler_params=_params("arbitrary", "arbitrary"),
        name="moe_grouped",
    )(*args)


def _combine_kernel(y1_ref, y2_ref, x_ref, g2_ref, rt_ref, o_ref):
    rt = rt_ref[...]
    w1, w2 = rt[:, 2:3], rt[:, 3:4]
    half = y1_ref.shape[1]
    for k, (a, c) in enumerate(zip(_unpack_bf16_pairs(y1_ref[...]), _unpack_bf16_pairs(y2_ref[...]))):
        cs = slice(k * half, (k + 1) * half)
        o_ref[:, cs] = x_ref[:, cs] + g2_ref[:, cs] * (w1 * a + w2 * c)


def _moe_combine(yg, x, g2, route):
    b, s, d = x.shape
    tm = _pick_tile(s, 512)
    tok = lambda n: pl.BlockSpec((None, tm, n), lambda bi, i: (bi, i, 0))
    return pl.pallas_call(
        _combine_kernel,
        out_shape=jax.ShapeDtypeStruct((b, s, d), F32),
        grid=(b, s // tm),
        in_specs=[tok(d // 2), pl.BlockSpec((None, tm, d // 2), lambda bi, i: (b + bi, i, 0)), tok(d),
                  pl.BlockSpec((None, 1, d), lambda bi, i: (bi, 0, 0)), tok(LANES)],
        out_specs=tok(d),
        compiler_params=_params("parallel", "parallel"),
        name="moe_combine",
    )(yg, yg, x, g2, route)


def _moe(h, route, x, g2, wg, wu, wd):
    b, s, d = x.shape
    n_tok = b * s
    tm = MOE_TM
    rt = route.reshape(n_tok, LANES)
    flat_e = jnp.concatenate([rt[:, 0], rt[:, 1]]).astype(jnp.int32)
    onehot = (flat_e[:, None] == jnp.arange(N_EXPERTS, dtype=jnp.int32)[None, :]).astype(jnp.int32)
    csum = jnp.cumsum(onehot, axis=0)
    rank = jnp.sum(csum * onehot, axis=1) - 1
    counts = csum[-1]
    tiles_e = (counts + tm - 1) // tm
    tiles_cum = jnp.cumsum(tiles_e)
    row_start = (tiles_cum - tiles_e) * tm
    dest = jnp.sum(onehot * row_start[None, :], axis=1) + rank
    n_tiles = 2 * n_tok // tm + N_EXPERTS
    tile_ids = jnp.arange(n_tiles, dtype=jnp.int32)
    tile_expert = jnp.sum((tile_ids[:, None] >= tiles_cum[None, :]).astype(jnp.int32), axis=1)
    last_e = jnp.max(jnp.where(tiles_e > 0, jnp.arange(N_EXPERTS, dtype=jnp.int32), 0))
    tile_expert = jnp.minimum(tile_expert, last_e).astype(jnp.int32)
    n_used = tiles_cum[-1:].astype(jnp.int32)
    dest = dest.astype(jnp.int32)
    row_src = _invert_rows(dest, n_tiles * tm)
    n_chunks = max(c for c in range(1, MOE_CHUNKS + 1) if n_tiles % c == 0)
    tiles_c = n_tiles // n_chunks
    h_flat = h.reshape(n_tok, d // 2)
    xs = [_gather_rows(row_src[c * tiles_c * tm:(c + 1) * tiles_c * tm], h_flat) for c in range(n_chunks)]
    y = None
    for c in range(n_chunks):
        y = _moe_grouped(tile_expert, n_used, xs[c], wg, wu, wd, y, c * tiles_c, n_tiles * tm)
    yg = _gather_rows(dest, y).reshape(2 * b, s, d // 2)
    return _moe_combine(yg, x, g2, route)


def _rope_tables(n):
    rows = n // GRID_W
    r = jnp.repeat(jnp.arange(rows), GRID_W).astype(F32)
    col = jnp.tile(jnp.arange(GRID_W), rows).astype(F32)
    freqs = ROPE_BASE ** (-jnp.arange(ROPE_FREQS, dtype=F32) / ROPE_FREQS)
    ang = jnp.stack([r[:, None] * freqs, col[:, None] * freqs], axis=1)
    ang = jnp.repeat(ang[:, :, None, :], 2, axis=2).reshape(n, HEAD_DIM)
    ang = jnp.tile(ang, (1, LANES // HEAD_DIM))
    cos, sin = jnp.cos(ang), jnp.sin(ang)
    first_half = (jnp.arange(LANES) % (2 * ROPE_FREQS)) < ROPE_FREQS
    return cos, jnp.where(first_half, -sin, 0.0), jnp.where(first_half, 0.0, sin)


def _lane_rows(lg):
    return jnp.repeat(lg.astype(F32), HEAD_DIM).reshape(RET_HEADS // 2, 1, LANES)


def kernel(x, c, ctx, c_ctx, ada_w, ada_b, norm1_w, norm2_w, w_in, w_out, q_norm_w, k_norm_w,
           attn_sink, ret_decay_f, ret_decay_b, ret_gn_w, conv_w, conv_b, conv_ln_w, conv_ln_b,
           ffn_w_gate, ffn_w_up, ffn_w_down, router_w, moe_w_gate, moe_w_up, moe_w_down):
    b, n, d = x.shape
    n_ctx = ctx.shape[1]
    depth = ada_w.shape[0]
    cond = jnp.zeros((SUBLANES, d), F32).at[0:b].set(c).at[b].set(c_ctx)
    mods = _adaln(cond, ada_w, ada_b).reshape(depth, SUBLANES, 6, d)
    cos, sa, sb = _rope_tables(n)
    ones_c = jnp.ones((n_ctx, LANES), F32)
    zeros_c = jnp.zeros((n_ctx, LANES), F32)
    zero_state = jnp.zeros((b, RET_HEADS // 2, LANES, LANES), F32)
    row = lambda v: v.reshape(1, -1)
    moe_f32 = [w.reshape(-1, w.shape[-1]) for w in (moe_w_gate, moe_w_up, moe_w_down)]
    cast_jobs = {l: [j for j in range(3) if j * min(depth, 2) // 3 == l] for l in range(depth)}
    moe_bf = [None, None, None]
    for l in range(depth):
        last = l == depth - 1
        m_lat = [mods[l, 0:b, k][:, None, :] for k in range(6)]
        m_ctx = [jnp.broadcast_to(mods[l, b, k][None, None, :], (b, 1, d)) for k in range(6)]
        w_in_bf = w_in[l].astype(BF16)
        w_out_bf = w_out[l].astype(BF16)
        qw = row(jnp.tile(q_norm_w[l], LANES // HEAD_DIM))
        kw = row(jnp.tile(k_norm_w[l], LANES // HEAD_DIM))
        lgf = _lane_rows(jax.nn.log_sigmoid(ret_decay_f[l].astype(F32)))
        lgb = _lane_rows(jax.nn.log_sigmoid(ret_decay_b[l].astype(F32)))
        sink_tab = jnp.repeat(attn_sink[l].astype(F32), ATT_BLOCK).reshape(ATT_Q_HEADS // 2, 2 * ATT_BLOCK)

        q, k, v, rk, rv, rq, rg, cv = _inproj(x, row(norm1_w[l]), m_lat[1], m_lat[0], w_in_bf,
                                               cos, sa, sb, qw, kw)
        qc, kc, vc, rkc, rvc, rqc, rgc, cvc = _inproj(ctx, row(norm1_w[l]), m_ctx[1], m_ctx[0], w_in_bf,
                                                       ones_c, zeros_c, zeros_c, qw, kw)
        rf_c, s_f = _ret_states(lgf, rkc, rvc, zero_state, reverse=False)
        rb_c, s_b = _ret_states(lgb, rkc, rvc, zero_state, reverse=True)
        rf, _ = _ret_states(lgf, rk, rv, s_f, reverse=False)
        rb, _ = _ret_states(lgb, rk, rv, s_b, reverse=True)

        att, cast_out = _attention(q, k, v, kc, vc, sink_tab, window=True,
                                   cast=[moe_f32[j] for j in cast_jobs[l]])
        for j, w_bf in zip(cast_jobs[l], cast_out):
            moe_bf[j] = w_bf.reshape((moe_w_gate, moe_w_up, moe_w_down)[j].shape)
        ret = _ret_out(lgf, lgb, rq, rk, rv, rg, rf, rb, row(ret_gn_w[l]))
        cnv = _conv(cv, conv_w[l], row(conv_b[l]), row(conv_ln_w[l]), row(conv_ln_b[l]))

        if l % 2 == 0:
            i = l // 2
            wg, wu, wd = ffn_w_gate[i].astype(BF16), ffn_w_up[i].astype(BF16), ffn_w_down[i].astype(BF16)
            x_mid, h2 = _outproj(att, ret, cnv, w_out_bf, x, m_lat[2], row(norm2_w[l]), m_lat[4], m_lat[3])
            x_new = _ffn(h2, wg, wu, wd, x_mid, m_lat[5])
        else:
            i = l // 2
            wg, wu, wd = moe_bf[0][i], moe_bf[1][i], moe_bf[2][i]
            x_mid, h2, route = _outproj(att, ret, cnv, w_out_bf, x, m_lat[2], row(norm2_w[l]),
                                        m_lat[4], m_lat[3], router=router_w[i])
            x_new = _moe(h2, route, x_mid, m_lat[5], wg, wu, wd)

        if not last:
            att_c, _ = _attention(qc, None, None, kc, vc, sink_tab, window=False)
            ret_c = _ret_out(lgf, lgb, rqc, rkc, rvc, rgc, rf_c, rb_c, row(ret_gn_w[l]))
            cnv_c = _conv(cvc, conv_w[l], row(conv_b[l]), row(conv_ln_w[l]), row(conv_ln_b[l]))
            if l % 2 == 0:
                c_mid, h2c = _outproj(att_c, ret_c, cnv_c, w_out_bf, ctx, m_ctx[2], row(norm2_w[l]),
                                      m_ctx[4], m_ctx[3])
                ctx = _ffn(h2c, wg, wu, wd, c_mid, m_ctx[5])
            else:
                c_mid, h2c, route_c = _outproj(att_c, ret_c, cnv_c, w_out_bf, ctx, m_ctx[2],
                                               row(norm2_w[l]), m_ctx[4], m_ctx[3],
                                               router=router_w[i])
                ctx = _moe(h2c, route_c, c_mid, m_ctx[5], wg, wu, wd)
        x = x_new
    return x
```

```python
import functools

import jax
import jax.numpy as jnp
from jax import lax
from jax.experimental import pallas as pl
from jax.experimental.pallas import tpu as pltpu
from jax.experimental.pallas import tpu_sc as plsc

F32 = jnp.float32
BF16 = jnp.bfloat16

GRID_W = 64
HEAD_DIM = 64
ATT_Q_HEADS = 8
ATT_KV_HEADS = 2
ATT_WINDOW = 128
ATT_BLOCK = 128
RET_HEADS = 4
RET_CHUNK = 128
RET_K_SCALE = HEAD_DIM ** -0.5
ATT_SCALE = HEAD_DIM ** -0.5
CONV_CH = 256
CONV_WIDTH = 31
ROPE_BASE = 10000.0
ROPE_FREQS = HEAD_DIM // 4
D_ATT = ATT_Q_HEADS * HEAD_DIM
D_RET = RET_HEADS * HEAD_DIM
ATT_KV_W = ATT_KV_HEADS * HEAD_DIM
C_ATT_K = 0
C_ATT_V = C_ATT_K + ATT_KV_W
C_RET_K = C_ATT_V + ATT_KV_W
C_RET_V = C_RET_K + D_RET
C_ATT_Q = C_RET_V + D_RET
C_RET_Q = C_ATT_Q + D_ATT
C_RET_G = C_RET_Q + D_RET
C_CONV = C_RET_G + D_RET
N_EXPERTS = 8
EPS = 1e-6
NEG_INF = -1e30

LANES = 128
SUBLANES = 8
BF16_SUBLANES = 16
VMEM_LIMIT = 48 * 1024 * 1024
CONV_HALO = 16
CONV_SUB = 32
CONV_CHAINS = 4
ATT_QBLOCKS = 4
RET_UNROLL = 4
MOE_TM = 512
MOE_CHUNKS = 2
MOE_SUB = 512
FFN_SUB = 512
MOE_TF = 1792
SC_LANES = 16
SC_WINDOW = 64


def _params(*sem):
    return pltpu.CompilerParams(dimension_semantics=sem, vmem_limit_bytes=VMEM_LIMIT)


def _sigmoid(x):
    return 1.0 / (1.0 + jnp.exp(-x))


def _pack_bf16_pairs(v):
    c = v.shape[1] // 2
    bits = pltpu.bitcast(v.astype(BF16).astype(F32), jnp.uint32)
    packed = (bits[:, c:] & jnp.uint32(0xFFFF0000)) | (bits[:, :c] >> 16)
    return pltpu.bitcast(packed, jnp.int32)


def _unpack_bf16_pairs(p):
    bits = pltpu.bitcast(p, jnp.uint32)
    return pltpu.bitcast(bits << 16, F32), pltpu.bitcast(bits & jnp.uint32(0xFFFF0000), F32)


def _pick_tile(n, pref):
    t = min(n, pref)
    assert n % t == 0, (n, t)
    return t


def _adaln_kernel(c_ref, w_ref, b_ref, o_ref):
    c = c_ref[...]
    s = c * _sigmoid(c)
    o_ref[...] = jnp.dot(s, w_ref[...], preferred_element_type=F32,
                         precision=lax.Precision.HIGHEST) + b_ref[...]


def _adaln(cond, ada_w, ada_b):
    depth, d, n = ada_w.shape
    tn = _pick_tile(n, 1536)
    return pl.pallas_call(
        _adaln_kernel,
        out_shape=jax.ShapeDtypeStruct((depth, cond.shape[0], n), F32),
        grid=(depth, n // tn),
        in_specs=[pl.BlockSpec(cond.shape, lambda l, j: (0, 0)),
                  pl.BlockSpec((None, d, tn), lambda l, j: (l, 0, j)),
                  pl.BlockSpec((None, 1, tn), lambda l, j: (l, 0, j))],
        out_specs=pl.BlockSpec((None, cond.shape[0], tn), lambda l, j: (l, 0, j)),
        compiler_params=_params("parallel", "parallel"),
        name="adaln",
    )(cond, ada_w, ada_b.reshape(depth, 1, n))


def _modulated_rms(x, nw, sc, sh):
    ms = jnp.mean(x * x, axis=-1, keepdims=True)
    return (x * lax.rsqrt(ms + EPS) * nw) * (1.0 + sc) + sh


def _head_group_matrix():
    r = lax.broadcasted_iota(jnp.int32, (2 * LANES, 2 * LANES), 0) // HEAD_DIM
    c = lax.broadcasted_iota(jnp.int32, (2 * LANES, 2 * LANES), 1) // HEAD_DIM
    return jnp.where(r == c, 1.0, 0.0).astype(BF16)


def _per_head_sum(v, grp):
    hi = v.astype(BF16)
    lo = (v - hi.astype(F32)).astype(BF16)
    r = jnp.dot(jnp.concatenate([hi, lo], axis=1), grp, preferred_element_type=F32)
    return r[:, 0:LANES] + r[:, LANES:]


def _dup_halves(t):
    sw = pltpu.roll(t, HEAD_DIM, 1)
    lo = lax.broadcasted_iota(jnp.int32, t.shape, 1) < HEAD_DIM
    return jnp.where(lo, t, sw), jnp.where(lo, sw, t)


def _inproj_kernel(x_ref, nw_ref, sc_ref, sh_ref, w_ref, cos_ref, sa_ref, sb_ref, qw_ref, kw_ref,
                   q_ref, k_ref, v_ref, rk_ref, rv_ref, rq_ref, rg_ref, cv_ref):
    hb = _modulated_rms(x_ref[...], nw_ref[...], sc_ref[...], sh_ref[...]).astype(BF16)

    def proj(c0, n):
        return jnp.dot(hb, w_ref[:, c0:c0 + n], preferred_element_type=F32)

    grp = _head_group_matrix()
    cos, sa, sb = cos_ref[...], sa_ref[...], sb_ref[...]

    def norm_rope(p, wrow):
        y = p * lax.rsqrt(_per_head_sum(p * p, grp) * (1.0 / HEAD_DIM) + EPS) * wrow
        return (y * cos + pltpu.roll(y, LANES - ROPE_FREQS, 1) * sa
                + pltpu.roll(y, ROPE_FREQS, 1) * sb)

    def put_kv(kv):
        k0, k1 = _dup_halves(norm_rope(kv[:, 0:ATT_KV_W], kw_ref[...]))
        k_ref[:, 0:LANES] = k0.astype(BF16)
        k_ref[:, LANES:2 * LANES] = k1.astype(BF16)
        v_ref[...] = kv[:, ATT_KV_W:].T.astype(BF16)

    def put_q(qall):
        qw = qw_ref[...] * ATT_SCALE
        for j in range(D_ATT // LANES):
            q_ref[:, j * LANES:(j + 1) * LANES] = norm_rope(qall[:, j * LANES:(j + 1) * LANES], qw).astype(BF16)

    def put_rk(p):
        rk_ref[...] = (p * RET_K_SCALE).astype(BF16)

    def put_rv(p):
        rv_ref[...] = p.astype(BF16)

    def put_rq(p):
        rq_ref[...] = p.astype(BF16)

    def put_rg(p):
        rg_ref[...] = p

    def put_conv(p):
        cv_ref[...] = p[:, 0:CONV_CH] * _sigmoid(p[:, CONV_CH:])

    units = [(C_ATT_K, 2 * ATT_KV_W, put_kv), (C_ATT_Q, D_ATT, put_q), (C_RET_K, D_RET, put_rk),
             (C_RET_V, D_RET, put_rv), (C_RET_Q, D_RET, put_rq), (C_RET_G, D_RET, put_rg),
             (C_CONV, 2 * CONV_CH, put_conv)]
    ahead = 2
    pending = [proj(c0, n) for c0, n, _ in units[:ahead]]
    for n, (_, _, put) in enumerate(units):
        if n + ahead < len(units):
            pending.append(proj(*units[n + ahead][:2]))
        put(pending.pop(0))


def _inproj(x, nw, sc, sh, w_bf, cos, sa, sb, qw, kw):
    b, s, d = x.shape
    tm = _pick_tile(s, 512)
    row = lambda n: pl.BlockSpec((1, n), lambda bi, i: (0, 0))
    per_b = pl.BlockSpec((None, 1, d), lambda bi, i: (bi, 0, 0))
    tab = pl.BlockSpec((tm, LANES), lambda bi, i: (i, 0))
    tok = lambda n: pl.BlockSpec((None, tm, n), lambda bi, i: (bi, i, 0))
    shp = lambda n, dt: jax.ShapeDtypeStruct((b, s, n), dt)
    return pl.pallas_call(
        _inproj_kernel,
        out_shape=(shp(D_ATT, BF16), shp(2 * LANES, BF16), jax.ShapeDtypeStruct((b, ATT_KV_W, s), BF16),
                   shp(D_RET, BF16), shp(D_RET, BF16), shp(D_RET, BF16), shp(D_RET, F32),
                   shp(CONV_CH, F32)),
        grid=(b, s // tm),
        in_specs=[tok(d), row(d), per_b, per_b,
                  pl.BlockSpec(w_bf.shape, lambda bi, i: (0, 0)),
                  tab, tab, tab, row(LANES), row(LANES)],
        out_specs=(tok(D_ATT), tok(2 * LANES), pl.BlockSpec((None, ATT_KV_W, tm), lambda bi, i: (bi, 0, i)),
                   tok(D_RET), tok(D_RET), tok(D_RET), tok(D_RET), tok(CONV_CH)),
        compiler_params=_params("parallel", "parallel"),
        name="inproj",
    )(x, nw, sc, sh, w_bf, cos, sa, sb, qw, kw)


def _attn_kernel(*refs, window, nq, n_cast):
    blk = ATT_BLOCK
    n_in = len(refs) - 1 - n_cast
    for src, dst in zip(refs[n_in - n_cast:n_in], refs[n_in + 1:]):
        dst[...] = src[...].astype(BF16)
    refs = refs[:n_in - n_cast] + refs[n_in:n_in + 1]
    if window:
        q_ref = refs[0]
        k_refs = refs[1:nq + 3]
        v_refs = refs[nq + 3:2 * nq + 5]
        kx_ref, vx_ref, sink_ref, o_ref = refs[2 * nq + 5:]
    else:
        q_ref, kx_ref, vx_ref, sink_ref, o_ref = refs
    n_ctx = kx_ref.shape[0]
    nk = 3 * blk + n_ctx if window else n_ctx
    if window:
        i = pl.program_id(1)
        last = nq * pl.num_programs(1) - 1
        key = lax.broadcasted_iota(jnp.int32, (blk, 2 * blk), 0)
        qry = lax.broadcasted_iota(jnp.int32, (blk, 2 * blk), 1) & (blk - 1)

        def band_masks(sub):
            off_prev = jnp.where(nq * i + sub > 0, 0, blk)
            off_next = jnp.where(nq * i + sub < last, 0, blk)
            return key >= qry + off_prev, key + off_next <= qry
    first_head = lax.broadcasted_iota(jnp.int32, (blk, LANES), 1) < HEAD_DIM
    ones = jnp.ones((2 * SUBLANES, nk), BF16)

    def scores(sub, pair):
        g = pair // 2
        gs = slice(g * LANES, (g + 1) * LANES)
        if window:
            kcat = jnp.concatenate([r[:, gs] for r in k_refs[sub:sub + 3]] + [kx_ref[:, gs]], axis=0)
        else:
            kcat = kx_ref[:, gs]
        qp = q_ref[sub * blk:(sub + 1) * blk, pair * LANES:(pair + 1) * LANES]
        zero = jnp.zeros_like(qp)
        w = jnp.concatenate([jnp.where(first_head, qp, zero), jnp.where(first_head, zero, qp)], axis=0)
        return lax.dot_general(kcat, w, (((1,), (1,)), ((), ())), preferred_element_type=F32)

    def masked_max(sub, pair, s):
        if window:
            mask_prev, mask_next = band_masks(sub)
            parts = [jnp.where(mask_prev, s[0:blk], NEG_INF), s[blk:2 * blk],
                     jnp.where(mask_next, s[2 * blk:3 * blk], NEG_INF), s[3 * blk:]]
        else:
            parts = [s]
        m = sink_ref[pair:pair + 1, :]
        for part in parts:
            m = jnp.maximum(m, jnp.max(part, axis=0, keepdims=True))
        return parts, m

    def softmax(sub, pair, parts_m):
        parts, m = parts_m
        p = jnp.concatenate([jnp.exp(part - m).astype(BF16) for part in parts], axis=0)
        return p, jnp.exp(sink_ref[pair:pair + 1, :] - m)

    def output(sub, pair, p, sink_p):
        g = pair // 2
        vs = slice(g * HEAD_DIM, (g + 1) * HEAD_DIM)
        if window:
            vt = jnp.concatenate([r[vs, :] for r in v_refs[sub:sub + 3]] + [vx_ref[vs, :]], axis=1)
        else:
            vt = vx_ref[vs, :]
        vaug = jnp.concatenate([vt, ones], axis=0)
        o = jnp.dot(vaug, p, preferred_element_type=F32)
        on = o[0:HEAD_DIM, :] * (1.0 / (o[HEAD_DIM:HEAD_DIM + 1, :] + sink_p))
        ot = jnp.concatenate([on[:, 0:blk], on[:, blk:2 * blk]], axis=0)
        o_ref[sub * blk:(sub + 1) * blk, pair * LANES:(pair + 1) * LANES] = ot.T.astype(BF16)

    units = [(sub, pair) for sub in range(nq) for pair in range(ATT_Q_HEADS // 2)]
    stages = (lambda sub, pair, _: scores(sub, pair), masked_max, softmax,
              lambda sub, pair, ps: output(sub, pair, *ps))
    vals = {}
    for step in range(len(units) + len(stages) - 1):
        for k, stage in enumerate(stages):
            u = step - k
            if 0 <= u < len(units):
                vals[u, k] = stage(*units[u], vals.pop((u, k - 1), None))


def _attention(q, k, vt, kx, vxt, sink_tab, window, cast=()):
    b, s, _ = q.shape
    blk = ATT_BLOCK
    nb = s // blk
    nq = ATT_QBLOCKS if nb % ATT_QBLOCKS == 0 else 1
    n_ctx = kx.shape[1]
    qspec = pl.BlockSpec((None, nq * blk, D_ATT), lambda bi, i: (bi, i, 0))
    kctx = pl.BlockSpec((None, n_ctx, 2 * LANES), lambda bi, i: (bi, 0, 0))
    vctx = pl.BlockSpec((None, ATT_KV_W, n_ctx), lambda bi, i: (bi, 0, 0))
    snk = pl.BlockSpec(sink_tab.shape, lambda bi, i: (0, 0))
    if window:
        at = lambda off: (lambda i: jnp.clip(nq * i + off, 0, nb - 1))
        kspec = lambda f: pl.BlockSpec((None, blk, 2 * LANES), lambda bi, i: (bi, f(i), 0))
        vspec = lambda f: pl.BlockSpec((None, ATT_KV_W, blk), lambda bi, i: (bi, 0, f(i)))
        offs = range(-1, nq + 1)
        in_specs = ([qspec] + [kspec(at(o)) for o in offs] + [vspec(at(o)) for o in offs]
                    + [kctx, vctx, snk])
        args = (q,) + (k,) * (nq + 2) + (vt,) * (nq + 2) + (kx, vxt, sink_tab)
    else:
        in_specs = [qspec, kctx, vctx, snk]
        args = (q, kx, vxt, sink_tab)
    steps = nb // nq
    out_shape, out_specs = [jax.ShapeDtypeStruct((b, s, D_ATT), BF16)], [qspec]
    for w in cast:
        rows = w.shape[0] // (b * steps)
        assert rows * b * steps == w.shape[0] and rows % (2 * SUBLANES) == 0, w.shape
        slab = pl.BlockSpec((rows, w.shape[1]), lambda bi, i: (bi * steps + i, 0))
        in_specs.append(slab)
        out_specs.append(slab)
        out_shape.append(jax.ShapeDtypeStruct(w.shape, BF16))
    res = pl.pallas_call(
        functools.partial(_attn_kernel, window=window, nq=nq, n_cast=len(cast)),
        out_shape=tuple(out_shape),
        grid=(b, steps),
        in_specs=in_specs,
        out_specs=tuple(out_specs),
        compiler_params=_params("parallel", "parallel"),
        name="attention_window" if window else "attention_ctx",
    )(*args, *cast)
    return res[0], tuple(res[1:])


def _ret_state_kernel(lgl_ref, k_ref, v_ref, s0_ref, r_ref, fin_ref, s_scr, *, cb, reverse):
    c = pl.program_id(1)
    L = RET_CHUNK

    @pl.when(c == 0)
    def _():
        s_scr[...] = s0_ref[...]

    jj = lax.broadcasted_iota(jnp.int32, (L, LANES), 0).astype(F32)
    expo = jj if reverse else (L - 1.0) - jj
    same_head = (lax.broadcasted_iota(jnp.int32, (LANES, LANES), 0) // HEAD_DIM
                 == lax.broadcasted_iota(jnp.int32, (LANES, LANES), 1) // HEAD_DIM)
    for pr in range(RET_HEADS // 2):
        cs = slice(pr * LANES, (pr + 1) * LANES)
        lgl = lgl_ref[pr]
        kdec = jnp.exp(expo * lgl)
        cdec = jnp.exp(float(L) * lgl)
        state = s_scr[pr]
        for t in range(cb):
            cc = cb - 1 - t if reverse else t
            rows = slice(cc * L, (cc + 1) * L)
            r_ref[cc, pr] = state.astype(BF16)
            kd = k_ref[rows, cs].astype(F32) * kdec
            u = jnp.dot(kd.T.astype(BF16), v_ref[rows, cs], preferred_element_type=F32)
            state = cdec * state + jnp.where(same_head, u, 0.0)
        s_scr[pr] = state

    @pl.when(c == pl.num_programs(1) - 1)
    def _():
        fin_ref[...] = s_scr[...]


def _ret_states(lgl, rk, rv, s0, reverse):
    b, s, _ = rk.shape
    nc = s // RET_CHUNK
    cb = _pick_tile(nc, 8)
    nblk = nc // cb
    npair = RET_HEADS // 2
    blk_idx = (lambda c: nblk - 1 - c) if reverse else (lambda c: c)
    tok = pl.BlockSpec((None, cb * RET_CHUNK, D_RET), lambda bi, c: (bi, blk_idx(c), 0))
    st = pl.BlockSpec((None, npair, LANES, LANES), lambda bi, c: (bi, 0, 0, 0))
    return pl.pallas_call(
        functools.partial(_ret_state_kernel, cb=cb, reverse=reverse),
        out_shape=(jax.ShapeDtypeStruct((b, nc, npair, LANES, LANES), BF16),
                   jax.ShapeDtypeStruct((b, npair, LANES, LANES), F32)),
        grid=(b, nblk),
        in_specs=[pl.BlockSpec(lgl.shape, lambda bi, c: (0, 0, 0)), tok, tok, st],
        out_specs=(pl.BlockSpec((None, cb, npair, LANES, LANES),
                                lambda bi, c: (bi, blk_idx(c), 0, 0, 0)), st),
        scratch_shapes=[pltpu.VMEM((npair, LANES, LANES), F32)],
        compiler_params=_params("parallel", "arbitrary"),
        name="ret_state_bwd" if reverse else "ret_state_fwd",
    )(lgl, rk, rv, s0)


def _ret_out_kernel(lgf_ref, lgb_ref, q_ref, k_ref, v_ref, g_ref, rf_ref, rb_ref, gnw_ref, o_ref, *, cb):
    L = RET_CHUNK
    i0 = lax.broadcasted_iota(jnp.int32, (L, LANES), 0).astype(F32)
    i1 = lax.broadcasted_iota(jnp.int32, (L, LANES), 1).astype(F32)
    diff = i0 - i1
    lo = lax.broadcasted_iota(jnp.int32, (L, LANES), 1) < HEAD_DIM
    grp = _head_group_matrix()
    inv = 1.0 / HEAD_DIM
    tabs = []
    for pr in range(RET_HEADS // 2):
        lgf, lgb = lgf_ref[pr], lgb_ref[pr]
        dmat = [jnp.where(diff >= 0.0,
                          jnp.exp(jnp.maximum(diff, 0.0) * lgf[:, a:a + 1]),
                          jnp.exp(jnp.maximum(-diff, 0.0) * lgb[:, a:a + 1]))
                for a in (0, HEAD_DIM)]
        tabs.append((jnp.exp((i0 + 1.0) * lgf), jnp.exp((float(L) - i0) * lgb),
                     jnp.concatenate(dmat, axis=0), gnw_ref[:, pr * LANES:(pr + 1) * LANES]))

    def decayed_scores(t, pr):
        rows = pl.ds(pl.multiple_of(t * L, L), L)
        cs = slice(pr * LANES, (pr + 1) * LANES)
        qp, kp = q_ref[rows, cs], k_ref[rows, cs]
        zero = jnp.zeros_like(kp)
        qz = jnp.concatenate([jnp.where(lo, qp, zero), jnp.where(lo, zero, qp)], axis=0)
        return lax.dot_general(qz, kp, (((1,), (1,)), ((), ())), preferred_element_type=F32) * tabs[pr][2]

    def mix(t, pr, a):
        rows = pl.ds(pl.multiple_of(t * L, L), L)
        cs = slice(pr * LANES, (pr + 1) * LANES)
        xif, xib, _, _ = tabs[pr]
        vp = v_ref[rows, cs]
        zero = jnp.zeros_like(vp)
        qf = q_ref[rows, cs].astype(F32)
        lhs = jnp.concatenate([a[0:L].astype(BF16), a[L:].astype(BF16),
                               (qf * xif).astype(BF16), (qf * xib).astype(BF16)], axis=1)
        rhs = jnp.concatenate([jnp.where(lo, vp, zero), jnp.where(lo, zero, vp),
                               rf_ref[t, pr], rb_ref[t, pr]], axis=0)
        return jnp.dot(lhs, rhs, preferred_element_type=F32)

    def centre(t, pr, acc):
        return acc - _per_head_sum(acc, grp) * inv

    def readout(t, pr, dl):
        rows = pl.ds(pl.multiple_of(t * L, L), L)
        cs = slice(pr * LANES, (pr + 1) * LANES)
        var = _per_head_sum(dl * dl, grp) * inv
        y = dl * lax.rsqrt(var + EPS) * tabs[pr][3]
        gt = g_ref[rows, cs]
        o_ref[rows, cs] = (gt * _sigmoid(gt) * y).astype(BF16)

    unroll = min(RET_UNROLL, cb)
    stages = (lambda t, pr, _: decayed_scores(t, pr), mix, centre, readout)

    def group(gi, carry):
        units = [(gi * unroll + u, pr) for u in range(unroll) for pr in range(RET_HEADS // 2)]
        vals = {}
        for step in range(len(units) + len(stages) - 1):
            for k, stage in enumerate(stages):
                u = step - k
                if 0 <= u < len(units):
                    vals[u, k] = stage(*units[u], vals.pop((u, k - 1), None))
        return carry

    lax.fori_loop(0, cb // unroll, group, 0)


def _ret_out(lgf, lgb, rq, rk, rv, rg, rf, rb, gnw):
    b, s, _ = rq.shape
    nc = s // RET_CHUNK
    cb = _pick_tile(nc, 8)
    npair = RET_HEADS // 2
    tok = pl.BlockSpec((None, cb * RET_CHUNK, D_RET), lambda bi, c: (bi, c, 0))
    st = pl.BlockSpec((None, cb, npair, LANES, LANES), lambda bi, c: (bi, c, 0, 0, 0))
    lg = pl.BlockSpec(lgf.shape, lambda bi, c: (0, 0, 0))
    return pl.pallas_call(
        functools.partial(_ret_out_kernel, cb=cb),
        out_shape=jax.ShapeDtypeStruct((b, s, D_RET), BF16),
        grid=(b, nc // cb),
        in_specs=[lg, lg, tok, tok, tok, tok, st, st, pl.BlockSpec((1, D_RET), lambda bi, c: (0, 0))],
        out_specs=tok,
        compiler_params=_params("parallel", "parallel"),
        name="ret_out",
    )(lgf, lgb, rq, rk, rv, rg, rf, rb, gnw)


def _conv_kernel(prev_ref, cur_ref, next_ref, w_ref, b_ref, lnw_ref, lnb_ref, o_ref,
                 xpad, shifted, hbuf, wtile):
    i = pl.program_id(1)
    tm = cur_ref.shape[0]
    halo = CONV_HALO
    first = i == 0
    last = i == pl.num_programs(1) - 1
    xpad[0:halo, :] = jnp.where(first, 0.0, prev_ref[...])
    xpad[halo:halo + tm, :] = cur_ref[...]
    xpad[halo + tm:2 * halo + tm, :] = jnp.where(last, 0.0, next_ref[...])
    span = tm + 2 * halo - SUBLANES
    for r in range(SUBLANES):
        shifted[r] = xpad[r:r + span, :]
    base = halo - CONV_WIDTH // 2
    bias = b_ref[...]
    ch = cur_ref.shape[1]
    for w in range(CONV_WIDTH):
        wtile[w] = jnp.broadcast_to(w_ref[w:w + 1, :], (SUBLANES, ch))

    taps_by_shift = {}
    for w in range(CONV_WIDTH):
        taps_by_shift.setdefault((base + w) % SUBLANES, []).append(((base + w) // SUBLANES, w))

    def sub(sb, carry):
        r0 = pl.multiple_of(sb * CONV_SUB, CONV_SUB)
        chains = [None] * CONV_CHAINS
        groups = CONV_SUB // SUBLANES
        for r, taps in sorted(taps_by_shift.items()):
            a_lo = min(a for a, _ in taps)
            n_g = max(a for a, _ in taps) - a_lo + groups
            win = shifted[r, pl.ds(r0 + a_lo * SUBLANES, n_g * SUBLANES), :].reshape(n_g, SUBLANES, ch)
            for a, w in taps:
                term = win[a - a_lo:a - a_lo + groups] * wtile[w]
                c = w % CONV_CHAINS
                chains[c] = term if chains[c] is None else chains[c] + term
        total = (chains[0] + chains[1]) + (chains[2] + chains[3])
        hbuf[pl.ds(r0, CONV_SUB), :] = total.reshape(CONV_SUB, ch) + bias
        return carry

    lax.fori_loop(0, tm // CONV_SUB, sub, 0)
    h = hbuf[...]
    mu = jnp.mean(h, axis=-1, keepdims=True)
    dl = h - mu
    var = jnp.mean(dl * dl, axis=-1, keepdims=True)
    y = dl * lax.rsqrt(var + EPS) * lnw_ref[...] + lnb_ref[...]
    o_ref[...] = (y * _sigmoid(y)).astype(BF16)


def _conv(cv, w, bias, lnw, lnb):
    b, s, ch = cv.shape
    tm = _pick_tile(s, 512)
    hpb = tm // CONV_HALO
    nh = s // CONV_HALO
    row = pl.BlockSpec((1, ch), lambda bi, i: (0, 0))
    span = tm + 2 * CONV_HALO - SUBLANES
    return pl.pallas_call(
        _conv_kernel,
        out_shape=jax.ShapeDtypeStruct((b, s, ch), BF16),
        grid=(b, s // tm),
        in_specs=[pl.BlockSpec((None, CONV_HALO, ch), lambda bi, i: (bi, jnp.maximum(i * hpb - 1, 0), 0)),
                  pl.BlockSpec((None, tm, ch), lambda bi, i: (bi, i, 0)),
                  pl.BlockSpec((None, CONV_HALO, ch),
                               lambda bi, i: (bi, jnp.minimum((i + 1) * hpb, nh - 1), 0)),
                  pl.BlockSpec(w.shape, lambda bi, i: (0, 0)), row, row, row],
        out_specs=pl.BlockSpec((None, tm, ch), lambda bi, i: (bi, i, 0)),
        scratch_shapes=[pltpu.VMEM((tm + 2 * CONV_HALO, ch), F32),
                        pltpu.VMEM((SUBLANES, span, ch), F32),
                        pltpu.VMEM((tm, ch), F32),
                        pltpu.VMEM((CONV_WIDTH, SUBLANES, ch), F32)],
        compiler_params=_params("parallel", "parallel"),
        name="conv",
    )(cv, cv, cv, w, bias, lnw, lnb)


def _outproj_kernel(*refs, route):
    if route:
        (att_ref, ret_ref, cnv_ref, w_ref, x_ref, g1_ref, nw_ref, sc_ref, sh_ref, rcat_ref,
         xo_ref, h_ref, rt_ref) = refs
    else:
        att_ref, ret_ref, cnv_ref, w_ref, x_ref, g1_ref, nw_ref, sc_ref, sh_ref, xo_ref, h_ref = refs
    tm = x_ref.shape[0]
    n_part = 2 if (not route and tm % (2 * BF16_SUBLANES) == 0) else 1
    rows_of = lambda part: slice(part * (tm // n_part), (part + 1) * (tm // n_part))

    def mixer_out(part, _):
        rows = rows_of(part)
        return (jnp.dot(att_ref[rows, :], w_ref[0:D_ATT, :], preferred_element_type=F32)
                + jnp.dot(ret_ref[rows, :], w_ref[D_ATT:D_ATT + D_RET, :], preferred_element_type=F32)
                + jnp.dot(cnv_ref[rows, :], w_ref[D_ATT + D_RET:, :], preferred_element_type=F32))

    def residual_norm(part, y):
        rows = rows_of(part)
        xn = x_ref[rows, :] + g1_ref[...] * y
        xo_ref[rows, :] = xn
        h = _modulated_rms(xn, nw_ref[...], sc_ref[...], sh_ref[...])
        h_ref[rows, :] = _pack_bf16_pairs(h) if route else h.astype(BF16)
        if not route:
            return None
        hi = h.astype(BF16)
        lo = (h - hi.astype(F32)).astype(BF16)
        return jnp.dot(jnp.concatenate([hi, lo], axis=0), rcat_ref[...], preferred_element_type=F32)

    def top2(part, r):
        n = r.shape[0] // 2
        logits = (r[0:n, 0:LANES] + r[0:n, LANES:]) + (r[n:, 0:LANES] + r[n:, LANES:])
        lane = lax.broadcasted_iota(jnp.int32, logits.shape, 1).astype(F32)
        logits = jnp.where(lane < N_EXPERTS, logits, NEG_INF)
        m1 = jnp.max(logits, axis=-1, keepdims=True)
        i1 = jnp.min(jnp.where(logits == m1, lane, float(LANES)), axis=-1, keepdims=True)
        rest = jnp.where(lane == i1, NEG_INF, logits)
        m2 = jnp.max(rest, axis=-1, keepdims=True)
        i2 = jnp.min(jnp.where(rest == m2, lane, float(LANES)), axis=-1, keepdims=True)
        e2 = jnp.exp(m2 - m1)
        w1 = 1.0 / (1.0 + e2)
        w2 = e2 / (1.0 + e2)
        rt_ref[rows_of(part), :] = jnp.where(
            lane == 0.0, i1, jnp.where(lane == 1.0, i2,
                                       jnp.where(lane == 2.0, w1, jnp.where(lane == 3.0, w2, 0.0))))

    finish = (lambda part, y: top2(part, residual_norm(part, y))) if route else residual_norm
    stages = (mixer_out, finish)
    vals = {}
    for step in range(n_part + len(stages) - 1):
        for k, stage in enumerate(stages):
            u = step - k
            if 0 <= u < n_part:
                vals[u, k] = stage(u, vals.pop((u, k - 1), None))


def _outproj(att, ret, cnv, w_bf, x, g1, nw, sc, sh, router=None):
    b, s, d = x.shape
    tm = _pick_tile(s, 512)
    tok = lambda n: pl.BlockSpec((None, tm, n), lambda bi, i: (bi, i, 0))
    per_b = pl.BlockSpec((None, 1, d), lambda bi, i: (bi, 0, 0))
    row = pl.BlockSpec((1, d), lambda bi, i: (0, 0))
    in_specs = [tok(D_ATT), tok(D_RET), tok(CONV_CH), pl.BlockSpec(w_bf.shape, lambda bi, i: (0, 0)),
                tok(d), per_b, row, per_b, per_b]
    args = [att, ret, cnv, w_bf, x, g1, nw, sc, sh]
    h_shape = (b, s, d // 2) if router is not None else (b, s, d)
    out_shape = [jax.ShapeDtypeStruct((b, s, d), F32),
                 jax.ShapeDtypeStruct(h_shape, jnp.int32 if router is not None else BF16)]
    out_specs = [tok(d), tok(h_shape[2])]
    if router is not None:
        rpad = jnp.zeros((d, LANES), F32).at[:, :N_EXPERTS].set(router)
        rhi = rpad.astype(BF16)
        rlo = (rpad - rhi.astype(F32)).astype(BF16)
        in_specs += [pl.BlockSpec((d, 2 * LANES), lambda bi, i: (0, 0))]
        args += [jnp.concatenate([rhi, rlo], axis=1)]
        out_shape.append(jax.ShapeDtypeStruct((b, s, LANES), F32))
        out_specs.append(tok(LANES))
    return pl.pallas_call(
        functools.partial(_outproj_kernel, route=router is not None),
        out_shape=tuple(out_shape),
        grid=(b, s // tm),
        in_specs=in_specs,
        out_specs=tuple(out_specs),
        compiler_params=_params("parallel", "parallel"),
        name="outproj_route" if router is not None else "outproj",
    )(*args)


def _swiglu_chunks(h, wg_ref, wu_ref, wd_ref, sub):
    total = None
    for c0 in range(0, wg_ref.shape[1], sub):
        cs = slice(c0, min(c0 + sub, wg_ref.shape[1]))
        gate = jnp.dot(h, wg_ref[:, cs].astype(BF16), preferred_element_type=F32)
        up = jnp.dot(h, wu_ref[:, cs].astype(BF16), preferred_element_type=F32)
        act = (gate * _sigmoid(gate) * up).astype(BF16)
        part = jnp.dot(act, wd_ref[cs, :].astype(BF16), preferred_element_type=F32)
        total = part if total is None else total + part
    return total


def _ffn_kernel(h_ref, wg_ref, wu_ref, wd_ref, x_ref, g2_ref, o_ref):
    y = _swiglu_chunks(h_ref[...], wg_ref, wu_ref, wd_ref, FFN_SUB)
    o_ref[...] = x_ref[...] + g2_ref[...] * y


def _ffn(h, wg, wu, wd, x, g2):
    b, s, d = x.shape
    f = wg.shape[1]
    tm = _pick_tile(s, 512)
    tok = lambda: pl.BlockSpec((None, tm, d), lambda bi, i: (bi, i, 0))
    res = lambda shape: pl.BlockSpec(shape, lambda bi, i: (0, 0), pipeline_mode=pl.Buffered(1))
    return pl.pallas_call(
        _ffn_kernel,
        out_shape=jax.ShapeDtypeStruct((b, s, d), F32),
        grid=(b, s // tm),
        in_specs=[tok(), res((d, f)), res((d, f)), res((f, d)), tok(),
                  pl.BlockSpec((None, 1, d), lambda bi, i: (bi, 0, 0))],
        out_specs=tok(),
        compiler_params=_params("parallel", "parallel"),
        name="ffn_dense",
    )(h, wg, wu, wd, x, g2)


def _gather_rows(idx, src):
    n = idx.shape[0]
    d = src.shape[1]
    mesh = plsc.VectorSubcoreMesh(core_axis_name="core", subcore_axis_name="subcore")

    @pl.kernel(out_type=jax.ShapeDtypeStruct((n, d), src.dtype), mesh=mesh, name="moe_gather")
    def gather(src_hbm, idx_hbm, out_hbm):
        def body(idx_vmem, out_vmem):
            pltpu.sync_copy(src_hbm.at[idx_vmem.at[0, pl.ds(0, SC_WINDOW)]], out_vmem)

        pltpu.emit_pipeline(
            body,
            grid=(n // SC_WINDOW,),
            in_specs=[pl.BlockSpec((1, LANES), lambda i: (i, 0))],
            out_specs=[pl.BlockSpec((SC_WINDOW, d), lambda i: (i, 0))],
            core_axis_name=("core", "subcore"),
            dimension_semantics=(pltpu.PARALLEL,),
        )(idx_hbm, out_hbm)

    idx_rows = jnp.pad(idx.reshape(n // SC_WINDOW, SC_WINDOW), ((0, 0), (0, LANES - SC_WINDOW)))
    return gather(src, idx_rows)


def _invert_rows(dest, n_rows):
    n = dest.shape[0]
    n_tok = n // 2
    mesh = plsc.VectorSubcoreMesh(core_axis_name="core", subcore_axis_name="subcore")

    @pl.kernel(out_type=jax.ShapeDtypeStruct((n_rows,), jnp.int32), mesh=mesh, name="moe_row_src",
               scratch_types=[pltpu.VMEM((n,), jnp.int32), pltpu.VMEM((n_rows,), jnp.int32)],
               compiler_params=pltpu.CompilerParams(needs_layout_passes=False))
    def invert(dest_hbm, out_hbm, dest_vmem, rows_vmem):
        @pl.when((lax.axis_index("core") == 0) & (lax.axis_index("subcore") == 0))
        def _():
            pltpu.sync_copy(dest_hbm, dest_vmem)

            @pl.loop(0, n_rows, step=SC_LANES)
            def _(r):
                rows_vmem[pl.ds(r, SC_LANES)] = lax.rem(lax.iota(jnp.int32, SC_LANES) + r, n_tok)

            @pl.loop(0, n, step=SC_LANES)
            def _(a):
                tok = lax.iota(jnp.int32, SC_LANES) + a
                tok = jnp.where(tok >= n_tok, tok - n_tok, tok)
                plsc.store_scatter(rows_vmem, [dest_vmem[pl.ds(a, SC_LANES)]], tok)

            pltpu.sync_copy(rows_vmem, out_hbm)

    return invert(dest)


def _moe_kernel(te_ref, nu_ref, x_ref, wg_ref, wu_ref, wd_ref, *rest, tile0, nj):
    o_ref, xb, acc = rest[-3:]
    t = pl.program_id(0) + tile0
    j = pl.program_id(1)
    used = t < nu_ref[0]

    def partial_sum():
        return _swiglu_chunks(xb[...], wg_ref, wu_ref, wd_ref, MOE_SUB)

    @pl.when(used & (j == 0))
    def _():
        lo, hi = _unpack_bf16_pairs(x_ref[...])
        half = lo.shape[1]
        xb[:, 0:half] = lo.astype(BF16)
        xb[:, half:] = hi.astype(BF16)
        if nj == 1:
            o_ref[...] = _pack_bf16_pairs(partial_sum())
        else:
            acc[...] = partial_sum()

    if nj > 2:
        @pl.when(used & (j > 0) & (j < nj - 1))
        def _():
            acc[...] += partial_sum()

    if nj > 1:
        @pl.when(used & (j == nj - 1))
        def _():
            o_ref[...] = _pack_bf16_pairs(acc[...] + partial_sum())

    @pl.when(jnp.logical_not(used) & (j == nj - 1))
    def _():
        o_ref[...] = jnp.zeros_like(o_ref)


def _moe_grouped(tile_expert, n_used, xs, wg, wu, wd, y_prev, tile0, n_rows):
    p, dp = xs.shape
    d = 2 * dp
    f = wg.shape[2]
    tm = MOE_TM
    tf = MOE_TF
    nj = f // tf

    def jj(t, j, te, nu):
        return jnp.where(t + tile0 < nu[0], j, nj - 1)

    in_specs = [pl.BlockSpec((tm, dp), lambda t, j, te, nu: (t, 0)),
                pl.BlockSpec((None, d, tf), lambda t, j, te, nu: (te[t + tile0], 0, jj(t, j, te, nu))),
                pl.BlockSpec((None, d, tf), lambda t, j, te, nu: (te[t + tile0], 0, jj(t, j, te, nu))),
                pl.BlockSpec((None, tf, d), lambda t, j, te, nu: (te[t + tile0], jj(t, j, te, nu), 0))]
    args = [tile_expert, n_used, xs, wg, wu, wd]
    aliases = {}
    if y_prev is not None:
        in_specs.append(pl.BlockSpec(memory_space=pl.ANY))
        args.append(y_prev)
        aliases = {len(args) - 1: 0}
    return pl.pallas_call(
        functools.partial(_moe_kernel, tile0=tile0, nj=nj),
        out_shape=jax.ShapeDtypeStruct((n_rows, dp), jnp.int32),
        grid_spec=pltpu.PrefetchScalarGridSpec(
            num_scalar_prefetch=2,
            grid=(p // tm, nj),
            in_specs=in_specs,
            out_specs=pl.BlockSpec((tm, dp), lambda t, j, te, nu: (t + tile0, 0)),
            scratch_shapes=[pltpu.VMEM((tm, d), BF16), pltpu.VMEM((tm, d), F32)]),
        input_output_aliases=aliases,
        compiler_params=_params("arbitrary", "arbitrary"),
        name="moe_grouped",
    )(*args)


def _combine_kernel(y1_ref, y2_ref, x_ref, g2_ref, rt_ref, o_ref):
    rt = rt_ref[...]
    w1, w2 = rt[:, 2:3], rt[:, 3:4]
    half = y1_ref.shape[1]
    for k, (a, c) in enumerate(zip(_unpack_bf16_pairs(y1_ref[...]), _unpack_bf16_pairs(y2_ref[...]))):
        cs = slice(k * half, (k + 1) * half)
        o_ref[:, cs] = x_ref[:, cs] + g2_ref[:, cs] * (w1 * a + w2 * c)


def _moe_combine(yg, x, g2, route):
    b, s, d = x.shape
    tm = _pick_tile(s, 512)
    tok = lambda n: pl.BlockSpec((None, tm, n), lambda bi, i: (bi, i, 0))
    return pl.pallas_call(
        _combine_kernel,
        out_shape=jax.ShapeDtypeStruct((b, s, d), F32),
        grid=(b, s // tm),
        in_specs=[tok(d // 2), pl.BlockSpec((None, tm, d // 2), lambda bi, i: (b + bi, i, 0)), tok(d),
                  pl.BlockSpec((None, 1, d), lambda bi, i: (bi, 0, 0)), tok(LANES)],
        out_specs=tok(d),
        compiler_params=_params("parallel", "parallel"),
        name="moe_combine",
    )(yg, yg, x, g2, route)


def _moe(h, route, x, g2, wg, wu, wd):
    b, s, d = x.shape
    n_tok = b * s
    tm = MOE_TM
    rt = route.reshape(n_tok, LANES)
    flat_e = jnp.concatenate([rt[:, 0], rt[:, 1]]).astype(jnp.int32)
    onehot = (flat_e[:, None] == jnp.arange(N_EXPERTS, dtype=jnp.int32)[None, :]).astype(jnp.int32)
    csum = jnp.cumsum(onehot, axis=0)
    rank = jnp.sum(csum * onehot, axis=1) - 1
    counts = csum[-1]
    tiles_e = (counts + tm - 1) // tm
    tiles_cum = jnp.cumsum(tiles_e)
    row_start = (tiles_cum - tiles_e) * tm
    dest = jnp.sum(onehot * row_start[None, :], axis=1) + rank
    n_tiles = 2 * n_tok // tm + N_EXPERTS
    tile_ids = jnp.arange(n_tiles, dtype=jnp.int32)
    tile_expert = jnp.sum((tile_ids[:, None] >= tiles_cum[None, :]).astype(jnp.int32), axis=1)
    last_e = jnp.max(jnp.where(tiles_e > 0, jnp.arange(N_EXPERTS, dtype=jnp.int32), 0))
    tile_expert = jnp.minimum(tile_expert, last_e).astype(jnp.int32)
    n_used = tiles_cum[-1:].astype(jnp.int32)
    dest = dest.astype(jnp.int32)
    row_src = _invert_rows(dest, n_tiles * tm)
    n_chunks = max(c for c in range(1, MOE_CHUNKS + 1) if n_tiles % c == 0)
    tiles_c = n_tiles // n_chunks
    h_flat = h.reshape(n_tok, d // 2)
    xs = [_gather_rows(row_src[c * tiles_c * tm:(c + 1) * tiles_c * tm], h_flat) for c in range(n_chunks)]
    y = None
    for c in range(n_chunks):
        y = _moe_grouped(tile_expert, n_used, xs[c], wg, wu, wd, y, c * tiles_c, n_tiles * tm)
    yg = _gather_rows(dest, y).reshape(2 * b, s, d // 2)
    return _moe_combine(yg, x, g2, route)


def _rope_tables(n):
    rows = n // GRID_W
    r = jnp.repeat(jnp.arange(rows), GRID_W).astype(F32)
    col = jnp.tile(jnp.arange(GRID_W), rows).astype(F32)
    freqs = ROPE_BASE ** (-jnp.arange(ROPE_FREQS, dtype=F32) / ROPE_FREQS)
    ang = jnp.stack([r[:, None] * freqs, col[:, None] * freqs], axis=1)
    ang = jnp.repeat(ang[:, :, None, :], 2, axis=2).reshape(n, HEAD_DIM)
    ang = jnp.tile(ang, (1, LANES // HEAD_DIM))
    cos, sin = jnp.cos(ang), jnp.sin(ang)
    first_half = (jnp.arange(LANES) % (2 * ROPE_FREQS)) < ROPE_FREQS
    return cos, jnp.where(first_half, -sin, 0.0), jnp.where(first_half, 0.0, sin)


def _lane_rows(lg):
    return jnp.repeat(lg.astype(F32), HEAD_DIM).reshape(RET_HEADS // 2, 1, LANES)


def kernel(x, c, ctx, c_ctx, ada_w, ada_b, norm1_w, norm2_w, w_in, w_out, q_norm_w, k_norm_w,
           attn_sink, ret_decay_f, ret_decay_b, ret_gn_w, conv_w, conv_b, conv_ln_w, conv_ln_b,
           ffn_w_gate, ffn_w_up, ffn_w_down, router_w, moe_w_gate, moe_w_up, moe_w_down):
    b, n, d = x.shape
    n_ctx = ctx.shape[1]
    depth = ada_w.shape[0]
    cond = jnp.zeros((SUBLANES, d), F32).at[0:b].set(c).at[b].set(c_ctx)
    mods = _adaln(cond, ada_w, ada_b).reshape(depth, SUBLANES, 6, d)
    cos, sa, sb = _rope_tables(n)
    ones_c = jnp.ones((n_ctx, LANES), F32)
    zeros_c = jnp.zeros((n_ctx, LANES), F32)
    zero_state = jnp.zeros((b, RET_HEADS // 2, LANES, LANES), F32)
    row = lambda v: v.reshape(1, -1)
    moe_f32 = [w.reshape(-1, w.shape[-1]) for w in (moe_w_gate, moe_w_up, moe_w_down)]
    cast_jobs = {l: [j for j in range(3) if j * min(depth, 2) // 3 == l] for l in range(depth)}
    moe_bf = [None, None, None]
    for l in range(depth):
        last = l == depth - 1
        m_lat = [mods[l, 0:b, k][:, None, :] for k in range(6)]
        m_ctx = [jnp.broadcast_to(mods[l, b, k][None, None, :], (b, 1, d)) for k in range(6)]
        w_in_bf = w_in[l].astype(BF16)
        w_out_bf = w_out[l].astype(BF16)
        qw = row(jnp.tile(q_norm_w[l], LANES // HEAD_DIM))
        kw = row(jnp.tile(k_norm_w[l], LANES // HEAD_DIM))
        lgf = _lane_rows(jax.nn.log_sigmoid(ret_decay_f[l].astype(F32)))
        lgb = _lane_rows(jax.nn.log_sigmoid(ret_decay_b[l].astype(F32)))
        sink_tab = jnp.repeat(attn_sink[l].astype(F32), ATT_BLOCK).reshape(ATT_Q_HEADS // 2, 2 * ATT_BLOCK)

        q, k, v, rk, rv, rq, rg, cv = _inproj(x, row(norm1_w[l]), m_lat[1], m_lat[0], w_in_bf,
                                               cos, sa, sb, qw, kw)
        qc, kc, vc, rkc, rvc, rqc, rgc, cvc = _inproj(ctx, row(norm1_w[l]), m_ctx[1], m_ctx[0], w_in_bf,
                                                       ones_c, zeros_c, zeros_c, qw, kw)
        rf_c, s_f = _ret_states(lgf, rkc, rvc, zero_state, reverse=False)
        rb_c, s_b = _ret_states(lgb, rkc, rvc, zero_state, reverse=True)
        rf, _ = _ret_states(lgf, rk, rv, s_f, reverse=False)
        rb, _ = _ret_states(lgb, rk, rv, s_b, reverse=True)

        att, cast_out = _attention(q, k, v, kc, vc, sink_tab, window=True,
                                   cast=[moe_f32[j] for j in cast_jobs[l]])
        for j, w_bf in zip(cast_jobs[l], cast_out):
            moe_bf[j] = w_bf.reshape((moe_w_gate, moe_w_up, moe_w_down)[j].shape)
        ret = _ret_out(lgf, lgb, rq, rk, rv, rg, rf, rb, row(ret_gn_w[l]))
        cnv = _conv(cv, conv_w[l], row(conv_b[l]), row(conv_ln_w[l]), row(conv_ln_b[l]))

        if l % 2 == 0:
            i = l // 2
            wg, wu, wd = ffn_w_gate[i].astype(BF16), ffn_w_up[i].astype(BF16), ffn_w_down[i].astype(BF16)
            x_mid, h2 = _outproj(att, ret, cnv, w_out_bf, x, m_lat[2], row(norm2_w[l]), m_lat[4], m_lat[3])
            x_new = _ffn(h2, wg, wu, wd, x_mid, m_lat[5])
        else:
            i = l // 2
            wg, wu, wd = moe_bf[0][i], moe_bf[1][i], moe_bf[2][i]
            x_mid, h2, route = _outproj(att, ret, cnv, w_out_bf, x, m_lat[2], row(norm2_w[l]),
                                        m_lat[4], m_lat[3], router=router_w[i])
            x_new = _moe(h2, route, x_mid, m_lat[5], wg, wu, wd)

        if not last:
            att_c, _ = _attention(qc, None, None, kc, vc, sink_tab, window=False)
            ret_c = _ret_out(lgf, lgb, rqc, rkc, rvc, rgc, rf_c, rb_c, row(ret_gn_w[l]))
            cnv_c = _conv(cvc, conv_w[l], row(conv_b[l]), row(conv_ln_w[l]), row(conv_ln_b[l]))
            if l % 2 == 0:
                c_mid, h2c = _outproj(att_c, ret_c, cnv_c, w_out_bf, ctx, m_ctx[2], row(norm2_w[l]),
                                      m_ctx[4], m_ctx[3])
                ctx = _ffn(h2c, wg, wu, wd, c_mid, m_ctx[5])
            else:
                c_mid, h2c, route_c = _outproj(att_c, ret_c, cnv_c, w_out_bf, ctx, m_ctx[2],
                                               row(norm2_w[l]), m_ctx[4], m_ctx[3],
                                               router=router_w[i])
                ctx = _moe(h2c, route_c, c_mid, m_ctx[5], wg, wu, wd)
        x = x_new
    return x
```

```python
import functools

import jax
import jax.numpy as jnp
from jax import lax
from jax.experimental import pallas as pl
from jax.experimental.pallas import tpu as pltpu
from jax.experimental.pallas import tpu_sc as plsc

F32 = jnp.float32
BF16 = jnp.bfloat16

GRID_W = 64
HEAD_DIM = 64
ATT_Q_HEADS = 8
ATT_KV_HEADS = 2
ATT_WINDOW = 128
ATT_BLOCK = 128
RET_HEADS = 4
RET_CHUNK = 128
RET_K_SCALE = HEAD_DIM ** -0.5
ATT_SCALE = HEAD_DIM ** -0.5
CONV_CH = 256
CONV_WIDTH = 31
ROPE_BASE = 10000.0
ROPE_FREQS = HEAD_DIM // 4
D_ATT = ATT_Q_HEADS * HEAD_DIM
D_RET = RET_HEADS * HEAD_DIM
ATT_KV_W = ATT_KV_HEADS * HEAD_DIM
C_ATT_K = 0
C_ATT_V = C_ATT_K + ATT_KV_W
C_RET_K = C_ATT_V + ATT_KV_W
C_RET_V = C_RET_K + D_RET
C_ATT_Q = C_RET_V + D_RET
C_RET_Q = C_ATT_Q + D_ATT
C_RET_G = C_RET_Q + D_RET
C_CONV = C_RET_G + D_RET
N_EXPERTS = 8
EPS = 1e-6
NEG_INF = -1e30

LANES = 128
SUBLANES = 8
BF16_SUBLANES = 16
VMEM_LIMIT = 48 * 1024 * 1024
CONV_HALO = 16
CONV_SUB = 32
CONV_CHAINS = 4
ATT_QBLOCKS = 4
RET_UNROLL = 4
MOE_TM = 512
MOE_CHUNKS = 2
MOE_SUB = 512
FFN_SUB = 512
MOE_TF = 1792
SC_LANES = 16
SC_WINDOW = 64


def _params(*sem):
    return pltpu.CompilerParams(dimension_semantics=sem, vmem_limit_bytes=VMEM_LIMIT)


def _sigmoid(x):
    return 1.0 / (1.0 + jnp.exp(-x))


def _pack_bf16_pairs(v):
    c = v.shape[1] // 2
    bits = pltpu.bitcast(v.astype(BF16).astype(F32), jnp.uint32)
    packed = (bits[:, c:] & jnp.uint32(0xFFFF0000)) | (bits[:, :c] >> 16)
    return pltpu.bitcast(packed, jnp.int32)


def _unpack_bf16_pairs(p):
    bits = pltpu.bitcast(p, jnp.uint32)
    return pltpu.bitcast(bits << 16, F32), pltpu.bitcast(bits & jnp.uint32(0xFFFF0000), F32)


def _pick_tile(n, pref):
    t = min(n, pref)
    assert n % t == 0, (n, t)
    return t


def _adaln_kernel(c_ref, w_ref, b_ref, o_ref):
    c = c_ref[...]
    s = c * _sigmoid(c)
    o_ref[...] = jnp.dot(s, w_ref[...], preferred_element_type=F32,
                         precision=lax.Precision.HIGHEST) + b_ref[...]


def _adaln(cond, ada_w, ada_b):
    depth, d, n = ada_w.shape
    tn = _pick_tile(n, 1536)
    return pl.pallas_call(
        _adaln_kernel,
        out_shape=jax.ShapeDtypeStruct((depth, cond.shape[0], n), F32),
        grid=(depth, n // tn),
        in_specs=[pl.BlockSpec(cond.shape, lambda l, j: (0, 0)),
                  pl.BlockSpec((None, d, tn), lambda l, j: (l, 0, j)),
                  pl.BlockSpec((None, 1, tn), lambda l, j: (l, 0, j))],
        out_specs=pl.BlockSpec((None, cond.shape[0], tn), lambda l, j: (l, 0, j)),
        compiler_params=_params("parallel", "parallel"),
        name="adaln",
    )(cond, ada_w, ada_b.reshape(depth, 1, n))


def _modulated_rms(x, nw, sc, sh):
    ms = jnp.mean(x * x, axis=-1, keepdims=True)
    return (x * lax.rsqrt(ms + EPS) * nw) * (1.0 + sc) + sh


def _head_group_matrix():
    r = lax.broadcasted_iota(jnp.int32, (2 * LANES, 2 * LANES), 0) // HEAD_DIM
    c = lax.broadcasted_iota(jnp.int32, (2 * LANES, 2 * LANES), 1) // HEAD_DIM
    return jnp.where(r == c, 1.0, 0.0).astype(BF16)


def _per_head_sum(v, grp):
    hi = v.astype(BF16)
    lo = (v - hi.astype(F32)).astype(BF16)
    r = jnp.dot(jnp.concatenate([hi, lo], axis=1), grp, preferred_element_type=F32)
    return r[:, 0:LANES] + r[:, LANES:]


def _dup_halves(t):
    sw = pltpu.roll(t, HEAD_DIM, 1)
    lo = lax.broadcasted_iota(jnp.int32, t.shape, 1) < HEAD_DIM
    return jnp.where(lo, t, sw), jnp.where(lo, sw, t)


def _inproj_kernel(x_ref, nw_ref, sc_ref, sh_ref, w_ref, cos_ref, sa_ref, sb_ref, qw_ref, kw_ref,
                   q_ref, k_ref, v_ref, rk_ref, rv_ref, rq_ref, rg_ref, cv_ref):
    hb = _modulated_rms(x_ref[...], nw_ref[...], sc_ref[...], sh_ref[...]).astype(BF16)

    def proj(c0, n):
        return jnp.dot(hb, w_ref[:, c0:c0 + n], preferred_element_type=F32)

    grp = _head_group_matrix()
    cos, sa, sb = cos_ref[...], sa_ref[...], sb_ref[...]

    def norm_rope(p, wrow):
        y = p * lax.rsqrt(_per_head_sum(p * p, grp) * (1.0 / HEAD_DIM) + EPS) * wrow
        return (y * cos + pltpu.roll(y, LANES - ROPE_FREQS, 1) * sa
                + pltpu.roll(y, ROPE_FREQS, 1) * sb)

    def put_kv(kv):
        k0, k1 = _dup_halves(norm_rope(kv[:, 0:ATT_KV_W], kw_ref[...]))
        k_ref[:, 0:LANES] = k0.astype(BF16)
        k_ref[:, LANES:2 * LANES] = k1.astype(BF16)
        v_ref[...] = kv[:, ATT_KV_W:].T.astype(BF16)

    def put_q(qall):
        qw = qw_ref[...] * ATT_SCALE
        for j in range(D_ATT // LANES):
            q_ref[:, j * LANES:(j + 1) * LANES] = norm_rope(qall[:, j * LANES:(j + 1) * LANES], qw).astype(BF16)

    def put_rk(p):
        rk_ref[...] = (p * RET_K_SCALE).astype(BF16)

    def put_rv(p):
        rv_ref[...] = p.astype(BF16)

    def put_rq(p):
        rq_ref[...] = p.astype(BF16)

    def put_rg(p):
        rg_ref[...] = p

    def put_conv(p):
        cv_ref[...] = p[:, 0:CONV_CH] * _sigmoid(p[:, CONV_CH:])

    units = [(C_ATT_K, 2 * ATT_KV_W, put_kv), (C_ATT_Q, D_ATT, put_q), (C_RET_K, D_RET, put_rk),
             (C_RET_V, D_RET, put_rv), (C_RET_Q, D_RET, put_rq), (C_RET_G, D_RET, put_rg),
             (C_CONV, 2 * CONV_CH, put_conv)]
    ahead = 2
    pending = [proj(c0, n) for c0, n, _ in units[:ahead]]
    for n, (_, _, put) in enumerate(units):
        if n + ahead < len(units):
            pending.append(proj(*units[n + ahead][:2]))
        put(pending.pop(0))


def _inproj(x, nw, sc, sh, w_bf, cos, sa, sb, qw, kw):
    b, s, d = x.shape
    tm = _pick_tile(s, 512)
    row = lambda n: pl.BlockSpec((1, n), lambda bi, i: (0, 0))
    per_b = pl.BlockSpec((None, 1, d), lambda bi, i: (bi, 0, 0))
    tab = pl.BlockSpec((tm, LANES), lambda bi, i: (i, 0))
    tok = lambda n: pl.BlockSpec((None, tm, n), lambda bi, i: (bi, i, 0))
    shp = lambda n, dt: jax.ShapeDtypeStruct((b, s, n), dt)
    return pl.pallas_call(
        _inproj_kernel,
        out_shape=(shp(D_ATT, BF16), shp(2 * LANES, BF16), jax.ShapeDtypeStruct((b, ATT_KV_W, s), BF16),
                   shp(D_RET, BF16), shp(D_RET, BF16), shp(D_RET, BF16), shp(D_RET, F32),
                   shp(CONV_CH, F32)),
        grid=(b, s // tm),
        in_specs=[tok(d), row(d), per_b, per_b,
                  pl.BlockSpec(w_bf.shape, lambda bi, i: (0, 0)),
                  tab, tab, tab, row(LANES), row(LANES)],
        out_specs=(tok(D_ATT), tok(2 * LANES), pl.BlockSpec((None, ATT_KV_W, tm), lambda bi, i: (bi, 0, i)),
                   tok(D_RET), tok(D_RET), tok(D_RET), tok(D_RET), tok(CONV_CH)),
        compiler_params=_params("parallel", "parallel"),
        name="inproj",
    )(x, nw, sc, sh, w_bf, cos, sa, sb, qw, kw)


def _attn_kernel(*refs, window, nq, n_cast):
    blk = ATT_BLOCK
    n_in = len(refs) - 1 - n_cast
    for src, dst in zip(refs[n_in - n_cast:n_in], refs[n_in + 1:]):
        dst[...] = src[...].astype(BF16)
    refs = refs[:n_in - n_cast] + refs[n_in:n_in + 1]
    if window:
        q_ref = refs[0]
        k_refs = refs[1:nq + 3]
        v_refs = refs[nq + 3:2 * nq + 5]
        kx_ref, vx_ref, sink_ref, o_ref = refs[2 * nq + 5:]
    else:
        q_ref, kx_ref, vx_ref, sink_ref, o_ref = refs
    n_ctx = kx_ref.shape[0]
    nk = 3 * blk + n_ctx if window else n_ctx
    if window:
        i = pl.program_id(1)
        last = nq * pl.num_programs(1) - 1
        key = lax.broadcasted_iota(jnp.int32, (blk, 2 * blk), 0)
        qry = lax.broadcasted_iota(jnp.int32, (blk, 2 * blk), 1) & (blk - 1)

        def band_masks(sub):
            off_prev = jnp.where(nq * i + sub > 0, 0, blk)
            off_next = jnp.where(nq * i + sub < last, 0, blk)
            return key >= qry + off_prev, key + off_next <= qry
    first_head = lax.broadcasted_iota(jnp.int32, (blk, LANES), 1) < HEAD_DIM
    ones = jnp.ones((2 * SUBLANES, nk), BF16)

    def scores(sub, pair):
        g = pair // 2
        gs = slice(g * LANES, (g + 1) * LANES)
        if window:
            kcat = jnp.concatenate([r[:, gs] for r in k_refs[sub:sub + 3]] + [kx_ref[:, gs]], axis=0)
        else:
            kcat = kx_ref[:, gs]
        qp = q_ref[sub * blk:(sub + 1) * blk, pair * LANES:(pair + 1) * LANES]
        zero = jnp.zeros_like(qp)
        w = jnp.concatenate([jnp.where(first_head, qp, zero), jnp.where(first_head, zero, qp)], axis=0)
        return lax.dot_general(kcat, w, (((1,), (1,)), ((), ())), preferred_element_type=F32)

    def masked_max(sub, pair, s):
        if window:
            mask_prev, mask_next = band_masks(sub)
            parts = [jnp.where(mask_prev, s[0:blk], NEG_INF), s[blk:2 * blk],
                     jnp.where(mask_next, s[2 * blk:3 * blk], NEG_INF), s[3 * blk:]]
        else:
            parts = [s]
        m = sink_ref[pair:pair + 1, :]
        for part in parts:
            m = jnp.maximum(m, jnp.max(part, axis=0, keepdims=True))
        return parts, m

    def softmax(sub, pair, parts_m):
        parts, m = parts_m
        p = jnp.concatenate([jnp.exp(part - m).astype(BF16) for part in parts], axis=0)
        return p, jnp.exp(sink_ref[pair:pair + 1, :] - m)

    def output(sub, pair, p, sink_p):
        g = pair // 2
        vs = slice(g * HEAD_DIM, (g + 1) * HEAD_DIM)
        if window:
            vt = jnp.concatenate([r[vs, :] for r in v_refs[sub:sub + 3]] + [vx_ref[vs, :]], axis=1)
        else:
            vt = vx_ref[vs, :]
        vaug = jnp.concatenate([vt, ones], axis=0)
        o = jnp.dot(vaug, p, preferred_element_type=F32)
        on = o[0:HEAD_DIM, :] * (1.0 / (o[HEAD_DIM:HEAD_DIM + 1, :] + sink_p))
        ot = jnp.concatenate([on[:, 0:blk], on[:, blk:2 * blk]], axis=0)
        o_ref[sub * blk:(sub + 1) * blk, pair * LANES:(pair + 1) * LANES] = ot.T.astype(BF16)

    units = [(sub, pair) for sub in range(nq) for pair in range(ATT_Q_HEADS // 2)]
    stages = (lambda sub, pair, _: scores(sub, pair), masked_max, softmax,
              lambda sub, pair, ps: output(sub, pair, *ps))
    vals = {}
    for step in range(len(units) + len(stages) - 1):
        for k, stage in enumerate(stages):
            u = step - k
            if 0 <= u < len(units):
                vals[u, k] = stage(*units[u], vals.pop((u, k - 1), None))


def _attention(q, k, vt, kx, vxt, sink_tab, window, cast=()):
    b, s, _ = q.shape
    blk = ATT_BLOCK
    nb = s // blk
    nq = ATT_QBLOCKS if nb % ATT_QBLOCKS == 0 else 1
    n_ctx = kx.shape[1]
    qspec = pl.BlockSpec((None, nq * blk, D_ATT), lambda bi, i: (bi, i, 0))
    kctx = pl.BlockSpec((None, n_ctx, 2 * LANES), lambda bi, i: (bi, 0, 0))
    vctx = pl.BlockSpec((None, ATT_KV_W, n_ctx), lambda bi, i: (bi, 0, 0))
    snk = pl.BlockSpec(sink_tab.shape, lambda bi, i: (0, 0))
    if window:
        at = lambda off: (lambda i: jnp.clip(nq * i + off, 0, nb - 1))
        kspec = lambda f: pl.BlockSpec((None, blk, 2 * LANES), lambda bi, i: (bi, f(i), 0))
        vspec = lambda f: pl.BlockSpec((None, ATT_KV_W, blk), lambda bi, i: (bi, 0, f(i)))
        offs = range(-1, nq + 1)
        in_specs = ([qspec] + [kspec(at(o)) for o in offs] + [vspec(at(o)) for o in offs]
                    + [kctx, vctx, snk])
        args = (q,) + (k,) * (nq + 2) + (vt,) * (nq + 2) + (kx, vxt, sink_tab)
    else:
        in_specs = [qspec, kctx, vctx, snk]
        args = (q, kx, vxt, sink_tab)
    steps = nb // nq
    out_shape, out_specs = [jax.ShapeDtypeStruct((b, s, D_ATT), BF16)], [qspec]
    for w in cast:
        rows = w.shape[0] // (b * steps)
        assert rows * b * steps == w.shape[0] and rows % (2 * SUBLANES) == 0, w.shape
        slab = pl.BlockSpec((rows, w.shape[1]), lambda bi, i: (bi * steps + i, 0))
        in_specs.append(slab)
        out_specs.append(slab)
        out_shape.append(jax.ShapeDtypeStruct(w.shape, BF16))
    res = pl.pallas_call(
        functools.partial(_attn_kernel, window=window, nq=nq, n_cast=len(cast)),
        out_shape=tuple(out_shape),
        grid=(b, steps),
        in_specs=in_specs,
        out_specs=tuple(out_specs),
        compiler_params=_params("parallel", "parallel"),
        name="attention_window" if window else "attention_ctx",
    )(*args, *cast)
    return res[0], tuple(res[1:])


def _ret_state_kernel(lgl_ref, k_ref, v_ref, s0_ref, r_ref, fin_ref, s_scr, *, cb, reverse):
    c = pl.program_id(1)
    L = RET_CHUNK

    @pl.when(c == 0)
    def _():
        s_scr[...] = s0_ref[...]

    jj = lax.broadcasted_iota(jnp.int32, (L, LANES), 0).astype(F32)
    expo = jj if reverse else (L - 1.0) - jj
    same_head = (lax.broadcasted_iota(jnp.int32, (LANES, LANES), 0) // HEAD_DIM
                 == lax.broadcasted_iota(jnp.int32, (LANES, LANES), 1) // HEAD_DIM)
    for pr in range(RET_HEADS // 2):
        cs = slice(pr * LANES, (pr + 1) * LANES)
        lgl = lgl_ref[pr]
        kdec = jnp.exp(expo * lgl)
        cdec = jnp.exp(float(L) * lgl)
        state = s_scr[pr]
        for t in range(cb):
            cc = cb - 1 - t if reverse else t
            rows = slice(cc * L, (cc + 1) * L)
            r_ref[cc, pr] = state.astype(BF16)
            kd = k_ref[rows, cs].astype(F32) * kdec
            u = jnp.dot(kd.T.astype(BF16), v_ref[rows, cs], preferred_element_type=F32)
            state = cdec * state + jnp.where(same_head, u, 0.0)
        s_scr[pr] = state

    @pl.when(c == pl.num_programs(1) - 1)
    def _():
        fin_ref[...] = s_scr[...]


def _ret_states(lgl, rk, rv, s0, reverse):
    b, s, _ = rk.shape
    nc = s // RET_CHUNK
    cb = _pick_tile(nc, 8)
    nblk = nc // cb
    npair = RET_HEADS // 2
    blk_idx = (lambda c: nblk - 1 - c) if reverse else (lambda c: c)
    tok = pl.BlockSpec((None, cb * RET_CHUNK, D_RET), lambda bi, c: (bi, blk_idx(c), 0))
    st = pl.BlockSpec((None, npair, LANES, LANES), lambda bi, c: (bi, 0, 0, 0))
    return pl.pallas_call(
        functools.partial(_ret_state_kernel, cb=cb, reverse=reverse),
        out_shape=(jax.ShapeDtypeStruct((b, nc, npair, LANES, LANES), BF16),
                   jax.ShapeDtypeStruct((b, npair, LANES, LANES), F32)),
        grid=(b, nblk),
        in_specs=[pl.BlockSpec(lgl.shape, lambda bi, c: (0, 0, 0)), tok, tok, st],
        out_specs=(pl.BlockSpec((None, cb, npair, LANES, LANES),
                                lambda bi, c: (bi, blk_idx(c), 0, 0, 0)), st),
        scratch_shapes=[pltpu.VMEM((npair, LANES, LANES), F32)],
        compiler_params=_params("parallel", "arbitrary"),
        name="ret_state_bwd" if reverse else "ret_state_fwd",
    )(lgl, rk, rv, s0)


def _ret_out_kernel(lgf_ref, lgb_ref, q_ref, k_ref, v_ref, g_ref, rf_ref, rb_ref, gnw_ref, o_ref, *, cb):
    L = RET_CHUNK
    i0 = lax.broadcasted_iota(jnp.int32, (L, LANES), 0).astype(F32)
    i1 = lax.broadcasted_iota(jnp.int32, (L, LANES), 1).astype(F32)
    diff = i0 - i1
    lo = lax.broadcasted_iota(jnp.int32, (L, LANES), 1) < HEAD_DIM
    grp = _head_group_matrix()
    inv = 1.0 / HEAD_DIM
    tabs = []
    for pr in range(RET_HEADS // 2):
        lgf, lgb = lgf_ref[pr], lgb_ref[pr]
        dmat = [jnp.where(diff >= 0.0,
                          jnp.exp(jnp.maximum(diff, 0.0) * lgf[:, a:a + 1]),
                          jnp.exp(jnp.maximum(-diff, 0.0) * lgb[:, a:a + 1]))
                for a in (0, HEAD_DIM)]
        tabs.append((jnp.exp((i0 + 1.0) * lgf), jnp.exp((float(L) - i0) * lgb),
                     jnp.concatenate(dmat, axis=0), gnw_ref[:, pr * LANES:(pr + 1) * LANES]))

    def decayed_scores(t, pr):
        rows = pl.ds(pl.multiple_of(t * L, L), L)
        cs = slice(pr * LANES, (pr + 1) * LANES)
        qp, kp = q_ref[rows, cs], k_ref[rows, cs]
        zero = jnp.zeros_like(kp)
        qz = jnp.concatenate([jnp.where(lo, qp, zero), jnp.where(lo, zero, qp)], axis=0)
        return lax.dot_general(qz, kp, (((1,), (1,)), ((), ())), preferred_element_type=F32) * tabs[pr][2]

    def mix(t, pr, a):
        rows = pl.ds(pl.multiple_of(t * L, L), L)
        cs = slice(pr * LANES, (pr + 1) * LANES)
        xif, xib, _, _ = tabs[pr]
        vp = v_ref[rows, cs]
        zero = jnp.zeros_like(vp)
        qf = q_ref[rows, cs].astype(F32)
        lhs = jnp.concatenate([a[0:L].astype(BF16), a[L:].astype(BF16),
                               (qf * xif).astype(BF16), (qf * xib).astype(BF16)], axis=1)
        rhs = jnp.concatenate([jnp.where(lo, vp, zero), jnp.where(lo, zero, vp),
                               rf_ref[t, pr], rb_ref[t, pr]], axis=0)
        return jnp.dot(lhs, rhs, preferred_element_type=F32)

    def centre(t, pr, acc):
        return acc - _per_head_sum(acc, grp) * inv

    def readout(t, pr, dl):
        rows = pl.ds(pl.multiple_of(t * L, L), L)
        cs = slice(pr * LANES, (pr + 1) * LANES)
        var = _per_head_sum(dl * dl, grp) * inv
        y = dl * lax.rsqrt(var + EPS) * tabs[pr][3]
        gt = g_ref[rows, cs]
        o_ref[rows, cs] = (gt * _sigmoid(gt) * y).astype(BF16)

    unroll = min(RET_UNROLL, cb)
    stages = (lambda t, pr, _: decayed_scores(t, pr), mix, centre, readout)

    def group(gi, carry):
        units = [(gi * unroll + u, pr) for u in range(unroll) for pr in range(RET_HEADS // 2)]
        vals = {}
        for step in range(len(units) + len(stages) - 1):
            for k, stage in enumerate(stages):
                u = step - k
                if 0 <= u < len(units):
                    vals[u, k] = stage(*units[u], vals.pop((u, k - 1), None))
        return carry

    lax.fori_loop(0, cb // unroll, group, 0)


def _ret_out(lgf, lgb, rq, rk, rv, rg, rf, rb, gnw):
    b, s, _ = rq.shape
    nc = s // RET_CHUNK
    cb = _pick_tile(nc, 8)
    npair = RET_HEADS // 2
    tok = pl.BlockSpec((None, cb * RET_CHUNK, D_RET), lambda bi, c: (bi, c, 0))
    st = pl.BlockSpec((None, cb, npair, LANES, LANES), lambda bi, c: (bi, c, 0, 0, 0))
    lg = pl.BlockSpec(lgf.shape, lambda bi, c: (0, 0, 0))
    return pl.pallas_call(
        functools.partial(_ret_out_kernel, cb=cb),
        out_shape=jax.ShapeDtypeStruct((b, s, D_RET), BF16),
        grid=(b, nc // cb),
        in_specs=[lg, lg, tok, tok, tok, tok, st, st, pl.BlockSpec((1, D_RET), lambda bi, c: (0, 0))],
        out_specs=tok,
        compiler_params=_params("parallel", "parallel"),
        name="ret_out",
    )(lgf, lgb, rq, rk, rv, rg, rf, rb, gnw)


def _conv_kernel(prev_ref, cur_ref, next_ref, w_ref, b_ref, lnw_ref, lnb_ref, o_ref,
                 xpad, shifted, hbuf, wtile):
    i = pl.program_id(1)
    tm = cur_ref.shape[0]
    halo = CONV_HALO
    first = i == 0
    last = i == pl.num_programs(1) - 1
    xpad[0:halo, :] = jnp.where(first, 0.0, prev_ref[...])
    xpad[halo:halo + tm, :] = cur_ref[...]
    xpad[halo + tm:2 * halo + tm, :] = jnp.where(last, 0.0, next_ref[...])
    span = tm + 2 * halo - SUBLANES
    for r in range(SUBLANES):
        shifted[r] = xpad[r:r + span, :]
    base = halo - CONV_WIDTH // 2
    bias = b_ref[...]
    ch = cur_ref.shape[1]
    for w in range(CONV_WIDTH):
        wtile[w] = jnp.broadcast_to(w_ref[w:w + 1, :], (SUBLANES, ch))

    taps_by_shift = {}
    for w in range(CONV_WIDTH):
        taps_by_shift.setdefault((base + w) % SUBLANES, []).append(((base + w) // SUBLANES, w))

    def sub(sb, carry):
        r0 = pl.multiple_of(sb * CONV_SUB, CONV_SUB)
        chains = [None] * CONV_CHAINS
        groups = CONV_SUB // SUBLANES
        for r, taps in sorted(taps_by_shift.items()):
            a_lo = min(a for a, _ in taps)
            n_g = max(a for a, _ in taps) - a_lo + groups
            win = shifted[r, pl.ds(r0 + a_lo * SUBLANES, n_g * SUBLANES), :].reshape(n_g, SUBLANES, ch)
            for a, w in taps:
                term = win[a - a_lo:a - a_lo + groups] * wtile[w]
                c = w % CONV_CHAINS
                chains[c] = term if chains[c] is None else chains[c] + term
        total = (chains[0] + chains[1]) + (chains[2] + chains[3])
        hbuf[pl.ds(r0, CONV_SUB), :] = total.reshape(CONV_SUB, ch) + bias
        return carry

    lax.fori_loop(0, tm // CONV_SUB, sub, 0)
    h = hbuf[...]
    mu = jnp.mean(h, axis=-1, keepdims=True)
    dl = h - mu
    var = jnp.mean(dl * dl, axis=-1, keepdims=True)
    y = dl * lax.rsqrt(var + EPS) * lnw_ref[...] + lnb_ref[...]
    o_ref[...] = (y * _sigmoid(y)).astype(BF16)


def _conv(cv, w, bias, lnw, lnb):
    b, s, ch = cv.shape
    tm = _pick_tile(s, 512)
    hpb = tm // CONV_HALO
    nh = s // CONV_HALO
    row = pl.BlockSpec((1, ch), lambda bi, i: (0, 0))
    span = tm + 2 * CONV_HALO - SUBLANES
    return pl.pallas_call(
        _conv_kernel,
        out_shape=jax.ShapeDtypeStruct((b, s, ch), BF16),
        grid=(b, s // tm),
        in_specs=[pl.BlockSpec((None, CONV_HALO, ch), lambda bi, i: (bi, jnp.maximum(i * hpb - 1, 0), 0)),
                  pl.BlockSpec((None, tm, ch), lambda bi, i: (bi, i, 0)),
                  pl.BlockSpec((None, CONV_HALO, ch),
                               lambda bi, i: (bi, jnp.minimum((i + 1) * hpb, nh - 1), 0)),
                  pl.BlockSpec(w.shape, lambda bi, i: (0, 0)), row, row, row],
        out_specs=pl.BlockSpec((None, tm, ch), lambda bi, i: (bi, i, 0)),
        scratch_shapes=[pltpu.VMEM((tm + 2 * CONV_HALO, ch), F32),
                        pltpu.VMEM((SUBLANES, span, ch), F32),
                        pltpu.VMEM((tm, ch), F32),
                        pltpu.VMEM((CONV_WIDTH, SUBLANES, ch), F32)],
        compiler_params=_params("parallel", "parallel"),
        name="conv",
    )(cv, cv, cv, w, bias, lnw, lnb)


def _outproj_kernel(*refs, route):
    if route:
        (att_ref, ret_ref, cnv_ref, w_ref, x_ref, g1_ref, nw_ref, sc_ref, sh_ref, rcat_ref,
         xo_ref, h_ref, rt_ref) = refs
    else:
        att_ref, ret_ref, cnv_ref, w_ref, x_ref, g1_ref, nw_ref, sc_ref, sh_ref, xo_ref, h_ref = refs
    tm = x_ref.shape[0]
    n_part = 2 if (not route and tm % (2 * BF16_SUBLANES) == 0) else 1
    rows_of = lambda part: slice(part * (tm // n_part), (part + 1) * (tm // n_part))

    def mixer_out(part, _):
        rows = rows_of(part)
        return (jnp.dot(att_ref[rows, :], w_ref[0:D_ATT, :], preferred_element_type=F32)
                + jnp.dot(ret_ref[rows, :], w_ref[D_ATT:D_ATT + D_RET, :], preferred_element_type=F32)
                + jnp.dot(cnv_ref[rows, :], w_ref[D_ATT + D_RET:, :], preferred_element_type=F32))

    def residual_norm(part, y):
        rows = rows_of(part)
        xn = x_ref[rows, :] + g1_ref[...] * y
        xo_ref[rows, :] = xn
        h = _modulated_rms(xn, nw_ref[...], sc_ref[...], sh_ref[...])
        h_ref[rows, :] = _pack_bf16_pairs(h) if route else h.astype(BF16)
        if not route:
            return None
        hi = h.astype(BF16)
        lo = (h - hi.astype(F32)).astype(BF16)
        return jnp.dot(jnp.concatenate([hi, lo], axis=0), rcat_ref[...], preferred_element_type=F32)

    def top2(part, r):
        n = r.shape[0] // 2
        logits = (r[0:n, 0:LANES] + r[0:n, LANES:]) + (r[n:, 0:LANES] + r[n:, LANES:])
        lane = lax.broadcasted_iota(jnp.int32, logits.shape, 1).astype(F32)
        logits = jnp.where(lane < N_EXPERTS, logits, NEG_INF)
        m1 = jnp.max(logits, axis=-1, keepdims=True)
        i1 = jnp.min(jnp.where(logits == m1, lane, float(LANES)), axis=-1, keepdims=True)
        rest = jnp.where(lane == i1, NEG_INF, logits)
        m2 = jnp.max(rest, axis=-1, keepdims=True)
        i2 = jnp.min(jnp.where(rest == m2, lane, float(LANES)), axis=-1, keepdims=True)
        e2 = jnp.exp(m2 - m1)
        w1 = 1.0 / (1.0 + e2)
        w2 = e2 / (1.0 + e2)
        rt_ref[rows_of(part), :] = jnp.where(
            lane == 0.0, i1, jnp.where(lane == 1.0, i2,
                                       jnp.where(lane == 2.0, w1, jnp.where(lane == 3.0, w2, 0.0))))

    finish = (lambda part, y: top2(part, residual_norm(part, y))) if route else residual_norm
    stages = (mixer_out, finish)
    vals = {}
    for step in range(n_part + len(stages) - 1):
        for k, stage in enumerate(stages):
            u = step - k
            if 0 <= u < n_part:
                vals[u, k] = stage(u, vals.pop((u, k - 1), None))


def _outproj(att, ret, cnv, w_bf, x, g1, nw, sc, sh, router=None):
    b, s, d = x.shape
    tm = _pick_tile(s, 512)
    tok = lambda n: pl.BlockSpec((None, tm, n), lambda bi, i: (bi, i, 0))
    per_b = pl.BlockSpec((None, 1, d), lambda bi, i: (bi, 0, 0))
    row = pl.BlockSpec((1, d), lambda bi, i: (0, 0))
    in_specs = [tok(D_ATT), tok(D_RET), tok(CONV_CH), pl.BlockSpec(w_bf.shape, lambda bi, i: (0, 0)),
                tok(d), per_b, row, per_b, per_b]
    args = [att, ret, cnv, w_bf, x, g1, nw, sc, sh]
    h_shape = (b, s, d // 2) if router is not None else (b, s, d)
    out_shape = [jax.ShapeDtypeStruct((b, s, d), F32),
                 jax.ShapeDtypeStruct(h_shape, jnp.int32 if router is not None else BF16)]
    out_specs = [tok(d), tok(h_shape[2])]
    if router is not None:
        rpad = jnp.zeros((d, LANES), F32).at[:, :N_EXPERTS].set(router)
        rhi = rpad.astype(BF16)
        rlo = (rpad - rhi.astype(F32)).astype(BF16)
        in_specs += [pl.BlockSpec((d, 2 * LANES), lambda bi, i: (0, 0))]
        args += [jnp.concatenate([rhi, rlo], axis=1)]
        out_shape.append(jax.ShapeDtypeStruct((b, s, LANES), F32))
        out_specs.append(tok(LANES))
    return pl.pallas_call(
        functools.partial(_outproj_kernel, route=router is not None),
        out_shape=tuple(out_shape),
        grid=(b, s // tm),
        in_specs=in_specs,
        out_specs=tuple(out_specs),
        compiler_params=_params("parallel", "parallel"),
        name="outproj_route" if router is not None else "outproj",
    )(*args)


def _swiglu_chunks(h, wg_ref, wu_ref, wd_ref, sub):
    total = None
    for c0 in range(0, wg_ref.shape[1], sub):
        cs = slice(c0, min(c0 + sub, wg_ref.shape[1]))
        gate = jnp.dot(h, wg_ref[:, cs].astype(BF16), preferred_element_type=F32)
        up = jnp.dot(h, wu_ref[:, cs].astype(BF16), preferred_element_type=F32)
        act = (gate * _sigmoid(gate) * up).astype(BF16)
        part = jnp.dot(act, wd_ref[cs, :].astype(BF16), preferred_element_type=F32)
        total = part if total is None else total + part
    return total


def _ffn_kernel(h_ref, wg_ref, wu_ref, wd_ref, x_ref, g2_ref, o_ref):
    y = _swiglu_chunks(h_ref[...], wg_ref, wu_ref, wd_ref, FFN_SUB)
    o_ref[...] = x_ref[...] + g2_ref[...] * y


def _ffn(h, wg, wu, wd, x, g2):
    b, s, d = x.shape
    f = wg.shape[1]
    tm = _pick_tile(s, 512)
    tok = lambda: pl.BlockSpec((None, tm, d), lambda bi, i: (bi, i, 0))
    res = lambda shape: pl.BlockSpec(shape, lambda bi, i: (0, 0), pipeline_mode=pl.Buffered(1))
    return pl.pallas_call(
        _ffn_kernel,
        out_shape=jax.ShapeDtypeStruct((b, s, d), F32),
        grid=(b, s // tm),
        in_specs=[tok(), res((d, f)), res((d, f)), res((f, d)), tok(),
                  pl.BlockSpec((None, 1, d), lambda bi, i: (bi, 0, 0))],
        out_specs=tok(),
        compiler_params=_params("parallel", "parallel"),
        name="ffn_dense",
    )(h, wg, wu, wd, x, g2)


def _gather_rows(idx, src):
    n = idx.shape[0]
    d = src.shape[1]
    mesh = plsc.VectorSubcoreMesh(core_axis_name="core", subcore_axis_name="subcore")

    @pl.kernel(out_type=jax.ShapeDtypeStruct((n, d), src.dtype), mesh=mesh, name="moe_gather")
    def gather(src_hbm, idx_hbm, out_hbm):
        def body(idx_vmem, out_vmem):
            pltpu.sync_copy(src_hbm.at[idx_vmem.at[0, pl.ds(0, SC_WINDOW)]], out_vmem)

        pltpu.emit_pipeline(
            body,
            grid=(n // SC_WINDOW,),
            in_specs=[pl.BlockSpec((1, LANES), lambda i: (i, 0))],
            out_specs=[pl.BlockSpec((SC_WINDOW, d), lambda i: (i, 0))],
            core_axis_name=("core", "subcore"),
            dimension_semantics=(pltpu.PARALLEL,),
        )(idx_hbm, out_hbm)

    idx_rows = jnp.pad(idx.reshape(n // SC_WINDOW, SC_WINDOW), ((0, 0), (0, LANES - SC_WINDOW)))
    return gather(src, idx_rows)


def _invert_rows(dest, n_rows):
    n = dest.shape[0]
    n_tok = n // 2
    mesh = plsc.VectorSubcoreMesh(core_axis_name="core", subcore_axis_name="subcore")

    @pl.kernel(out_type=jax.ShapeDtypeStruct((n_rows,), jnp.int32), mesh=mesh, name="moe_row_src",
               scratch_types=[pltpu.VMEM((n,), jnp.int32), pltpu.VMEM((n_rows,), jnp.int32)],
               compiler_params=pltpu.CompilerParams(needs_layout_passes=False))
    def invert(dest_hbm, out_hbm, dest_vmem, rows_vmem):
        @pl.when((lax.axis_index("core") == 0) & (lax.axis_index("subcore") == 0))
        def _():
            pltpu.sync_copy(dest_hbm, dest_vmem)

            @pl.loop(0, n_rows, step=SC_LANES)
            def _(r):
                rows_vmem[pl.ds(r, SC_LANES)] = lax.rem(lax.iota(jnp.int32, SC_LANES) + r, n_tok)

            @pl.loop(0, n, step=SC_LANES)
            def _(a):
                tok = lax.iota(jnp.int32, SC_LANES) + a
                tok = jnp.where(tok >= n_tok, tok - n_tok, tok)
                plsc.store_scatter(rows_vmem, [dest_vmem[pl.ds(a, SC_LANES)]], tok)

            pltpu.sync_copy(rows_vmem, out_hbm)

    return invert(dest)


def _moe_kernel(te_ref, nu_ref, x_ref, wg_ref, wu_ref, wd_ref, *rest, tile0, nj):
    o_ref, xb, acc = rest[-3:]
    t = pl.program_id(0) + tile0
    j = pl.program_id(1)
    used = t < nu_ref[0]

    def partial_sum():
        return _swiglu_chunks(xb[...], wg_ref, wu_ref, wd_ref, MOE_SUB)

    @pl.when(used & (j == 0))
    def _():
        lo, hi = _unpack_bf16_pairs(x_ref[...])
        half = lo.shape[1]
        xb[:, 0:half] = lo.astype(BF16)
        xb[:, half:] = hi.astype(BF16)
        if nj == 1:
            o_ref[...] = _pack_bf16_pairs(partial_sum())
        else:
            acc[...] = partial_sum()

    if nj > 2:
        @pl.when(used & (j > 0) & (j < nj - 1))
        def _():
            acc[...] += partial_sum()

    if nj > 1:
        @pl.when(used & (j == nj - 1))
        def _():
            o_ref[...] = _pack_bf16_pairs(acc[...] + partial_sum())

    @pl.when(jnp.logical_not(used) & (j == nj - 1))
    def _():
        o_ref[...] = jnp.zeros_like(o_ref)


def _moe_grouped(tile_expert, n_used, xs, wg, wu, wd, y_prev, tile0, n_rows):
    p, dp = xs.shape
    d = 2 * dp
    f = wg.shape[2]
    tm = MOE_TM
    tf = MOE_TF
    nj = f // tf

    def jj(t, j, te, nu):
        return jnp.where(t + tile0 < nu[0], j, nj - 1)

    in_specs = [pl.BlockSpec((tm, dp), lambda t, j, te, nu: (t, 0)),
                pl.BlockSpec((None, d, tf), lambda t, j, te, nu: (te[t + tile0], 0, jj(t, j, te, nu))),
                pl.BlockSpec((None, d, tf), lambda t, j, te, nu: (te[t + tile0], 0, jj(t, j, te, nu))),
                pl.BlockSpec((None, tf, d), lambda t, j, te, nu: (te[t + tile0], jj(t, j, te, nu), 0))]
    args = [tile_expert, n_used, xs, wg, wu, wd]
    aliases = {}
    if y_prev is not None:
        in_specs.append(pl.BlockSpec(memory_space=pl.ANY))
        args.append(y_prev)
        aliases = {len(args) - 1: 0}
    return pl.pallas_call(
        functools.partial(_moe_kernel, tile0=tile0, nj=nj),
        out_shape=jax.ShapeDtypeStruct((n_rows, dp), jnp.int32),
        grid_spec=pltpu.PrefetchScalarGridSpec(
            num_scalar_prefetch=2,
            grid=(p // tm, nj),
            in_specs=in_specs,
            out_specs=pl.BlockSpec((tm, dp), lambda t, j, te, nu: (t + tile0, 0)),
            scratch_shapes=[pltpu.VMEM((tm, d), BF16), pltpu.VMEM((tm, d), F32)]),
        input_output_aliases=aliases,
        compiler_params=_params("arbitrary", "arbitrary"),
        name="moe_grouped",
    )(*args)


def _combine_kernel(y1_ref, y2_ref, x_ref, g2_ref, rt_ref, *rest):
    o_ref = rest[-1]
    rt = rt_ref[...]
    w1, w2 = rt[:, 2:3], rt[:, 3:4]
    half = y1_ref.shape[1]
    for k, (a, c) in enumerate(zip(_unpack_bf16_pairs(y1_ref[...]), _unpack_bf16_pairs(y2_ref[...]))):
        cs = slice(k * half, (k + 1) * half)
        o_ref[:, cs] = x_ref[:, cs] + g2_ref[:, cs] * (w1 * a + w2 * c)


def _moe_combine(yg, x, g2, route, out_prev, bi):
    b, s, d = x.shape
    tm = _pick_tile(s, 512)
    tok = lambda n: pl.BlockSpec((None, tm, n), lambda i: (bi, i, 0))
    in_specs = [pl.BlockSpec((None, tm, d // 2), lambda i: (0, i, 0)),
                pl.BlockSpec((None, tm, d // 2), lambda i: (1, i, 0)), tok(d),
                pl.BlockSpec((None, 1, d), lambda i: (bi, 0, 0)), tok(LANES)]
    args = [yg, yg, x, g2, route]
    aliases = {}
    if out_prev is not None:
        in_specs.append(pl.BlockSpec(memory_space=pl.ANY))
        args.append(out_prev)
        aliases = {len(args) - 1: 0}
    return pl.pallas_call(
        _combine_kernel,
        out_shape=jax.ShapeDtypeStruct((b, s, d), F32),
        grid=(s // tm,),
        in_specs=in_specs,
        out_specs=tok(d),
        input_output_aliases=aliases,
        compiler_params=_params("parallel"),
        name="moe_combine",
    )(*args)


def _moe(h, route, x, g2, wg, wu, wd):
    b, s, d = x.shape
    n_tok = b * s
    tm = MOE_TM
    rt = route.reshape(n_tok, LANES)
    flat_e = jnp.concatenate([rt[:, 0], rt[:, 1]]).astype(jnp.int32)
    onehot = (flat_e[:, None] == jnp.arange(N_EXPERTS, dtype=jnp.int32)[None, :]).astype(jnp.int32)
    csum = jnp.cumsum(onehot, axis=0)
    rank = jnp.sum(csum * onehot, axis=1) - 1
    counts = csum[-1]
    tiles_e = (counts + tm - 1) // tm
    tiles_cum = jnp.cumsum(tiles_e)
    row_start = (tiles_cum - tiles_e) * tm
    dest = jnp.sum(onehot * row_start[None, :], axis=1) + rank
    n_tiles = 2 * n_tok // tm + N_EXPERTS
    tile_ids = jnp.arange(n_tiles, dtype=jnp.int32)
    tile_expert = jnp.sum((tile_ids[:, None] >= tiles_cum[None, :]).astype(jnp.int32), axis=1)
    last_e = jnp.max(jnp.where(tiles_e > 0, jnp.arange(N_EXPERTS, dtype=jnp.int32), 0))
    tile_expert = jnp.minimum(tile_expert, last_e).astype(jnp.int32)
    n_used = tiles_cum[-1:].astype(jnp.int32)
    dest = dest.astype(jnp.int32)
    row_src = _invert_rows(dest, n_tiles * tm)
    n_chunks = max(c for c in range(1, MOE_CHUNKS + 1) if n_tiles % c == 0)
    tiles_c = n_tiles // n_chunks
    h_flat = h.reshape(n_tok, d // 2)
    xs = [_gather_rows(row_src[c * tiles_c * tm:(c + 1) * tiles_c * tm], h_flat) for c in range(n_chunks)]
    y = None
    for c in range(n_chunks):
        y = _moe_grouped(tile_expert, n_used, xs[c], wg, wu, wd, y, c * tiles_c, n_tiles * tm)
    dest2 = dest.reshape(2, b, s)
    ygs = [_gather_rows(dest2[:, bi].reshape(-1), y).reshape(2, s, d // 2) for bi in range(b)]
    out = None
    for bi in range(b):
        out = _moe_combine(ygs[bi], x, g2, route, out, bi)
    return out


def _rope_tables(n):
    rows = n // GRID_W
    r = jnp.repeat(jnp.arange(rows), GRID_W).astype(F32)
    col = jnp.tile(jnp.arange(GRID_W), rows).astype(F32)
    freqs = ROPE_BASE ** (-jnp.arange(ROPE_FREQS, dtype=F32) / ROPE_FREQS)
    ang = jnp.stack([r[:, None] * freqs, col[:, None] * freqs], axis=1)
    ang = jnp.repeat(ang[:, :, None, :], 2, axis=2).reshape(n, HEAD_DIM)
    ang = jnp.tile(ang, (1, LANES // HEAD_DIM))
    cos, sin = jnp.cos(ang), jnp.sin(ang)
    first_half = (jnp.arange(LANES) % (2 * ROPE_FREQS)) < ROPE_FREQS
    return cos, jnp.where(first_half, -sin, 0.0), jnp.where(first_half, 0.0, sin)


def _lane_rows(lg):
    return jnp.repeat(lg.astype(F32), HEAD_DIM).reshape(RET_HEADS // 2, 1, LANES)


def kernel(x, c, ctx, c_ctx, ada_w, ada_b, norm1_w, norm2_w, w_in, w_out, q_norm_w, k_norm_w,
           attn_sink, ret_decay_f, ret_decay_b, ret_gn_w, conv_w, conv_b, conv_ln_w, conv_ln_b,
           ffn_w_gate, ffn_w_up, ffn_w_down, router_w, moe_w_gate, moe_w_up, moe_w_down):
    b, n, d = x.shape
    n_ctx = ctx.shape[1]
    depth = ada_w.shape[0]
    cond = jnp.zeros((SUBLANES, d), F32).at[0:b].set(c).at[b].set(c_ctx)
    mods = _adaln(cond, ada_w, ada_b).reshape(depth, SUBLANES, 6, d)
    cos, sa, sb = _rope_tables(n)
    ones_c = jnp.ones((n_ctx, LANES), F32)
    zeros_c = jnp.zeros((n_ctx, LANES), F32)
    zero_state = jnp.zeros((b, RET_HEADS // 2, LANES, LANES), F32)
    row = lambda v: v.reshape(1, -1)
    moe_f32 = [w.reshape(-1, w.shape[-1]) for w in (moe_w_gate, moe_w_up, moe_w_down)]
    cast_jobs = {l: [j for j in range(3) if j * min(depth, 2) // 3 == l] for l in range(depth)}
    moe_bf = [None, None, None]
    for l in range(depth):
        last = l == depth - 1
        m_lat = [mods[l, 0:b, k][:, None, :] for k in range(6)]
        m_ctx = [jnp.broadcast_to(mods[l, b, k][None, None, :], (b, 1, d)) for k in range(6)]
        w_in_bf = w_in[l].astype(BF16)
        w_out_bf = w_out[l].astype(BF16)
        qw = row(jnp.tile(q_norm_w[l], LANES // HEAD_DIM))
        kw = row(jnp.tile(k_norm_w[l], LANES // HEAD_DIM))
        lgf = _lane_rows(jax.nn.log_sigmoid(ret_decay_f[l].astype(F32)))
        lgb = _lane_rows(jax.nn.log_sigmoid(ret_decay_b[l].astype(F32)))
        sink_tab = jnp.repeat(attn_sink[l].astype(F32), ATT_BLOCK).reshape(ATT_Q_HEADS // 2, 2 * ATT_BLOCK)

        q, k, v, rk, rv, rq, rg, cv = _inproj(x, row(norm1_w[l]), m_lat[1], m_lat[0], w_in_bf,
                                               cos, sa, sb, qw, kw)
        qc, kc, vc, rkc, rvc, rqc, rgc, cvc = _inproj(ctx, row(norm1_w[l]), m_ctx[1], m_ctx[0], w_in_bf,
                                                       ones_c, zeros_c, zeros_c, qw, kw)
        rf_c, s_f = _ret_states(lgf, rkc, rvc, zero_state, reverse=False)
        rb_c, s_b = _ret_states(lgb, rkc, rvc, zero_state, reverse=True)
        rf, _ = _ret_states(lgf, rk, rv, s_f, reverse=False)
        rb, _ = _ret_states(lgb, rk, rv, s_b, reverse=True)

        att, cast_out = _attention(q, k, v, kc, vc, sink_tab, window=True,
                                   cast=[moe_f32[j] for j in cast_jobs[l]])
        for j, w_bf in zip(cast_jobs[l], cast_out):
            moe_bf[j] = w_bf.reshape((moe_w_gate, moe_w_up, moe_w_down)[j].shape)
        ret = _ret_out(lgf, lgb, rq, rk, rv, rg, rf, rb, row(ret_gn_w[l]))
        cnv = _conv(cv, conv_w[l], row(conv_b[l]), row(conv_ln_w[l]), row(conv_ln_b[l]))

        if l % 2 == 0:
            i = l // 2
            wg, wu, wd = ffn_w_gate[i].astype(BF16), ffn_w_up[i].astype(BF16), ffn_w_down[i].astype(BF16)
            x_mid, h2 = _outproj(att, ret, cnv, w_out_bf, x, m_lat[2], row(norm2_w[l]), m_lat[4], m_lat[3])
            x_new = _ffn(h2, wg, wu, wd, x_mid, m_lat[5])
        else:
            i = l // 2
            wg, wu, wd = moe_bf[0][i], moe_bf[1][i], moe_bf[2][i]
            x_mid, h2, route = _outproj(att, ret, cnv, w_out_bf, x, m_lat[2], row(norm2_w[l]),
                                        m_lat[4], m_lat[3], router=router_w[i])
            x_new = _moe(h2, route, x_mid, m_lat[5], wg, wu, wd)

        if not last:
            att_c, _ = _attention(qc, None, None, kc, vc, sink_tab, window=False)
            ret_c = _ret_out(lgf, lgb, rqc, rkc, rvc, rgc, rf_c, rb_c, row(ret_gn_w[l]))
            cnv_c = _conv(cvc, conv_w[l], row(conv_b[l]), row(conv_ln_w[l]), row(conv_ln_b[l]))
            if l % 2 == 0:
                c_mid, h2c = _outproj(att_c, ret_c, cnv_c, w_out_bf, ctx, m_ctx[2], row(norm2_w[l]),
                                      m_ctx[4], m_ctx[3])
                ctx = _ffn(h2c, wg, wu, wd, c_mid, m_ctx[5])
            else:
                c_mid, h2c, route_c = _outproj(att_c, ret_c, cnv_c, w_out_bf, ctx, m_ctx[2],
                                               row(norm2_w[l]), m_ctx[4], m_ctx[3],
                                               router=router_w[i])
                ctx = _moe(h2c, route_c, c_mid, m_ctx[5], wg, wu, wd)
        x = x_new
    return x
```

```python
import functools

import jax
import jax.numpy as jnp
import numpy as np
from jax import lax
from jax.experimental import pallas as pl
from jax.experimental.pallas import tpu as pltpu
from jax.experimental.pallas import tpu_sc as plsc

F32 = jnp.float32
BF16 = jnp.bfloat16

GRID_W = 64
HEAD_DIM = 64
ATT_Q_HEADS = 8
ATT_KV_HEADS = 2
ATT_WINDOW = 128
ATT_BLOCK = 128
RET_HEADS = 4
RET_CHUNK = 128
RET_K_SCALE = HEAD_DIM ** -0.5
ATT_SCALE = HEAD_DIM ** -0.5
CONV_CH = 256
CONV_WIDTH = 31
ROPE_BASE = 10000.0
ROPE_FREQS = HEAD_DIM // 4
D_ATT = ATT_Q_HEADS * HEAD_DIM
D_RET = RET_HEADS * HEAD_DIM
ATT_KV_W = ATT_KV_HEADS * HEAD_DIM
C_ATT_K = 0
C_ATT_V = C_ATT_K + ATT_KV_W
C_RET_K = C_ATT_V + ATT_KV_W
C_RET_V = C_RET_K + D_RET
C_ATT_Q = C_RET_V + D_RET
C_RET_Q = C_ATT_Q + D_ATT
C_RET_G = C_RET_Q + D_RET
C_CONV = C_RET_G + D_RET
N_EXPERTS = 8
EPS = 1e-6
NEG_INF = -1e30

LANES = 128
SUBLANES = 8
BF16_SUBLANES = 16
VMEM_LIMIT = 48 * 1024 * 1024
CONV_HALO = 16
CONV_SUB = 32
CONV_CHAINS = 4
ATT_QBLOCKS = 4
RET_UNROLL = 4
MOE_TM = 512
MOE_CHUNKS = 2
MOE_SUB = 512
FFN_SUB = 512
MOE_TF = 1792
SC_LANES = 16
SC_WINDOW = 64


def _params(*sem):
    return pltpu.CompilerParams(dimension_semantics=sem, vmem_limit_bytes=VMEM_LIMIT)


def _sigmoid(x):
    return 1.0 / (1.0 + jnp.exp(-x))


def _pack_bf16_pairs(v):
    c = v.shape[1] // 2
    bits = pltpu.bitcast(v.astype(BF16).astype(F32), jnp.uint32)
    packed = (bits[:, c:] & jnp.uint32(0xFFFF0000)) | (bits[:, :c] >> 16)
    return pltpu.bitcast(packed, jnp.int32)


def _unpack_bf16_pairs(p):
    bits = pltpu.bitcast(p, jnp.uint32)
    return pltpu.bitcast(bits << 16, F32), pltpu.bitcast(bits & jnp.uint32(0xFFFF0000), F32)


def _pick_tile(n, pref):
    t = min(n, pref)
    assert n % t == 0, (n, t)
    return t


def _adaln_kernel(c_ref, w_ref, b_ref, o_ref):
    c = c_ref[...]
    s = c * _sigmoid(c)
    o_ref[...] = jnp.dot(s, w_ref[...], preferred_element_type=F32,
                         precision=lax.Precision.HIGHEST) + b_ref[...]


def _adaln(cond, ada_w, ada_b):
    depth, d, n = ada_w.shape
    tn = _pick_tile(n, 1536)
    return pl.pallas_call(
        _adaln_kernel,
        out_shape=jax.ShapeDtypeStruct((depth, cond.shape[0], n), F32),
        grid=(depth, n // tn),
        in_specs=[pl.BlockSpec(cond.shape, lambda l, j: (0, 0)),
                  pl.BlockSpec((None, d, tn), lambda l, j: (l, 0, j)),
                  pl.BlockSpec((None, 1, tn), lambda l, j: (l, 0, j))],
        out_specs=pl.BlockSpec((None, cond.shape[0], tn), lambda l, j: (l, 0, j)),
        compiler_params=_params("parallel", "parallel"),
        name="adaln",
    )(cond, ada_w, ada_b.reshape(depth, 1, n))


def _modulated_rms(x, nw, sc, sh):
    ms = jnp.mean(x * x, axis=-1, keepdims=True)
    return (x * lax.rsqrt(ms + EPS) * nw) * (1.0 + sc) + sh


def _head_group_matrix():
    r = lax.broadcasted_iota(jnp.int32, (2 * LANES, 2 * LANES), 0) // HEAD_DIM
    c = lax.broadcasted_iota(jnp.int32, (2 * LANES, 2 * LANES), 1) // HEAD_DIM
    return jnp.where(r == c, 1.0, 0.0).astype(BF16)


def _per_head_sum(v, grp):
    hi = v.astype(BF16)
    lo = (v - hi.astype(F32)).astype(BF16)
    r = jnp.dot(jnp.concatenate([hi, lo], axis=1), grp, preferred_element_type=F32)
    return r[:, 0:LANES] + r[:, LANES:]


def _dup_halves(t):
    sw = pltpu.roll(t, HEAD_DIM, 1)
    lo = lax.broadcasted_iota(jnp.int32, t.shape, 1) < HEAD_DIM
    return jnp.where(lo, t, sw), jnp.where(lo, sw, t)


def _inproj_kernel(x_ref, nw_ref, sc_ref, sh_ref, w_ref, cos_ref, sa_ref, sb_ref, qw_ref, kw_ref,
                   q_ref, k_ref, v_ref, rk_ref, rv_ref, rq_ref, rg_ref, cv_ref):
    hb = _modulated_rms(x_ref[...], nw_ref[...], sc_ref[...], sh_ref[...]).astype(BF16)

    def proj(c0, n):
        return jnp.dot(hb, w_ref[:, c0:c0 + n], preferred_element_type=F32)

    grp = _head_group_matrix()
    cos, sa, sb = cos_ref[...], sa_ref[...], sb_ref[...]

    def norm_rope(p, wrow):
        y = p * lax.rsqrt(_per_head_sum(p * p, grp) * (1.0 / HEAD_DIM) + EPS) * wrow
        return (y * cos + pltpu.roll(y, LANES - ROPE_FREQS, 1) * sa
                + pltpu.roll(y, ROPE_FREQS, 1) * sb)

    def put_kv(kv):
        k0, k1 = _dup_halves(norm_rope(kv[:, 0:ATT_KV_W], kw_ref[...]))
        k_ref[:, 0:LANES] = k0.astype(BF16)
        k_ref[:, LANES:2 * LANES] = k1.astype(BF16)
        v_ref[...] = kv[:, ATT_KV_W:].T.astype(BF16)

    def put_q(qall):
        qw = qw_ref[...] * ATT_SCALE
        for j in range(D_ATT // LANES):
            q_ref[:, j * LANES:(j + 1) * LANES] = norm_rope(qall[:, j * LANES:(j + 1) * LANES], qw).astype(BF16)

    def put_rk(p):
        rk_ref[...] = (p * RET_K_SCALE).astype(BF16)

    def put_rv(p):
        rv_ref[...] = p.astype(BF16)

    def put_rq(p):
        rq_ref[...] = p.astype(BF16)

    def put_rg(p):
        rg_ref[...] = p

    def put_conv(p):
        cv_ref[...] = p[:, 0:CONV_CH] * _sigmoid(p[:, CONV_CH:])

    units = [(C_ATT_K, 2 * ATT_KV_W, put_kv), (C_ATT_Q, D_ATT, put_q), (C_RET_K, D_RET, put_rk),
             (C_RET_V, D_RET, put_rv), (C_RET_Q, D_RET, put_rq), (C_RET_G, D_RET, put_rg),
             (C_CONV, 2 * CONV_CH, put_conv)]
    ahead = 2
    pending = [proj(c0, n) for c0, n, _ in units[:ahead]]
    for n, (_, _, put) in enumerate(units):
        if n + ahead < len(units):
            pending.append(proj(*units[n + ahead][:2]))
        put(pending.pop(0))


def _inproj(x, nw, sc, sh, w_bf, cos, sa, sb, qw, kw):
    b, s, d = x.shape
    tm = _pick_tile(s, 512)
    row = lambda n: pl.BlockSpec((1, n), lambda bi, i: (0, 0))
    per_b = pl.BlockSpec((None, 1, d), lambda bi, i: (bi, 0, 0))
    tab = pl.BlockSpec((tm, LANES), lambda bi, i: (i, 0))
    tok = lambda n: pl.BlockSpec((None, tm, n), lambda bi, i: (bi, i, 0))
    shp = lambda n, dt: jax.ShapeDtypeStruct((b, s, n), dt)
    return pl.pallas_call(
        _inproj_kernel,
        out_shape=(shp(D_ATT, BF16), shp(2 * LANES, BF16), jax.ShapeDtypeStruct((b, ATT_KV_W, s), BF16),
                   shp(D_RET, BF16), shp(D_RET, BF16), shp(D_RET, BF16), shp(D_RET, F32),
                   shp(CONV_CH, F32)),
        grid=(b, s // tm),
        in_specs=[tok(d), row(d), per_b, per_b,
                  pl.BlockSpec(w_bf.shape, lambda bi, i: (0, 0)),
                  tab, tab, tab, row(LANES), row(LANES)],
        out_specs=(tok(D_ATT), tok(2 * LANES), pl.BlockSpec((None, ATT_KV_W, tm), lambda bi, i: (bi, 0, i)),
                   tok(D_RET), tok(D_RET), tok(D_RET), tok(D_RET), tok(CONV_CH)),
        compiler_params=_params("parallel", "parallel"),
        name="inproj",
    )(x, nw, sc, sh, w_bf, cos, sa, sb, qw, kw)


def _attn_kernel(*refs, window, nq, n_cast):
    blk = ATT_BLOCK
    n_in = len(refs) - 1 - n_cast
    for src, dst in zip(refs[n_in - n_cast:n_in], refs[n_in + 1:]):
        dst[...] = src[...].astype(BF16)
    refs = refs[:n_in - n_cast] + refs[n_in:n_in + 1]
    if window:
        q_ref = refs[0]
        k_refs = refs[1:nq + 3]
        v_refs = refs[nq + 3:2 * nq + 5]
        kx_ref, vx_ref, sink_ref, o_ref = refs[2 * nq + 5:]
    else:
        q_ref, kx_ref, vx_ref, sink_ref, o_ref = refs
    n_ctx = kx_ref.shape[0]
    nk = 3 * blk + n_ctx if window else n_ctx
    if window:
        i = pl.program_id(1)
        last = nq * pl.num_programs(1) - 1
        key = lax.broadcasted_iota(jnp.int32, (blk, 2 * blk), 0)
        qry = lax.broadcasted_iota(jnp.int32, (blk, 2 * blk), 1) & (blk - 1)

        def band_masks(sub):
            off_prev = jnp.where(nq * i + sub > 0, 0, blk)
            off_next = jnp.where(nq * i + sub < last, 0, blk)
            return key >= qry + off_prev, key + off_next <= qry
    first_head = lax.broadcasted_iota(jnp.int32, (blk, LANES), 1) < HEAD_DIM
    ones = jnp.ones((2 * SUBLANES, nk), BF16)

    def scores(sub, pair):
        g = pair // 2
        gs = slice(g * LANES, (g + 1) * LANES)
        if window:
            kcat = jnp.concatenate([r[:, gs] for r in k_refs[sub:sub + 3]] + [kx_ref[:, gs]], axis=0)
        else:
            kcat = kx_ref[:, gs]
        qp = q_ref[sub * blk:(sub + 1) * blk, pair * LANES:(pair + 1) * LANES]
        zero = jnp.zeros_like(qp)
        w = jnp.concatenate([jnp.where(first_head, qp, zero), jnp.where(first_head, zero, qp)], axis=0)
        return lax.dot_general(kcat, w, (((1,), (1,)), ((), ())), preferred_element_type=F32)

    def masked_max(sub, pair, s):
        if window:
            mask_prev, mask_next = band_masks(sub)
            parts = [jnp.where(mask_prev, s[0:blk], NEG_INF), s[blk:2 * blk],
                     jnp.where(mask_next, s[2 * blk:3 * blk], NEG_INF), s[3 * blk:]]
        else:
            parts = [s]
        m = sink_ref[pair:pair + 1, :]
        for part in parts:
            m = jnp.maximum(m, jnp.max(part, axis=0, keepdims=True))
        return parts, m

    def softmax(sub, pair, parts_m):
        parts, m = parts_m
        p = jnp.concatenate([jnp.exp(part - m).astype(BF16) for part in parts], axis=0)
        return p, jnp.exp(sink_ref[pair:pair + 1, :] - m)

    def output(sub, pair, p, sink_p):
        g = pair // 2
        vs = slice(g * HEAD_DIM, (g + 1) * HEAD_DIM)
        if window:
            vt = jnp.concatenate([r[vs, :] for r in v_refs[sub:sub + 3]] + [vx_ref[vs, :]], axis=1)
        else:
            vt = vx_ref[vs, :]
        vaug = jnp.concatenate([vt, ones], axis=0)
        o = jnp.dot(vaug, p, preferred_element_type=F32)
        on = o[0:HEAD_DIM, :] * (1.0 / (o[HEAD_DIM:HEAD_DIM + 1, :] + sink_p))
        ot = jnp.concatenate([on[:, 0:blk], on[:, blk:2 * blk]], axis=0)
        o_ref[sub * blk:(sub + 1) * blk, pair * LANES:(pair + 1) * LANES] = ot.T.astype(BF16)

    units = [(sub, pair) for sub in range(nq) for pair in range(ATT_Q_HEADS // 2)]
    stages = (lambda sub, pair, _: scores(sub, pair), masked_max, softmax,
              lambda sub, pair, ps: output(sub, pair, *ps))
    vals = {}
    for step in range(len(units) + len(stages) - 1):
        for k, stage in enumerate(stages):
            u = step - k
            if 0 <= u < len(units):
                vals[u, k] = stage(*units[u], vals.pop((u, k - 1), None))


def _attention(q, k, vt, kx, vxt, sink_tab, window, cast=()):
    b, s, _ = q.shape
    blk = ATT_BLOCK
    nb = s // blk
    nq = ATT_QBLOCKS if nb % ATT_QBLOCKS == 0 else 1
    n_ctx = kx.shape[1]
    qspec = pl.BlockSpec((None, nq * blk, D_ATT), lambda bi, i: (bi, i, 0))
    kctx = pl.BlockSpec((None, n_ctx, 2 * LANES), lambda bi, i: (bi, 0, 0))
    vctx = pl.BlockSpec((None, ATT_KV_W, n_ctx), lambda bi, i: (bi, 0, 0))
    snk = pl.BlockSpec(sink_tab.shape, lambda bi, i: (0, 0))
    if window:
        at = lambda off: (lambda i: jnp.clip(nq * i + off, 0, nb - 1))
        kspec = lambda f: pl.BlockSpec((None, blk, 2 * LANES), lambda bi, i: (bi, f(i), 0))
        vspec = lambda f: pl.BlockSpec((None, ATT_KV_W, blk), lambda bi, i: (bi, 0, f(i)))
        offs = range(-1, nq + 1)
        in_specs = ([qspec] + [kspec(at(o)) for o in offs] + [vspec(at(o)) for o in offs]
                    + [kctx, vctx, snk])
        args = (q,) + (k,) * (nq + 2) + (vt,) * (nq + 2) + (kx, vxt, sink_tab)
    else:
        in_specs = [qspec, kctx, vctx, snk]
        args = (q, kx, vxt, sink_tab)
    steps = nb // nq
    out_shape, out_specs = [jax.ShapeDtypeStruct((b, s, D_ATT), BF16)], [qspec]
    for w in cast:
        rows = w.shape[0] // (b * steps)
        assert rows * b * steps == w.shape[0] and rows % (2 * SUBLANES) == 0, w.shape
        slab = pl.BlockSpec((rows, w.shape[1]), lambda bi, i: (bi * steps + i, 0))
        in_specs.append(slab)
        out_specs.append(slab)
        out_shape.append(jax.ShapeDtypeStruct(w.shape, BF16))
    res = pl.pallas_call(
        functools.partial(_attn_kernel, window=window, nq=nq, n_cast=len(cast)),
        out_shape=tuple(out_shape),
        grid=(b, steps),
        in_specs=in_specs,
        out_specs=tuple(out_specs),
        compiler_params=_params("parallel", "parallel"),
        name="attention_window" if window else "attention_ctx",
    )(*args, *cast)
    return res[0], tuple(res[1:])


def _ret_state_kernel(lgl_ref, k_ref, v_ref, s0_ref, r_ref, fin_ref, s_scr, *, cb, reverse):
    c = pl.program_id(1)
    L = RET_CHUNK

    @pl.when(c == 0)
    def _():
        s_scr[...] = s0_ref[...]

    jj = lax.broadcasted_iota(jnp.int32, (L, LANES), 0).astype(F32)
    expo = jj if reverse else (L - 1.0) - jj
    same_head = (lax.broadcasted_iota(jnp.int32, (LANES, LANES), 0) // HEAD_DIM
                 == lax.broadcasted_iota(jnp.int32, (LANES, LANES), 1) // HEAD_DIM)
    for pr in range(RET_HEADS // 2):
        cs = slice(pr * LANES, (pr + 1) * LANES)
        lgl = lgl_ref[pr]
        kdec = jnp.exp(expo * lgl)
        cdec = jnp.exp(float(L) * lgl)
        state = s_scr[pr]
        for t in range(cb):
            cc = cb - 1 - t if reverse else t
            rows = slice(cc * L, (cc + 1) * L)
            r_ref[cc, pr] = state.astype(BF16)
            kd = k_ref[rows, cs].astype(F32) * kdec
            u = jnp.dot(kd.T.astype(BF16), v_ref[rows, cs], preferred_element_type=F32)
            state = cdec * state + jnp.where(same_head, u, 0.0)
        s_scr[pr] = state

    @pl.when(c == pl.num_programs(1) - 1)
    def _():
        fin_ref[...] = s_scr[...]


def _ret_states(lgl, rk, rv, s0, reverse):
    b, s, _ = rk.shape
    nc = s // RET_CHUNK
    cb = _pick_tile(nc, 8)
    nblk = nc // cb
    npair = RET_HEADS // 2
    blk_idx = (lambda c: nblk - 1 - c) if reverse else (lambda c: c)
    tok = pl.BlockSpec((None, cb * RET_CHUNK, D_RET), lambda bi, c: (bi, blk_idx(c), 0))
    st = pl.BlockSpec((None, npair, LANES, LANES), lambda bi, c: (bi, 0, 0, 0))
    return pl.pallas_call(
        functools.partial(_ret_state_kernel, cb=cb, reverse=reverse),
        out_shape=(jax.ShapeDtypeStruct((b, nc, npair, LANES, LANES), BF16),
                   jax.ShapeDtypeStruct((b, npair, LANES, LANES), F32)),
        grid=(b, nblk),
        in_specs=[pl.BlockSpec(lgl.shape, lambda bi, c: (0, 0, 0)), tok, tok, st],
        out_specs=(pl.BlockSpec((None, cb, npair, LANES, LANES),
                                lambda bi, c: (bi, blk_idx(c), 0, 0, 0)), st),
        scratch_shapes=[pltpu.VMEM((npair, LANES, LANES), F32)],
        compiler_params=_params("parallel", "arbitrary"),
        name="ret_state_bwd" if reverse else "ret_state_fwd",
    )(lgl, rk, rv, s0)


def _ret_out_kernel(lgf_ref, lgb_ref, q_ref, k_ref, v_ref, g_ref, rf_ref, rb_ref, gnw_ref, o_ref, *, cb):
    L = RET_CHUNK
    i0 = lax.broadcasted_iota(jnp.int32, (L, LANES), 0).astype(F32)
    i1 = lax.broadcasted_iota(jnp.int32, (L, LANES), 1).astype(F32)
    diff = i0 - i1
    lo = lax.broadcasted_iota(jnp.int32, (L, LANES), 1) < HEAD_DIM
    grp = _head_group_matrix()
    inv = 1.0 / HEAD_DIM
    tabs = []
    for pr in range(RET_HEADS // 2):
        lgf, lgb = lgf_ref[pr], lgb_ref[pr]
        dmat = [jnp.where(diff >= 0.0,
                          jnp.exp(jnp.maximum(diff, 0.0) * lgf[:, a:a + 1]),
                          jnp.exp(jnp.maximum(-diff, 0.0) * lgb[:, a:a + 1]))
                for a in (0, HEAD_DIM)]
        tabs.append((jnp.exp((i0 + 1.0) * lgf), jnp.exp((float(L) - i0) * lgb),
                     jnp.concatenate(dmat, axis=0), gnw_ref[:, pr * LANES:(pr + 1) * LANES]))

    def decayed_scores(t, pr):
        rows = pl.ds(pl.multiple_of(t * L, L), L)
        cs = slice(pr * LANES, (pr + 1) * LANES)
        qp, kp = q_ref[rows, cs], k_ref[rows, cs]
        zero = jnp.zeros_like(kp)
        qz = jnp.concatenate([jnp.where(lo, qp, zero), jnp.where(lo, zero, qp)], axis=0)
        return lax.dot_general(qz, kp, (((1,), (1,)), ((), ())), preferred_element_type=F32) * tabs[pr][2]

    def mix(t, pr, a):
        rows = pl.ds(pl.multiple_of(t * L, L), L)
        cs = slice(pr * LANES, (pr + 1) * LANES)
        xif, xib, _, _ = tabs[pr]
        vp = v_ref[rows, cs]
        zero = jnp.zeros_like(vp)
        qf = q_ref[rows, cs].astype(F32)
        lhs = jnp.concatenate([a[0:L].astype(BF16), a[L:].astype(BF16),
                               (qf * xif).astype(BF16), (qf * xib).astype(BF16)], axis=1)
        rhs = jnp.concatenate([jnp.where(lo, vp, zero), jnp.where(lo, zero, vp),
                               rf_ref[t, pr], rb_ref[t, pr]], axis=0)
        return jnp.dot(lhs, rhs, preferred_element_type=F32)

    def centre(t, pr, acc):
        return acc - _per_head_sum(acc, grp) * inv

    def readout(t, pr, dl):
        rows = pl.ds(pl.multiple_of(t * L, L), L)
        cs = slice(pr * LANES, (pr + 1) * LANES)
        var = _per_head_sum(dl * dl, grp) * inv
        y = dl * lax.rsqrt(var + EPS) * tabs[pr][3]
        gt = g_ref[rows, cs]
        o_ref[rows, cs] = (gt * _sigmoid(gt) * y).astype(BF16)

    unroll = min(RET_UNROLL, cb)
    stages = (lambda t, pr, _: decayed_scores(t, pr), mix, centre, readout)

    def group(gi, carry):
        units = [(gi * unroll + u, pr) for u in range(unroll) for pr in range(RET_HEADS // 2)]
        vals = {}
        for step in range(len(units) + len(stages) - 1):
            for k, stage in enumerate(stages):
                u = step - k
                if 0 <= u < len(units):
                    vals[u, k] = stage(*units[u], vals.pop((u, k - 1), None))
        return carry

    lax.fori_loop(0, cb // unroll, group, 0)


def _ret_out(lgf, lgb, rq, rk, rv, rg, rf, rb, gnw):
    b, s, _ = rq.shape
    nc = s // RET_CHUNK
    cb = _pick_tile(nc, 8)
    npair = RET_HEADS // 2
    tok = pl.BlockSpec((None, cb * RET_CHUNK, D_RET), lambda bi, c: (bi, c, 0))
    st = pl.BlockSpec((None, cb, npair, LANES, LANES), lambda bi, c: (bi, c, 0, 0, 0))
    lg = pl.BlockSpec(lgf.shape, lambda bi, c: (0, 0, 0))
    return pl.pallas_call(
        functools.partial(_ret_out_kernel, cb=cb),
        out_shape=jax.ShapeDtypeStruct((b, s, D_RET), BF16),
        grid=(b, nc // cb),
        in_specs=[lg, lg, tok, tok, tok, tok, st, st, pl.BlockSpec((1, D_RET), lambda bi, c: (0, 0))],
        out_specs=tok,
        compiler_params=_params("parallel", "parallel"),
        name="ret_out",
    )(lgf, lgb, rq, rk, rv, rg, rf, rb, gnw)


def _conv_kernel(prev_ref, cur_ref, next_ref, w_ref, b_ref, lnw_ref, lnb_ref, o_ref,
                 xpad, shifted, hbuf, wtile):
    i = pl.program_id(1)
    tm = cur_ref.shape[0]
    halo = CONV_HALO
    first = i == 0
    last = i == pl.num_programs(1) - 1
    xpad[0:halo, :] = jnp.where(first, 0.0, prev_ref[...])
    xpad[halo:halo + tm, :] = cur_ref[...]
    xpad[halo + tm:2 * halo + tm, :] = jnp.where(last, 0.0, next_ref[...])
    span = tm + 2 * halo - SUBLANES
    for r in range(SUBLANES):
        shifted[r] = xpad[r:r + span, :]
    base = halo - CONV_WIDTH // 2
    bias = b_ref[...]
    ch = cur_ref.shape[1]
    for w in range(CONV_WIDTH):
        wtile[w] = jnp.broadcast_to(w_ref[w:w + 1, :], (SUBLANES, ch))

    taps_by_shift = {}
    for w in range(CONV_WIDTH):
        taps_by_shift.setdefault((base + w) % SUBLANES, []).append(((base + w) // SUBLANES, w))

    def sub(sb, carry):
        r0 = pl.multiple_of(sb * CONV_SUB, CONV_SUB)
        chains = [None] * CONV_CHAINS
        groups = CONV_SUB // SUBLANES
        for r, taps in sorted(taps_by_shift.items()):
            a_lo = min(a for a, _ in taps)
            n_g = max(a for a, _ in taps) - a_lo + groups
            win = shifted[r, pl.ds(r0 + a_lo * SUBLANES, n_g * SUBLANES), :].reshape(n_g, SUBLANES, ch)
            for a, w in taps:
                term = win[a - a_lo:a - a_lo + groups] * wtile[w]
                c = w % CONV_CHAINS
                chains[c] = term if chains[c] is None else chains[c] + term
        total = (chains[0] + chains[1]) + (chains[2] + chains[3])
        hbuf[pl.ds(r0, CONV_SUB), :] = total.reshape(CONV_SUB, ch) + bias
        return carry

    lax.fori_loop(0, tm // CONV_SUB, sub, 0)
    h = hbuf[...]
    mu = jnp.mean(h, axis=-1, keepdims=True)
    dl = h - mu
    var = jnp.mean(dl * dl, axis=-1, keepdims=True)
    y = dl * lax.rsqrt(var + EPS) * lnw_ref[...] + lnb_ref[...]
    o_ref[...] = (y * _sigmoid(y)).astype(BF16)


def _conv(cv, w, bias, lnw, lnb):
    b, s, ch = cv.shape
    tm = _pick_tile(s, 512)
    hpb = tm // CONV_HALO
    nh = s // CONV_HALO
    row = pl.BlockSpec((1, ch), lambda bi, i: (0, 0))
    span = tm + 2 * CONV_HALO - SUBLANES
    return pl.pallas_call(
        _conv_kernel,
        out_shape=jax.ShapeDtypeStruct((b, s, ch), BF16),
        grid=(b, s // tm),
        in_specs=[pl.BlockSpec((None, CONV_HALO, ch), lambda bi, i: (bi, jnp.maximum(i * hpb - 1, 0), 0)),
                  pl.BlockSpec((None, tm, ch), lambda bi, i: (bi, i, 0)),
                  pl.BlockSpec((None, CONV_HALO, ch),
                               lambda bi, i: (bi, jnp.minimum((i + 1) * hpb, nh - 1), 0)),
                  pl.BlockSpec(w.shape, lambda bi, i: (0, 0)), row, row, row],
        out_specs=pl.BlockSpec((None, tm, ch), lambda bi, i: (bi, i, 0)),
        scratch_shapes=[pltpu.VMEM((tm + 2 * CONV_HALO, ch), F32),
                        pltpu.VMEM((SUBLANES, span, ch), F32),
                        pltpu.VMEM((tm, ch), F32),
                        pltpu.VMEM((CONV_WIDTH, SUBLANES, ch), F32)],
        compiler_params=_params("parallel", "parallel"),
        name="conv",
    )(cv, cv, cv, w, bias, lnw, lnb)


def _outproj_kernel(*refs, route):
    if route:
        (att_ref, ret_ref, cnv_ref, w_ref, x_ref, g1_ref, nw_ref, sc_ref, sh_ref, rcat_ref,
         xo_ref, h_ref, rt_ref) = refs
    else:
        att_ref, ret_ref, cnv_ref, w_ref, x_ref, g1_ref, nw_ref, sc_ref, sh_ref, xo_ref, h_ref = refs
    tm = x_ref.shape[0]
    n_part = 2 if (not route and tm % (2 * BF16_SUBLANES) == 0) else 1
    rows_of = lambda part: slice(part * (tm // n_part), (part + 1) * (tm // n_part))

    def mixer_out(part, _):
        rows = rows_of(part)
        return (jnp.dot(att_ref[rows, :], w_ref[0:D_ATT, :], preferred_element_type=F32)
                + jnp.dot(ret_ref[rows, :], w_ref[D_ATT:D_ATT + D_RET, :], preferred_element_type=F32)
                + jnp.dot(cnv_ref[rows, :], w_ref[D_ATT + D_RET:, :], preferred_element_type=F32))

    def residual_norm(part, y):
        rows = rows_of(part)
        xn = x_ref[rows, :] + g1_ref[...] * y
        xo_ref[rows, :] = xn
        h = _modulated_rms(xn, nw_ref[...], sc_ref[...], sh_ref[...])
        h_ref[rows, :] = _pack_bf16_pairs(h) if route else h.astype(BF16)
        if not route:
            return None
        hi = h.astype(BF16)
        lo = (h - hi.astype(F32)).astype(BF16)
        return jnp.dot(jnp.concatenate([hi, lo], axis=0), rcat_ref[...], preferred_element_type=F32)

    def top2(part, r):
        n = r.shape[0] // 2
        logits = (r[0:n, 0:LANES] + r[0:n, LANES:]) + (r[n:, 0:LANES] + r[n:, LANES:])
        lane = lax.broadcasted_iota(jnp.int32, logits.shape, 1).astype(F32)
        logits = jnp.where(lane < N_EXPERTS, logits, NEG_INF)
        m1 = jnp.max(logits, axis=-1, keepdims=True)
        i1 = jnp.min(jnp.where(logits == m1, lane, float(LANES)), axis=-1, keepdims=True)
        rest = jnp.where(lane == i1, NEG_INF, logits)
        m2 = jnp.max(rest, axis=-1, keepdims=True)
        i2 = jnp.min(jnp.where(rest == m2, lane, float(LANES)), axis=-1, keepdims=True)
        e2 = jnp.exp(m2 - m1)
        w1 = 1.0 / (1.0 + e2)
        w2 = e2 / (1.0 + e2)
        rt_ref[rows_of(part), :] = jnp.where(
            lane == 0.0, i1, jnp.where(lane == 1.0, i2,
                                       jnp.where(lane == 2.0, w1, jnp.where(lane == 3.0, w2, 0.0))))

    finish = (lambda part, y: top2(part, residual_norm(part, y))) if route else residual_norm
    stages = (mixer_out, finish)
    vals = {}
    for step in range(n_part + len(stages) - 1):
        for k, stage in enumerate(stages):
            u = step - k
            if 0 <= u < n_part:
                vals[u, k] = stage(u, vals.pop((u, k - 1), None))


def _outproj(att, ret, cnv, w_bf, x, g1, nw, sc, sh, router=None):
    b, s, d = x.shape
    tm = _pick_tile(s, 512)
    tok = lambda n: pl.BlockSpec((None, tm, n), lambda bi, i: (bi, i, 0))
    per_b = pl.BlockSpec((None, 1, d), lambda bi, i: (bi, 0, 0))
    row = pl.BlockSpec((1, d), lambda bi, i: (0, 0))
    in_specs = [tok(D_ATT), tok(D_RET), tok(CONV_CH), pl.BlockSpec(w_bf.shape, lambda bi, i: (0, 0)),
                tok(d), per_b, row, per_b, per_b]
    args = [att, ret, cnv, w_bf, x, g1, nw, sc, sh]
    h_shape = (b, s, d // 2) if router is not None else (b, s, d)
    out_shape = [jax.ShapeDtypeStruct((b, s, d), F32),
                 jax.ShapeDtypeStruct(h_shape, jnp.int32 if router is not None else BF16)]
    out_specs = [tok(d), tok(h_shape[2])]
    if router is not None:
        rpad = jnp.zeros((d, LANES), F32).at[:, :N_EXPERTS].set(router)
        rhi = rpad.astype(BF16)
        rlo = (rpad - rhi.astype(F32)).astype(BF16)
        in_specs += [pl.BlockSpec((d, 2 * LANES), lambda bi, i: (0, 0))]
        args += [jnp.concatenate([rhi, rlo], axis=1)]
        out_shape.append(jax.ShapeDtypeStruct((b, s, LANES), F32))
        out_specs.append(tok(LANES))
    return pl.pallas_call(
        functools.partial(_outproj_kernel, route=router is not None),
        out_shape=tuple(out_shape),
        grid=(b, s // tm),
        in_specs=in_specs,
        out_specs=tuple(out_specs),
        compiler_params=_params("parallel", "parallel"),
        name="outproj_route" if router is not None else "outproj",
    )(*args)


def _swiglu_chunks(h, wg_ref, wu_ref, wd_ref, sub):
    total = None
    for c0 in range(0, wg_ref.shape[1], sub):
        cs = slice(c0, min(c0 + sub, wg_ref.shape[1]))
        gate = jnp.dot(h, wg_ref[:, cs].astype(BF16), preferred_element_type=F32)
        up = jnp.dot(h, wu_ref[:, cs].astype(BF16), preferred_element_type=F32)
        act = (gate * _sigmoid(gate) * up).astype(BF16)
        part = jnp.dot(act, wd_ref[cs, :].astype(BF16), preferred_element_type=F32)
        total = part if total is None else total + part
    return total


def _ffn_kernel(h_ref, wg_ref, wu_ref, wd_ref, x_ref, g2_ref, o_ref):
    y = _swiglu_chunks(h_ref[...], wg_ref, wu_ref, wd_ref, FFN_SUB)
    o_ref[...] = x_ref[...] + g2_ref[...] * y


def _ffn(h, wg, wu, wd, x, g2):
    b, s, d = x.shape
    f = wg.shape[1]
    tm = _pick_tile(s, 512)
    tok = lambda: pl.BlockSpec((None, tm, d), lambda bi, i: (bi, i, 0))
    res = lambda shape: pl.BlockSpec(shape, lambda bi, i: (0, 0), pipeline_mode=pl.Buffered(1))
    return pl.pallas_call(
        _ffn_kernel,
        out_shape=jax.ShapeDtypeStruct((b, s, d), F32),
        grid=(b, s // tm),
        in_specs=[tok(), res((d, f)), res((d, f)), res((f, d)), tok(),
                  pl.BlockSpec((None, 1, d), lambda bi, i: (bi, 0, 0))],
        out_specs=tok(),
        compiler_params=_params("parallel", "parallel"),
        name="ffn_dense",
    )(h, wg, wu, wd, x, g2)


def _gather_rows(idx, src):
    n = idx.shape[0]
    d = src.shape[1]
    mesh = plsc.VectorSubcoreMesh(core_axis_name="core", subcore_axis_name="subcore")

    @pl.kernel(out_type=jax.ShapeDtypeStruct((n, d), src.dtype), mesh=mesh, name="moe_gather")
    def gather(src_hbm, idx_hbm, out_hbm):
        def body(idx_vmem, out_vmem):
            pltpu.sync_copy(src_hbm.at[idx_vmem.at[0, pl.ds(0, SC_WINDOW)]], out_vmem)

        pltpu.emit_pipeline(
            body,
            grid=(n // SC_WINDOW,),
            in_specs=[pl.BlockSpec((1, LANES), lambda i: (i, 0))],
            out_specs=[pl.BlockSpec((SC_WINDOW, d), lambda i: (i, 0))],
            core_axis_name=("core", "subcore"),
            dimension_semantics=(pltpu.PARALLEL,),
        )(idx_hbm, out_hbm)

    idx_rows = jnp.pad(idx.reshape(n // SC_WINDOW, SC_WINDOW), ((0, 0), (0, LANES - SC_WINDOW)))
    return gather(src, idx_rows)


def _invert_rows(dest, n_rows):
    n = dest.shape[0]
    n_tok = n // 2
    mesh = plsc.VectorSubcoreMesh(core_axis_name="core", subcore_axis_name="subcore")

    @pl.kernel(out_type=jax.ShapeDtypeStruct((n_rows,), jnp.int32), mesh=mesh, name="moe_row_src",
               scratch_types=[pltpu.VMEM((n,), jnp.int32), pltpu.VMEM((n_rows,), jnp.int32)],
               compiler_params=pltpu.CompilerParams(needs_layout_passes=False))
    def invert(dest_hbm, out_hbm, dest_vmem, rows_vmem):
        @pl.when((lax.axis_index("core") == 0) & (lax.axis_index("subcore") == 0))
        def _():
            pltpu.sync_copy(dest_hbm, dest_vmem)

            @pl.loop(0, n_rows, step=SC_LANES)
            def _(r):
                rows_vmem[pl.ds(r, SC_LANES)] = lax.rem(lax.iota(jnp.int32, SC_LANES) + r, n_tok)

            @pl.loop(0, n, step=SC_LANES)
            def _(a):
                tok = lax.iota(jnp.int32, SC_LANES) + a
                tok = jnp.where(tok >= n_tok, tok - n_tok, tok)
                plsc.store_scatter(rows_vmem, [dest_vmem[pl.ds(a, SC_LANES)]], tok)

            pltpu.sync_copy(rows_vmem, out_hbm)

    return invert(dest)


def _moe_kernel(te_ref, nu_ref, x_ref, wg_ref, wu_ref, wd_ref, *rest, tile0, nj):
    o_ref, xb, acc = rest[-3:]
    t = pl.program_id(0) + tile0
    j = pl.program_id(1)
    used = t < nu_ref[0]

    def partial_sum():
        return _swiglu_chunks(xb[...], wg_ref, wu_ref, wd_ref, MOE_SUB)

    @pl.when(used & (j == 0))
    def _():
        lo, hi = _unpack_bf16_pairs(x_ref[...])
        half = lo.shape[1]
        xb[:, 0:half] = lo.astype(BF16)
        xb[:, half:] = hi.astype(BF16)
        if nj == 1:
            o_ref[...] = _pack_bf16_pairs(partial_sum())
        else:
            acc[...] = partial_sum()

    if nj > 2:
        @pl.when(used & (j > 0) & (j < nj - 1))
        def _():
            acc[...] += partial_sum()

    if nj > 1:
        @pl.when(used & (j == nj - 1))
        def _():
            o_ref[...] = _pack_bf16_pairs(acc[...] + partial_sum())

    @pl.when(jnp.logical_not(used) & (j == nj - 1))
    def _():
        o_ref[...] = jnp.zeros_like(o_ref)


def _moe_grouped(tile_expert, n_used, xs, wg, wu, wd, y_prev, tile0, n_rows):
    p, dp = xs.shape
    d = 2 * dp
    f = wg.shape[2]
    tm = MOE_TM
    tf = MOE_TF
    nj = f // tf

    def jj(t, j, te, nu):
        return jnp.where(t + tile0 < nu[0], j, nj - 1)

    in_specs = [pl.BlockSpec((tm, dp), lambda t, j, te, nu: (t, 0)),
                pl.BlockSpec((None, d, tf), lambda t, j, te, nu: (te[t + tile0], 0, jj(t, j, te, nu))),
                pl.BlockSpec((None, d, tf), lambda t, j, te, nu: (te[t + tile0], 0, jj(t, j, te, nu))),
                pl.BlockSpec((None, tf, d), lambda t, j, te, nu: (te[t + tile0], jj(t, j, te, nu), 0))]
    args = [tile_expert, n_used, xs, wg, wu, wd]
    aliases = {}
    if y_prev is not None:
        in_specs.append(pl.BlockSpec(memory_space=pl.ANY))
        args.append(y_prev)
        aliases = {len(args) - 1: 0}
    return pl.pallas_call(
        functools.partial(_moe_kernel, tile0=tile0, nj=nj),
        out_shape=jax.ShapeDtypeStruct((n_rows, dp), jnp.int32),
        grid_spec=pltpu.PrefetchScalarGridSpec(
            num_scalar_prefetch=2,
            grid=(p // tm, nj),
            in_specs=in_specs,
            out_specs=pl.BlockSpec((tm, dp), lambda t, j, te, nu: (t + tile0, 0)),
            scratch_shapes=[pltpu.VMEM((tm, d), BF16), pltpu.VMEM((tm, d), F32)]),
        input_output_aliases=aliases,
        compiler_params=_params("arbitrary", "arbitrary"),
        name="moe_grouped",
    )(*args)


def _combine_kernel(y1_ref, y2_ref, x_ref, g2_ref, rt_ref, o_ref):
    rt = rt_ref[...]
    w1, w2 = rt[:, 2:3], rt[:, 3:4]
    half = y1_ref.shape[1]
    for k, (a, c) in enumerate(zip(_unpack_bf16_pairs(y1_ref[...]), _unpack_bf16_pairs(y2_ref[...]))):
        cs = slice(k * half, (k + 1) * half)
        o_ref[:, cs] = x_ref[:, cs] + g2_ref[:, cs] * (w1 * a + w2 * c)


def _moe_combine(yg, x, g2, route):
    b, s, d = x.shape
    tm = _pick_tile(s, 512)
    tok = lambda n: pl.BlockSpec((None, tm, n), lambda bi, i: (bi, i, 0))
    return pl.pallas_call(
        _combine_kernel,
        out_shape=jax.ShapeDtypeStruct((b, s, d), F32),
        grid=(b, s // tm),
        in_specs=[tok(d // 2), pl.BlockSpec((None, tm, d // 2), lambda bi, i: (b + bi, i, 0)), tok(d),
                  pl.BlockSpec((None, 1, d), lambda bi, i: (bi, 0, 0)), tok(LANES)],
        out_specs=tok(d),
        compiler_params=_params("parallel", "parallel"),
        name="moe_combine",
    )(yg, yg, x, g2, route)


def _moe(h, route, x, g2, wg, wu, wd):
    b, s, d = x.shape
    n_tok = b * s
    tm = MOE_TM
    rt = route.reshape(n_tok, LANES)
    flat_e = jnp.concatenate([rt[:, 0], rt[:, 1]]).astype(jnp.int32)
    onehot = (flat_e[:, None] == jnp.arange(N_EXPERTS, dtype=jnp.int32)[None, :]).astype(jnp.int32)
    csum = jnp.cumsum(onehot, axis=0)
    rank = jnp.sum(csum * onehot, axis=1) - 1
    counts = csum[-1]
    tiles_e = (counts + tm - 1) // tm
    tiles_cum = jnp.cumsum(tiles_e)
    row_start = (tiles_cum - tiles_e) * tm
    dest = jnp.sum(onehot * row_start[None, :], axis=1) + rank
    n_tiles = 2 * n_tok // tm + N_EXPERTS
    tile_ids = jnp.arange(n_tiles, dtype=jnp.int32)
    tile_expert = jnp.sum((tile_ids[:, None] >= tiles_cum[None, :]).astype(jnp.int32), axis=1)
    last_e = jnp.max(jnp.where(tiles_e > 0, jnp.arange(N_EXPERTS, dtype=jnp.int32), 0))
    tile_expert = jnp.minimum(tile_expert, last_e).astype(jnp.int32)
    n_used = tiles_cum[-1:].astype(jnp.int32)
    dest = dest.astype(jnp.int32)
    row_src = _invert_rows(dest, n_tiles * tm)
    n_chunks = max(c for c in range(1, MOE_CHUNKS + 1) if n_tiles % c == 0)
    tiles_c = n_tiles // n_chunks
    h_flat = h.reshape(n_tok, d // 2)
    xs = [_gather_rows(row_src[c * tiles_c * tm:(c + 1) * tiles_c * tm], h_flat) for c in range(n_chunks)]
    y = None
    for c in range(n_chunks):
        y = _moe_grouped(tile_expert, n_used, xs[c], wg, wu, wd, y, c * tiles_c, n_tiles * tm)
    yg = _gather_rows(dest, y).reshape(2 * b, s, d // 2)
    return _moe_combine(yg, x, g2, route)


def _rope_tables(n):
    rows = n // GRID_W
    r = np.repeat(np.arange(rows), GRID_W).astype(np.float64)
    col = np.tile(np.arange(GRID_W), rows).astype(np.float64)
    freqs = ROPE_BASE ** (-np.arange(ROPE_FREQS, dtype=np.float64) / ROPE_FREQS)
    ang = np.stack([r[:, None] * freqs, col[:, None] * freqs], axis=1)
    ang = np.repeat(ang[:, :, None, :], 2, axis=2).reshape(n, HEAD_DIM)
    ang = np.tile(ang, (1, LANES // HEAD_DIM))
    cos, sin = np.cos(ang), np.sin(ang)
    first_half = (np.arange(LANES) % (2 * ROPE_FREQS)) < ROPE_FREQS
    to_dev = lambda t: jnp.asarray(t.astype(np.float32))
    return to_dev(cos), to_dev(np.where(first_half, -sin, 0.0)), to_dev(np.where(first_half, 0.0, sin))


def _lane_rows(lg):
    return jnp.repeat(lg.astype(F32), HEAD_DIM).reshape(RET_HEADS // 2, 1, LANES)


def kernel(x, c, ctx, c_ctx, ada_w, ada_b, norm1_w, norm2_w, w_in, w_out, q_norm_w, k_norm_w,
           attn_sink, ret_decay_f, ret_decay_b, ret_gn_w, conv_w, conv_b, conv_ln_w, conv_ln_b,
           ffn_w_gate, ffn_w_up, ffn_w_down, router_w, moe_w_gate, moe_w_up, moe_w_down):
    b, n, d = x.shape
    n_ctx = ctx.shape[1]
    depth = ada_w.shape[0]
    cond = jnp.zeros((SUBLANES, d), F32).at[0:b].set(c).at[b].set(c_ctx)
    mods = _adaln(cond, ada_w, ada_b).reshape(depth, SUBLANES, 6, d)
    cos, sa, sb = _rope_tables(n)
    ones_c = jnp.ones((n_ctx, LANES), F32)
    zeros_c = jnp.zeros((n_ctx, LANES), F32)
    zero_state = jnp.zeros((b, RET_HEADS // 2, LANES, LANES), F32)
    row = lambda v: v.reshape(1, -1)
    slab_rows = b * (n // ATT_BLOCK // ATT_QBLOCKS) * BF16_SUBLANES
    riders = {l: {} for l in range(depth)}

    def ride(l, key, w):
        w2 = w.reshape(-1, w.shape[-1])
        if 0 <= l < depth and (n // ATT_BLOCK) % ATT_QBLOCKS == 0 and w2.shape[0] % slab_rows == 0:
            riders[l][key] = w2

    for j, w in enumerate((moe_w_gate, moe_w_up, moe_w_down)):
        ride(j * min(depth, 2) // 3, ("moe", j), w)
    for l in range(depth):
        if l % 2 == 0:
            ride(l, ("ffn", l, 0), ffn_w_gate[l // 2])
            ride(l, ("ffn", l, 1), ffn_w_up[l // 2])
            ride(l, ("ffn", l, 2), ffn_w_down[l // 2])
        ride(l - 1, ("w_in", l), w_in[l])
        ride(l - 1, ("w_out", l), w_out[l])
    rounded = {}

    def bf16_of(key, w):
        return rounded[key].reshape(w.shape) if key in rounded else w.astype(BF16)

    for l in range(depth):
        last = l == depth - 1
        m_lat = [mods[l, 0:b, k][:, None, :] for k in range(6)]
        m_ctx = [jnp.broadcast_to(mods[l, b, k][None, None, :], (b, 1, d)) for k in range(6)]
        w_in_bf = bf16_of(("w_in", l), w_in[l])
        w_out_bf = bf16_of(("w_out", l), w_out[l])
        qw = row(jnp.tile(q_norm_w[l], LANES // HEAD_DIM))
        kw = row(jnp.tile(k_norm_w[l], LANES // HEAD_DIM))
        lgf = _lane_rows(jax.nn.log_sigmoid(ret_decay_f[l].astype(F32)))
        lgb = _lane_rows(jax.nn.log_sigmoid(ret_decay_b[l].astype(F32)))
        sink_tab = jnp.repeat(attn_sink[l].astype(F32), ATT_BLOCK).reshape(ATT_Q_HEADS // 2, 2 * ATT_BLOCK)

        q, k, v, rk, rv, rq, rg, cv = _inproj(x, row(norm1_w[l]), m_lat[1], m_lat[0], w_in_bf,
                                               cos, sa, sb, qw, kw)
        qc, kc, vc, rkc, rvc, rqc, rgc, cvc = _inproj(ctx, row(norm1_w[l]), m_ctx[1], m_ctx[0], w_in_bf,
                                                       ones_c, zeros_c, zeros_c, qw, kw)
        rf_c, s_f = _ret_states(lgf, rkc, rvc, zero_state, reverse=False)
        rb_c, s_b = _ret_states(lgb, rkc, rvc, zero_state, reverse=True)
        rf, _ = _ret_states(lgf, rk, rv, s_f, reverse=False)
        rb, _ = _ret_states(lgb, rk, rv, s_b, reverse=True)

        att, cast_out = _attention(q, k, v, kc, vc, sink_tab, window=True, cast=list(riders[l].values()))
        rounded.update(zip(riders[l].keys(), cast_out))
        ret = _ret_out(lgf, lgb, rq, rk, rv, rg, rf, rb, row(ret_gn_w[l]))
        cnv = _conv(cv, conv_w[l], row(conv_b[l]), row(conv_ln_w[l]), row(conv_ln_b[l]))

        if l % 2 == 0:
            i = l // 2
            wg, wu, wd = (bf16_of(("ffn", l, j), w[i]) for j, w in enumerate((ffn_w_gate, ffn_w_up, ffn_w_down)))
            x_mid, h2 = _outproj(att, ret, cnv, w_out_bf, x, m_lat[2], row(norm2_w[l]), m_lat[4], m_lat[3])
            x_new = _ffn(h2, wg, wu, wd, x_mid, m_lat[5])
        else:
            i = l // 2
            wg, wu, wd = (bf16_of(("moe", j), w)[i] for j, w in enumerate((moe_w_gate, moe_w_up, moe_w_down)))
            x_mid, h2, route = _outproj(att, ret, cnv, w_out_bf, x, m_lat[2], row(norm2_w[l]),
                                        m_lat[4], m_lat[3], router=router_w[i])
            x_new = _moe(h2, route, x_mid, m_lat[5], wg, wu, wd)

        if not last:
            att_c, _ = _attention(qc, None, None, kc, vc, sink_tab, window=False)
            ret_c = _ret_out(lgf, lgb, rqc, rkc, rvc, rgc, rf_c, rb_c, row(ret_gn_w[l]))
            cnv_c = _conv(cvc, conv_w[l], row(conv_b[l]), row(conv_ln_w[l]), row(conv_ln_b[l]))
            if l % 2 == 0:
                c_mid, h2c = _outproj(att_c, ret_c, cnv_c, w_out_bf, ctx, m_ctx[2], row(norm2_w[l]),
                                      m_ctx[4], m_ctx[3])
                ctx = _ffn(h2c, wg, wu, wd, c_mid, m_ctx[5])
            else:
                c_mid, h2c, route_c = _outproj(att_c, ret_c, cnv_c, w_out_bf, ctx, m_ctx[2],
                                               row(norm2_w[l]), m_ctx[4], m_ctx[3],
                                               router=router_w[i])
                ctx = _moe(h2c, route_c, c_mid, m_ctx[5], wg, wu, wd)
        x = x_new
    return x
```

```python
import functools

import jax
import jax.numpy as jnp
import numpy as np
from jax import lax
from jax.experimental import pallas as pl
from jax.experimental.pallas import tpu as pltpu
from jax.experimental.pallas import tpu_sc as plsc

F32 = jnp.float32
BF16 = jnp.bfloat16

GRID_W = 64
HEAD_DIM = 64
ATT_Q_HEADS = 8
ATT_KV_HEADS = 2
ATT_WINDOW = 128
ATT_BLOCK = 128
RET_HEADS = 4
RET_CHUNK = 128
RET_K_SCALE = HEAD_DIM ** -0.5
ATT_SCALE = HEAD_DIM ** -0.5
CONV_CH = 256
CONV_WIDTH = 31
ROPE_BASE = 10000.0
ROPE_FREQS = HEAD_DIM // 4
D_ATT = ATT_Q_HEADS * HEAD_DIM
D_RET = RET_HEADS * HEAD_DIM
ATT_KV_W = ATT_KV_HEADS * HEAD_DIM
C_ATT_K = 0
C_ATT_V = C_ATT_K + ATT_KV_W
C_RET_K = C_ATT_V + ATT_KV_W
C_RET_V = C_RET_K + D_RET
C_ATT_Q = C_RET_V + D_RET
C_RET_Q = C_ATT_Q + D_ATT
C_RET_G = C_RET_Q + D_RET
C_CONV = C_RET_G + D_RET
N_EXPERTS = 8
EPS = 1e-6
NEG_INF = -1e30

LANES = 128
SUBLANES = 8
BF16_SUBLANES = 16
VMEM_LIMIT = 48 * 1024 * 1024
CONV_HALO = 16
CONV_SUB = 32
CONV_CHAINS = 4
ATT_QBLOCKS = 4
RET_UNROLL = 4
MOE_TM = 512
MOE_CHUNKS = 2
MOE_SUB = 512
FFN_SUB = 512
MOE_TF = 1792
SC_LANES = 16
SC_WINDOW = 64


def _params(*sem):
    return pltpu.CompilerParams(dimension_semantics=sem, vmem_limit_bytes=VMEM_LIMIT)


def _sigmoid(x):
    return 1.0 / (1.0 + jnp.exp(-x))


def _pack_bf16_pairs(v):
    c = v.shape[1] // 2
    bits = pltpu.bitcast(v.astype(BF16).astype(F32), jnp.uint32)
    packed = (bits[:, c:] & jnp.uint32(0xFFFF0000)) | (bits[:, :c] >> 16)
    return pltpu.bitcast(packed, jnp.int32)


def _unpack_bf16_pairs(p):
    bits = pltpu.bitcast(p, jnp.uint32)
    return pltpu.bitcast(bits << 16, F32), pltpu.bitcast(bits & jnp.uint32(0xFFFF0000), F32)


def _pick_tile(n, pref):
    t = min(n, pref)
    assert n % t == 0, (n, t)
    return t


def _adaln_kernel(c_ref, w_ref, b_ref, o_ref):
    c = c_ref[...]
    s = c * _sigmoid(c)
    o_ref[...] = jnp.dot(s, w_ref[...], preferred_element_type=F32,
                         precision=lax.Precision.HIGHEST) + b_ref[...]


def _adaln(cond, ada_w, ada_b):
    depth, d, n = ada_w.shape
    tn = _pick_tile(n, 1536)
    return pl.pallas_call(
        _adaln_kernel,
        out_shape=jax.ShapeDtypeStruct((depth, cond.shape[0], n), F32),
        grid=(depth, n // tn),
        in_specs=[pl.BlockSpec(cond.shape, lambda l, j: (0, 0)),
                  pl.BlockSpec((None, d, tn), lambda l, j: (l, 0, j)),
                  pl.BlockSpec((None, 1, tn), lambda l, j: (l, 0, j))],
        out_specs=pl.BlockSpec((None, cond.shape[0], tn), lambda l, j: (l, 0, j)),
        compiler_params=_params("parallel", "parallel"),
        name="adaln",
    )(cond, ada_w, ada_b.reshape(depth, 1, n))


def _modulated_rms(x, nw, sc, sh):
    ms = jnp.mean(x * x, axis=-1, keepdims=True)
    return (x * lax.rsqrt(ms + EPS) * nw) * (1.0 + sc) + sh


def _head_group_matrix():
    r = lax.broadcasted_iota(jnp.int32, (2 * LANES, 2 * LANES), 0) // HEAD_DIM
    c = lax.broadcasted_iota(jnp.int32, (2 * LANES, 2 * LANES), 1) // HEAD_DIM
    return jnp.where(r == c, 1.0, 0.0).astype(BF16)


def _per_head_sum(v, grp):
    hi = v.astype(BF16)
    lo = (v - hi.astype(F32)).astype(BF16)
    r = jnp.dot(jnp.concatenate([hi, lo], axis=1), grp, preferred_element_type=F32)
    return r[:, 0:LANES] + r[:, LANES:]


def _dup_halves(t):
    sw = pltpu.roll(t, HEAD_DIM, 1)
    lo = lax.broadcasted_iota(jnp.int32, t.shape, 1) < HEAD_DIM
    return jnp.where(lo, t, sw), jnp.where(lo, sw, t)


def _inproj_kernel(x_ref, nw_ref, sc_ref, sh_ref, w_ref, cos_ref, sa_ref, sb_ref, qw_ref, kw_ref,
                   q_ref, k_ref, v_ref, rk_ref, rv_ref, rq_ref, rg_ref, cv_ref):
    hb = _modulated_rms(x_ref[...], nw_ref[...], sc_ref[...], sh_ref[...]).astype(BF16)

    def proj(c0, n):
        return jnp.dot(hb, w_ref[:, c0:c0 + n], preferred_element_type=F32)

    grp = _head_group_matrix()
    cos, sa, sb = cos_ref[...], sa_ref[...], sb_ref[...]

    def norm_rope(p, wrow):
        y = p * lax.rsqrt(_per_head_sum(p * p, grp) * (1.0 / HEAD_DIM) + EPS) * wrow
        return (y * cos + pltpu.roll(y, LANES - ROPE_FREQS, 1) * sa
                + pltpu.roll(y, ROPE_FREQS, 1) * sb)

    def put_kv(kv):
        k0, k1 = _dup_halves(norm_rope(kv[:, 0:ATT_KV_W], kw_ref[...]))
        k_ref[:, 0:LANES] = k0.astype(BF16)
        k_ref[:, LANES:2 * LANES] = k1.astype(BF16)
        v_ref[...] = kv[:, ATT_KV_W:].T.astype(BF16)

    def put_q(qall):
        qw = qw_ref[...] * ATT_SCALE
        for j in range(D_ATT // LANES):
            q_ref[:, j * LANES:(j + 1) * LANES] = norm_rope(qall[:, j * LANES:(j + 1) * LANES], qw).astype(BF16)

    def put_rk(p):
        rk_ref[...] = (p * RET_K_SCALE).astype(BF16)

    def put_rv(p):
        rv_ref[...] = p.astype(BF16)

    def put_rq(p):
        rq_ref[...] = p.astype(BF16)

    def put_rg(p):
        rg_ref[...] = p

    def put_conv(p):
        cv_ref[...] = p[:, 0:CONV_CH] * _sigmoid(p[:, CONV_CH:])

    units = [(C_ATT_K, 2 * ATT_KV_W, put_kv), (C_ATT_Q, D_ATT, put_q), (C_RET_K, D_RET, put_rk),
             (C_RET_V, D_RET, put_rv), (C_RET_Q, D_RET, put_rq), (C_RET_G, D_RET, put_rg),
             (C_CONV, 2 * CONV_CH, put_conv)]
    ahead = 2
    pending = [proj(c0, n) for c0, n, _ in units[:ahead]]
    for n, (_, _, put) in enumerate(units):
        if n + ahead < len(units):
            pending.append(proj(*units[n + ahead][:2]))
        put(pending.pop(0))


def _inproj(x, nw, sc, sh, w_bf, cos, sa, sb, qw, kw):
    b, s, d = x.shape
    tm = _pick_tile(s, 512)
    row = lambda n: pl.BlockSpec((1, n), lambda bi, i: (0, 0))
    per_b = pl.BlockSpec((None, 1, d), lambda bi, i: (bi, 0, 0))
    tab = pl.BlockSpec((tm, LANES), lambda bi, i: (i, 0))
    tok = lambda n: pl.BlockSpec((None, tm, n), lambda bi, i: (bi, i, 0))
    shp = lambda n, dt: jax.ShapeDtypeStruct((b, s, n), dt)
    return pl.pallas_call(
        _inproj_kernel,
        out_shape=(shp(D_ATT, BF16), shp(2 * LANES, BF16), jax.ShapeDtypeStruct((b, ATT_KV_W, s), BF16),
                   shp(D_RET, BF16), shp(D_RET, BF16), shp(D_RET, BF16), shp(D_RET, F32),
                   shp(CONV_CH, F32)),
        grid=(b, s // tm),
        in_specs=[tok(d), row(d), per_b, per_b,
                  pl.BlockSpec(w_bf.shape, lambda bi, i: (0, 0)),
                  tab, tab, tab, row(LANES), row(LANES)],
        out_specs=(tok(D_ATT), tok(2 * LANES), pl.BlockSpec((None, ATT_KV_W, tm), lambda bi, i: (bi, 0, i)),
                   tok(D_RET), tok(D_RET), tok(D_RET), tok(D_RET), tok(CONV_CH)),
        compiler_params=_params("parallel", "parallel"),
        name="inproj",
    )(x, nw, sc, sh, w_bf, cos, sa, sb, qw, kw)


def _attn_kernel(*refs, window, nq, n_cast):
    blk = ATT_BLOCK
    n_in = len(refs) - 1 - n_cast
    for src, dst in zip(refs[n_in - n_cast:n_in], refs[n_in + 1:]):
        dst[...] = src[...].astype(BF16)
    refs = refs[:n_in - n_cast] + refs[n_in:n_in + 1]
    if window:
        q_ref = refs[0]
        k_refs = refs[1:nq + 3]
        v_refs = refs[nq + 3:2 * nq + 5]
        kx_ref, vx_ref, sink_ref, o_ref = refs[2 * nq + 5:]
    else:
        q_ref, kx_ref, vx_ref, sink_ref, o_ref = refs
    n_ctx = kx_ref.shape[0]
    nk = 3 * blk + n_ctx if window else n_ctx
    if window:
        i = pl.program_id(1)
        last = nq * pl.num_programs(1) - 1
        key = lax.broadcasted_iota(jnp.int32, (blk, 2 * blk), 0)
        qry = lax.broadcasted_iota(jnp.int32, (blk, 2 * blk), 1) & (blk - 1)

        def band_masks(sub):
            off_prev = jnp.where(nq * i + sub > 0, 0, blk)
            off_next = jnp.where(nq * i + sub < last, 0, blk)
            return key >= qry + off_prev, key + off_next <= qry
    first_head = lax.broadcasted_iota(jnp.int32, (blk, LANES), 1) < HEAD_DIM
    ones = jnp.ones((2 * SUBLANES, nk), BF16)

    def scores(sub, pair):
        g = pair // 2
        gs = slice(g * LANES, (g + 1) * LANES)
        if window:
            kcat = jnp.concatenate([r[:, gs] for r in k_refs[sub:sub + 3]] + [kx_ref[:, gs]], axis=0)
        else:
            kcat = kx_ref[:, gs]
        qp = q_ref[sub * blk:(sub + 1) * blk, pair * LANES:(pair + 1) * LANES]
        zero = jnp.zeros_like(qp)
        w = jnp.concatenate([jnp.where(first_head, qp, zero), jnp.where(first_head, zero, qp)], axis=0)
        return lax.dot_general(kcat, w, (((1,), (1,)), ((), ())), preferred_element_type=F32)

    def masked_max(sub, pair, s):
        if window:
            mask_prev, mask_next = band_masks(sub)
            parts = [jnp.where(mask_prev, s[0:blk], NEG_INF), s[blk:2 * blk],
                     jnp.where(mask_next, s[2 * blk:3 * blk], NEG_INF), s[3 * blk:]]
        else:
            parts = [s]
        m = sink_ref[pair:pair + 1, :]
        for part in parts:
            m = jnp.maximum(m, jnp.max(part, axis=0, keepdims=True))
        return parts, m

    def softmax(sub, pair, parts_m):
        parts, m = parts_m
        p = jnp.concatenate([jnp.exp(part - m).astype(BF16) for part in parts], axis=0)
        return p, jnp.exp(sink_ref[pair:pair + 1, :] - m)

    def output(sub, pair, p, sink_p):
        g = pair // 2
        vs = slice(g * HEAD_DIM, (g + 1) * HEAD_DIM)
        if window:
            vt = jnp.concatenate([r[vs, :] for r in v_refs[sub:sub + 3]] + [vx_ref[vs, :]], axis=1)
        else:
            vt = vx_ref[vs, :]
        vaug = jnp.concatenate([vt, ones], axis=0)
        o = jnp.dot(vaug, p, preferred_element_type=F32)
        on = o[0:HEAD_DIM, :] * (1.0 / (o[HEAD_DIM:HEAD_DIM + 1, :] + sink_p))
        ot = jnp.concatenate([on[:, 0:blk], on[:, blk:2 * blk]], axis=0)
        o_ref[sub * blk:(sub + 1) * blk, pair * LANES:(pair + 1) * LANES] = ot.T.astype(BF16)

    units = [(sub, pair) for sub in range(nq) for pair in range(ATT_Q_HEADS // 2)]
    stages = (lambda sub, pair, _: scores(sub, pair), masked_max, softmax,
              lambda sub, pair, ps: output(sub, pair, *ps))
    vals = {}
    for step in range(len(units) + len(stages) - 1):
        for k, stage in enumerate(stages):
            u = step - k
            if 0 <= u < len(units):
                vals[u, k] = stage(*units[u], vals.pop((u, k - 1), None))


def _attention(q, k, vt, kx, vxt, sink_tab, window, cast=()):
    b, s, _ = q.shape
    blk = ATT_BLOCK
    nb = s // blk
    nq = ATT_QBLOCKS if nb % ATT_QBLOCKS == 0 else 1
    n_ctx = kx.shape[1]
    qspec = pl.BlockSpec((None, nq * blk, D_ATT), lambda bi, i: (bi, i, 0))
    kctx = pl.BlockSpec((None, n_ctx, 2 * LANES), lambda bi, i: (bi, 0, 0))
    vctx = pl.BlockSpec((None, ATT_KV_W, n_ctx), lambda bi, i: (bi, 0, 0))
    snk = pl.BlockSpec(sink_tab.shape, lambda bi, i: (0, 0))
    if window:
        at = lambda off: (lambda i: jnp.clip(nq * i + off, 0, nb - 1))
        kspec = lambda f: pl.BlockSpec((None, blk, 2 * LANES), lambda bi, i: (bi, f(i), 0))
        vspec = lambda f: pl.BlockSpec((None, ATT_KV_W, blk), lambda bi, i: (bi, 0, f(i)))
        offs = range(-1, nq + 1)
        in_specs = ([qspec] + [kspec(at(o)) for o in offs] + [vspec(at(o)) for o in offs]
                    + [kctx, vctx, snk])
        args = (q,) + (k,) * (nq + 2) + (vt,) * (nq + 2) + (kx, vxt, sink_tab)
    else:
        in_specs = [qspec, kctx, vctx, snk]
        args = (q, kx, vxt, sink_tab)
    steps = nb // nq
    out_shape, out_specs = [jax.ShapeDtypeStruct((b, s, D_ATT), BF16)], [qspec]
    for w in cast:
        rows = w.shape[0] // (b * steps)
        assert rows * b * steps == w.shape[0] and rows % (2 * SUBLANES) == 0, w.shape
        slab = pl.BlockSpec((rows, w.shape[1]), lambda bi, i: (bi * steps + i, 0))
        in_specs.append(slab)
        out_specs.append(slab)
        out_shape.append(jax.ShapeDtypeStruct(w.shape, BF16))
    res = pl.pallas_call(
        functools.partial(_attn_kernel, window=window, nq=nq, n_cast=len(cast)),
        out_shape=tuple(out_shape),
        grid=(b, steps),
        in_specs=in_specs,
        out_specs=tuple(out_specs),
        compiler_params=_params("parallel", "parallel"),
        name="attention_window" if window else "attention_ctx",
    )(*args, *cast)
    return res[0], tuple(res[1:])


def _ret_state_kernel(lgl_ref, k_ref, v_ref, s0_ref, r_ref, fin_ref, s_scr, *, cb, reverse):
    c = pl.program_id(1)
    L = RET_CHUNK

    @pl.when(c == 0)
    def _():
        s_scr[...] = s0_ref[...]

    jj = lax.broadcasted_iota(jnp.int32, (L, LANES), 0).astype(F32)
    expo = jj if reverse else (L - 1.0) - jj
    same_head = (lax.broadcasted_iota(jnp.int32, (LANES, LANES), 0) // HEAD_DIM
                 == lax.broadcasted_iota(jnp.int32, (LANES, LANES), 1) // HEAD_DIM)
    for pr in range(RET_HEADS // 2):
        cs = slice(pr * LANES, (pr + 1) * LANES)
        lgl = lgl_ref[pr]
        kdec = jnp.exp(expo * lgl)
        cdec = jnp.exp(float(L) * lgl)
        state = s_scr[pr]
        for t in range(cb):
            cc = cb - 1 - t if reverse else t
            rows = slice(cc * L, (cc + 1) * L)
            r_ref[cc, pr] = state.astype(BF16)
            kd = k_ref[rows, cs].astype(F32) * kdec
            u = jnp.dot(kd.T.astype(BF16), v_ref[rows, cs], preferred_element_type=F32)
            state = cdec * state + jnp.where(same_head, u, 0.0)
        s_scr[pr] = state

    @pl.when(c == pl.num_programs(1) - 1)
    def _():
        fin_ref[...] = s_scr[...]


def _ret_states(lgl, rk, rv, s0, reverse):
    b, s, _ = rk.shape
    nc = s // RET_CHUNK
    cb = _pick_tile(nc, 8)
    nblk = nc // cb
    npair = RET_HEADS // 2
    blk_idx = (lambda c: nblk - 1 - c) if reverse else (lambda c: c)
    tok = pl.BlockSpec((None, cb * RET_CHUNK, D_RET), lambda bi, c: (bi, blk_idx(c), 0))
    st = pl.BlockSpec((None, npair, LANES, LANES), lambda bi, c: (bi, 0, 0, 0))
    return pl.pallas_call(
        functools.partial(_ret_state_kernel, cb=cb, reverse=reverse),
        out_shape=(jax.ShapeDtypeStruct((b, nc, npair, LANES, LANES), BF16),
                   jax.ShapeDtypeStruct((b, npair, LANES, LANES), F32)),
        grid=(b, nblk),
        in_specs=[pl.BlockSpec(lgl.shape, lambda bi, c: (0, 0, 0)), tok, tok, st],
        out_specs=(pl.BlockSpec((None, cb, npair, LANES, LANES),
                                lambda bi, c: (bi, blk_idx(c), 0, 0, 0)), st),
        scratch_shapes=[pltpu.VMEM((npair, LANES, LANES), F32)],
        compiler_params=_params("parallel", "arbitrary"),
        name="ret_state_bwd" if reverse else "ret_state_fwd",
    )(lgl, rk, rv, s0)


def _ret_out_kernel(lgf_ref, lgb_ref, q_ref, k_ref, v_ref, g_ref, rf_ref, rb_ref, gnw_ref, o_ref, *, cb):
    L = RET_CHUNK
    i0 = lax.broadcasted_iota(jnp.int32, (L, LANES), 0).astype(F32)
    i1 = lax.broadcasted_iota(jnp.int32, (L, LANES), 1).astype(F32)
    diff = i0 - i1
    lo = lax.broadcasted_iota(jnp.int32, (L, LANES), 1) < HEAD_DIM
    grp = _head_group_matrix()
    inv = 1.0 / HEAD_DIM
    tabs = []
    for pr in range(RET_HEADS // 2):
        lgf, lgb = lgf_ref[pr], lgb_ref[pr]
        dmat = [jnp.where(diff >= 0.0,
                          jnp.exp(jnp.maximum(diff, 0.0) * lgf[:, a:a + 1]),
                          jnp.exp(jnp.maximum(-diff, 0.0) * lgb[:, a:a + 1]))
                for a in (0, HEAD_DIM)]
        tabs.append((jnp.exp((i0 + 1.0) * lgf), jnp.exp((float(L) - i0) * lgb),
                     jnp.concatenate(dmat, axis=0), gnw_ref[:, pr * LANES:(pr + 1) * LANES]))

    def decayed_scores(t, pr):
        rows = pl.ds(pl.multiple_of(t * L, L), L)
        cs = slice(pr * LANES, (pr + 1) * LANES)
        qp, kp = q_ref[rows, cs], k_ref[rows, cs]
        zero = jnp.zeros_like(kp)
        qz = jnp.concatenate([jnp.where(lo, qp, zero), jnp.where(lo, zero, qp)], axis=0)
        return lax.dot_general(qz, kp, (((1,), (1,)), ((), ())), preferred_element_type=F32) * tabs[pr][2]

    def mix(t, pr, a):
        rows = pl.ds(pl.multiple_of(t * L, L), L)
        cs = slice(pr * LANES, (pr + 1) * LANES)
        xif, xib, _, _ = tabs[pr]
        vp = v_ref[rows, cs]
        zero = jnp.zeros_like(vp)
        qf = q_ref[rows, cs].astype(F32)
        lhs = jnp.concatenate([a[0:L].astype(BF16), a[L:].astype(BF16),
                               (qf * xif).astype(BF16), (qf * xib).astype(BF16)], axis=1)
        rhs = jnp.concatenate([jnp.where(lo, vp, zero), jnp.where(lo, zero, vp),
                               rf_ref[t, pr], rb_ref[t, pr]], axis=0)
        return jnp.dot(lhs, rhs, preferred_element_type=F32)

    def centre(t, pr, acc):
        return acc - _per_head_sum(acc, grp) * inv

    def readout(t, pr, dl):
        rows = pl.ds(pl.multiple_of(t * L, L), L)
        cs = slice(pr * LANES, (pr + 1) * LANES)
        var = _per_head_sum(dl * dl, grp) * inv
        y = dl * lax.rsqrt(var + EPS) * tabs[pr][3]
        gt = g_ref[rows, cs]
        o_ref[rows, cs] = (gt * _sigmoid(gt) * y).astype(BF16)

    unroll = min(RET_UNROLL, cb)
    stages = (lambda t, pr, _: decayed_scores(t, pr), mix, centre, readout)

    def group(gi, carry):
        units = [(gi * unroll + u, pr) for u in range(unroll) for pr in range(RET_HEADS // 2)]
        vals = {}
        for step in range(len(units) + len(stages) - 1):
            for k, stage in enumerate(stages):
                u = step - k
                if 0 <= u < len(units):
                    vals[u, k] = stage(*units[u], vals.pop((u, k - 1), None))
        return carry

    lax.fori_loop(0, cb // unroll, group, 0)


def _ret_out(lgf, lgb, rq, rk, rv, rg, rf, rb, gnw):
    b, s, _ = rq.shape
    nc = s // RET_CHUNK
    cb = _pick_tile(nc, 8)
    npair = RET_HEADS // 2
    tok = pl.BlockSpec((None, cb * RET_CHUNK, D_RET), lambda bi, c: (bi, c, 0))
    st = pl.BlockSpec((None, cb, npair, LANES, LANES), lambda bi, c: (bi, c, 0, 0, 0))
    lg = pl.BlockSpec(lgf.shape, lambda bi, c: (0, 0, 0))
    return pl.pallas_call(
        functools.partial(_ret_out_kernel, cb=cb),
        out_shape=jax.ShapeDtypeStruct((b, s, D_RET), BF16),
        grid=(b, nc // cb),
        in_specs=[lg, lg, tok, tok, tok, tok, st, st, pl.BlockSpec((1, D_RET), lambda bi, c: (0, 0))],
        out_specs=tok,
        compiler_params=_params("parallel", "parallel"),
        name="ret_out",
    )(lgf, lgb, rq, rk, rv, rg, rf, rb, gnw)


def _conv_kernel(prev_ref, cur_ref, next_ref, w_ref, b_ref, lnw_ref, lnb_ref, o_ref,
                 xpad, shifted, hbuf, wtile):
    i = pl.program_id(1)
    tm = cur_ref.shape[0]
    halo = CONV_HALO
    first = i == 0
    last = i == pl.num_programs(1) - 1
    xpad[0:halo, :] = jnp.where(first, 0.0, prev_ref[...])
    xpad[halo:halo + tm, :] = cur_ref[...]
    xpad[halo + tm:2 * halo + tm, :] = jnp.where(last, 0.0, next_ref[...])
    span = tm + 2 * halo - SUBLANES
    for r in range(SUBLANES):
        shifted[r] = xpad[r:r + span, :]
    base = halo - CONV_WIDTH // 2
    bias = b_ref[...]
    ch = cur_ref.shape[1]
    for w in range(CONV_WIDTH):
        wtile[w] = jnp.broadcast_to(w_ref[w:w + 1, :], (SUBLANES, ch))

    taps_by_shift = {}
    for w in range(CONV_WIDTH):
        taps_by_shift.setdefault((base + w) % SUBLANES, []).append(((base + w) // SUBLANES, w))

    def sub(sb, carry):
        r0 = pl.multiple_of(sb * CONV_SUB, CONV_SUB)
        chains = [None] * CONV_CHAINS
        groups = CONV_SUB // SUBLANES
        for r, taps in sorted(taps_by_shift.items()):
            a_lo = min(a for a, _ in taps)
            n_g = max(a for a, _ in taps) - a_lo + groups
            win = shifted[r, pl.ds(r0 + a_lo * SUBLANES, n_g * SUBLANES), :].reshape(n_g, SUBLANES, ch)
            for a, w in taps:
                term = win[a - a_lo:a - a_lo + groups] * wtile[w]
                c = w % CONV_CHAINS
                chains[c] = term if chains[c] is None else chains[c] + term
        total = (chains[0] + chains[1]) + (chains[2] + chains[3])
        hbuf[pl.ds(r0, CONV_SUB), :] = total.reshape(CONV_SUB, ch) + bias
        return carry

    lax.fori_loop(0, tm // CONV_SUB, sub, 0)
    h = hbuf[...]
    mu = jnp.mean(h, axis=-1, keepdims=True)
    dl = h - mu
    var = jnp.mean(dl * dl, axis=-1, keepdims=True)
    y = dl * lax.rsqrt(var + EPS) * lnw_ref[...] + lnb_ref[...]
    o_ref[...] = (y * _sigmoid(y)).astype(BF16)


def _conv(cv, w, bias, lnw, lnb):
    b, s, ch = cv.shape
    tm = _pick_tile(s, 512)
    hpb = tm // CONV_HALO
    nh = s // CONV_HALO
    row = pl.BlockSpec((1, ch), lambda bi, i: (0, 0))
    span = tm + 2 * CONV_HALO - SUBLANES
    return pl.pallas_call(
        _conv_kernel,
        out_shape=jax.ShapeDtypeStruct((b, s, ch), BF16),
        grid=(b, s // tm),
        in_specs=[pl.BlockSpec((None, CONV_HALO, ch), lambda bi, i: (bi, jnp.maximum(i * hpb - 1, 0), 0)),
                  pl.BlockSpec((None, tm, ch), lambda bi, i: (bi, i, 0)),
                  pl.BlockSpec((None, CONV_HALO, ch),
                               lambda bi, i: (bi, jnp.minimum((i + 1) * hpb, nh - 1), 0)),
                  pl.BlockSpec(w.shape, lambda bi, i: (0, 0)), row, row, row],
        out_specs=pl.BlockSpec((None, tm, ch), lambda bi, i: (bi, i, 0)),
        scratch_shapes=[pltpu.VMEM((tm + 2 * CONV_HALO, ch), F32),
                        pltpu.VMEM((SUBLANES, span, ch), F32),
                        pltpu.VMEM((tm, ch), F32),
                        pltpu.VMEM((CONV_WIDTH, SUBLANES, ch), F32)],
        compiler_params=_params("parallel", "parallel"),
        name="conv",
    )(cv, cv, cv, w, bias, lnw, lnb)


def _outproj_route_kernel(att_ref, ret_ref, cnv_ref, w_ref, x_ref, g1_ref, nw_ref, sc_ref, sh_ref, rcat_ref,
                          xo_ref, h_ref, rt_ref):
    y = (jnp.dot(att_ref[...], w_ref[0:D_ATT, :], preferred_element_type=F32)
         + jnp.dot(ret_ref[...], w_ref[D_ATT:D_ATT + D_RET, :], preferred_element_type=F32)
         + jnp.dot(cnv_ref[...], w_ref[D_ATT + D_RET:, :], preferred_element_type=F32))
    xn = x_ref[...] + g1_ref[...] * y
    xo_ref[...] = xn
    h = _modulated_rms(xn, nw_ref[...], sc_ref[...], sh_ref[...])
    h_ref[...] = _pack_bf16_pairs(h)
    hi = h.astype(BF16)
    lo = (h - hi.astype(F32)).astype(BF16)
    tm = h.shape[0]
    r = jnp.dot(jnp.concatenate([hi, lo], axis=0), rcat_ref[...], preferred_element_type=F32)
    logits = (r[0:tm, 0:LANES] + r[0:tm, LANES:]) + (r[tm:, 0:LANES] + r[tm:, LANES:])
    lane = lax.broadcasted_iota(jnp.int32, logits.shape, 1).astype(F32)
    logits = jnp.where(lane < N_EXPERTS, logits, NEG_INF)
    m1 = jnp.max(logits, axis=-1, keepdims=True)
    i1 = jnp.min(jnp.where(logits == m1, lane, float(LANES)), axis=-1, keepdims=True)
    rest = jnp.where(lane == i1, NEG_INF, logits)
    m2 = jnp.max(rest, axis=-1, keepdims=True)
    i2 = jnp.min(jnp.where(rest == m2, lane, float(LANES)), axis=-1, keepdims=True)
    e2 = jnp.exp(m2 - m1)
    w1 = 1.0 / (1.0 + e2)
    w2 = e2 / (1.0 + e2)
    rt_ref[...] = jnp.where(lane == 0.0, i1,
                            jnp.where(lane == 1.0, i2,
                                      jnp.where(lane == 2.0, w1, jnp.where(lane == 3.0, w2, 0.0))))


def _outproj_route(att, ret, cnv, w_bf, x, g1, nw, sc, sh, router):
    b, s, d = x.shape
    tm = _pick_tile(s, 512)
    tok = lambda n: pl.BlockSpec((None, tm, n), lambda bi, i: (bi, i, 0))
    per_b = pl.BlockSpec((None, 1, d), lambda bi, i: (bi, 0, 0))
    rpad = jnp.zeros((d, LANES), F32).at[:, :N_EXPERTS].set(router)
    rhi = rpad.astype(BF16)
    rlo = (rpad - rhi.astype(F32)).astype(BF16)
    return pl.pallas_call(
        _outproj_route_kernel,
        out_shape=(jax.ShapeDtypeStruct((b, s, d), F32), jax.ShapeDtypeStruct((b, s, d // 2), jnp.int32),
                   jax.ShapeDtypeStruct((b, s, LANES), F32)),
        grid=(b, s // tm),
        in_specs=[tok(D_ATT), tok(D_RET), tok(CONV_CH), pl.BlockSpec(w_bf.shape, lambda bi, i: (0, 0)),
                  tok(d), per_b, pl.BlockSpec((1, d), lambda bi, i: (0, 0)), per_b, per_b,
                  pl.BlockSpec((d, 2 * LANES), lambda bi, i: (0, 0))],
        out_specs=(tok(d), tok(d // 2), tok(LANES)),
        compiler_params=_params("parallel", "parallel"),
        name="outproj_route",
    )(att, ret, cnv, w_bf, x, g1, nw, sc, sh, jnp.concatenate([rhi, rlo], axis=1))


def _swiglu_chunks(h, wg_ref, wu_ref, wd_ref, sub):
    total = None
    for c0 in range(0, wg_ref.shape[1], sub):
        cs = slice(c0, min(c0 + sub, wg_ref.shape[1]))
        gate = jnp.dot(h, wg_ref[:, cs].astype(BF16), preferred_element_type=F32)
        up = jnp.dot(h, wu_ref[:, cs].astype(BF16), preferred_element_type=F32)
        act = (gate * _sigmoid(gate) * up).astype(BF16)
        part = jnp.dot(act, wd_ref[cs, :].astype(BF16), preferred_element_type=F32)
        total = part if total is None else total + part
    return total


def _mixer_ffn_kernel(att_ref, ret_ref, cnv_ref, wo_ref, x_ref, g1_ref, nw_ref, sc_ref, sh_ref,
                      wg_ref, wu_ref, wd_ref, g2_ref, o_ref):
    y = (jnp.dot(att_ref[...], wo_ref[0:D_ATT, :], preferred_element_type=F32)
         + jnp.dot(ret_ref[...], wo_ref[D_ATT:D_ATT + D_RET, :], preferred_element_type=F32)
         + jnp.dot(cnv_ref[...], wo_ref[D_ATT + D_RET:, :], preferred_element_type=F32))
    xn = x_ref[...] + g1_ref[...] * y
    h = _modulated_rms(xn, nw_ref[...], sc_ref[...], sh_ref[...]).astype(BF16)
    o_ref[...] = xn + g2_ref[...] * _swiglu_chunks(h, wg_ref, wu_ref, wd_ref, FFN_SUB)


def _mixer_ffn(att, ret, cnv, wo, x, g1, nw, sc, sh, wg, wu, wd, g2):
    b, s, d = x.shape
    f = wg.shape[1]
    tm = _pick_tile(s, 512)
    tok = lambda n: pl.BlockSpec((None, tm, n), lambda bi, i: (bi, i, 0))
    per_b = pl.BlockSpec((None, 1, d), lambda bi, i: (bi, 0, 0))
    res = lambda shape: pl.BlockSpec(shape, lambda bi, i: (0, 0), pipeline_mode=pl.Buffered(1))
    return pl.pallas_call(
        _mixer_ffn_kernel,
        out_shape=jax.ShapeDtypeStruct((b, s, d), F32),
        grid=(b, s // tm),
        in_specs=[tok(D_ATT), tok(D_RET), tok(CONV_CH), res(wo.shape), tok(d), per_b,
                  pl.BlockSpec((1, d), lambda bi, i: (0, 0)), per_b, per_b,
                  res((d, f)), res((d, f)), res((f, d)), per_b],
        out_specs=tok(d),
        compiler_params=_params("parallel", "parallel"),
        name="mixer_ffn_dense",
    )(att, ret, cnv, wo, x, g1, nw, sc, sh, wg, wu, wd, g2)


def _gather_rows(idx, src):
    n = idx.shape[0]
    d = src.shape[1]
    mesh = plsc.VectorSubcoreMesh(core_axis_name="core", subcore_axis_name="subcore")

    @pl.kernel(out_type=jax.ShapeDtypeStruct((n, d), src.dtype), mesh=mesh, name="moe_gather")
    def gather(src_hbm, idx_hbm, out_hbm):
        def body(idx_vmem, out_vmem):
            pltpu.sync_copy(src_hbm.at[idx_vmem.at[0, pl.ds(0, SC_WINDOW)]], out_vmem)

        pltpu.emit_pipeline(
            body,
            grid=(n // SC_WINDOW,),
            in_specs=[pl.BlockSpec((1, LANES), lambda i: (i, 0))],
            out_specs=[pl.BlockSpec((SC_WINDOW, d), lambda i: (i, 0))],
            core_axis_name=("core", "subcore"),
            dimension_semantics=(pltpu.PARALLEL,),
        )(idx_hbm, out_hbm)

    idx_rows = jnp.pad(idx.reshape(n // SC_WINDOW, SC_WINDOW), ((0, 0), (0, LANES - SC_WINDOW)))
    return gather(src, idx_rows)


def _invert_rows(dest, n_rows):
    n = dest.shape[0]
    n_tok = n // 2
    mesh = plsc.VectorSubcoreMesh(core_axis_name="core", subcore_axis_name="subcore")

    @pl.kernel(out_type=jax.ShapeDtypeStruct((n_rows,), jnp.int32), mesh=mesh, name="moe_row_src",
               scratch_types=[pltpu.VMEM((n,), jnp.int32), pltpu.VMEM((n_rows,), jnp.int32)],
               compiler_params=pltpu.CompilerParams(needs_layout_passes=False))
    def invert(dest_hbm, out_hbm, dest_vmem, rows_vmem):
        @pl.when((lax.axis_index("core") == 0) & (lax.axis_index("subcore") == 0))
        def _():
            pltpu.sync_copy(dest_hbm, dest_vmem)

            @pl.loop(0, n_rows, step=SC_LANES)
            def _(r):
                rows_vmem[pl.ds(r, SC_LANES)] = lax.rem(lax.iota(jnp.int32, SC_LANES) + r, n_tok)

            @pl.loop(0, n, step=SC_LANES)
            def _(a):
                tok = lax.iota(jnp.int32, SC_LANES) + a
                tok = jnp.where(tok >= n_tok, tok - n_tok, tok)
                plsc.store_scatter(rows_vmem, [dest_vmem[pl.ds(a, SC_LANES)]], tok)

            pltpu.sync_copy(rows_vmem, out_hbm)

    return invert(dest)


def _moe_kernel(te_ref, nu_ref, x_ref, wg_ref, wu_ref, wd_ref, *rest, tile0, nj):
    o_ref, xb, acc = rest[-3:]
    t = pl.program_id(0) + tile0
    j = pl.program_id(1)
    used = t < nu_ref[0]

    def partial_sum():
        return _swiglu_chunks(xb[...], wg_ref, wu_ref, wd_ref, MOE_SUB)

    @pl.when(used & (j == 0))
    def _():
        lo, hi = _unpack_bf16_pairs(x_ref[...])
        half = lo.shape[1]
        xb[:, 0:half] = lo.astype(BF16)
        xb[:, half:] = hi.astype(BF16)
        if nj == 1:
            o_ref[...] = _pack_bf16_pairs(partial_sum())
        else:
            acc[...] = partial_sum()

    if nj > 2:
        @pl.when(used & (j > 0) & (j < nj - 1))
        def _():
            acc[...] += partial_sum()

    if nj > 1:
        @pl.when(used & (j == nj - 1))
        def _():
            o_ref[...] = _pack_bf16_pairs(acc[...] + partial_sum())

    @pl.when(jnp.logical_not(used) & (j == nj - 1))
    def _():
        o_ref[...] = jnp.zeros_like(o_ref)


def _moe_grouped(tile_expert, n_used, xs, wg, wu, wd, y_prev, tile0, n_rows):
    p, dp = xs.shape
    d = 2 * dp
    f = wg.shape[2]
    tm = MOE_TM
    tf = MOE_TF
    nj = f // tf

    def jj(t, j, te, nu):
        return jnp.where(t + tile0 < nu[0], j, nj - 1)

    in_specs = [pl.BlockSpec((tm, dp), lambda t, j, te, nu: (t, 0)),
                pl.BlockSpec((None, d, tf), lambda t, j, te, nu: (te[t + tile0], 0, jj(t, j, te, nu))),
                pl.BlockSpec((None, d, tf), lambda t, j, te, nu: (te[t + tile0], 0, jj(t, j, te, nu))),
                pl.BlockSpec((None, tf, d), lambda t, j, te, nu: (te[t + tile0], jj(t, j, te, nu), 0))]
    args = [tile_expert, n_used, xs, wg, wu, wd]
    aliases = {}
    if y_prev is not None:
        in_specs.append(pl.BlockSpec(memory_space=pl.ANY))
        args.append(y_prev)
        aliases = {len(args) - 1: 0}
    return pl.pallas_call(
        functools.partial(_moe_kernel, tile0=tile0, nj=nj),
        out_shape=jax.ShapeDtypeStruct((n_rows, dp), jnp.int32),
        grid_spec=pltpu.PrefetchScalarGridSpec(
            num_scalar_prefetch=2,
            grid=(p // tm, nj),
            in_specs=in_specs,
            out_specs=pl.BlockSpec((tm, dp), lambda t, j, te, nu: (t + tile0, 0)),
            scratch_shapes=[pltpu.VMEM((tm, d), BF16), pltpu.VMEM((tm, d), F32)]),
        input_output_aliases=aliases,
        compiler_params=_params("arbitrary", "arbitrary"),
        name="moe_grouped",
    )(*args)


def _combine_kernel(y1_ref, y2_ref, x_ref, g2_ref, rt_ref, o_ref):
    rt = rt_ref[...]
    w1, w2 = rt[:, 2:3], rt[:, 3:4]
    half = y1_ref.shape[1]
    for k, (a, c) in enumerate(zip(_unpack_bf16_pairs(y1_ref[...]), _unpack_bf16_pairs(y2_ref[...]))):
        cs = slice(k * half, (k + 1) * half)
        o_ref[:, cs] = x_ref[:, cs] + g2_ref[:, cs] * (w1 * a + w2 * c)


def _moe_combine(yg, x, g2, route):
    b, s, d = x.shape
    tm = _pick_tile(s, 512)
    tok = lambda n: pl.BlockSpec((None, tm, n), lambda bi, i: (bi, i, 0))
    return pl.pallas_call(
        _combine_kernel,
        out_shape=jax.ShapeDtypeStruct((b, s, d), F32),
        grid=(b, s // tm),
        in_specs=[tok(d // 2), pl.BlockSpec((None, tm, d // 2), lambda bi, i: (b + bi, i, 0)), tok(d),
                  pl.BlockSpec((None, 1, d), lambda bi, i: (bi, 0, 0)), tok(LANES)],
        out_specs=tok(d),
        compiler_params=_params("parallel", "parallel"),
        name="moe_combine",
    )(yg, yg, x, g2, route)


def _moe(h, route, x, g2, wg, wu, wd):
    b, s, d = x.shape
    n_tok = b * s
    tm = MOE_TM
    rt = route.reshape(n_tok, LANES)
    flat_e = jnp.concatenate([rt[:, 0], rt[:, 1]]).astype(jnp.int32)
    onehot = (flat_e[:, None] == jnp.arange(N_EXPERTS, dtype=jnp.int32)[None, :]).astype(jnp.int32)
    csum = jnp.cumsum(onehot, axis=0)
    rank = jnp.sum(csum * onehot, axis=1) - 1
    counts = csum[-1]
    tiles_e = (counts + tm - 1) // tm
    tiles_cum = jnp.cumsum(tiles_e)
    row_start = (tiles_cum - tiles_e) * tm
    dest = jnp.sum(onehot * row_start[None, :], axis=1) + rank
    n_tiles = 2 * n_tok // tm + N_EXPERTS
    tile_ids = jnp.arange(n_tiles, dtype=jnp.int32)
    tile_expert = jnp.sum((tile_ids[:, None] >= tiles_cum[None, :]).astype(jnp.int32), axis=1)
    last_e = jnp.max(jnp.where(tiles_e > 0, jnp.arange(N_EXPERTS, dtype=jnp.int32), 0))
    tile_expert = jnp.minimum(tile_expert, last_e).astype(jnp.int32)
    n_used = tiles_cum[-1:].astype(jnp.int32)
    dest = dest.astype(jnp.int32)
    row_src = _invert_rows(dest, n_tiles * tm)
    n_chunks = max(c for c in range(1, MOE_CHUNKS + 1) if n_tiles % c == 0)
    tiles_c = n_tiles // n_chunks
    h_flat = h.reshape(n_tok, d // 2)
    xs = [_gather_rows(row_src[c * tiles_c * tm:(c + 1) * tiles_c * tm], h_flat) for c in range(n_chunks)]
    y = None
    for c in range(n_chunks):
        y = _moe_grouped(tile_expert, n_used, xs[c], wg, wu, wd, y, c * tiles_c, n_tiles * tm)
    yg = _gather_rows(dest, y).reshape(2 * b, s, d // 2)
    return _moe_combine(yg, x, g2, route)


def _rope_tables(n):
    rows = n // GRID_W
    r = np.repeat(np.arange(rows), GRID_W).astype(np.float64)
    col = np.tile(np.arange(GRID_W), rows).astype(np.float64)
    freqs = ROPE_BASE ** (-np.arange(ROPE_FREQS, dtype=np.float64) / ROPE_FREQS)
    ang = np.stack([r[:, None] * freqs, col[:, None] * freqs], axis=1)
    ang = np.repeat(ang[:, :, None, :], 2, axis=2).reshape(n, HEAD_DIM)
    ang = np.tile(ang, (1, LANES // HEAD_DIM))
    cos, sin = np.cos(ang), np.sin(ang)
    first_half = (np.arange(LANES) % (2 * ROPE_FREQS)) < ROPE_FREQS
    to_dev = lambda t: jnp.asarray(t.astype(np.float32))
    return to_dev(cos), to_dev(np.where(first_half, -sin, 0.0)), to_dev(np.where(first_half, 0.0, sin))


def _lane_rows(lg):
    return jnp.repeat(lg.astype(F32), HEAD_DIM).reshape(RET_HEADS // 2, 1, LANES)


def kernel(x, c, ctx, c_ctx, ada_w, ada_b, norm1_w, norm2_w, w_in, w_out, q_norm_w, k_norm_w,
           attn_sink, ret_decay_f, ret_decay_b, ret_gn_w, conv_w, conv_b, conv_ln_w, conv_ln_b,
           ffn_w_gate, ffn_w_up, ffn_w_down, router_w, moe_w_gate, moe_w_up, moe_w_down):
    b, n, d = x.shape
    n_ctx = ctx.shape[1]
    depth = ada_w.shape[0]
    cond = jnp.zeros((SUBLANES, d), F32).at[0:b].set(c).at[b].set(c_ctx)
    mods = _adaln(cond, ada_w, ada_b).reshape(depth, SUBLANES, 6, d)
    cos, sa, sb = _rope_tables(n)
    ones_c = jnp.ones((n_ctx, LANES), F32)
    zeros_c = jnp.zeros((n_ctx, LANES), F32)
    zero_state = jnp.zeros((b, RET_HEADS // 2, LANES, LANES), F32)
    row = lambda v: v.reshape(1, -1)
    slab_rows = b * (n // ATT_BLOCK // ATT_QBLOCKS) * BF16_SUBLANES
    riders = {l: {} for l in range(depth)}

    def ride(l, key, w):
        w2 = w.reshape(-1, w.shape[-1])
        if 0 <= l < depth and (n // ATT_BLOCK) % ATT_QBLOCKS == 0 and w2.shape[0] % slab_rows == 0:
            riders[l][key] = w2

    for j, w in enumerate((moe_w_gate, moe_w_up, moe_w_down)):
        ride(j * min(depth, 2) // 3, ("moe", j), w)
    for l in range(depth):
        if l % 2 == 0:
            ride(l, ("ffn", l, 0), ffn_w_gate[l // 2])
            ride(l, ("ffn", l, 1), ffn_w_up[l // 2])
            ride(l, ("ffn", l, 2), ffn_w_down[l // 2])
        ride(l - 1, ("w_in", l), w_in[l])
        ride(l - 1, ("w_out", l), w_out[l])
    rounded = {}

    def bf16_of(key, w):
        return rounded[key].reshape(w.shape) if key in rounded else w.astype(BF16)

    for l in range(depth):
        last = l == depth - 1
        m_lat = [mods[l, 0:b, k][:, None, :] for k in range(6)]
        m_ctx = [jnp.broadcast_to(mods[l, b, k][None, None, :], (b, 1, d)) for k in range(6)]
        w_in_bf = bf16_of(("w_in", l), w_in[l])
        w_out_bf = bf16_of(("w_out", l), w_out[l])
        qw = row(jnp.tile(q_norm_w[l], LANES // HEAD_DIM))
        kw = row(jnp.tile(k_norm_w[l], LANES // HEAD_DIM))
        lgf = _lane_rows(jax.nn.log_sigmoid(ret_decay_f[l].astype(F32)))
        lgb = _lane_rows(jax.nn.log_sigmoid(ret_decay_b[l].astype(F32)))
        sink_tab = jnp.repeat(attn_sink[l].astype(F32), ATT_BLOCK).reshape(ATT_Q_HEADS // 2, 2 * ATT_BLOCK)

        q, k, v, rk, rv, rq, rg, cv = _inproj(x, row(norm1_w[l]), m_lat[1], m_lat[0], w_in_bf,
                                               cos, sa, sb, qw, kw)
        qc, kc, vc, rkc, rvc, rqc, rgc, cvc = _inproj(ctx, row(norm1_w[l]), m_ctx[1], m_ctx[0], w_in_bf,
                                                       ones_c, zeros_c, zeros_c, qw, kw)
        rf_c, s_f = _ret_states(lgf, rkc, rvc, zero_state, reverse=False)
        rb_c, s_b = _ret_states(lgb, rkc, rvc, zero_state, reverse=True)
        rf, _ = _ret_states(lgf, rk, rv, s_f, reverse=False)
        rb, _ = _ret_states(lgb, rk, rv, s_b, reverse=True)

        att, cast_out = _attention(q, k, v, kc, vc, sink_tab, window=True, cast=list(riders[l].values()))
        rounded.update(zip(riders[l].keys(), cast_out))
        ret = _ret_out(lgf, lgb, rq, rk, rv, rg, rf, rb, row(ret_gn_w[l]))
        cnv = _conv(cv, conv_w[l], row(conv_b[l]), row(conv_ln_w[l]), row(conv_ln_b[l]))

        if l % 2 == 0:
            i = l // 2
            wg, wu, wd = (bf16_of(("ffn", l, j), w[i]) for j, w in enumerate((ffn_w_gate, ffn_w_up, ffn_w_down)))
            x_new = _mixer_ffn(att, ret, cnv, w_out_bf, x, m_lat[2], row(norm2_w[l]), m_lat[4], m_lat[3],
                               wg, wu, wd, m_lat[5])
        else:
            i = l // 2
            wg, wu, wd = (bf16_of(("moe", j), w)[i] for j, w in enumerate((moe_w_gate, moe_w_up, moe_w_down)))
            x_mid, h2, route = _outproj_route(att, ret, cnv, w_out_bf, x, m_lat[2], row(norm2_w[l]),
                                              m_lat[4], m_lat[3], router_w[i])
            x_new = _moe(h2, route, x_mid, m_lat[5], wg, wu, wd)

        if not last:
            att_c, _ = _attention(qc, None, None, kc, vc, sink_tab, window=False)
            ret_c = _ret_out(lgf, lgb, rqc, rkc, rvc, rgc, rf_c, rb_c, row(ret_gn_w[l]))
            cnv_c = _conv(cvc, conv_w[l], row(conv_b[l]), row(conv_ln_w[l]), row(conv_ln_b[l]))
            if l % 2 == 0:
                ctx = _mixer_ffn(att_c, ret_c, cnv_c, w_out_bf, ctx, m_ctx[2], row(norm2_w[l]), m_ctx[4],
                                 m_ctx[3], wg, wu, wd, m_ctx[5])
            else:
                c_mid, h2c, route_c = _outproj_route(att_c, ret_c, cnv_c, w_out_bf, ctx, m_ctx[2],
                                                     row(norm2_w[l]), m_ctx[4], m_ctx[3], router_w[i])
                ctx = _moe(h2c, route_c, c_mid, m_ctx[5], wg, wu, wd)
        x = x_new
    return x
```

```python
import functools

import jax
import jax.numpy as jnp
import numpy as np
from jax import lax
from jax.experimental import pallas as pl
from jax.experimental.pallas import tpu as pltpu
from jax.experimental.pallas import tpu_sc as plsc

F32 = jnp.float32
BF16 = jnp.bfloat16

GRID_W = 64
HEAD_DIM = 64
ATT_Q_HEADS = 8
ATT_KV_HEADS = 2
ATT_WINDOW = 128
ATT_BLOCK = 128
RET_HEADS = 4
RET_CHUNK = 128
RET_K_SCALE = HEAD_DIM ** -0.5
ATT_SCALE = HEAD_DIM ** -0.5
CONV_CH = 256
CONV_WIDTH = 31
ROPE_BASE = 10000.0
ROPE_FREQS = HEAD_DIM // 4
D_ATT = ATT_Q_HEADS * HEAD_DIM
D_RET = RET_HEADS * HEAD_DIM
ATT_KV_W = ATT_KV_HEADS * HEAD_DIM
C_ATT_K = 0
C_ATT_V = C_ATT_K + ATT_KV_W
C_RET_K = C_ATT_V + ATT_KV_W
C_RET_V = C_RET_K + D_RET
C_ATT_Q = C_RET_V + D_RET
C_RET_Q = C_ATT_Q + D_ATT
C_RET_G = C_RET_Q + D_RET
C_CONV = C_RET_G + D_RET
N_EXPERTS = 8
EPS = 1e-6
NEG_INF = -1e30

LANES = 128
SUBLANES = 8
BF16_SUBLANES = 16
VMEM_LIMIT = 48 * 1024 * 1024
CONV_HALO = 16
CONV_SUB = 32
CONV_CHAINS = 4
ATT_QBLOCKS = 4
RET_UNROLL = 4
MOE_TM = 512
MOE_CHUNKS = 2
MOE_SUB = 512
FFN_SUB = 512
MOE_TF = 1792
SC_LANES = 16
SC_WINDOW = 64


def _params(*sem):
    return pltpu.CompilerParams(dimension_semantics=sem, vmem_limit_bytes=VMEM_LIMIT)


def _sigmoid(x):
    return 1.0 / (1.0 + jnp.exp(-x))


def _pack_bf16_pairs(v):
    c = v.shape[1] // 2
    bits = pltpu.bitcast(v.astype(BF16).astype(F32), jnp.uint32)
    packed = (bits[:, c:] & jnp.uint32(0xFFFF0000)) | (bits[:, :c] >> 16)
    return pltpu.bitcast(packed, jnp.int32)


def _unpack_bf16_pairs(p):
    bits = pltpu.bitcast(p, jnp.uint32)
    return pltpu.bitcast(bits << 16, F32), pltpu.bitcast(bits & jnp.uint32(0xFFFF0000), F32)


def _pick_tile(n, pref):
    t = min(n, pref)
    assert n % t == 0, (n, t)
    return t


def _adaln_kernel(c_ref, w_ref, b_ref, o_ref):
    c = c_ref[...]
    s = c * _sigmoid(c)
    o_ref[...] = jnp.dot(s, w_ref[...], preferred_element_type=F32,
                         precision=lax.Precision.HIGHEST) + b_ref[...]


def _adaln(cond, ada_w, ada_b):
    depth, d, n = ada_w.shape
    tn = _pick_tile(n, 1536)
    return pl.pallas_call(
        _adaln_kernel,
        out_shape=jax.ShapeDtypeStruct((depth, cond.shape[0], n), F32),
        grid=(depth, n // tn),
        in_specs=[pl.BlockSpec(cond.shape, lambda l, j: (0, 0)),
                  pl.BlockSpec((None, d, tn), lambda l, j: (l, 0, j)),
                  pl.BlockSpec((None, 1, tn), lambda l, j: (l, 0, j))],
        out_specs=pl.BlockSpec((None, cond.shape[0], tn), lambda l, j: (l, 0, j)),
        compiler_params=_params("parallel", "parallel"),
        name="adaln",
    )(cond, ada_w, ada_b.reshape(depth, 1, n))


def _modulated_rms(x, nw, sc, sh):
    ms = jnp.mean(x * x, axis=-1, keepdims=True)
    return (x * lax.rsqrt(ms + EPS) * nw) * (1.0 + sc) + sh


def _head_group_matrix():
    r = lax.broadcasted_iota(jnp.int32, (2 * LANES, 2 * LANES), 0) // HEAD_DIM
    c = lax.broadcasted_iota(jnp.int32, (2 * LANES, 2 * LANES), 1) // HEAD_DIM
    return jnp.where(r == c, 1.0, 0.0).astype(BF16)


def _per_head_sum(v, grp):
    hi = v.astype(BF16)
    lo = (v - hi.astype(F32)).astype(BF16)
    r = jnp.dot(jnp.concatenate([hi, lo], axis=1), grp, preferred_element_type=F32)
    return r[:, 0:LANES] + r[:, LANES:]


def _dup_halves(t):
    sw = pltpu.roll(t, HEAD_DIM, 1)
    lo = lax.broadcasted_iota(jnp.int32, t.shape, 1) < HEAD_DIM
    return jnp.where(lo, t, sw), jnp.where(lo, sw, t)


def _inproj_kernel(x_ref, nw_ref, sc_ref, sh_ref, w_ref, cos_ref, sa_ref, sb_ref, qw_ref, kw_ref,
                   q_ref, k_ref, v_ref, rk_ref, rv_ref, rq_ref, rg_ref, cv_ref):
    hb = _modulated_rms(x_ref[...], nw_ref[...], sc_ref[...], sh_ref[...]).astype(BF16)

    def proj(c0, n):
        return jnp.dot(hb, w_ref[:, c0:c0 + n], preferred_element_type=F32)

    grp = _head_group_matrix()
    cos, sa, sb = cos_ref[...], sa_ref[...], sb_ref[...]

    def norm_rope(p, wrow):
        y = p * lax.rsqrt(_per_head_sum(p * p, grp) * (1.0 / HEAD_DIM) + EPS) * wrow
        return (y * cos + pltpu.roll(y, LANES - ROPE_FREQS, 1) * sa
                + pltpu.roll(y, ROPE_FREQS, 1) * sb)

    def put_kv(kv):
        k0, k1 = _dup_halves(norm_rope(kv[:, 0:ATT_KV_W], kw_ref[...]))
        k_ref[:, 0:LANES] = k0.astype(BF16)
        k_ref[:, LANES:2 * LANES] = k1.astype(BF16)
        v_ref[...] = kv[:, ATT_KV_W:].T.astype(BF16)

    def put_q(qall):
        qw = qw_ref[...] * ATT_SCALE
        for j in range(D_ATT // LANES):
            q_ref[:, j * LANES:(j + 1) * LANES] = norm_rope(qall[:, j * LANES:(j + 1) * LANES], qw).astype(BF16)

    def put_rk(p):
        rk_ref[...] = (p * RET_K_SCALE).astype(BF16)

    def put_rv(p):
        rv_ref[...] = p.astype(BF16)

    def put_rq(p):
        rq_ref[...] = p.astype(BF16)

    def put_rg(p):
        rg_ref[...] = p

    def put_conv(p):
        cv_ref[...] = p[:, 0:CONV_CH] * _sigmoid(p[:, CONV_CH:])

    units = [(C_ATT_K, 2 * ATT_KV_W, put_kv), (C_ATT_Q, D_ATT, put_q), (C_RET_K, D_RET, put_rk),
             (C_RET_V, D_RET, put_rv), (C_RET_Q, D_RET, put_rq), (C_RET_G, D_RET, put_rg),
             (C_CONV, 2 * CONV_CH, put_conv)]
    ahead = 2
    pending = [proj(c0, n) for c0, n, _ in units[:ahead]]
    for n, (_, _, put) in enumerate(units):
        if n + ahead < len(units):
            pending.append(proj(*units[n + ahead][:2]))
        put(pending.pop(0))


def _inproj(x, nw, sc, sh, w_bf, cos, sa, sb, qw, kw):
    b, s, d = x.shape
    tm = _pick_tile(s, 1024)
    row = lambda n: pl.BlockSpec((1, n), lambda bi, i: (0, 0))
    per_b = pl.BlockSpec((None, 1, d), lambda bi, i: (bi, 0, 0))
    tab = pl.BlockSpec((tm, LANES), lambda bi, i: (i, 0))
    tok = lambda n: pl.BlockSpec((None, tm, n), lambda bi, i: (bi, i, 0))
    shp = lambda n, dt: jax.ShapeDtypeStruct((b, s, n), dt)
    return pl.pallas_call(
        _inproj_kernel,
        out_shape=(shp(D_ATT, BF16), shp(2 * LANES, BF16), jax.ShapeDtypeStruct((b, ATT_KV_W, s), BF16),
                   shp(D_RET, BF16), shp(D_RET, BF16), shp(D_RET, BF16), shp(D_RET, F32),
                   shp(CONV_CH, F32)),
        grid=(b, s // tm),
        in_specs=[tok(d), row(d), per_b, per_b,
                  pl.BlockSpec(w_bf.shape, lambda bi, i: (0, 0)),
                  tab, tab, tab, row(LANES), row(LANES)],
        out_specs=(tok(D_ATT), tok(2 * LANES), pl.BlockSpec((None, ATT_KV_W, tm), lambda bi, i: (bi, 0, i)),
                   tok(D_RET), tok(D_RET), tok(D_RET), tok(D_RET), tok(CONV_CH)),
        compiler_params=_params("parallel", "parallel"),
        name="inproj",
    )(x, nw, sc, sh, w_bf, cos, sa, sb, qw, kw)


def _attn_kernel(*refs, window, nq, n_cast):
    blk = ATT_BLOCK
    n_in = len(refs) - 1 - n_cast
    for src, dst in zip(refs[n_in - n_cast:n_in], refs[n_in + 1:]):
        dst[...] = src[...].astype(BF16)
    refs = refs[:n_in - n_cast] + refs[n_in:n_in + 1]
    if window:
        q_ref = refs[0]
        k_refs = refs[1:nq + 3]
        v_refs = refs[nq + 3:2 * nq + 5]
        kx_ref, vx_ref, sink_ref, o_ref = refs[2 * nq + 5:]
    else:
        q_ref, kx_ref, vx_ref, sink_ref, o_ref = refs
    n_ctx = kx_ref.shape[0]
    nk = 3 * blk + n_ctx if window else n_ctx
    if window:
        i = pl.program_id(1)
        last = nq * pl.num_programs(1) - 1
        key = lax.broadcasted_iota(jnp.int32, (blk, 2 * blk), 0)
        qry = lax.broadcasted_iota(jnp.int32, (blk, 2 * blk), 1) & (blk - 1)

        def band_masks(sub):
            off_prev = jnp.where(nq * i + sub > 0, 0, blk)
            off_next = jnp.where(nq * i + sub < last, 0, blk)
            return key >= qry + off_prev, key + off_next <= qry
    first_head = lax.broadcasted_iota(jnp.int32, (blk, LANES), 1) < HEAD_DIM
    ones = jnp.ones((2 * SUBLANES, nk), BF16)

    def scores(sub, pair):
        g = pair // 2
        gs = slice(g * LANES, (g + 1) * LANES)
        if window:
            kcat = jnp.concatenate([r[:, gs] for r in k_refs[sub:sub + 3]] + [kx_ref[:, gs]], axis=0)
        else:
            kcat = kx_ref[:, gs]
        qp = q_ref[sub * blk:(sub + 1) * blk, pair * LANES:(pair + 1) * LANES]
        zero = jnp.zeros_like(qp)
        w = jnp.concatenate([jnp.where(first_head, qp, zero), jnp.where(first_head, zero, qp)], axis=0)
        return lax.dot_general(kcat, w, (((1,), (1,)), ((), ())), preferred_element_type=F32)

    def masked_max(sub, pair, s):
        if window:
            mask_prev, mask_next = band_masks(sub)
            parts = [jnp.where(mask_prev, s[0:blk], NEG_INF), s[blk:2 * blk],
                     jnp.where(mask_next, s[2 * blk:3 * blk], NEG_INF), s[3 * blk:]]
        else:
            parts = [s]
        m = sink_ref[pair:pair + 1, :]
        for part in parts:
            m = jnp.maximum(m, jnp.max(part, axis=0, keepdims=True))
        return parts, m

    def softmax(sub, pair, parts_m):
        parts, m = parts_m
        p = jnp.concatenate([jnp.exp(part - m).astype(BF16) for part in parts], axis=0)
        return p, jnp.exp(sink_ref[pair:pair + 1, :] - m)

    def output(sub, pair, p, sink_p):
        g = pair // 2
        vs = slice(g * HEAD_DIM, (g + 1) * HEAD_DIM)
        if window:
            vt = jnp.concatenate([r[vs, :] for r in v_refs[sub:sub + 3]] + [vx_ref[vs, :]], axis=1)
        else:
            vt = vx_ref[vs, :]
        vaug = jnp.concatenate([vt, ones], axis=0)
        o = jnp.dot(vaug, p, preferred_element_type=F32)
        on = o[0:HEAD_DIM, :] * (1.0 / (o[HEAD_DIM:HEAD_DIM + 1, :] + sink_p))
        ot = jnp.concatenate([on[:, 0:blk], on[:, blk:2 * blk]], axis=0)
        o_ref[sub * blk:(sub + 1) * blk, pair * LANES:(pair + 1) * LANES] = ot.T.astype(BF16)

    units = [(sub, pair) for sub in range(nq) for pair in range(ATT_Q_HEADS // 2)]
    stages = (lambda sub, pair, _: scores(sub, pair), masked_max, softmax,
              lambda sub, pair, ps: output(sub, pair, *ps))
    vals = {}
    for step in range(len(units) + len(stages) - 1):
        for k, stage in enumerate(stages):
            u = step - k
            if 0 <= u < len(units):
                vals[u, k] = stage(*units[u], vals.pop((u, k - 1), None))


def _attention(q, k, vt, kx, vxt, sink_tab, window, cast=()):
    b, s, _ = q.shape
    blk = ATT_BLOCK
    assert ATT_WINDOW == blk, "the band masks assume a window of exactly one key block on each side"
    nb = s // blk
    nq = ATT_QBLOCKS if nb % ATT_QBLOCKS == 0 else 1
    n_ctx = kx.shape[1]
    qspec = pl.BlockSpec((None, nq * blk, D_ATT), lambda bi, i: (bi, i, 0))
    kctx = pl.BlockSpec((None, n_ctx, 2 * LANES), lambda bi, i: (bi, 0, 0))
    vctx = pl.BlockSpec((None, ATT_KV_W, n_ctx), lambda bi, i: (bi, 0, 0))
    snk = pl.BlockSpec(sink_tab.shape, lambda bi, i: (0, 0))
    if window:
        at = lambda off: (lambda i: jnp.clip(nq * i + off, 0, nb - 1))
        kspec = lambda f: pl.BlockSpec((None, blk, 2 * LANES), lambda bi, i: (bi, f(i), 0))
        vspec = lambda f: pl.BlockSpec((None, ATT_KV_W, blk), lambda bi, i: (bi, 0, f(i)))
        offs = range(-1, nq + 1)
        in_specs = ([qspec] + [kspec(at(o)) for o in offs] + [vspec(at(o)) for o in offs]
                    + [kctx, vctx, snk])
        args = (q,) + (k,) * (nq + 2) + (vt,) * (nq + 2) + (kx, vxt, sink_tab)
    else:
        in_specs = [qspec, kctx, vctx, snk]
        args = (q, kx, vxt, sink_tab)
    steps = nb // nq
    out_shape, out_specs = [jax.ShapeDtypeStruct((b, s, D_ATT), BF16)], [qspec]
    for w in cast:
        rows = w.shape[0] // (b * steps)
        assert rows * b * steps == w.shape[0] and rows % (2 * SUBLANES) == 0, w.shape
        slab = pl.BlockSpec((rows, w.shape[1]), lambda bi, i: (bi * steps + i, 0))
        in_specs.append(slab)
        out_specs.append(slab)
        out_shape.append(jax.ShapeDtypeStruct(w.shape, BF16))
    res = pl.pallas_call(
        functools.partial(_attn_kernel, window=window, nq=nq, n_cast=len(cast)),
        out_shape=tuple(out_shape),
        grid=(b, steps),
        in_specs=in_specs,
        out_specs=tuple(out_specs),
        compiler_params=_params("parallel", "parallel"),
        name="attention_window" if window else "attention_ctx",
    )(*args, *cast)
    return res[0], tuple(res[1:])


def _ret_state_kernel(lgl_ref, k_ref, v_ref, s0_ref, r_ref, fin_ref, s_scr, *, cb, reverse):
    c = pl.program_id(1)
    L = RET_CHUNK

    @pl.when(c == 0)
    def _():
        s_scr[...] = s0_ref[...]

    jj = lax.broadcasted_iota(jnp.int32, (L, LANES), 0).astype(F32)
    expo = jj if reverse else (L - 1.0) - jj
    same_head = (lax.broadcasted_iota(jnp.int32, (LANES, LANES), 0) // HEAD_DIM
                 == lax.broadcasted_iota(jnp.int32, (LANES, LANES), 1) // HEAD_DIM)
    for pr in range(RET_HEADS // 2):
        cs = slice(pr * LANES, (pr + 1) * LANES)
        lgl = lgl_ref[pr]
        kdec = jnp.exp(expo * lgl)
        cdec = jnp.exp(float(L) * lgl)
        state = s_scr[pr]
        for t in range(cb):
            cc = cb - 1 - t if reverse else t
            rows = slice(cc * L, (cc + 1) * L)
            r_ref[cc, pr] = state.astype(BF16)
            kd = k_ref[rows, cs].astype(F32) * kdec
            u = jnp.dot(kd.T.astype(BF16), v_ref[rows, cs], preferred_element_type=F32)
            state = cdec * state + jnp.where(same_head, u, 0.0)
        s_scr[pr] = state

    @pl.when(c == pl.num_programs(1) - 1)
    def _():
        fin_ref[...] = s_scr[...]


def _ret_states(lgl, rk, rv, s0, reverse):
    b, s, _ = rk.shape
    nc = s // RET_CHUNK
    cb = _pick_tile(nc, 8)
    nblk = nc // cb
    npair = RET_HEADS // 2
    blk_idx = (lambda c: nblk - 1 - c) if reverse else (lambda c: c)
    tok = pl.BlockSpec((None, cb * RET_CHUNK, D_RET), lambda bi, c: (bi, blk_idx(c), 0))
    st = pl.BlockSpec((None, npair, LANES, LANES), lambda bi, c: (bi, 0, 0, 0))
    return pl.pallas_call(
        functools.partial(_ret_state_kernel, cb=cb, reverse=reverse),
        out_shape=(jax.ShapeDtypeStruct((b, nc, npair, LANES, LANES), BF16),
                   jax.ShapeDtypeStruct((b, npair, LANES, LANES), F32)),
        grid=(b, nblk),
        in_specs=[pl.BlockSpec(lgl.shape, lambda bi, c: (0, 0, 0)), tok, tok, st],
        out_specs=(pl.BlockSpec((None, cb, npair, LANES, LANES),
                                lambda bi, c: (bi, blk_idx(c), 0, 0, 0)), st),
        scratch_shapes=[pltpu.VMEM((npair, LANES, LANES), F32)],
        compiler_params=_params("parallel", "arbitrary"),
        name="ret_state_bwd" if reverse else "ret_state_fwd",
    )(lgl, rk, rv, s0)


def _ret_out_kernel(lgf_ref, lgb_ref, q_ref, k_ref, v_ref, g_ref, rf_ref, rb_ref, gnw_ref, o_ref, *, cb):
    L = RET_CHUNK
    i0 = lax.broadcasted_iota(jnp.int32, (L, LANES), 0).astype(F32)
    i1 = lax.broadcasted_iota(jnp.int32, (L, LANES), 1).astype(F32)
    diff = i0 - i1
    lo = lax.broadcasted_iota(jnp.int32, (L, LANES), 1) < HEAD_DIM
    grp = _head_group_matrix()
    inv = 1.0 / HEAD_DIM
    tabs = []
    for pr in range(RET_HEADS // 2):
        lgf, lgb = lgf_ref[pr], lgb_ref[pr]
        dmat = [jnp.where(diff >= 0.0,
                          jnp.exp(jnp.maximum(diff, 0.0) * lgf[:, a:a + 1]),
                          jnp.exp(jnp.maximum(-diff, 0.0) * lgb[:, a:a + 1]))
                for a in (0, HEAD_DIM)]
        tabs.append((jnp.exp((i0 + 1.0) * lgf), jnp.exp((float(L) - i0) * lgb),
                     jnp.concatenate(dmat, axis=0), gnw_ref[:, pr * LANES:(pr + 1) * LANES]))

    def decayed_scores(t, pr):
        rows = pl.ds(pl.multiple_of(t * L, L), L)
        cs = slice(pr * LANES, (pr + 1) * LANES)
        qp, kp = q_ref[rows, cs], k_ref[rows, cs]
        zero = jnp.zeros_like(kp)
        qz = jnp.concatenate([jnp.where(lo, qp, zero), jnp.where(lo, zero, qp)], axis=0)
        return lax.dot_general(qz, kp, (((1,), (1,)), ((), ())), preferred_element_type=F32) * tabs[pr][2]

    def mix(t, pr, a):
        rows = pl.ds(pl.multiple_of(t * L, L), L)
        cs = slice(pr * LANES, (pr + 1) * LANES)
        xif, xib, _, _ = tabs[pr]
        vp = v_ref[rows, cs]
        zero = jnp.zeros_like(vp)
        qf = q_ref[rows, cs].astype(F32)
        lhs = jnp.concatenate([a[0:L].astype(BF16), a[L:].astype(BF16),
                               (qf * xif).astype(BF16), (qf * xib).astype(BF16)], axis=1)
        rhs = jnp.concatenate([jnp.where(lo, vp, zero), jnp.where(lo, zero, vp),
                               rf_ref[t, pr], rb_ref[t, pr]], axis=0)
        return jnp.dot(lhs, rhs, preferred_element_type=F32)

    def centre(t, pr, acc):
        return acc - _per_head_sum(acc, grp) * inv

    def readout(t, pr, dl):
        rows = pl.ds(pl.multiple_of(t * L, L), L)
        cs = slice(pr * LANES, (pr + 1) * LANES)
        var = _per_head_sum(dl * dl, grp) * inv
        y = dl * lax.rsqrt(var + EPS) * tabs[pr][3]
        gt = g_ref[rows, cs]
        o_ref[rows, cs] = (gt * _sigmoid(gt) * y).astype(BF16)

    unroll = min(RET_UNROLL, cb)
    stages = (lambda t, pr, _: decayed_scores(t, pr), mix, centre, readout)

    def group(gi, carry):
        units = [(gi * unroll + u, pr) for u in range(unroll) for pr in range(RET_HEADS // 2)]
        vals = {}
        for step in range(len(units) + len(stages) - 1):
            for k, stage in enumerate(stages):
                u = step - k
                if 0 <= u < len(units):
                    vals[u, k] = stage(*units[u], vals.pop((u, k - 1), None))
        return carry

    lax.fori_loop(0, cb // unroll, group, 0)


def _ret_out(lgf, lgb, rq, rk, rv, rg, rf, rb, gnw):
    b, s, _ = rq.shape
    nc = s // RET_CHUNK
    cb = _pick_tile(nc, 8)
    npair = RET_HEADS // 2
    tok = pl.BlockSpec((None, cb * RET_CHUNK, D_RET), lambda bi, c: (bi, c, 0))
    st = pl.BlockSpec((None, cb, npair, LANES, LANES), lambda bi, c: (bi, c, 0, 0, 0))
    lg = pl.BlockSpec(lgf.shape, lambda bi, c: (0, 0, 0))
    return pl.pallas_call(
        functools.partial(_ret_out_kernel, cb=cb),
        out_shape=jax.ShapeDtypeStruct((b, s, D_RET), BF16),
        grid=(b, nc // cb),
        in_specs=[lg, lg, tok, tok, tok, tok, st, st, pl.BlockSpec((1, D_RET), lambda bi, c: (0, 0))],
        out_specs=tok,
        compiler_params=_params("parallel", "parallel"),
        name="ret_out",
    )(lgf, lgb, rq, rk, rv, rg, rf, rb, gnw)


def _conv_kernel(prev_ref, cur_ref, next_ref, w_ref, b_ref, lnw_ref, lnb_ref, o_ref,
                 xpad, shifted, hbuf, wtile):
    i = pl.program_id(1)
    tm = cur_ref.shape[0]
    halo = CONV_HALO
    first = i == 0
    last = i == pl.num_programs(1) - 1
    xpad[0:halo, :] = jnp.where(first, 0.0, prev_ref[...])
    xpad[halo:halo + tm, :] = cur_ref[...]
    xpad[halo + tm:2 * halo + tm, :] = jnp.where(last, 0.0, next_ref[...])
    span = tm + 2 * halo - SUBLANES
    for r in range(SUBLANES):
        shifted[r] = xpad[r:r + span, :]
    base = halo - CONV_WIDTH // 2
    bias = b_ref[...]
    ch = cur_ref.shape[1]
    for w in range(CONV_WIDTH):
        wtile[w] = jnp.broadcast_to(w_ref[w:w + 1, :], (SUBLANES, ch))

    taps_by_shift = {}
    for w in range(CONV_WIDTH):
        taps_by_shift.setdefault((base + w) % SUBLANES, []).append(((base + w) // SUBLANES, w))

    def sub(sb, carry):
        r0 = pl.multiple_of(sb * CONV_SUB, CONV_SUB)
        chains = [None] * CONV_CHAINS
        groups = CONV_SUB // SUBLANES
        for r, taps in sorted(taps_by_shift.items()):
            a_lo = min(a for a, _ in taps)
            n_g = max(a for a, _ in taps) - a_lo + groups
            win = shifted[r, pl.ds(r0 + a_lo * SUBLANES, n_g * SUBLANES), :].reshape(n_g, SUBLANES, ch)
            for a, w in taps:
                term = win[a - a_lo:a - a_lo + groups] * wtile[w]
                c = w % CONV_CHAINS
                chains[c] = term if chains[c] is None else chains[c] + term
        total = (chains[0] + chains[1]) + (chains[2] + chains[3])
        hbuf[pl.ds(r0, CONV_SUB), :] = total.reshape(CONV_SUB, ch) + bias
        return carry

    lax.fori_loop(0, tm // CONV_SUB, sub, 0)
    h = hbuf[...]
    mu = jnp.mean(h, axis=-1, keepdims=True)
    dl = h - mu
    var = jnp.mean(dl * dl, axis=-1, keepdims=True)
    y = dl * lax.rsqrt(var + EPS) * lnw_ref[...] + lnb_ref[...]
    o_ref[...] = (y * _sigmoid(y)).astype(BF16)


def _conv(cv, w, bias, lnw, lnb):
    b, s, ch = cv.shape
    tm = _pick_tile(s, 1024)
    hpb = tm // CONV_HALO
    nh = s // CONV_HALO
    row = pl.BlockSpec((1, ch), lambda bi, i: (0, 0))
    span = tm + 2 * CONV_HALO - SUBLANES
    return pl.pallas_call(
        _conv_kernel,
        out_shape=jax.ShapeDtypeStruct((b, s, ch), BF16),
        grid=(b, s // tm),
        in_specs=[pl.BlockSpec((None, CONV_HALO, ch), lambda bi, i: (bi, jnp.maximum(i * hpb - 1, 0), 0)),
                  pl.BlockSpec((None, tm, ch), lambda bi, i: (bi, i, 0)),
                  pl.BlockSpec((None, CONV_HALO, ch),
                               lambda bi, i: (bi, jnp.minimum((i + 1) * hpb, nh - 1), 0)),
                  pl.BlockSpec(w.shape, lambda bi, i: (0, 0)), row, row, row],
        out_specs=pl.BlockSpec((None, tm, ch), lambda bi, i: (bi, i, 0)),
        scratch_shapes=[pltpu.VMEM((tm + 2 * CONV_HALO, ch), F32),
                        pltpu.VMEM((SUBLANES, span, ch), F32),
                        pltpu.VMEM((tm, ch), F32),
                        pltpu.VMEM((CONV_WIDTH, SUBLANES, ch), F32)],
        compiler_params=_params("parallel", "parallel"),
        name="conv",
    )(cv, cv, cv, w, bias, lnw, lnb)


def _outproj_route_kernel(att_ref, ret_ref, cnv_ref, w_ref, x_ref, g1_ref, nw_ref, sc_ref, sh_ref, rcat_ref,
                          xo_ref, h_ref, rt_ref):
    y = (jnp.dot(att_ref[...], w_ref[0:D_ATT, :], preferred_element_type=F32)
         + jnp.dot(ret_ref[...], w_ref[D_ATT:D_ATT + D_RET, :], preferred_element_type=F32)
         + jnp.dot(cnv_ref[...], w_ref[D_ATT + D_RET:, :], preferred_element_type=F32))
    xn = x_ref[...] + g1_ref[...] * y
    xo_ref[...] = xn
    h = _modulated_rms(xn, nw_ref[...], sc_ref[...], sh_ref[...])
    h_ref[...] = _pack_bf16_pairs(h)
    hi = h.astype(BF16)
    lo = (h - hi.astype(F32)).astype(BF16)
    tm = h.shape[0]
    r = jnp.dot(jnp.concatenate([hi, lo], axis=0), rcat_ref[...], preferred_element_type=F32)
    logits = (r[0:tm, 0:LANES] + r[0:tm, LANES:]) + (r[tm:, 0:LANES] + r[tm:, LANES:])
    lane = lax.broadcasted_iota(jnp.int32, logits.shape, 1).astype(F32)
    logits = jnp.where(lane < N_EXPERTS, logits, NEG_INF)
    m1 = jnp.max(logits, axis=-1, keepdims=True)
    i1 = jnp.min(jnp.where(logits == m1, lane, float(LANES)), axis=-1, keepdims=True)
    rest = jnp.where(lane == i1, NEG_INF, logits)
    m2 = jnp.max(rest, axis=-1, keepdims=True)
    i2 = jnp.min(jnp.where(rest == m2, lane, float(LANES)), axis=-1, keepdims=True)
    e2 = jnp.exp(m2 - m1)
    w1 = 1.0 / (1.0 + e2)
    w2 = e2 / (1.0 + e2)
    rt_ref[...] = jnp.where(lane == 0.0, i1,
                            jnp.where(lane == 1.0, i2,
                                      jnp.where(lane == 2.0, w1, jnp.where(lane == 3.0, w2, 0.0))))


def _outproj_route(att, ret, cnv, w_bf, x, g1, nw, sc, sh, router):
    b, s, d = x.shape
    tm = _pick_tile(s, 512)
    tok = lambda n: pl.BlockSpec((None, tm, n), lambda bi, i: (bi, i, 0))
    per_b = pl.BlockSpec((None, 1, d), lambda bi, i: (bi, 0, 0))
    rpad = jnp.zeros((d, LANES), F32).at[:, :N_EXPERTS].set(router)
    rhi = rpad.astype(BF16)
    rlo = (rpad - rhi.astype(F32)).astype(BF16)
    return pl.pallas_call(
        _outproj_route_kernel,
        out_shape=(jax.ShapeDtypeStruct((b, s, d), F32), jax.ShapeDtypeStruct((b, s, d // 2), jnp.int32),
                   jax.ShapeDtypeStruct((b, s, LANES), F32)),
        grid=(b, s // tm),
        in_specs=[tok(D_ATT), tok(D_RET), tok(CONV_CH), pl.BlockSpec(w_bf.shape, lambda bi, i: (0, 0)),
                  tok(d), per_b, pl.BlockSpec((1, d), lambda bi, i: (0, 0)), per_b, per_b,
                  pl.BlockSpec((d, 2 * LANES), lambda bi, i: (0, 0))],
        out_specs=(tok(d), tok(d // 2), tok(LANES)),
        compiler_params=_params("parallel", "parallel"),
        name="outproj_route",
    )(att, ret, cnv, w_bf, x, g1, nw, sc, sh, jnp.concatenate([rhi, rlo], axis=1))


def _swiglu_chunks(h, wg_ref, wu_ref, wd_ref, sub):
    total = None
    for c0 in range(0, wg_ref.shape[1], sub):
        cs = slice(c0, min(c0 + sub, wg_ref.shape[1]))
        gate = jnp.dot(h, wg_ref[:, cs].astype(BF16), preferred_element_type=F32)
        up = jnp.dot(h, wu_ref[:, cs].astype(BF16), preferred_element_type=F32)
        act = (gate * _sigmoid(gate) * up).astype(BF16)
        part = jnp.dot(act, wd_ref[cs, :].astype(BF16), preferred_element_type=F32)
        total = part if total is None else total + part
    return total


def _mixer_ffn_kernel(att_ref, ret_ref, cnv_ref, wo_ref, x_ref, g1_ref, nw_ref, sc_ref, sh_ref,
                      wg_ref, wu_ref, wd_ref, g2_ref, o_ref):
    y = (jnp.dot(att_ref[...], wo_ref[0:D_ATT, :], preferred_element_type=F32)
         + jnp.dot(ret_ref[...], wo_ref[D_ATT:D_ATT + D_RET, :], preferred_element_type=F32)
         + jnp.dot(cnv_ref[...], wo_ref[D_ATT + D_RET:, :], preferred_element_type=F32))
    xn = x_ref[...] + g1_ref[...] * y
    h = _modulated_rms(xn, nw_ref[...], sc_ref[...], sh_ref[...]).astype(BF16)
    o_ref[...] = xn + g2_ref[...] * _swiglu_chunks(h, wg_ref, wu_ref, wd_ref, FFN_SUB)


def _mixer_ffn(att, ret, cnv, wo, x, g1, nw, sc, sh, wg, wu, wd, g2):
    b, s, d = x.shape
    f = wg.shape[1]
    tm = _pick_tile(s, 512)
    tok = lambda n: pl.BlockSpec((None, tm, n), lambda bi, i: (bi, i, 0))
    per_b = pl.BlockSpec((None, 1, d), lambda bi, i: (bi, 0, 0))
    res = lambda shape: pl.BlockSpec(shape, lambda bi, i: (0, 0), pipeline_mode=pl.Buffered(1))
    return pl.pallas_call(
        _mixer_ffn_kernel,
        out_shape=jax.ShapeDtypeStruct((b, s, d), F32),
        grid=(b, s // tm),
        in_specs=[tok(D_ATT), tok(D_RET), tok(CONV_CH), res(wo.shape), tok(d), per_b,
                  pl.BlockSpec((1, d), lambda bi, i: (0, 0)), per_b, per_b,
                  res((d, f)), res((d, f)), res((f, d)), per_b],
        out_specs=tok(d),
        compiler_params=_params("parallel", "parallel"),
        name="mixer_ffn_dense",
    )(att, ret, cnv, wo, x, g1, nw, sc, sh, wg, wu, wd, g2)


def _gather_rows(idx, src):
    n = idx.shape[0]
    d = src.shape[1]
    mesh = plsc.VectorSubcoreMesh(core_axis_name="core", subcore_axis_name="subcore")

    @pl.kernel(out_type=jax.ShapeDtypeStruct((n, d), src.dtype), mesh=mesh, name="moe_gather")
    def gather(src_hbm, idx_hbm, out_hbm):
        def body(idx_vmem, out_vmem):
            pltpu.sync_copy(src_hbm.at[idx_vmem.at[0, pl.ds(0, SC_WINDOW)]], out_vmem)

        pltpu.emit_pipeline(
            body,
            grid=(n // SC_WINDOW,),
            in_specs=[pl.BlockSpec((1, LANES), lambda i: (i, 0))],
            out_specs=[pl.BlockSpec((SC_WINDOW, d), lambda i: (i, 0))],
            core_axis_name=("core", "subcore"),
            dimension_semantics=(pltpu.PARALLEL,),
        )(idx_hbm, out_hbm)

    idx_rows = jnp.pad(idx.reshape(n // SC_WINDOW, SC_WINDOW), ((0, 0), (0, LANES - SC_WINDOW)))
    return gather(src, idx_rows)


def _invert_rows(dest, n_rows):
    n = dest.shape[0]
    n_tok = n // 2
    mesh = plsc.VectorSubcoreMesh(core_axis_name="core", subcore_axis_name="subcore")

    @pl.kernel(out_type=jax.ShapeDtypeStruct((n_rows,), jnp.int32), mesh=mesh, name="moe_row_src",
               scratch_types=[pltpu.VMEM((n,), jnp.int32), pltpu.VMEM((n_rows,), jnp.int32)],
               compiler_params=pltpu.CompilerParams(needs_layout_passes=False))
    def invert(dest_hbm, out_hbm, dest_vmem, rows_vmem):
        @pl.when((lax.axis_index("core") == 0) & (lax.axis_index("subcore") == 0))
        def _():
            pltpu.sync_copy(dest_hbm, dest_vmem)

            @pl.loop(0, n_rows, step=SC_LANES)
            def _(r):
                rows_vmem[pl.ds(r, SC_LANES)] = lax.rem(lax.iota(jnp.int32, SC_LANES) + r, n_tok)

            @pl.loop(0, n, step=SC_LANES)
            def _(a):
                tok = lax.iota(jnp.int32, SC_LANES) + a
                tok = jnp.where(tok >= n_tok, tok - n_tok, tok)
                plsc.store_scatter(rows_vmem, [dest_vmem[pl.ds(a, SC_LANES)]], tok)

            pltpu.sync_copy(rows_vmem, out_hbm)

    return invert(dest)


def _moe_kernel(te_ref, nu_ref, x_ref, wg_ref, wu_ref, wd_ref, *rest, tile0, nj):
    o_ref, xb, acc = rest[-3:]
    t = pl.program_id(0) + tile0
    j = pl.program_id(1)
    used = t < nu_ref[0]

    def partial_sum():
        return _swiglu_chunks(xb[...], wg_ref, wu_ref, wd_ref, MOE_SUB)

    @pl.when(used & (j == 0))
    def _():
        lo, hi = _unpack_bf16_pairs(x_ref[...])
        half = lo.shape[1]
        xb[:, 0:half] = lo.astype(BF16)
        xb[:, half:] = hi.astype(BF16)
        if nj == 1:
            o_ref[...] = _pack_bf16_pairs(partial_sum())
        else:
            acc[...] = partial_sum()

    if nj > 2:
        @pl.when(used & (j > 0) & (j < nj - 1))
        def _():
            acc[...] += partial_sum()

    if nj > 1:
        @pl.when(used & (j == nj - 1))
        def _():
            o_ref[...] = _pack_bf16_pairs(acc[...] + partial_sum())

    @pl.when(jnp.logical_not(used) & (j == nj - 1))
    def _():
        o_ref[...] = jnp.zeros_like(o_ref)


def _moe_grouped(tile_expert, n_used, xs, wg, wu, wd, y_prev, tile0, n_rows):
    p, dp = xs.shape
    d = 2 * dp
    f = wg.shape[2]
    tm = MOE_TM
    tf = MOE_TF
    nj = f // tf

    def jj(t, j, te, nu):
        return jnp.where(t + tile0 < nu[0], j, nj - 1)

    in_specs = [pl.BlockSpec((tm, dp), lambda t, j, te, nu: (t, 0)),
                pl.BlockSpec((None, d, tf), lambda t, j, te, nu: (te[t + tile0], 0, jj(t, j, te, nu))),
                pl.BlockSpec((None, d, tf), lambda t, j, te, nu: (te[t + tile0], 0, jj(t, j, te, nu))),
                pl.BlockSpec((None, tf, d), lambda t, j, te, nu: (te[t + tile0], jj(t, j, te, nu), 0))]
    args = [tile_expert, n_used, xs, wg, wu, wd]
    aliases = {}
    if y_prev is not None:
        in_specs.append(pl.BlockSpec(memory_space=pl.ANY))
        args.append(y_prev)
        aliases = {len(args) - 1: 0}
    return pl.pallas_call(
        functools.partial(_moe_kernel, tile0=tile0, nj=nj),
        out_shape=jax.ShapeDtypeStruct((n_rows, dp), jnp.int32),
        grid_spec=pltpu.PrefetchScalarGridSpec(
            num_scalar_prefetch=2,
            grid=(p // tm, nj),
            in_specs=in_specs,
            out_specs=pl.BlockSpec((tm, dp), lambda t, j, te, nu: (t + tile0, 0)),
            scratch_shapes=[pltpu.VMEM((tm, d), BF16), pltpu.VMEM((tm, d), F32)]),
        input_output_aliases=aliases,
        compiler_params=_params("arbitrary", "arbitrary"),
        name="moe_grouped",
    )(*args)


def _combine_kernel(y1_ref, y2_ref, x_ref, g2_ref, rt_ref, o_ref):
    rt = rt_ref[...]
    w1, w2 = rt[:, 2:3], rt[:, 3:4]
    half = y1_ref.shape[1]
    for k, (a, c) in enumerate(zip(_unpack_bf16_pairs(y1_ref[...]), _unpack_bf16_pairs(y2_ref[...]))):
        cs = slice(k * half, (k + 1) * half)
        o_ref[:, cs] = x_ref[:, cs] + g2_ref[:, cs] * (w1 * a + w2 * c)


def _moe_combine(yg, x, g2, route):
    b, s, d = x.shape
    tm = _pick_tile(s, 512)
    tok = lambda n: pl.BlockSpec((None, tm, n), lambda bi, i: (bi, i, 0))
    return pl.pallas_call(
        _combine_kernel,
        out_shape=jax.ShapeDtypeStruct((b, s, d), F32),
        grid=(b, s // tm),
        in_specs=[tok(d // 2), pl.BlockSpec((None, tm, d // 2), lambda bi, i: (b + bi, i, 0)), tok(d),
                  pl.BlockSpec((None, 1, d), lambda bi, i: (bi, 0, 0)), tok(LANES)],
        out_specs=tok(d),
        compiler_params=_params("parallel", "parallel"),
        name="moe_combine",
    )(yg, yg, x, g2, route)


def _moe(h, route, x, g2, wg, wu, wd):
    b, s, d = x.shape
    n_tok = b * s
    tm = MOE_TM
    rt = route.reshape(n_tok, LANES)
    flat_e = jnp.concatenate([rt[:, 0], rt[:, 1]]).astype(jnp.int32)
    onehot = (flat_e[:, None] == jnp.arange(N_EXPERTS, dtype=jnp.int32)[None, :]).astype(jnp.int32)
    csum = jnp.cumsum(onehot, axis=0)
    rank = jnp.sum(csum * onehot, axis=1) - 1
    counts = csum[-1]
    tiles_e = (counts + tm - 1) // tm
    tiles_cum = jnp.cumsum(tiles_e)
    row_start = (tiles_cum - tiles_e) * tm
    dest = jnp.sum(onehot * row_start[None, :], axis=1) + rank
    n_tiles = 2 * n_tok // tm + N_EXPERTS
    tile_ids = jnp.arange(n_tiles, dtype=jnp.int32)
    tile_expert = jnp.sum((tile_ids[:, None] >= tiles_cum[None, :]).astype(jnp.int32), axis=1)
    last_e = jnp.max(jnp.where(tiles_e > 0, jnp.arange(N_EXPERTS, dtype=jnp.int32), 0))
    tile_expert = jnp.minimum(tile_expert, last_e).astype(jnp.int32)
    n_used = tiles_cum[-1:].astype(jnp.int32)
    dest = dest.astype(jnp.int32)
    row_src = _invert_rows(dest, n_tiles * tm)
    n_chunks = max(c for c in range(1, MOE_CHUNKS + 1) if n_tiles % c == 0)
    tiles_c = n_tiles // n_chunks
    h_flat = h.reshape(n_tok, d // 2)
    xs = [_gather_rows(row_src[c * tiles_c * tm:(c + 1) * tiles_c * tm], h_flat) for c in range(n_chunks)]
    y = None
    for c in range(n_chunks):
        y = _moe_grouped(tile_expert, n_used, xs[c], wg, wu, wd, y, c * tiles_c, n_tiles * tm)
    yg = _gather_rows(dest, y).reshape(2 * b, s, d // 2)
    return _moe_combine(yg, x, g2, route)


def _rope_tables(n):
    rows = n // GRID_W
    r = np.repeat(np.arange(rows), GRID_W).astype(np.float64)
    col = np.tile(np.arange(GRID_W), rows).astype(np.float64)
    freqs = ROPE_BASE ** (-np.arange(ROPE_FREQS, dtype=np.float64) / ROPE_FREQS)
    ang = np.stack([r[:, None] * freqs, col[:, None] * freqs], axis=1)
    ang = np.repeat(ang[:, :, None, :], 2, axis=2).reshape(n, HEAD_DIM)
    ang = np.tile(ang, (1, LANES // HEAD_DIM))
    cos, sin = np.cos(ang), np.sin(ang)
    first_half = (np.arange(LANES) % (2 * ROPE_FREQS)) < ROPE_FREQS
    to_dev = lambda t: jnp.asarray(t.astype(np.float32))
    return to_dev(cos), to_dev(np.where(first_half, -sin, 0.0)), to_dev(np.where(first_half, 0.0, sin))


def _lane_rows(lg):
    return jnp.repeat(lg.astype(F32), HEAD_DIM).reshape(RET_HEADS // 2, 1, LANES)


def kernel(x, c, ctx, c_ctx, ada_w, ada_b, norm1_w, norm2_w, w_in, w_out, q_norm_w, k_norm_w,
           attn_sink, ret_decay_f, ret_decay_b, ret_gn_w, conv_w, conv_b, conv_ln_w, conv_ln_b,
           ffn_w_gate, ffn_w_up, ffn_w_down, router_w, moe_w_gate, moe_w_up, moe_w_down):
    b, n, d = x.shape
    n_ctx = ctx.shape[1]
    depth = ada_w.shape[0]
    cond = jnp.zeros((SUBLANES, d), F32).at[0:b].set(c).at[b].set(c_ctx)
    mods = _adaln(cond, ada_w, ada_b).reshape(depth, SUBLANES, 6, d)
    cos, sa, sb = _rope_tables(n)
    ones_c = jnp.ones((n_ctx, LANES), F32)
    zeros_c = jnp.zeros((n_ctx, LANES), F32)
    zero_state = jnp.zeros((b, RET_HEADS // 2, LANES, LANES), F32)
    row = lambda v: v.reshape(1, -1)
    slab_rows = b * (n // ATT_BLOCK // ATT_QBLOCKS) * BF16_SUBLANES
    riders = {l: {} for l in range(depth)}

    def ride(l, key, w):
        w2 = w.reshape(-1, w.shape[-1])
        if 0 <= l < depth and (n // ATT_BLOCK) % ATT_QBLOCKS == 0 and w2.shape[0] % slab_rows == 0:
            riders[l][key] = w2

    for j, w in enumerate((moe_w_gate, moe_w_up, moe_w_down)):
        ride(j * min(depth, 2) // 3, ("moe", j), w)
    for l in range(depth):
        if l % 2 == 0:
            ride(l, ("ffn", l, 0), ffn_w_gate[l // 2])
            ride(l, ("ffn", l, 1), ffn_w_up[l // 2])
            ride(l, ("ffn", l, 2), ffn_w_down[l // 2])
        ride(l - 1, ("w_in", l), w_in[l])
        ride(l - 1, ("w_out", l), w_out[l])
    rounded = {}

    def bf16_of(key, w):
        return rounded[key].reshape(w.shape) if key in rounded else w.astype(BF16)

    for l in range(depth):
        last = l == depth - 1
        m_lat = [mods[l, 0:b, k][:, None, :] for k in range(6)]
        m_ctx = [jnp.broadcast_to(mods[l, b, k][None, None, :], (b, 1, d)) for k in range(6)]
        w_in_bf = bf16_of(("w_in", l), w_in[l])
        w_out_bf = bf16_of(("w_out", l), w_out[l])
        qw = row(jnp.tile(q_norm_w[l], LANES // HEAD_DIM))
        kw = row(jnp.tile(k_norm_w[l], LANES // HEAD_DIM))
        lgf = _lane_rows(jax.nn.log_sigmoid(ret_decay_f[l].astype(F32)))
        lgb = _lane_rows(jax.nn.log_sigmoid(ret_decay_b[l].astype(F32)))
        sink_tab = jnp.repeat(attn_sink[l].astype(F32), ATT_BLOCK).reshape(ATT_Q_HEADS // 2, 2 * ATT_BLOCK)

        q, k, v, rk, rv, rq, rg, cv = _inproj(x, row(norm1_w[l]), m_lat[1], m_lat[0], w_in_bf,
                                               cos, sa, sb, qw, kw)
        qc, kc, vc, rkc, rvc, rqc, rgc, cvc = _inproj(ctx, row(norm1_w[l]), m_ctx[1], m_ctx[0], w_in_bf,
                                                       ones_c, zeros_c, zeros_c, qw, kw)
        rf_c, s_f = _ret_states(lgf, rkc, rvc, zero_state, reverse=False)
        rb_c, s_b = _ret_states(lgb, rkc, rvc, zero_state, reverse=True)
        rf, _ = _ret_states(lgf, rk, rv, s_f, reverse=False)
        rb, _ = _ret_states(lgb, rk, rv, s_b, reverse=True)

        att, cast_out = _attention(q, k, v, kc, vc, sink_tab, window=True, cast=list(riders[l].values()))
        rounded.update(zip(riders[l].keys(), cast_out))
        ret = _ret_out(lgf, lgb, rq, rk, rv, rg, rf, rb, row(ret_gn_w[l]))
        cnv = _conv(cv, conv_w[l], row(conv_b[l]), row(conv_ln_w[l]), row(conv_ln_b[l]))

        if l % 2 == 0:
            i = l // 2
            wg, wu, wd = (bf16_of(("ffn", l, j), w[i]) for j, w in enumerate((ffn_w_gate, ffn_w_up, ffn_w_down)))
            x_new = _mixer_ffn(att, ret, cnv, w_out_bf, x, m_lat[2], row(norm2_w[l]), m_lat[4], m_lat[3],
                               wg, wu, wd, m_lat[5])
        else:
            i = l // 2
            wg, wu, wd = (bf16_of(("moe", j), w)[i] for j, w in enumerate((moe_w_gate, moe_w_up, moe_w_down)))
            x_mid, h2, route = _outproj_route(att, ret, cnv, w_out_bf, x, m_lat[2], row(norm2_w[l]),
                                              m_lat[4], m_lat[3], router_w[i])
            x_new = _moe(h2, route, x_mid, m_lat[5], wg, wu, wd)

        if not last:
            att_c, _ = _attention(qc, None, None, kc, vc, sink_tab, window=False)
            ret_c = _ret_out(lgf, lgb, rqc, rkc, rvc, rgc, rf_c, rb_c, row(ret_gn_w[l]))
            cnv_c = _conv(cvc, conv_w[l], row(conv_b[l]), row(conv_ln_w[l]), row(conv_ln_b[l]))
            if l % 2 == 0:
                ctx = _mixer_ffn(att_c, ret_c, cnv_c, w_out_bf, ctx, m_ctx[2], row(norm2_w[l]), m_ctx[4],
                                 m_ctx[3], wg, wu, wd, m_ctx[5])
            else:
                c_mid, h2c, route_c = _outproj_route(att_c, ret_c, cnv_c, w_out_bf, ctx, m_ctx[2],
                                                     row(norm2_w[l]), m_ctx[4], m_ctx[3], router_w[i])
                ctx = _moe(h2c, route_c, c_mid, m_ctx[5], wg, wu, wd)
        x = x_new
    return x
```

```python
import functools

import jax
import jax.numpy as jnp
import numpy as np
from jax import lax
from jax.experimental import pallas as pl
from jax.experimental.pallas import tpu as pltpu
from jax.experimental.pallas import tpu_sc as plsc

F32 = jnp.float32
BF16 = jnp.bfloat16

GRID_W = 64
HEAD_DIM = 64
ATT_Q_HEADS = 8
ATT_KV_HEADS = 2
ATT_WINDOW = 128
ATT_BLOCK = 128
RET_HEADS = 4
RET_CHUNK = 128
RET_K_SCALE = HEAD_DIM ** -0.5
ATT_SCALE = HEAD_DIM ** -0.5
CONV_CH = 256
CONV_WIDTH = 31
ROPE_BASE = 10000.0
ROPE_FREQS = HEAD_DIM // 4
D_ATT = ATT_Q_HEADS * HEAD_DIM
D_RET = RET_HEADS * HEAD_DIM
ATT_KV_W = ATT_KV_HEADS * HEAD_DIM
C_ATT_K = 0
C_ATT_V = C_ATT_K + ATT_KV_W
C_RET_K = C_ATT_V + ATT_KV_W
C_RET_V = C_RET_K + D_RET
C_ATT_Q = C_RET_V + D_RET
C_RET_Q = C_ATT_Q + D_ATT
C_RET_G = C_RET_Q + D_RET
C_CONV = C_RET_G + D_RET
N_EXPERTS = 8
EPS = 1e-6
NEG_INF = -1e30

LANES = 128
SUBLANES = 8
BF16_SUBLANES = 16
VMEM_LIMIT = 48 * 1024 * 1024
CONV_HALO = 16
CONV_SUB = 32
CONV_CHAINS = 4
ATT_QBLOCKS = 4
RET_UNROLL = 4
MOE_TM = 512
MOE_CHUNKS = 2
MOE_SUB = 512
FFN_SUB = 512
MOE_TF = 1792
SC_LANES = 16
SC_WINDOW = 64


def _params(*sem):
    return pltpu.CompilerParams(dimension_semantics=sem, vmem_limit_bytes=VMEM_LIMIT)


def _sigmoid(x):
    return 1.0 / (1.0 + jnp.exp(-x))


def _pack_bf16_pairs(v):
    c = v.shape[1] // 2
    bits = pltpu.bitcast(v.astype(BF16).astype(F32), jnp.uint32)
    packed = (bits[:, c:] & jnp.uint32(0xFFFF0000)) | (bits[:, :c] >> 16)
    return pltpu.bitcast(packed, jnp.int32)


def _unpack_bf16_pairs(p):
    bits = pltpu.bitcast(p, jnp.uint32)
    return pltpu.bitcast(bits << 16, F32), pltpu.bitcast(bits & jnp.uint32(0xFFFF0000), F32)


def _pick_tile(n, pref):
    t = min(n, pref)
    assert n % t == 0, (n, t)
    return t


def _adaln_kernel(c_ref, w_ref, b_ref, o_ref):
    c = c_ref[...]
    s = c * _sigmoid(c)
    w = w_ref[...]
    s_hi, w_hi = s.astype(BF16), w.astype(BF16)
    s_lo = (s - s_hi.astype(F32)).astype(BF16)
    w_lo = (w - w_hi.astype(F32)).astype(BF16)
    dot = functools.partial(jnp.dot, preferred_element_type=F32)
    o_ref[...] = (dot(s_hi, w_hi) + dot(s_lo, w_hi)) + dot(s_hi, w_lo) + b_ref[...]


def _adaln(cond, ada_w, ada_b):
    depth, d, n = ada_w.shape
    tn = _pick_tile(n, 3072)
    return pl.pallas_call(
        _adaln_kernel,
        out_shape=jax.ShapeDtypeStruct((depth, cond.shape[0], n), F32),
        grid=(depth, n // tn),
        in_specs=[pl.BlockSpec(cond.shape, lambda l, j: (0, 0)),
                  pl.BlockSpec((None, d, tn), lambda l, j: (l, 0, j)),
                  pl.BlockSpec((None, 1, tn), lambda l, j: (l, 0, j))],
        out_specs=pl.BlockSpec((None, cond.shape[0], tn), lambda l, j: (l, 0, j)),
        compiler_params=_params("parallel", "parallel"),
        name="adaln",
    )(cond, ada_w, ada_b.reshape(depth, 1, n))


def _modulated_rms(x, nw, sc, sh):
    ms = jnp.mean(x * x, axis=-1, keepdims=True)
    return (x * lax.rsqrt(ms + EPS) * nw) * (1.0 + sc) + sh


def _head_group_matrix():
    r = lax.broadcasted_iota(jnp.int32, (2 * LANES, 2 * LANES), 0) // HEAD_DIM
    c = lax.broadcasted_iota(jnp.int32, (2 * LANES, 2 * LANES), 1) // HEAD_DIM
    return jnp.where(r == c, 1.0, 0.0).astype(BF16)


def _per_head_sum(v, grp):
    hi = v.astype(BF16)
    lo = (v - hi.astype(F32)).astype(BF16)
    r = jnp.dot(jnp.concatenate([hi, lo], axis=1), grp, preferred_element_type=F32)
    return r[:, 0:LANES] + r[:, LANES:]


def _dup_halves(t):
    sw = pltpu.roll(t, HEAD_DIM, 1)
    lo = lax.broadcasted_iota(jnp.int32, t.shape, 1) < HEAD_DIM
    return jnp.where(lo, t, sw), jnp.where(lo, sw, t)


def _inproj_kernel(x_ref, nw_ref, sc_ref, sh_ref, w_ref, cos_ref, sa_ref, sb_ref, qw_ref, kw_ref,
                   q_ref, k_ref, v_ref, rk_ref, rv_ref, rq_ref, rg_ref, cv_ref):
    hb = _modulated_rms(x_ref[...], nw_ref[...], sc_ref[...], sh_ref[...]).astype(BF16)

    def proj(c0, n):
        return jnp.dot(hb, w_ref[:, c0:c0 + n], preferred_element_type=F32)

    grp = _head_group_matrix()
    cos, sa, sb = cos_ref[...], sa_ref[...], sb_ref[...]

    def norm_rope(p, wrow):
        y = p * lax.rsqrt(_per_head_sum(p * p, grp) * (1.0 / HEAD_DIM) + EPS) * wrow
        return (y * cos + pltpu.roll(y, LANES - ROPE_FREQS, 1) * sa
                + pltpu.roll(y, ROPE_FREQS, 1) * sb)

    def put_kv(kv):
        k0, k1 = _dup_halves(norm_rope(kv[:, 0:ATT_KV_W], kw_ref[...]))
        k_ref[:, 0:LANES] = k0.astype(BF16)
        k_ref[:, LANES:2 * LANES] = k1.astype(BF16)
        v_ref[...] = kv[:, ATT_KV_W:].T.astype(BF16)

    def put_q(qall):
        qw = qw_ref[...] * ATT_SCALE
        for j in range(D_ATT // LANES):
            q_ref[:, j * LANES:(j + 1) * LANES] = norm_rope(qall[:, j * LANES:(j + 1) * LANES], qw).astype(BF16)

    def put_rk(p):
        rk_ref[...] = (p * RET_K_SCALE).astype(BF16)

    def put_rv(p):
        rv_ref[...] = p.astype(BF16)

    def put_rq(p):
        rq_ref[...] = p.astype(BF16)

    def put_rg(p):
        rg_ref[...] = p

    def put_conv(p):
        cv_ref[...] = p[:, 0:CONV_CH] * _sigmoid(p[:, CONV_CH:])

    units = [(C_ATT_K, 2 * ATT_KV_W, put_kv), (C_ATT_Q, D_ATT, put_q), (C_RET_K, D_RET, put_rk),
             (C_RET_V, D_RET, put_rv), (C_RET_Q, D_RET, put_rq), (C_RET_G, D_RET, put_rg),
             (C_CONV, 2 * CONV_CH, put_conv)]
    ahead = 2
    pending = [proj(c0, n) for c0, n, _ in units[:ahead]]
    for n, (_, _, put) in enumerate(units):
        if n + ahead < len(units):
            pending.append(proj(*units[n + ahead][:2]))
        put(pending.pop(0))


def _inproj(x, nw, sc, sh, w_bf, cos, sa, sb, qw, kw):
    b, s, d = x.shape
    tm = _pick_tile(s, 1024)
    row = lambda n: pl.BlockSpec((1, n), lambda bi, i: (0, 0))
    per_b = pl.BlockSpec((None, 1, d), lambda bi, i: (bi, 0, 0))
    tab = pl.BlockSpec((tm, LANES), lambda bi, i: (i, 0))
    tok = lambda n: pl.BlockSpec((None, tm, n), lambda bi, i: (bi, i, 0))
    shp = lambda n, dt: jax.ShapeDtypeStruct((b, s, n), dt)
    return pl.pallas_call(
        _inproj_kernel,
        out_shape=(shp(D_ATT, BF16), shp(2 * LANES, BF16), jax.ShapeDtypeStruct((b, ATT_KV_W, s), BF16),
                   shp(D_RET, BF16), shp(D_RET, BF16), shp(D_RET, BF16), shp(D_RET, F32),
                   shp(CONV_CH, F32)),
        grid=(b, s // tm),
        in_specs=[tok(d), row(d), per_b, per_b,
                  pl.BlockSpec(w_bf.shape, lambda bi, i: (0, 0)),
                  tab, tab, tab, row(LANES), row(LANES)],
        out_specs=(tok(D_ATT), tok(2 * LANES), pl.BlockSpec((None, ATT_KV_W, tm), lambda bi, i: (bi, 0, i)),
                   tok(D_RET), tok(D_RET), tok(D_RET), tok(D_RET), tok(CONV_CH)),
        compiler_params=_params("parallel", "parallel"),
        name="inproj",
    )(x, nw, sc, sh, w_bf, cos, sa, sb, qw, kw)


def _attn_kernel(*refs, window, nq, n_cast):
    blk = ATT_BLOCK
    n_in = len(refs) - 1 - n_cast
    for src, dst in zip(refs[n_in - n_cast:n_in], refs[n_in + 1:]):
        dst[...] = src[...].astype(BF16)
    refs = refs[:n_in - n_cast] + refs[n_in:n_in + 1]
    if window:
        q_ref = refs[0]
        k_refs = refs[1:nq + 3]
        v_refs = refs[nq + 3:2 * nq + 5]
        kx_ref, vx_ref, sink_ref, o_ref = refs[2 * nq + 5:]
    else:
        q_ref, kx_ref, vx_ref, sink_ref, o_ref = refs
    n_ctx = kx_ref.shape[0]
    nk = 3 * blk + n_ctx if window else n_ctx
    if window:
        i = pl.program_id(1)
        last = nq * pl.num_programs(1) - 1
        key = lax.broadcasted_iota(jnp.int32, (blk, 2 * blk), 0)
        qry = lax.broadcasted_iota(jnp.int32, (blk, 2 * blk), 1) & (blk - 1)

        def band_masks(sub):
            off_prev = jnp.where(nq * i + sub > 0, 0, blk)
            off_next = jnp.where(nq * i + sub < last, 0, blk)
            return key >= qry + off_prev, key + off_next <= qry
    first_head = lax.broadcasted_iota(jnp.int32, (blk, LANES), 1) < HEAD_DIM
    ones = jnp.ones((2 * SUBLANES, nk), BF16)

    def scores(sub, pair):
        g = pair // 2
        gs = slice(g * LANES, (g + 1) * LANES)
        if window:
            kcat = jnp.concatenate([r[:, gs] for r in k_refs[sub:sub + 3]] + [kx_ref[:, gs]], axis=0)
        else:
            kcat = kx_ref[:, gs]
        qp = q_ref[sub * blk:(sub + 1) * blk, pair * LANES:(pair + 1) * LANES]
        zero = jnp.zeros_like(qp)
        w = jnp.concatenate([jnp.where(first_head, qp, zero), jnp.where(first_head, zero, qp)], axis=0)
        return lax.dot_general(kcat, w, (((1,), (1,)), ((), ())), preferred_element_type=F32)

    def masked_max(sub, pair, s):
        if window:
            mask_prev, mask_next = band_masks(sub)
            parts = [jnp.where(mask_prev, s[0:blk], NEG_INF), s[blk:2 * blk],
                     jnp.where(mask_next, s[2 * blk:3 * blk], NEG_INF), s[3 * blk:]]
        else:
            parts = [s]
        m = sink_ref[pair:pair + 1, :]
        for part in parts:
            m = jnp.maximum(m, jnp.max(part, axis=0, keepdims=True))
        return parts, m

    def softmax(sub, pair, parts_m):
        parts, m = parts_m
        p = jnp.concatenate([jnp.exp(part - m).astype(BF16) for part in parts], axis=0)
        return p, jnp.exp(sink_ref[pair:pair + 1, :] - m)

    def output(sub, pair, p, sink_p):
        g = pair // 2
        vs = slice(g * HEAD_DIM, (g + 1) * HEAD_DIM)
        if window:
            vt = jnp.concatenate([r[vs, :] for r in v_refs[sub:sub + 3]] + [vx_ref[vs, :]], axis=1)
        else:
            vt = vx_ref[vs, :]
        vaug = jnp.concatenate([vt, ones], axis=0)
        o = jnp.dot(vaug, p, preferred_element_type=F32)
        on = o[0:HEAD_DIM, :] * (1.0 / (o[HEAD_DIM:HEAD_DIM + 1, :] + sink_p))
        ot = jnp.concatenate([on[:, 0:blk], on[:, blk:2 * blk]], axis=0)
        o_ref[sub * blk:(sub + 1) * blk, pair * LANES:(pair + 1) * LANES] = ot.T.astype(BF16)

    units = [(sub, pair) for sub in range(nq) for pair in range(ATT_Q_HEADS // 2)]
    stages = (lambda sub, pair, _: scores(sub, pair), masked_max, softmax,
              lambda sub, pair, ps: output(sub, pair, *ps))
    vals = {}
    for step in range(len(units) + len(stages) - 1):
        for k, stage in enumerate(stages):
            u = step - k
            if 0 <= u < len(units):
                vals[u, k] = stage(*units[u], vals.pop((u, k - 1), None))


def _attention(q, k, vt, kx, vxt, sink_tab, window, cast=()):
    b, s, _ = q.shape
    blk = ATT_BLOCK
    assert ATT_WINDOW == blk, "the band masks assume a window of exactly one key block on each side"
    nb = s // blk
    nq = ATT_QBLOCKS if nb % ATT_QBLOCKS == 0 else 1
    n_ctx = kx.shape[1]
    qspec = pl.BlockSpec((None, nq * blk, D_ATT), lambda bi, i: (bi, i, 0))
    kctx = pl.BlockSpec((None, n_ctx, 2 * LANES), lambda bi, i: (bi, 0, 0))
    vctx = pl.BlockSpec((None, ATT_KV_W, n_ctx), lambda bi, i: (bi, 0, 0))
    snk = pl.BlockSpec(sink_tab.shape, lambda bi, i: (0, 0))
    if window:
        at = lambda off: (lambda i: jnp.clip(nq * i + off, 0, nb - 1))
        kspec = lambda f: pl.BlockSpec((None, blk, 2 * LANES), lambda bi, i: (bi, f(i), 0))
        vspec = lambda f: pl.BlockSpec((None, ATT_KV_W, blk), lambda bi, i: (bi, 0, f(i)))
        offs = range(-1, nq + 1)
        in_specs = ([qspec] + [kspec(at(o)) for o in offs] + [vspec(at(o)) for o in offs]
                    + [kctx, vctx, snk])
        args = (q,) + (k,) * (nq + 2) + (vt,) * (nq + 2) + (kx, vxt, sink_tab)
    else:
        in_specs = [qspec, kctx, vctx, snk]
        args = (q, kx, vxt, sink_tab)
    steps = nb // nq
    out_shape, out_specs = [jax.ShapeDtypeStruct((b, s, D_ATT), BF16)], [qspec]
    for w in cast:
        rows = w.shape[0] // (b * steps)
        assert rows * b * steps == w.shape[0] and rows % (2 * SUBLANES) == 0, w.shape
        slab = pl.BlockSpec((rows, w.shape[1]), lambda bi, i: (bi * steps + i, 0))
        in_specs.append(slab)
        out_specs.append(slab)
        out_shape.append(jax.ShapeDtypeStruct(w.shape, BF16))
    res = pl.pallas_call(
        functools.partial(_attn_kernel, window=window, nq=nq, n_cast=len(cast)),
        out_shape=tuple(out_shape),
        grid=(b, steps),
        in_specs=in_specs,
        out_specs=tuple(out_specs),
        compiler_params=_params("parallel", "parallel"),
        name="attention_window" if window else "attention_ctx",
    )(*args, *cast)
    return res[0], tuple(res[1:])


def _ret_state_kernel(lgl_ref, k_ref, v_ref, s0_ref, r_ref, fin_ref, s_scr, *, cb, reverse):
    c = pl.program_id(1)
    L = RET_CHUNK

    @pl.when(c == 0)
    def _():
        s_scr[...] = s0_ref[...]

    jj = lax.broadcasted_iota(jnp.int32, (L, LANES), 0).astype(F32)
    expo = jj if reverse else (L - 1.0) - jj
    same_head = (lax.broadcasted_iota(jnp.int32, (LANES, LANES), 0) // HEAD_DIM
                 == lax.broadcasted_iota(jnp.int32, (LANES, LANES), 1) // HEAD_DIM)
    for pr in range(RET_HEADS // 2):
        cs = slice(pr * LANES, (pr + 1) * LANES)
        lgl = lgl_ref[pr]
        kdec = jnp.exp(expo * lgl)
        cdec = jnp.exp(float(L) * lgl)
        state = s_scr[pr]
        for t in range(cb):
            cc = cb - 1 - t if reverse else t
            rows = slice(cc * L, (cc + 1) * L)
            r_ref[cc, pr] = state.astype(BF16)
            kd = k_ref[rows, cs].astype(F32) * kdec
            u = jnp.dot(kd.T.astype(BF16), v_ref[rows, cs], preferred_element_type=F32)
            state = cdec * state + jnp.where(same_head, u, 0.0)
        s_scr[pr] = state

    @pl.when(c == pl.num_programs(1) - 1)
    def _():
        fin_ref[...] = s_scr[...]


def _ret_states(lgl, rk, rv, s0, reverse):
    b, s, _ = rk.shape
    nc = s // RET_CHUNK
    cb = _pick_tile(nc, 16)
    nblk = nc // cb
    npair = RET_HEADS // 2
    blk_idx = (lambda c: nblk - 1 - c) if reverse else (lambda c: c)
    tok = pl.BlockSpec((None, cb * RET_CHUNK, D_RET), lambda bi, c: (bi, blk_idx(c), 0))
    st = pl.BlockSpec((None, npair, LANES, LANES), lambda bi, c: (bi, 0, 0, 0))
    return pl.pallas_call(
        functools.partial(_ret_state_kernel, cb=cb, reverse=reverse),
        out_shape=(jax.ShapeDtypeStruct((b, nc, npair, LANES, LANES), BF16),
                   jax.ShapeDtypeStruct((b, npair, LANES, LANES), F32)),
        grid=(b, nblk),
        in_specs=[pl.BlockSpec(lgl.shape, lambda bi, c: (0, 0, 0)), tok, tok, st],
        out_specs=(pl.BlockSpec((None, cb, npair, LANES, LANES),
                                lambda bi, c: (bi, blk_idx(c), 0, 0, 0)), st),
        scratch_shapes=[pltpu.VMEM((npair, LANES, LANES), F32)],
        compiler_params=_params("parallel", "arbitrary"),
        name="ret_state_bwd" if reverse else "ret_state_fwd",
    )(lgl, rk, rv, s0)


def _ret_out_kernel(lgf_ref, lgb_ref, q_ref, k_ref, v_ref, g_ref, rf_ref, rb_ref, gnw_ref, o_ref, *, cb):
    L = RET_CHUNK
    i0 = lax.broadcasted_iota(jnp.int32, (L, LANES), 0).astype(F32)
    i1 = lax.broadcasted_iota(jnp.int32, (L, LANES), 1).astype(F32)
    diff = i0 - i1
    lo = lax.broadcasted_iota(jnp.int32, (L, LANES), 1) < HEAD_DIM
    grp = _head_group_matrix()
    inv = 1.0 / HEAD_DIM
    tabs = []
    for pr in range(RET_HEADS // 2):
        lgf, lgb = lgf_ref[pr], lgb_ref[pr]
        dmat = [jnp.where(diff >= 0.0,
                          jnp.exp(jnp.maximum(diff, 0.0) * lgf[:, a:a + 1]),
                          jnp.exp(jnp.maximum(-diff, 0.0) * lgb[:, a:a + 1]))
                for a in (0, HEAD_DIM)]
        tabs.append((jnp.exp((i0 + 1.0) * lgf), jnp.exp((float(L) - i0) * lgb),
                     jnp.concatenate(dmat, axis=0), gnw_ref[:, pr * LANES:(pr + 1) * LANES]))

    def decayed_scores(t, pr):
        rows = pl.ds(pl.multiple_of(t * L, L), L)
        cs = slice(pr * LANES, (pr + 1) * LANES)
        qp, kp = q_ref[rows, cs], k_ref[rows, cs]
        zero = jnp.zeros_like(kp)
        qz = jnp.concatenate([jnp.where(lo, qp, zero), jnp.where(lo, zero, qp)], axis=0)
        return lax.dot_general(qz, kp, (((1,), (1,)), ((), ())), preferred_element_type=F32) * tabs[pr][2]

    def mix(t, pr, a):
        rows = pl.ds(pl.multiple_of(t * L, L), L)
        cs = slice(pr * LANES, (pr + 1) * LANES)
        xif, xib, _, _ = tabs[pr]
        vp = v_ref[rows, cs]
        zero = jnp.zeros_like(vp)
        qf = q_ref[rows, cs].astype(F32)
        lhs = jnp.concatenate([a[0:L].astype(BF16), a[L:].astype(BF16),
                               (qf * xif).astype(BF16), (qf * xib).astype(BF16)], axis=1)
        rhs = jnp.concatenate([jnp.where(lo, vp, zero), jnp.where(lo, zero, vp),
                               rf_ref[t, pr], rb_ref[t, pr]], axis=0)
        return jnp.dot(lhs, rhs, preferred_element_type=F32)

    def centre(t, pr, acc):
        return acc - _per_head_sum(acc, grp) * inv

    def readout(t, pr, dl):
        rows = pl.ds(pl.multiple_of(t * L, L), L)
        cs = slice(pr * LANES, (pr + 1) * LANES)
        var = _per_head_sum(dl * dl, grp) * inv
        y = dl * lax.rsqrt(var + EPS) * tabs[pr][3]
        gt = g_ref[rows, cs]
        o_ref[rows, cs] = (gt * _sigmoid(gt) * y).astype(BF16)

    unroll = min(RET_UNROLL, cb)
    stages = (lambda t, pr, _: decayed_scores(t, pr), mix, centre, readout)

    def group(gi, carry):
        units = [(gi * unroll + u, pr) for u in range(unroll) for pr in range(RET_HEADS // 2)]
        vals = {}
        for step in range(len(units) + len(stages) - 1):
            for k, stage in enumerate(stages):
                u = step - k
                if 0 <= u < len(units):
                    vals[u, k] = stage(*units[u], vals.pop((u, k - 1), None))
        return carry

    lax.fori_loop(0, cb // unroll, group, 0)


def _ret_out(lgf, lgb, rq, rk, rv, rg, rf, rb, gnw):
    b, s, _ = rq.shape
    nc = s // RET_CHUNK
    cb = _pick_tile(nc, 8)
    npair = RET_HEADS // 2
    tok = pl.BlockSpec((None, cb * RET_CHUNK, D_RET), lambda bi, c: (bi, c, 0))
    st = pl.BlockSpec((None, cb, npair, LANES, LANES), lambda bi, c: (bi, c, 0, 0, 0))
    lg = pl.BlockSpec(lgf.shape, lambda bi, c: (0, 0, 0))
    return pl.pallas_call(
        functools.partial(_ret_out_kernel, cb=cb),
        out_shape=jax.ShapeDtypeStruct((b, s, D_RET), BF16),
        grid=(b, nc // cb),
        in_specs=[lg, lg, tok, tok, tok, tok, st, st, pl.BlockSpec((1, D_RET), lambda bi, c: (0, 0))],
        out_specs=tok,
        compiler_params=_params("parallel", "parallel"),
        name="ret_out",
    )(lgf, lgb, rq, rk, rv, rg, rf, rb, gnw)


def _conv_kernel(prev_ref, cur_ref, next_ref, w_ref, b_ref, lnw_ref, lnb_ref, o_ref,
                 xpad, shifted, hbuf, wtile):
    i = pl.program_id(1)
    tm = cur_ref.shape[0]
    halo = CONV_HALO
    first = i == 0
    last = i == pl.num_programs(1) - 1
    xpad[0:halo, :] = jnp.where(first, 0.0, prev_ref[...])
    xpad[halo:halo + tm, :] = cur_ref[...]
    xpad[halo + tm:2 * halo + tm, :] = jnp.where(last, 0.0, next_ref[...])
    span = tm + 2 * halo - SUBLANES
    for r in range(SUBLANES):
        shifted[r] = xpad[r:r + span, :]
    base = halo - CONV_WIDTH // 2
    bias = b_ref[...]
    ch = cur_ref.shape[1]
    for w in range(CONV_WIDTH):
        wtile[w] = jnp.broadcast_to(w_ref[w:w + 1, :], (SUBLANES, ch))

    taps_by_shift = {}
    for w in range(CONV_WIDTH):
        taps_by_shift.setdefault((base + w) % SUBLANES, []).append(((base + w) // SUBLANES, w))

    def sub(sb, carry):
        r0 = pl.multiple_of(sb * CONV_SUB, CONV_SUB)
        chains = [None] * CONV_CHAINS
        groups = CONV_SUB // SUBLANES
        for r, taps in sorted(taps_by_shift.items()):
            a_lo = min(a for a, _ in taps)
            n_g = max(a for a, _ in taps) - a_lo + groups
            win = shifted[r, pl.ds(r0 + a_lo * SUBLANES, n_g * SUBLANES), :].reshape(n_g, SUBLANES, ch)
            for a, w in taps:
                term = win[a - a_lo:a - a_lo + groups] * wtile[w]
                c = w % CONV_CHAINS
                chains[c] = term if chains[c] is None else chains[c] + term
        total = (chains[0] + chains[1]) + (chains[2] + chains[3])
        hbuf[pl.ds(r0, CONV_SUB), :] = total.reshape(CONV_SUB, ch) + bias
        return carry

    lax.fori_loop(0, tm // CONV_SUB, sub, 0)
    h = hbuf[...]
    mu = jnp.mean(h, axis=-1, keepdims=True)
    dl = h - mu
    var = jnp.mean(dl * dl, axis=-1, keepdims=True)
    y = dl * lax.rsqrt(var + EPS) * lnw_ref[...] + lnb_ref[...]
    o_ref[...] = (y * _sigmoid(y)).astype(BF16)


def _conv(cv, w, bias, lnw, lnb):
    b, s, ch = cv.shape
    tm = _pick_tile(s, 1024)
    hpb = tm // CONV_HALO
    nh = s // CONV_HALO
    row = pl.BlockSpec((1, ch), lambda bi, i: (0, 0))
    span = tm + 2 * CONV_HALO - SUBLANES
    return pl.pallas_call(
        _conv_kernel,
        out_shape=jax.ShapeDtypeStruct((b, s, ch), BF16),
        grid=(b, s // tm),
        in_specs=[pl.BlockSpec((None, CONV_HALO, ch), lambda bi, i: (bi, jnp.maximum(i * hpb - 1, 0), 0)),
                  pl.BlockSpec((None, tm, ch), lambda bi, i: (bi, i, 0)),
                  pl.BlockSpec((None, CONV_HALO, ch),
                               lambda bi, i: (bi, jnp.minimum((i + 1) * hpb, nh - 1), 0)),
                  pl.BlockSpec(w.shape, lambda bi, i: (0, 0)), row, row, row],
        out_specs=pl.BlockSpec((None, tm, ch), lambda bi, i: (bi, i, 0)),
        scratch_shapes=[pltpu.VMEM((tm + 2 * CONV_HALO, ch), F32),
                        pltpu.VMEM((SUBLANES, span, ch), F32),
                        pltpu.VMEM((tm, ch), F32),
                        pltpu.VMEM((CONV_WIDTH, SUBLANES, ch), F32)],
        compiler_params=_params("parallel", "parallel"),
        name="conv",
    )(cv, cv, cv, w, bias, lnw, lnb)


def _outproj_route_kernel(att_ref, ret_ref, cnv_ref, w_ref, x_ref, g1_ref, nw_ref, sc_ref, sh_ref, rcat_ref,
                          xo_ref, h_ref, rt_ref):
    y = (jnp.dot(att_ref[...], w_ref[0:D_ATT, :], preferred_element_type=F32)
         + jnp.dot(ret_ref[...], w_ref[D_ATT:D_ATT + D_RET, :], preferred_element_type=F32)
         + jnp.dot(cnv_ref[...], w_ref[D_ATT + D_RET:, :], preferred_element_type=F32))
    xn = x_ref[...] + g1_ref[...] * y
    xo_ref[...] = xn
    h = _modulated_rms(xn, nw_ref[...], sc_ref[...], sh_ref[...])
    h_ref[...] = _pack_bf16_pairs(h)
    hi = h.astype(BF16)
    lo = (h - hi.astype(F32)).astype(BF16)
    tm = h.shape[0]
    r = jnp.dot(jnp.concatenate([hi, lo], axis=0), rcat_ref[...], preferred_element_type=F32)
    logits = (r[0:tm, 0:LANES] + r[0:tm, LANES:]) + (r[tm:, 0:LANES] + r[tm:, LANES:])
    lane = lax.broadcasted_iota(jnp.int32, logits.shape, 1).astype(F32)
    logits = jnp.where(lane < N_EXPERTS, logits, NEG_INF)
    m1 = jnp.max(logits, axis=-1, keepdims=True)
    i1 = jnp.min(jnp.where(logits == m1, lane, float(LANES)), axis=-1, keepdims=True)
    rest = jnp.where(lane == i1, NEG_INF, logits)
    m2 = jnp.max(rest, axis=-1, keepdims=True)
    i2 = jnp.min(jnp.where(rest == m2, lane, float(LANES)), axis=-1, keepdims=True)
    e2 = jnp.exp(m2 - m1)
    w1 = 1.0 / (1.0 + e2)
    w2 = e2 / (1.0 + e2)
    rt_ref[...] = jnp.where(lane == 0.0, i1,
                            jnp.where(lane == 1.0, i2,
                                      jnp.where(lane == 2.0, w1, jnp.where(lane == 3.0, w2, 0.0))))


def _outproj_route(att, ret, cnv, w_bf, x, g1, nw, sc, sh, router):
    b, s, d = x.shape
    tm = _pick_tile(s, 1024)
    tok = lambda n: pl.BlockSpec((None, tm, n), lambda bi, i: (bi, i, 0))
    per_b = pl.BlockSpec((None, 1, d), lambda bi, i: (bi, 0, 0))
    rpad = jnp.zeros((d, LANES), F32).at[:, :N_EXPERTS].set(router)
    rhi = rpad.astype(BF16)
    rlo = (rpad - rhi.astype(F32)).astype(BF16)
    return pl.pallas_call(
        _outproj_route_kernel,
        out_shape=(jax.ShapeDtypeStruct((b, s, d), F32), jax.ShapeDtypeStruct((b, s, d // 2), jnp.int32),
                   jax.ShapeDtypeStruct((b, s, LANES), F32)),
        grid=(b, s // tm),
        in_specs=[tok(D_ATT), tok(D_RET), tok(CONV_CH), pl.BlockSpec(w_bf.shape, lambda bi, i: (0, 0)),
                  tok(d), per_b, pl.BlockSpec((1, d), lambda bi, i: (0, 0)), per_b, per_b,
                  pl.BlockSpec((d, 2 * LANES), lambda bi, i: (0, 0))],
        out_specs=(tok(d), tok(d // 2), tok(LANES)),
        compiler_params=_params("parallel", "parallel"),
        name="outproj_route",
    )(att, ret, cnv, w_bf, x, g1, nw, sc, sh, jnp.concatenate([rhi, rlo], axis=1))


def _swiglu_chunks(h, wg_ref, wu_ref, wd_ref, sub):
    total = None
    for c0 in range(0, wg_ref.shape[1], sub):
        cs = slice(c0, min(c0 + sub, wg_ref.shape[1]))
        gate = jnp.dot(h, wg_ref[:, cs].astype(BF16), preferred_element_type=F32)
        up = jnp.dot(h, wu_ref[:, cs].astype(BF16), preferred_element_type=F32)
        act = (gate * _sigmoid(gate) * up).astype(BF16)
        part = jnp.dot(act, wd_ref[cs, :].astype(BF16), preferred_element_type=F32)
        total = part if total is None else total + part
    return total


def _mixer_ffn_kernel(att_ref, ret_ref, cnv_ref, wo_ref, x_ref, g1_ref, nw_ref, sc_ref, sh_ref,
                      wg_ref, wu_ref, wd_ref, g2_ref, o_ref):
    y = (jnp.dot(att_ref[...], wo_ref[0:D_ATT, :], preferred_element_type=F32)
         + jnp.dot(ret_ref[...], wo_ref[D_ATT:D_ATT + D_RET, :], preferred_element_type=F32)
         + jnp.dot(cnv_ref[...], wo_ref[D_ATT + D_RET:, :], preferred_element_type=F32))
    xn = x_ref[...] + g1_ref[...] * y
    h = _modulated_rms(xn, nw_ref[...], sc_ref[...], sh_ref[...]).astype(BF16)
    o_ref[...] = xn + g2_ref[...] * _swiglu_chunks(h, wg_ref, wu_ref, wd_ref, FFN_SUB)


def _mixer_ffn(att, ret, cnv, wo, x, g1, nw, sc, sh, wg, wu, wd, g2):
    b, s, d = x.shape
    f = wg.shape[1]
    tm = _pick_tile(s, 512)
    tok = lambda n: pl.BlockSpec((None, tm, n), lambda bi, i: (bi, i, 0))
    per_b = pl.BlockSpec((None, 1, d), lambda bi, i: (bi, 0, 0))
    res = lambda shape: pl.BlockSpec(shape, lambda bi, i: (0, 0), pipeline_mode=pl.Buffered(1))
    return pl.pallas_call(
        _mixer_ffn_kernel,
        out_shape=jax.ShapeDtypeStruct((b, s, d), F32),
        grid=(b, s // tm),
        in_specs=[tok(D_ATT), tok(D_RET), tok(CONV_CH), res(wo.shape), tok(d), per_b,
                  pl.BlockSpec((1, d), lambda bi, i: (0, 0)), per_b, per_b,
                  res((d, f)), res((d, f)), res((f, d)), per_b],
        out_specs=tok(d),
        compiler_params=_params("parallel", "parallel"),
        name="mixer_ffn_dense",
    )(att, ret, cnv, wo, x, g1, nw, sc, sh, wg, wu, wd, g2)


def _gather_rows(idx, src):
    n = idx.shape[0]
    d = src.shape[1]
    mesh = plsc.VectorSubcoreMesh(core_axis_name="core", subcore_axis_name="subcore")

    @pl.kernel(out_type=jax.ShapeDtypeStruct((n, d), src.dtype), mesh=mesh, name="moe_gather")
    def gather(src_hbm, idx_hbm, out_hbm):
        def body(idx_vmem, out_vmem):
            pltpu.sync_copy(src_hbm.at[idx_vmem.at[0, pl.ds(0, SC_WINDOW)]], out_vmem)

        pltpu.emit_pipeline(
            body,
            grid=(n // SC_WINDOW,),
            in_specs=[pl.BlockSpec((1, LANES), lambda i: (i, 0))],
            out_specs=[pl.BlockSpec((SC_WINDOW, d), lambda i: (i, 0))],
            core_axis_name=("core", "subcore"),
            dimension_semantics=(pltpu.PARALLEL,),
        )(idx_hbm, out_hbm)

    idx_rows = jnp.pad(idx.reshape(n // SC_WINDOW, SC_WINDOW), ((0, 0), (0, LANES - SC_WINDOW)))
    return gather(src, idx_rows)


def _invert_rows(dest, n_rows):
    n = dest.shape[0]
    n_tok = n // 2
    mesh = plsc.VectorSubcoreMesh(core_axis_name="core", subcore_axis_name="subcore")

    @pl.kernel(out_type=jax.ShapeDtypeStruct((n_rows,), jnp.int32), mesh=mesh, name="moe_row_src",
               scratch_types=[pltpu.VMEM((n,), jnp.int32), pltpu.VMEM((n_rows,), jnp.int32)],
               compiler_params=pltpu.CompilerParams(needs_layout_passes=False))
    def invert(dest_hbm, out_hbm, dest_vmem, rows_vmem):
        @pl.when((lax.axis_index("core") == 0) & (lax.axis_index("subcore") == 0))
        def _():
            pltpu.sync_copy(dest_hbm, dest_vmem)

            @pl.loop(0, n_rows, step=SC_LANES)
            def _(r):
                rows_vmem[pl.ds(r, SC_LANES)] = lax.rem(lax.iota(jnp.int32, SC_LANES) + r, n_tok)

            @pl.loop(0, n, step=SC_LANES)
            def _(a):
                tok = lax.iota(jnp.int32, SC_LANES) + a
                tok = jnp.where(tok >= n_tok, tok - n_tok, tok)
                plsc.store_scatter(rows_vmem, [dest_vmem[pl.ds(a, SC_LANES)]], tok)

            pltpu.sync_copy(rows_vmem, out_hbm)

    return invert(dest)


def _moe_kernel(te_ref, nu_ref, x_ref, wg_ref, wu_ref, wd_ref, *rest, tile0, nj):
    o_ref, xb, acc = rest[-3:]
    t = pl.program_id(0) + tile0
    j = pl.program_id(1)
    used = t < nu_ref[0]

    def partial_sum():
        return _swiglu_chunks(xb[...], wg_ref, wu_ref, wd_ref, MOE_SUB)

    @pl.when(used & (j == 0))
    def _():
        lo, hi = _unpack_bf16_pairs(x_ref[...])
        half = lo.shape[1]
        xb[:, 0:half] = lo.astype(BF16)
        xb[:, half:] = hi.astype(BF16)
        if nj == 1:
            o_ref[...] = _pack_bf16_pairs(partial_sum())
        else:
            acc[...] = partial_sum()

    if nj > 2:
        @pl.when(used & (j > 0) & (j < nj - 1))
        def _():
            acc[...] += partial_sum()

    if nj > 1:
        @pl.when(used & (j == nj - 1))
        def _():
            o_ref[...] = _pack_bf16_pairs(acc[...] + partial_sum())

    @pl.when(jnp.logical_not(used) & (j == nj - 1))
    def _():
        o_ref[...] = jnp.zeros_like(o_ref)


def _moe_grouped(tile_expert, n_used, xs, wg, wu, wd, y_prev, tile0, n_rows):
    p, dp = xs.shape
    d = 2 * dp
    f = wg.shape[2]
    tm = MOE_TM
    tf = MOE_TF
    nj = f // tf

    def jj(t, j, te, nu):
        return jnp.where(t + tile0 < nu[0], j, nj - 1)

    in_specs = [pl.BlockSpec((tm, dp), lambda t, j, te, nu: (t, 0)),
                pl.BlockSpec((None, d, tf), lambda t, j, te, nu: (te[t + tile0], 0, jj(t, j, te, nu))),
                pl.BlockSpec((None, d, tf), lambda t, j, te, nu: (te[t + tile0], 0, jj(t, j, te, nu))),
                pl.BlockSpec((None, tf, d), lambda t, j, te, nu: (te[t + tile0], jj(t, j, te, nu), 0))]
    args = [tile_expert, n_used, xs, wg, wu, wd]
    aliases = {}
    if y_prev is not None:
        in_specs.append(pl.BlockSpec(memory_space=pl.ANY))
        args.append(y_prev)
        aliases = {len(args) - 1: 0}
    return pl.pallas_call(
        functools.partial(_moe_kernel, tile0=tile0, nj=nj),
        out_shape=jax.ShapeDtypeStruct((n_rows, dp), jnp.int32),
        grid_spec=pltpu.PrefetchScalarGridSpec(
            num_scalar_prefetch=2,
            grid=(p // tm, nj),
            in_specs=in_specs,
            out_specs=pl.BlockSpec((tm, dp), lambda t, j, te, nu: (t + tile0, 0)),
            scratch_shapes=[pltpu.VMEM((tm, d), BF16), pltpu.VMEM((tm, d), F32)]),
        input_output_aliases=aliases,
        compiler_params=_params("arbitrary", "arbitrary"),
        name="moe_grouped",
    )(*args)


def _combine_kernel(y1_ref, y2_ref, x_ref, g2_ref, rt_ref, o_ref):
    rt = rt_ref[...]
    w1, w2 = rt[:, 2:3], rt[:, 3:4]
    half = y1_ref.shape[1]
    for k, (a, c) in enumerate(zip(_unpack_bf16_pairs(y1_ref[...]), _unpack_bf16_pairs(y2_ref[...]))):
        cs = slice(k * half, (k + 1) * half)
        o_ref[:, cs] = x_ref[:, cs] + g2_ref[:, cs] * (w1 * a + w2 * c)


def _moe_combine(yg, x, g2, route):
    b, s, d = x.shape
    tm = _pick_tile(s, 512)
    tok = lambda n: pl.BlockSpec((None, tm, n), lambda bi, i: (bi, i, 0))
    return pl.pallas_call(
        _combine_kernel,
        out_shape=jax.ShapeDtypeStruct((b, s, d), F32),
        grid=(b, s // tm),
        in_specs=[tok(d // 2), pl.BlockSpec((None, tm, d // 2), lambda bi, i: (b + bi, i, 0)), tok(d),
                  pl.BlockSpec((None, 1, d), lambda bi, i: (bi, 0, 0)), tok(LANES)],
        out_specs=tok(d),
        compiler_params=_params("parallel", "parallel"),
        name="moe_combine",
    )(yg, yg, x, g2, route)


def _moe(h, route, x, g2, wg, wu, wd):
    b, s, d = x.shape
    n_tok = b * s
    tm = MOE_TM
    rt = route.reshape(n_tok, LANES)
    flat_e = jnp.concatenate([rt[:, 0], rt[:, 1]]).astype(jnp.int32)
    onehot = (flat_e[:, None] == jnp.arange(N_EXPERTS, dtype=jnp.int32)[None, :]).astype(jnp.int32)
    csum = jnp.cumsum(onehot, axis=0)
    rank = jnp.sum(csum * onehot, axis=1) - 1
    counts = csum[-1]
    tiles_e = (counts + tm - 1) // tm
    tiles_cum = jnp.cumsum(tiles_e)
    row_start = (tiles_cum - tiles_e) * tm
    dest = jnp.sum(onehot * row_start[None, :], axis=1) + rank
    n_tiles = 2 * n_tok // tm + N_EXPERTS
    tile_ids = jnp.arange(n_tiles, dtype=jnp.int32)
    tile_expert = jnp.sum((tile_ids[:, None] >= tiles_cum[None, :]).astype(jnp.int32), axis=1)
    last_e = jnp.max(jnp.where(tiles_e > 0, jnp.arange(N_EXPERTS, dtype=jnp.int32), 0))
    tile_expert = jnp.minimum(tile_expert, last_e).astype(jnp.int32)
    n_used = tiles_cum[-1:].astype(jnp.int32)
    dest = dest.astype(jnp.int32)
    row_src = _invert_rows(dest, n_tiles * tm)
    n_chunks = max(c for c in range(1, MOE_CHUNKS + 1) if n_tiles % c == 0)
    tiles_c = n_tiles // n_chunks
    h_flat = h.reshape(n_tok, d // 2)
    xs = [_gather_rows(row_src[c * tiles_c * tm:(c + 1) * tiles_c * tm], h_flat) for c in range(n_chunks)]
    y = None
    for c in range(n_chunks):
        y = _moe_grouped(tile_expert, n_used, xs[c], wg, wu, wd, y, c * tiles_c, n_tiles * tm)
    yg = _gather_rows(dest, y).reshape(2 * b, s, d // 2)
    return _moe_combine(yg, x, g2, route)


def _rope_tables(n):
    rows = n // GRID_W
    r = np.repeat(np.arange(rows), GRID_W).astype(np.float64)
    col = np.tile(np.arange(GRID_W), rows).astype(np.float64)
    freqs = ROPE_BASE ** (-np.arange(ROPE_FREQS, dtype=np.float64) / ROPE_FREQS)
    ang = np.stack([r[:, None] * freqs, col[:, None] * freqs], axis=1)
    ang = np.repeat(ang[:, :, None, :], 2, axis=2).reshape(n, HEAD_DIM)
    ang = np.tile(ang, (1, LANES // HEAD_DIM))
    cos, sin = np.cos(ang), np.sin(ang)
    first_half = (np.arange(LANES) % (2 * ROPE_FREQS)) < ROPE_FREQS
    to_dev = lambda t: jnp.asarray(t.astype(np.float32))
    return to_dev(cos), to_dev(np.where(first_half, -sin, 0.0)), to_dev(np.where(first_half, 0.0, sin))


def _lane_rows(lg):
    return jnp.repeat(lg.astype(F32), HEAD_DIM).reshape(RET_HEADS // 2, 1, LANES)


def kernel(x, c, ctx, c_ctx, ada_w, ada_b, norm1_w, norm2_w, w_in, w_out, q_norm_w, k_norm_w,
           attn_sink, ret_decay_f, ret_decay_b, ret_gn_w, conv_w, conv_b, conv_ln_w, conv_ln_b,
           ffn_w_gate, ffn_w_up, ffn_w_down, router_w, moe_w_gate, moe_w_up, moe_w_down):
    b, n, d = x.shape
    n_ctx = ctx.shape[1]
    depth = ada_w.shape[0]
    cond = jnp.zeros((SUBLANES, d), F32).at[0:b].set(c).at[b].set(c_ctx)
    mods = _adaln(cond, ada_w, ada_b).reshape(depth, SUBLANES, 6, d)
    cos, sa, sb = _rope_tables(n)
    ones_c = jnp.ones((n_ctx, LANES), F32)
    zeros_c = jnp.zeros((n_ctx, LANES), F32)
    zero_state = jnp.zeros((b, RET_HEADS // 2, LANES, LANES), F32)
    row = lambda v: v.reshape(1, -1)
    slab_rows = b * (n // ATT_BLOCK // ATT_QBLOCKS) * BF16_SUBLANES
    riders = {l: {} for l in range(depth)}

    def ride(l, key, w):
        w2 = w.reshape(-1, w.shape[-1])
        if 0 <= l < depth and (n // ATT_BLOCK) % ATT_QBLOCKS == 0 and w2.shape[0] % slab_rows == 0:
            riders[l][key] = w2

    for j, w in enumerate((moe_w_gate, moe_w_up, moe_w_down)):
        ride(j * min(depth, 2) // 3, ("moe", j), w)
    for l in range(depth):
        if l % 2 == 0:
            ride(l, ("ffn", l, 0), ffn_w_gate[l // 2])
            ride(l, ("ffn", l, 1), ffn_w_up[l // 2])
            ride(l, ("ffn", l, 2), ffn_w_down[l // 2])
        ride(l - 1, ("w_in", l), w_in[l])
        ride(l - 1, ("w_out", l), w_out[l])
    rounded = {}

    def bf16_of(key, w):
        return rounded[key].reshape(w.shape) if key in rounded else w.astype(BF16)

    for l in range(depth):
        last = l == depth - 1
        m_lat = [mods[l, 0:b, k][:, None, :] for k in range(6)]
        m_ctx = [jnp.broadcast_to(mods[l, b, k][None, None, :], (b, 1, d)) for k in range(6)]
        w_in_bf = bf16_of(("w_in", l), w_in[l])
        w_out_bf = bf16_of(("w_out", l), w_out[l])
        qw = row(jnp.tile(q_norm_w[l], LANES // HEAD_DIM))
        kw = row(jnp.tile(k_norm_w[l], LANES // HEAD_DIM))
        lgf = _lane_rows(jax.nn.log_sigmoid(ret_decay_f[l].astype(F32)))
        lgb = _lane_rows(jax.nn.log_sigmoid(ret_decay_b[l].astype(F32)))
        sink_tab = jnp.repeat(attn_sink[l].astype(F32), ATT_BLOCK).reshape(ATT_Q_HEADS // 2, 2 * ATT_BLOCK)

        q, k, v, rk, rv, rq, rg, cv = _inproj(x, row(norm1_w[l]), m_lat[1], m_lat[0], w_in_bf,
                                               cos, sa, sb, qw, kw)
        qc, kc, vc, rkc, rvc, rqc, rgc, cvc = _inproj(ctx, row(norm1_w[l]), m_ctx[1], m_ctx[0], w_in_bf,
                                                       ones_c, zeros_c, zeros_c, qw, kw)
        rf_c, s_f = _ret_states(lgf, rkc, rvc, zero_state, reverse=False)
        rb_c, s_b = _ret_states(lgb, rkc, rvc, zero_state, reverse=True)
        rf, _ = _ret_states(lgf, rk, rv, s_f, reverse=False)
        rb, _ = _ret_states(lgb, rk, rv, s_b, reverse=True)

        att, cast_out = _attention(q, k, v, kc, vc, sink_tab, window=True, cast=list(riders[l].values()))
        rounded.update(zip(riders[l].keys(), cast_out))
        ret = _ret_out(lgf, lgb, rq, rk, rv, rg, rf, rb, row(ret_gn_w[l]))
        cnv = _conv(cv, conv_w[l], row(conv_b[l]), row(conv_ln_w[l]), row(conv_ln_b[l]))

        if l % 2 == 0:
            i = l // 2
            wg, wu, wd = (bf16_of(("ffn", l, j), w[i]) for j, w in enumerate((ffn_w_gate, ffn_w_up, ffn_w_down)))
            x_new = _mixer_ffn(att, ret, cnv, w_out_bf, x, m_lat[2], row(norm2_w[l]), m_lat[4], m_lat[3],
                               wg, wu, wd, m_lat[5])
        else:
            i = l // 2
            wg, wu, wd = (bf16_of(("moe", j), w)[i] for j, w in enumerate((moe_w_gate, moe_w_up, moe_w_down)))
            x_mid, h2, route = _outproj_route(att, ret, cnv, w_out_bf, x, m_lat[2], row(norm2_w[l]),
                                              m_lat[4], m_lat[3], router_w[i])
            x_new = _moe(h2, route, x_mid, m_lat[5], wg, wu, wd)

        if not last:
            att_c, _ = _attention(qc, None, None, kc, vc, sink_tab, window=False)
            ret_c = _ret_out(lgf, lgb, rqc, rkc, rvc, rgc, rf_c, rb_c, row(ret_gn_w[l]))
            cnv_c = _conv(cvc, conv_w[l], row(conv_b[l]), row(conv_ln_w[l]), row(conv_ln_b[l]))
            if l % 2 == 0:
                ctx = _mixer_ffn(att_c, ret_c, cnv_c, w_out_bf, ctx, m_ctx[2], row(norm2_w[l]), m_ctx[4],
                                 m_ctx[3], wg, wu, wd, m_ctx[5])
            else:
                c_mid, h2c, route_c = _outproj_route(att_c, ret_c, cnv_c, w_out_bf, ctx, m_ctx[2],
                                                     row(norm2_w[l]), m_ctx[4], m_ctx[3], router_w[i])
                ctx = _moe(h2c, route_c, c_mid, m_ctx[5], wg, wu, wd)
        x = x_new
    return x
```

```python
import functools

import jax
import jax.numpy as jnp
import numpy as np
from jax import lax
from jax.experimental import pallas as pl
from jax.experimental.pallas import tpu as pltpu
from jax.experimental.pallas import tpu_sc as plsc

F32 = jnp.float32
BF16 = jnp.bfloat16

GRID_W = 64
HEAD_DIM = 64
ATT_Q_HEADS = 8
ATT_KV_HEADS = 2
ATT_WINDOW = 128
ATT_BLOCK = 128
RET_HEADS = 4
RET_CHUNK = 128
RET_K_SCALE = HEAD_DIM ** -0.5
ATT_SCALE = HEAD_DIM ** -0.5
CONV_CH = 256
CONV_WIDTH = 31
ROPE_BASE = 10000.0
ROPE_FREQS = HEAD_DIM // 4
D_ATT = ATT_Q_HEADS * HEAD_DIM
D_RET = RET_HEADS * HEAD_DIM
ATT_KV_W = ATT_KV_HEADS * HEAD_DIM
C_ATT_K = 0
C_ATT_V = C_ATT_K + ATT_KV_W
C_RET_K = C_ATT_V + ATT_KV_W
C_RET_V = C_RET_K + D_RET
C_ATT_Q = C_RET_V + D_RET
C_RET_Q = C_ATT_Q + D_ATT
C_RET_G = C_RET_Q + D_RET
C_CONV = C_RET_G + D_RET
N_EXPERTS = 8
EPS = 1e-6
NEG_INF = -1e30

LANES = 128
SUBLANES = 8
BF16_SUBLANES = 16
VMEM_LIMIT = 48 * 1024 * 1024
CONV_HALO = 16
CONV_SUB = 32
CONV_CHAINS = 4
ATT_QBLOCKS = 4
RET_UNROLL = 4
MOE_TM = 512
MOE_CHUNKS = 3
MOE_SUB = 512
FFN_SUB = 512
MOE_TF = 1792
SC_LANES = 16
SC_WINDOW = 64


def _params(*sem):
    return pltpu.CompilerParams(dimension_semantics=sem, vmem_limit_bytes=VMEM_LIMIT)


def _sigmoid(x):
    return 1.0 / (1.0 + jnp.exp(-x))


def _pack_bf16_pairs(v):
    c = v.shape[1] // 2
    bits = pltpu.bitcast(v.astype(BF16).astype(F32), jnp.uint32)
    packed = (bits[:, c:] & jnp.uint32(0xFFFF0000)) | (bits[:, :c] >> 16)
    return pltpu.bitcast(packed, jnp.int32)


def _unpack_bf16_pairs(p):
    bits = pltpu.bitcast(p, jnp.uint32)
    return pltpu.bitcast(bits << 16, F32), pltpu.bitcast(bits & jnp.uint32(0xFFFF0000), F32)


def _pick_tile(n, pref):
    t = min(n, pref)
    assert n % t == 0, (n, t)
    return t


def _adaln_kernel(c_ref, w_ref, b_ref, o_ref):
    c = c_ref[...]
    s = c * _sigmoid(c)
    w = w_ref[...]
    s_hi, w_hi = s.astype(BF16), w.astype(BF16)
    s_lo = (s - s_hi.astype(F32)).astype(BF16)
    w_lo = (w - w_hi.astype(F32)).astype(BF16)
    dot = functools.partial(jnp.dot, preferred_element_type=F32)
    o_ref[...] = (dot(s_hi, w_hi) + dot(s_lo, w_hi)) + dot(s_hi, w_lo) + b_ref[...]


def _adaln(cond, ada_w, ada_b):
    depth, d, n = ada_w.shape
    tn = _pick_tile(n, 3072)
    return pl.pallas_call(
        _adaln_kernel,
        out_shape=jax.ShapeDtypeStruct((depth, cond.shape[0], n), F32),
        grid=(depth, n // tn),
        in_specs=[pl.BlockSpec(cond.shape, lambda l, j: (0, 0)),
                  pl.BlockSpec((None, d, tn), lambda l, j: (l, 0, j)),
                  pl.BlockSpec((None, 1, tn), lambda l, j: (l, 0, j))],
        out_specs=pl.BlockSpec((None, cond.shape[0], tn), lambda l, j: (l, 0, j)),
        compiler_params=_params("parallel", "parallel"),
        name="adaln",
    )(cond, ada_w, ada_b.reshape(depth, 1, n))


def _modulated_rms(x, nw, sc, sh):
    ms = jnp.mean(x * x, axis=-1, keepdims=True)
    return (x * lax.rsqrt(ms + EPS) * nw) * (1.0 + sc) + sh


def _head_group_matrix():
    r = lax.broadcasted_iota(jnp.int32, (2 * LANES, 2 * LANES), 0) // HEAD_DIM
    c = lax.broadcasted_iota(jnp.int32, (2 * LANES, 2 * LANES), 1) // HEAD_DIM
    return jnp.where(r == c, 1.0, 0.0).astype(BF16)


def _per_head_sum(v, grp):
    hi = v.astype(BF16)
    lo = (v - hi.astype(F32)).astype(BF16)
    r = jnp.dot(jnp.concatenate([hi, lo], axis=1), grp, preferred_element_type=F32)
    return r[:, 0:LANES] + r[:, LANES:]


def _dup_halves(t):
    sw = pltpu.roll(t, HEAD_DIM, 1)
    lo = lax.broadcasted_iota(jnp.int32, t.shape, 1) < HEAD_DIM
    return jnp.where(lo, t, sw), jnp.where(lo, sw, t)


def _inproj_kernel(x_ref, nw_ref, sc_ref, sh_ref, w_ref, cos_ref, sa_ref, sb_ref, qw_ref, kw_ref,
                   q_ref, k_ref, v_ref, rk_ref, rv_ref, rq_ref, rg_ref, cv_ref):
    hb = _modulated_rms(x_ref[...], nw_ref[...], sc_ref[...], sh_ref[...]).astype(BF16)

    def proj(c0, n):
        return jnp.dot(hb, w_ref[:, c0:c0 + n], preferred_element_type=F32)

    grp = _head_group_matrix()
    cos, sa, sb = cos_ref[...], sa_ref[...], sb_ref[...]

    def norm_rope(p, wrow):
        y = p * lax.rsqrt(_per_head_sum(p * p, grp) * (1.0 / HEAD_DIM) + EPS) * wrow
        return (y * cos + pltpu.roll(y, LANES - ROPE_FREQS, 1) * sa
                + pltpu.roll(y, ROPE_FREQS, 1) * sb)

    def put_kv(kv):
        k0, k1 = _dup_halves(norm_rope(kv[:, 0:ATT_KV_W], kw_ref[...]))
        k_ref[:, 0:LANES] = k0.astype(BF16)
        k_ref[:, LANES:2 * LANES] = k1.astype(BF16)
        v_ref[...] = kv[:, ATT_KV_W:].T.astype(BF16)

    def put_q(qall):
        qw = qw_ref[...] * ATT_SCALE
        for j in range(D_ATT // LANES):
            q_ref[:, j * LANES:(j + 1) * LANES] = norm_rope(qall[:, j * LANES:(j + 1) * LANES], qw).astype(BF16)

    def put_rk(p):
        rk_ref[...] = (p * RET_K_SCALE).astype(BF16)

    def put_rv(p):
        rv_ref[...] = p.astype(BF16)

    def put_rq(p):
        rq_ref[...] = p.astype(BF16)

    def put_rg(p):
        rg_ref[...] = p

    def put_conv(p):
        cv_ref[...] = p[:, 0:CONV_CH] * _sigmoid(p[:, CONV_CH:])

    units = [(C_ATT_K, 2 * ATT_KV_W, put_kv), (C_ATT_Q, D_ATT, put_q), (C_RET_K, D_RET, put_rk),
             (C_RET_V, D_RET, put_rv), (C_RET_Q, D_RET, put_rq), (C_RET_G, D_RET, put_rg),
             (C_CONV, 2 * CONV_CH, put_conv)]
    ahead = 2
    pending = [proj(c0, n) for c0, n, _ in units[:ahead]]
    for n, (_, _, put) in enumerate(units):
        if n + ahead < len(units):
            pending.append(proj(*units[n + ahead][:2]))
        put(pending.pop(0))


def _inproj(x, nw, sc, sh, w_bf, cos, sa, sb, qw, kw):
    b, s, d = x.shape
    tm = _pick_tile(s, 1024)
    row = lambda n: pl.BlockSpec((1, n), lambda bi, i: (0, 0))
    per_b = pl.BlockSpec((None, 1, d), lambda bi, i: (bi, 0, 0))
    tab = pl.BlockSpec((tm, LANES), lambda bi, i: (i, 0))
    tok = lambda n: pl.BlockSpec((None, tm, n), lambda bi, i: (bi, i, 0))
    shp = lambda n, dt: jax.ShapeDtypeStruct((b, s, n), dt)
    return pl.pallas_call(
        _inproj_kernel,
        out_shape=(shp(D_ATT, BF16), shp(2 * LANES, BF16), jax.ShapeDtypeStruct((b, ATT_KV_W, s), BF16),
                   shp(D_RET, BF16), shp(D_RET, BF16), shp(D_RET, BF16), shp(D_RET, F32),
                   shp(CONV_CH, F32)),
        grid=(b, s // tm),
        in_specs=[tok(d), row(d), per_b, per_b,
                  pl.BlockSpec(w_bf.shape, lambda bi, i: (0, 0)),
                  tab, tab, tab, row(LANES), row(LANES)],
        out_specs=(tok(D_ATT), tok(2 * LANES), pl.BlockSpec((None, ATT_KV_W, tm), lambda bi, i: (bi, 0, i)),
                   tok(D_RET), tok(D_RET), tok(D_RET), tok(D_RET), tok(CONV_CH)),
        compiler_params=_params("parallel", "parallel"),
        name="inproj",
    )(x, nw, sc, sh, w_bf, cos, sa, sb, qw, kw)


def _attn_kernel(*refs, window, nq, n_cast):
    blk = ATT_BLOCK
    n_in = len(refs) - 1 - n_cast
    for src, dst in zip(refs[n_in - n_cast:n_in], refs[n_in + 1:]):
        dst[...] = src[...].astype(BF16)
    refs = refs[:n_in - n_cast] + refs[n_in:n_in + 1]
    if window:
        q_ref = refs[0]
        k_refs = refs[1:nq + 3]
        v_refs = refs[nq + 3:2 * nq + 5]
        kx_ref, vx_ref, sink_ref, o_ref = refs[2 * nq + 5:]
    else:
        q_ref, kx_ref, vx_ref, sink_ref, o_ref = refs
    n_ctx = kx_ref.shape[0]
    nk = 3 * blk + n_ctx if window else n_ctx
    if window:
        i = pl.program_id(1)
        last = nq * pl.num_programs(1) - 1
        key = lax.broadcasted_iota(jnp.int32, (blk, 2 * blk), 0)
        qry = lax.broadcasted_iota(jnp.int32, (blk, 2 * blk), 1) & (blk - 1)

        def band_masks(sub):
            off_prev = jnp.where(nq * i + sub > 0, 0, blk)
            off_next = jnp.where(nq * i + sub < last, 0, blk)
            return key >= qry + off_prev, key + off_next <= qry
    first_head = lax.broadcasted_iota(jnp.int32, (blk, LANES), 1) < HEAD_DIM
    ones = jnp.ones((2 * SUBLANES, nk), BF16)

    def scores(sub, pair):
        g = pair // 2
        gs = slice(g * LANES, (g + 1) * LANES)
        if window:
            kcat = jnp.concatenate([r[:, gs] for r in k_refs[sub:sub + 3]] + [kx_ref[:, gs]], axis=0)
        else:
            kcat = kx_ref[:, gs]
        qp = q_ref[sub * blk:(sub + 1) * blk, pair * LANES:(pair + 1) * LANES]
        zero = jnp.zeros_like(qp)
        w = jnp.concatenate([jnp.where(first_head, qp, zero), jnp.where(first_head, zero, qp)], axis=0)
        return lax.dot_general(kcat, w, (((1,), (1,)), ((), ())), preferred_element_type=F32)

    def masked_max(sub, pair, s):
        if window:
            mask_prev, mask_next = band_masks(sub)
            parts = [jnp.where(mask_prev, s[0:blk], NEG_INF), s[blk:2 * blk],
                     jnp.where(mask_next, s[2 * blk:3 * blk], NEG_INF), s[3 * blk:]]
        else:
            parts = [s]
        m = sink_ref[pair:pair + 1, :]
        for part in parts:
            m = jnp.maximum(m, jnp.max(part, axis=0, keepdims=True))
        return parts, m

    def softmax(sub, pair, parts_m):
        parts, m = parts_m
        p = jnp.concatenate([jnp.exp(part - m).astype(BF16) for part in parts], axis=0)
        return p, jnp.exp(sink_ref[pair:pair + 1, :] - m)

    def output(sub, pair, p, sink_p):
        g = pair // 2
        vs = slice(g * HEAD_DIM, (g + 1) * HEAD_DIM)
        if window:
            vt = jnp.concatenate([r[vs, :] for r in v_refs[sub:sub + 3]] + [vx_ref[vs, :]], axis=1)
        else:
            vt = vx_ref[vs, :]
        vaug = jnp.concatenate([vt, ones], axis=0)
        o = jnp.dot(vaug, p, preferred_element_type=F32)
        on = o[0:HEAD_DIM, :] * (1.0 / (o[HEAD_DIM:HEAD_DIM + 1, :] + sink_p))
        ot = jnp.concatenate([on[:, 0:blk], on[:, blk:2 * blk]], axis=0)
        o_ref[sub * blk:(sub + 1) * blk, pair * LANES:(pair + 1) * LANES] = ot.T.astype(BF16)

    units = [(sub, pair) for sub in range(nq) for pair in range(ATT_Q_HEADS // 2)]
    stages = (lambda sub, pair, _: scores(sub, pair), masked_max, softmax,
              lambda sub, pair, ps: output(sub, pair, *ps))
    vals = {}
    for step in range(len(units) + len(stages) - 1):
        for k, stage in enumerate(stages):
            u = step - k
            if 0 <= u < len(units):
                vals[u, k] = stage(*units[u], vals.pop((u, k - 1), None))


def _attention(q, k, vt, kx, vxt, sink_tab, window, cast=()):
    b, s, _ = q.shape
    blk = ATT_BLOCK
    assert ATT_WINDOW == blk, "the band masks assume a window of exactly one key block on each side"
    nb = s // blk
    nq = ATT_QBLOCKS if nb % ATT_QBLOCKS == 0 else 1
    n_ctx = kx.shape[1]
    qspec = pl.BlockSpec((None, nq * blk, D_ATT), lambda bi, i: (bi, i, 0))
    kctx = pl.BlockSpec((None, n_ctx, 2 * LANES), lambda bi, i: (bi, 0, 0))
    vctx = pl.BlockSpec((None, ATT_KV_W, n_ctx), lambda bi, i: (bi, 0, 0))
    snk = pl.BlockSpec(sink_tab.shape, lambda bi, i: (0, 0))
    if window:
        at = lambda off: (lambda i: jnp.clip(nq * i + off, 0, nb - 1))
        kspec = lambda f: pl.BlockSpec((None, blk, 2 * LANES), lambda bi, i: (bi, f(i), 0))
        vspec = lambda f: pl.BlockSpec((None, ATT_KV_W, blk), lambda bi, i: (bi, 0, f(i)))
        offs = range(-1, nq + 1)
        in_specs = ([qspec] + [kspec(at(o)) for o in offs] + [vspec(at(o)) for o in offs]
                    + [kctx, vctx, snk])
        args = (q,) + (k,) * (nq + 2) + (vt,) * (nq + 2) + (kx, vxt, sink_tab)
    else:
        in_specs = [qspec, kctx, vctx, snk]
        args = (q, kx, vxt, sink_tab)
    steps = nb // nq
    out_shape, out_specs = [jax.ShapeDtypeStruct((b, s, D_ATT), BF16)], [qspec]
    for w in cast:
        rows = w.shape[0] // (b * steps)
        assert rows * b * steps == w.shape[0] and rows % (2 * SUBLANES) == 0, w.shape
        slab = pl.BlockSpec((rows, w.shape[1]), lambda bi, i: (bi * steps + i, 0))
        in_specs.append(slab)
        out_specs.append(slab)
        out_shape.append(jax.ShapeDtypeStruct(w.shape, BF16))
    res = pl.pallas_call(
        functools.partial(_attn_kernel, window=window, nq=nq, n_cast=len(cast)),
        out_shape=tuple(out_shape),
        grid=(b, steps),
        in_specs=in_specs,
        out_specs=tuple(out_specs),
        compiler_params=_params("parallel", "parallel"),
        name="attention_window" if window else "attention_ctx",
    )(*args, *cast)
    return res[0], tuple(res[1:])


def _ret_state_kernel(lgl_ref, k_ref, v_ref, s0_ref, r_ref, fin_ref, s_scr, *, cb, reverse):
    c = pl.program_id(1)
    L = RET_CHUNK

    @pl.when(c == 0)
    def _():
        s_scr[...] = s0_ref[...]

    jj = lax.broadcasted_iota(jnp.int32, (L, LANES), 0).astype(F32)
    expo = jj if reverse else (L - 1.0) - jj
    same_head = (lax.broadcasted_iota(jnp.int32, (LANES, LANES), 0) // HEAD_DIM
                 == lax.broadcasted_iota(jnp.int32, (LANES, LANES), 1) // HEAD_DIM)
    for pr in range(RET_HEADS // 2):
        cs = slice(pr * LANES, (pr + 1) * LANES)
        lgl = lgl_ref[pr]
        kdec = jnp.exp(expo * lgl)
        cdec = jnp.exp(float(L) * lgl)
        state = s_scr[pr]
        for t in range(cb):
            cc = cb - 1 - t if reverse else t
            rows = slice(cc * L, (cc + 1) * L)
            r_ref[cc, pr] = state.astype(BF16)
            kd = k_ref[rows, cs].astype(F32) * kdec
            u = jnp.dot(kd.T.astype(BF16), v_ref[rows, cs], preferred_element_type=F32)
            state = cdec * state + jnp.where(same_head, u, 0.0)
        s_scr[pr] = state

    @pl.when(c == pl.num_programs(1) - 1)
    def _():
        fin_ref[...] = s_scr[...]


def _ret_states(lgl, rk, rv, s0, reverse):
    b, s, _ = rk.shape
    nc = s // RET_CHUNK
    cb = _pick_tile(nc, 16)
    nblk = nc // cb
    npair = RET_HEADS // 2
    blk_idx = (lambda c: nblk - 1 - c) if reverse else (lambda c: c)
    tok = pl.BlockSpec((None, cb * RET_CHUNK, D_RET), lambda bi, c: (bi, blk_idx(c), 0))
    st = pl.BlockSpec((None, npair, LANES, LANES), lambda bi, c: (bi, 0, 0, 0))
    return pl.pallas_call(
        functools.partial(_ret_state_kernel, cb=cb, reverse=reverse),
        out_shape=(jax.ShapeDtypeStruct((b, nc, npair, LANES, LANES), BF16),
                   jax.ShapeDtypeStruct((b, npair, LANES, LANES), F32)),
        grid=(b, nblk),
        in_specs=[pl.BlockSpec(lgl.shape, lambda bi, c: (0, 0, 0)), tok, tok, st],
        out_specs=(pl.BlockSpec((None, cb, npair, LANES, LANES),
                                lambda bi, c: (bi, blk_idx(c), 0, 0, 0)), st),
        scratch_shapes=[pltpu.VMEM((npair, LANES, LANES), F32)],
        compiler_params=_params("parallel", "arbitrary"),
        name="ret_state_bwd" if reverse else "ret_state_fwd",
    )(lgl, rk, rv, s0)


def _ret_out_kernel(lgf_ref, lgb_ref, q_ref, k_ref, v_ref, g_ref, rf_ref, rb_ref, gnw_ref, o_ref, *, cb):
    L = RET_CHUNK
    i0 = lax.broadcasted_iota(jnp.int32, (L, LANES), 0).astype(F32)
    i1 = lax.broadcasted_iota(jnp.int32, (L, LANES), 1).astype(F32)
    diff = i0 - i1
    lo = lax.broadcasted_iota(jnp.int32, (L, LANES), 1) < HEAD_DIM
    grp = _head_group_matrix()
    inv = 1.0 / HEAD_DIM
    tabs = []
    for pr in range(RET_HEADS // 2):
        lgf, lgb = lgf_ref[pr], lgb_ref[pr]
        dmat = [jnp.where(diff >= 0.0,
                          jnp.exp(jnp.maximum(diff, 0.0) * lgf[:, a:a + 1]),
                          jnp.exp(jnp.maximum(-diff, 0.0) * lgb[:, a:a + 1]))
                for a in (0, HEAD_DIM)]
        tabs.append((jnp.exp((i0 + 1.0) * lgf), jnp.exp((float(L) - i0) * lgb),
                     jnp.concatenate(dmat, axis=0), gnw_ref[:, pr * LANES:(pr + 1) * LANES]))

    def decayed_scores(t, pr):
        rows = pl.ds(pl.multiple_of(t * L, L), L)
        cs = slice(pr * LANES, (pr + 1) * LANES)
        qp, kp = q_ref[rows, cs], k_ref[rows, cs]
        zero = jnp.zeros_like(kp)
        qz = jnp.concatenate([jnp.where(lo, qp, zero), jnp.where(lo, zero, qp)], axis=0)
        return lax.dot_general(qz, kp, (((1,), (1,)), ((), ())), preferred_element_type=F32) * tabs[pr][2]

    def mix(t, pr, a):
        rows = pl.ds(pl.multiple_of(t * L, L), L)
        cs = slice(pr * LANES, (pr + 1) * LANES)
        xif, xib, _, _ = tabs[pr]
        vp = v_ref[rows, cs]
        zero = jnp.zeros_like(vp)
        qf = q_ref[rows, cs].astype(F32)
        lhs = jnp.concatenate([a[0:L].astype(BF16), a[L:].astype(BF16),
                               (qf * xif).astype(BF16), (qf * xib).astype(BF16)], axis=1)
        rhs = jnp.concatenate([jnp.where(lo, vp, zero), jnp.where(lo, zero, vp),
                               rf_ref[t, pr], rb_ref[t, pr]], axis=0)
        return jnp.dot(lhs, rhs, preferred_element_type=F32)

    def centre(t, pr, acc):
        return acc - _per_head_sum(acc, grp) * inv

    def readout(t, pr, dl):
        rows = pl.ds(pl.multiple_of(t * L, L), L)
        cs = slice(pr * LANES, (pr + 1) * LANES)
        var = _per_head_sum(dl * dl, grp) * inv
        y = dl * lax.rsqrt(var + EPS) * tabs[pr][3]
        gt = g_ref[rows, cs]
        o_ref[rows, cs] = (gt * _sigmoid(gt) * y).astype(BF16)

    unroll = min(RET_UNROLL, cb)
    stages = (lambda t, pr, _: decayed_scores(t, pr), mix, centre, readout)

    def group(gi, carry):
        units = [(gi * unroll + u, pr) for u in range(unroll) for pr in range(RET_HEADS // 2)]
        vals = {}
        for step in range(len(units) + len(stages) - 1):
            for k, stage in enumerate(stages):
                u = step - k
                if 0 <= u < len(units):
                    vals[u, k] = stage(*units[u], vals.pop((u, k - 1), None))
        return carry

    lax.fori_loop(0, cb // unroll, group, 0)


def _ret_out(lgf, lgb, rq, rk, rv, rg, rf, rb, gnw):
    b, s, _ = rq.shape
    nc = s // RET_CHUNK
    cb = _pick_tile(nc, 8)
    npair = RET_HEADS // 2
    tok = pl.BlockSpec((None, cb * RET_CHUNK, D_RET), lambda bi, c: (bi, c, 0))
    st = pl.BlockSpec((None, cb, npair, LANES, LANES), lambda bi, c: (bi, c, 0, 0, 0))
    lg = pl.BlockSpec(lgf.shape, lambda bi, c: (0, 0, 0))
    return pl.pallas_call(
        functools.partial(_ret_out_kernel, cb=cb),
        out_shape=jax.ShapeDtypeStruct((b, s, D_RET), BF16),
        grid=(b, nc // cb),
        in_specs=[lg, lg, tok, tok, tok, tok, st, st, pl.BlockSpec((1, D_RET), lambda bi, c: (0, 0))],
        out_specs=tok,
        compiler_params=_params("parallel", "parallel"),
        name="ret_out",
    )(lgf, lgb, rq, rk, rv, rg, rf, rb, gnw)


def _conv_kernel(prev_ref, cur_ref, next_ref, w_ref, b_ref, lnw_ref, lnb_ref, o_ref,
                 xpad, shifted, hbuf, wtile):
    i = pl.program_id(1)
    tm = cur_ref.shape[0]
    halo = CONV_HALO
    first = i == 0
    last = i == pl.num_programs(1) - 1
    xpad[0:halo, :] = jnp.where(first, 0.0, prev_ref[...])
    xpad[halo:halo + tm, :] = cur_ref[...]
    xpad[halo + tm:2 * halo + tm, :] = jnp.where(last, 0.0, next_ref[...])
    span = tm + 2 * halo - SUBLANES
    for r in range(SUBLANES):
        shifted[r] = xpad[r:r + span, :]
    base = halo - CONV_WIDTH // 2
    bias = b_ref[...]
    ch = cur_ref.shape[1]
    for w in range(CONV_WIDTH):
        wtile[w] = jnp.broadcast_to(w_ref[w:w + 1, :], (SUBLANES, ch))

    taps_by_shift = {}
    for w in range(CONV_WIDTH):
        taps_by_shift.setdefault((base + w) % SUBLANES, []).append(((base + w) // SUBLANES, w))

    def sub(sb, carry):
        r0 = pl.multiple_of(sb * CONV_SUB, CONV_SUB)
        chains = [None] * CONV_CHAINS
        groups = CONV_SUB // SUBLANES
        for r, taps in sorted(taps_by_shift.items()):
            a_lo = min(a for a, _ in taps)
            n_g = max(a for a, _ in taps) - a_lo + groups
            win = shifted[r, pl.ds(r0 + a_lo * SUBLANES, n_g * SUBLANES), :].reshape(n_g, SUBLANES, ch)
            for a, w in taps:
                term = win[a - a_lo:a - a_lo + groups] * wtile[w]
                c = w % CONV_CHAINS
                chains[c] = term if chains[c] is None else chains[c] + term
        total = (chains[0] + chains[1]) + (chains[2] + chains[3])
        hbuf[pl.ds(r0, CONV_SUB), :] = total.reshape(CONV_SUB, ch) + bias
        return carry

    lax.fori_loop(0, tm // CONV_SUB, sub, 0)
    h = hbuf[...]
    mu = jnp.mean(h, axis=-1, keepdims=True)
    dl = h - mu
    var = jnp.mean(dl * dl, axis=-1, keepdims=True)
    y = dl * lax.rsqrt(var + EPS) * lnw_ref[...] + lnb_ref[...]
    o_ref[...] = (y * _sigmoid(y)).astype(BF16)


def _conv(cv, w, bias, lnw, lnb):
    b, s, ch = cv.shape
    tm = _pick_tile(s, 1024)
    hpb = tm // CONV_HALO
    nh = s // CONV_HALO
    row = pl.BlockSpec((1, ch), lambda bi, i: (0, 0))
    span = tm + 2 * CONV_HALO - SUBLANES
    return pl.pallas_call(
        _conv_kernel,
        out_shape=jax.ShapeDtypeStruct((b, s, ch), BF16),
        grid=(b, s // tm),
        in_specs=[pl.BlockSpec((None, CONV_HALO, ch), lambda bi, i: (bi, jnp.maximum(i * hpb - 1, 0), 0)),
                  pl.BlockSpec((None, tm, ch), lambda bi, i: (bi, i, 0)),
                  pl.BlockSpec((None, CONV_HALO, ch),
                               lambda bi, i: (bi, jnp.minimum((i + 1) * hpb, nh - 1), 0)),
                  pl.BlockSpec(w.shape, lambda bi, i: (0, 0)), row, row, row],
        out_specs=pl.BlockSpec((None, tm, ch), lambda bi, i: (bi, i, 0)),
        scratch_shapes=[pltpu.VMEM((tm + 2 * CONV_HALO, ch), F32),
                        pltpu.VMEM((SUBLANES, span, ch), F32),
                        pltpu.VMEM((tm, ch), F32),
                        pltpu.VMEM((CONV_WIDTH, SUBLANES, ch), F32)],
        compiler_params=_params("parallel", "parallel"),
        name="conv",
    )(cv, cv, cv, w, bias, lnw, lnb)


def _outproj_route_kernel(att_ref, ret_ref, cnv_ref, w_ref, x_ref, g1_ref, nw_ref, sc_ref, sh_ref, rcat_ref,
                          xo_ref, h_ref, rt_ref):
    y = (jnp.dot(att_ref[...], w_ref[0:D_ATT, :], preferred_element_type=F32)
         + jnp.dot(ret_ref[...], w_ref[D_ATT:D_ATT + D_RET, :], preferred_element_type=F32)
         + jnp.dot(cnv_ref[...], w_ref[D_ATT + D_RET:, :], preferred_element_type=F32))
    xn = x_ref[...] + g1_ref[...] * y
    xo_ref[...] = xn
    h = _modulated_rms(xn, nw_ref[...], sc_ref[...], sh_ref[...])
    h_ref[...] = _pack_bf16_pairs(h)
    hi = h.astype(BF16)
    lo = (h - hi.astype(F32)).astype(BF16)
    tm = h.shape[0]
    r = jnp.dot(jnp.concatenate([hi, lo], axis=0), rcat_ref[...], preferred_element_type=F32)
    logits = (r[0:tm, 0:LANES] + r[0:tm, LANES:]) + (r[tm:, 0:LANES] + r[tm:, LANES:])
    lane = lax.broadcasted_iota(jnp.int32, logits.shape, 1).astype(F32)
    logits = jnp.where(lane < N_EXPERTS, logits, NEG_INF)
    m1 = jnp.max(logits, axis=-1, keepdims=True)
    i1 = jnp.min(jnp.where(logits == m1, lane, float(LANES)), axis=-1, keepdims=True)
    rest = jnp.where(lane == i1, NEG_INF, logits)
    m2 = jnp.max(rest, axis=-1, keepdims=True)
    i2 = jnp.min(jnp.where(rest == m2, lane, float(LANES)), axis=-1, keepdims=True)
    e2 = jnp.exp(m2 - m1)
    w1 = 1.0 / (1.0 + e2)
    w2 = e2 / (1.0 + e2)
    rt_ref[...] = jnp.where(lane == 0.0, i1,
                            jnp.where(lane == 1.0, i2,
                                      jnp.where(lane == 2.0, w1, jnp.where(lane == 3.0, w2, 0.0))))


def _outproj_route(att, ret, cnv, w_bf, x, g1, nw, sc, sh, router):
    b, s, d = x.shape
    tm = _pick_tile(s, 1024)
    tok = lambda n: pl.BlockSpec((None, tm, n), lambda bi, i: (bi, i, 0))
    per_b = pl.BlockSpec((None, 1, d), lambda bi, i: (bi, 0, 0))
    rpad = jnp.zeros((d, LANES), F32).at[:, :N_EXPERTS].set(router)
    rhi = rpad.astype(BF16)
    rlo = (rpad - rhi.astype(F32)).astype(BF16)
    return pl.pallas_call(
        _outproj_route_kernel,
        out_shape=(jax.ShapeDtypeStruct((b, s, d), F32), jax.ShapeDtypeStruct((b, s, d // 2), jnp.int32),
                   jax.ShapeDtypeStruct((b, s, LANES), F32)),
        grid=(b, s // tm),
        in_specs=[tok(D_ATT), tok(D_RET), tok(CONV_CH), pl.BlockSpec(w_bf.shape, lambda bi, i: (0, 0)),
                  tok(d), per_b, pl.BlockSpec((1, d), lambda bi, i: (0, 0)), per_b, per_b,
                  pl.BlockSpec((d, 2 * LANES), lambda bi, i: (0, 0))],
        out_specs=(tok(d), tok(d // 2), tok(LANES)),
        compiler_params=_params("parallel", "parallel"),
        name="outproj_route",
    )(att, ret, cnv, w_bf, x, g1, nw, sc, sh, jnp.concatenate([rhi, rlo], axis=1))


def _swiglu_chunks(h, wg_ref, wu_ref, wd_ref, sub):
    total = None
    for c0 in range(0, wg_ref.shape[1], sub):
        cs = slice(c0, min(c0 + sub, wg_ref.shape[1]))
        gate = jnp.dot(h, wg_ref[:, cs].astype(BF16), preferred_element_type=F32)
        up = jnp.dot(h, wu_ref[:, cs].astype(BF16), preferred_element_type=F32)
        act = (gate * _sigmoid(gate) * up).astype(BF16)
        part = jnp.dot(act, wd_ref[cs, :].astype(BF16), preferred_element_type=F32)
        total = part if total is None else total + part
    return total


def _mixer_ffn_kernel(att_ref, ret_ref, cnv_ref, wo_ref, x_ref, g1_ref, nw_ref, sc_ref, sh_ref,
                      wg_ref, wu_ref, wd_ref, g2_ref, o_ref):
    y = (jnp.dot(att_ref[...], wo_ref[0:D_ATT, :], preferred_element_type=F32)
         + jnp.dot(ret_ref[...], wo_ref[D_ATT:D_ATT + D_RET, :], preferred_element_type=F32)
         + jnp.dot(cnv_ref[...], wo_ref[D_ATT + D_RET:, :], preferred_element_type=F32))
    xn = x_ref[...] + g1_ref[...] * y
    h = _modulated_rms(xn, nw_ref[...], sc_ref[...], sh_ref[...]).astype(BF16)
    o_ref[...] = xn + g2_ref[...] * _swiglu_chunks(h, wg_ref, wu_ref, wd_ref, FFN_SUB)


def _mixer_ffn(att, ret, cnv, wo, x, g1, nw, sc, sh, wg, wu, wd, g2):
    b, s, d = x.shape
    f = wg.shape[1]
    tm = _pick_tile(s, 512)
    tok = lambda n: pl.BlockSpec((None, tm, n), lambda bi, i: (bi, i, 0))
    per_b = pl.BlockSpec((None, 1, d), lambda bi, i: (bi, 0, 0))
    res = lambda shape: pl.BlockSpec(shape, lambda bi, i: (0, 0), pipeline_mode=pl.Buffered(1))
    return pl.pallas_call(
        _mixer_ffn_kernel,
        out_shape=jax.ShapeDtypeStruct((b, s, d), F32),
        grid=(b, s // tm),
        in_specs=[tok(D_ATT), tok(D_RET), tok(CONV_CH), res(wo.shape), tok(d), per_b,
                  pl.BlockSpec((1, d), lambda bi, i: (0, 0)), per_b, per_b,
                  res((d, f)), res((d, f)), res((f, d)), per_b],
        out_specs=tok(d),
        compiler_params=_params("parallel", "parallel"),
        name="mixer_ffn_dense",
    )(att, ret, cnv, wo, x, g1, nw, sc, sh, wg, wu, wd, g2)


def _gather_rows(idx, src):
    n = idx.shape[0]
    d = src.shape[1]
    mesh = plsc.VectorSubcoreMesh(core_axis_name="core", subcore_axis_name="subcore")

    @pl.kernel(out_type=jax.ShapeDtypeStruct((n, d), src.dtype), mesh=mesh, name="moe_gather")
    def gather(src_hbm, idx_hbm, out_hbm):
        def body(idx_vmem, out_vmem):
            pltpu.sync_copy(src_hbm.at[idx_vmem.at[0, pl.ds(0, SC_WINDOW)]], out_vmem)

        pltpu.emit_pipeline(
            body,
            grid=(n // SC_WINDOW,),
            in_specs=[pl.BlockSpec((1, LANES), lambda i: (i, 0))],
            out_specs=[pl.BlockSpec((SC_WINDOW, d), lambda i: (i, 0))],
            core_axis_name=("core", "subcore"),
            dimension_semantics=(pltpu.PARALLEL,),
        )(idx_hbm, out_hbm)

    idx_rows = jnp.pad(idx.reshape(n // SC_WINDOW, SC_WINDOW), ((0, 0), (0, LANES - SC_WINDOW)))
    return gather(src, idx_rows)


def _invert_rows(dest, n_rows):
    n = dest.shape[0]
    n_tok = n // 2
    mesh = plsc.VectorSubcoreMesh(core_axis_name="core", subcore_axis_name="subcore")

    @pl.kernel(out_type=jax.ShapeDtypeStruct((n_rows,), jnp.int32), mesh=mesh, name="moe_row_src",
               scratch_types=[pltpu.VMEM((n,), jnp.int32), pltpu.VMEM((n_rows,), jnp.int32)],
               compiler_params=pltpu.CompilerParams(needs_layout_passes=False))
    def invert(dest_hbm, out_hbm, dest_vmem, rows_vmem):
        @pl.when((lax.axis_index("core") == 0) & (lax.axis_index("subcore") == 0))
        def _():
            pltpu.sync_copy(dest_hbm, dest_vmem)

            @pl.loop(0, n_rows, step=SC_LANES)
            def _(r):
                rows_vmem[pl.ds(r, SC_LANES)] = lax.rem(lax.iota(jnp.int32, SC_LANES) + r, n_tok)

            @pl.loop(0, n, step=SC_LANES)
            def _(a):
                tok = lax.iota(jnp.int32, SC_LANES) + a
                tok = jnp.where(tok >= n_tok, tok - n_tok, tok)
                plsc.store_scatter(rows_vmem, [dest_vmem[pl.ds(a, SC_LANES)]], tok)

            pltpu.sync_copy(rows_vmem, out_hbm)

    return invert(dest)


def _moe_kernel(te_ref, nu_ref, x_ref, wg_ref, wu_ref, wd_ref, *rest, tile0, nj):
    o_ref, xb, acc = rest[-3:]
    t = pl.program_id(0) + tile0
    j = pl.program_id(1)
    used = t < nu_ref[0]

    def partial_sum():
        return _swiglu_chunks(xb[...], wg_ref, wu_ref, wd_ref, MOE_SUB)

    @pl.when(used & (j == 0))
    def _():
        lo, hi = _unpack_bf16_pairs(x_ref[...])
        half = lo.shape[1]
        xb[:, 0:half] = lo.astype(BF16)
        xb[:, half:] = hi.astype(BF16)
        if nj == 1:
            o_ref[...] = _pack_bf16_pairs(partial_sum())
        else:
            acc[...] = partial_sum()

    if nj > 2:
        @pl.when(used & (j > 0) & (j < nj - 1))
        def _():
            acc[...] += partial_sum()

    if nj > 1:
        @pl.when(used & (j == nj - 1))
        def _():
            o_ref[...] = _pack_bf16_pairs(acc[...] + partial_sum())

    @pl.when(jnp.logical_not(used) & (j == nj - 1))
    def _():
        o_ref[...] = jnp.zeros_like(o_ref)


def _moe_grouped(tile_expert, n_used, xs, wg, wu, wd, y_prev, tile0, n_rows):
    p, dp = xs.shape
    d = 2 * dp
    f = wg.shape[2]
    tm = MOE_TM
    tf = MOE_TF
    nj = f // tf

    def jj(t, j, te, nu):
        return jnp.where(t + tile0 < nu[0], j, nj - 1)

    in_specs = [pl.BlockSpec((tm, dp), lambda t, j, te, nu: (t, 0)),
                pl.BlockSpec((None, d, tf), lambda t, j, te, nu: (te[t + tile0], 0, jj(t, j, te, nu))),
                pl.BlockSpec((None, d, tf), lambda t, j, te, nu: (te[t + tile0], 0, jj(t, j, te, nu))),
                pl.BlockSpec((None, tf, d), lambda t, j, te, nu: (te[t + tile0], jj(t, j, te, nu), 0))]
    args = [tile_expert, n_used, xs, wg, wu, wd]
    aliases = {}
    if y_prev is not None:
        in_specs.append(pl.BlockSpec(memory_space=pl.ANY))
        args.append(y_prev)
        aliases = {len(args) - 1: 0}
    return pl.pallas_call(
        functools.partial(_moe_kernel, tile0=tile0, nj=nj),
        out_shape=jax.ShapeDtypeStruct((n_rows, dp), jnp.int32),
        grid_spec=pltpu.PrefetchScalarGridSpec(
            num_scalar_prefetch=2,
            grid=(p // tm, nj),
            in_specs=in_specs,
            out_specs=pl.BlockSpec((tm, dp), lambda t, j, te, nu: (t + tile0, 0)),
            scratch_shapes=[pltpu.VMEM((tm, d), BF16), pltpu.VMEM((tm, d), F32)]),
        input_output_aliases=aliases,
        compiler_params=_params("arbitrary", "arbitrary"),
        name="moe_grouped",
    )(*args)


def _combine_kernel(y1_ref, y2_ref, x_ref, g2_ref, rt_ref, o_ref):
    rt = rt_ref[...]
    w1, w2 = rt[:, 2:3], rt[:, 3:4]
    half = y1_ref.shape[1]
    for k, (a, c) in enumerate(zip(_unpack_bf16_pairs(y1_ref[...]), _unpack_bf16_pairs(y2_ref[...]))):
        cs = slice(k * half, (k + 1) * half)
        o_ref[:, cs] = x_ref[:, cs] + g2_ref[:, cs] * (w1 * a + w2 * c)


def _moe_combine(yg, x, g2, route):
    b, s, d = x.shape
    tm = _pick_tile(s, 512)
    tok = lambda n: pl.BlockSpec((None, tm, n), lambda bi, i: (bi, i, 0))
    return pl.pallas_call(
        _combine_kernel,
        out_shape=jax.ShapeDtypeStruct((b, s, d), F32),
        grid=(b, s // tm),
        in_specs=[tok(d // 2), pl.BlockSpec((None, tm, d // 2), lambda bi, i: (b + bi, i, 0)), tok(d),
                  pl.BlockSpec((None, 1, d), lambda bi, i: (bi, 0, 0)), tok(LANES)],
        out_specs=tok(d),
        compiler_params=_params("parallel", "parallel"),
        name="moe_combine",
    )(yg, yg, x, g2, route)


def _moe(h, route, x, g2, wg, wu, wd):
    b, s, d = x.shape
    n_tok = b * s
    tm = MOE_TM
    rt = route.reshape(n_tok, LANES)
    flat_e = jnp.concatenate([rt[:, 0], rt[:, 1]]).astype(jnp.int32)
    onehot = (flat_e[:, None] == jnp.arange(N_EXPERTS, dtype=jnp.int32)[None, :]).astype(jnp.int32)
    csum = jnp.cumsum(onehot, axis=0)
    rank = jnp.sum(csum * onehot, axis=1) - 1
    counts = csum[-1]
    tiles_e = (counts + tm - 1) // tm
    tiles_cum = jnp.cumsum(tiles_e)
    row_start = (tiles_cum - tiles_e) * tm
    dest = jnp.sum(onehot * row_start[None, :], axis=1) + rank
    n_tiles = 2 * n_tok // tm + N_EXPERTS
    tile_ids = jnp.arange(n_tiles, dtype=jnp.int32)
    tile_expert = jnp.sum((tile_ids[:, None] >= tiles_cum[None, :]).astype(jnp.int32), axis=1)
    last_e = jnp.max(jnp.where(tiles_e > 0, jnp.arange(N_EXPERTS, dtype=jnp.int32), 0))
    tile_expert = jnp.minimum(tile_expert, last_e).astype(jnp.int32)
    n_used = tiles_cum[-1:].astype(jnp.int32)
    dest = dest.astype(jnp.int32)
    row_src = _invert_rows(dest, n_tiles * tm)
    n_chunks = max(c for c in range(1, MOE_CHUNKS + 1) if n_tiles % c == 0)
    tiles_c = n_tiles // n_chunks
    h_flat = h.reshape(n_tok, d // 2)
    xs = [_gather_rows(row_src[c * tiles_c * tm:(c + 1) * tiles_c * tm], h_flat) for c in range(n_chunks)]
    y = None
    for c in range(n_chunks):
        y = _moe_grouped(tile_expert, n_used, xs[c], wg, wu, wd, y, c * tiles_c, n_tiles * tm)
    yg = _gather_rows(dest, y).reshape(2 * b, s, d // 2)
    return _moe_combine(yg, x, g2, route)


def _rope_tables(n):
    rows = n // GRID_W
    r = np.repeat(np.arange(rows), GRID_W).astype(np.float64)
    col = np.tile(np.arange(GRID_W), rows).astype(np.float64)
    freqs = ROPE_BASE ** (-np.arange(ROPE_FREQS, dtype=np.float64) / ROPE_FREQS)
    ang = np.stack([r[:, None] * freqs, col[:, None] * freqs], axis=1)
    ang = np.repeat(ang[:, :, None, :], 2, axis=2).reshape(n, HEAD_DIM)
    ang = np.tile(ang, (1, LANES // HEAD_DIM))
    cos, sin = np.cos(ang), np.sin(ang)
    first_half = (np.arange(LANES) % (2 * ROPE_FREQS)) < ROPE_FREQS
    to_dev = lambda t: jnp.asarray(t.astype(np.float32))
    return to_dev(cos), to_dev(np.where(first_half, -sin, 0.0)), to_dev(np.where(first_half, 0.0, sin))


def _lane_rows(lg):
    return jnp.repeat(lg.astype(F32), HEAD_DIM).reshape(RET_HEADS // 2, 1, LANES)


def kernel(x, c, ctx, c_ctx, ada_w, ada_b, norm1_w, norm2_w, w_in, w_out, q_norm_w, k_norm_w,
           attn_sink, ret_decay_f, ret_decay_b, ret_gn_w, conv_w, conv_b, conv_ln_w, conv_ln_b,
           ffn_w_gate, ffn_w_up, ffn_w_down, router_w, moe_w_gate, moe_w_up, moe_w_down):
    b, n, d = x.shape
    n_ctx = ctx.shape[1]
    depth = ada_w.shape[0]
    cond = jnp.zeros((SUBLANES, d), F32).at[0:b].set(c).at[b].set(c_ctx)
    mods = _adaln(cond, ada_w, ada_b).reshape(depth, SUBLANES, 6, d)
    cos, sa, sb = _rope_tables(n)
    ones_c = jnp.ones((n_ctx, LANES), F32)
    zeros_c = jnp.zeros((n_ctx, LANES), F32)
    zero_state = jnp.zeros((b, RET_HEADS // 2, LANES, LANES), F32)
    row = lambda v: v.reshape(1, -1)
    slab_rows = b * (n // ATT_BLOCK // ATT_QBLOCKS) * BF16_SUBLANES
    riders = {l: {} for l in range(depth)}

    def ride(l, key, w):
        w2 = w.reshape(-1, w.shape[-1])
        if 0 <= l < depth and (n // ATT_BLOCK) % ATT_QBLOCKS == 0 and w2.shape[0] % slab_rows == 0:
            riders[l][key] = w2

    for j, w in enumerate((moe_w_gate, moe_w_up, moe_w_down)):
        ride(j * min(depth, 2) // 3, ("moe", j), w)
    for l in range(depth):
        if l % 2 == 0:
            ride(l, ("ffn", l, 0), ffn_w_gate[l // 2])
            ride(l, ("ffn", l, 1), ffn_w_up[l // 2])
            ride(l, ("ffn", l, 2), ffn_w_down[l // 2])
        ride(l - 1, ("w_in", l), w_in[l])
        ride(l - 1, ("w_out", l), w_out[l])
    rounded = {}

    def bf16_of(key, w):
        return rounded[key].reshape(w.shape) if key in rounded else w.astype(BF16)

    for l in range(depth):
        last = l == depth - 1
        m_lat = [mods[l, 0:b, k][:, None, :] for k in range(6)]
        m_ctx = [jnp.broadcast_to(mods[l, b, k][None, None, :], (b, 1, d)) for k in range(6)]
        w_in_bf = bf16_of(("w_in", l), w_in[l])
        w_out_bf = bf16_of(("w_out", l), w_out[l])
        qw = row(jnp.tile(q_norm_w[l], LANES // HEAD_DIM))
        kw = row(jnp.tile(k_norm_w[l], LANES // HEAD_DIM))
        lgf = _lane_rows(jax.nn.log_sigmoid(ret_decay_f[l].astype(F32)))
        lgb = _lane_rows(jax.nn.log_sigmoid(ret_decay_b[l].astype(F32)))
        sink_tab = jnp.repeat(attn_sink[l].astype(F32), ATT_BLOCK).reshape(ATT_Q_HEADS // 2, 2 * ATT_BLOCK)

        q, k, v, rk, rv, rq, rg, cv = _inproj(x, row(norm1_w[l]), m_lat[1], m_lat[0], w_in_bf,
                                               cos, sa, sb, qw, kw)
        qc, kc, vc, rkc, rvc, rqc, rgc, cvc = _inproj(ctx, row(norm1_w[l]), m_ctx[1], m_ctx[0], w_in_bf,
                                                       ones_c, zeros_c, zeros_c, qw, kw)
        rf_c, s_f = _ret_states(lgf, rkc, rvc, zero_state, reverse=False)
        rb_c, s_b = _ret_states(lgb, rkc, rvc, zero_state, reverse=True)
        rf, _ = _ret_states(lgf, rk, rv, s_f, reverse=False)
        rb, _ = _ret_states(lgb, rk, rv, s_b, reverse=True)

        att, cast_out = _attention(q, k, v, kc, vc, sink_tab, window=True, cast=list(riders[l].values()))
        rounded.update(zip(riders[l].keys(), cast_out))
        ret = _ret_out(lgf, lgb, rq, rk, rv, rg, rf, rb, row(ret_gn_w[l]))
        cnv = _conv(cv, conv_w[l], row(conv_b[l]), row(conv_ln_w[l]), row(conv_ln_b[l]))

        if l % 2 == 0:
            i = l // 2
            wg, wu, wd = (bf16_of(("ffn", l, j), w[i]) for j, w in enumerate((ffn_w_gate, ffn_w_up, ffn_w_down)))
            x_new = _mixer_ffn(att, ret, cnv, w_out_bf, x, m_lat[2], row(norm2_w[l]), m_lat[4], m_lat[3],
                               wg, wu, wd, m_lat[5])
        else:
            i = l // 2
            wg, wu, wd = (bf16_of(("moe", j), w)[i] for j, w in enumerate((moe_w_gate, moe_w_up, moe_w_down)))
            x_mid, h2, route = _outproj_route(att, ret, cnv, w_out_bf, x, m_lat[2], row(norm2_w[l]),
                                              m_lat[4], m_lat[3], router_w[i])
            x_new = _moe(h2, route, x_mid, m_lat[5], wg, wu, wd)

        if not last:
            att_c, _ = _attention(qc, None, None, kc, vc, sink_tab, window=False)
            ret_c = _ret_out(lgf, lgb, rqc, rkc, rvc, rgc, rf_c, rb_c, row(ret_gn_w[l]))
            cnv_c = _conv(cvc, conv_w[l], row(conv_b[l]), row(conv_ln_w[l]), row(conv_ln_b[l]))
            if l % 2 == 0:
                ctx = _mixer_ffn(att_c, ret_c, cnv_c, w_out_bf, ctx, m_ctx[2], row(norm2_w[l]), m_ctx[4],
                                 m_ctx[3], wg, wu, wd, m_ctx[5])
            else:
                c_mid, h2c, route_c = _outproj_route(att_c, ret_c, cnv_c, w_out_bf, ctx, m_ctx[2],
                                                     row(norm2_w[l]), m_ctx[4], m_ctx[3], router_w[i])
                ctx = _moe(h2c, route_c, c_mid, m_ctx[5], wg, wu, wd)
        x = x_new
    return x
```

```python
import functools

import jax
import jax.numpy as jnp
import numpy as np
from jax import lax
from jax.experimental import pallas as pl
from jax.experimental.pallas import tpu as pltpu
from jax.experimental.pallas import tpu_sc as plsc

F32 = jnp.float32
BF16 = jnp.bfloat16

GRID_W = 64
HEAD_DIM = 64
ATT_Q_HEADS = 8
ATT_KV_HEADS = 2
ATT_WINDOW = 128
ATT_BLOCK = 128
RET_HEADS = 4
RET_CHUNK = 128
RET_K_SCALE = HEAD_DIM ** -0.5
ATT_SCALE = HEAD_DIM ** -0.5
CONV_CH = 256
CONV_WIDTH = 31
ROPE_BASE = 10000.0
ROPE_FREQS = HEAD_DIM // 4
D_ATT = ATT_Q_HEADS * HEAD_DIM
D_RET = RET_HEADS * HEAD_DIM
ATT_KV_W = ATT_KV_HEADS * HEAD_DIM
C_ATT_K = 0
C_ATT_V = C_ATT_K + ATT_KV_W
C_RET_K = C_ATT_V + ATT_KV_W
C_RET_V = C_RET_K + D_RET
C_ATT_Q = C_RET_V + D_RET
C_RET_Q = C_ATT_Q + D_ATT
C_RET_G = C_RET_Q + D_RET
C_CONV = C_RET_G + D_RET
N_EXPERTS = 8
EPS = 1e-6
NEG_INF = -1e30

LANES = 128
SUBLANES = 8
BF16_SUBLANES = 16
VMEM_LIMIT = 48 * 1024 * 1024
CONV_HALO = 16
CONV_SUB = 32
CONV_CHAINS = 4
ATT_QBLOCKS = 4
RET_UNROLL = 4
MOE_TM = 512
MOE_CHUNKS = 2
MOE_SUB = 512
FFN_SUB = 512
MOE_TF = 1792
SC_LANES = 16
SC_WINDOW = 64


def _params(*sem):
    return pltpu.CompilerParams(dimension_semantics=sem, vmem_limit_bytes=VMEM_LIMIT)


def _sigmoid(x):
    return 1.0 / (1.0 + jnp.exp(-x))


def _pack_bf16_pairs(v):
    c = v.shape[1] // 2
    bits = pltpu.bitcast(v.astype(BF16).astype(F32), jnp.uint32)
    packed = (bits[:, c:] & jnp.uint32(0xFFFF0000)) | (bits[:, :c] >> 16)
    return pltpu.bitcast(packed, jnp.int32)


def _unpack_bf16_pairs(p):
    bits = pltpu.bitcast(p, jnp.uint32)
    return pltpu.bitcast(bits << 16, F32), pltpu.bitcast(bits & jnp.uint32(0xFFFF0000), F32)


def _pick_tile(n, pref):
    t = min(n, pref)
    assert n % t == 0, (n, t)
    return t


def _adaln_kernel(c_ref, w_ref, b_ref, o_ref):
    c = c_ref[...]
    s = c * _sigmoid(c)
    w = w_ref[...]
    s_hi, w_hi = s.astype(BF16), w.astype(BF16)
    s_lo = (s - s_hi.astype(F32)).astype(BF16)
    w_lo = (w - w_hi.astype(F32)).astype(BF16)
    dot = functools.partial(jnp.dot, preferred_element_type=F32)
    o_ref[...] = (dot(s_hi, w_hi) + dot(s_lo, w_hi)) + dot(s_hi, w_lo) + b_ref[...]


def _adaln(cond, ada_w, ada_b):
    depth, d, n = ada_w.shape
    tn = _pick_tile(n, 3072)
    return pl.pallas_call(
        _adaln_kernel,
        out_shape=jax.ShapeDtypeStruct((depth, cond.shape[0], n), F32),
        grid=(depth, n // tn),
        in_specs=[pl.BlockSpec(cond.shape, lambda l, j: (0, 0)),
                  pl.BlockSpec((None, d, tn), lambda l, j: (l, 0, j)),
                  pl.BlockSpec((None, 1, tn), lambda l, j: (l, 0, j))],
        out_specs=pl.BlockSpec((None, cond.shape[0], tn), lambda l, j: (l, 0, j)),
        compiler_params=_params("parallel", "parallel"),
        name="adaln",
    )(cond, ada_w, ada_b.reshape(depth, 1, n))


def _modulated_rms(x, nw, sc, sh):
    ms = jnp.mean(x * x, axis=-1, keepdims=True)
    return (x * lax.rsqrt(ms + EPS) * nw) * (1.0 + sc) + sh


def _head_group_matrix():
    r = lax.broadcasted_iota(jnp.int32, (2 * LANES, 2 * LANES), 0) // HEAD_DIM
    c = lax.broadcasted_iota(jnp.int32, (2 * LANES, 2 * LANES), 1) // HEAD_DIM
    return jnp.where(r == c, 1.0, 0.0).astype(BF16)


def _per_head_sum(v, grp):
    hi = v.astype(BF16)
    lo = (v - hi.astype(F32)).astype(BF16)
    r = jnp.dot(jnp.concatenate([hi, lo], axis=1), grp, preferred_element_type=F32)
    return r[:, 0:LANES] + r[:, LANES:]


def _dup_halves(t):
    sw = pltpu.roll(t, HEAD_DIM, 1)
    lo = lax.broadcasted_iota(jnp.int32, t.shape, 1) < HEAD_DIM
    return jnp.where(lo, t, sw), jnp.where(lo, sw, t)


def _inproj_kernel(x_ref, nw_ref, sc_ref, sh_ref, w_ref, cos_ref, sa_ref, sb_ref, qw_ref, kw_ref,
                   q_ref, k_ref, v_ref, rk_ref, rv_ref, rq_ref, rg_ref, cv_ref):
    hb = _modulated_rms(x_ref[...], nw_ref[...], sc_ref[...], sh_ref[...]).astype(BF16)

    def proj(c0, n):
        return jnp.dot(hb, w_ref[:, c0:c0 + n], preferred_element_type=F32)

    grp = _head_group_matrix()
    cos, sa, sb = cos_ref[...], sa_ref[...], sb_ref[...]

    def norm_rope(p, wrow):
        y = p * lax.rsqrt(_per_head_sum(p * p, grp) * (1.0 / HEAD_DIM) + EPS) * wrow
        return (y * cos + pltpu.roll(y, LANES - ROPE_FREQS, 1) * sa
                + pltpu.roll(y, ROPE_FREQS, 1) * sb)

    def put_kv(kv):
        k0, k1 = _dup_halves(norm_rope(kv[:, 0:ATT_KV_W], kw_ref[...]))
        k_ref[:, 0:LANES] = k0.astype(BF16)
        k_ref[:, LANES:2 * LANES] = k1.astype(BF16)
        v_ref[...] = kv[:, ATT_KV_W:].T.astype(BF16)

    def put_q(qall):
        qw = qw_ref[...] * ATT_SCALE
        for j in range(D_ATT // LANES):
            q_ref[:, j * LANES:(j + 1) * LANES] = norm_rope(qall[:, j * LANES:(j + 1) * LANES], qw).astype(BF16)

    def put_rk(p):
        rk_ref[...] = (p * RET_K_SCALE).astype(BF16)

    def put_rv(p):
        rv_ref[...] = p.astype(BF16)

    def put_rq(p):
        rq_ref[...] = p.astype(BF16)

    def put_rg(p):
        rg_ref[...] = p

    def put_conv(p):
        cv_ref[...] = p[:, 0:CONV_CH] * _sigmoid(p[:, CONV_CH:])

    units = [(C_ATT_K, 2 * ATT_KV_W, put_kv), (C_ATT_Q, D_ATT, put_q), (C_RET_K, D_RET, put_rk),
             (C_RET_V, D_RET, put_rv), (C_RET_Q, D_RET, put_rq), (C_RET_G, D_RET, put_rg),
             (C_CONV, 2 * CONV_CH, put_conv)]
    ahead = 2
    pending = [proj(c0, n) for c0, n, _ in units[:ahead]]
    for n, (_, _, put) in enumerate(units):
        if n + ahead < len(units):
            pending.append(proj(*units[n + ahead][:2]))
        put(pending.pop(0))


def _inproj(x, nw, sc, sh, w_bf, cos, sa, sb, qw, kw):
    b, s, d = x.shape
    tm = _pick_tile(s, 1024)
    row = lambda n: pl.BlockSpec((1, n), lambda bi, i: (0, 0))
    per_b = pl.BlockSpec((None, 1, d), lambda bi, i: (bi, 0, 0))
    tab = pl.BlockSpec((tm, LANES), lambda bi, i: (i, 0))
    tok = lambda n: pl.BlockSpec((None, tm, n), lambda bi, i: (bi, i, 0))
    shp = lambda n, dt: jax.ShapeDtypeStruct((b, s, n), dt)
    return pl.pallas_call(
        _inproj_kernel,
        out_shape=(shp(D_ATT, BF16), shp(2 * LANES, BF16), jax.ShapeDtypeStruct((b, ATT_KV_W, s), BF16),
                   shp(D_RET, BF16), shp(D_RET, BF16), shp(D_RET, BF16), shp(D_RET, F32),
                   shp(CONV_CH, F32)),
        grid=(b, s // tm),
        in_specs=[tok(d), row(d), per_b, per_b,
                  pl.BlockSpec(w_bf.shape, lambda bi, i: (0, 0)),
                  tab, tab, tab, row(LANES), row(LANES)],
        out_specs=(tok(D_ATT), tok(2 * LANES), pl.BlockSpec((None, ATT_KV_W, tm), lambda bi, i: (bi, 0, i)),
                   tok(D_RET), tok(D_RET), tok(D_RET), tok(D_RET), tok(CONV_CH)),
        compiler_params=_params("parallel", "parallel"),
        name="inproj",
    )(x, nw, sc, sh, w_bf, cos, sa, sb, qw, kw)


def _attn_kernel(*refs, window, nq, n_cast):
    blk = ATT_BLOCK
    n_in = len(refs) - 1 - n_cast
    for src, dst in zip(refs[n_in - n_cast:n_in], refs[n_in + 1:]):
        dst[...] = src[...].astype(BF16)
    refs = refs[:n_in - n_cast] + refs[n_in:n_in + 1]
    if window:
        q_ref = refs[0]
        k_refs = refs[1:nq + 3]
        v_refs = refs[nq + 3:2 * nq + 5]
        kx_ref, vx_ref, sink_ref, o_ref = refs[2 * nq + 5:]
    else:
        q_ref, kx_ref, vx_ref, sink_ref, o_ref = refs
    n_ctx = kx_ref.shape[0]
    nk = 3 * blk + n_ctx if window else n_ctx
    if window:
        i = pl.program_id(1)
        last = nq * pl.num_programs(1) - 1
        key = lax.broadcasted_iota(jnp.int32, (blk, 2 * blk), 0)
        qry = lax.broadcasted_iota(jnp.int32, (blk, 2 * blk), 1) & (blk - 1)

        def band_masks(sub):
            off_prev = jnp.where(nq * i + sub > 0, 0, blk)
            off_next = jnp.where(nq * i + sub < last, 0, blk)
            return key >= qry + off_prev, key + off_next <= qry
    first_head = lax.broadcasted_iota(jnp.int32, (blk, LANES), 1) < HEAD_DIM
    ones = jnp.ones((2 * SUBLANES, nk), BF16)

    def scores(sub, pair):
        g = pair // 2
        gs = slice(g * LANES, (g + 1) * LANES)
        if window:
            kcat = jnp.concatenate([r[:, gs] for r in k_refs[sub:sub + 3]] + [kx_ref[:, gs]], axis=0)
        else:
            kcat = kx_ref[:, gs]
        qp = q_ref[sub * blk:(sub + 1) * blk, pair * LANES:(pair + 1) * LANES]
        zero = jnp.zeros_like(qp)
        w = jnp.concatenate([jnp.where(first_head, qp, zero), jnp.where(first_head, zero, qp)], axis=0)
        return lax.dot_general(kcat, w, (((1,), (1,)), ((), ())), preferred_element_type=F32)

    def masked_max(sub, pair, s):
        if window:
            mask_prev, mask_next = band_masks(sub)
            parts = [jnp.where(mask_prev, s[0:blk], NEG_INF), s[blk:2 * blk],
                     jnp.where(mask_next, s[2 * blk:3 * blk], NEG_INF), s[3 * blk:]]
        else:
            parts = [s]
        m = sink_ref[pair:pair + 1, :]
        for part in parts:
            m = jnp.maximum(m, jnp.max(part, axis=0, keepdims=True))
        return parts, m

    def softmax(sub, pair, parts_m):
        parts, m = parts_m
        p = jnp.concatenate([jnp.exp(part - m).astype(BF16) for part in parts], axis=0)
        return p, jnp.exp(sink_ref[pair:pair + 1, :] - m)

    def output(sub, pair, p, sink_p):
        g = pair // 2
        vs = slice(g * HEAD_DIM, (g + 1) * HEAD_DIM)
        if window:
            vt = jnp.concatenate([r[vs, :] for r in v_refs[sub:sub + 3]] + [vx_ref[vs, :]], axis=1)
        else:
            vt = vx_ref[vs, :]
        vaug = jnp.concatenate([vt, ones], axis=0)
        o = jnp.dot(vaug, p, preferred_element_type=F32)
        on = o[0:HEAD_DIM, :] * (1.0 / (o[HEAD_DIM:HEAD_DIM + 1, :] + sink_p))
        ot = jnp.concatenate([on[:, 0:blk], on[:, blk:2 * blk]], axis=0)
        o_ref[sub * blk:(sub + 1) * blk, pair * LANES:(pair + 1) * LANES] = ot.T.astype(BF16)

    units = [(sub, pair) for sub in range(nq) for pair in range(ATT_Q_HEADS // 2)]
    stages = (lambda sub, pair, _: scores(sub, pair), masked_max, softmax,
              lambda sub, pair, ps: output(sub, pair, *ps))
    vals = {}
    for step in range(len(units) + len(stages) - 1):
        for k, stage in enumerate(stages):
            u = step - k
            if 0 <= u < len(units):
                vals[u, k] = stage(*units[u], vals.pop((u, k - 1), None))


def _attention(q, k, vt, kx, vxt, sink_tab, window, cast=()):
    b, s, _ = q.shape
    blk = ATT_BLOCK
    assert ATT_WINDOW == blk, "the band masks assume a window of exactly one key block on each side"
    nb = s // blk
    nq = ATT_QBLOCKS if nb % ATT_QBLOCKS == 0 else 1
    n_ctx = kx.shape[1]
    qspec = pl.BlockSpec((None, nq * blk, D_ATT), lambda bi, i: (bi, i, 0))
    kctx = pl.BlockSpec((None, n_ctx, 2 * LANES), lambda bi, i: (bi, 0, 0))
    vctx = pl.BlockSpec((None, ATT_KV_W, n_ctx), lambda bi, i: (bi, 0, 0))
    snk = pl.BlockSpec(sink_tab.shape, lambda bi, i: (0, 0))
    if window:
        at = lambda off: (lambda i: jnp.clip(nq * i + off, 0, nb - 1))
        kspec = lambda f: pl.BlockSpec((None, blk, 2 * LANES), lambda bi, i: (bi, f(i), 0))
        vspec = lambda f: pl.BlockSpec((None, ATT_KV_W, blk), lambda bi, i: (bi, 0, f(i)))
        offs = range(-1, nq + 1)
        in_specs = ([qspec] + [kspec(at(o)) for o in offs] + [vspec(at(o)) for o in offs]
                    + [kctx, vctx, snk])
        args = (q,) + (k,) * (nq + 2) + (vt,) * (nq + 2) + (kx, vxt, sink_tab)
    else:
        in_specs = [qspec, kctx, vctx, snk]
        args = (q, kx, vxt, sink_tab)
    steps = nb // nq
    out_shape, out_specs = [jax.ShapeDtypeStruct((b, s, D_ATT), BF16)], [qspec]
    for w in cast:
        rows = w.shape[0] // (b * steps)
        assert rows * b * steps == w.shape[0] and rows % (2 * SUBLANES) == 0, w.shape
        slab = pl.BlockSpec((rows, w.shape[1]), lambda bi, i: (bi * steps + i, 0))
        in_specs.append(slab)
        out_specs.append(slab)
        out_shape.append(jax.ShapeDtypeStruct(w.shape, BF16))
    res = pl.pallas_call(
        functools.partial(_attn_kernel, window=window, nq=nq, n_cast=len(cast)),
        out_shape=tuple(out_shape),
        grid=(b, steps),
        in_specs=in_specs,
        out_specs=tuple(out_specs),
        compiler_params=_params("parallel", "parallel"),
        name="attention_window" if window else "attention_ctx",
    )(*args, *cast)
    return res[0], tuple(res[1:])


def _ret_state_kernel(lgl_ref, k_ref, v_ref, s0_ref, r_ref, fin_ref, s_scr, *, cb, reverse):
    c = pl.program_id(1)
    L = RET_CHUNK

    @pl.when(c == 0)
    def _():
        s_scr[...] = s0_ref[...]

    jj = lax.broadcasted_iota(jnp.int32, (L, LANES), 0).astype(F32)
    expo = jj if reverse else (L - 1.0) - jj
    same_head = (lax.broadcasted_iota(jnp.int32, (LANES, LANES), 0) // HEAD_DIM
                 == lax.broadcasted_iota(jnp.int32, (LANES, LANES), 1) // HEAD_DIM)
    for pr in range(RET_HEADS // 2):
        cs = slice(pr * LANES, (pr + 1) * LANES)
        lgl = lgl_ref[pr]
        kdec = jnp.exp(expo * lgl)
        cdec = jnp.exp(float(L) * lgl)
        state = s_scr[pr]
        for t in range(cb):
            cc = cb - 1 - t if reverse else t
            rows = slice(cc * L, (cc + 1) * L)
            r_ref[cc, pr] = state.astype(BF16)
            kd = k_ref[rows, cs].astype(F32) * kdec
            u = jnp.dot(kd.T.astype(BF16), v_ref[rows, cs], preferred_element_type=F32)
            state = cdec * state + jnp.where(same_head, u, 0.0)
        s_scr[pr] = state

    @pl.when(c == pl.num_programs(1) - 1)
    def _():
        fin_ref[...] = s_scr[...]


def _ret_states(lgl, rk, rv, s0, reverse):
    b, s, _ = rk.shape
    nc = s // RET_CHUNK
    cb = _pick_tile(nc, 16)
    nblk = nc // cb
    npair = RET_HEADS // 2
    blk_idx = (lambda c: nblk - 1 - c) if reverse else (lambda c: c)
    tok = pl.BlockSpec((None, cb * RET_CHUNK, D_RET), lambda bi, c: (bi, blk_idx(c), 0))
    st = pl.BlockSpec((None, npair, LANES, LANES), lambda bi, c: (bi, 0, 0, 0))
    return pl.pallas_call(
        functools.partial(_ret_state_kernel, cb=cb, reverse=reverse),
        out_shape=(jax.ShapeDtypeStruct((b, nc, npair, LANES, LANES), BF16),
                   jax.ShapeDtypeStruct((b, npair, LANES, LANES), F32)),
        grid=(b, nblk),
        in_specs=[pl.BlockSpec(lgl.shape, lambda bi, c: (0, 0, 0)), tok, tok, st],
        out_specs=(pl.BlockSpec((None, cb, npair, LANES, LANES),
                                lambda bi, c: (bi, blk_idx(c), 0, 0, 0)), st),
        scratch_shapes=[pltpu.VMEM((npair, LANES, LANES), F32)],
        compiler_params=_params("parallel", "arbitrary"),
        name="ret_state_bwd" if reverse else "ret_state_fwd",
    )(lgl, rk, rv, s0)


def _ret_out_kernel(lgf_ref, lgb_ref, q_ref, k_ref, v_ref, g_ref, rf_ref, rb_ref, gnw_ref, o_ref, *, cb):
    L = RET_CHUNK
    i0 = lax.broadcasted_iota(jnp.int32, (L, LANES), 0).astype(F32)
    i1 = lax.broadcasted_iota(jnp.int32, (L, LANES), 1).astype(F32)
    diff = i0 - i1
    lo = lax.broadcasted_iota(jnp.int32, (L, LANES), 1) < HEAD_DIM
    grp = _head_group_matrix()
    inv = 1.0 / HEAD_DIM
    tabs = []
    for pr in range(RET_HEADS // 2):
        lgf, lgb = lgf_ref[pr], lgb_ref[pr]
        dmat = [jnp.where(diff >= 0.0,
                          jnp.exp(jnp.maximum(diff, 0.0) * lgf[:, a:a + 1]),
                          jnp.exp(jnp.maximum(-diff, 0.0) * lgb[:, a:a + 1]))
                for a in (0, HEAD_DIM)]
        tabs.append((jnp.exp((i0 + 1.0) * lgf), jnp.exp((float(L) - i0) * lgb),
                     jnp.concatenate(dmat, axis=0), gnw_ref[:, pr * LANES:(pr + 1) * LANES]))

    def decayed_scores(t, pr):
        rows = pl.ds(pl.multiple_of(t * L, L), L)
        cs = slice(pr * LANES, (pr + 1) * LANES)
        qp, kp = q_ref[rows, cs], k_ref[rows, cs]
        zero = jnp.zeros_like(kp)
        qz = jnp.concatenate([jnp.where(lo, qp, zero), jnp.where(lo, zero, qp)], axis=0)
        return lax.dot_general(qz, kp, (((1,), (1,)), ((), ())), preferred_element_type=F32) * tabs[pr][2]

    def mix(t, pr, a):
        rows = pl.ds(pl.multiple_of(t * L, L), L)
        cs = slice(pr * LANES, (pr + 1) * LANES)
        xif, xib, _, _ = tabs[pr]
        vp = v_ref[rows, cs]
        zero = jnp.zeros_like(vp)
        qf = q_ref[rows, cs].astype(F32)
        lhs = jnp.concatenate([a[0:L].astype(BF16), a[L:].astype(BF16),
                               (qf * xif).astype(BF16), (qf * xib).astype(BF16)], axis=1)
        rhs = jnp.concatenate([jnp.where(lo, vp, zero), jnp.where(lo, zero, vp),
                               rf_ref[t, pr], rb_ref[t, pr]], axis=0)
        return jnp.dot(lhs, rhs, preferred_element_type=F32)

    def centre(t, pr, acc):
        return acc - _per_head_sum(acc, grp) * inv

    def readout(t, pr, dl):
        rows = pl.ds(pl.multiple_of(t * L, L), L)
        cs = slice(pr * LANES, (pr + 1) * LANES)
        var = _per_head_sum(dl * dl, grp) * inv
        y = dl * lax.rsqrt(var + EPS) * tabs[pr][3]
        gt = g_ref[rows, cs]
        o_ref[rows, cs] = (gt * _sigmoid(gt) * y).astype(BF16)

    unroll = min(RET_UNROLL, cb)
    stages = (lambda t, pr, _: decayed_scores(t, pr), mix, centre, readout)

    def group(gi, carry):
        units = [(gi * unroll + u, pr) for u in range(unroll) for pr in range(RET_HEADS // 2)]
        vals = {}
        for step in range(len(units) + len(stages) - 1):
            for k, stage in enumerate(stages):
                u = step - k
                if 0 <= u < len(units):
                    vals[u, k] = stage(*units[u], vals.pop((u, k - 1), None))
        return carry

    lax.fori_loop(0, cb // unroll, group, 0)


def _ret_out(lgf, lgb, rq, rk, rv, rg, rf, rb, gnw):
    b, s, _ = rq.shape
    nc = s // RET_CHUNK
    cb = _pick_tile(nc, 8)
    npair = RET_HEADS // 2
    tok = pl.BlockSpec((None, cb * RET_CHUNK, D_RET), lambda bi, c: (bi, c, 0))
    st = pl.BlockSpec((None, cb, npair, LANES, LANES), lambda bi, c: (bi, c, 0, 0, 0))
    lg = pl.BlockSpec(lgf.shape, lambda bi, c: (0, 0, 0))
    return pl.pallas_call(
        functools.partial(_ret_out_kernel, cb=cb),
        out_shape=jax.ShapeDtypeStruct((b, s, D_RET), BF16),
        grid=(b, nc // cb),
        in_specs=[lg, lg, tok, tok, tok, tok, st, st, pl.BlockSpec((1, D_RET), lambda bi, c: (0, 0))],
        out_specs=tok,
        compiler_params=_params("parallel", "parallel"),
        name="ret_out",
    )(lgf, lgb, rq, rk, rv, rg, rf, rb, gnw)


def _conv_kernel(prev_ref, cur_ref, next_ref, w_ref, b_ref, lnw_ref, lnb_ref, o_ref,
                 xpad, shifted, hbuf, wtile):
    i = pl.program_id(1)
    tm = cur_ref.shape[0]
    halo = CONV_HALO
    first = i == 0
    last = i == pl.num_programs(1) - 1
    xpad[0:halo, :] = jnp.where(first, 0.0, prev_ref[...])
    xpad[halo:halo + tm, :] = cur_ref[...]
    xpad[halo + tm:2 * halo + tm, :] = jnp.where(last, 0.0, next_ref[...])
    span = tm + 2 * halo - SUBLANES
    for r in range(SUBLANES):
        shifted[r] = xpad[r:r + span, :]
    base = halo - CONV_WIDTH // 2
    bias = b_ref[...]
    ch = cur_ref.shape[1]
    for w in range(CONV_WIDTH):
        wtile[w] = jnp.broadcast_to(w_ref[w:w + 1, :], (SUBLANES, ch))

    taps_by_shift = {}
    for w in range(CONV_WIDTH):
        taps_by_shift.setdefault((base + w) % SUBLANES, []).append(((base + w) // SUBLANES, w))

    def sub(sb, carry):
        r0 = pl.multiple_of(sb * CONV_SUB, CONV_SUB)
        chains = [None] * CONV_CHAINS
        groups = CONV_SUB // SUBLANES
        for r, taps in sorted(taps_by_shift.items()):
            a_lo = min(a for a, _ in taps)
            n_g = max(a for a, _ in taps) - a_lo + groups
            win = shifted[r, pl.ds(r0 + a_lo * SUBLANES, n_g * SUBLANES), :].reshape(n_g, SUBLANES, ch)
            for a, w in taps:
                term = win[a - a_lo:a - a_lo + groups] * wtile[w]
                c = w % CONV_CHAINS
                chains[c] = term if chains[c] is None else chains[c] + term
        total = (chains[0] + chains[1]) + (chains[2] + chains[3])
        hbuf[pl.ds(r0, CONV_SUB), :] = total.reshape(CONV_SUB, ch) + bias
        return carry

    lax.fori_loop(0, tm // CONV_SUB, sub, 0)
    h = hbuf[...]
    mu = jnp.mean(h, axis=-1, keepdims=True)
    dl = h - mu
    var = jnp.mean(dl * dl, axis=-1, keepdims=True)
    y = dl * lax.rsqrt(var + EPS) * lnw_ref[...] + lnb_ref[...]
    o_ref[...] = (y * _sigmoid(y)).astype(BF16)


def _conv(cv, w, bias, lnw, lnb):
    b, s, ch = cv.shape
    tm = _pick_tile(s, 1024)
    hpb = tm // CONV_HALO
    nh = s // CONV_HALO
    row = pl.BlockSpec((1, ch), lambda bi, i: (0, 0))
    span = tm + 2 * CONV_HALO - SUBLANES
    return pl.pallas_call(
        _conv_kernel,
        out_shape=jax.ShapeDtypeStruct((b, s, ch), BF16),
        grid=(b, s // tm),
        in_specs=[pl.BlockSpec((None, CONV_HALO, ch), lambda bi, i: (bi, jnp.maximum(i * hpb - 1, 0), 0)),
                  pl.BlockSpec((None, tm, ch), lambda bi, i: (bi, i, 0)),
                  pl.BlockSpec((None, CONV_HALO, ch),
                               lambda bi, i: (bi, jnp.minimum((i + 1) * hpb, nh - 1), 0)),
                  pl.BlockSpec(w.shape, lambda bi, i: (0, 0)), row, row, row],
        out_specs=pl.BlockSpec((None, tm, ch), lambda bi, i: (bi, i, 0)),
        scratch_shapes=[pltpu.VMEM((tm + 2 * CONV_HALO, ch), F32),
                        pltpu.VMEM((SUBLANES, span, ch), F32),
                        pltpu.VMEM((tm, ch), F32),
                        pltpu.VMEM((CONV_WIDTH, SUBLANES, ch), F32)],
        compiler_params=_params("parallel", "parallel"),
        name="conv",
    )(cv, cv, cv, w, bias, lnw, lnb)


def _outproj_route_kernel(att_ref, ret_ref, cnv_ref, w_ref, x_ref, g1_ref, nw_ref, sc_ref, sh_ref, rcat_ref,
                          xo_ref, h_ref, rt_ref):
    y = (jnp.dot(att_ref[...], w_ref[0:D_ATT, :], preferred_element_type=F32)
         + jnp.dot(ret_ref[...], w_ref[D_ATT:D_ATT + D_RET, :], preferred_element_type=F32)
         + jnp.dot(cnv_ref[...], w_ref[D_ATT + D_RET:, :], preferred_element_type=F32))
    xn = x_ref[...] + g1_ref[...] * y
    xo_ref[...] = xn
    h = _modulated_rms(xn, nw_ref[...], sc_ref[...], sh_ref[...])
    h_ref[...] = _pack_bf16_pairs(h)
    hi = h.astype(BF16)
    lo = (h - hi.astype(F32)).astype(BF16)
    tm = h.shape[0]
    r = jnp.dot(jnp.concatenate([hi, lo], axis=0), rcat_ref[...], preferred_element_type=F32)
    logits = (r[0:tm, 0:LANES] + r[0:tm, LANES:]) + (r[tm:, 0:LANES] + r[tm:, LANES:])
    lane = lax.broadcasted_iota(jnp.int32, logits.shape, 1).astype(F32)
    logits = jnp.where(lane < N_EXPERTS, logits, NEG_INF)
    m1 = jnp.max(logits, axis=-1, keepdims=True)
    i1 = jnp.min(jnp.where(logits == m1, lane, float(LANES)), axis=-1, keepdims=True)
    rest = jnp.where(lane == i1, NEG_INF, logits)
    m2 = jnp.max(rest, axis=-1, keepdims=True)
    i2 = jnp.min(jnp.where(rest == m2, lane, float(LANES)), axis=-1, keepdims=True)
    e2 = jnp.exp(m2 - m1)
    w1 = 1.0 / (1.0 + e2)
    w2 = e2 / (1.0 + e2)
    rt_ref[...] = jnp.where(lane == 0.0, i1,
                            jnp.where(lane == 1.0, i2,
                                      jnp.where(lane == 2.0, w1, jnp.where(lane == 3.0, w2, 0.0))))


def _outproj_route(att, ret, cnv, w_bf, x, g1, nw, sc, sh, router):
    b, s, d = x.shape
    tm = _pick_tile(s, 1024)
    tok = lambda n: pl.BlockSpec((None, tm, n), lambda bi, i: (bi, i, 0))
    per_b = pl.BlockSpec((None, 1, d), lambda bi, i: (bi, 0, 0))
    rpad = jnp.zeros((d, LANES), F32).at[:, :N_EXPERTS].set(router)
    rhi = rpad.astype(BF16)
    rlo = (rpad - rhi.astype(F32)).astype(BF16)
    return pl.pallas_call(
        _outproj_route_kernel,
        out_shape=(jax.ShapeDtypeStruct((b, s, d), F32), jax.ShapeDtypeStruct((b, s, d // 2), jnp.int32),
                   jax.ShapeDtypeStruct((b, s, LANES), F32)),
        grid=(b, s // tm),
        in_specs=[tok(D_ATT), tok(D_RET), tok(CONV_CH), pl.BlockSpec(w_bf.shape, lambda bi, i: (0, 0)),
                  tok(d), per_b, pl.BlockSpec((1, d), lambda bi, i: (0, 0)), per_b, per_b,
                  pl.BlockSpec((d, 2 * LANES), lambda bi, i: (0, 0))],
        out_specs=(tok(d), tok(d // 2), tok(LANES)),
        compiler_params=_params("parallel", "parallel"),
        name="outproj_route",
    )(att, ret, cnv, w_bf, x, g1, nw, sc, sh, jnp.concatenate([rhi, rlo], axis=1))


def _swiglu_chunks(h, wg_ref, wu_ref, wd_ref, sub):
    total = None
    for c0 in range(0, wg_ref.shape[1], sub):
        cs = slice(c0, min(c0 + sub, wg_ref.shape[1]))
        gate = jnp.dot(h, wg_ref[:, cs].astype(BF16), preferred_element_type=F32)
        up = jnp.dot(h, wu_ref[:, cs].astype(BF16), preferred_element_type=F32)
        act = (gate * _sigmoid(gate) * up).astype(BF16)
        part = jnp.dot(act, wd_ref[cs, :].astype(BF16), preferred_element_type=F32)
        total = part if total is None else total + part
    return total


def _mixer_ffn_kernel(att_ref, ret_ref, cnv_ref, wo_ref, x_ref, g1_ref, nw_ref, sc_ref, sh_ref,
                      wg_ref, wu_ref, wd_ref, g2_ref, o_ref):
    y = (jnp.dot(att_ref[...], wo_ref[0:D_ATT, :], preferred_element_type=F32)
         + jnp.dot(ret_ref[...], wo_ref[D_ATT:D_ATT + D_RET, :], preferred_element_type=F32)
         + jnp.dot(cnv_ref[...], wo_ref[D_ATT + D_RET:, :], preferred_element_type=F32))
    xn = x_ref[...] + g1_ref[...] * y
    h = _modulated_rms(xn, nw_ref[...], sc_ref[...], sh_ref[...]).astype(BF16)
    o_ref[...] = xn + g2_ref[...] * _swiglu_chunks(h, wg_ref, wu_ref, wd_ref, FFN_SUB)


def _mixer_ffn(att, ret, cnv, wo, x, g1, nw, sc, sh, wg, wu, wd, g2):
    b, s, d = x.shape
    f = wg.shape[1]
    tm = _pick_tile(s, 512)
    tok = lambda n: pl.BlockSpec((None, tm, n), lambda bi, i: (bi, i, 0))
    per_b = pl.BlockSpec((None, 1, d), lambda bi, i: (bi, 0, 0))
    res = lambda shape: pl.BlockSpec(shape, lambda bi, i: (0, 0), pipeline_mode=pl.Buffered(1))
    return pl.pallas_call(
        _mixer_ffn_kernel,
        out_shape=jax.ShapeDtypeStruct((b, s, d), F32),
        grid=(b, s // tm),
        in_specs=[tok(D_ATT), tok(D_RET), tok(CONV_CH), res(wo.shape), tok(d), per_b,
                  pl.BlockSpec((1, d), lambda bi, i: (0, 0)), per_b, per_b,
                  res((d, f)), res((d, f)), res((f, d)), per_b],
        out_specs=tok(d),
        compiler_params=_params("parallel", "parallel"),
        name="mixer_ffn_dense",
    )(att, ret, cnv, wo, x, g1, nw, sc, sh, wg, wu, wd, g2)


def _gather_rows(idx, src):
    n = idx.shape[0]
    d = src.shape[1]
    mesh = plsc.VectorSubcoreMesh(core_axis_name="core", subcore_axis_name="subcore")

    @pl.kernel(out_type=jax.ShapeDtypeStruct((n, d), src.dtype), mesh=mesh, name="moe_gather")
    def gather(src_hbm, idx_hbm, out_hbm):
        def body(idx_vmem, out_vmem):
            pltpu.sync_copy(src_hbm.at[idx_vmem.at[0, pl.ds(0, SC_WINDOW)]], out_vmem)

        pltpu.emit_pipeline(
            body,
            grid=(n // SC_WINDOW,),
            in_specs=[pl.BlockSpec((1, LANES), lambda i: (i, 0))],
            out_specs=[pl.BlockSpec((SC_WINDOW, d), lambda i: (i, 0))],
            core_axis_name=("core", "subcore"),
            dimension_semantics=(pltpu.PARALLEL,),
        )(idx_hbm, out_hbm)

    idx_rows = jnp.pad(idx.reshape(n // SC_WINDOW, SC_WINDOW), ((0, 0), (0, LANES - SC_WINDOW)))
    return gather(src, idx_rows)


def _invert_rows(dest, n_rows):
    n = dest.shape[0]
    n_tok = n // 2
    mesh = plsc.VectorSubcoreMesh(core_axis_name="core", subcore_axis_name="subcore")

    @pl.kernel(out_type=jax.ShapeDtypeStruct((n_rows,), jnp.int32), mesh=mesh, name="moe_row_src",
               scratch_types=[pltpu.VMEM((n,), jnp.int32), pltpu.VMEM((n_rows,), jnp.int32)],
               compiler_params=pltpu.CompilerParams(needs_layout_passes=False))
    def invert(dest_hbm, out_hbm, dest_vmem, rows_vmem):
        @pl.when((lax.axis_index("core") == 0) & (lax.axis_index("subcore") == 0))
        def _():
            pltpu.sync_copy(dest_hbm, dest_vmem)

            @pl.loop(0, n_rows, step=SC_LANES)
            def _(r):
                rows_vmem[pl.ds(r, SC_LANES)] = lax.rem(lax.iota(jnp.int32, SC_LANES) + r, n_tok)

            @pl.loop(0, n, step=SC_LANES)
            def _(a):
                tok = lax.iota(jnp.int32, SC_LANES) + a
                tok = jnp.where(tok >= n_tok, tok - n_tok, tok)
                plsc.store_scatter(rows_vmem, [dest_vmem[pl.ds(a, SC_LANES)]], tok)

            pltpu.sync_copy(rows_vmem, out_hbm)

    return invert(dest)


def _moe_kernel(te_ref, nu_ref, x_ref, wg_ref, wu_ref, wd_ref, *rest, tile0, nj):
    o_ref, xb, acc = rest[-3:]
    t = pl.program_id(0) + tile0
    j = pl.program_id(1)
    used = t < nu_ref[0]

    def partial_sum():
        return _swiglu_chunks(xb[...], wg_ref, wu_ref, wd_ref, MOE_SUB)

    @pl.when(used & (j == 0))
    def _():
        lo, hi = _unpack_bf16_pairs(x_ref[...])
        half = lo.shape[1]
        xb[:, 0:half] = lo.astype(BF16)
        xb[:, half:] = hi.astype(BF16)
        if nj == 1:
            o_ref[...] = _pack_bf16_pairs(partial_sum())
        else:
            acc[...] = partial_sum()

    if nj > 2:
        @pl.when(used & (j > 0) & (j < nj - 1))
        def _():
            acc[...] += partial_sum()

    if nj > 1:
        @pl.when(used & (j == nj - 1))
        def _():
            o_ref[...] = _pack_bf16_pairs(acc[...] + partial_sum())

    @pl.when(jnp.logical_not(used) & (j == nj - 1))
    def _():
        o_ref[...] = jnp.zeros_like(o_ref)


def _moe_grouped(tile_expert, n_used, xs, wg, wu, wd, y_prev, tile0, n_rows):
    p, dp = xs.shape
    d = 2 * dp
    f = wg.shape[2]
    tm = MOE_TM
    tf = MOE_TF
    nj = f // tf

    def jj(t, j, te, nu):
        return jnp.where(t + tile0 < nu[0], j, nj - 1)

    in_specs = [pl.BlockSpec((tm, dp), lambda t, j, te, nu: (t, 0)),
                pl.BlockSpec((None, d, tf), lambda t, j, te, nu: (te[t + tile0], 0, jj(t, j, te, nu))),
                pl.BlockSpec((None, d, tf), lambda t, j, te, nu: (te[t + tile0], 0, jj(t, j, te, nu))),
                pl.BlockSpec((None, tf, d), lambda t, j, te, nu: (te[t + tile0], jj(t, j, te, nu), 0))]
    args = [tile_expert, n_used, xs, wg, wu, wd]
    aliases = {}
    if y_prev is not None:
        in_specs.append(pl.BlockSpec(memory_space=pl.ANY))
        args.append(y_prev)
        aliases = {len(args) - 1: 0}
    return pl.pallas_call(
        functools.partial(_moe_kernel, tile0=tile0, nj=nj),
        out_shape=jax.ShapeDtypeStruct((n_rows, dp), jnp.int32),
        grid_spec=pltpu.PrefetchScalarGridSpec(
            num_scalar_prefetch=2,
            grid=(p // tm, nj),
            in_specs=in_specs,
            out_specs=pl.BlockSpec((tm, dp), lambda t, j, te, nu: (t + tile0, 0)),
            scratch_shapes=[pltpu.VMEM((tm, d), BF16), pltpu.VMEM((tm, d), F32)]),
        input_output_aliases=aliases,
        compiler_params=_params("arbitrary", "arbitrary"),
        name="moe_grouped",
    )(*args)


def _combine_kernel(y1_ref, y2_ref, x_ref, g2_ref, rt_ref, o_ref):
    rt = rt_ref[...]
    w1, w2 = rt[:, 2:3], rt[:, 3:4]
    half = y1_ref.shape[1]
    for k, (a, c) in enumerate(zip(_unpack_bf16_pairs(y1_ref[...]), _unpack_bf16_pairs(y2_ref[...]))):
        cs = slice(k * half, (k + 1) * half)
        o_ref[:, cs] = x_ref[:, cs] + g2_ref[:, cs] * (w1 * a + w2 * c)


def _moe_combine(yg, x, g2, route):
    b, s, d = x.shape
    tm = _pick_tile(s, 512)
    tok = lambda n: pl.BlockSpec((None, tm, n), lambda bi, i: (bi, i, 0))
    return pl.pallas_call(
        _combine_kernel,
        out_shape=jax.ShapeDtypeStruct((b, s, d), F32),
        grid=(b, s // tm),
        in_specs=[tok(d // 2), pl.BlockSpec((None, tm, d // 2), lambda bi, i: (b + bi, i, 0)), tok(d),
                  pl.BlockSpec((None, 1, d), lambda bi, i: (bi, 0, 0)), tok(LANES)],
        out_specs=tok(d),
        compiler_params=_params("parallel", "parallel"),
        name="moe_combine",
    )(yg, yg, x, g2, route)


def _moe(h, route, x, g2, wg, wu, wd):
    b, s, d = x.shape
    n_tok = b * s
    tm = MOE_TM
    rt = route.reshape(n_tok, LANES)
    flat_e = jnp.concatenate([rt[:, 0], rt[:, 1]]).astype(jnp.int32)
    onehot = (jnp.arange(N_EXPERTS, dtype=jnp.int32)[:, None] == flat_e[None, :]).astype(jnp.int32)
    csum = jnp.cumsum(onehot, axis=1)
    rank = jnp.sum(csum * onehot, axis=0) - 1
    counts = csum[:, -1]
    tiles_e = (counts + tm - 1) // tm
    tiles_cum = jnp.cumsum(tiles_e)
    row_start = (tiles_cum - tiles_e) * tm
    dest = jnp.sum(onehot * row_start[:, None], axis=0) + rank
    n_tiles = 2 * n_tok // tm + N_EXPERTS
    tile_ids = jnp.arange(n_tiles, dtype=jnp.int32)
    tile_expert = jnp.sum((tile_ids[:, None] >= tiles_cum[None, :]).astype(jnp.int32), axis=1)
    last_e = jnp.max(jnp.where(tiles_e > 0, jnp.arange(N_EXPERTS, dtype=jnp.int32), 0))
    tile_expert = jnp.minimum(tile_expert, last_e).astype(jnp.int32)
    n_used = tiles_cum[-1:].astype(jnp.int32)
    dest = dest.astype(jnp.int32)
    row_src = _invert_rows(dest, n_tiles * tm)
    n_chunks = max(c for c in range(1, MOE_CHUNKS + 1) if n_tiles % c == 0)
    tiles_c = n_tiles // n_chunks
    h_flat = h.reshape(n_tok, d // 2)
    xs = [_gather_rows(row_src[c * tiles_c * tm:(c + 1) * tiles_c * tm], h_flat) for c in range(n_chunks)]
    y = None
    for c in range(n_chunks):
        y = _moe_grouped(tile_expert, n_used, xs[c], wg, wu, wd, y, c * tiles_c, n_tiles * tm)
    yg = _gather_rows(dest, y).reshape(2 * b, s, d // 2)
    return _moe_combine(yg, x, g2, route)


def _rope_tables(n):
    rows = n // GRID_W
    r = np.repeat(np.arange(rows), GRID_W).astype(np.float64)
    col = np.tile(np.arange(GRID_W), rows).astype(np.float64)
    freqs = ROPE_BASE ** (-np.arange(ROPE_FREQS, dtype=np.float64) / ROPE_FREQS)
    ang = np.stack([r[:, None] * freqs, col[:, None] * freqs], axis=1)
    ang = np.repeat(ang[:, :, None, :], 2, axis=2).reshape(n, HEAD_DIM)
    ang = np.tile(ang, (1, LANES // HEAD_DIM))
    cos, sin = np.cos(ang), np.sin(ang)
    first_half = (np.arange(LANES) % (2 * ROPE_FREQS)) < ROPE_FREQS
    to_dev = lambda t: jnp.asarray(t.astype(np.float32))
    return to_dev(cos), to_dev(np.where(first_half, -sin, 0.0)), to_dev(np.where(first_half, 0.0, sin))


def _lane_rows(lg):
    return jnp.repeat(lg.astype(F32), HEAD_DIM).reshape(RET_HEADS // 2, 1, LANES)


def kernel(x, c, ctx, c_ctx, ada_w, ada_b, norm1_w, norm2_w, w_in, w_out, q_norm_w, k_norm_w,
           attn_sink, ret_decay_f, ret_decay_b, ret_gn_w, conv_w, conv_b, conv_ln_w, conv_ln_b,
           ffn_w_gate, ffn_w_up, ffn_w_down, router_w, moe_w_gate, moe_w_up, moe_w_down):
    b, n, d = x.shape
    n_ctx = ctx.shape[1]
    depth = ada_w.shape[0]
    cond = jnp.zeros((SUBLANES, d), F32).at[0:b].set(c).at[b].set(c_ctx)
    mods = _adaln(cond, ada_w, ada_b).reshape(depth, SUBLANES, 6, d)
    cos, sa, sb = _rope_tables(n)
    ones_c = jnp.ones((n_ctx, LANES), F32)
    zeros_c = jnp.zeros((n_ctx, LANES), F32)
    zero_state = jnp.zeros((b, RET_HEADS // 2, LANES, LANES), F32)
    row = lambda v: v.reshape(1, -1)
    slab_rows = b * (n // ATT_BLOCK // ATT_QBLOCKS) * BF16_SUBLANES
    riders = {l: {} for l in range(depth)}

    def ride(l, key, w):
        w2 = w.reshape(-1, w.shape[-1])
        if 0 <= l < depth and (n // ATT_BLOCK) % ATT_QBLOCKS == 0 and w2.shape[0] % slab_rows == 0:
            riders[l][key] = w2

    for j, w in enumerate((moe_w_gate, moe_w_up, moe_w_down)):
        ride(j * min(depth, 2) // 3, ("moe", j), w)
    for l in range(depth):
        if l % 2 == 0:
            ride(l, ("ffn", l, 0), ffn_w_gate[l // 2])
            ride(l, ("ffn", l, 1), ffn_w_up[l // 2])
            ride(l, ("ffn", l, 2), ffn_w_down[l // 2])
        ride(l - 1, ("w_in", l), w_in[l])
        ride(l - 1, ("w_out", l), w_out[l])
    rounded = {}

    def bf16_of(key, w):
        return rounded[key].reshape(w.shape) if key in rounded else w.astype(BF16)

    for l in range(depth):
        last = l == depth - 1
        m_lat = [mods[l, 0:b, k][:, None, :] for k in range(6)]
        m_ctx = [jnp.broadcast_to(mods[l, b, k][None, None, :], (b, 1, d)) for k in range(6)]
        w_in_bf = bf16_of(("w_in", l), w_in[l])
        w_out_bf = bf16_of(("w_out", l), w_out[l])
        qw = row(jnp.tile(q_norm_w[l], LANES // HEAD_DIM))
        kw = row(jnp.tile(k_norm_w[l], LANES // HEAD_DIM))
        lgf = _lane_rows(jax.nn.log_sigmoid(ret_decay_f[l].astype(F32)))
        lgb = _lane_rows(jax.nn.log_sigmoid(ret_decay_b[l].astype(F32)))
        sink_tab = jnp.repeat(attn_sink[l].astype(F32), ATT_BLOCK).reshape(ATT_Q_HEADS // 2, 2 * ATT_BLOCK)

        q, k, v, rk, rv, rq, rg, cv = _inproj(x, row(norm1_w[l]), m_lat[1], m_lat[0], w_in_bf,
                                               cos, sa, sb, qw, kw)
        qc, kc, vc, rkc, rvc, rqc, rgc, cvc = _inproj(ctx, row(norm1_w[l]), m_ctx[1], m_ctx[0], w_in_bf,
                                                       ones_c, zeros_c, zeros_c, qw, kw)
        rf_c, s_f = _ret_states(lgf, rkc, rvc, zero_state, reverse=False)
        rb_c, s_b = _ret_states(lgb, rkc, rvc, zero_state, reverse=True)
        rf, _ = _ret_states(lgf, rk, rv, s_f, reverse=False)
        rb, _ = _ret_states(lgb, rk, rv, s_b, reverse=True)

        att, cast_out = _attention(q, k, v, kc, vc, sink_tab, window=True, cast=list(riders[l].values()))
        rounded.update(zip(riders[l].keys(), cast_out))
        ret = _ret_out(lgf, lgb, rq, rk, rv, rg, rf, rb, row(ret_gn_w[l]))
        cnv = _conv(cv, conv_w[l], row(conv_b[l]), row(conv_ln_w[l]), row(conv_ln_b[l]))

        if l % 2 == 0:
            i = l // 2
            wg, wu, wd = (bf16_of(("ffn", l, j), w[i]) for j, w in enumerate((ffn_w_gate, ffn_w_up, ffn_w_down)))
            x_new = _mixer_ffn(att, ret, cnv, w_out_bf, x, m_lat[2], row(norm2_w[l]), m_lat[4], m_lat[3],
                               wg, wu, wd, m_lat[5])
        else:
            i = l // 2
            wg, wu, wd = (bf16_of(("moe", j), w)[i] for j, w in enumerate((moe_w_gate, moe_w_up, moe_w_down)))
            x_mid, h2, route = _outproj_route(att, ret, cnv, w_out_bf, x, m_lat[2], row(norm2_w[l]),
                                              m_lat[4], m_lat[3], router_w[i])
            x_new = _moe(h2, route, x_mid, m_lat[5], wg, wu, wd)

        if not last:
            att_c, _ = _attention(qc, None, None, kc, vc, sink_tab, window=False)
            ret_c = _ret_out(lgf, lgb, rqc, rkc, rvc, rgc, rf_c, rb_c, row(ret_gn_w[l]))
            cnv_c = _conv(cvc, conv_w[l], row(conv_b[l]), row(conv_ln_w[l]), row(conv_ln_b[l]))
            if l % 2 == 0:
                ctx = _mixer_ffn(att_c, ret_c, cnv_c, w_out_bf, ctx, m_ctx[2], row(norm2_w[l]), m_ctx[4],
                                 m_ctx[3], wg, wu, wd, m_ctx[5])
            else:
                c_mid, h2c, route_c = _outproj_route(att_c, ret_c, cnv_c, w_out_bf, ctx, m_ctx[2],
                                                     row(norm2_w[l]), m_ctx[4], m_ctx[3], router_w[i])
                ctx = _moe(h2c, route_c, c_mid, m_ctx[5], wg, wu, wd)
        x = x_new
    return x
```

```python
import functools

import jax
import jax.numpy as jnp
import numpy as np
from jax import lax
from jax.experimental import pallas as pl
from jax.experimental.pallas import tpu as pltpu
from jax.experimental.pallas import tpu_sc as plsc

F32 = jnp.float32
BF16 = jnp.bfloat16

GRID_W = 64
HEAD_DIM = 64
ATT_Q_HEADS = 8
ATT_KV_HEADS = 2
ATT_WINDOW = 128
ATT_BLOCK = 128
RET_HEADS = 4
RET_CHUNK = 128
RET_K_SCALE = HEAD_DIM ** -0.5
ATT_SCALE = HEAD_DIM ** -0.5
CONV_CH = 256
CONV_WIDTH = 31
ROPE_BASE = 10000.0
ROPE_FREQS = HEAD_DIM // 4
D_ATT = ATT_Q_HEADS * HEAD_DIM
D_RET = RET_HEADS * HEAD_DIM
ATT_KV_W = ATT_KV_HEADS * HEAD_DIM
C_ATT_K = 0
C_ATT_V = C_ATT_K + ATT_KV_W
C_RET_K = C_ATT_V + ATT_KV_W
C_RET_V = C_RET_K + D_RET
C_ATT_Q = C_RET_V + D_RET
C_RET_Q = C_ATT_Q + D_ATT
C_RET_G = C_RET_Q + D_RET
C_CONV = C_RET_G + D_RET
N_EXPERTS = 8
EPS = 1e-6
NEG_INF = -1e30

LANES = 128
SUBLANES = 8
BF16_SUBLANES = 16
VMEM_LIMIT = 48 * 1024 * 1024
CONV_HALO = 16
CONV_SUB = 32
CONV_CHAINS = 4
ATT_QBLOCKS = 4
RET_UNROLL = 4
XRING_SLOTS = 3
MOE_TM = 512
MOE_CHUNKS = 2
MOE_SUB = 512
FFN_SUB = 512
MOE_TF = 1792
SC_LANES = 16
SC_WINDOW = 64


def _params(*sem):
    return pltpu.CompilerParams(dimension_semantics=sem, vmem_limit_bytes=VMEM_LIMIT)


def _sigmoid(x):
    return 1.0 / (1.0 + jnp.exp(-x))


def _pack_bf16_pairs(v):
    c = v.shape[1] // 2
    bits = pltpu.bitcast(v.astype(BF16).astype(F32), jnp.uint32)
    packed = (bits[:, c:] & jnp.uint32(0xFFFF0000)) | (bits[:, :c] >> 16)
    return pltpu.bitcast(packed, jnp.int32)


def _unpack_bf16_pairs(p):
    bits = pltpu.bitcast(p, jnp.uint32)
    return pltpu.bitcast(bits << 16, F32), pltpu.bitcast(bits & jnp.uint32(0xFFFF0000), F32)


def _pick_tile(n, pref):
    t = min(n, pref)
    assert n % t == 0, (n, t)
    return t


def _adaln_kernel(c_ref, w_ref, b_ref, o_ref):
    c = c_ref[...]
    s = c * _sigmoid(c)
    w = w_ref[...]
    s_hi, w_hi = s.astype(BF16), w.astype(BF16)
    s_lo = (s - s_hi.astype(F32)).astype(BF16)
    w_lo = (w - w_hi.astype(F32)).astype(BF16)
    dot = functools.partial(jnp.dot, preferred_element_type=F32)
    o_ref[...] = (dot(s_hi, w_hi) + dot(s_lo, w_hi)) + dot(s_hi, w_lo) + b_ref[...]


def _adaln(cond, ada_w, ada_b):
    depth, d, n = ada_w.shape
    tn = _pick_tile(n, 3072)
    return pl.pallas_call(
        _adaln_kernel,
        out_shape=jax.ShapeDtypeStruct((depth, cond.shape[0], n), F32),
        grid=(depth, n // tn),
        in_specs=[pl.BlockSpec(cond.shape, lambda l, j: (0, 0)),
                  pl.BlockSpec((None, d, tn), lambda l, j: (l, 0, j)),
                  pl.BlockSpec((None, 1, tn), lambda l, j: (l, 0, j))],
        out_specs=pl.BlockSpec((None, cond.shape[0], tn), lambda l, j: (l, 0, j)),
        compiler_params=_params("parallel", "parallel"),
        name="adaln",
    )(cond, ada_w, ada_b.reshape(depth, 1, n))


def _modulated_rms(x, nw, sc, sh):
    ms = jnp.mean(x * x, axis=-1, keepdims=True)
    return (x * lax.rsqrt(ms + EPS) * nw) * (1.0 + sc) + sh


def _head_group_matrix():
    r = lax.broadcasted_iota(jnp.int32, (2 * LANES, 2 * LANES), 0) // HEAD_DIM
    c = lax.broadcasted_iota(jnp.int32, (2 * LANES, 2 * LANES), 1) // HEAD_DIM
    return jnp.where(r == c, 1.0, 0.0).astype(BF16)


def _per_head_sum(v, grp):
    hi = v.astype(BF16)
    lo = (v - hi.astype(F32)).astype(BF16)
    r = jnp.dot(jnp.concatenate([hi, lo], axis=1), grp, preferred_element_type=F32)
    return r[:, 0:LANES] + r[:, LANES:]


def _dup_halves(t):
    sw = pltpu.roll(t, HEAD_DIM, 1)
    lo = lax.broadcasted_iota(jnp.int32, t.shape, 1) < HEAD_DIM
    return jnp.where(lo, t, sw), jnp.where(lo, sw, t)


def _inproj_kernel(x_ref, nw_ref, sc_ref, sh_ref, w_ref, cos_ref, sa_ref, sb_ref, qw_ref, kw_ref,
                   q_ref, k_ref, v_ref, rk_ref, rv_ref, rq_ref, rg_ref, cv_ref):
    hb = _modulated_rms(x_ref[...], nw_ref[...], sc_ref[...], sh_ref[...]).astype(BF16)

    def proj(c0, n):
        return jnp.dot(hb, w_ref[:, c0:c0 + n], preferred_element_type=F32)

    grp = _head_group_matrix()
    cos, sa, sb = cos_ref[...], sa_ref[...], sb_ref[...]

    def norm_rope(p, wrow):
        y = p * lax.rsqrt(_per_head_sum(p * p, grp) * (1.0 / HEAD_DIM) + EPS) * wrow
        return (y * cos + pltpu.roll(y, LANES - ROPE_FREQS, 1) * sa
                + pltpu.roll(y, ROPE_FREQS, 1) * sb)

    def put_kv(kv):
        k0, k1 = _dup_halves(norm_rope(kv[:, 0:ATT_KV_W], kw_ref[...]))
        k_ref[:, 0:LANES] = k0.astype(BF16)
        k_ref[:, LANES:2 * LANES] = k1.astype(BF16)
        v_ref[...] = kv[:, ATT_KV_W:].T.astype(BF16)

    def put_q(qall):
        qw = qw_ref[...] * ATT_SCALE
        for j in range(D_ATT // LANES):
            q_ref[:, j * LANES:(j + 1) * LANES] = norm_rope(qall[:, j * LANES:(j + 1) * LANES], qw).astype(BF16)

    def put_rk(p):
        rk_ref[...] = (p * RET_K_SCALE).astype(BF16)

    def put_rv(p):
        rv_ref[...] = p.astype(BF16)

    def put_rq(p):
        rq_ref[...] = p.astype(BF16)

    def put_rg(p):
        rg_ref[...] = p

    def put_conv(p):
        cv_ref[...] = p[:, 0:CONV_CH] * _sigmoid(p[:, CONV_CH:])

    units = [(C_ATT_K, 2 * ATT_KV_W, put_kv), (C_ATT_Q, D_ATT, put_q), (C_RET_K, D_RET, put_rk),
             (C_RET_V, D_RET, put_rv), (C_RET_Q, D_RET, put_rq), (C_RET_G, D_RET, put_rg),
             (C_CONV, 2 * CONV_CH, put_conv)]
    ahead = 2
    pending = [proj(c0, n) for c0, n, _ in units[:ahead]]
    for n, (_, _, put) in enumerate(units):
        if n + ahead < len(units):
            pending.append(proj(*units[n + ahead][:2]))
        put(pending.pop(0))


def _inproj(x, nw, sc, sh, w_bf, cos, sa, sb, qw, kw):
    b, s, d = x.shape
    tm = _pick_tile(s, 1024)
    row = lambda n: pl.BlockSpec((1, n), lambda bi, i: (0, 0))
    per_b = pl.BlockSpec((None, 1, d), lambda bi, i: (bi, 0, 0))
    tab = pl.BlockSpec((tm, LANES), lambda bi, i: (i, 0))
    tok = lambda n: pl.BlockSpec((None, tm, n), lambda bi, i: (bi, i, 0))
    shp = lambda n, dt: jax.ShapeDtypeStruct((b, s, n), dt)
    return pl.pallas_call(
        _inproj_kernel,
        out_shape=(shp(D_ATT, BF16), shp(2 * LANES, BF16), jax.ShapeDtypeStruct((b, ATT_KV_W, s), BF16),
                   shp(D_RET, BF16), shp(D_RET, BF16), shp(D_RET, BF16), shp(D_RET, F32),
                   shp(CONV_CH, F32)),
        grid=(b, s // tm),
        in_specs=[tok(d), row(d), per_b, per_b,
                  pl.BlockSpec(w_bf.shape, lambda bi, i: (0, 0)),
                  tab, tab, tab, row(LANES), row(LANES)],
        out_specs=(tok(D_ATT), tok(2 * LANES), pl.BlockSpec((None, ATT_KV_W, tm), lambda bi, i: (bi, 0, i)),
                   tok(D_RET), tok(D_RET), tok(D_RET), tok(D_RET), tok(CONV_CH)),
        compiler_params=_params("parallel", "parallel"),
        name="inproj",
    )(x, nw, sc, sh, w_bf, cos, sa, sb, qw, kw)


def _attn_kernel(*refs, window, nq, n_cast):
    blk = ATT_BLOCK
    n_in = len(refs) - 1 - n_cast
    for src, dst in zip(refs[n_in - n_cast:n_in], refs[n_in + 1:]):
        dst[...] = src[...].astype(BF16)
    refs = refs[:n_in - n_cast] + refs[n_in:n_in + 1]
    if window:
        q_ref = refs[0]
        k_refs = refs[1:nq + 3]
        v_refs = refs[nq + 3:2 * nq + 5]
        kx_ref, vx_ref, sink_ref, o_ref = refs[2 * nq + 5:]
    else:
        q_ref, kx_ref, vx_ref, sink_ref, o_ref = refs
    n_ctx = kx_ref.shape[0]
    nk = 3 * blk + n_ctx if window else n_ctx
    if window:
        i = pl.program_id(1)
        last = nq * pl.num_programs(1) - 1
        key = lax.broadcasted_iota(jnp.int32, (blk, 2 * blk), 0)
        qry = lax.broadcasted_iota(jnp.int32, (blk, 2 * blk), 1) & (blk - 1)

        def band_masks(sub):
            off_prev = jnp.where(nq * i + sub > 0, 0, blk)
            off_next = jnp.where(nq * i + sub < last, 0, blk)
            return key >= qry + off_prev, key + off_next <= qry
    first_head = lax.broadcasted_iota(jnp.int32, (blk, LANES), 1) < HEAD_DIM
    ones = jnp.ones((2 * SUBLANES, nk), BF16)

    def scores(sub, pair):
        g = pair // 2
        gs = slice(g * LANES, (g + 1) * LANES)
        if window:
            kcat = jnp.concatenate([r[:, gs] for r in k_refs[sub:sub + 3]] + [kx_ref[:, gs]], axis=0)
        else:
            kcat = kx_ref[:, gs]
        qp = q_ref[sub * blk:(sub + 1) * blk, pair * LANES:(pair + 1) * LANES]
        zero = jnp.zeros_like(qp)
        w = jnp.concatenate([jnp.where(first_head, qp, zero), jnp.where(first_head, zero, qp)], axis=0)
        return lax.dot_general(kcat, w, (((1,), (1,)), ((), ())), preferred_element_type=F32)

    def masked_max(sub, pair, s):
        if window:
            mask_prev, mask_next = band_masks(sub)
            parts = [jnp.where(mask_prev, s[0:blk], NEG_INF), s[blk:2 * blk],
                     jnp.where(mask_next, s[2 * blk:3 * blk], NEG_INF), s[3 * blk:]]
        else:
            parts = [s]
        m = sink_ref[pair:pair + 1, :]
        for part in parts:
            m = jnp.maximum(m, jnp.max(part, axis=0, keepdims=True))
        return parts, m

    def softmax(sub, pair, parts_m):
        parts, m = parts_m
        p = jnp.concatenate([jnp.exp(part - m).astype(BF16) for part in parts], axis=0)
        return p, jnp.exp(sink_ref[pair:pair + 1, :] - m)

    def output(sub, pair, p, sink_p):
        g = pair // 2
        vs = slice(g * HEAD_DIM, (g + 1) * HEAD_DIM)
        if window:
            vt = jnp.concatenate([r[vs, :] for r in v_refs[sub:sub + 3]] + [vx_ref[vs, :]], axis=1)
        else:
            vt = vx_ref[vs, :]
        vaug = jnp.concatenate([vt, ones], axis=0)
        o = jnp.dot(vaug, p, preferred_element_type=F32)
        on = o[0:HEAD_DIM, :] * (1.0 / (o[HEAD_DIM:HEAD_DIM + 1, :] + sink_p))
        ot = jnp.concatenate([on[:, 0:blk], on[:, blk:2 * blk]], axis=0)
        o_ref[sub * blk:(sub + 1) * blk, pair * LANES:(pair + 1) * LANES] = ot.T.astype(BF16)

    units = [(sub, pair) for sub in range(nq) for pair in range(ATT_Q_HEADS // 2)]
    stages = (lambda sub, pair, _: scores(sub, pair), masked_max, softmax,
              lambda sub, pair, ps: output(sub, pair, *ps))
    vals = {}
    for step in range(len(units) + len(stages) - 1):
        for k, stage in enumerate(stages):
            u = step - k
            if 0 <= u < len(units):
                vals[u, k] = stage(*units[u], vals.pop((u, k - 1), None))


def _attention(q, k, vt, kx, vxt, sink_tab, window, cast=()):
    b, s, _ = q.shape
    blk = ATT_BLOCK
    assert ATT_WINDOW == blk, "the band masks assume a window of exactly one key block on each side"
    nb = s // blk
    nq = ATT_QBLOCKS if nb % ATT_QBLOCKS == 0 else 1
    n_ctx = kx.shape[1]
    qspec = pl.BlockSpec((None, nq * blk, D_ATT), lambda bi, i: (bi, i, 0))
    kctx = pl.BlockSpec((None, n_ctx, 2 * LANES), lambda bi, i: (bi, 0, 0))
    vctx = pl.BlockSpec((None, ATT_KV_W, n_ctx), lambda bi, i: (bi, 0, 0))
    snk = pl.BlockSpec(sink_tab.shape, lambda bi, i: (0, 0))
    if window:
        at = lambda off: (lambda i: jnp.clip(nq * i + off, 0, nb - 1))
        kspec = lambda f: pl.BlockSpec((None, blk, 2 * LANES), lambda bi, i: (bi, f(i), 0))
        vspec = lambda f: pl.BlockSpec((None, ATT_KV_W, blk), lambda bi, i: (bi, 0, f(i)))
        offs = range(-1, nq + 1)
        in_specs = ([qspec] + [kspec(at(o)) for o in offs] + [vspec(at(o)) for o in offs]
                    + [kctx, vctx, snk])
        args = (q,) + (k,) * (nq + 2) + (vt,) * (nq + 2) + (kx, vxt, sink_tab)
    else:
        in_specs = [qspec, kctx, vctx, snk]
        args = (q, kx, vxt, sink_tab)
    steps = nb // nq
    out_shape, out_specs = [jax.ShapeDtypeStruct((b, s, D_ATT), BF16)], [qspec]
    for w in cast:
        rows = w.shape[0] // (b * steps)
        assert rows * b * steps == w.shape[0] and rows % (2 * SUBLANES) == 0, w.shape
        slab = pl.BlockSpec((rows, w.shape[1]), lambda bi, i: (bi * steps + i, 0))
        in_specs.append(slab)
        out_specs.append(slab)
        out_shape.append(jax.ShapeDtypeStruct(w.shape, BF16))
    res = pl.pallas_call(
        functools.partial(_attn_kernel, window=window, nq=nq, n_cast=len(cast)),
        out_shape=tuple(out_shape),
        grid=(b, steps),
        in_specs=in_specs,
        out_specs=tuple(out_specs),
        compiler_params=_params("parallel", "parallel"),
        name="attention_window" if window else "attention_ctx",
    )(*args, *cast)
    return res[0], tuple(res[1:])


def _ret_state_kernel(lgl_ref, k_ref, v_ref, s0_ref, r_ref, fin_ref, s_scr, *, cb, reverse):
    c = pl.program_id(1)
    L = RET_CHUNK

    @pl.when(c == 0)
    def _():
        s_scr[...] = s0_ref[...]

    jj = lax.broadcasted_iota(jnp.int32, (L, LANES), 0).astype(F32)
    expo = jj if reverse else (L - 1.0) - jj
    same_head = (lax.broadcasted_iota(jnp.int32, (LANES, LANES), 0) // HEAD_DIM
                 == lax.broadcasted_iota(jnp.int32, (LANES, LANES), 1) // HEAD_DIM)
    for pr in range(RET_HEADS // 2):
        cs = slice(pr * LANES, (pr + 1) * LANES)
        lgl = lgl_ref[pr]
        kdec = jnp.exp(expo * lgl)
        cdec = jnp.exp(float(L) * lgl)
        state = s_scr[pr]
        for t in range(cb):
            cc = cb - 1 - t if reverse else t
            rows = slice(cc * L, (cc + 1) * L)
            r_ref[cc, pr] = state.astype(BF16)
            kd = k_ref[rows, cs].astype(F32) * kdec
            u = jnp.dot(kd.T.astype(BF16), v_ref[rows, cs], preferred_element_type=F32)
            state = cdec * state + jnp.where(same_head, u, 0.0)
        s_scr[pr] = state

    @pl.when(c == pl.num_programs(1) - 1)
    def _():
        fin_ref[...] = s_scr[...]


def _ret_states(lgl, rk, rv, s0, reverse):
    b, s, _ = rk.shape
    nc = s // RET_CHUNK
    cb = _pick_tile(nc, 16)
    nblk = nc // cb
    npair = RET_HEADS // 2
    blk_idx = (lambda c: nblk - 1 - c) if reverse else (lambda c: c)
    tok = pl.BlockSpec((None, cb * RET_CHUNK, D_RET), lambda bi, c: (bi, blk_idx(c), 0))
    st = pl.BlockSpec((None, npair, LANES, LANES), lambda bi, c: (bi, 0, 0, 0))
    return pl.pallas_call(
        functools.partial(_ret_state_kernel, cb=cb, reverse=reverse),
        out_shape=(jax.ShapeDtypeStruct((b, nc, npair, LANES, LANES), BF16),
                   jax.ShapeDtypeStruct((b, npair, LANES, LANES), F32)),
        grid=(b, nblk),
        in_specs=[pl.BlockSpec(lgl.shape, lambda bi, c: (0, 0, 0)), tok, tok, st],
        out_specs=(pl.BlockSpec((None, cb, npair, LANES, LANES),
                                lambda bi, c: (bi, blk_idx(c), 0, 0, 0)), st),
        scratch_shapes=[pltpu.VMEM((npair, LANES, LANES), F32)],
        compiler_params=_params("parallel", "arbitrary"),
        name="ret_state_bwd" if reverse else "ret_state_fwd",
    )(lgl, rk, rv, s0)


def _ret_out_kernel(lgf_ref, lgb_ref, q_ref, k_ref, v_ref, g_ref, rf_ref, rb_ref, gnw_ref, o_ref, *, cb):
    L = RET_CHUNK
    i0 = lax.broadcasted_iota(jnp.int32, (L, LANES), 0).astype(F32)
    i1 = lax.broadcasted_iota(jnp.int32, (L, LANES), 1).astype(F32)
    diff = i0 - i1
    lo = lax.broadcasted_iota(jnp.int32, (L, LANES), 1) < HEAD_DIM
    grp = _head_group_matrix()
    inv = 1.0 / HEAD_DIM
    tabs = []
    for pr in range(RET_HEADS // 2):
        lgf, lgb = lgf_ref[pr], lgb_ref[pr]
        dmat = [jnp.where(diff >= 0.0,
                          jnp.exp(jnp.maximum(diff, 0.0) * lgf[:, a:a + 1]),
                          jnp.exp(jnp.maximum(-diff, 0.0) * lgb[:, a:a + 1]))
                for a in (0, HEAD_DIM)]
        tabs.append((jnp.exp((i0 + 1.0) * lgf), jnp.exp((float(L) - i0) * lgb),
                     jnp.concatenate(dmat, axis=0), gnw_ref[:, pr * LANES:(pr + 1) * LANES]))

    def decayed_scores(t, pr):
        rows = pl.ds(pl.multiple_of(t * L, L), L)
        cs = slice(pr * LANES, (pr + 1) * LANES)
        qp, kp = q_ref[rows, cs], k_ref[rows, cs]
        zero = jnp.zeros_like(kp)
        qz = jnp.concatenate([jnp.where(lo, qp, zero), jnp.where(lo, zero, qp)], axis=0)
        return lax.dot_general(qz, kp, (((1,), (1,)), ((), ())), preferred_element_type=F32) * tabs[pr][2]

    def mix(t, pr, a):
        rows = pl.ds(pl.multiple_of(t * L, L), L)
        cs = slice(pr * LANES, (pr + 1) * LANES)
        xif, xib, _, _ = tabs[pr]
        vp = v_ref[rows, cs]
        zero = jnp.zeros_like(vp)
        qf = q_ref[rows, cs].astype(F32)
        lhs = jnp.concatenate([a[0:L].astype(BF16), a[L:].astype(BF16),
                               (qf * xif).astype(BF16), (qf * xib).astype(BF16)], axis=1)
        rhs = jnp.concatenate([jnp.where(lo, vp, zero), jnp.where(lo, zero, vp),
                               rf_ref[t, pr], rb_ref[t, pr]], axis=0)
        return jnp.dot(lhs, rhs, preferred_element_type=F32)

    def centre(t, pr, acc):
        return acc - _per_head_sum(acc, grp) * inv

    def readout(t, pr, dl):
        rows = pl.ds(pl.multiple_of(t * L, L), L)
        cs = slice(pr * LANES, (pr + 1) * LANES)
        var = _per_head_sum(dl * dl, grp) * inv
        y = dl * lax.rsqrt(var + EPS) * tabs[pr][3]
        gt = g_ref[rows, cs]
        o_ref[rows, cs] = (gt * _sigmoid(gt) * y).astype(BF16)

    unroll = min(RET_UNROLL, cb)
    stages = (lambda t, pr, _: decayed_scores(t, pr), mix, centre, readout)

    def group(gi, carry):
        units = [(gi * unroll + u, pr) for u in range(unroll) for pr in range(RET_HEADS // 2)]
        vals = {}
        for step in range(len(units) + len(stages) - 1):
            for k, stage in enumerate(stages):
                u = step - k
                if 0 <= u < len(units):
                    vals[u, k] = stage(*units[u], vals.pop((u, k - 1), None))
        return carry

    lax.fori_loop(0, cb // unroll, group, 0)


def _ret_out(lgf, lgb, rq, rk, rv, rg, rf, rb, gnw):
    b, s, _ = rq.shape
    nc = s // RET_CHUNK
    cb = _pick_tile(nc, 8)
    npair = RET_HEADS // 2
    tok = pl.BlockSpec((None, cb * RET_CHUNK, D_RET), lambda bi, c: (bi, c, 0))
    st = pl.BlockSpec((None, cb, npair, LANES, LANES), lambda bi, c: (bi, c, 0, 0, 0))
    lg = pl.BlockSpec(lgf.shape, lambda bi, c: (0, 0, 0))
    return pl.pallas_call(
        functools.partial(_ret_out_kernel, cb=cb),
        out_shape=jax.ShapeDtypeStruct((b, s, D_RET), BF16),
        grid=(b, nc // cb),
        in_specs=[lg, lg, tok, tok, tok, tok, st, st, pl.BlockSpec((1, D_RET), lambda bi, c: (0, 0))],
        out_specs=tok,
        compiler_params=_params("parallel", "parallel"),
        name="ret_out",
    )(lgf, lgb, rq, rk, rv, rg, rf, rb, gnw)


def _conv_kernel(prev_ref, cur_ref, next_ref, w_ref, b_ref, lnw_ref, lnb_ref, o_ref,
                 xpad, shifted, hbuf, wtile):
    i = pl.program_id(1)
    tm = cur_ref.shape[0]
    halo = CONV_HALO
    first = i == 0
    last = i == pl.num_programs(1) - 1
    xpad[0:halo, :] = jnp.where(first, 0.0, prev_ref[...])
    xpad[halo:halo + tm, :] = cur_ref[...]
    xpad[halo + tm:2 * halo + tm, :] = jnp.where(last, 0.0, next_ref[...])
    span = tm + 2 * halo - SUBLANES
    for r in range(SUBLANES):
        shifted[r] = xpad[r:r + span, :]
    base = halo - CONV_WIDTH // 2
    bias = b_ref[...]
    ch = cur_ref.shape[1]
    for w in range(CONV_WIDTH):
        wtile[w] = jnp.broadcast_to(w_ref[w:w + 1, :], (SUBLANES, ch))

    taps_by_shift = {}
    for w in range(CONV_WIDTH):
        taps_by_shift.setdefault((base + w) % SUBLANES, []).append(((base + w) // SUBLANES, w))

    def sub(sb, carry):
        r0 = pl.multiple_of(sb * CONV_SUB, CONV_SUB)
        chains = [None] * CONV_CHAINS
        groups = CONV_SUB // SUBLANES
        for r, taps in sorted(taps_by_shift.items()):
            a_lo = min(a for a, _ in taps)
            n_g = max(a for a, _ in taps) - a_lo + groups
            win = shifted[r, pl.ds(r0 + a_lo * SUBLANES, n_g * SUBLANES), :].reshape(n_g, SUBLANES, ch)
            for a, w in taps:
                term = win[a - a_lo:a - a_lo + groups] * wtile[w]
                c = w % CONV_CHAINS
                chains[c] = term if chains[c] is None else chains[c] + term
        total = (chains[0] + chains[1]) + (chains[2] + chains[3])
        hbuf[pl.ds(r0, CONV_SUB), :] = total.reshape(CONV_SUB, ch) + bias
        return carry

    lax.fori_loop(0, tm // CONV_SUB, sub, 0)
    h = hbuf[...]
    mu = jnp.mean(h, axis=-1, keepdims=True)
    dl = h - mu
    var = jnp.mean(dl * dl, axis=-1, keepdims=True)
    y = dl * lax.rsqrt(var + EPS) * lnw_ref[...] + lnb_ref[...]
    o_ref[...] = (y * _sigmoid(y)).astype(BF16)


def _conv(cv, w, bias, lnw, lnb):
    b, s, ch = cv.shape
    tm = _pick_tile(s, 1024)
    hpb = tm // CONV_HALO
    nh = s // CONV_HALO
    row = pl.BlockSpec((1, ch), lambda bi, i: (0, 0))
    span = tm + 2 * CONV_HALO - SUBLANES
    return pl.pallas_call(
        _conv_kernel,
        out_shape=jax.ShapeDtypeStruct((b, s, ch), BF16),
        grid=(b, s // tm),
        in_specs=[pl.BlockSpec((None, CONV_HALO, ch), lambda bi, i: (bi, jnp.maximum(i * hpb - 1, 0), 0)),
                  pl.BlockSpec((None, tm, ch), lambda bi, i: (bi, i, 0)),
                  pl.BlockSpec((None, CONV_HALO, ch),
                               lambda bi, i: (bi, jnp.minimum((i + 1) * hpb, nh - 1), 0)),
                  pl.BlockSpec(w.shape, lambda bi, i: (0, 0)), row, row, row],
        out_specs=pl.BlockSpec((None, tm, ch), lambda bi, i: (bi, i, 0)),
        scratch_shapes=[pltpu.VMEM((tm + 2 * CONV_HALO, ch), F32),
                        pltpu.VMEM((SUBLANES, span, ch), F32),
                        pltpu.VMEM((tm, ch), F32),
                        pltpu.VMEM((CONV_WIDTH, SUBLANES, ch), F32)],
        compiler_params=_params("parallel", "parallel"),
        name="conv",
    )(cv, cv, cv, w, bias, lnw, lnb)


def _outproj_route_kernel(att_ref, ret_ref, cnv_ref, w_ref, x_hbm, g1_ref, nw_ref, sc_ref, sh_ref, rcat_ref,
                          xo_ref, h_ref, rt_ref, xring, sems):
    ni = pl.num_programs(1)
    n = pl.program_id(0) * ni + pl.program_id(1)
    total = pl.num_programs(0) * ni
    tm = xring.shape[1]

    def x_copy(step):
        slot = lax.rem(step, XRING_SLOTS)
        src = x_hbm.at[step // ni, pl.ds(pl.multiple_of(lax.rem(step, ni) * tm, tm), tm)]
        return pltpu.make_async_copy(src, xring.at[slot], sems.at[slot])

    @pl.when(n == 0)
    def _():
        for ahead in range(XRING_SLOTS - 1):
            @pl.when(ahead < total)
            def _():
                x_copy(n + ahead).start()

    @pl.when(n + XRING_SLOTS - 1 < total)
    def _():
        x_copy(n + XRING_SLOTS - 1).start()

    y = (jnp.dot(att_ref[...], w_ref[0:D_ATT, :], preferred_element_type=F32)
         + jnp.dot(ret_ref[...], w_ref[D_ATT:D_ATT + D_RET, :], preferred_element_type=F32)
         + jnp.dot(cnv_ref[...], w_ref[D_ATT + D_RET:, :], preferred_element_type=F32))
    x_copy(n).wait()
    xn = xring[lax.rem(n, XRING_SLOTS)] + g1_ref[...] * y
    xo_ref[...] = xn
    h = _modulated_rms(xn, nw_ref[...], sc_ref[...], sh_ref[...])
    h_ref[...] = _pack_bf16_pairs(h)
    hi = h.astype(BF16)
    lo = (h - hi.astype(F32)).astype(BF16)
    tm = h.shape[0]
    r = jnp.dot(jnp.concatenate([hi, lo], axis=0), rcat_ref[...], preferred_element_type=F32)
    logits = (r[0:tm, 0:LANES] + r[0:tm, LANES:]) + (r[tm:, 0:LANES] + r[tm:, LANES:])
    lane = lax.broadcasted_iota(jnp.int32, logits.shape, 1).astype(F32)
    logits = jnp.where(lane < N_EXPERTS, logits, NEG_INF)
    m1 = jnp.max(logits, axis=-1, keepdims=True)
    i1 = jnp.min(jnp.where(logits == m1, lane, float(LANES)), axis=-1, keepdims=True)
    rest = jnp.where(lane == i1, NEG_INF, logits)
    m2 = jnp.max(rest, axis=-1, keepdims=True)
    i2 = jnp.min(jnp.where(rest == m2, lane, float(LANES)), axis=-1, keepdims=True)
    e2 = jnp.exp(m2 - m1)
    w1 = 1.0 / (1.0 + e2)
    w2 = e2 / (1.0 + e2)
    rt_ref[...] = jnp.where(lane == 0.0, i1,
                            jnp.where(lane == 1.0, i2,
                                      jnp.where(lane == 2.0, w1, jnp.where(lane == 3.0, w2, 0.0))))


def _outproj_route(att, ret, cnv, w_bf, x, g1, nw, sc, sh, router):
    b, s, d = x.shape
    tm = _pick_tile(s, 1024)
    tok = lambda n: pl.BlockSpec((None, tm, n), lambda bi, i: (bi, i, 0))
    per_b = pl.BlockSpec((None, 1, d), lambda bi, i: (bi, 0, 0))
    rpad = jnp.zeros((d, LANES), F32).at[:, :N_EXPERTS].set(router)
    rhi = rpad.astype(BF16)
    rlo = (rpad - rhi.astype(F32)).astype(BF16)
    return pl.pallas_call(
        _outproj_route_kernel,
        out_shape=(jax.ShapeDtypeStruct((b, s, d), F32), jax.ShapeDtypeStruct((b, s, d // 2), jnp.int32),
                   jax.ShapeDtypeStruct((b, s, LANES), F32)),
        grid=(b, s // tm),
        in_specs=[tok(D_ATT), tok(D_RET), tok(CONV_CH), pl.BlockSpec(w_bf.shape, lambda bi, i: (0, 0)),
                  pl.BlockSpec(memory_space=pl.ANY), per_b, pl.BlockSpec((1, d), lambda bi, i: (0, 0)),
                  per_b, per_b, pl.BlockSpec((d, 2 * LANES), lambda bi, i: (0, 0))],
        out_specs=(tok(d), tok(d // 2), tok(LANES)),
        scratch_shapes=[pltpu.VMEM((XRING_SLOTS, tm, d), F32), pltpu.SemaphoreType.DMA((XRING_SLOTS,))],
        compiler_params=_params("arbitrary", "arbitrary"),
        name="outproj_route",
    )(att, ret, cnv, w_bf, x, g1, nw, sc, sh, jnp.concatenate([rhi, rlo], axis=1))


def _swiglu_chunks(h, wg_ref, wu_ref, wd_ref, sub):
    total = None
    for c0 in range(0, wg_ref.shape[1], sub):
        cs = slice(c0, min(c0 + sub, wg_ref.shape[1]))
        gate = jnp.dot(h, wg_ref[:, cs].astype(BF16), preferred_element_type=F32)
        up = jnp.dot(h, wu_ref[:, cs].astype(BF16), preferred_element_type=F32)
        act = (gate * _sigmoid(gate) * up).astype(BF16)
        part = jnp.dot(act, wd_ref[cs, :].astype(BF16), preferred_element_type=F32)
        total = part if total is None else total + part
    return total


def _mixer_ffn_kernel(att_ref, ret_ref, cnv_ref, wo_ref, x_ref, g1_ref, nw_ref, sc_ref, sh_ref,
                      wg_ref, wu_ref, wd_ref, g2_ref, o_ref):
    y = (jnp.dot(att_ref[...], wo_ref[0:D_ATT, :], preferred_element_type=F32)
         + jnp.dot(ret_ref[...], wo_ref[D_ATT:D_ATT + D_RET, :], preferred_element_type=F32)
         + jnp.dot(cnv_ref[...], wo_ref[D_ATT + D_RET:, :], preferred_element_type=F32))
    xn = x_ref[...] + g1_ref[...] * y
    h = _modulated_rms(xn, nw_ref[...], sc_ref[...], sh_ref[...]).astype(BF16)
    o_ref[...] = xn + g2_ref[...] * _swiglu_chunks(h, wg_ref, wu_ref, wd_ref, FFN_SUB)


def _mixer_ffn(att, ret, cnv, wo, x, g1, nw, sc, sh, wg, wu, wd, g2):
    b, s, d = x.shape
    f = wg.shape[1]
    tm = _pick_tile(s, 512)
    tok = lambda n: pl.BlockSpec((None, tm, n), lambda bi, i: (bi, i, 0))
    per_b = pl.BlockSpec((None, 1, d), lambda bi, i: (bi, 0, 0))
    res = lambda shape: pl.BlockSpec(shape, lambda bi, i: (0, 0), pipeline_mode=pl.Buffered(1))
    return pl.pallas_call(
        _mixer_ffn_kernel,
        out_shape=jax.ShapeDtypeStruct((b, s, d), F32),
        grid=(b, s // tm),
        in_specs=[tok(D_ATT), tok(D_RET), tok(CONV_CH), res(wo.shape), tok(d), per_b,
                  pl.BlockSpec((1, d), lambda bi, i: (0, 0)), per_b, per_b,
                  res((d, f)), res((d, f)), res((f, d)), per_b],
        out_specs=tok(d),
        compiler_params=_params("parallel", "parallel"),
        name="mixer_ffn_dense",
    )(att, ret, cnv, wo, x, g1, nw, sc, sh, wg, wu, wd, g2)


def _gather_rows(idx, src):
    n = idx.shape[0]
    d = src.shape[1]
    mesh = plsc.VectorSubcoreMesh(core_axis_name="core", subcore_axis_name="subcore")

    @pl.kernel(out_type=jax.ShapeDtypeStruct((n, d), src.dtype), mesh=mesh, name="moe_gather")
    def gather(src_hbm, idx_hbm, out_hbm):
        def body(idx_vmem, out_vmem):
            pltpu.sync_copy(src_hbm.at[idx_vmem.at[0, pl.ds(0, SC_WINDOW)]], out_vmem)

        pltpu.emit_pipeline(
            body,
            grid=(n // SC_WINDOW,),
            in_specs=[pl.BlockSpec((1, LANES), lambda i: (i, 0))],
            out_specs=[pl.BlockSpec((SC_WINDOW, d), lambda i: (i, 0))],
            core_axis_name=("core", "subcore"),
            dimension_semantics=(pltpu.PARALLEL,),
        )(idx_hbm, out_hbm)

    idx_rows = jnp.pad(idx.reshape(n // SC_WINDOW, SC_WINDOW), ((0, 0), (0, LANES - SC_WINDOW)))
    return gather(src, idx_rows)


def _invert_rows(dest, n_rows):
    n = dest.shape[0]
    n_tok = n // 2
    mesh = plsc.VectorSubcoreMesh(core_axis_name="core", subcore_axis_name="subcore")

    @pl.kernel(out_type=jax.ShapeDtypeStruct((n_rows,), jnp.int32), mesh=mesh, name="moe_row_src",
               scratch_types=[pltpu.VMEM((n,), jnp.int32), pltpu.VMEM((n_rows,), jnp.int32)],
               compiler_params=pltpu.CompilerParams(needs_layout_passes=False))
    def invert(dest_hbm, out_hbm, dest_vmem, rows_vmem):
        @pl.when((lax.axis_index("core") == 0) & (lax.axis_index("subcore") == 0))
        def _():
            pltpu.sync_copy(dest_hbm, dest_vmem)

            @pl.loop(0, n_rows, step=SC_LANES)
            def _(r):
                rows_vmem[pl.ds(r, SC_LANES)] = lax.rem(lax.iota(jnp.int32, SC_LANES) + r, n_tok)

            @pl.loop(0, n, step=SC_LANES)
            def _(a):
                tok = lax.iota(jnp.int32, SC_LANES) + a
                tok = jnp.where(tok >= n_tok, tok - n_tok, tok)
                plsc.store_scatter(rows_vmem, [dest_vmem[pl.ds(a, SC_LANES)]], tok)

            pltpu.sync_copy(rows_vmem, out_hbm)

    return invert(dest)


def _moe_kernel(te_ref, nu_ref, x_ref, wg_ref, wu_ref, wd_ref, *rest, tile0, nj):
    o_ref, xb, acc = rest[-3:]
    t = pl.program_id(0) + tile0
    j = pl.program_id(1)
    used = t < nu_ref[0]

    def partial_sum():
        return _swiglu_chunks(xb[...], wg_ref, wu_ref, wd_ref, MOE_SUB)

    @pl.when(used & (j == 0))
    def _():
        lo, hi = _unpack_bf16_pairs(x_ref[...])
        half = lo.shape[1]
        xb[:, 0:half] = lo.astype(BF16)
        xb[:, half:] = hi.astype(BF16)
        if nj == 1:
            o_ref[...] = _pack_bf16_pairs(partial_sum())
        else:
            acc[...] = partial_sum()

    if nj > 2:
        @pl.when(used & (j > 0) & (j < nj - 1))
        def _():
            acc[...] += partial_sum()

    if nj > 1:
        @pl.when(used & (j == nj - 1))
        def _():
            o_ref[...] = _pack_bf16_pairs(acc[...] + partial_sum())

    @pl.when(jnp.logical_not(used) & (j == nj - 1))
    def _():
        o_ref[...] = jnp.zeros_like(o_ref)


def _moe_grouped(tile_expert, n_used, xs, wg, wu, wd, y_prev, tile0, n_rows):
    p, dp = xs.shape
    d = 2 * dp
    f = wg.shape[2]
    tm = MOE_TM
    tf = MOE_TF
    nj = f // tf

    def jj(t, j, te, nu):
        return jnp.where(t + tile0 < nu[0], j, nj - 1)

    in_specs = [pl.BlockSpec((tm, dp), lambda t, j, te, nu: (t, 0)),
                pl.BlockSpec((None, d, tf), lambda t, j, te, nu: (te[t + tile0], 0, jj(t, j, te, nu))),
                pl.BlockSpec((None, d, tf), lambda t, j, te, nu: (te[t + tile0], 0, jj(t, j, te, nu))),
                pl.BlockSpec((None, tf, d), lambda t, j, te, nu: (te[t + tile0], jj(t, j, te, nu), 0))]
    args = [tile_expert, n_used, xs, wg, wu, wd]
    aliases = {}
    if y_prev is not None:
        in_specs.append(pl.BlockSpec(memory_space=pl.ANY))
        args.append(y_prev)
        aliases = {len(args) - 1: 0}
    return pl.pallas_call(
        functools.partial(_moe_kernel, tile0=tile0, nj=nj),
        out_shape=jax.ShapeDtypeStruct((n_rows, dp), jnp.int32),
        grid_spec=pltpu.PrefetchScalarGridSpec(
            num_scalar_prefetch=2,
            grid=(p // tm, nj),
            in_specs=in_specs,
            out_specs=pl.BlockSpec((tm, dp), lambda t, j, te, nu: (t + tile0, 0)),
            scratch_shapes=[pltpu.VMEM((tm, d), BF16), pltpu.VMEM((tm, d), F32)]),
        input_output_aliases=aliases,
        compiler_params=_params("arbitrary", "arbitrary"),
        name="moe_grouped",
    )(*args)


def _combine_kernel(y1_ref, y2_ref, x_ref, g2_ref, rt_ref, o_ref):
    rt = rt_ref[...]
    w1, w2 = rt[:, 2:3], rt[:, 3:4]
    half = y1_ref.shape[1]
    for k, (a, c) in enumerate(zip(_unpack_bf16_pairs(y1_ref[...]), _unpack_bf16_pairs(y2_ref[...]))):
        cs = slice(k * half, (k + 1) * half)
        o_ref[:, cs] = x_ref[:, cs] + g2_ref[:, cs] * (w1 * a + w2 * c)


def _moe_combine(yg, x, g2, route):
    b, s, d = x.shape
    tm = _pick_tile(s, 512)
    tok = lambda n: pl.BlockSpec((None, tm, n), lambda bi, i: (bi, i, 0))
    return pl.pallas_call(
        _combine_kernel,
        out_shape=jax.ShapeDtypeStruct((b, s, d), F32),
        grid=(b, s // tm),
        in_specs=[tok(d // 2), pl.BlockSpec((None, tm, d // 2), lambda bi, i: (b + bi, i, 0)), tok(d),
                  pl.BlockSpec((None, 1, d), lambda bi, i: (bi, 0, 0)), tok(LANES)],
        out_specs=tok(d),
        compiler_params=_params("parallel", "parallel"),
        name="moe_combine",
    )(yg, yg, x, g2, route)


def _moe(h, route, x, g2, wg, wu, wd):
    b, s, d = x.shape
    n_tok = b * s
    tm = MOE_TM
    rt = route.reshape(n_tok, LANES)
    flat_e = jnp.concatenate([rt[:, 0], rt[:, 1]]).astype(jnp.int32)
    onehot = (flat_e[:, None] == jnp.arange(N_EXPERTS, dtype=jnp.int32)[None, :]).astype(jnp.int32)
    csum = jnp.cumsum(onehot, axis=0)
    rank = jnp.sum(csum * onehot, axis=1) - 1
    counts = csum[-1]
    tiles_e = (counts + tm - 1) // tm
    tiles_cum = jnp.cumsum(tiles_e)
    row_start = (tiles_cum - tiles_e) * tm
    dest = jnp.sum(onehot * row_start[None, :], axis=1) + rank
    n_tiles = 2 * n_tok // tm + N_EXPERTS
    tile_ids = jnp.arange(n_tiles, dtype=jnp.int32)
    tile_expert = jnp.sum((tile_ids[:, None] >= tiles_cum[None, :]).astype(jnp.int32), axis=1)
    last_e = jnp.max(jnp.where(tiles_e > 0, jnp.arange(N_EXPERTS, dtype=jnp.int32), 0))
    tile_expert = jnp.minimum(tile_expert, last_e).astype(jnp.int32)
    n_used = tiles_cum[-1:].astype(jnp.int32)
    dest = dest.astype(jnp.int32)
    row_src = _invert_rows(dest, n_tiles * tm)
    n_chunks = max(c for c in range(1, MOE_CHUNKS + 1) if n_tiles % c == 0)
    tiles_c = n_tiles // n_chunks
    h_flat = h.reshape(n_tok, d // 2)
    xs = [_gather_rows(row_src[c * tiles_c * tm:(c + 1) * tiles_c * tm], h_flat) for c in range(n_chunks)]
    y = None
    for c in range(n_chunks):
        y = _moe_grouped(tile_expert, n_used, xs[c], wg, wu, wd, y, c * tiles_c, n_tiles * tm)
    yg = _gather_rows(dest, y).reshape(2 * b, s, d // 2)
    return _moe_combine(yg, x, g2, route)


def _rope_tables(n):
    rows = n // GRID_W
    r = np.repeat(np.arange(rows), GRID_W).astype(np.float64)
    col = np.tile(np.arange(GRID_W), rows).astype(np.float64)
    freqs = ROPE_BASE ** (-np.arange(ROPE_FREQS, dtype=np.float64) / ROPE_FREQS)
    ang = np.stack([r[:, None] * freqs, col[:, None] * freqs], axis=1)
    ang = np.repeat(ang[:, :, None, :], 2, axis=2).reshape(n, HEAD_DIM)
    ang = np.tile(ang, (1, LANES // HEAD_DIM))
    cos, sin = np.cos(ang), np.sin(ang)
    first_half = (np.arange(LANES) % (2 * ROPE_FREQS)) < ROPE_FREQS
    to_dev = lambda t: jnp.asarray(t.astype(np.float32))
    return to_dev(cos), to_dev(np.where(first_half, -sin, 0.0)), to_dev(np.where(first_half, 0.0, sin))


def _lane_rows(lg):
    return jnp.repeat(lg.astype(F32), HEAD_DIM).reshape(RET_HEADS // 2, 1, LANES)


def kernel(x, c, ctx, c_ctx, ada_w, ada_b, norm1_w, norm2_w, w_in, w_out, q_norm_w, k_norm_w,
           attn_sink, ret_decay_f, ret_decay_b, ret_gn_w, conv_w, conv_b, conv_ln_w, conv_ln_b,
           ffn_w_gate, ffn_w_up, ffn_w_down, router_w, moe_w_gate, moe_w_up, moe_w_down):
    b, n, d = x.shape
    n_ctx = ctx.shape[1]
    depth = ada_w.shape[0]
    cond = jnp.zeros((SUBLANES, d), F32).at[0:b].set(c).at[b].set(c_ctx)
    mods = _adaln(cond, ada_w, ada_b).reshape(depth, SUBLANES, 6, d)
    cos, sa, sb = _rope_tables(n)
    ones_c = jnp.ones((n_ctx, LANES), F32)
    zeros_c = jnp.zeros((n_ctx, LANES), F32)
    zero_state = jnp.zeros((b, RET_HEADS // 2, LANES, LANES), F32)
    row = lambda v: v.reshape(1, -1)
    slab_rows = b * (n // ATT_BLOCK // ATT_QBLOCKS) * BF16_SUBLANES
    riders = {l: {} for l in range(depth)}

    def ride(l, key, w):
        w2 = w.reshape(-1, w.shape[-1])
        if 0 <= l < depth and (n // ATT_BLOCK) % ATT_QBLOCKS == 0 and w2.shape[0] % slab_rows == 0:
            riders[l][key] = w2

    for j, w in enumerate((moe_w_gate, moe_w_up, moe_w_down)):
        ride(j * min(depth, 2) // 3, ("moe", j), w)
    for l in range(depth):
        if l % 2 == 0:
            ride(l, ("ffn", l, 0), ffn_w_gate[l // 2])
            ride(l, ("ffn", l, 1), ffn_w_up[l // 2])
            ride(l, ("ffn", l, 2), ffn_w_down[l // 2])
        ride(l - 1, ("w_in", l), w_in[l])
        ride(l - 1, ("w_out", l), w_out[l])
    rounded = {}

    def bf16_of(key, w):
        return rounded[key].reshape(w.shape) if key in rounded else w.astype(BF16)

    for l in range(depth):
        last = l == depth - 1
        m_lat = [mods[l, 0:b, k][:, None, :] for k in range(6)]
        m_ctx = [jnp.broadcast_to(mods[l, b, k][None, None, :], (b, 1, d)) for k in range(6)]
        w_in_bf = bf16_of(("w_in", l), w_in[l])
        w_out_bf = bf16_of(("w_out", l), w_out[l])
        qw = row(jnp.tile(q_norm_w[l], LANES // HEAD_DIM))
        kw = row(jnp.tile(k_norm_w[l], LANES // HEAD_DIM))
        lgf = _lane_rows(jax.nn.log_sigmoid(ret_decay_f[l].astype(F32)))
        lgb = _lane_rows(jax.nn.log_sigmoid(ret_decay_b[l].astype(F32)))
        sink_tab = jnp.repeat(attn_sink[l].astype(F32), ATT_BLOCK).reshape(ATT_Q_HEADS // 2, 2 * ATT_BLOCK)

        q, k, v, rk, rv, rq, rg, cv = _inproj(x, row(norm1_w[l]), m_lat[1], m_lat[0], w_in_bf,
                                               cos, sa, sb, qw, kw)
        qc, kc, vc, rkc, rvc, rqc, rgc, cvc = _inproj(ctx, row(norm1_w[l]), m_ctx[1], m_ctx[0], w_in_bf,
                                                       ones_c, zeros_c, zeros_c, qw, kw)
        rf_c, s_f = _ret_states(lgf, rkc, rvc, zero_state, reverse=False)
        rb_c, s_b = _ret_states(lgb, rkc, rvc, zero_state, reverse=True)
        rf, _ = _ret_states(lgf, rk, rv, s_f, reverse=False)
        rb, _ = _ret_states(lgb, rk, rv, s_b, reverse=True)

        att, cast_out = _attention(q, k, v, kc, vc, sink_tab, window=True, cast=list(riders[l].values()))
        rounded.update(zip(riders[l].keys(), cast_out))
        ret = _ret_out(lgf, lgb, rq, rk, rv, rg, rf, rb, row(ret_gn_w[l]))
        cnv = _conv(cv, conv_w[l], row(conv_b[l]), row(conv_ln_w[l]), row(conv_ln_b[l]))

        if l % 2 == 0:
            i = l // 2
            wg, wu, wd = (bf16_of(("ffn", l, j), w[i]) for j, w in enumerate((ffn_w_gate, ffn_w_up, ffn_w_down)))
            x_new = _mixer_ffn(att, ret, cnv, w_out_bf, x, m_lat[2], row(norm2_w[l]), m_lat[4], m_lat[3],
                               wg, wu, wd, m_lat[5])
        else:
            i = l // 2
            wg, wu, wd = (bf16_of(("moe", j), w)[i] for j, w in enumerate((moe_w_gate, moe_w_up, moe_w_down)))
            x_mid, h2, route = _outproj_route(att, ret, cnv, w_out_bf, x, m_lat[2], row(norm2_w[l]),
                                              m_lat[4], m_lat[3], router_w[i])
            x_new = _moe(h2, route, x_mid, m_lat[5], wg, wu, wd)

        if not last:
            att_c, _ = _attention(qc, None, None, kc, vc, sink_tab, window=False)
            ret_c = _ret_out(lgf, lgb, rqc, rkc, rvc, rgc, rf_c, rb_c, row(ret_gn_w[l]))
            cnv_c = _conv(cvc, conv_w[l], row(conv_b[l]), row(conv_ln_w[l]), row(conv_ln_b[l]))
            if l % 2 == 0:
                ctx = _mixer_ffn(att_c, ret_c, cnv_c, w_out_bf, ctx, m_ctx[2], row(norm2_w[l]), m_ctx[4],
                                 m_ctx[3], wg, wu, wd, m_ctx[5])
            else:
                c_mid, h2c, route_c = _outproj_route(att_c, ret_c, cnv_c, w_out_bf, ctx, m_ctx[2],
                                                     row(norm2_w[l]), m_ctx[4], m_ctx[3], router_w[i])
                ctx = _moe(h2c, route_c, c_mid, m_ctx[5], wg, wu, wd)
        x = x_new
    return x
```
